```python
import jax
import jax.numpy as jnp
from jax import lax
import numpy as np

D_MODEL = 1024
BATCH = 32
SEQ = 256
DEPTH = 4
DEC_BATCH = 2
DEC_SEQ = 1024
PAST_LEN = 512

GRID_W = 64
EPS = 1e-6
ML_HEADS = 4
ML_DIM = 128
ML_WIDTH = ML_HEADS * ML_DIM
ML_CHUNK = 64
MLA_HEADS = 8
MLA_NOPE = 64
MLA_ROPE = 32
MLA_V = 64
MLA_QK = MLA_NOPE + MLA_ROPE
MLA_Q_RANK = 256
MLA_KV_RANK = 128
MLA_WIDTH = MLA_HEADS * MLA_V
ROPE_THETA = 10000.0
Q_BLOCK = 128
SG_GROUPS = 4
SG_DIM = 128
SG_WIDTH = SG_GROUPS * SG_DIM
SG_CHUNK = 128
N_BRANCH = 3
N_EXPERTS = 32
TOP_K = 4
D_EXPERT = 1024
SWIGLU_LIMIT = 7.0
SWIGLU_ALPHA = 1.702
MOE_BLOCK = 128
IN_SPLITS = (ML_WIDTH, ML_WIDTH, ML_WIDTH, ML_WIDTH, 4 * ML_HEADS, MLA_Q_RANK, MLA_KV_RANK, MLA_ROPE, SG_WIDTH, SG_WIDTH, N_BRANCH * D_MODEL)
D_IN = 4 * ML_WIDTH + 4 * ML_HEADS + MLA_Q_RANK + MLA_KV_RANK + MLA_ROPE + 2 * SG_WIDTH + N_BRANCH * D_MODEL

kernel_name = 'hybrid_mlstm_mla_gmlp_moe_diffusion_step'


def rms_norm(x, g):
    xf = x.astype(jnp.float32)
    y = xf * lax.rsqrt(jnp.mean(xf * xf, axis=-1, keepdims=True) + EPS)
    return (y * g.astype(jnp.float32)).astype(x.dtype)


def adaln_params(cvec, w_ada, b_ada):
    m = jax.nn.silu(cvec) @ w_ada + b_ada
    return jnp.split(m[:, None, :], 6, axis=-1)


def modulate(x, g, shift, scale):
    return rms_norm(x, g) * (1.0 + scale) + shift


def split_projection(h, w_in):
    z = h @ w_in
    idx, acc = [], 0
    for s in IN_SPLITS[:-1]:
        acc += s
        idx.append(acc)
    return jnp.split(z, idx, axis=-1)


def to_heads(t, n_heads):
    B, S, W = t.shape
    return t.reshape(B, S, n_heads, W // n_heads).transpose(0, 2, 1, 3)


def from_heads(t):
    B, H, S, d = t.shape
    return t.transpose(0, 2, 1, 3).reshape(B, S, H * d)


def mlstm_gate_preacts(g_raw, b_gates):
    g = (g_raw + b_gates).astype(jnp.float32)
    B, S, _ = g.shape
    g = g.reshape(B, S, 4, ML_HEADS).transpose(2, 0, 3, 1)
    return g[0], jax.nn.log_sigmoid(g[1]), g[2], jax.nn.log_sigmoid(g[3])


def mlstm_chunkwise(q, k, v, ig, lf, init):
    B, H, S, dk = q.shape
    dv = v.shape[-1]
    nc = S // ML_CHUNK
    chunk = lambda t: jnp.moveaxis(t.reshape((B, H, nc, ML_CHUNK) + t.shape[3:]), 2, 0)
    lower = jnp.tril(jnp.ones((ML_CHUNK, ML_CHUNK), dtype=bool))

    def step(carry, xs):
        C, n, m = carry
        qc, kc, vc, ic, fc = xs
        b = jnp.cumsum(fc, axis=-1)
        a = b + m[..., None]
        d = jnp.where(lower, b[..., :, None] - b[..., None, :] + ic[..., None, :], -jnp.inf)
        m_t = jnp.maximum(a, jnp.max(d, axis=-1))
        w_in = jnp.exp(d - m_t[..., None])
        w_c = jnp.exp(a - m_t)
        s = jnp.einsum('bhtd,bhsd->bhts', qc, kc) * w_in
        num = w_c[..., None] * jnp.einsum('bhtd,bhde->bhte', qc, C) + jnp.einsum('bhts,bhse->bhte', s, vc)
        den = w_c * jnp.einsum('bhtd,bhd->bht', qc, n) + jnp.sum(s, axis=-1)
        h = num / jnp.maximum(jnp.abs(den), jnp.exp(-m_t))[..., None]
        b_last = b[..., -1]
        g = b_last[..., None] - b + ic
        m_new = jnp.maximum(b_last + m, jnp.max(g, axis=-1))
        w_s = jnp.exp(g - m_new[..., None])
        w_old = jnp.exp(b_last + m - m_new)
        C_new = w_old[..., None, None] * C + jnp.einsum('bhsd,bhse->bhde', kc * w_s[..., None], vc)
        n_new = w_old[..., None] * n + jnp.einsum('bhs,bhsd->bhd', w_s, kc)
        return (C_new, n_new, m_new), h

    final, hs = lax.scan(step, init, (chunk(q), chunk(k), chunk(v), chunk(ig), chunk(lf)))
    return jnp.moveaxis(hs, 0, 2).reshape(B, H, S, dv), final


def mlstm_branch(mq, mk, mv, mo, mg, b_gates, norm_g, init_f, init_b):
    q = to_heads(mq, ML_HEADS).astype(jnp.float32)
    k = to_heads(mk, ML_HEADS).astype(jnp.float32) * (ML_DIM ** -0.5)
    v = to_heads(mv, ML_HEADS).astype(jnp.float32)
    i_f, lf_f, i_b, lf_b = mlstm_gate_preacts(mg, b_gates)
    h_f, fin_f = mlstm_chunkwise(q, k, v, i_f, lf_f, init_f)
    rev = lambda t: jnp.flip(t, axis=2)
    h_b, fin_b = mlstm_chunkwise(rev(q), rev(k), rev(v), rev(i_b), rev(lf_b), init_b)
    h = rms_norm(h_f + rev(h_b), norm_g.reshape(ML_HEADS, 1, ML_DIM))
    out = from_heads(h).astype(mo.dtype) * jax.nn.sigmoid(mo)
    return out, fin_f, fin_b


def rope_2d(n_tokens):
    rows = n_tokens // GRID_W
    row = jnp.repeat(jnp.arange(rows), GRID_W).astype(jnp.float32)
    col = jnp.tile(jnp.arange(GRID_W), rows).astype(jnp.float32)
    half = MLA_ROPE // 2
    inv_freq = ROPE_THETA ** (-jnp.arange(0, half, 2, dtype=jnp.float32) / half)
    ang_r = row[:, None] * inv_freq[None, :]
    ang_c = col[:, None] * inv_freq[None, :]
    return (jnp.cos(ang_r), jnp.sin(ang_r), jnp.cos(ang_c), jnp.sin(ang_c))


def rotate_half(x, cos, sin):
    x1, x2 = jnp.split(x, 2, axis=-1)
    cos, sin = cos.astype(x.dtype), sin.astype(x.dtype)
    return jnp.concatenate([x1 * cos - x2 * sin, x2 * cos + x1 * sin], axis=-1)


def apply_rope_2d(x, rope):
    cos_r, sin_r, cos_c, sin_c = rope
    xr, xc = jnp.split(x, 2, axis=-1)
    return jnp.concatenate([rotate_half(xr, cos_r, sin_r), rotate_half(xc, cos_c, sin_c)], axis=-1)


def mla_queries(cq_raw, q_a_norm, w_uq, q_norm, rope):
    cq = rms_norm(cq_raw, q_a_norm)
    q = rms_norm(to_heads(cq @ w_uq, MLA_HEADS), q_norm)
    if rope is not None:
        q = jnp.concatenate([q[..., :MLA_NOPE], apply_rope_2d(q[..., MLA_NOPE:], rope)], axis=-1)
    return q


def mla_keys_values(ckv, krope, w_ukv, k_norm, rope):
    B, S, _ = ckv.shape
    kv = (ckv @ w_ukv).reshape(B, S, MLA_HEADS, MLA_NOPE + MLA_V)
    k_pe = jnp.broadcast_to(krope[:, :, None, :], (B, S, MLA_HEADS, MLA_ROPE))
    k = rms_norm(jnp.concatenate([kv[..., :MLA_NOPE], k_pe], axis=-1), k_norm).transpose(0, 2, 1, 3)
    if rope is not None:
        k = jnp.concatenate([k[..., :MLA_NOPE], apply_rope_2d(k[..., MLA_NOPE:], rope)], axis=-1)
    return k, kv[..., MLA_NOPE:].transpose(0, 2, 1, 3)


def block_attention(q, k, v):
    B, H, Sq, dk = q.shape
    nb = Sq // Q_BLOCK
    qb = q.reshape(B, H, nb, Q_BLOCK, dk).transpose(2, 0, 1, 3, 4)
    scale = dk ** -0.5

    def one(qblk):
        s = jnp.einsum('bhqd,bhkd->bhqk', qblk, k).astype(jnp.float32) * scale
        p = jax.nn.softmax(s, axis=-1).astype(v.dtype)
        return jnp.einsum('bhqk,bhkd->bhqd', p, v)

    o = lax.map(one, qb)
    return o.transpose(1, 2, 0, 3, 4).reshape(B, H, Sq, v.shape[-1])


def chunk_mlp_branch(su, sv, sg_norm, w_s, b_s):
    B, S, _ = su.shape
    u = jax.nn.gelu(su)
    v = rms_norm(jax.nn.gelu(sv).reshape(B, S, SG_GROUPS, SG_DIM), sg_norm.reshape(SG_GROUPS, SG_DIM))
    v = v.reshape(B, S // SG_CHUNK, SG_CHUNK, SG_GROUPS, SG_DIM)
    v = jnp.einsum('gts,bcsgd->bctgd', w_s, v) + b_s.T[None, None, :, :, None]
    return u * v.reshape(B, S, SG_WIDTH)


def merge_branches(outs, gate_raw, w_branch, w_out):
    B, S, _ = gate_raw.shape
    br = jnp.einsum('bsjw,jwd->bsjd', jnp.stack(outs, axis=2), w_branch)
    gates = jax.nn.sigmoid(gate_raw.reshape(B, S, N_BRANCH, D_MODEL))
    return jnp.sum(gates * br, axis=2) @ w_out


def token_mixers(h, lw, init_f, init_b, rope, ctx_cache):
    (mq, mk, mv, mo, mg, cq, ckv_raw, krope, su, sv, bg) = split_projection(h, lw['w_in'])
    out_a, fin_f, fin_b = mlstm_branch(mq, mk, mv, mo, mg, lw['b_gates'], lw['mlstm_norm'], init_f, init_b)
    ckv = rms_norm(ckv_raw, lw['kv_a_norm'])
    q = mla_queries(cq, lw['q_a_norm'], lw['w_uq'], lw['q_norm'], rope)
    k, v = mla_keys_values(ckv, krope, lw['w_ukv'], lw['k_norm'], rope)
    if ctx_cache is not None:
        k_ctx, v_ctx = mla_keys_values(ctx_cache[0], ctx_cache[1], lw['w_ukv'], lw['k_norm'], None)
        k = jnp.concatenate([k, k_ctx.astype(k.dtype)], axis=2)
        v = jnp.concatenate([v, v_ctx.astype(v.dtype)], axis=2)
    out_b = from_heads(block_attention(q, k, v))
    out_c = chunk_mlp_branch(su, sv, lw['sg_norm'], lw['w_s'], lw['b_s'])
    mix = merge_branches((out_a, out_b, out_c), bg, lw['w_branch'], lw['w_out'])
    return mix, (ckv, krope, fin_f, fin_b)


def moe(h, w_router, b_router, w1, b1, w2, b2):
    B, S, D = h.shape
    N = B * S
    x = h.reshape(N, D)
    logits = (x @ w_router + b_router).astype(jnp.float32)
    top_v, top_i = lax.top_k(logits, TOP_K)
    wts = jax.nn.softmax(top_v, axis=-1)
    A = N * TOP_K
    e_flat = top_i.reshape(A).astype(jnp.int32)
    tok_flat = jnp.arange(A, dtype=jnp.int32) // TOP_K
    w_flat = wts.reshape(A)
    order = jnp.argsort(e_flat)
    e_sorted = e_flat[order]
    counts = jax.ops.segment_sum(jnp.ones((A,), jnp.int32), e_flat, num_segments=N_EXPERTS)
    starts = jnp.cumsum(counts) - counts
    padded = (counts + MOE_BLOCK - 1) // MOE_BLOCK * MOE_BLOCK
    pends = jnp.cumsum(padded)
    pstarts = pends - padded
    dest = pstarts[e_sorted] + jnp.arange(A, dtype=jnp.int32) - starts[e_sorted]
    n_blocks = -(-A // MOE_BLOCK) + N_EXPERTS
    P = n_blocks * MOE_BLOCK
    slot_tok = jnp.zeros((P,), jnp.int32).at[dest].set(tok_flat[order])
    slot_w = jnp.zeros((P,), jnp.float32).at[dest].set(w_flat[order])
    block_e = jnp.minimum(jnp.searchsorted(pends, jnp.arange(n_blocks, dtype=jnp.int32) * MOE_BLOCK, side='right'), N_EXPERTS - 1)

    def expert_block(args):
        tok, e = args
        g = x[tok] @ w1[e] + b1[e]
        gate = jnp.minimum(g[:, :D_EXPERT], SWIGLU_LIMIT)
        up = jnp.clip(g[:, D_EXPERT:], -SWIGLU_LIMIT, SWIGLU_LIMIT)
        act = gate * jax.nn.sigmoid(SWIGLU_ALPHA * gate) * (up + 1.0)
        return act @ w2[e] + b2[e]

    y = lax.map(expert_block, (slot_tok.reshape(n_blocks, MOE_BLOCK), block_e))
    y = y.reshape(P, D) * slot_w[:, None].astype(h.dtype)
    return jax.ops.segment_sum(y, slot_tok, num_segments=N).reshape(B, S, D)


def setup_inputs(seed: int = 0) -> dict:
    key = jax.random.key(seed)
    keys = jax.random.split(key, 48)
    cnt = [0]

    def nk():
        cnt[0] += 1
        return keys[cnt[0] - 1]

    nrm = lambda shape, s=1.0: jax.random.normal(nk(), shape, jnp.float32) * s
    gain = lambda shape: 1.0 + 0.02 * jax.random.normal(nk(), shape, jnp.float32)
    ig_b = nrm((DEPTH, 2, ML_HEADS), 0.1)
    fg_b = jnp.linspace(3.0, 6.0, ML_HEADS)[None, None, :] + nrm((DEPTH, 2, ML_HEADS), 0.1)
    b_mlstm_gates = jnp.stack([ig_b[:, 0], fg_b[:, 0], ig_b[:, 1], fg_b[:, 1]], axis=1).reshape(DEPTH, 4 * ML_HEADS)
    return {
        'x_prompt': nrm((BATCH, SEQ, D_MODEL)),
        'x_sample': nrm((DEC_BATCH, DEC_SEQ, D_MODEL)),
        'cache_mla_ckv': nrm((DEC_BATCH, DEPTH, PAST_LEN, MLA_KV_RANK)),
        'cache_mla_krope': nrm((DEC_BATCH, DEPTH, PAST_LEN, MLA_ROPE)),
        'state_mlstm_C': nrm((DEC_BATCH, DEPTH, 2, ML_HEADS, ML_DIM, ML_DIM), 0.3),
        'state_mlstm_n': nrm((DEC_BATCH, DEPTH, 2, ML_HEADS, ML_DIM), 0.3),
        'state_mlstm_m': 1.0 + nrm((DEC_BATCH, DEPTH, 2, ML_HEADS), 0.5),
        'c': nrm((DEC_BATCH, D_MODEL)),
        'c_ctx': nrm((D_MODEL,)),
        'norm1': gain((DEPTH, D_MODEL)),
        'norm2': gain((DEPTH, D_MODEL)),
        'w_ada': nrm((DEPTH, D_MODEL, 6 * D_MODEL), 0.5 * D_MODEL ** -0.5),
        'b_ada': nrm((DEPTH, 6 * D_MODEL), 0.02),
        'w_in': nrm((DEPTH, D_MODEL, D_IN), D_MODEL ** -0.5),
        'b_mlstm_gates': b_mlstm_gates,
        'mlstm_norm': gain((DEPTH, ML_WIDTH)),
        'mla_q_a_norm': gain((DEPTH, MLA_Q_RANK)),
        'mla_kv_a_norm': gain((DEPTH, MLA_KV_RANK)),
        'w_uq': nrm((DEPTH, MLA_Q_RANK, MLA_HEADS * MLA_QK), MLA_Q_RANK ** -0.5),
        'w_ukv': nrm((DEPTH, MLA_KV_RANK, MLA_HEADS * (MLA_NOPE + MLA_V)), MLA_KV_RANK ** -0.5),
        'mla_q_norm': gain((DEPTH, MLA_QK)),
        'mla_k_norm': gain((DEPTH, MLA_QK)),
        'sg_norm': gain((DEPTH, SG_WIDTH)),
        'w_spatial': nrm((DEPTH, SG_GROUPS, SG_CHUNK, SG_CHUNK), SG_CHUNK ** -0.5),
        'b_spatial': 1.0 + nrm((DEPTH, SG_GROUPS, SG_CHUNK), 0.02),
        'w_branch': nrm((DEPTH, N_BRANCH, ML_WIDTH, D_MODEL), ML_WIDTH ** -0.5),
        'w_out': nrm((DEPTH, D_MODEL, D_MODEL), D_MODEL ** -0.5),
        'w_router': nrm((DEPTH, D_MODEL, N_EXPERTS), D_MODEL ** -0.5),
        'b_router': nrm((DEPTH, N_EXPERTS), 0.01),
        'w_exp1': nrm((DEPTH, N_EXPERTS, D_MODEL, 2 * D_EXPERT), D_MODEL ** -0.5),
        'b_exp1': nrm((DEPTH, N_EXPERTS, 2 * D_EXPERT), 0.02),
        'w_exp2': nrm((DEPTH, N_EXPERTS, D_EXPERT, D_MODEL), D_EXPERT ** -0.5),
        'b_exp2': nrm((DEPTH, N_EXPERTS, D_MODEL), 0.02),
    }


def reference(x_prompt, x_sample, cache_mla_ckv, cache_mla_krope, state_mlstm_C, state_mlstm_n, state_mlstm_m, c, c_ctx, norm1, norm2, w_ada, b_ada, w_in, b_mlstm_gates, mlstm_norm, mla_q_a_norm, mla_kv_a_norm, w_uq, w_ukv, mla_q_norm, mla_k_norm, sg_norm, w_spatial, b_spatial, w_branch, w_out, w_router, b_router, w_exp1, b_exp1, w_exp2, b_exp2):
    f32 = jnp.float32
    xp, xs = x_prompt, x_sample
    Bp = xp.shape[0]
    zero_state = (jnp.zeros((Bp, ML_HEADS, ML_DIM, ML_DIM), f32), jnp.zeros((Bp, ML_HEADS, ML_DIM), f32), jnp.zeros((Bp, ML_HEADS), f32))
    rope = rope_2d(xs.shape[1])
    ckv_l, krope_l, C_l, n_l, m_l = [], [], [], [], []
    for l in range(DEPTH):
        lw = {'w_in': w_in[l], 'b_gates': b_mlstm_gates[l], 'mlstm_norm': mlstm_norm[l],
              'q_a_norm': mla_q_a_norm[l], 'kv_a_norm': mla_kv_a_norm[l], 'w_uq': w_uq[l], 'w_ukv': w_ukv[l],
              'q_norm': mla_q_norm[l], 'k_norm': mla_k_norm[l], 'sg_norm': sg_norm[l], 'w_s': w_spatial[l],
              'b_s': b_spatial[l], 'w_branch': w_branch[l], 'w_out': w_out[l]}
        sh1, sc1, g1, sh2, sc2, g2 = adaln_params(c_ctx[None, :], w_ada[l], b_ada[l])
        mix, (ckv, krope, fin_f, fin_b) = token_mixers(modulate(xp, norm1[l], sh1, sc1), lw, zero_state, zero_state, None, None)
        xp = xp + g1 * mix
        xp = xp + g2 * moe(modulate(xp, norm2[l], sh2, sc2), w_router[l], b_router[l], w_exp1[l], b_exp1[l], w_exp2[l], b_exp2[l])
        ckv_l.append(ckv)
        krope_l.append(krope)
        C_l.append(jnp.stack([fin_f[0], fin_b[0]], axis=1))
        n_l.append(jnp.stack([fin_f[1], fin_b[1]], axis=1))
        m_l.append(jnp.stack([fin_f[2], fin_b[2]], axis=1))
        sh1, sc1, g1, sh2, sc2, g2 = adaln_params(c, w_ada[l], b_ada[l])
        st_C = state_mlstm_C[:, l].astype(f32)
        st_n = state_mlstm_n[:, l].astype(f32)
        st_m = state_mlstm_m[:, l].astype(f32)
        init_f = (st_C[:, 0], st_n[:, 0], st_m[:, 0])
        init_b = (st_C[:, 1], st_n[:, 1], st_m[:, 1])
        mix, _ = token_mixers(modulate(xs, norm1[l], sh1, sc1), lw, init_f, init_b, rope, (cache_mla_ckv[:, l], cache_mla_krope[:, l]))
        xs = xs + g1 * mix
        xs = xs + g2 * moe(modulate(xs, norm2[l], sh2, sc2), w_router[l], b_router[l], w_exp1[l], b_exp1[l], w_exp2[l], b_exp2[l])
    new_mla_ckv = jnp.stack(ckv_l, axis=1)
    new_mla_krope = jnp.stack(krope_l, axis=1)
    new_mlstm_C = jnp.stack(C_l, axis=1)
    new_mlstm_n = jnp.stack(n_l, axis=1)
    new_mlstm_m = jnp.stack(m_l, axis=1)
    return (xp, xs, new_mla_ckv, new_mla_krope, new_mlstm_C, new_mlstm_n, new_mlstm_m)
```

```python
import functools

import numpy as np
import jax
import jax.numpy as jnp
from jax import lax
from jax.experimental import pallas as pl
from jax.experimental.pallas import tpu as pltpu

F32 = jnp.float32
BF16 = jnp.bfloat16
HI = lax.Precision.HIGHEST

D_MODEL = 1024
N_SEQ_CTX, S_CTX = 32, 256
N_SEQ_SMP, S_SMP = 2, 1024
DEPTH = 4
PAST_LEN = 512
GRID_W = 64
EPS = 1e-6
ML_HEADS, ML_DIM = 4, 128
ML_WIDTH = ML_HEADS * ML_DIM
MLA_HEADS, MLA_NOPE, MLA_ROPE, MLA_V = 8, 64, 32, 64
MLA_QK = MLA_NOPE + MLA_ROPE
MLA_Q_RANK, MLA_KV_RANK = 256, 128
ROPE_THETA = 10000.0
SG_GROUPS, SG_DIM, SG_CHUNK = 4, 128, 128
SG_WIDTH = SG_GROUPS * SG_DIM
N_BRANCH = 3
N_EXPERTS, TOP_K, D_EXPERT = 32, 4, 1024
SWIGLU_LIMIT, SWIGLU_ALPHA = 7.0, 1.702

N_CTX = N_SEQ_CTX * S_CTX
N_SMP = N_SEQ_SMP * S_SMP
N_TOK = N_CTX + N_SMP

LANES = 128
SUBLANES = 8
VMEM_LIMIT = 56 * 1024 * 1024

TW = 256
N_TILES = N_TOK // TW
N_TILES_CTX = N_CTX // TW
TILES_PER_SMP_SEQ = S_SMP // TW
HEAD_PAD = LANES
TQ = 256
MOE_PARTS = 4
TP = N_TOK // MOE_PARTS
WIN_PER_PART = TP // TW
SLOT_CHUNK = 256

ZA_W = 4 * ML_WIDTH
ZS_W = 512
ZC_W = 2 * SG_WIDTH
ZB_W = N_BRANCH * D_MODEL
ZIN_W = ZA_W + ZS_W + ZC_W + ZB_W
GATE_LANE0 = MLA_ROPE


def _cparams(sem):
    return pltpu.CompilerParams(dimension_semantics=sem, vmem_limit_bytes=VMEM_LIMIT)


def _mod_row(i):
    return jnp.where(i < N_TILES_CTX, 0, 1 + (i - N_TILES_CTX) // TILES_PER_SMP_SEQ)


def _rms(x, g, n=None):
    ms = jnp.sum(x * x, axis=-1, keepdims=True) * (1.0 / (n or x.shape[-1]))
    return x * lax.rsqrt(ms + EPS) * g


def _gelu(x):
    return 0.5 * x * (1.0 + jnp.tanh(0.7978845608028654 * (x + 0.044715 * (x * x * x))))


def _dot(a, b):
    return jnp.dot(a, b, preferred_element_type=F32)


def _dot_nt(a, b):
    return lax.dot_general(a, b, (((1,), (1,)), ((), ())), preferred_element_type=F32)


def _adaln_kernel(c_ref, w_ref, b_ref, o_ref):
    c = c_ref[...]
    s = c * jax.nn.sigmoid(c)
    o_ref[0] = jnp.dot(s, w_ref[0], precision=HI, preferred_element_type=F32) + b_ref[0]


def _adaln(cvec, w_ada, b_ada):
    nchunk = 4
    cw = 6 * D_MODEL // nchunk
    return pl.pallas_call(
        _adaln_kernel,
        grid=(DEPTH, nchunk),
        in_specs=[
            pl.BlockSpec((SUBLANES, D_MODEL), lambda l, j: (0, 0)),
            pl.BlockSpec((1, D_MODEL, cw), lambda l, j: (l, 0, j)),
            pl.BlockSpec((1, 1, cw), lambda l, j: (l, 0, j)),
        ],
        out_specs=pl.BlockSpec((1, SUBLANES, cw), lambda l, j: (l, 0, j)),
        out_shape=jax.ShapeDtypeStruct((DEPTH, SUBLANES, 6 * D_MODEL), F32),
        compiler_params=_cparams(("arbitrary", "arbitrary")),
        name="adaln",
    )(cvec, w_ada, b_ada.reshape(DEPTH, 1, 6 * D_MODEL))


def _cast_kernel(w_ref, o_ref):
    o_ref[...] = w_ref[...].astype(BF16)


def _cast_experts(w):
    _, _, r, c = w.shape
    return pl.pallas_call(
        _cast_kernel,
        grid=(DEPTH, N_EXPERTS),
        in_specs=[pl.BlockSpec((1, 1, r, c), lambda l, e: (l, e, 0, 0))],
        out_specs=pl.BlockSpec((1, 1, r, c), lambda l, e: (l, e, 0, 0)),
        out_shape=jax.ShapeDtypeStruct(w.shape, BF16),
        compiler_params=_cparams(("arbitrary", "arbitrary")),
        name="cast_experts",
    )(w)


def _inproj_kernel(x_ref, g_ref, sh_ref, sc_ref, w_ref, za_ref, zs_ref, zc_ref, zb_ref):
    h = _rms(x_ref[...], g_ref[0]) * (1.0 + sc_ref[0, 0]) + sh_ref[0, 0]
    hb = h.astype(BF16)
    za_ref[...] = _dot(hb, w_ref[0, :, 0:ZA_W]).astype(BF16)
    zs_ref[...] = _dot(hb, w_ref[0, :, ZA_W:ZA_W + ZS_W])
    zc_ref[...] = _dot(hb, w_ref[0, :, ZA_W + ZS_W:ZA_W + ZS_W + ZC_W]).astype(BF16)
    zb_ref[...] = _dot(hb, w_ref[0, :, ZA_W + ZS_W + ZC_W:ZIN_W]).astype(BF16)


def _mod_spec(layer, k):
    return pl.BlockSpec((1, 1, 1, D_MODEL), lambda i: (layer, _mod_row(i), 0, k))


def _inproj(layer, x, norm1, mod, w_in_r):
    tok = lambda w: pl.BlockSpec((TW, w), lambda i: (i, 0))
    return pl.pallas_call(
        _inproj_kernel,
        grid=(N_TILES,),
        in_specs=[
            tok(D_MODEL),
            pl.BlockSpec((1, 1, D_MODEL), lambda i: (layer, 0, 0)),
            _mod_spec(layer, 0),
            _mod_spec(layer, 1),
            pl.BlockSpec((1, D_MODEL, ZIN_W), lambda i: (layer, 0, 0)),
        ],
        out_specs=[tok(ZA_W), tok(ZS_W), tok(ZC_W), tok(ZB_W)],
        out_shape=[
            jax.ShapeDtypeStruct((N_TOK, ZA_W), BF16),
            jax.ShapeDtypeStruct((N_TOK, ZS_W), F32),
            jax.ShapeDtypeStruct((N_TOK, ZC_W), BF16),
            jax.ShapeDtypeStruct((N_TOK, ZB_W), BF16),
        ],
        compiler_params=_cparams(("arbitrary",)),
        name="inproj",
    )(x, norm1, mod, mod, w_in_r)


def _make_mlstm_kernel(seq, layer, has_init):
    nq = seq // TQ
    lane_if, lane_ff, lane_ib, lane_fb = (GATE_LANE0 + ML_HEADS * j for j in range(4))

    def kern(*refs):
        if has_init:
            m0_ref, zq, zk, zv, zo, gz, bg, nrm, c0_ref, n0_ref, out, bp_scr, bs_scr = refs
        else:
            zq, zk, zv, zo, gz, bg, nrm, out, cf_ref, nf_ref, mf_ref, bp_scr, bs_scr = refs
        b = pl.program_id(0)
        g = gz[...] + bg[0]
        lane = lax.broadcasted_iota(jnp.int32, g.shape, 1)
        is_forget = ((lane >= lane_ff) & (lane < lane_ib)) | ((lane >= lane_fb) & (lane < lane_fb + ML_HEADS))
        log_sig = jnp.minimum(g, 0.0) - jnp.log1p(jnp.exp(-jnp.abs(g)))
        a = jnp.where(is_forget, log_sig, g)
        r_i = lax.broadcasted_iota(jnp.int32, (seq, seq), 0)
        c_i = lax.broadcasted_iota(jnp.int32, (seq, seq), 1)
        ltri = (c_i <= r_i).astype(F32)
        bp = jnp.dot(ltri, a, precision=HI, preferred_element_type=F32)
        bs = bp[seq - 1:seq, :] - bp + a
        bp_scr[...] = bp
        bs_scr[...] = bs
        a_t, bp_t, bs_t = a.T, bp.T, bs.T

        for h in range(ML_HEADS):
            hs = slice(h * ML_DIM, (h + 1) * ML_DIM)
            k = zk[:, hs]
            v = zv[:, hs]
            rows = (
                a_t[lane_if + h:lane_if + h + 1, :] - bp_t[lane_ff + h:lane_ff + h + 1, :],
                a_t[lane_ib + h:lane_ib + h + 1, :] - bs_t[lane_fb + h:lane_fb + h + 1, :],
            )
            col_refs = ((bp_scr, lane_ff + h), (bs_scr, lane_fb + h))
            if has_init:
                m0 = tuple(m0_ref[((b * DEPTH + layer) * 2 + dr) * ML_HEADS + h] for dr in range(2))
                c0 = tuple(c0_ref[0, 0, dr, h].astype(BF16) for dr in range(2))
                n0 = tuple(n0_ref[0, 0, dr, h:h + 1, :] for dr in range(2))
            else:
                m0 = (0.0, 0.0)

            def qblock(qi, carry):
                q0 = pl.multiple_of(qi * TQ, TQ)
                qb = zq[pl.ds(q0, TQ), hs]
                sc = _dot_nt(qb, k)
                t_idx = q0 + lax.broadcasted_iota(jnp.int32, (TQ, seq), 0)
                s_idx = lax.broadcasted_iota(jnp.int32, (TQ, seq), 1)
                hsum = jnp.zeros((TQ, ML_DIM), F32)
                for dr in range(2):
                    cref, cl = col_refs[dr]
                    col = cref[pl.ds(q0, TQ), cl:cl + 1]
                    mask = (s_idx <= t_idx) if dr == 0 else (s_idx >= t_idx)
                    d = jnp.where(mask, col + rows[dr], -jnp.inf)
                    a0 = col + m0[dr]
                    m_t = jnp.maximum(a0, jnp.max(d, axis=1, keepdims=True))
                    s = sc * jnp.exp(d - m_t)
                    num = _dot(s.astype(BF16), v)
                    den = jnp.sum(s, axis=1, keepdims=True)
                    if has_init:
                        w_c = jnp.exp(a0 - m_t)
                        num = num + w_c * _dot(qb, c0[dr])
                        den = den + w_c * jnp.sum(qb.astype(F32) * n0[dr], axis=1, keepdims=True)
                    hsum = hsum + num / jnp.maximum(jnp.abs(den), jnp.exp(-m_t))
                hn = _rms(hsum, nrm[0][:, hs])
                og = zo[pl.ds(q0, TQ), hs].astype(F32)
                out[pl.ds(q0, TQ), hs] = (hn * jax.nn.sigmoid(og)).astype(out.dtype)
                return carry

            if nq == 1:
                qblock(0, 0)
            else:
                lax.fori_loop(0, nq, qblock, 0)

            if not has_init:
                k_t = k.astype(F32).T
                kf = k.astype(F32)
                tot = (bp_t[lane_ff + h:lane_ff + h + 1, seq - 1:seq], bp_t[lane_fb + h:lane_fb + h + 1, seq - 1:seq])
                gl = (
                    tot[0] + rows[0],
                    bp_t[lane_fb + h:lane_fb + h + 1, :] - a_t[lane_fb + h:lane_fb + h + 1, :]
                    + a_t[lane_ib + h:lane_ib + h + 1, :],
                )
                for dr in range(2):
                    m_new = jnp.maximum(tot[dr] + m0[dr], jnp.max(gl[dr], axis=1, keepdims=True))
                    w_s = jnp.exp(gl[dr] - m_new)
                    cf_ref[0, dr, h] = _dot((k_t * w_s).astype(BF16), v)
                    n_new = jnp.dot(jnp.broadcast_to(w_s, (SUBLANES, seq)), kf, precision=HI, preferred_element_type=F32)
                    nf_ref[0, dr, h:h + 1, :] = n_new[0:1, :]
                    mf_ref[0, dr * ML_HEADS + h:dr * ML_HEADS + h + 1, :] = jnp.broadcast_to(m_new, (1, LANES))

    return kern


def _mlstm(layer, za, zs, b_gates, mlstm_norm, init=None):
    has_init = init is not None
    seq, nseq, row0 = (S_SMP, N_SEQ_SMP, N_CTX // S_SMP) if has_init else (S_CTX, N_SEQ_CTX, 0)
    qkvo = [pl.BlockSpec((seq, ML_WIDTH), functools.partial(lambda j, b: (row0 + b, j), j)) for j in range(4)]
    in_specs = qkvo + [
        pl.BlockSpec((seq, LANES), lambda b: (row0 + b, ZS_W // LANES - 1)),
        pl.BlockSpec((1, 1, LANES), lambda b: (layer, 0, 0)),
        pl.BlockSpec((1, 1, ML_WIDTH), lambda b: (layer, 0, 0)),
    ]
    args = [za, za, za, za, zs, b_gates, mlstm_norm]
    out_specs = [pl.BlockSpec((seq, ML_WIDTH), lambda b: (b, 0))]
    out_shape = [jax.ShapeDtypeStruct((nseq * seq, ML_WIDTH), BF16)]
    if has_init:
        st_c, st_n, st_m = init
        in_specs = [pl.BlockSpec(memory_space=pltpu.SMEM)] + in_specs + [
            pl.BlockSpec((1, 1, 2, ML_HEADS, ML_DIM, ML_DIM), lambda b: (b, layer, 0, 0, 0, 0)),
            pl.BlockSpec((1, 1, 2, ML_HEADS, ML_DIM), lambda b: (b, layer, 0, 0, 0)),
        ]
        args = [st_m.reshape(-1)] + args + [st_c, st_n]
    else:
        out_specs += [
            pl.BlockSpec((1, 2, ML_HEADS, ML_DIM, ML_DIM), lambda b: (b, 0, 0, 0, 0)),
            pl.BlockSpec((1, 2, ML_HEADS, ML_DIM), lambda b: (b, 0, 0, 0)),
            pl.BlockSpec((1, 2 * ML_HEADS, LANES), lambda b: (b, 0, 0)),
        ]
        out_shape += [
            jax.ShapeDtypeStruct((nseq, 2, ML_HEADS, ML_DIM, ML_DIM), F32),
            jax.ShapeDtypeStruct((nseq, 2, ML_HEADS, ML_DIM), F32),
            jax.ShapeDtypeStruct((nseq, 2 * ML_HEADS, LANES), F32),
        ]
    return pl.pallas_call(
        _make_mlstm_kernel(seq, layer, has_init),
        grid=(nseq,),
        in_specs=in_specs,
        out_specs=out_specs,
        out_shape=out_shape,
        scratch_shapes=[pltpu.VMEM((seq, LANES), F32), pltpu.VMEM((seq, LANES), F32)],
        compiler_params=_cparams(("arbitrary",)),
        name="mlstm_smp" if has_init else "mlstm_ctx",
    )(*args)


def _rope(x, cos, sin_a, sin_b):
    return x * cos + pltpu.roll(x, LANES - 8, 1) * sin_a + pltpu.roll(x, 8, 1) * sin_b


def _mla_prep_kernel(zs_ref, qa_ref, kva_ref, wuq_ref, wk_ref, wv_ref, qn_ref, kn_ref, cos_ref, sa_ref, sb_ref,
                     q_ref, kk_ref, v_ref, ckv_ref):
    cq = zs_ref[:, 0:MLA_Q_RANK]
    ckv = zs_ref[:, MLA_Q_RANK:MLA_Q_RANK + MLA_KV_RANK]
    last = zs_ref[:, ZS_W - LANES:ZS_W]
    qf = _dot(_rms(cq, qa_ref[0]).astype(BF16), wuq_ref[0])
    ckvn = _rms(ckv, kva_ref[0])
    ckv_ref[...] = ckvn
    cb = ckvn.astype(BF16)
    kf = _dot(cb, wk_ref[0])
    v_ref[...] = _dot(cb, wv_ref[0]).astype(BF16)
    lane = lax.broadcasted_iota(jnp.int32, last.shape, 1)
    kr = jnp.where((lane >= MLA_NOPE) & (lane < MLA_QK), pltpu.roll(last, MLA_NOPE, 1), 0.0)
    cos, sa, sb = cos_ref[...], sa_ref[...], sb_ref[...]
    for h in range(MLA_HEADS):
        hs = slice(h * HEAD_PAD, (h + 1) * HEAD_PAD)
        q_ref[:, hs] = _rope(_rms(qf[:, hs], qn_ref[0], n=MLA_QK), cos, sa, sb).astype(BF16)
        kk_ref[:, hs] = _rope(_rms(kf[:, hs] + kr, kn_ref[0], n=MLA_QK), cos, sa, sb).astype(BF16)


def _mla_prep(layer, zs, q_a_norm, kv_a_norm, w_uq_r, w_k_r, w_v_r, q_norm_p, k_norm_p, rope_tabs):
    lw = lambda shape: pl.BlockSpec((1,) + shape, lambda i: (layer,) + (0,) * len(shape))
    tab = pl.BlockSpec((TW, LANES), lambda i: (jnp.where(i < N_TILES_CTX, 0, 1 + (i - N_TILES_CTX) % TILES_PER_SMP_SEQ), 0))
    tok = lambda w: pl.BlockSpec((TW, w), lambda i: (i, 0))
    return pl.pallas_call(
        _mla_prep_kernel,
        grid=(N_TILES,),
        in_specs=[
            tok(ZS_W), lw((1, MLA_Q_RANK)), lw((1, MLA_KV_RANK)),
            lw((MLA_Q_RANK, MLA_HEADS * HEAD_PAD)), lw((MLA_KV_RANK, MLA_HEADS * HEAD_PAD)),
            lw((MLA_KV_RANK, MLA_HEADS * MLA_V)), lw((1, HEAD_PAD)), lw((1, HEAD_PAD)), tab, tab, tab,
        ],
        out_specs=[tok(MLA_HEADS * HEAD_PAD), tok(MLA_HEADS * HEAD_PAD), tok(MLA_HEADS * MLA_V), tok(MLA_KV_RANK)],
        out_shape=[
            jax.ShapeDtypeStruct((N_TOK, MLA_HEADS * HEAD_PAD), BF16),
            jax.ShapeDtypeStruct((N_TOK, MLA_HEADS * HEAD_PAD), BF16),
            jax.ShapeDtypeStruct((N_TOK, MLA_HEADS * MLA_V), BF16),
            jax.ShapeDtypeStruct((N_TOK, MLA_KV_RANK), F32),
        ],
        compiler_params=_cparams(("arbitrary",)),
        name="mla_prep",
    )(zs, q_a_norm, kv_a_norm, w_uq_r, w_k_r, w_v_r, q_norm_p, k_norm_p, *rope_tabs)


def _cache_kv_kernel(ckv_ref, kr_ref, wk_ref, wv_ref, kn_ref, kk_ref, v_ref):
    cb = ckv_ref[...].astype(BF16)
    kf = _dot(cb, wk_ref[0])
    v_ref[...] = _dot(cb, wv_ref[0]).astype(BF16)
    kr = kr_ref[...]
    for h in range(MLA_HEADS):
        hs = slice(h * HEAD_PAD, (h + 1) * HEAD_PAD)
        kk_ref[:, hs] = _rms(kf[:, hs] + kr, kn_ref[0], n=MLA_QK).astype(BF16)


def _cache_kv(cache_ckv, cache_kr_pad, w_k_r, w_v_r, k_norm_p):
    lw = lambda shape: pl.BlockSpec((1,) + shape, lambda b, l: (l,) + (0,) * len(shape))
    blk = lambda w: pl.BlockSpec((None, None, PAST_LEN, w), lambda b, l: (b, l, 0, 0))
    return pl.pallas_call(
        _cache_kv_kernel,
        grid=(N_SEQ_SMP, DEPTH),
        in_specs=[blk(MLA_KV_RANK), blk(LANES), lw((MLA_KV_RANK, MLA_HEADS * HEAD_PAD)),
                  lw((MLA_KV_RANK, MLA_HEADS * MLA_V)), lw((1, HEAD_PAD))],
        out_specs=[blk(MLA_HEADS * HEAD_PAD), blk(MLA_HEADS * MLA_V)],
        out_shape=[
            jax.ShapeDtypeStruct((N_SEQ_SMP, DEPTH, PAST_LEN, MLA_HEADS * HEAD_PAD), BF16),
            jax.ShapeDtypeStruct((N_SEQ_SMP, DEPTH, PAST_LEN, MLA_HEADS * MLA_V), BF16),
        ],
        compiler_params=_cparams(("arbitrary", "arbitrary")),
        name="cache_kv",
    )(cache_ckv, cache_kr_pad, w_k_r, w_v_r, k_norm_p)


def _make_attn_kernel(n_src):
    scale = MLA_QK ** -0.5

    def kern(q_ref, *refs):
        o_ref = refs[2 * n_src]
        for h in range(MLA_HEADS):
            hs = slice(h * HEAD_PAD, (h + 1) * HEAD_PAD)
            vs = slice(h * MLA_V, (h + 1) * MLA_V)
            q = q_ref[:, hs]
            ss = [_dot_nt(q, refs[2 * j][:, hs]) * scale for j in range(n_src)]
            m = functools.reduce(jnp.maximum, [jnp.max(s, axis=1, keepdims=True) for s in ss])
            ps = [jnp.exp(s - m) for s in ss]
            l = functools.reduce(jnp.add, [jnp.sum(p, axis=1, keepdims=True) for p in ps])
            o = functools.reduce(jnp.add, [_dot(ps[j].astype(BF16), refs[2 * j + 1][:, vs]) for j in range(n_src)])
            o_ref[:, vs] = (o / l).astype(o_ref.dtype)

    return kern


def _attn_ctx(q, kk, v):
    blk = lambda w: pl.BlockSpec((S_CTX, w), lambda b: (b, 0))
    return pl.pallas_call(
        _make_attn_kernel(1),
        grid=(N_SEQ_CTX,),
        in_specs=[blk(MLA_HEADS * HEAD_PAD), blk(MLA_HEADS * HEAD_PAD), blk(MLA_HEADS * MLA_V)],
        out_specs=blk(MLA_HEADS * MLA_V),
        out_shape=jax.ShapeDtypeStruct((N_CTX, MLA_HEADS * MLA_V), BF16),
        compiler_params=_cparams(("arbitrary",)),
        name="attn_ctx",
    )(q, kk, v)


def _attn_smp(layer, q, kk, v, kk_cache, v_cache):
    row0 = N_CTX // S_SMP
    nq = S_SMP // TQ
    seqb = lambda w: pl.BlockSpec((S_SMP, w), lambda b, i: (row0 + b, 0))
    cache = lambda w: pl.BlockSpec((None, None, PAST_LEN, w), lambda b, i: (b, layer, 0, 0))
    return pl.pallas_call(
        _make_attn_kernel(2),
        grid=(N_SEQ_SMP, nq),
        in_specs=[
            pl.BlockSpec((TQ, MLA_HEADS * HEAD_PAD), lambda b, i: (N_CTX // TQ + b * nq + i, 0)),
            seqb(MLA_HEADS * HEAD_PAD), seqb(MLA_HEADS * MLA_V),
            cache(MLA_HEADS * HEAD_PAD), cache(MLA_HEADS * MLA_V),
        ],
        out_specs=pl.BlockSpec((TQ, MLA_HEADS * MLA_V), lambda b, i: (b * nq + i, 0)),
        out_shape=jax.ShapeDtypeStruct((N_SMP, MLA_HEADS * MLA_V), BF16),
        compiler_params=_cparams(("arbitrary", "arbitrary")),
        name="attn_smp",
    )(q, kk, v, kk_cache, v_cache)


def _merge_kernel(x_ref, oa_ref, ob_ref, zc_ref, zb_ref, sgn_ref, ws_ref, bs_ref, wb_ref, wo_ref, g1_ref, n2_ref,
                  sh2_ref, sc2_ref, wr_ref, br_ref,
                  xmid_ref, h2_ref, selt_ref, wgt_ref, rankt_ref, before_ref, after_ref, oc_scr, carry_scr):
    i = pl.program_id(0)

    @pl.when(i % WIN_PER_PART == 0)
    def _():
        carry_scr[...] = jnp.zeros_like(carry_scr)

    u = _gelu(zc_ref[:, 0:SG_WIDTH].astype(F32))
    vg = _gelu(zc_ref[:, SG_WIDTH:2 * SG_WIDTH].astype(F32))
    for g in range(SG_GROUPS):
        gs = slice(g * SG_DIM, (g + 1) * SG_DIM)
        vn = _rms(vg[:, gs], sgn_ref[0][:, gs]).astype(BF16)
        for c in range(TW // SG_CHUNK):
            cs = slice(c * SG_CHUNK, (c + 1) * SG_CHUNK)
            mixed = _dot(ws_ref[0, g], vn[cs, :]) + bs_ref[0][:, g:g + 1]
            oc_scr[cs, gs] = (u[cs, gs] * mixed).astype(BF16)

    acc = jnp.zeros((TW, D_MODEL), F32)
    for j, src in enumerate((oa_ref, ob_ref, oc_scr)):
        gate = jax.nn.sigmoid(zb_ref[:, j * D_MODEL:(j + 1) * D_MODEL].astype(F32))
        acc = acc + gate * _dot(src[...], wb_ref[0, j])
    xm = x_ref[...] + g1_ref[0, 0] * _dot(acc.astype(BF16), wo_ref[0])
    xmid_ref[...] = xm
    h2 = _rms(xm, n2_ref[0]) * (1.0 + sc2_ref[0, 0]) + sh2_ref[0, 0]
    h2_ref[...] = h2.astype(BF16)

    logits = jnp.dot(h2, wr_ref[0], precision=HI, preferred_element_type=F32) + br_ref[0]
    lane = lax.broadcasted_iota(jnp.int32, logits.shape, 1)
    sel = jnp.zeros(logits.shape, F32)
    wg = jnp.zeros(logits.shape, F32)
    denom = jnp.zeros((TW, 1), F32)
    top = None
    for _ in range(TOP_K):
        m = jnp.max(logits, axis=1, keepdims=True)
        idx = jnp.min(jnp.where(logits == m, lane, LANES), axis=1, keepdims=True)
        hit = lane == idx
        top = m if top is None else top
        e = jnp.exp(m - top)
        sel = jnp.where(hit, 1.0, sel)
        wg = jnp.where(hit, e, wg)
        denom = denom + e
        logits = jnp.where(hit, -jnp.inf, logits)
    wg = wg / denom

    r_i = lax.broadcasted_iota(jnp.int32, (TW, TW), 0)
    c_i = lax.broadcasted_iota(jnp.int32, (TW, TW), 1)
    carry = carry_scr[0:1, :]
    rank = _dot(jnp.where(c_i < r_i, 1.0, 0.0).astype(BF16), sel.astype(BF16)) + carry
    new_carry = carry + jnp.sum(sel, axis=0, keepdims=True)
    before_ref[0] = jnp.broadcast_to(carry, (SUBLANES, LANES))
    after_ref[0] = jnp.broadcast_to(new_carry, (SUBLANES, LANES))
    carry_scr[...] = jnp.broadcast_to(new_carry, (SUBLANES, LANES))
    selt_ref[...] = sel.T[0:N_EXPERTS, :]
    wgt_ref[...] = wg.T[0:N_EXPERTS, :]
    rankt_ref[...] = rank.T[0:N_EXPERTS, :]


def _merge(layer, x, oa, ob, zc, zb, sg_norm, w_sp, b_sp, w_branch, w_out, mod, norm2, w_router_p, b_router_p):
    lw = lambda shape: pl.BlockSpec((1,) + shape, lambda i: (layer,) + (0,) * len(shape))
    tok = lambda w: pl.BlockSpec((TW, w), lambda i: (i, 0))
    tr = pl.BlockSpec((N_EXPERTS, TW), lambda i: (0, i))
    cnt = pl.BlockSpec((1, SUBLANES, LANES), lambda i: (i, 0, 0))
    return pl.pallas_call(
        _merge_kernel,
        grid=(N_TILES,),
        in_specs=[
            tok(D_MODEL), tok(ML_WIDTH), tok(MLA_HEADS * MLA_V), tok(ZC_W), tok(ZB_W),
            lw((1, SG_WIDTH)), lw((SG_GROUPS, SG_CHUNK, SG_CHUNK)), lw((SG_CHUNK, LANES)),
            lw((N_BRANCH, ML_WIDTH, D_MODEL)), lw((D_MODEL, D_MODEL)),
            _mod_spec(layer, 2), lw((1, D_MODEL)), _mod_spec(layer, 3), _mod_spec(layer, 4),
            lw((D_MODEL, LANES)), lw((1, LANES)),
        ],
        out_specs=[tok(D_MODEL), tok(D_MODEL), tr, tr, tr, cnt, cnt],
        out_shape=[
            jax.ShapeDtypeStruct((N_TOK, D_MODEL), F32),
            jax.ShapeDtypeStruct((N_TOK, D_MODEL), BF16),
            jax.ShapeDtypeStruct((N_EXPERTS, N_TOK), F32),
            jax.ShapeDtypeStruct((N_EXPERTS, N_TOK), F32),
            jax.ShapeDtypeStruct((N_EXPERTS, N_TOK), F32),
            jax.ShapeDtypeStruct((N_TILES, SUBLANES, LANES), F32),
            jax.ShapeDtypeStruct((N_TILES, SUBLANES, LANES), F32),
        ],
        scratch_shapes=[pltpu.VMEM((TW, SG_WIDTH), BF16), pltpu.VMEM((SUBLANES, LANES), F32)],
        compiler_params=_cparams(("arbitrary",)),
        name="merge_router",
    )(x, oa, ob, zc, zb, sg_norm, w_sp, b_sp, w_branch, w_out, mod, norm2, mod, mod, w_router_p, b_router_p)


def _moe_kernel(wstart_ref, cnt_ref, h2_ref, selt_ref, wgt_ref, rankt_ref, w1_ref, b1_ref, w2_ref, b2_ref,
                acc_ref, xs_scr, ws_scr, y_scr):
    p = pl.program_id(0)
    e = pl.program_id(1)

    @pl.when(e == 0)
    def _():
        acc_ref[...] = jnp.zeros_like(acc_ref)

    cnt = cnt_ref[p * N_EXPERTS + e]
    slot_i = lax.broadcasted_iota(jnp.int32, (SLOT_CHUNK, TW), 0).astype(F32)

    def window_bounds(w):
        r0 = wstart_ref[(p * WIN_PER_PART + w) * N_EXPERTS + e]
        r1 = cnt if w == WIN_PER_PART - 1 else wstart_ref[(p * WIN_PER_PART + w + 1) * N_EXPERTS + e]
        return r0, r1

    def one_hot(w, s0):
        ws = slice(w * TW, (w + 1) * TW)
        rank_row = rankt_ref[pl.ds(e, 1), ws]
        sel_row = selt_ref[pl.ds(e, 1), ws]
        hit = (rank_row - s0.astype(F32) == slot_i) & (sel_row > 0.5)
        return hit, ws

    def chunk(j, carry):
        s0 = j * SLOT_CHUNK
        xs_scr[...] = jnp.zeros_like(xs_scr)
        ws_scr[...] = jnp.zeros_like(ws_scr)
        for w in range(WIN_PER_PART):
            r0, r1 = window_bounds(w)

            @pl.when((r1 > s0) & (r0 < s0 + SLOT_CHUNK))
            def _():
                hit, ws = one_hot(w, s0)
                xs_scr[...] += _dot(jnp.where(hit, 1.0, 0.0).astype(BF16), h2_ref[ws, :])
                wrow = wgt_ref[pl.ds(e, 1), ws]
                ws_scr[...] += jnp.sum(jnp.where(hit, wrow, 0.0), axis=1, keepdims=True)

        g = _dot(xs_scr[...].astype(BF16), w1_ref[0, 0]) + b1_ref[0, 0]
        gate = jnp.minimum(g[:, :D_EXPERT], SWIGLU_LIMIT)
        up = jnp.clip(g[:, D_EXPERT:], -SWIGLU_LIMIT, SWIGLU_LIMIT)
        act = gate * jax.nn.sigmoid(SWIGLU_ALPHA * gate) * (up + 1.0)
        y = _dot(act.astype(BF16), w2_ref[0, 0]) + b2_ref[0, 0]
        y_scr[...] = (y * ws_scr[...]).astype(BF16)

        for w in range(WIN_PER_PART):
            r0, r1 = window_bounds(w)

            @pl.when((r1 > s0) & (r0 < s0 + SLOT_CHUNK))
            def _():
                hit, ws = one_hot(w, s0)
                pt = jnp.where(hit, 1.0, 0.0).astype(BF16)
                acc_ref[ws, :] += lax.dot_general(pt, y_scr[...], (((0,), (0,)), ((), ())), preferred_element_type=F32)

        return carry

    lax.fori_loop(0, (cnt + SLOT_CHUNK - 1) // SLOT_CHUNK, chunk, 0)


def _moe(layer, h2, selt, wgt, rankt, wstart, cnt, w1b, b1, w2b, b2):
    meta = pl.BlockSpec((N_EXPERTS, TP), lambda p, e, *_: (0, p))
    ew = lambda r, c: pl.BlockSpec((1, 1, r, c), lambda p, e, *_: (layer, e, 0, 0))
    grid_spec = pltpu.PrefetchScalarGridSpec(
        num_scalar_prefetch=2,
        grid=(MOE_PARTS, N_EXPERTS),
        in_specs=[
            pl.BlockSpec((TP, D_MODEL), lambda p, e, *_: (p, 0)),
            meta, meta, meta,
            ew(D_MODEL, 2 * D_EXPERT), ew(1, 2 * D_EXPERT), ew(D_EXPERT, D_MODEL), ew(1, D_MODEL),
        ],
        out_specs=pl.BlockSpec((TP, D_MODEL), lambda p, e, *_: (p, 0)),
        scratch_shapes=[
            pltpu.VMEM((SLOT_CHUNK, D_MODEL), F32),
            pltpu.VMEM((SLOT_CHUNK, 1), F32),
            pltpu.VMEM((SLOT_CHUNK, D_MODEL), BF16),
        ],
    )
    return pl.pallas_call(
        _moe_kernel,
        grid_spec=grid_spec,
        out_shape=jax.ShapeDtypeStruct((N_TOK, D_MODEL), F32),
        compiler_params=_cparams(("arbitrary", "arbitrary")),
        name="moe",
    )(wstart, cnt, h2, selt, wgt, rankt, w1b, b1, w2b, b2)


def _residual_kernel(x_ref, a_ref, g_ref, o_ref):
    o_ref[...] = x_ref[...] + g_ref[0, 0] * a_ref[...]


def _residual(layer, xmid, acc, mod):
    tok = pl.BlockSpec((TW, D_MODEL), lambda i: (i, 0))
    return pl.pallas_call(
        _residual_kernel,
        grid=(N_TILES,),
        in_specs=[tok, tok, _mod_spec(layer, 5)],
        out_specs=tok,
        out_shape=jax.ShapeDtypeStruct((N_TOK, D_MODEL), F32),
        compiler_params=_cparams(("arbitrary",)),
        name="residual",
    )(xmid, acc, mod)


def _rope_tables():
    pos = np.arange(S_SMP)
    half = MLA_ROPE // 2
    inv_freq = (ROPE_THETA ** (-(np.arange(0, half, 2, dtype=np.float32) / np.float32(half)))).astype(np.float32)
    angs = [((pos // GRID_W).astype(np.float32)[:, None] * inv_freq[None, :]).astype(np.float32),
            ((pos % GRID_W).astype(np.float32)[:, None] * inv_freq[None, :]).astype(np.float32)]
    nf = half // 2
    cos = np.ones((TW + S_SMP, LANES), np.float32)
    sin_a = np.zeros((TW + S_SMP, LANES), np.float32)
    sin_b = np.zeros((TW + S_SMP, LANES), np.float32)
    for axis, ang in enumerate(angs):
        base = MLA_NOPE + axis * half
        c, s = np.cos(ang.astype(np.float64)), np.sin(ang.astype(np.float64))
        cos[TW:, base:base + nf] = c
        cos[TW:, base + nf:base + half] = c
        sin_a[TW:, base:base + nf] = -s
        sin_b[TW:, base + nf:base + half] = s
    return jnp.asarray(cos), jnp.asarray(sin_a), jnp.asarray(sin_b)


def _pad_last(a, width):
    return jnp.pad(a, [(0, 0)] * (a.ndim - 1) + [(0, width - a.shape[-1])])


def kernel(x_prompt, x_sample, cache_mla_ckv, cache_mla_krope, state_mlstm_C, state_mlstm_n, state_mlstm_m, c, c_ctx, norm1, norm2, w_ada, b_ada, w_in, b_mlstm_gates, mlstm_norm, mla_q_a_norm, mla_kv_a_norm, w_uq, w_ukv, mla_q_norm, mla_k_norm, sg_norm, w_spatial, b_spatial, w_branch, w_out, w_router, b_router, w_exp1, b_exp1, w_exp2, b_exp2):
    o = np.cumsum((0, ML_WIDTH, ML_WIDTH, ML_WIDTH, ML_WIDTH, 4 * ML_HEADS, MLA_Q_RANK, MLA_KV_RANK, MLA_ROPE,
                   SG_WIDTH, SG_WIDTH, N_BRANCH * D_MODEL))
    seg = lambda j: w_in[:, :, o[j]:o[j + 1]]
    w_in_r = jnp.concatenate(
        [seg(0), seg(1) * (ML_DIM ** -0.5), seg(2), seg(3), seg(5), seg(6), seg(7), seg(4),
         jnp.zeros((DEPTH, D_MODEL, ZS_W - (MLA_Q_RANK + MLA_KV_RANK + MLA_ROPE + 4 * ML_HEADS)), F32),
         seg(8), seg(9), seg(10)], axis=-1).astype(BF16)
    w_uq_r = _pad_last(w_uq.reshape(DEPTH, MLA_Q_RANK, MLA_HEADS, MLA_QK), HEAD_PAD).reshape(
        DEPTH, MLA_Q_RANK, MLA_HEADS * HEAD_PAD).astype(BF16)
    w_ukv4 = w_ukv.reshape(DEPTH, MLA_KV_RANK, MLA_HEADS, MLA_NOPE + MLA_V)
    w_k_r = _pad_last(w_ukv4[..., :MLA_NOPE], HEAD_PAD).reshape(DEPTH, MLA_KV_RANK, MLA_HEADS * HEAD_PAD).astype(BF16)
    w_v_r = w_ukv4[..., MLA_NOPE:].reshape(DEPTH, MLA_KV_RANK, MLA_HEADS * MLA_V).astype(BF16)
    q_norm_p = _pad_last(mla_q_norm, HEAD_PAD).reshape(DEPTH, 1, HEAD_PAD)
    k_norm_p = _pad_last(mla_k_norm, HEAD_PAD).reshape(DEPTH, 1, HEAD_PAD)
    b_gates_p = jnp.pad(b_mlstm_gates, ((0, 0), (GATE_LANE0, LANES - GATE_LANE0 - 4 * ML_HEADS))).reshape(DEPTH, 1, LANES)
    b_sp = _pad_last(jnp.swapaxes(b_spatial, 1, 2), LANES)
    w_router_p = _pad_last(w_router, LANES)
    b_router_p = jnp.pad(b_router, ((0, 0), (0, LANES - N_EXPERTS)), constant_values=-1e30).reshape(DEPTH, 1, LANES)
    r3 = lambda a: a.reshape(DEPTH, 1, a.shape[-1])
    cache_kr_pad = jnp.pad(cache_mla_krope, ((0, 0), (0, 0), (0, 0), (MLA_NOPE, LANES - MLA_QK)))
    rope_tabs = _rope_tables()

    cvec = jnp.concatenate([c_ctx[None, :], c, jnp.zeros((SUBLANES - 1 - N_SEQ_SMP, D_MODEL), F32)], axis=0)
    mod = _adaln(cvec, w_ada, b_ada).reshape(DEPTH, SUBLANES, 1, 6 * D_MODEL)
    w1b = _cast_experts(w_exp1)
    w2b = _cast_experts(w_exp2)
    b1 = b_exp1.reshape(DEPTH, N_EXPERTS, 1, 2 * D_EXPERT)
    b2 = b_exp2.reshape(DEPTH, N_EXPERTS, 1, D_MODEL)
    kk_cache, v_cache = _cache_kv(cache_mla_ckv, cache_kr_pad, w_k_r, w_v_r, k_norm_p)

    x = jnp.concatenate([x_prompt.reshape(N_CTX, D_MODEL), x_sample.reshape(N_SMP, D_MODEL)], axis=0)
    ckv_l, krope_l, c_l, n_l, m_l = [], [], [], [], []
    for l in range(DEPTH):
        za, zs, zc, zb = _inproj(l, x, r3(norm1), mod, w_in_r)
        oa_ctx, c_fin, n_fin, m_fin = _mlstm(l, za, zs, b_gates_p, r3(mlstm_norm))
        (oa_smp,) = _mlstm(l, za, zs, b_gates_p, r3(mlstm_norm), init=(state_mlstm_C, state_mlstm_n, state_mlstm_m))
        q, kk, v, ckvn = _mla_prep(l, zs, r3(mla_q_a_norm), r3(mla_kv_a_norm), w_uq_r, w_k_r, w_v_r, q_norm_p, k_norm_p, rope_tabs)
        ob_ctx = _attn_ctx(q, kk, v)
        ob_smp = _attn_smp(l, q, kk, v, kk_cache, v_cache)
        oa = jnp.concatenate([oa_ctx, oa_smp], axis=0)
        ob = jnp.concatenate([ob_ctx, ob_smp], axis=0)
        xmid, h2, selt, wgt, rankt, before, after = _merge(
            l, x, oa, ob, zc, zb, r3(sg_norm), w_spatial.astype(BF16), b_sp, w_branch.astype(BF16), w_out.astype(BF16),
            mod, r3(norm2), w_router_p, b_router_p)
        wstart = before[:, 0, :N_EXPERTS].astype(jnp.int32).reshape(-1)
        cnt = after[WIN_PER_PART - 1::WIN_PER_PART, 0, :N_EXPERTS].astype(jnp.int32).reshape(-1)
        acc = _moe(l, h2, selt, wgt, rankt, wstart, cnt, w1b, b1, w2b, b2)
        x = _residual(l, xmid, acc, mod)
        ckv_l.append(ckvn[:N_CTX].reshape(N_SEQ_CTX, S_CTX, MLA_KV_RANK))
        krope_l.append(zs[:N_CTX, ZS_W - LANES:ZS_W - LANES + MLA_ROPE].reshape(N_SEQ_CTX, S_CTX, MLA_ROPE))
        c_l.append(c_fin)
        n_l.append(n_fin)
        m_l.append(m_fin[:, :, 0].reshape(N_SEQ_CTX, 2, ML_HEADS))
    return (
        x[:N_CTX].reshape(N_SEQ_CTX, S_CTX, D_MODEL),
        x[N_CTX:].reshape(N_SEQ_SMP, S_SMP, D_MODEL),
        jnp.stack(ckv_l, axis=1),
        jnp.stack(krope_l, axis=1),
        jnp.stack(c_l, axis=1),
        jnp.stack(n_l, axis=1),
        jnp.stack(m_l, axis=1),
    )
```

```python
import functools

import numpy as np
import jax
import jax.numpy as jnp
from jax import lax
from jax.experimental import pallas as pl
from jax.experimental.pallas import tpu as pltpu
from jax.experimental.pallas import tpu_sc as plsc

F32 = jnp.float32
BF16 = jnp.bfloat16
HI = lax.Precision.HIGHEST

D_MODEL = 1024
N_SEQ_CTX, S_CTX = 32, 256
N_SEQ_SMP, S_SMP = 2, 1024
DEPTH = 4
PAST_LEN = 512
GRID_W = 64
EPS = 1e-6
ML_HEADS, ML_DIM = 4, 128
ML_WIDTH = ML_HEADS * ML_DIM
MLA_HEADS, MLA_NOPE, MLA_ROPE, MLA_V = 8, 64, 32, 64
MLA_QK = MLA_NOPE + MLA_ROPE
MLA_Q_RANK, MLA_KV_RANK = 256, 128
ROPE_THETA = 10000.0
SG_GROUPS, SG_DIM, SG_CHUNK = 4, 128, 128
SG_WIDTH = SG_GROUPS * SG_DIM
N_BRANCH = 3
N_EXPERTS, TOP_K, D_EXPERT = 32, 4, 1024
SWIGLU_LIMIT, SWIGLU_ALPHA = 7.0, 1.702

N_CTX = N_SEQ_CTX * S_CTX
N_SMP = N_SEQ_SMP * S_SMP
N_TOK = N_CTX + N_SMP

LANES = 128
SUBLANES = 8
VMEM_LIMIT = 56 * 1024 * 1024

TW = 256
N_TILES = N_TOK // TW
N_TILES_CTX = N_CTX // TW
TILES_PER_SMP_SEQ = S_SMP // TW
HEAD_PAD = LANES
TQ = 256
EXPERT_CAP = N_TOK
SLOT_CHUNK = 256
N_CHUNK_STEPS = N_TOK * TOP_K // SLOT_CHUNK + N_EXPERTS
SC_ROW = 256
SC_WIN = 128

ZA_W = 4 * ML_WIDTH
ZS_W = 512
ZC_W = 2 * SG_WIDTH
ZB_W = N_BRANCH * D_MODEL
ZIN_W = ZA_W + ZS_W + ZC_W + ZB_W
GATE_LANE0 = MLA_ROPE


def _cparams(sem):
    return pltpu.CompilerParams(dimension_semantics=sem, vmem_limit_bytes=VMEM_LIMIT)


def _mod_row(i):
    return jnp.where(i < N_TILES_CTX, 0, 1 + (i - N_TILES_CTX) // TILES_PER_SMP_SEQ)


def _rms(x, g, n=None):
    ms = jnp.sum(x * x, axis=-1, keepdims=True) * (1.0 / (n or x.shape[-1]))
    return x * lax.rsqrt(ms + EPS) * g


def _gelu(x):
    return 0.5 * x * (1.0 + jnp.tanh(0.7978845608028654 * (x + 0.044715 * (x * x * x))))


def _dot(a, b):
    return jnp.dot(a, b, preferred_element_type=F32)


def _dot_nt(a, b):
    return lax.dot_general(a, b, (((1,), (1,)), ((), ())), preferred_element_type=F32)


def _adaln_kernel(c_ref, w_ref, b_ref, o_ref):
    c = c_ref[...]
    s = c * jax.nn.sigmoid(c)
    o_ref[0] = jnp.dot(s, w_ref[0], precision=HI, preferred_element_type=F32) + b_ref[0]


def _adaln(cvec, w_ada, b_ada):
    nchunk = 4
    cw = 6 * D_MODEL // nchunk
    return pl.pallas_call(
        _adaln_kernel,
        grid=(DEPTH, nchunk),
        in_specs=[
            pl.BlockSpec((SUBLANES, D_MODEL), lambda l, j: (0, 0)),
            pl.BlockSpec((1, D_MODEL, cw), lambda l, j: (l, 0, j)),
            pl.BlockSpec((1, 1, cw), lambda l, j: (l, 0, j)),
        ],
        out_specs=pl.BlockSpec((1, SUBLANES, cw), lambda l, j: (l, 0, j)),
        out_shape=jax.ShapeDtypeStruct((DEPTH, SUBLANES, 6 * D_MODEL), F32),
        compiler_params=_cparams(("arbitrary", "arbitrary")),
        name="adaln",
    )(cvec, w_ada, b_ada.reshape(DEPTH, 1, 6 * D_MODEL))


def _inproj_kernel(x_ref, g_ref, sh_ref, sc_ref, w_ref, za_ref, zs_ref, zc_ref, zb_ref):
    h = _rms(x_ref[...], g_ref[0]) * (1.0 + sc_ref[0, 0]) + sh_ref[0, 0]
    hb = h.astype(BF16)
    za_ref[...] = _dot(hb, w_ref[0, :, 0:ZA_W]).astype(BF16)
    zs_ref[...] = _dot(hb, w_ref[0, :, ZA_W:ZA_W + ZS_W])
    zc_ref[...] = _dot(hb, w_ref[0, :, ZA_W + ZS_W:ZA_W + ZS_W + ZC_W]).astype(BF16)
    zb_ref[...] = _dot(hb, w_ref[0, :, ZA_W + ZS_W + ZC_W:ZIN_W]).astype(BF16)


def _mod_spec(layer, k):
    return pl.BlockSpec((1, 1, 1, D_MODEL), lambda i: (layer, _mod_row(i), 0, k))


def _inproj(layer, x, norm1, mod, w_in_r):
    tok = lambda w: pl.BlockSpec((TW, w), lambda i: (i, 0))
    return pl.pallas_call(
        _inproj_kernel,
        grid=(N_TILES,),
        in_specs=[
            tok(D_MODEL),
            pl.BlockSpec((1, 1, D_MODEL), lambda i: (layer, 0, 0)),
            _mod_spec(layer, 0),
            _mod_spec(layer, 1),
            pl.BlockSpec((1, D_MODEL, ZIN_W), lambda i: (layer, 0, 0)),
        ],
        out_specs=[tok(ZA_W), tok(ZS_W), tok(ZC_W), tok(ZB_W)],
        out_shape=[
            jax.ShapeDtypeStruct((N_TOK, ZA_W), BF16),
            jax.ShapeDtypeStruct((N_TOK, ZS_W), F32),
            jax.ShapeDtypeStruct((N_TOK, ZC_W), BF16),
            jax.ShapeDtypeStruct((N_TOK, ZB_W), BF16),
        ],
        compiler_params=_cparams(("arbitrary",)),
        name="inproj",
    )(x, norm1, mod, mod, w_in_r)


def _make_mlstm_kernel(seq, layer, has_init):
    nq = seq // TQ
    lane_if, lane_ff, lane_ib, lane_fb = (GATE_LANE0 + ML_HEADS * j for j in range(4))

    def kern(*refs):
        if has_init:
            m0_ref, zq, zk, zv, zo, gz, bg, nrm, c0_ref, n0_ref, out, bp_scr, bs_scr = refs
        else:
            zq, zk, zv, zo, gz, bg, nrm, out, cf_ref, nf_ref, mf_ref, bp_scr, bs_scr = refs
        b = pl.program_id(0)
        g = gz[...] + bg[0]
        lane = lax.broadcasted_iota(jnp.int32, g.shape, 1)
        is_forget = ((lane >= lane_ff) & (lane < lane_ib)) | ((lane >= lane_fb) & (lane < lane_fb + ML_HEADS))
        log_sig = jnp.minimum(g, 0.0) - jnp.log1p(jnp.exp(-jnp.abs(g)))
        a = jnp.where(is_forget, log_sig, g)
        r_i = lax.broadcasted_iota(jnp.int32, (seq, seq), 0)
        c_i = lax.broadcasted_iota(jnp.int32, (seq, seq), 1)
        ltri = (c_i <= r_i).astype(F32)
        bp = jnp.dot(ltri, a, precision=HI, preferred_element_type=F32)
        bs = bp[seq - 1:seq, :] - bp + a
        bp_scr[...] = bp
        bs_scr[...] = bs
        a_t, bp_t, bs_t = a.T, bp.T, bs.T

        for h in range(ML_HEADS):
            hs = slice(h * ML_DIM, (h + 1) * ML_DIM)
            k = zk[:, hs]
            v = zv[:, hs]
            rows = (
                a_t[lane_if + h:lane_if + h + 1, :] - bp_t[lane_ff + h:lane_ff + h + 1, :],
                a_t[lane_ib + h:lane_ib + h + 1, :] - bs_t[lane_fb + h:lane_fb + h + 1, :],
            )
            col_refs = ((bp_scr, lane_ff + h), (bs_scr, lane_fb + h))
            if has_init:
                m0 = tuple(m0_ref[((b * DEPTH + layer) * 2 + dr) * ML_HEADS + h] for dr in range(2))
                c0 = tuple(c0_ref[0, 0, dr, h].astype(BF16) for dr in range(2))
                n0 = tuple(n0_ref[0, 0, dr, h:h + 1, :] for dr in range(2))
            else:
                m0 = (0.0, 0.0)

            def qblock(qi, carry):
                q0 = pl.multiple_of(qi * TQ, TQ)
                qb = zq[pl.ds(q0, TQ), hs]
                sc = _dot_nt(qb, k)
                t_idx = q0 + lax.broadcasted_iota(jnp.int32, (TQ, seq), 0)
                s_idx = lax.broadcasted_iota(jnp.int32, (TQ, seq), 1)
                hsum = jnp.zeros((TQ, ML_DIM), F32)
                for dr in range(2):
                    cref, cl = col_refs[dr]
                    col = cref[pl.ds(q0, TQ), cl:cl + 1]
                    mask = (s_idx <= t_idx) if dr == 0 else (s_idx >= t_idx)
                    d = jnp.where(mask, col + rows[dr], -jnp.inf)
                    a0 = col + m0[dr]
                    m_t = jnp.maximum(a0, jnp.max(d, axis=1, keepdims=True))
                    s = sc * jnp.exp(d - m_t)
                    num = _dot(s.astype(BF16), v)
                    den = jnp.sum(s, axis=1, keepdims=True)
                    if has_init:
                        w_c = jnp.exp(a0 - m_t)
                        num = num + w_c * _dot(qb, c0[dr])
                        den = den + w_c * jnp.sum(qb.astype(F32) * n0[dr], axis=1, keepdims=True)
                    hsum = hsum + num / jnp.maximum(jnp.abs(den), jnp.exp(-m_t))
                hn = _rms(hsum, nrm[0][:, hs])
                og = zo[pl.ds(q0, TQ), hs].astype(F32)
                out[pl.ds(q0, TQ), hs] = (hn * jax.nn.sigmoid(og)).astype(out.dtype)
                return carry

            if nq == 1:
                qblock(0, 0)
            else:
                lax.fori_loop(0, nq, qblock, 0)

            if not has_init:
                k_t = k.astype(F32).T
                kf = k.astype(F32)
                tot = (bp_t[lane_ff + h:lane_ff + h + 1, seq - 1:seq], bp_t[lane_fb + h:lane_fb + h + 1, seq - 1:seq])
                gl = (
                    tot[0] + rows[0],
                    bp_t[lane_fb + h:lane_fb + h + 1, :] - a_t[lane_fb + h:lane_fb + h + 1, :]
                    + a_t[lane_ib + h:lane_ib + h + 1, :],
                )
                for dr in range(2):
                    m_new = jnp.maximum(tot[dr] + m0[dr], jnp.max(gl[dr], axis=1, keepdims=True))
                    w_s = jnp.exp(gl[dr] - m_new)
                    cf_ref[0, dr, h] = _dot((k_t * w_s).astype(BF16), v)
                    n_new = jnp.dot(jnp.broadcast_to(w_s, (SUBLANES, seq)), kf, precision=HI, preferred_element_type=F32)
                    nf_ref[0, dr, h:h + 1, :] = n_new[0:1, :]
                    mf_ref[0, dr * ML_HEADS + h:dr * ML_HEADS + h + 1, :] = jnp.broadcast_to(m_new, (1, LANES))

    return kern


def _mlstm(layer, za, zs, b_gates, mlstm_norm, init=None):
    has_init = init is not None
    seq, nseq, row0 = (S_SMP, N_SEQ_SMP, N_CTX // S_SMP) if has_init else (S_CTX, N_SEQ_CTX, 0)
    qkvo = [pl.BlockSpec((seq, ML_WIDTH), functools.partial(lambda j, b: (row0 + b, j), j)) for j in range(4)]
    in_specs = qkvo + [
        pl.BlockSpec((seq, LANES), lambda b: (row0 + b, ZS_W // LANES - 1)),
        pl.BlockSpec((1, 1, LANES), lambda b: (layer, 0, 0)),
        pl.BlockSpec((1, 1, ML_WIDTH), lambda b: (layer, 0, 0)),
    ]
    args = [za, za, za, za, zs, b_gates, mlstm_norm]
    out_specs = [pl.BlockSpec((seq, ML_WIDTH), lambda b: (b, 0))]
    out_shape = [jax.ShapeDtypeStruct((nseq * seq, ML_WIDTH), BF16)]
    if has_init:
        st_c, st_n, st_m = init
        in_specs = [pl.BlockSpec(memory_space=pltpu.SMEM)] + in_specs + [
            pl.BlockSpec((1, 1, 2, ML_HEADS, ML_DIM, ML_DIM), lambda b: (b, layer, 0, 0, 0, 0)),
            pl.BlockSpec((1, 1, 2, ML_HEADS, ML_DIM), lambda b: (b, layer, 0, 0, 0)),
        ]
        args = [st_m.reshape(-1)] + args + [st_c, st_n]
    else:
        out_specs += [
            pl.BlockSpec((1, 2, ML_HEADS, ML_DIM, ML_DIM), lambda b: (b, 0, 0, 0, 0)),
            pl.BlockSpec((1, 2, ML_HEADS, ML_DIM), lambda b: (b, 0, 0, 0)),
            pl.BlockSpec((1, 2 * ML_HEADS, LANES), lambda b: (b, 0, 0)),
        ]
        out_shape += [
            jax.ShapeDtypeStruct((nseq, 2, ML_HEADS, ML_DIM, ML_DIM), F32),
            jax.ShapeDtypeStruct((nseq, 2, ML_HEADS, ML_DIM), F32),
            jax.ShapeDtypeStruct((nseq, 2 * ML_HEADS, LANES), F32),
        ]
    return pl.pallas_call(
        _make_mlstm_kernel(seq, layer, has_init),
        grid=(nseq,),
        in_specs=in_specs,
        out_specs=out_specs,
        out_shape=out_shape,
        scratch_shapes=[pltpu.VMEM((seq, LANES), F32), pltpu.VMEM((seq, LANES), F32)],
        compiler_params=_cparams(("arbitrary",)),
        name="mlstm_smp" if has_init else "mlstm_ctx",
    )(*args)


def _rope(x, cos, sin_a, sin_b):
    return x * cos + pltpu.roll(x, LANES - 8, 1) * sin_a + pltpu.roll(x, 8, 1) * sin_b


def _mla_prep_kernel(zs_ref, qa_ref, kva_ref, wuq_ref, wk_ref, wv_ref, qn_ref, kn_ref, cos_ref, sa_ref, sb_ref,
                     q_ref, kk_ref, v_ref, ckv_ref):
    cq = zs_ref[:, 0:MLA_Q_RANK]
    ckv = zs_ref[:, MLA_Q_RANK:MLA_Q_RANK + MLA_KV_RANK]
    last = zs_ref[:, ZS_W - LANES:ZS_W]
    qf = _dot(_rms(cq, qa_ref[0]).astype(BF16), wuq_ref[0])
    ckvn = _rms(ckv, kva_ref[0])
    ckv_ref[...] = ckvn
    cb = ckvn.astype(BF16)
    kf = _dot(cb, wk_ref[0])
    v_ref[...] = _dot(cb, wv_ref[0]).astype(BF16)
    lane = lax.broadcasted_iota(jnp.int32, last.shape, 1)
    kr = jnp.where((lane >= MLA_NOPE) & (lane < MLA_QK), pltpu.roll(last, MLA_NOPE, 1), 0.0)
    cos, sa, sb = cos_ref[...], sa_ref[...], sb_ref[...]
    for h in range(MLA_HEADS):
        hs = slice(h * HEAD_PAD, (h + 1) * HEAD_PAD)
        q_ref[:, hs] = _rope(_rms(qf[:, hs], qn_ref[0], n=MLA_QK), cos, sa, sb).astype(BF16)
        kk_ref[:, hs] = _rope(_rms(kf[:, hs] + kr, kn_ref[0], n=MLA_QK), cos, sa, sb).astype(BF16)


def _mla_prep(layer, zs, q_a_norm, kv_a_norm, w_uq_r, w_k_r, w_v_r, q_norm_p, k_norm_p, rope_tabs):
    lw = lambda shape: pl.BlockSpec((1,) + shape, lambda i: (layer,) + (0,) * len(shape))
    tab = pl.BlockSpec((TW, LANES), lambda i: (jnp.where(i < N_TILES_CTX, 0, 1 + (i - N_TILES_CTX) % TILES_PER_SMP_SEQ), 0))
    tok = lambda w: pl.BlockSpec((TW, w), lambda i: (i, 0))
    return pl.pallas_call(
        _mla_prep_kernel,
        grid=(N_TILES,),
        in_specs=[
            tok(ZS_W), lw((1, MLA_Q_RANK)), lw((1, MLA_KV_RANK)),
            lw((MLA_Q_RANK, MLA_HEADS * HEAD_PAD)), lw((MLA_KV_RANK, MLA_HEADS * HEAD_PAD)),
            lw((MLA_KV_RANK, MLA_HEADS * MLA_V)), lw((1, HEAD_PAD)), lw((1, HEAD_PAD)), tab, tab, tab,
        ],
        out_specs=[tok(MLA_HEADS * HEAD_PAD), tok(MLA_HEADS * HEAD_PAD), tok(MLA_HEADS * MLA_V), tok(MLA_KV_RANK)],
        out_shape=[
            jax.ShapeDtypeStruct((N_TOK, MLA_HEADS * HEAD_PAD), BF16),
            jax.ShapeDtypeStruct((N_TOK, MLA_HEADS * HEAD_PAD), BF16),
            jax.ShapeDtypeStruct((N_TOK, MLA_HEADS * MLA_V), BF16),
            jax.ShapeDtypeStruct((N_TOK, MLA_KV_RANK), F32),
        ],
        compiler_params=_cparams(("arbitrary",)),
        name="mla_prep",
    )(zs, q_a_norm, kv_a_norm, w_uq_r, w_k_r, w_v_r, q_norm_p, k_norm_p, *rope_tabs)


def _cache_kv_kernel(ckv_ref, kr_ref, wk_ref, wv_ref, kn_ref, kk_ref, v_ref):
    cb = ckv_ref[...].astype(BF16)
    kf = _dot(cb, wk_ref[0])
    v_ref[...] = _dot(cb, wv_ref[0]).astype(BF16)
    kr = kr_ref[...]
    for h in range(MLA_HEADS):
        hs = slice(h * HEAD_PAD, (h + 1) * HEAD_PAD)
        kk_ref[:, hs] = _rms(kf[:, hs] + kr, kn_ref[0], n=MLA_QK).astype(BF16)


def _cache_kv(cache_ckv, cache_kr_pad, w_k_r, w_v_r, k_norm_p):
    lw = lambda shape: pl.BlockSpec((1,) + shape, lambda b, l: (l,) + (0,) * len(shape))
    blk = lambda w: pl.BlockSpec((None, None, PAST_LEN, w), lambda b, l: (b, l, 0, 0))
    return pl.pallas_call(
        _cache_kv_kernel,
        grid=(N_SEQ_SMP, DEPTH),
        in_specs=[blk(MLA_KV_RANK), blk(LANES), lw((MLA_KV_RANK, MLA_HEADS * HEAD_PAD)),
                  lw((MLA_KV_RANK, MLA_HEADS * MLA_V)), lw((1, HEAD_PAD))],
        out_specs=[blk(MLA_HEADS * HEAD_PAD), blk(MLA_HEADS * MLA_V)],
        out_shape=[
            jax.ShapeDtypeStruct((N_SEQ_SMP, DEPTH, PAST_LEN, MLA_HEADS * HEAD_PAD), BF16),
            jax.ShapeDtypeStruct((N_SEQ_SMP, DEPTH, PAST_LEN, MLA_HEADS * MLA_V), BF16),
        ],
        compiler_params=_cparams(("arbitrary", "arbitrary")),
        name="cache_kv",
    )(cache_ckv, cache_kr_pad, w_k_r, w_v_r, k_norm_p)


def _make_attn_kernel(n_src):
    scale = MLA_QK ** -0.5

    def kern(q_ref, *refs):
        o_ref = refs[2 * n_src]
        for h in range(MLA_HEADS):
            hs = slice(h * HEAD_PAD, (h + 1) * HEAD_PAD)
            vs = slice(h * MLA_V, (h + 1) * MLA_V)
            q = q_ref[:, hs]
            ss = [_dot_nt(q, refs[2 * j][:, hs]) * scale for j in range(n_src)]
            m = functools.reduce(jnp.maximum, [jnp.max(s, axis=1, keepdims=True) for s in ss])
            ps = [jnp.exp(s - m) for s in ss]
            l = functools.reduce(jnp.add, [jnp.sum(p, axis=1, keepdims=True) for p in ps])
            o = functools.reduce(jnp.add, [_dot(ps[j].astype(BF16), refs[2 * j + 1][:, vs]) for j in range(n_src)])
            o_ref[:, vs] = (o / l).astype(o_ref.dtype)

    return kern


def _attn_ctx(q, kk, v):
    blk = lambda w: pl.BlockSpec((S_CTX, w), lambda b: (b, 0))
    return pl.pallas_call(
        _make_attn_kernel(1),
        grid=(N_SEQ_CTX,),
        in_specs=[blk(MLA_HEADS * HEAD_PAD), blk(MLA_HEADS * HEAD_PAD), blk(MLA_HEADS * MLA_V)],
        out_specs=blk(MLA_HEADS * MLA_V),
        out_shape=jax.ShapeDtypeStruct((N_CTX, MLA_HEADS * MLA_V), BF16),
        compiler_params=_cparams(("arbitrary",)),
        name="attn_ctx",
    )(q, kk, v)


def _attn_smp(layer, q, kk, v, kk_cache, v_cache):
    row0 = N_CTX // S_SMP
    nq = S_SMP // TQ
    seqb = lambda w: pl.BlockSpec((S_SMP, w), lambda b, i: (row0 + b, 0))
    cache = lambda w: pl.BlockSpec((None, None, PAST_LEN, w), lambda b, i: (b, layer, 0, 0))
    return pl.pallas_call(
        _make_attn_kernel(2),
        grid=(N_SEQ_SMP, nq),
        in_specs=[
            pl.BlockSpec((TQ, MLA_HEADS * HEAD_PAD), lambda b, i: (N_CTX // TQ + b * nq + i, 0)),
            seqb(MLA_HEADS * HEAD_PAD), seqb(MLA_HEADS * MLA_V),
            cache(MLA_HEADS * HEAD_PAD), cache(MLA_HEADS * MLA_V),
        ],
        out_specs=pl.BlockSpec((TQ, MLA_HEADS * MLA_V), lambda b, i: (b * nq + i, 0)),
        out_shape=jax.ShapeDtypeStruct((N_SMP, MLA_HEADS * MLA_V), BF16),
        compiler_params=_cparams(("arbitrary", "arbitrary")),
        name="attn_smp",
    )(q, kk, v, kk_cache, v_cache)


def _merge_kernel(x_ref, oa_ref, ob_ref, zc_ref, zb_ref, sgn_ref, ws_ref, bs_ref, wb_ref, wo_ref, g1_ref, n2_ref,
                  sh2_ref, sc2_ref, wr_ref, br_ref,
                  xmid_ref, h2_ref, dest_ref, wsel_ref, cnt_ref, oc_scr, carry_scr):
    i = pl.program_id(0)

    @pl.when(i == 0)
    def _():
        carry_scr[...] = jnp.zeros_like(carry_scr)

    u = _gelu(zc_ref[:, 0:SG_WIDTH].astype(F32))
    vg = _gelu(zc_ref[:, SG_WIDTH:2 * SG_WIDTH].astype(F32))
    for g in range(SG_GROUPS):
        gs = slice(g * SG_DIM, (g + 1) * SG_DIM)
        vn = _rms(vg[:, gs], sgn_ref[0][:, gs]).astype(BF16)
        for c in range(TW // SG_CHUNK):
            cs = slice(c * SG_CHUNK, (c + 1) * SG_CHUNK)
            mixed = _dot(ws_ref[0, g], vn[cs, :]) + bs_ref[0][:, g:g + 1]
            oc_scr[cs, gs] = (u[cs, gs] * mixed).astype(BF16)

    acc = jnp.zeros((TW, D_MODEL), F32)
    for j, src in enumerate((oa_ref, ob_ref, oc_scr)):
        gate = jax.nn.sigmoid(zb_ref[:, j * D_MODEL:(j + 1) * D_MODEL].astype(F32))
        acc = acc + gate * _dot(src[...], wb_ref[0, j])
    xm = x_ref[...] + g1_ref[0, 0] * _dot(acc.astype(BF16), wo_ref[0])
    xmid_ref[...] = xm
    h2 = _rms(xm, n2_ref[0]) * (1.0 + sc2_ref[0, 0]) + sh2_ref[0, 0]
    h2_ref[...] = h2

    logits = jnp.dot(h2, wr_ref[0], precision=HI, preferred_element_type=F32) + br_ref[0]
    lane = lax.broadcasted_iota(jnp.int32, logits.shape, 1)
    hits, exps = [], []
    sel = jnp.zeros(logits.shape, F32)
    denom = jnp.zeros((TW, 1), F32)
    top = None
    for _ in range(TOP_K):
        m = jnp.max(logits, axis=1, keepdims=True)
        idx = jnp.min(jnp.where(logits == m, lane, LANES), axis=1, keepdims=True)
        hit = lane == idx
        top = m if top is None else top
        hits.append(hit)
        exps.append(jnp.exp(m - top))
        sel = jnp.where(hit, 1.0, sel)
        denom = denom + exps[-1]
        logits = jnp.where(hit, -jnp.inf, logits)

    r_i = lax.broadcasted_iota(jnp.int32, (TW, TW), 0)
    c_i = lax.broadcasted_iota(jnp.int32, (TW, TW), 1)
    carry = carry_scr[0:1, :]
    rank = _dot(jnp.where(c_i < r_i, 1.0, 0.0).astype(BF16), sel.astype(BF16)) + carry
    new_carry = carry + jnp.sum(sel, axis=0, keepdims=True)
    carry_scr[...] = jnp.broadcast_to(new_carry, (SUBLANES, LANES))
    cnt_ref[...] = jnp.broadcast_to(new_carry, (SUBLANES, LANES))
    slot = rank + lane.astype(F32) * float(EXPERT_CAP)
    dmat = jnp.zeros(logits.shape, F32)
    wmat = jnp.zeros(logits.shape, F32)
    for k in range(TOP_K):
        dk = jnp.sum(jnp.where(hits[k], slot, 0.0), axis=1, keepdims=True)
        dmat = jnp.where(lane == k, dk, dmat)
        wmat = jnp.where(lane == k, exps[k] / denom, wmat)
    dest_ref[...] = dmat.T[0:SUBLANES, :].astype(jnp.int32)
    wsel_ref[...] = wmat


def _merge(layer, x, oa, ob, zc, zb, sg_norm, w_sp, b_sp, w_branch, w_out, mod, norm2, w_router_p, b_router_p):
    lw = lambda shape: pl.BlockSpec((1,) + shape, lambda i: (layer,) + (0,) * len(shape))
    tok = lambda w: pl.BlockSpec((TW, w), lambda i: (i, 0))
    return pl.pallas_call(
        _merge_kernel,
        grid=(N_TILES,),
        in_specs=[
            tok(D_MODEL), tok(ML_WIDTH), tok(MLA_HEADS * MLA_V), tok(ZC_W), tok(ZB_W),
            lw((1, SG_WIDTH)), lw((SG_GROUPS, SG_CHUNK, SG_CHUNK)), lw((SG_CHUNK, LANES)),
            lw((N_BRANCH, ML_WIDTH, D_MODEL)), lw((D_MODEL, D_MODEL)),
            _mod_spec(layer, 2), lw((1, D_MODEL)), _mod_spec(layer, 3), _mod_spec(layer, 4),
            lw((D_MODEL, LANES)), lw((1, LANES)),
        ],
        out_specs=[tok(D_MODEL), tok(D_MODEL), pl.BlockSpec((SUBLANES, TW), lambda i: (0, i)), tok(LANES),
                   pl.BlockSpec((SUBLANES, LANES), lambda i: (0, 0))],
        out_shape=[
            jax.ShapeDtypeStruct((N_TOK, D_MODEL), F32),
            jax.ShapeDtypeStruct((N_TOK, D_MODEL), F32),
            jax.ShapeDtypeStruct((SUBLANES, N_TOK), jnp.int32),
            jax.ShapeDtypeStruct((N_TOK, LANES), F32),
            jax.ShapeDtypeStruct((SUBLANES, LANES), F32),
        ],
        scratch_shapes=[pltpu.VMEM((TW, SG_WIDTH), BF16), pltpu.VMEM((SUBLANES, LANES), F32)],
        compiler_params=_cparams(("arbitrary",)),
        name="merge_router",
    )(x, oa, ob, zc, zb, sg_norm, w_sp, b_sp, w_branch, w_out, mod, norm2, mod, mod, w_router_p, b_router_p)


def _sc_mesh():
    return plsc.VectorSubcoreMesh(core_axis_name="core", subcore_axis_name="subcore")


def _sc_scatter_rows(x, idx, n_rows):
    m = idx.shape[1]
    n_src_blocks = x.shape[0] // SC_WIN

    @pl.kernel(out_type=jax.ShapeDtypeStruct((n_rows, SC_ROW), x.dtype), mesh=_sc_mesh(), scratch_types=[])
    def scatter(x_hbm, i_hbm, o_hbm):
        def body(x_vmem, i_vmem):
            pltpu.sync_copy(x_vmem, o_hbm.at[i_vmem.at[0]])

        pltpu.emit_pipeline(
            body,
            grid=(m // SC_WIN,),
            in_specs=[pl.BlockSpec((SC_WIN, SC_ROW), lambda i: (i % n_src_blocks, 0)),
                      pl.BlockSpec((1, SC_WIN), lambda i: (0, i))],
            out_specs=[],
            core_axis_name=("core", "subcore"),
            dimension_semantics=(pltpu.PARALLEL,),
        )(x_hbm, i_hbm)

    return scatter(x, idx)


def _sc_gather_rows(x, idx):
    m = idx.shape[1]

    @pl.kernel(out_type=jax.ShapeDtypeStruct((m, SC_ROW), x.dtype), mesh=_sc_mesh())
    def gather(x_hbm, i_hbm, o_hbm):
        def body(i_vmem, o_vmem):
            pltpu.sync_copy(x_hbm.at[i_vmem.at[0]], o_vmem)

        pltpu.emit_pipeline(
            body,
            grid=(m // SC_WIN,),
            in_specs=[pl.BlockSpec((1, SC_WIN), lambda i: (0, i))],
            out_specs=[pl.BlockSpec((SC_WIN, SC_ROW), lambda i: (i, 0))],
            core_axis_name=("core", "subcore"),
            dimension_semantics=(pltpu.PARALLEL,),
        )(i_hbm, o_hbm)

    return gather(x, idx)


def _moe_ffn_kernel(be_ref, br_ref, fl_ref, xs_ref, w1_ref, b1_ref, w2_ref, b2_ref, y_ref, w1b, w2b):
    flags = fl_ref[pl.program_id(0)]

    @pl.when((flags & 2) != 0)
    def _():
        w1b[...] = w1_ref[0, 0].astype(BF16)
        w2b[...] = w2_ref[0, 0].astype(BF16)

    @pl.when((flags & 1) != 0)
    def _():
        g = _dot(xs_ref[...].astype(BF16), w1b[...]) + b1_ref[0, 0]
        gate = jnp.minimum(g[:, :D_EXPERT], SWIGLU_LIMIT)
        up = jnp.clip(g[:, D_EXPERT:], -SWIGLU_LIMIT, SWIGLU_LIMIT)
        act = gate * jax.nn.sigmoid(SWIGLU_ALPHA * gate) * (up + 1.0)
        y_ref[...] = _dot(act.astype(BF16), w2b[...]) + b2_ref[0, 0]


def _moe_ffn(layer, xs, blk_e, blk_row, flags, w1, b1, w2, b2):
    ew = lambda r, c: pl.BlockSpec((1, 1, r, c), lambda g, be, br, fl: (layer, be[g], 0, 0))
    rows = pl.BlockSpec((SLOT_CHUNK, D_MODEL), lambda g, be, br, fl: (br[g], 0))
    grid_spec = pltpu.PrefetchScalarGridSpec(
        num_scalar_prefetch=3,
        grid=(N_CHUNK_STEPS,),
        in_specs=[rows, ew(D_MODEL, 2 * D_EXPERT), ew(1, 2 * D_EXPERT), ew(D_EXPERT, D_MODEL), ew(1, D_MODEL)],
        out_specs=rows,
        scratch_shapes=[pltpu.VMEM((D_MODEL, 2 * D_EXPERT), BF16), pltpu.VMEM((D_EXPERT, D_MODEL), BF16)],
    )
    return pl.pallas_call(
        _moe_ffn_kernel,
        grid_spec=grid_spec,
        out_shape=jax.ShapeDtypeStruct(xs.shape, F32),
        compiler_params=_cparams(("arbitrary",)),
        name="moe_ffn",
    )(blk_e, blk_row, flags, xs, w1, b1, w2, b2)


def _chunk_plan(cnt):
    nch = (cnt + SLOT_CHUNK - 1) // SLOT_CHUNK
    cum = jnp.cumsum(nch)
    steps = jnp.arange(N_CHUNK_STEPS, dtype=jnp.int32)
    g = jnp.minimum(steps, cum[-1] - 1)
    e = jnp.minimum(jnp.searchsorted(cum, g, side="right"), N_EXPERTS - 1).astype(jnp.int32)
    j = g - (cum[e] - nch[e])
    valid = steps < cum[-1]
    flags = valid.astype(jnp.int32) + 2 * (valid & (j == 0)).astype(jnp.int32)
    return e, (e * (EXPERT_CAP // SLOT_CHUNK) + j).astype(jnp.int32), flags


def _combine_kernel(x_ref, yg_ref, w_ref, g_ref, o_ref):
    w = w_ref[...]
    acc = w[:, 0:1] * yg_ref[0]
    for k in range(1, TOP_K):
        acc = acc + w[:, k:k + 1] * yg_ref[k]
    o_ref[...] = x_ref[...] + g_ref[0, 0] * acc


def _combine(layer, xmid, yg, wsel, mod):
    tok = lambda w: pl.BlockSpec((TW, w), lambda i: (i, 0))
    return pl.pallas_call(
        _combine_kernel,
        grid=(N_TILES,),
        in_specs=[tok(D_MODEL), pl.BlockSpec((TOP_K, TW, D_MODEL), lambda i: (0, i, 0)), tok(LANES), _mod_spec(layer, 5)],
        out_specs=tok(D_MODEL),
        out_shape=jax.ShapeDtypeStruct((N_TOK, D_MODEL), F32),
        compiler_params=_cparams(("arbitrary",)),
        name="combine",
    )(xmid, yg, wsel, mod)


def _moe(layer, xmid, h2, dest, wsel, cnt, mod, w1, b1, w2, b2):
    sub = D_MODEL // SC_ROW
    idx = dest[0:TOP_K].reshape(TOP_K * N_TOK, 1) * sub + jnp.arange(sub, dtype=jnp.int32)[None, :]
    idx = idx.reshape(1, TOP_K * N_TOK * sub)
    xs = _sc_scatter_rows(h2.reshape(N_TOK * sub, SC_ROW), idx, N_EXPERTS * EXPERT_CAP * sub)
    blk_e, blk_row, flags = _chunk_plan(cnt[0, :N_EXPERTS].astype(jnp.int32))
    y = _moe_ffn(layer, xs.reshape(N_EXPERTS * EXPERT_CAP, D_MODEL), blk_e, blk_row, flags, w1, b1, w2, b2)
    yg = _sc_gather_rows(y.reshape(N_EXPERTS * EXPERT_CAP * sub, SC_ROW), idx)
    return _combine(layer, xmid, yg.reshape(TOP_K, N_TOK, D_MODEL), wsel, mod)


def _rope_tables():
    pos = np.arange(S_SMP)
    half = MLA_ROPE // 2
    inv_freq = (ROPE_THETA ** (-(np.arange(0, half, 2, dtype=np.float32) / np.float32(half)))).astype(np.float32)
    angs = [((pos // GRID_W).astype(np.float32)[:, None] * inv_freq[None, :]).astype(np.float32),
            ((pos % GRID_W).astype(np.float32)[:, None] * inv_freq[None, :]).astype(np.float32)]
    nf = half // 2
    cos = np.ones((TW + S_SMP, LANES), np.float32)
    sin_a = np.zeros((TW + S_SMP, LANES), np.float32)
    sin_b = np.zeros((TW + S_SMP, LANES), np.float32)
    for axis, ang in enumerate(angs):
        base = MLA_NOPE + axis * half
        c, s = np.cos(ang.astype(np.float64)), np.sin(ang.astype(np.float64))
        cos[TW:, base:base + nf] = c
        cos[TW:, base + nf:base + half] = c
        sin_a[TW:, base:base + nf] = -s
        sin_b[TW:, base + nf:base + half] = s
    return jnp.asarray(cos), jnp.asarray(sin_a), jnp.asarray(sin_b)


def _pad_last(a, width):
    return jnp.pad(a, [(0, 0)] * (a.ndim - 1) + [(0, width - a.shape[-1])])


def kernel(x_prompt, x_sample, cache_mla_ckv, cache_mla_krope, state_mlstm_C, state_mlstm_n, state_mlstm_m, c, c_ctx, norm1, norm2, w_ada, b_ada, w_in, b_mlstm_gates, mlstm_norm, mla_q_a_norm, mla_kv_a_norm, w_uq, w_ukv, mla_q_norm, mla_k_norm, sg_norm, w_spatial, b_spatial, w_branch, w_out, w_router, b_router, w_exp1, b_exp1, w_exp2, b_exp2):
    o = np.cumsum((0, ML_WIDTH, ML_WIDTH, ML_WIDTH, ML_WIDTH, 4 * ML_HEADS, MLA_Q_RANK, MLA_KV_RANK, MLA_ROPE,
                   SG_WIDTH, SG_WIDTH, N_BRANCH * D_MODEL))
    seg = lambda j: w_in[:, :, o[j]:o[j + 1]]
    w_in_r = jnp.concatenate(
        [seg(0), seg(1) * (ML_DIM ** -0.5), seg(2), seg(3), seg(5), seg(6), seg(7), seg(4),
         jnp.zeros((DEPTH, D_MODEL, ZS_W - (MLA_Q_RANK + MLA_KV_RANK + MLA_ROPE + 4 * ML_HEADS)), F32),
         seg(8), seg(9), seg(10)], axis=-1).astype(BF16)
    w_uq_r = _pad_last(w_uq.reshape(DEPTH, MLA_Q_RANK, MLA_HEADS, MLA_QK), HEAD_PAD).reshape(
        DEPTH, MLA_Q_RANK, MLA_HEADS * HEAD_PAD).astype(BF16)
    w_ukv4 = w_ukv.reshape(DEPTH, MLA_KV_RANK, MLA_HEADS, MLA_NOPE + MLA_V)
    w_k_r = _pad_last(w_ukv4[..., :MLA_NOPE], HEAD_PAD).reshape(DEPTH, MLA_KV_RANK, MLA_HEADS * HEAD_PAD).astype(BF16)
    w_v_r = w_ukv4[..., MLA_NOPE:].reshape(DEPTH, MLA_KV_RANK, MLA_HEADS * MLA_V).astype(BF16)
    q_norm_p = _pad_last(mla_q_norm, HEAD_PAD).reshape(DEPTH, 1, HEAD_PAD)
    k_norm_p = _pad_last(mla_k_norm, HEAD_PAD).reshape(DEPTH, 1, HEAD_PAD)
    b_gates_p = jnp.pad(b_mlstm_gates, ((0, 0), (GATE_LANE0, LANES - GATE_LANE0 - 4 * ML_HEADS))).reshape(DEPTH, 1, LANES)
    b_sp = _pad_last(jnp.swapaxes(b_spatial, 1, 2), LANES)
    w_router_p = _pad_last(w_router, LANES)
    b_router_p = jnp.pad(b_router, ((0, 0), (0, LANES - N_EXPERTS)), constant_values=-1e30).reshape(DEPTH, 1, LANES)
    r3 = lambda a: a.reshape(DEPTH, 1, a.shape[-1])
    cache_kr_pad = jnp.pad(cache_mla_krope, ((0, 0), (0, 0), (0, 0), (MLA_NOPE, LANES - MLA_QK)))
    rope_tabs = _rope_tables()

    cvec = jnp.concatenate([c_ctx[None, :], c, jnp.zeros((SUBLANES - 1 - N_SEQ_SMP, D_MODEL), F32)], axis=0)
    mod = _adaln(cvec, w_ada, b_ada).reshape(DEPTH, SUBLANES, 1, 6 * D_MODEL)
    b1 = b_exp1.reshape(DEPTH, N_EXPERTS, 1, 2 * D_EXPERT)
    b2 = b_exp2.reshape(DEPTH, N_EXPERTS, 1, D_MODEL)
    kk_cache, v_cache = _cache_kv(cache_mla_ckv, cache_kr_pad, w_k_r, w_v_r, k_norm_p)

    x = jnp.concatenate([x_prompt.reshape(N_CTX, D_MODEL), x_sample.reshape(N_SMP, D_MODEL)], axis=0)
    ckv_l, krope_l, c_l, n_l, m_l = [], [], [], [], []
    for l in range(DEPTH):
        za, zs, zc, zb = _inproj(l, x, r3(norm1), mod, w_in_r)
        oa_ctx, c_fin, n_fin, m_fin = _mlstm(l, za, zs, b_gates_p, r3(mlstm_norm))
        (oa_smp,) = _mlstm(l, za, zs, b_gates_p, r3(mlstm_norm), init=(state_mlstm_C, state_mlstm_n, state_mlstm_m))
        q, kk, v, ckvn = _mla_prep(l, zs, r3(mla_q_a_norm), r3(mla_kv_a_norm), w_uq_r, w_k_r, w_v_r, q_norm_p, k_norm_p, rope_tabs)
        ob_ctx = _attn_ctx(q, kk, v)
        ob_smp = _attn_smp(l, q, kk, v, kk_cache, v_cache)
        oa = jnp.concatenate([oa_ctx, oa_smp], axis=0)
        ob = jnp.concatenate([ob_ctx, ob_smp], axis=0)
        xmid, h2, dest, wsel, cnt = _merge(
            l, x, oa, ob, zc, zb, r3(sg_norm), w_spatial.astype(BF16), b_sp, w_branch.astype(BF16), w_out.astype(BF16),
            mod, r3(norm2), w_router_p, b_router_p)
        x = _moe(l, xmid, h2, dest, wsel, cnt, mod, w_exp1, b1, w_exp2, b2)
        ckv_l.append(ckvn[:N_CTX].reshape(N_SEQ_CTX, S_CTX, MLA_KV_RANK))
        krope_l.append(zs[:N_CTX, ZS_W - LANES:ZS_W - LANES + MLA_ROPE].reshape(N_SEQ_CTX, S_CTX, MLA_ROPE))
        c_l.append(c_fin)
        n_l.append(n_fin)
        m_l.append(m_fin[:, :, 0].reshape(N_SEQ_CTX, 2, ML_HEADS))
    return (
        x[:N_CTX].reshape(N_SEQ_CTX, S_CTX, D_MODEL),
        x[N_CTX:].reshape(N_SEQ_SMP, S_SMP, D_MODEL),
        jnp.stack(ckv_l, axis=1),
        jnp.stack(krope_l, axis=1),
        jnp.stack(c_l, axis=1),
        jnp.stack(n_l, axis=1),
        jnp.stack(m_l, axis=1),
    )
```

```python
import functools

import numpy as np
import jax
import jax.numpy as jnp
from jax import lax
from jax.experimental import pallas as pl
from jax.experimental.pallas import tpu as pltpu
from jax.experimental.pallas import tpu_sc as plsc

F32 = jnp.float32
BF16 = jnp.bfloat16
HI = lax.Precision.HIGHEST

D_MODEL = 1024
N_SEQ_CTX, S_CTX = 32, 256
N_SEQ_SMP, S_SMP = 2, 1024
DEPTH = 4
PAST_LEN = 512
GRID_W = 64
EPS = 1e-6
ML_HEADS, ML_DIM = 4, 128
ML_WIDTH = ML_HEADS * ML_DIM
MLA_HEADS, MLA_NOPE, MLA_ROPE, MLA_V = 8, 64, 32, 64
MLA_QK = MLA_NOPE + MLA_ROPE
MLA_Q_RANK, MLA_KV_RANK = 256, 128
ROPE_THETA = 10000.0
SG_GROUPS, SG_DIM, SG_CHUNK = 4, 128, 128
SG_WIDTH = SG_GROUPS * SG_DIM
N_BRANCH = 3
N_EXPERTS, TOP_K, D_EXPERT = 32, 4, 1024
SWIGLU_LIMIT, SWIGLU_ALPHA = 7.0, 1.702

N_CTX = N_SEQ_CTX * S_CTX
N_SMP = N_SEQ_SMP * S_SMP
N_TOK = N_CTX + N_SMP

LANES = 128
SUBLANES = 8
VMEM_LIMIT = 56 * 1024 * 1024

TW = 256
N_TILES = N_TOK // TW
N_TILES_CTX = N_CTX // TW
TILES_PER_SMP_SEQ = S_SMP // TW
HEAD_PAD = LANES
TQ = 256
EXPERT_CAP = N_TOK
SLOT_CHUNK = 256
N_CHUNK_STEPS = N_TOK * TOP_K // SLOT_CHUNK + N_EXPERTS
SC_ROW = 256
SC_SPLIT = D_MODEL // SC_ROW
SC_WIN = 128

ZA_W = 4 * ML_WIDTH
ZS_W = 512
ZC_W = 2 * SG_WIDTH
ZB_W = N_BRANCH * D_MODEL
ZIN_W = ZA_W + ZS_W + ZC_W + ZB_W
GATE_LANE0 = MLA_ROPE


def _cparams(sem):
    return pltpu.CompilerParams(dimension_semantics=sem, vmem_limit_bytes=VMEM_LIMIT)


def _mod_row(i):
    return jnp.where(i < N_TILES_CTX, 0, 1 + (i - N_TILES_CTX) // TILES_PER_SMP_SEQ)


def _rms(x, g, n=None):
    ms = jnp.sum(x * x, axis=-1, keepdims=True) * (1.0 / (n or x.shape[-1]))
    return x * lax.rsqrt(ms + EPS) * g


def _gelu(x):
    return 0.5 * x * (1.0 + jnp.tanh(0.7978845608028654 * (x + 0.044715 * (x * x * x))))


def _dot(a, b):
    return jnp.dot(a, b, preferred_element_type=F32)


def _dot_nt(a, b):
    return lax.dot_general(a, b, (((1,), (1,)), ((), ())), preferred_element_type=F32)


def _adaln_kernel(c_ref, w_ref, b_ref, o_ref):
    c = c_ref[...]
    s = c * jax.nn.sigmoid(c)
    o_ref[0] = jnp.dot(s, w_ref[0], precision=HI, preferred_element_type=F32) + b_ref[0]


def _adaln(cvec, w_ada, b_ada):
    nchunk = 4
    cw = 6 * D_MODEL // nchunk
    return pl.pallas_call(
        _adaln_kernel,
        grid=(DEPTH, nchunk),
        in_specs=[
            pl.BlockSpec((SUBLANES, D_MODEL), lambda l, j: (0, 0)),
            pl.BlockSpec((1, D_MODEL, cw), lambda l, j: (l, 0, j)),
            pl.BlockSpec((1, 1, cw), lambda l, j: (l, 0, j)),
        ],
        out_specs=pl.BlockSpec((1, SUBLANES, cw), lambda l, j: (l, 0, j)),
        out_shape=jax.ShapeDtypeStruct((DEPTH, SUBLANES, 6 * D_MODEL), F32),
        compiler_params=_cparams(("arbitrary", "arbitrary")),
        name="adaln",
    )(cvec, w_ada, b_ada.reshape(DEPTH, 1, 6 * D_MODEL))


def _inproj_kernel(x_ref, g_ref, sh_ref, sc_ref, w_ref, za_ref, zs_ref, zc_ref, zb_ref):
    h = _rms(x_ref[...], g_ref[0]) * (1.0 + sc_ref[0, 0]) + sh_ref[0, 0]
    hb = h.astype(BF16)
    za_ref[...] = _dot(hb, w_ref[0, :, 0:ZA_W]).astype(BF16)
    zs_ref[...] = _dot(hb, w_ref[0, :, ZA_W:ZA_W + ZS_W])
    zc_ref[...] = _dot(hb, w_ref[0, :, ZA_W + ZS_W:ZA_W + ZS_W + ZC_W]).astype(BF16)
    zb_ref[...] = _dot(hb, w_ref[0, :, ZA_W + ZS_W + ZC_W:ZIN_W]).astype(BF16)


def _mod_spec(layer, k):
    return pl.BlockSpec((1, 1, 1, D_MODEL), lambda i: (layer, _mod_row(i), 0, k))


def _inproj(layer, x, norm1, mod, w_in_r):
    tok = lambda w: pl.BlockSpec((TW, w), lambda i: (i, 0))
    return pl.pallas_call(
        _inproj_kernel,
        grid=(N_TILES,),
        in_specs=[
            tok(D_MODEL),
            pl.BlockSpec((1, 1, D_MODEL), lambda i: (layer, 0, 0)),
            _mod_spec(layer, 0),
            _mod_spec(layer, 1),
            pl.BlockSpec((1, D_MODEL, ZIN_W), lambda i: (layer, 0, 0)),
        ],
        out_specs=[tok(ZA_W), tok(ZS_W), tok(ZC_W), tok(ZB_W)],
        out_shape=[
            jax.ShapeDtypeStruct((N_TOK, ZA_W), BF16),
            jax.ShapeDtypeStruct((N_TOK, ZS_W), F32),
            jax.ShapeDtypeStruct((N_TOK, ZC_W), BF16),
            jax.ShapeDtypeStruct((N_TOK, ZB_W), BF16),
        ],
        compiler_params=_cparams(("arbitrary",)),
        name="inproj",
    )(x, norm1, mod, mod, w_in_r)


def _make_mlstm_kernel(seq, layer, has_init):
    nq = seq // TQ
    lane_if, lane_ff, lane_ib, lane_fb = (GATE_LANE0 + ML_HEADS * j for j in range(4))

    def kern(*refs):
        if has_init:
            m0_ref, zq, zk, zv, zo, gz, bg, nrm, c0_ref, n0_ref, _, out, bp_scr, bs_scr = refs
        else:
            zq, zk, zv, zo, gz, bg, nrm, out, cf_ref, nf_ref, mf_ref, bp_scr, bs_scr = refs
        b = pl.program_id(0)
        g = gz[...] + bg[0]
        lane = lax.broadcasted_iota(jnp.int32, g.shape, 1)
        is_forget = ((lane >= lane_ff) & (lane < lane_ib)) | ((lane >= lane_fb) & (lane < lane_fb + ML_HEADS))
        log_sig = jnp.minimum(g, 0.0) - jnp.log1p(jnp.exp(-jnp.abs(g)))
        a = jnp.where(is_forget, log_sig, g)
        r_i = lax.broadcasted_iota(jnp.int32, (seq, seq), 0)
        c_i = lax.broadcasted_iota(jnp.int32, (seq, seq), 1)
        ltri = (c_i <= r_i).astype(F32)
        bp = jnp.dot(ltri, a, precision=HI, preferred_element_type=F32)
        bs = bp[seq - 1:seq, :] - bp + a
        bp_scr[...] = bp
        bs_scr[...] = bs
        a_t, bp_t, bs_t = a.T, bp.T, bs.T

        for h in range(ML_HEADS):
            hs = slice(h * ML_DIM, (h + 1) * ML_DIM)
            k = zk[:, hs]
            v = zv[:, hs]
            rows = (
                a_t[lane_if + h:lane_if + h + 1, :] - bp_t[lane_ff + h:lane_ff + h + 1, :],
                a_t[lane_ib + h:lane_ib + h + 1, :] - bs_t[lane_fb + h:lane_fb + h + 1, :],
            )
            col_refs = ((bp_scr, lane_ff + h), (bs_scr, lane_fb + h))
            if has_init:
                m0 = tuple(m0_ref[((b * DEPTH + layer) * 2 + dr) * ML_HEADS + h] for dr in range(2))
                c0 = tuple(c0_ref[0, 0, dr, h].astype(BF16) for dr in range(2))
                n0 = tuple(n0_ref[0, 0, dr, h:h + 1, :] for dr in range(2))
            else:
                m0 = (0.0, 0.0)

            def qblock(qi, carry):
                q0 = pl.multiple_of(qi * TQ, TQ)
                qb = zq[pl.ds(q0, TQ), hs]
                sc = _dot_nt(qb, k)
                t_idx = q0 + lax.broadcasted_iota(jnp.int32, (TQ, seq), 0)
                s_idx = lax.broadcasted_iota(jnp.int32, (TQ, seq), 1)
                hsum = jnp.zeros((TQ, ML_DIM), F32)
                for dr in range(2):
                    cref, cl = col_refs[dr]
                    col = cref[pl.ds(q0, TQ), cl:cl + 1]
                    mask = (s_idx <= t_idx) if dr == 0 else (s_idx >= t_idx)
                    d = jnp.where(mask, col + rows[dr], -jnp.inf)
                    a0 = col + m0[dr]
                    m_t = jnp.maximum(a0, jnp.max(d, axis=1, keepdims=True))
                    s = sc * jnp.exp(d - m_t)
                    num = _dot(s.astype(BF16), v)
                    den = jnp.sum(s, axis=1, keepdims=True)
                    if has_init:
                        w_c = jnp.exp(a0 - m_t)
                        num = num + w_c * _dot(qb, c0[dr])
                        den = den + w_c * jnp.sum(qb.astype(F32) * n0[dr], axis=1, keepdims=True)
                    hsum = hsum + num / jnp.maximum(jnp.abs(den), jnp.exp(-m_t))
                hn = _rms(hsum, nrm[0][:, hs])
                og = zo[pl.ds(q0, TQ), hs].astype(F32)
                out[pl.ds(q0, TQ), hs] = (hn * jax.nn.sigmoid(og)).astype(out.dtype)
                return carry

            if nq == 1:
                qblock(0, 0)
            else:
                lax.fori_loop(0, nq, qblock, 0)

            if not has_init:
                k_t = k.astype(F32).T
                kf = k.astype(F32)
                tot = (bp_t[lane_ff + h:lane_ff + h + 1, seq - 1:seq], bp_t[lane_fb + h:lane_fb + h + 1, seq - 1:seq])
                gl = (
                    tot[0] + rows[0],
                    bp_t[lane_fb + h:lane_fb + h + 1, :] - a_t[lane_fb + h:lane_fb + h + 1, :]
                    + a_t[lane_ib + h:lane_ib + h + 1, :],
                )
                for dr in range(2):
                    m_new = jnp.maximum(tot[dr] + m0[dr], jnp.max(gl[dr], axis=1, keepdims=True))
                    w_s = jnp.exp(gl[dr] - m_new)
                    cf_ref[0, dr, h] = _dot((k_t * w_s).astype(BF16), v)
                    n_new = jnp.dot(jnp.broadcast_to(w_s, (SUBLANES, seq)), kf, precision=HI, preferred_element_type=F32)
                    nf_ref[0, dr, h:h + 1, :] = n_new[0:1, :]
                    mf_ref[0, dr * ML_HEADS + h:dr * ML_HEADS + h + 1, :] = jnp.broadcast_to(m_new, (1, LANES))

    return kern


def _mlstm(layer, za, zs, b_gates, mlstm_norm, init=None, ctx_out=None):
    has_init = init is not None
    seq, nseq, row0 = (S_SMP, N_SEQ_SMP, N_CTX // S_SMP) if has_init else (S_CTX, N_SEQ_CTX, 0)
    qkvo = [pl.BlockSpec((seq, ML_WIDTH), functools.partial(lambda j, b: (row0 + b, j), j)) for j in range(4)]
    in_specs = qkvo + [
        pl.BlockSpec((seq, LANES), lambda b: (row0 + b, ZS_W // LANES - 1)),
        pl.BlockSpec((1, 1, LANES), lambda b: (layer, 0, 0)),
        pl.BlockSpec((1, 1, ML_WIDTH), lambda b: (layer, 0, 0)),
    ]
    args = [za, za, za, za, zs, b_gates, mlstm_norm]
    out_specs = [pl.BlockSpec((seq, ML_WIDTH), lambda b: (row0 + b, 0))]
    out_shape = [jax.ShapeDtypeStruct((N_TOK, ML_WIDTH), BF16)]
    aliases = {}
    if has_init:
        st_c, st_n, st_m = init
        in_specs = [pl.BlockSpec(memory_space=pltpu.SMEM)] + in_specs + [
            pl.BlockSpec((1, 1, 2, ML_HEADS, ML_DIM, ML_DIM), lambda b: (b, layer, 0, 0, 0, 0)),
            pl.BlockSpec((1, 1, 2, ML_HEADS, ML_DIM), lambda b: (b, layer, 0, 0, 0)),
            pl.BlockSpec(memory_space=pl.ANY),
        ]
        args = [st_m.reshape(-1)] + args + [st_c, st_n, ctx_out]
        aliases = {len(args) - 1: 0}
    else:
        out_specs += [
            pl.BlockSpec((1, 2, ML_HEADS, ML_DIM, ML_DIM), lambda b: (b, 0, 0, 0, 0)),
            pl.BlockSpec((1, 2, ML_HEADS, ML_DIM), lambda b: (b, 0, 0, 0)),
            pl.BlockSpec((1, 2 * ML_HEADS, LANES), lambda b: (b, 0, 0)),
        ]
        out_shape += [
            jax.ShapeDtypeStruct((nseq, 2, ML_HEADS, ML_DIM, ML_DIM), F32),
            jax.ShapeDtypeStruct((nseq, 2, ML_HEADS, ML_DIM), F32),
            jax.ShapeDtypeStruct((nseq, 2 * ML_HEADS, LANES), F32),
        ]
    return pl.pallas_call(
        _make_mlstm_kernel(seq, layer, has_init),
        grid=(nseq,),
        in_specs=in_specs,
        out_specs=out_specs,
        out_shape=out_shape,
        scratch_shapes=[pltpu.VMEM((seq, LANES), F32), pltpu.VMEM((seq, LANES), F32)],
        input_output_aliases=aliases,
        compiler_params=_cparams(("arbitrary",)),
        name="mlstm_smp" if has_init else "mlstm_ctx",
    )(*args)


def _rope(x, cos, sin_a, sin_b):
    return x * cos + pltpu.roll(x, LANES - 8, 1) * sin_a + pltpu.roll(x, 8, 1) * sin_b


def _mla_prep_kernel(zs_ref, qa_ref, kva_ref, wuq_ref, wk_ref, wv_ref, qn_ref, kn_ref, cos_ref, sa_ref, sb_ref,
                     q_ref, kk_ref, v_ref, ckv_ref):
    cq = zs_ref[:, 0:MLA_Q_RANK]
    ckv = zs_ref[:, MLA_Q_RANK:MLA_Q_RANK + MLA_KV_RANK]
    last = zs_ref[:, ZS_W - LANES:ZS_W]
    qf = _dot(_rms(cq, qa_ref[0]).astype(BF16), wuq_ref[0])
    ckvn = _rms(ckv, kva_ref[0])
    ckv_ref[...] = ckvn
    cb = ckvn.astype(BF16)
    kf = _dot(cb, wk_ref[0])
    v_ref[...] = _dot(cb, wv_ref[0]).astype(BF16)
    lane = lax.broadcasted_iota(jnp.int32, last.shape, 1)
    kr = jnp.where((lane >= MLA_NOPE) & (lane < MLA_QK), pltpu.roll(last, MLA_NOPE, 1), 0.0)
    cos, sa, sb = cos_ref[...], sa_ref[...], sb_ref[...]
    for h in range(MLA_HEADS):
        hs = slice(h * HEAD_PAD, (h + 1) * HEAD_PAD)
        q_ref[:, hs] = _rope(_rms(qf[:, hs], qn_ref[0], n=MLA_QK), cos, sa, sb).astype(BF16)
        kk_ref[:, hs] = _rope(_rms(kf[:, hs] + kr, kn_ref[0], n=MLA_QK), cos, sa, sb).astype(BF16)


def _mla_prep(layer, zs, q_a_norm, kv_a_norm, w_uq_r, w_k_r, w_v_r, q_norm_p, k_norm_p, rope_tabs):
    lw = lambda shape: pl.BlockSpec((1,) + shape, lambda i: (layer,) + (0,) * len(shape))
    tab = pl.BlockSpec((TW, LANES), lambda i: (jnp.where(i < N_TILES_CTX, 0, 1 + (i - N_TILES_CTX) % TILES_PER_SMP_SEQ), 0))
    tok = lambda w: pl.BlockSpec((TW, w), lambda i: (i, 0))
    return pl.pallas_call(
        _mla_prep_kernel,
        grid=(N_TILES,),
        in_specs=[
            tok(ZS_W), lw((1, MLA_Q_RANK)), lw((1, MLA_KV_RANK)),
            lw((MLA_Q_RANK, MLA_HEADS * HEAD_PAD)), lw((MLA_KV_RANK, MLA_HEADS * HEAD_PAD)),
            lw((MLA_KV_RANK, MLA_HEADS * MLA_V)), lw((1, HEAD_PAD)), lw((1, HEAD_PAD)), tab, tab, tab,
        ],
        out_specs=[tok(MLA_HEADS * HEAD_PAD), tok(MLA_HEADS * HEAD_PAD), tok(MLA_HEADS * MLA_V), tok(MLA_KV_RANK)],
        out_shape=[
            jax.ShapeDtypeStruct((N_TOK, MLA_HEADS * HEAD_PAD), BF16),
            jax.ShapeDtypeStruct((N_TOK, MLA_HEADS * HEAD_PAD), BF16),
            jax.ShapeDtypeStruct((N_TOK, MLA_HEADS * MLA_V), BF16),
            jax.ShapeDtypeStruct((N_TOK, MLA_KV_RANK), F32),
        ],
        compiler_params=_cparams(("arbitrary",)),
        name="mla_prep",
    )(zs, q_a_norm, kv_a_norm, w_uq_r, w_k_r, w_v_r, q_norm_p, k_norm_p, *rope_tabs)


def _cache_kv_kernel(ckv_ref, kr_ref, wk_ref, wv_ref, kn_ref, kk_ref, v_ref):
    cb = ckv_ref[...].astype(BF16)
    kf = _dot(cb, wk_ref[0])
    v_ref[...] = _dot(cb, wv_ref[0]).astype(BF16)
    kr = kr_ref[...]
    for h in range(MLA_HEADS):
        hs = slice(h * HEAD_PAD, (h + 1) * HEAD_PAD)
        kk_ref[:, hs] = _rms(kf[:, hs] + kr, kn_ref[0], n=MLA_QK).astype(BF16)


def _cache_kv(cache_ckv, cache_kr_pad, w_k_r, w_v_r, k_norm_p):
    lw = lambda shape: pl.BlockSpec((1,) + shape, lambda b, l: (l,) + (0,) * len(shape))
    blk = lambda w: pl.BlockSpec((None, None, PAST_LEN, w), lambda b, l: (b, l, 0, 0))
    return pl.pallas_call(
        _cache_kv_kernel,
        grid=(N_SEQ_SMP, DEPTH),
        in_specs=[blk(MLA_KV_RANK), blk(LANES), lw((MLA_KV_RANK, MLA_HEADS * HEAD_PAD)),
                  lw((MLA_KV_RANK, MLA_HEADS * MLA_V)), lw((1, HEAD_PAD))],
        out_specs=[blk(MLA_HEADS * HEAD_PAD), blk(MLA_HEADS * MLA_V)],
        out_shape=[
            jax.ShapeDtypeStruct((N_SEQ_SMP, DEPTH, PAST_LEN, MLA_HEADS * HEAD_PAD), BF16),
            jax.ShapeDtypeStruct((N_SEQ_SMP, DEPTH, PAST_LEN, MLA_HEADS * MLA_V), BF16),
        ],
        compiler_params=_cparams(("arbitrary", "arbitrary")),
        name="cache_kv",
    )(cache_ckv, cache_kr_pad, w_k_r, w_v_r, k_norm_p)


def _make_attn_kernel(n_src):
    scale = MLA_QK ** -0.5

    def kern(q_ref, *refs):
        o_ref = refs[-1]
        for h in range(MLA_HEADS):
            hs = slice(h * HEAD_PAD, (h + 1) * HEAD_PAD)
            vs = slice(h * MLA_V, (h + 1) * MLA_V)
            q = q_ref[:, hs]
            ss = [_dot_nt(q, refs[2 * j][:, hs]) * scale for j in range(n_src)]
            m = functools.reduce(jnp.maximum, [jnp.max(s, axis=1, keepdims=True) for s in ss])
            ps = [jnp.exp(s - m) for s in ss]
            l = functools.reduce(jnp.add, [jnp.sum(p, axis=1, keepdims=True) for p in ps])
            o = functools.reduce(jnp.add, [_dot(ps[j].astype(BF16), refs[2 * j + 1][:, vs]) for j in range(n_src)])
            o_ref[:, vs] = (o / l).astype(o_ref.dtype)

    return kern


def _attn_ctx(q, kk, v):
    blk = lambda w: pl.BlockSpec((S_CTX, w), lambda b: (b, 0))
    return pl.pallas_call(
        _make_attn_kernel(1),
        grid=(N_SEQ_CTX,),
        in_specs=[blk(MLA_HEADS * HEAD_PAD), blk(MLA_HEADS * HEAD_PAD), blk(MLA_HEADS * MLA_V)],
        out_specs=blk(MLA_HEADS * MLA_V),
        out_shape=jax.ShapeDtypeStruct((N_TOK, MLA_HEADS * MLA_V), BF16),
        compiler_params=_cparams(("arbitrary",)),
        name="attn_ctx",
    )(q, kk, v)


def _attn_smp(layer, q, kk, v, kk_cache, v_cache, ctx_out):
    row0 = N_CTX // S_SMP
    nq = S_SMP // TQ
    seqb = lambda w: pl.BlockSpec((S_SMP, w), lambda b, i: (row0 + b, 0))
    cache = lambda w: pl.BlockSpec((None, None, PAST_LEN, w), lambda b, i: (b, layer, 0, 0))
    return pl.pallas_call(
        _make_attn_kernel(2),
        grid=(N_SEQ_SMP, nq),
        in_specs=[
            pl.BlockSpec((TQ, MLA_HEADS * HEAD_PAD), lambda b, i: (N_CTX // TQ + b * nq + i, 0)),
            seqb(MLA_HEADS * HEAD_PAD), seqb(MLA_HEADS * MLA_V),
            cache(MLA_HEADS * HEAD_PAD), cache(MLA_HEADS * MLA_V),
            pl.BlockSpec(memory_space=pl.ANY),
        ],
        out_specs=pl.BlockSpec((TQ, MLA_HEADS * MLA_V), lambda b, i: (N_CTX // TQ + b * nq + i, 0)),
        out_shape=jax.ShapeDtypeStruct((N_TOK, MLA_HEADS * MLA_V), BF16),
        input_output_aliases={5: 0},
        compiler_params=_cparams(("arbitrary", "arbitrary")),
        name="attn_smp",
    )(q, kk, v, kk_cache, v_cache, ctx_out)


def _merge_kernel(x_ref, oa_ref, ob_ref, zc_ref, zb_ref, sgn_ref, ws_ref, bs_ref, wb_ref, wo_ref, g1_ref, n2_ref,
                  sh2_ref, sc2_ref, wr_ref, br_ref,
                  xmid_ref, h2_ref, dest_ref, wsel_ref, cnt_ref, oc_scr, carry_scr):
    i = pl.program_id(0)

    @pl.when(i == 0)
    def _():
        carry_scr[...] = jnp.zeros_like(carry_scr)

    u = _gelu(zc_ref[:, 0:SG_WIDTH].astype(F32))
    vg = _gelu(zc_ref[:, SG_WIDTH:2 * SG_WIDTH].astype(F32))
    for g in range(SG_GROUPS):
        gs = slice(g * SG_DIM, (g + 1) * SG_DIM)
        vn = _rms(vg[:, gs], sgn_ref[0][:, gs]).astype(BF16)
        for c in range(TW // SG_CHUNK):
            cs = slice(c * SG_CHUNK, (c + 1) * SG_CHUNK)
            mixed = _dot(ws_ref[0, g], vn[cs, :]) + bs_ref[0][:, g:g + 1]
            oc_scr[cs, gs] = (u[cs, gs] * mixed).astype(BF16)

    acc = jnp.zeros((TW, D_MODEL), F32)
    for j, src in enumerate((oa_ref, ob_ref, oc_scr)):
        gate = jax.nn.sigmoid(zb_ref[:, j * D_MODEL:(j + 1) * D_MODEL].astype(F32))
        acc = acc + gate * _dot(src[...], wb_ref[0, j])
    xm = x_ref[...] + g1_ref[0, 0] * _dot(acc.astype(BF16), wo_ref[0])
    xmid_ref[...] = xm
    h2 = _rms(xm, n2_ref[0]) * (1.0 + sc2_ref[0, 0]) + sh2_ref[0, 0]
    for c in range(SC_SPLIT):
        h2_ref[c] = h2[:, c * SC_ROW:(c + 1) * SC_ROW]

    logits = jnp.dot(h2, wr_ref[0], precision=HI, preferred_element_type=F32) + br_ref[0]
    lane = lax.broadcasted_iota(jnp.int32, logits.shape, 1)
    hits, exps = [], []
    sel = jnp.zeros(logits.shape, F32)
    denom = jnp.zeros((TW, 1), F32)
    top = None
    for _ in range(TOP_K):
        m = jnp.max(logits, axis=1, keepdims=True)
        idx = jnp.min(jnp.where(logits == m, lane, LANES), axis=1, keepdims=True)
        hit = lane == idx
        top = m if top is None else top
        hits.append(hit)
        exps.append(jnp.exp(m - top))
        sel = jnp.where(hit, 1.0, sel)
        denom = denom + exps[-1]
        logits = jnp.where(hit, -jnp.inf, logits)

    r_i = lax.broadcasted_iota(jnp.int32, (TW, TW), 0)
    c_i = lax.broadcasted_iota(jnp.int32, (TW, TW), 1)
    carry = carry_scr[0:1, :]
    rank = _dot(jnp.where(c_i < r_i, 1.0, 0.0).astype(BF16), sel.astype(BF16)) + carry
    new_carry = carry + jnp.sum(sel, axis=0, keepdims=True)
    carry_scr[...] = jnp.broadcast_to(new_carry, (SUBLANES, LANES))
    cnt_ref[...] = jnp.broadcast_to(new_carry, (SUBLANES, LANES))
    slot = rank + lane.astype(F32) * float(EXPERT_CAP)
    dmat = jnp.zeros(logits.shape, F32)
    wmat = jnp.zeros(logits.shape, F32)
    for k in range(TOP_K):
        dk = jnp.sum(jnp.where(hits[k], slot, 0.0), axis=1, keepdims=True)
        dmat = jnp.where(lane == k, dk, dmat)
        wmat = jnp.where(lane == k, exps[k] / denom, wmat)
    dest_ref[...] = dmat.T[0:SUBLANES, :].astype(jnp.int32)
    wsel_ref[...] = wmat


def _merge(layer, x, oa, ob, zc, zb, sg_norm, w_sp, b_sp, w_branch, w_out, mod, norm2, w_router_p, b_router_p):
    lw = lambda shape: pl.BlockSpec((1,) + shape, lambda i: (layer,) + (0,) * len(shape))
    tok = lambda w: pl.BlockSpec((TW, w), lambda i: (i, 0))
    return pl.pallas_call(
        _merge_kernel,
        grid=(N_TILES,),
        in_specs=[
            tok(D_MODEL), tok(ML_WIDTH), tok(MLA_HEADS * MLA_V), tok(ZC_W), tok(ZB_W),
            lw((1, SG_WIDTH)), lw((SG_GROUPS, SG_CHUNK, SG_CHUNK)), lw((SG_CHUNK, LANES)),
            lw((N_BRANCH, ML_WIDTH, D_MODEL)), lw((D_MODEL, D_MODEL)),
            _mod_spec(layer, 2), lw((1, D_MODEL)), _mod_spec(layer, 3), _mod_spec(layer, 4),
            lw((D_MODEL, LANES)), lw((1, LANES)),
        ],
        out_specs=[tok(D_MODEL), pl.BlockSpec((SC_SPLIT, TW, SC_ROW), lambda i: (0, i, 0)),
                   pl.BlockSpec((SUBLANES, TW), lambda i: (0, i)), tok(LANES),
                   pl.BlockSpec((SUBLANES, LANES), lambda i: (0, 0))],
        out_shape=[
            jax.ShapeDtypeStruct((N_TOK, D_MODEL), F32),
            jax.ShapeDtypeStruct((SC_SPLIT, N_TOK, SC_ROW), F32),
            jax.ShapeDtypeStruct((SUBLANES, N_TOK), jnp.int32),
            jax.ShapeDtypeStruct((N_TOK, LANES), F32),
            jax.ShapeDtypeStruct((SUBLANES, LANES), F32),
        ],
        scratch_shapes=[pltpu.VMEM((TW, SG_WIDTH), BF16), pltpu.VMEM((SUBLANES, LANES), F32)],
        compiler_params=_cparams(("arbitrary",)),
        name="merge_router",
    )(x, oa, ob, zc, zb, sg_norm, w_sp, b_sp, w_branch, w_out, mod, norm2, mod, mod, w_router_p, b_router_p)


def _sc_mesh():
    return plsc.VectorSubcoreMesh(core_axis_name="core", subcore_axis_name="subcore")


def _sc_scatter_rows(x, idx, n_rows):
    m = idx.shape[1]
    n_src_blocks = x.shape[0] // SC_WIN

    @pl.kernel(out_type=jax.ShapeDtypeStruct((n_rows, SC_ROW), x.dtype), mesh=_sc_mesh(), scratch_types=[])
    def scatter(x_hbm, i_hbm, o_hbm):
        def body(x_vmem, i_vmem):
            pltpu.sync_copy(x_vmem, o_hbm.at[i_vmem.at[0]])

        pltpu.emit_pipeline(
            body,
            grid=(m // SC_WIN,),
            in_specs=[pl.BlockSpec((SC_WIN, SC_ROW), lambda i: (i % n_src_blocks, 0)),
                      pl.BlockSpec((1, SC_WIN), lambda i: (0, i))],
            out_specs=[],
            core_axis_name=("core", "subcore"),
            dimension_semantics=(pltpu.PARALLEL,),
        )(x_hbm, i_hbm)

    return scatter(x, idx)


def _sc_gather_rows(x, idx):
    m = idx.shape[1]

    @pl.kernel(out_type=jax.ShapeDtypeStruct((m, SC_ROW), x.dtype), mesh=_sc_mesh())
    def gather(x_hbm, i_hbm, o_hbm):
        def body(i_vmem, o_vmem):
            pltpu.sync_copy(x_hbm.at[i_vmem.at[0]], o_vmem)

        pltpu.emit_pipeline(
            body,
            grid=(m // SC_WIN,),
            in_specs=[pl.BlockSpec((1, SC_WIN), lambda i: (0, i))],
            out_specs=[pl.BlockSpec((SC_WIN, SC_ROW), lambda i: (i, 0))],
            core_axis_name=("core", "subcore"),
            dimension_semantics=(pltpu.PARALLEL,),
        )(i_hbm, o_hbm)

    return gather(x, idx)


def _moe_ffn_kernel(be_ref, br_ref, fl_ref, xs_ref, w1_ref, b1_ref, w2_ref, b2_ref, y_ref, w1b, w2b):
    flags = fl_ref[pl.program_id(0)]

    @pl.when((flags & 2) != 0)
    def _():
        w1b[...] = w1_ref[0, 0].astype(BF16)
        w2b[...] = w2_ref[0, 0].astype(BF16)

    @pl.when((flags & 1) != 0)
    def _():
        g = b1_ref[0, 0]
        for c in range(SC_SPLIT):
            g = g + _dot(xs_ref[c].astype(BF16), w1b[c * SC_ROW:(c + 1) * SC_ROW, :])
        gate = jnp.minimum(g[:, :D_EXPERT], SWIGLU_LIMIT)
        up = jnp.clip(g[:, D_EXPERT:], -SWIGLU_LIMIT, SWIGLU_LIMIT)
        act = gate * jax.nn.sigmoid(SWIGLU_ALPHA * gate) * (up + 1.0)
        y = _dot(act.astype(BF16), w2b[...]) + b2_ref[0, 0]
        for c in range(SC_SPLIT):
            y_ref[c] = y[:, c * SC_ROW:(c + 1) * SC_ROW]


def _moe_ffn(layer, xs, blk_e, blk_row, flags, w1, b1, w2, b2):
    ew = lambda r, c: pl.BlockSpec((1, 1, r, c), lambda g, be, br, fl: (layer, be[g], 0, 0))
    rows = pl.BlockSpec((SC_SPLIT, SLOT_CHUNK, SC_ROW), lambda g, be, br, fl: (0, br[g], 0))
    grid_spec = pltpu.PrefetchScalarGridSpec(
        num_scalar_prefetch=3,
        grid=(N_CHUNK_STEPS,),
        in_specs=[rows, ew(D_MODEL, 2 * D_EXPERT), ew(1, 2 * D_EXPERT), ew(D_EXPERT, D_MODEL), ew(1, D_MODEL)],
        out_specs=rows,
        scratch_shapes=[pltpu.VMEM((D_MODEL, 2 * D_EXPERT), BF16), pltpu.VMEM((D_EXPERT, D_MODEL), BF16)],
    )
    return pl.pallas_call(
        _moe_ffn_kernel,
        grid_spec=grid_spec,
        out_shape=jax.ShapeDtypeStruct(xs.shape, F32),
        compiler_params=_cparams(("arbitrary",)),
        name="moe_ffn",
    )(blk_e, blk_row, flags, xs, w1, b1, w2, b2)


def _chunk_plan(cnt):
    nch = (cnt + SLOT_CHUNK - 1) // SLOT_CHUNK
    cum = jnp.cumsum(nch)
    steps = jnp.arange(N_CHUNK_STEPS, dtype=jnp.int32)
    g = jnp.minimum(steps, cum[-1] - 1)
    e = jnp.minimum(jnp.sum((cum[None, :] <= g[:, None]).astype(jnp.int32), axis=1), N_EXPERTS - 1)
    is_e = jnp.arange(N_EXPERTS, dtype=jnp.int32)[None, :] == e[:, None]
    j = g - jnp.sum(jnp.where(is_e, (cum - nch)[None, :], 0), axis=1)
    valid = steps < cum[-1]
    flags = valid.astype(jnp.int32) + 2 * (valid & (j == 0)).astype(jnp.int32)
    return e, (e * (EXPERT_CAP // SLOT_CHUNK) + j).astype(jnp.int32), flags


def _combine_kernel(x_ref, yg_ref, w_ref, g_ref, o_ref):
    w = w_ref[...]
    for c in range(SC_SPLIT):
        cs = slice(c * SC_ROW, (c + 1) * SC_ROW)
        acc = w[:, 0:1] * yg_ref[0, c]
        for k in range(1, TOP_K):
            acc = acc + w[:, k:k + 1] * yg_ref[k, c]
        o_ref[:, cs] = x_ref[:, cs] + g_ref[0, 0][:, cs] * acc


def _combine(layer, xmid, yg, wsel, mod):
    tok = lambda w: pl.BlockSpec((TW, w), lambda i: (i, 0))
    return pl.pallas_call(
        _combine_kernel,
        grid=(N_TILES,),
        in_specs=[tok(D_MODEL), pl.BlockSpec((TOP_K, SC_SPLIT, TW, SC_ROW), lambda i: (0, 0, i, 0)), tok(LANES),
                  _mod_spec(layer, 5)],
        out_specs=tok(D_MODEL),
        out_shape=jax.ShapeDtypeStruct((N_TOK, D_MODEL), F32),
        compiler_params=_cparams(("arbitrary",)),
        name="combine",
    )(xmid, yg, wsel, mod)


def _moe(layer, xmid, h2, dest, wsel, cnt, mod, w1, b1, w2, b2):
    n_slots = N_EXPERTS * EXPERT_CAP
    idx = dest[0:TOP_K][:, None, :] + (jnp.arange(SC_SPLIT, dtype=jnp.int32) * n_slots)[None, :, None]
    idx = idx.reshape(1, TOP_K * SC_SPLIT * N_TOK)
    xs = _sc_scatter_rows(h2.reshape(SC_SPLIT * N_TOK, SC_ROW), idx, SC_SPLIT * n_slots)
    blk_e, blk_row, flags = _chunk_plan(cnt[0, :N_EXPERTS].astype(jnp.int32))
    y = _moe_ffn(layer, xs.reshape(SC_SPLIT, n_slots, SC_ROW), blk_e, blk_row, flags, w1, b1, w2, b2)
    yg = _sc_gather_rows(y.reshape(SC_SPLIT * n_slots, SC_ROW), idx)
    return _combine(layer, xmid, yg.reshape(TOP_K, SC_SPLIT, N_TOK, SC_ROW), wsel, mod)


def _rope_tables():
    pos = np.arange(S_SMP)
    half = MLA_ROPE // 2
    inv_freq = (ROPE_THETA ** (-(np.arange(0, half, 2, dtype=np.float32) / np.float32(half)))).astype(np.float32)
    angs = [((pos // GRID_W).astype(np.float32)[:, None] * inv_freq[None, :]).astype(np.float32),
            ((pos % GRID_W).astype(np.float32)[:, None] * inv_freq[None, :]).astype(np.float32)]
    nf = half // 2
    cos = np.ones((TW + S_SMP, LANES), np.float32)
    sin_a = np.zeros((TW + S_SMP, LANES), np.float32)
    sin_b = np.zeros((TW + S_SMP, LANES), np.float32)
    for axis, ang in enumerate(angs):
        base = MLA_NOPE + axis * half
        c, s = np.cos(ang.astype(np.float64)), np.sin(ang.astype(np.float64))
        cos[TW:, base:base + nf] = c
        cos[TW:, base + nf:base + half] = c
        sin_a[TW:, base:base + nf] = -s
        sin_b[TW:, base + nf:base + half] = s
    return jnp.asarray(cos), jnp.asarray(sin_a), jnp.asarray(sin_b)


def _pad_last(a, width):
    return jnp.pad(a, [(0, 0)] * (a.ndim - 1) + [(0, width - a.shape[-1])])


def kernel(x_prompt, x_sample, cache_mla_ckv, cache_mla_krope, state_mlstm_C, state_mlstm_n, state_mlstm_m, c, c_ctx, norm1, norm2, w_ada, b_ada, w_in, b_mlstm_gates, mlstm_norm, mla_q_a_norm, mla_kv_a_norm, w_uq, w_ukv, mla_q_norm, mla_k_norm, sg_norm, w_spatial, b_spatial, w_branch, w_out, w_router, b_router, w_exp1, b_exp1, w_exp2, b_exp2):
    o = np.cumsum((0, ML_WIDTH, ML_WIDTH, ML_WIDTH, ML_WIDTH, 4 * ML_HEADS, MLA_Q_RANK, MLA_KV_RANK, MLA_ROPE,
                   SG_WIDTH, SG_WIDTH, N_BRANCH * D_MODEL))
    seg = lambda j: w_in[:, :, o[j]:o[j + 1]]
    w_in_r = jnp.concatenate(
        [seg(0), seg(1) * (ML_DIM ** -0.5), seg(2), seg(3), seg(5), seg(6), seg(7), seg(4),
         jnp.zeros((DEPTH, D_MODEL, ZS_W - (MLA_Q_RANK + MLA_KV_RANK + MLA_ROPE + 4 * ML_HEADS)), F32),
         seg(8), seg(9), seg(10)], axis=-1).astype(BF16)
    w_uq_r = _pad_last(w_uq.reshape(DEPTH, MLA_Q_RANK, MLA_HEADS, MLA_QK), HEAD_PAD).reshape(
        DEPTH, MLA_Q_RANK, MLA_HEADS * HEAD_PAD).astype(BF16)
    w_ukv4 = w_ukv.reshape(DEPTH, MLA_KV_RANK, MLA_HEADS, MLA_NOPE + MLA_V)
    w_k_r = _pad_last(w_ukv4[..., :MLA_NOPE], HEAD_PAD).reshape(DEPTH, MLA_KV_RANK, MLA_HEADS * HEAD_PAD).astype(BF16)
    w_v_r = w_ukv4[..., MLA_NOPE:].reshape(DEPTH, MLA_KV_RANK, MLA_HEADS * MLA_V).astype(BF16)
    q_norm_p = _pad_last(mla_q_norm, HEAD_PAD).reshape(DEPTH, 1, HEAD_PAD)
    k_norm_p = _pad_last(mla_k_norm, HEAD_PAD).reshape(DEPTH, 1, HEAD_PAD)
    b_gates_p = jnp.pad(b_mlstm_gates, ((0, 0), (GATE_LANE0, LANES - GATE_LANE0 - 4 * ML_HEADS))).reshape(DEPTH, 1, LANES)
    b_sp = _pad_last(jnp.swapaxes(b_spatial, 1, 2), LANES)
    w_router_p = _pad_last(w_router, LANES)
    b_router_p = jnp.pad(b_router, ((0, 0), (0, LANES - N_EXPERTS)), constant_values=-1e30).reshape(DEPTH, 1, LANES)
    r3 = lambda a: a.reshape(DEPTH, 1, a.shape[-1])
    cache_kr_pad = jnp.pad(cache_mla_krope, ((0, 0), (0, 0), (0, 0), (MLA_NOPE, LANES - MLA_QK)))
    rope_tabs = _rope_tables()

    cvec = jnp.concatenate([c_ctx[None, :], c, jnp.zeros((SUBLANES - 1 - N_SEQ_SMP, D_MODEL), F32)], axis=0)
    mod = _adaln(cvec, w_ada, b_ada).reshape(DEPTH, SUBLANES, 1, 6 * D_MODEL)
    b1 = b_exp1.reshape(DEPTH, N_EXPERTS, 1, 2 * D_EXPERT)
    b2 = b_exp2.reshape(DEPTH, N_EXPERTS, 1, D_MODEL)
    kk_cache, v_cache = _cache_kv(cache_mla_ckv, cache_kr_pad, w_k_r, w_v_r, k_norm_p)

    x = jnp.concatenate([x_prompt.reshape(N_CTX, D_MODEL), x_sample.reshape(N_SMP, D_MODEL)], axis=0)
    ckv_l, krope_l, c_l, n_l, m_l = [], [], [], [], []
    for l in range(DEPTH):
        za, zs, zc, zb = _inproj(l, x, r3(norm1), mod, w_in_r)
        oa, c_fin, n_fin, m_fin = _mlstm(l, za, zs, b_gates_p, r3(mlstm_norm))
        (oa,) = _mlstm(l, za, zs, b_gates_p, r3(mlstm_norm), init=(state_mlstm_C, state_mlstm_n, state_mlstm_m), ctx_out=oa)
        q, kk, v, ckvn = _mla_prep(l, zs, r3(mla_q_a_norm), r3(mla_kv_a_norm), w_uq_r, w_k_r, w_v_r, q_norm_p, k_norm_p, rope_tabs)
        ob = _attn_smp(l, q, kk, v, kk_cache, v_cache, _attn_ctx(q, kk, v))
        xmid, h2, dest, wsel, cnt = _merge(
            l, x, oa, ob, zc, zb, r3(sg_norm), w_spatial.astype(BF16), b_sp, w_branch.astype(BF16), w_out.astype(BF16),
            mod, r3(norm2), w_router_p, b_router_p)
        x = _moe(l, xmid, h2, dest, wsel, cnt, mod, w_exp1, b1, w_exp2, b2)
        ckv_l.append(ckvn[:N_CTX].reshape(N_SEQ_CTX, S_CTX, MLA_KV_RANK))
        krope_l.append(zs[:N_CTX, ZS_W - LANES:ZS_W - LANES + MLA_ROPE].reshape(N_SEQ_CTX, S_CTX, MLA_ROPE))
        c_l.append(c_fin)
        n_l.append(n_fin)
        m_l.append(m_fin[:, :, 0].reshape(N_SEQ_CTX, 2, ML_HEADS))
    return (
        x[:N_CTX].reshape(N_SEQ_CTX, S_CTX, D_MODEL),
        x[N_CTX:].reshape(N_SEQ_SMP, S_SMP, D_MODEL),
        jnp.stack(ckv_l, axis=1),
        jnp.stack(krope_l, axis=1),
        jnp.stack(c_l, axis=1),
        jnp.stack(n_l, axis=1),
        jnp.stack(m_l, axis=1),
    )
```

```python
import functools

import numpy as np
import jax
import jax.numpy as jnp
from jax import lax
from jax.experimental import pallas as pl
from jax.experimental.pallas import tpu as pltpu
from jax.experimental.pallas import tpu_sc as plsc

F32 = jnp.float32
BF16 = jnp.bfloat16
HI = lax.Precision.HIGHEST

D_MODEL = 1024
N_SEQ_CTX, S_CTX = 32, 256
N_SEQ_SMP, S_SMP = 2, 1024
DEPTH = 4
PAST_LEN = 512
GRID_W = 64
EPS = 1e-6
ML_HEADS, ML_DIM = 4, 128
ML_WIDTH = ML_HEADS * ML_DIM
MLA_HEADS, MLA_NOPE, MLA_ROPE, MLA_V = 8, 64, 32, 64
MLA_QK = MLA_NOPE + MLA_ROPE
MLA_Q_RANK, MLA_KV_RANK = 256, 128
ROPE_THETA = 10000.0
SG_GROUPS, SG_DIM, SG_CHUNK = 4, 128, 128
SG_WIDTH = SG_GROUPS * SG_DIM
N_BRANCH = 3
N_EXPERTS, TOP_K, D_EXPERT = 32, 4, 1024
SWIGLU_LIMIT, SWIGLU_ALPHA = 7.0, 1.702

N_CTX = N_SEQ_CTX * S_CTX
N_SMP = N_SEQ_SMP * S_SMP
N_TOK = N_CTX + N_SMP

LANES = 128
SUBLANES = 8
VMEM_LIMIT = 56 * 1024 * 1024

TW = 256
N_TILES = N_TOK // TW
N_TILES_CTX = N_CTX // TW
TILES_PER_SMP_SEQ = S_SMP // TW
HEAD_PAD = LANES
TQ = 256
EXPERT_CAP = N_TOK
SLOT_CHUNK = 256
N_CHUNK_STEPS = N_TOK * TOP_K // SLOT_CHUNK + N_EXPERTS
SC_ROW = 256
SC_SPLIT = D_MODEL // (2 * SC_ROW)
SC_WIN = 128

ZA_W = 4 * ML_WIDTH
ZS_W = 512
ZC_W = 2 * SG_WIDTH
ZB_W = N_BRANCH * D_MODEL
ZIN_W = ZA_W + ZS_W + ZC_W + ZB_W
GATE_LANE0 = MLA_ROPE


def _cparams(sem):
    return pltpu.CompilerParams(dimension_semantics=sem, vmem_limit_bytes=VMEM_LIMIT)


def _mod_row(i):
    return jnp.where(i < N_TILES_CTX, 0, 1 + (i - N_TILES_CTX) // TILES_PER_SMP_SEQ)


def _rms(x, g, n=None):
    ms = jnp.sum(x * x, axis=-1, keepdims=True) * (1.0 / (n or x.shape[-1]))
    return x * lax.rsqrt(ms + EPS) * g


def _gelu(x):
    return 0.5 * x * (1.0 + jnp.tanh(0.7978845608028654 * (x + 0.044715 * (x * x * x))))


def _pack_pairs(lo, hi):
    lo_bits = lax.bitcast_convert_type(lo.astype(BF16).astype(F32), jnp.uint32)
    hi_bits = lax.bitcast_convert_type(hi.astype(BF16).astype(F32), jnp.uint32)
    return (lo_bits >> 16) | (hi_bits & jnp.uint32(0xFFFF0000))


def _unpack_pairs(u):
    return (lax.bitcast_convert_type(u << 16, F32), lax.bitcast_convert_type(u & jnp.uint32(0xFFFF0000), F32))


def _dot(a, b):
    return jnp.dot(a, b, preferred_element_type=F32)


def _dot_nt(a, b):
    return lax.dot_general(a, b, (((1,), (1,)), ((), ())), preferred_element_type=F32)


def _adaln_kernel(c_ref, w_ref, b_ref, o_ref):
    c = c_ref[...]
    s = c * jax.nn.sigmoid(c)
    o_ref[0] = jnp.dot(s, w_ref[0], precision=HI, preferred_element_type=F32) + b_ref[0]


def _adaln(cvec, w_ada, b_ada):
    nchunk = 4
    cw = 6 * D_MODEL // nchunk
    return pl.pallas_call(
        _adaln_kernel,
        grid=(DEPTH, nchunk),
        in_specs=[
            pl.BlockSpec((SUBLANES, D_MODEL), lambda l, j: (0, 0)),
            pl.BlockSpec((1, D_MODEL, cw), lambda l, j: (l, 0, j)),
            pl.BlockSpec((1, 1, cw), lambda l, j: (l, 0, j)),
        ],
        out_specs=pl.BlockSpec((1, SUBLANES, cw), lambda l, j: (l, 0, j)),
        out_shape=jax.ShapeDtypeStruct((DEPTH, SUBLANES, 6 * D_MODEL), F32),
        compiler_params=_cparams(("arbitrary", "arbitrary")),
        name="adaln",
    )(cvec, w_ada, b_ada.reshape(DEPTH, 1, 6 * D_MODEL))


IN_SPLITS = (ML_WIDTH, ML_WIDTH, ML_WIDTH, ML_WIDTH, 4 * ML_HEADS, MLA_Q_RANK, MLA_KV_RANK, MLA_ROPE, SG_WIDTH, SG_WIDTH,
             N_BRANCH * D_MODEL)
IN_OFFS = tuple(int(v) for v in np.cumsum((0,) + IN_SPLITS))
D_IN = IN_OFFS[-1]
W_PREP_ROWS = 256


def _w_in_prep_kernel(w_ref, o_ref):
    w = w_ref[0]
    o = IN_OFFS
    o_ref[0, :, 0:o[1]] = w[:, 0:o[1]].astype(BF16)
    o_ref[0, :, o[1]:o[2]] = (w[:, o[1]:o[2]] * (ML_DIM ** -0.5)).astype(BF16)
    o_ref[0, :, o[2]:o[4]] = w[:, o[2]:o[4]].astype(BF16)
    small = jnp.concatenate(
        [w[:, o[5]:o[8]], w[:, o[4]:o[5]], jnp.zeros((W_PREP_ROWS, ZS_W - (o[8] - o[4])), F32)], axis=1)
    o_ref[0, :, ZA_W:ZA_W + ZS_W] = small.astype(BF16)
    o_ref[0, :, ZA_W + ZS_W:ZIN_W] = w[:, o[8]:o[11]].astype(BF16)


def _w_in_prep(w_in):
    return pl.pallas_call(
        _w_in_prep_kernel,
        grid=(DEPTH, D_MODEL // W_PREP_ROWS),
        in_specs=[pl.BlockSpec((1, W_PREP_ROWS, D_IN), lambda l, r: (l, r, 0))],
        out_specs=pl.BlockSpec((1, W_PREP_ROWS, ZIN_W), lambda l, r: (l, r, 0)),
        out_shape=jax.ShapeDtypeStruct((DEPTH, D_MODEL, ZIN_W), BF16),
        compiler_params=_cparams(("arbitrary", "arbitrary")),
        name="w_in_prep",
    )(w_in)


def _inproj_kernel(x_ref, g_ref, sh_ref, sc_ref, w_ref, za_ref, zs_ref, zc_ref, zb_ref):
    h = _rms(x_ref[...], g_ref[0]) * (1.0 + sc_ref[0, 0]) + sh_ref[0, 0]
    hb = h.astype(BF16)
    za_ref[...] = _dot(hb, w_ref[0, :, 0:ZA_W]).astype(BF16)
    zs_ref[...] = _dot(hb, w_ref[0, :, ZA_W:ZA_W + ZS_W])
    zc_ref[...] = _dot(hb, w_ref[0, :, ZA_W + ZS_W:ZA_W + ZS_W + ZC_W]).astype(BF16)
    zb_ref[...] = _dot(hb, w_ref[0, :, ZA_W + ZS_W + ZC_W:ZIN_W]).astype(BF16)


def _mod_spec(layer, k):
    return pl.BlockSpec((1, 1, 1, D_MODEL), lambda i: (layer, _mod_row(i), 0, k))


def _inproj(layer, x, norm1, mod, w_in_r):
    tok = lambda w: pl.BlockSpec((TW, w), lambda i: (i, 0))
    return pl.pallas_call(
        _inproj_kernel,
        grid=(N_TILES,),
        in_specs=[
            tok(D_MODEL),
            pl.BlockSpec((1, 1, D_MODEL), lambda i: (layer, 0, 0)),
            _mod_spec(layer, 0),
            _mod_spec(layer, 1),
            pl.BlockSpec((1, D_MODEL, ZIN_W), lambda i: (layer, 0, 0)),
        ],
        out_specs=[tok(ZA_W), tok(ZS_W), tok(ZC_W), tok(ZB_W)],
        out_shape=[
            jax.ShapeDtypeStruct((N_TOK, ZA_W), BF16),
            jax.ShapeDtypeStruct((N_TOK, ZS_W), F32),
            jax.ShapeDtypeStruct((N_TOK, ZC_W), BF16),
            jax.ShapeDtypeStruct((N_TOK, ZB_W), BF16),
        ],
        compiler_params=_cparams(("arbitrary",)),
        name="inproj",
    )(x, norm1, mod, mod, w_in_r)


def _make_mlstm_kernel(seq, layer, has_init):
    nq = seq // TQ
    lane_if, lane_ff, lane_ib, lane_fb = (GATE_LANE0 + ML_HEADS * j for j in range(4))

    def kern(*refs):
        if has_init:
            body(*refs)
            return
        out = refs[7]
        b = pl.program_id(0)

        @pl.when(b < N_SEQ_CTX)
        def _():
            body(*refs)

        @pl.when(b >= N_SEQ_CTX)
        def _():
            out[...] = jnp.zeros_like(out)

    def body(*refs):
        if has_init:
            m0_ref, zq, zk, zv, zo, gz, bg, nrm, c0_ref, n0_ref, _, out, bp_scr, bs_scr = refs
        else:
            zq, zk, zv, zo, gz, bg, nrm, out, cf_ref, nf_ref, mf_ref, bp_scr, bs_scr = refs
        b = pl.program_id(0)
        g = gz[...] + bg[0]
        lane = lax.broadcasted_iota(jnp.int32, g.shape, 1)
        is_forget = ((lane >= lane_ff) & (lane < lane_ib)) | ((lane >= lane_fb) & (lane < lane_fb + ML_HEADS))
        log_sig = jnp.minimum(g, 0.0) - jnp.log1p(jnp.exp(-jnp.abs(g)))
        a = jnp.where(is_forget, log_sig, g)
        r_i = lax.broadcasted_iota(jnp.int32, (seq, seq), 0)
        c_i = lax.broadcasted_iota(jnp.int32, (seq, seq), 1)
        ltri = (c_i <= r_i).astype(F32)
        bp = jnp.dot(ltri, a, precision=HI, preferred_element_type=F32)
        bs = bp[seq - 1:seq, :] - bp + a
        bp_scr[...] = bp
        bs_scr[...] = bs
        a_t, bp_t, bs_t = a.T, bp.T, bs.T

        for h in range(ML_HEADS):
            hs = slice(h * ML_DIM, (h + 1) * ML_DIM)
            k = zk[:, hs]
            v = zv[:, hs]
            rows = (
                a_t[lane_if + h:lane_if + h + 1, :] - bp_t[lane_ff + h:lane_ff + h + 1, :],
                a_t[lane_ib + h:lane_ib + h + 1, :] - bs_t[lane_fb + h:lane_fb + h + 1, :],
            )
            col_refs = ((bp_scr, lane_ff + h), (bs_scr, lane_fb + h))
            if has_init:
                m0 = tuple(m0_ref[((b * DEPTH + layer) * 2 + dr) * ML_HEADS + h] for dr in range(2))
                c0 = tuple(c0_ref[0, 0, dr, h].astype(BF16) for dr in range(2))
                n0 = tuple(n0_ref[0, 0, dr, h:h + 1, :] for dr in range(2))
            else:
                m0 = (0.0, 0.0)

            def qblock(qi, carry):
                q0 = pl.multiple_of(qi * TQ, TQ)
                qb = zq[pl.ds(q0, TQ), hs]
                sc = _dot_nt(qb, k)
                t_idx = q0 + lax.broadcasted_iota(jnp.int32, (TQ, seq), 0)
                s_idx = lax.broadcasted_iota(jnp.int32, (TQ, seq), 1)
                hsum = jnp.zeros((TQ, ML_DIM), F32)
                for dr in range(2):
                    cref, cl = col_refs[dr]
                    col = cref[pl.ds(q0, TQ), cl:cl + 1]
                    mask = (s_idx <= t_idx) if dr == 0 else (s_idx >= t_idx)
                    d = jnp.where(mask, col + rows[dr], -jnp.inf)
                    a0 = col + m0[dr]
                    m_t = jnp.maximum(a0, jnp.max(d, axis=1, keepdims=True))
                    s = sc * jnp.exp(d - m_t)
                    num = _dot(s.astype(BF16), v)
                    den = jnp.sum(s, axis=1, keepdims=True)
                    if has_init:
                        w_c = jnp.exp(a0 - m_t)
                        num = num + w_c * _dot(qb, c0[dr])
                        den = den + w_c * jnp.sum(qb.astype(F32) * n0[dr], axis=1, keepdims=True)
                    hsum = hsum + num / jnp.maximum(jnp.abs(den), jnp.exp(-m_t))
                hn = _rms(hsum, nrm[0][:, hs])
                og = zo[pl.ds(q0, TQ), hs].astype(F32)
                out[pl.ds(q0, TQ), hs] = (hn * jax.nn.sigmoid(og)).astype(out.dtype)
                return carry

            if nq == 1:
                qblock(0, 0)
            else:
                lax.fori_loop(0, nq, qblock, 0)

            if not has_init:
                k_t = k.astype(F32).T
                kf = k.astype(F32)
                tot = (bp_t[lane_ff + h:lane_ff + h + 1, seq - 1:seq], bp_t[lane_fb + h:lane_fb + h + 1, seq - 1:seq])
                gl = (
                    tot[0] + rows[0],
                    bp_t[lane_fb + h:lane_fb + h + 1, :] - a_t[lane_fb + h:lane_fb + h + 1, :]
                    + a_t[lane_ib + h:lane_ib + h + 1, :],
                )
                for dr in range(2):
                    m_new = jnp.maximum(tot[dr] + m0[dr], jnp.max(gl[dr], axis=1, keepdims=True))
                    w_s = jnp.exp(gl[dr] - m_new)
                    cf_ref[0, dr, h] = _dot((k_t * w_s).astype(BF16), v)
                    n_new = jnp.dot(jnp.broadcast_to(w_s, (SUBLANES, seq)), kf, precision=HI, preferred_element_type=F32)
                    nf_ref[0, dr, h:h + 1, :] = n_new[0:1, :]
                    mf_ref[0, dr * ML_HEADS + h:dr * ML_HEADS + h + 1, :] = jnp.broadcast_to(m_new, (1, LANES))

    return kern


def _mlstm(layer, za, zs, b_gates, mlstm_norm, init=None, ctx_out=None):
    has_init = init is not None
    seq, nseq, row0 = (S_SMP, N_SEQ_SMP, N_CTX // S_SMP) if has_init else (S_CTX, N_SEQ_CTX, 0)
    qkvo = [pl.BlockSpec((seq, ML_WIDTH), functools.partial(lambda j, b: (row0 + b, j), j)) for j in range(4)]
    in_specs = qkvo + [
        pl.BlockSpec((seq, LANES), lambda b: (row0 + b, ZS_W // LANES - 1)),
        pl.BlockSpec((1, 1, LANES), lambda b: (layer, 0, 0)),
        pl.BlockSpec((1, 1, ML_WIDTH), lambda b: (layer, 0, 0)),
    ]
    args = [za, za, za, za, zs, b_gates, mlstm_norm]
    out_specs = [pl.BlockSpec((seq, ML_WIDTH), lambda b: (row0 + b, 0))]
    out_shape = [jax.ShapeDtypeStruct((N_TOK, ML_WIDTH), BF16)]
    aliases = {}
    if has_init:
        st_c, st_n, st_m = init
        in_specs = [pl.BlockSpec(memory_space=pltpu.SMEM)] + in_specs + [
            pl.BlockSpec((1, 1, 2, ML_HEADS, ML_DIM, ML_DIM), lambda b: (b, layer, 0, 0, 0, 0)),
            pl.BlockSpec((1, 1, 2, ML_HEADS, ML_DIM), lambda b: (b, layer, 0, 0, 0)),
            pl.BlockSpec(memory_space=pl.ANY),
        ]
        args = [st_m.reshape(-1)] + args + [st_c, st_n, ctx_out]
        aliases = {len(args) - 1: 0}
    else:
        out_specs += [
            pl.BlockSpec((1, 2, ML_HEADS, ML_DIM, ML_DIM), lambda b: (jnp.minimum(b, nseq - 1), 0, 0, 0, 0)),
            pl.BlockSpec((1, 2, ML_HEADS, ML_DIM), lambda b: (jnp.minimum(b, nseq - 1), 0, 0, 0)),
            pl.BlockSpec((1, 2 * ML_HEADS, LANES), lambda b: (jnp.minimum(b, nseq - 1), 0, 0)),
        ]
        out_shape += [
            jax.ShapeDtypeStruct((nseq, 2, ML_HEADS, ML_DIM, ML_DIM), F32),
            jax.ShapeDtypeStruct((nseq, 2, ML_HEADS, ML_DIM), F32),
            jax.ShapeDtypeStruct((nseq, 2 * ML_HEADS, LANES), F32),
        ]
    return pl.pallas_call(
        _make_mlstm_kernel(seq, layer, has_init),
        grid=(nseq if has_init else N_TILES,),
        in_specs=in_specs,
        out_specs=out_specs,
        out_shape=out_shape,
        scratch_shapes=[pltpu.VMEM((seq, LANES), F32), pltpu.VMEM((seq, LANES), F32)],
        input_output_aliases=aliases,
        compiler_params=_cparams(("arbitrary",)),
        name="mlstm_smp" if has_init else "mlstm_ctx",
    )(*args)


def _rope(x, cos, sin_a, sin_b):
    return x * cos + pltpu.roll(x, LANES - 8, 1) * sin_a + pltpu.roll(x, 8, 1) * sin_b


def _mla_prep_kernel(zs_ref, qa_ref, kva_ref, wuq_ref, wk_ref, wv_ref, qn_ref, kn_ref, cos_ref, sa_ref, sb_ref,
                     q_ref, kk_ref, v_ref, ckv_ref, qf_scr, kf_scr):
    cq = zs_ref[:, 0:MLA_Q_RANK]
    ckv = zs_ref[:, MLA_Q_RANK:MLA_Q_RANK + MLA_KV_RANK]
    last = zs_ref[:, ZS_W - LANES:ZS_W]
    qf_scr[...] = _dot(_rms(cq, qa_ref[0]).astype(BF16), wuq_ref[0])
    ckvn = _rms(ckv, kva_ref[0])
    ckv_ref[...] = ckvn
    cb = ckvn.astype(BF16)
    kf_scr[...] = _dot(cb, wk_ref[0])
    v_ref[...] = _dot(cb, wv_ref[0]).astype(BF16)
    lane = lax.broadcasted_iota(jnp.int32, last.shape, 1)
    kr = jnp.where((lane >= MLA_NOPE) & (lane < MLA_QK), pltpu.roll(last, MLA_NOPE, 1), 0.0)
    is_latent = pl.program_id(0) >= N_TILES_CTX

    def heads(rotate):
        for h in range(MLA_HEADS):
            hs = slice(h * HEAD_PAD, (h + 1) * HEAD_PAD)
            q_ref[:, hs] = rotate(_rms(qf_scr[:, hs], qn_ref[0], n=MLA_QK)).astype(BF16)
            kk_ref[:, hs] = rotate(_rms(kf_scr[:, hs] + kr, kn_ref[0], n=MLA_QK)).astype(BF16)

    @pl.when(is_latent)
    def _():
        cos, sa, sb = cos_ref[...], sa_ref[...], sb_ref[...]
        heads(lambda x: _rope(x, cos, sa, sb))

    @pl.when(jnp.logical_not(is_latent))
    def _():
        heads(lambda x: x)


def _mla_prep(layer, zs, q_a_norm, kv_a_norm, w_uq_r, w_k_r, w_v_r, q_norm_p, k_norm_p, rope_tabs):
    lw = lambda shape: pl.BlockSpec((1,) + shape, lambda i: (layer,) + (0,) * len(shape))
    tab = pl.BlockSpec((TW, LANES), lambda i: (jnp.where(i < N_TILES_CTX, 0, 1 + (i - N_TILES_CTX) % TILES_PER_SMP_SEQ), 0))
    tok = lambda w: pl.BlockSpec((TW, w), lambda i: (i, 0))
    return pl.pallas_call(
        _mla_prep_kernel,
        grid=(N_TILES,),
        in_specs=[
            tok(ZS_W), lw((1, MLA_Q_RANK)), lw((1, MLA_KV_RANK)),
            lw((MLA_Q_RANK, MLA_HEADS * HEAD_PAD)), lw((MLA_KV_RANK, MLA_HEADS * HEAD_PAD)),
            lw((MLA_KV_RANK, MLA_HEADS * MLA_V)), lw((1, HEAD_PAD)), lw((1, HEAD_PAD)), tab, tab, tab,
        ],
        out_specs=[tok(MLA_HEADS * HEAD_PAD), tok(MLA_HEADS * HEAD_PAD), tok(MLA_HEADS * MLA_V), tok(MLA_KV_RANK)],
        out_shape=[
            jax.ShapeDtypeStruct((N_TOK, MLA_HEADS * HEAD_PAD), BF16),
            jax.ShapeDtypeStruct((N_TOK, MLA_HEADS * HEAD_PAD), BF16),
            jax.ShapeDtypeStruct((N_TOK, MLA_HEADS * MLA_V), BF16),
            jax.ShapeDtypeStruct((N_TOK, MLA_KV_RANK), F32),
        ],
        scratch_shapes=[pltpu.VMEM((TW, MLA_HEADS * HEAD_PAD), F32), pltpu.VMEM((TW, MLA_HEADS * HEAD_PAD), F32)],
        compiler_params=_cparams(("arbitrary",)),
        name="mla_prep",
    )(zs, q_a_norm, kv_a_norm, w_uq_r, w_k_r, w_v_r, q_norm_p, k_norm_p, *rope_tabs)


def _cache_kv_kernel(ckv_ref, kr_ref, wk_ref, wv_ref, kn_ref, kk_ref, v_ref):
    cb = ckv_ref[...].astype(BF16)
    kf = _dot(cb, wk_ref[0])
    v_ref[...] = _dot(cb, wv_ref[0]).astype(BF16)
    kr = kr_ref[...]
    for h in range(MLA_HEADS):
        hs = slice(h * HEAD_PAD, (h + 1) * HEAD_PAD)
        kk_ref[:, hs] = _rms(kf[:, hs] + kr, kn_ref[0], n=MLA_QK).astype(BF16)


def _cache_kv(cache_ckv, cache_kr_pad, w_k_r, w_v_r, k_norm_p):
    lw = lambda shape: pl.BlockSpec((1,) + shape, lambda b, l: (l,) + (0,) * len(shape))
    blk = lambda w: pl.BlockSpec((None, None, PAST_LEN, w), lambda b, l: (b, l, 0, 0))
    return pl.pallas_call(
        _cache_kv_kernel,
        grid=(N_SEQ_SMP, DEPTH),
        in_specs=[blk(MLA_KV_RANK), blk(LANES), lw((MLA_KV_RANK, MLA_HEADS * HEAD_PAD)),
                  lw((MLA_KV_RANK, MLA_HEADS * MLA_V)), lw((1, HEAD_PAD))],
        out_specs=[blk(MLA_HEADS * HEAD_PAD), blk(MLA_HEADS * MLA_V)],
        out_shape=[
            jax.ShapeDtypeStruct((N_SEQ_SMP, DEPTH, PAST_LEN, MLA_HEADS * HEAD_PAD), BF16),
            jax.ShapeDtypeStruct((N_SEQ_SMP, DEPTH, PAST_LEN, MLA_HEADS * MLA_V), BF16),
        ],
        compiler_params=_cparams(("arbitrary", "arbitrary")),
        name="cache_kv",
    )(cache_ckv, cache_kr_pad, w_k_r, w_v_r, k_norm_p)


def _make_attn_kernel(n_src):
    scale = MLA_QK ** -0.5

    def kern(q_ref, *refs):
        o_ref = refs[-1]
        if n_src > 1:
            body(q_ref, *refs)
            return

        @pl.when(pl.program_id(0) < N_SEQ_CTX)
        def _():
            body(q_ref, *refs)

        @pl.when(pl.program_id(0) >= N_SEQ_CTX)
        def _():
            o_ref[...] = jnp.zeros_like(o_ref)

    def body(q_ref, *refs):
        o_ref = refs[-1]
        for h in range(MLA_HEADS):
            hs = slice(h * HEAD_PAD, (h + 1) * HEAD_PAD)
            vs = slice(h * MLA_V, (h + 1) * MLA_V)
            q = q_ref[:, hs]
            ss = [_dot_nt(q, refs[2 * j][:, hs]) * scale for j in range(n_src)]
            m = functools.reduce(jnp.maximum, [jnp.max(s, axis=1, keepdims=True) for s in ss])
            ps = [jnp.exp(s - m) for s in ss]
            l = functools.reduce(jnp.add, [jnp.sum(p, axis=1, keepdims=True) for p in ps])
            o = functools.reduce(jnp.add, [_dot(ps[j].astype(BF16), refs[2 * j + 1][:, vs]) for j in range(n_src)])
            o_ref[:, vs] = (o / l).astype(o_ref.dtype)

    return kern


def _attn_ctx(q, kk, v):
    blk = lambda w: pl.BlockSpec((S_CTX, w), lambda b: (b, 0))
    return pl.pallas_call(
        _make_attn_kernel(1),
        grid=(N_TILES,),
        in_specs=[blk(MLA_HEADS * HEAD_PAD), blk(MLA_HEADS * HEAD_PAD), blk(MLA_HEADS * MLA_V)],
        out_specs=blk(MLA_HEADS * MLA_V),
        out_shape=jax.ShapeDtypeStruct((N_TOK, MLA_HEADS * MLA_V), BF16),
        compiler_params=_cparams(("arbitrary",)),
        name="attn_ctx",
    )(q, kk, v)


def _attn_smp(layer, q, kk, v, kk_cache, v_cache, ctx_out):
    row0 = N_CTX // S_SMP
    nq = S_SMP // TQ
    seqb = lambda w: pl.BlockSpec((S_SMP, w), lambda b, i: (row0 + b, 0))
    cache = lambda w: pl.BlockSpec((None, None, PAST_LEN, w), lambda b, i: (b, layer, 0, 0))
    return pl.pallas_call(
        _make_attn_kernel(2),
        grid=(N_SEQ_SMP, nq),
        in_specs=[
            pl.BlockSpec((TQ, MLA_HEADS * HEAD_PAD), lambda b, i: (N_CTX // TQ + b * nq + i, 0)),
            seqb(MLA_HEADS * HEAD_PAD), seqb(MLA_HEADS * MLA_V),
            cache(MLA_HEADS * HEAD_PAD), cache(MLA_HEADS * MLA_V),
            pl.BlockSpec(memory_space=pl.ANY),
        ],
        out_specs=pl.BlockSpec((TQ, MLA_HEADS * MLA_V), lambda b, i: (N_CTX // TQ + b * nq + i, 0)),
        out_shape=jax.ShapeDtypeStruct((N_TOK, MLA_HEADS * MLA_V), BF16),
        input_output_aliases={5: 0},
        compiler_params=_cparams(("arbitrary", "arbitrary")),
        name="attn_smp",
    )(q, kk, v, kk_cache, v_cache, ctx_out)


def _merge_kernel(x_ref, oa_ref, ob_ref, zc_ref, zb_ref, sgn_ref, ws_ref, bs_ref, wb_ref, wo_ref, g1_ref, n2_ref,
                  sh2_ref, sc2_ref, wr_ref, br_ref,
                  xmid_ref, h2_ref, dest_ref, wsel_ref, cnt_ref, oc_scr, carry_scr):
    i = pl.program_id(0)

    @pl.when(i == 0)
    def _():
        carry_scr[...] = jnp.zeros_like(carry_scr)

    u = _gelu(zc_ref[:, 0:SG_WIDTH].astype(F32))
    vg = _gelu(zc_ref[:, SG_WIDTH:2 * SG_WIDTH].astype(F32))
    for g in range(SG_GROUPS):
        gs = slice(g * SG_DIM, (g + 1) * SG_DIM)
        vn = _rms(vg[:, gs], sgn_ref[0][:, gs]).astype(BF16)
        for c in range(TW // SG_CHUNK):
            cs = slice(c * SG_CHUNK, (c + 1) * SG_CHUNK)
            mixed = _dot(ws_ref[0, g], vn[cs, :]) + bs_ref[0][:, g:g + 1]
            oc_scr[cs, gs] = (u[cs, gs] * mixed).astype(BF16)

    acc = jnp.zeros((TW, D_MODEL), F32)
    for j, src in enumerate((oa_ref, ob_ref, oc_scr)):
        gate = jax.nn.sigmoid(zb_ref[:, j * D_MODEL:(j + 1) * D_MODEL].astype(F32))
        acc = acc + gate * _dot(src[...], wb_ref[0, j])
    xm = x_ref[...] + g1_ref[0, 0] * _dot(acc.astype(BF16), wo_ref[0])
    xmid_ref[...] = xm
    h2 = _rms(xm, n2_ref[0]) * (1.0 + sc2_ref[0, 0]) + sh2_ref[0, 0]
    for c in range(SC_SPLIT):
        h2_ref[c] = _pack_pairs(h2[:, 2 * c * SC_ROW:(2 * c + 1) * SC_ROW], h2[:, (2 * c + 1) * SC_ROW:(2 * c + 2) * SC_ROW])

    h_hi = h2.astype(BF16)
    h_lo = (h2 - h_hi.astype(F32)).astype(BF16)
    p_hi = _dot(h_hi, wr_ref[0])
    logits = p_hi[:, 0:LANES] + p_hi[:, LANES:2 * LANES] + _dot(h_lo, wr_ref[0, :, 0:LANES]) + br_ref[0]
    lane = lax.broadcasted_iota(jnp.int32, logits.shape, 1)
    hits, exps = [], []
    sel = jnp.zeros(logits.shape, F32)
    denom = jnp.zeros((TW, 1), F32)
    top = None
    for _ in range(TOP_K):
        m = jnp.max(logits, axis=1, keepdims=True)
        idx = jnp.min(jnp.where(logits == m, lane, LANES), axis=1, keepdims=True)
        hit = lane == idx
        top = m if top is None else top
        hits.append(hit)
        exps.append(jnp.exp(m - top))
        sel = jnp.where(hit, 1.0, sel)
        denom = denom + exps[-1]
        logits = jnp.where(hit, -jnp.inf, logits)

    r_i = lax.broadcasted_iota(jnp.int32, (TW, TW), 0)
    c_i = lax.broadcasted_iota(jnp.int32, (TW, TW), 1)
    carry = carry_scr[0:1, :]
    rank = _dot(jnp.where(c_i < r_i, 1.0, 0.0).astype(BF16), sel.astype(BF16)) + carry
    new_carry = carry + jnp.sum(sel, axis=0, keepdims=True)
    carry_scr[...] = jnp.broadcast_to(new_carry, (SUBLANES, LANES))
    cnt_ref[...] = jnp.broadcast_to(new_carry, (SUBLANES, LANES))
    slot = rank + lane.astype(F32) * float(EXPERT_CAP)
    dmat = jnp.zeros(logits.shape, F32)
    wmat = jnp.zeros(logits.shape, F32)
    for k in range(TOP_K):
        dk = jnp.sum(jnp.where(hits[k], slot, 0.0), axis=1, keepdims=True)
        dmat = jnp.where(lane == k, dk, dmat)
        wmat = jnp.where(lane == k, exps[k] / denom, wmat)
    dest_ref[...] = dmat.T[0:SUBLANES, :].astype(jnp.int32)
    wsel_ref[...] = wmat


def _merge(layer, x, oa, ob, zc, zb, sg_norm, w_sp, b_sp, w_branch, w_out, mod, norm2, w_router_p, b_router_p):
    lw = lambda shape: pl.BlockSpec((1,) + shape, lambda i: (layer,) + (0,) * len(shape))
    tok = lambda w: pl.BlockSpec((TW, w), lambda i: (i, 0))
    return pl.pallas_call(
        _merge_kernel,
        grid=(N_TILES,),
        in_specs=[
            tok(D_MODEL), tok(ML_WIDTH), tok(MLA_HEADS * MLA_V), tok(ZC_W), tok(ZB_W),
            lw((1, SG_WIDTH)), lw((SG_GROUPS, SG_CHUNK, SG_CHUNK)), lw((SG_CHUNK, LANES)),
            lw((N_BRANCH, ML_WIDTH, D_MODEL)), lw((D_MODEL, D_MODEL)),
            _mod_spec(layer, 2), lw((1, D_MODEL)), _mod_spec(layer, 3), _mod_spec(layer, 4),
            lw((D_MODEL, 2 * LANES)), lw((1, LANES)),
        ],
        out_specs=[tok(D_MODEL), pl.BlockSpec((SC_SPLIT, TW, SC_ROW), lambda i: (0, i, 0)),
                   pl.BlockSpec((SUBLANES, TW), lambda i: (0, i)), tok(LANES),
                   pl.BlockSpec((SUBLANES, LANES), lambda i: (0, 0))],
        out_shape=[
            jax.ShapeDtypeStruct((N_TOK, D_MODEL), F32),
            jax.ShapeDtypeStruct((SC_SPLIT, N_TOK, SC_ROW), jnp.uint32),
            jax.ShapeDtypeStruct((SUBLANES, N_TOK), jnp.int32),
            jax.ShapeDtypeStruct((N_TOK, LANES), F32),
            jax.ShapeDtypeStruct((SUBLANES, LANES), F32),
        ],
        scratch_shapes=[pltpu.VMEM((TW, SG_WIDTH), BF16), pltpu.VMEM((SUBLANES, LANES), F32)],
        compiler_params=_cparams(("arbitrary",)),
        name="merge_router",
    )(x, oa, ob, zc, zb, sg_norm, w_sp, b_sp, w_branch, w_out, mod, norm2, mod, mod, w_router_p, b_router_p)


def _sc_mesh():
    return plsc.VectorSubcoreMesh(core_axis_name="core", subcore_axis_name="subcore")


def _sc_scatter_rows(x, idx, n_rows):
    m = idx.shape[1]
    n_src_blocks = x.shape[0] // SC_WIN

    @pl.kernel(out_type=jax.ShapeDtypeStruct((n_rows, SC_ROW), x.dtype), mesh=_sc_mesh(), scratch_types=[])
    def scatter(x_hbm, i_hbm, o_hbm):
        def body(x_vmem, i_vmem):
            pltpu.sync_copy(x_vmem, o_hbm.at[i_vmem.at[0]])

        pltpu.emit_pipeline(
            body,
            grid=(m // SC_WIN,),
            in_specs=[pl.BlockSpec((SC_WIN, SC_ROW), lambda i: (i % n_src_blocks, 0)),
                      pl.BlockSpec((1, SC_WIN), lambda i: (0, i))],
            out_specs=[],
            core_axis_name=("core", "subcore"),
            dimension_semantics=(pltpu.PARALLEL,),
        )(x_hbm, i_hbm)

    return scatter(x, idx)


def _sc_gather_rows(x, idx):
    m = idx.shape[1]

    @pl.kernel(out_type=jax.ShapeDtypeStruct((m, SC_ROW), x.dtype), mesh=_sc_mesh())
    def gather(x_hbm, i_hbm, o_hbm):
        def body(i_vmem, o_vmem):
            pltpu.sync_copy(x_hbm.at[i_vmem.at[0]], o_vmem)

        pltpu.emit_pipeline(
            body,
            grid=(m // SC_WIN,),
            in_specs=[pl.BlockSpec((1, SC_WIN), lambda i: (0, i))],
            out_specs=[pl.BlockSpec((SC_WIN, SC_ROW), lambda i: (i, 0))],
            core_axis_name=("core", "subcore"),
            dimension_semantics=(pltpu.PARALLEL,),
        )(i_hbm, o_hbm)

    return gather(x, idx)


STEP_VALID, STEP_FIRST, STEP_HAS_NEXT = 1, 2, 4


def _moe_ffn_kernel(layer, be_ref, nx_ref, br_ref, fl_ref, xs_ref, b1_ref, b2_ref, w1_hbm, w2_hbm, y_ref,
                    w1f, w2f, w1b, w2b, sem):
    g = pl.program_id(0)
    flags = fl_ref[g]

    def weight_copies(e):
        return (pltpu.make_async_copy(w1_hbm.at[layer, e], w1f, sem.at[0]),
                pltpu.make_async_copy(w2_hbm.at[layer, e], w2f, sem.at[1]))

    @pl.when(g == 0)
    def _():
        for cp in weight_copies(be_ref[0]):
            cp.start()

    @pl.when((flags & STEP_FIRST) != 0)
    def _():
        for cp in weight_copies(be_ref[g]):
            cp.wait()
        w1b[...] = w1f[...].astype(BF16)
        w2b[...] = w2f[...].astype(BF16)

        @pl.when((flags & STEP_HAS_NEXT) != 0)
        def _():
            for cp in weight_copies(nx_ref[g]):
                cp.start()

    @pl.when((flags & STEP_VALID) != 0)
    def _():
        halves = [h.astype(BF16) for c in range(SC_SPLIT) for h in _unpack_pairs(xs_ref[c])]
        g1 = _dot(jnp.concatenate(halves, axis=1), w1b[...]) + b1_ref[0, 0]
        gate = jnp.minimum(g1[:, :D_EXPERT], SWIGLU_LIMIT)
        up = jnp.clip(g1[:, D_EXPERT:], -SWIGLU_LIMIT, SWIGLU_LIMIT)
        act = gate * jax.nn.sigmoid(SWIGLU_ALPHA * gate) * (up + 1.0)
        y = _dot(act.astype(BF16), w2b[...]) + b2_ref[0, 0]
        for c in range(SC_SPLIT):
            y_ref[c] = _pack_pairs(y[:, 2 * c * SC_ROW:(2 * c + 1) * SC_ROW], y[:, (2 * c + 1) * SC_ROW:(2 * c + 2) * SC_ROW])


def _moe_ffn(layer, xs, plan, w1, b1, w2, b2):
    eb = lambda c: pl.BlockSpec((1, 1, 1, c), lambda g, be, nx, br, fl: (layer, be[g], 0, 0))
    rows = pl.BlockSpec((SC_SPLIT, SLOT_CHUNK, SC_ROW), lambda g, be, nx, br, fl: (0, br[g], 0))
    hbm = pl.BlockSpec(memory_space=pl.ANY)
    grid_spec = pltpu.PrefetchScalarGridSpec(
        num_scalar_prefetch=4,
        grid=(N_CHUNK_STEPS,),
        in_specs=[rows, eb(2 * D_EXPERT), eb(D_MODEL), hbm, hbm],
        out_specs=rows,
        scratch_shapes=[
            pltpu.VMEM((D_MODEL, 2 * D_EXPERT), F32), pltpu.VMEM((D_EXPERT, D_MODEL), F32),
            pltpu.VMEM((D_MODEL, 2 * D_EXPERT), BF16), pltpu.VMEM((D_EXPERT, D_MODEL), BF16),
            pltpu.SemaphoreType.DMA((2,)),
        ],
    )
    return pl.pallas_call(
        functools.partial(_moe_ffn_kernel, layer),
        grid_spec=grid_spec,
        out_shape=jax.ShapeDtypeStruct(xs.shape, xs.dtype),
        compiler_params=_cparams(("arbitrary",)),
        name="moe_ffn",
    )(*plan, xs, b1, b2, w1, w2)


def _chunk_plan(cnt):
    nch = (cnt + SLOT_CHUNK - 1) // SLOT_CHUNK
    cum = jnp.cumsum(nch)
    expert_at = lambda step: jnp.minimum(
        jnp.sum((cum[None, :] <= step[:, None]).astype(jnp.int32), axis=1), N_EXPERTS - 1)
    steps = jnp.arange(N_CHUNK_STEPS, dtype=jnp.int32)
    g = jnp.minimum(steps, cum[-1] - 1)
    e = expert_at(g)
    is_e = jnp.arange(N_EXPERTS, dtype=jnp.int32)[None, :] == e[:, None]
    end_e = jnp.sum(jnp.where(is_e, cum[None, :], 0), axis=1)
    j = g - jnp.sum(jnp.where(is_e, (cum - nch)[None, :], 0), axis=1)
    valid = steps < cum[-1]
    first = valid & (j == 0)
    flags = (STEP_VALID * valid.astype(jnp.int32) + STEP_FIRST * first.astype(jnp.int32)
             + STEP_HAS_NEXT * (first & (end_e < cum[-1])).astype(jnp.int32))
    return e, expert_at(end_e), (e * (EXPERT_CAP // SLOT_CHUNK) + j).astype(jnp.int32), flags


def _combine_kernel(x_ref, yg_ref, w_ref, g_ref, o_ref):
    w = w_ref[...]
    for c in range(SC_SPLIT):
        parts = [_unpack_pairs(yg_ref[k, c]) for k in range(TOP_K)]
        for half in range(2):
            cs = slice((2 * c + half) * SC_ROW, (2 * c + half + 1) * SC_ROW)
            acc = w[:, 0:1] * parts[0][half]
            for k in range(1, TOP_K):
                acc = acc + w[:, k:k + 1] * parts[k][half]
            o_ref[:, cs] = x_ref[:, cs] + g_ref[0, 0][:, cs] * acc


def _combine(layer, xmid, yg, wsel, mod):
    tok = lambda w: pl.BlockSpec((TW, w), lambda i: (i, 0))
    return pl.pallas_call(
        _combine_kernel,
        grid=(N_TILES,),
        in_specs=[tok(D_MODEL), pl.BlockSpec((TOP_K, SC_SPLIT, TW, SC_ROW), lambda i: (0, 0, i, 0)), tok(LANES),
                  _mod_spec(layer, 5)],
        out_specs=tok(D_MODEL),
        out_shape=jax.ShapeDtypeStruct((N_TOK, D_MODEL), F32),
        compiler_params=_cparams(("arbitrary",)),
        name="combine",
    )(xmid, yg, wsel, mod)


def _moe(layer, xmid, h2, dest, wsel, cnt, mod, w1, b1, w2, b2):
    n_slots = N_EXPERTS * EXPERT_CAP
    idx = dest[0:TOP_K][:, None, :] + (jnp.arange(SC_SPLIT, dtype=jnp.int32) * n_slots)[None, :, None]
    idx = idx.reshape(1, TOP_K * SC_SPLIT * N_TOK)
    xs = _sc_scatter_rows(h2.reshape(SC_SPLIT * N_TOK, SC_ROW), idx, SC_SPLIT * n_slots)
    plan = _chunk_plan(cnt[0, :N_EXPERTS].astype(jnp.int32))
    y = _moe_ffn(layer, xs.reshape(SC_SPLIT, n_slots, SC_ROW), plan, w1, b1, w2, b2)
    yg = _sc_gather_rows(y.reshape(SC_SPLIT * n_slots, SC_ROW), idx)
    return _combine(layer, xmid, yg.reshape(TOP_K, SC_SPLIT, N_TOK, SC_ROW), wsel, mod)


def _rope_tables():
    pos = np.arange(S_SMP)
    half = MLA_ROPE // 2
    inv_freq = (ROPE_THETA ** (-(np.arange(0, half, 2, dtype=np.float32) / np.float32(half)))).astype(np.float32)
    angs = [((pos // GRID_W).astype(np.float32)[:, None] * inv_freq[None, :]).astype(np.float32),
            ((pos % GRID_W).astype(np.float32)[:, None] * inv_freq[None, :]).astype(np.float32)]
    nf = half // 2
    cos = np.ones((TW + S_SMP, LANES), np.float32)
    sin_a = np.zeros((TW + S_SMP, LANES), np.float32)
    sin_b = np.zeros((TW + S_SMP, LANES), np.float32)
    for axis, ang in enumerate(angs):
        base = MLA_NOPE + axis * half
        c, s = np.cos(ang.astype(np.float64)), np.sin(ang.astype(np.float64))
        cos[TW:, base:base + nf] = c
        cos[TW:, base + nf:base + half] = c
        sin_a[TW:, base:base + nf] = -s
        sin_b[TW:, base + nf:base + half] = s
    return jnp.asarray(cos), jnp.asarray(sin_a), jnp.asarray(sin_b)


def _pad_last(a, width):
    return jnp.pad(a, [(0, 0)] * (a.ndim - 1) + [(0, width - a.shape[-1])])


def kernel(x_prompt, x_sample, cache_mla_ckv, cache_mla_krope, state_mlstm_C, state_mlstm_n, state_mlstm_m, c, c_ctx, norm1, norm2, w_ada, b_ada, w_in, b_mlstm_gates, mlstm_norm, mla_q_a_norm, mla_kv_a_norm, w_uq, w_ukv, mla_q_norm, mla_k_norm, sg_norm, w_spatial, b_spatial, w_branch, w_out, w_router, b_router, w_exp1, b_exp1, w_exp2, b_exp2):
    w_in_r = _w_in_prep(w_in)
    w_uq_r = _pad_last(w_uq.reshape(DEPTH, MLA_Q_RANK, MLA_HEADS, MLA_QK), HEAD_PAD).reshape(
        DEPTH, MLA_Q_RANK, MLA_HEADS * HEAD_PAD).astype(BF16)
    w_ukv4 = w_ukv.reshape(DEPTH, MLA_KV_RANK, MLA_HEADS, MLA_NOPE + MLA_V)
    w_k_r = _pad_last(w_ukv4[..., :MLA_NOPE], HEAD_PAD).reshape(DEPTH, MLA_KV_RANK, MLA_HEADS * HEAD_PAD).astype(BF16)
    w_v_r = w_ukv4[..., MLA_NOPE:].reshape(DEPTH, MLA_KV_RANK, MLA_HEADS * MLA_V).astype(BF16)
    q_norm_p = _pad_last(mla_q_norm, HEAD_PAD).reshape(DEPTH, 1, HEAD_PAD)
    k_norm_p = _pad_last(mla_k_norm, HEAD_PAD).reshape(DEPTH, 1, HEAD_PAD)
    b_gates_p = jnp.pad(b_mlstm_gates, ((0, 0), (GATE_LANE0, LANES - GATE_LANE0 - 4 * ML_HEADS))).reshape(DEPTH, 1, LANES)
    b_sp = _pad_last(jnp.swapaxes(b_spatial, 1, 2), LANES)
    w_router_p = _pad_last(w_router, LANES)
    w_router_hi = w_router_p.astype(BF16)
    w_router_p = jnp.concatenate([w_router_hi, (w_router_p - w_router_hi.astype(F32)).astype(BF16)], axis=-1)
    b_router_p = jnp.pad(b_router, ((0, 0), (0, LANES - N_EXPERTS)), constant_values=-1e30).reshape(DEPTH, 1, LANES)
    r3 = lambda a: a.reshape(DEPTH, 1, a.shape[-1])
    cache_kr_pad = jnp.pad(cache_mla_krope, ((0, 0), (0, 0), (0, 0), (MLA_NOPE, LANES - MLA_QK)))
    rope_tabs = _rope_tables()

    cvec = jnp.concatenate([c_ctx[None, :], c, jnp.zeros((SUBLANES - 1 - N_SEQ_SMP, D_MODEL), F32)], axis=0)
    mod = _adaln(cvec, w_ada, b_ada).reshape(DEPTH, SUBLANES, 1, 6 * D_MODEL)
    b1 = b_exp1.reshape(DEPTH, N_EXPERTS, 1, 2 * D_EXPERT)
    b2 = b_exp2.reshape(DEPTH, N_EXPERTS, 1, D_MODEL)
    kk_cache, v_cache = _cache_kv(cache_mla_ckv, cache_kr_pad, w_k_r, w_v_r, k_norm_p)

    x = jnp.concatenate([x_prompt.reshape(N_CTX, D_MODEL), x_sample.reshape(N_SMP, D_MODEL)], axis=0)
    ckv_l, krope_l, c_l, n_l, m_l = [], [], [], [], []
    for l in range(DEPTH):
        za, zs, zc, zb = _inproj(l, x, r3(norm1), mod, w_in_r)
        oa, c_fin, n_fin, m_fin = _mlstm(l, za, zs, b_gates_p, r3(mlstm_norm))
        (oa,) = _mlstm(l, za, zs, b_gates_p, r3(mlstm_norm), init=(state_mlstm_C, state_mlstm_n, state_mlstm_m), ctx_out=oa)
        q, kk, v, ckvn = _mla_prep(l, zs, r3(mla_q_a_norm), r3(mla_kv_a_norm), w_uq_r, w_k_r, w_v_r, q_norm_p, k_norm_p, rope_tabs)
        ob = _attn_smp(l, q, kk, v, kk_cache, v_cache, _attn_ctx(q, kk, v))
        xmid, h2, dest, wsel, cnt = _merge(
            l, x, oa, ob, zc, zb, r3(sg_norm), w_spatial.astype(BF16), b_sp, w_branch.astype(BF16), w_out.astype(BF16),
            mod, r3(norm2), w_router_p, b_router_p)
        x = _moe(l, xmid, h2, dest, wsel, cnt, mod, w_exp1, b1, w_exp2, b2)
        ckv_l.append(ckvn[:N_CTX].reshape(N_SEQ_CTX, S_CTX, MLA_KV_RANK))
        krope_l.append(zs[:N_CTX, ZS_W - LANES:ZS_W - LANES + MLA_ROPE].reshape(N_SEQ_CTX, S_CTX, MLA_ROPE))
        c_l.append(c_fin)
        n_l.append(n_fin)
        m_l.append(m_fin[:, :, 0].reshape(N_SEQ_CTX, 2, ML_HEADS))
    return (
        x[:N_CTX].reshape(N_SEQ_CTX, S_CTX, D_MODEL),
        x[N_CTX:].reshape(N_SEQ_SMP, S_SMP, D_MODEL),
        jnp.stack(ckv_l, axis=1),
        jnp.stack(krope_l, axis=1),
        jnp.stack(c_l, axis=1),
        jnp.stack(n_l, axis=1),
        jnp.stack(m_l, axis=1),
    )
```

```python
import functools

import numpy as np
import jax
import jax.numpy as jnp
from jax import lax
from jax.experimental import pallas as pl
from jax.experimental.pallas import tpu as pltpu
from jax.experimental.pallas import tpu_sc as plsc

F32 = jnp.float32
BF16 = jnp.bfloat16
HI = lax.Precision.HIGHEST

D_MODEL = 1024
N_SEQ_CTX, S_CTX = 32, 256
N_SEQ_SMP, S_SMP = 2, 1024
DEPTH = 4
PAST_LEN = 512
GRID_W = 64
EPS = 1e-6
ML_HEADS, ML_DIM = 4, 128
ML_WIDTH = ML_HEADS * ML_DIM
MLA_HEADS, MLA_NOPE, MLA_ROPE, MLA_V = 8, 64, 32, 64
MLA_QK = MLA_NOPE + MLA_ROPE
MLA_Q_RANK, MLA_KV_RANK = 256, 128
ROPE_THETA = 10000.0
SG_GROUPS, SG_DIM, SG_CHUNK = 4, 128, 128
SG_WIDTH = SG_GROUPS * SG_DIM
N_BRANCH = 3
N_EXPERTS, TOP_K, D_EXPERT = 32, 4, 1024
SWIGLU_LIMIT, SWIGLU_ALPHA = 7.0, 1.702

N_CTX = N_SEQ_CTX * S_CTX
N_SMP = N_SEQ_SMP * S_SMP
N_TOK = N_CTX + N_SMP

LANES = 128
SUBLANES = 8
VMEM_LIMIT = 56 * 1024 * 1024

TW = 256
N_TILES = N_TOK // TW
N_TILES_CTX = N_CTX // TW
TILES_PER_SMP_SEQ = S_SMP // TW
HEAD_PAD = LANES
TQ = 256
EXPERT_CAP = N_TOK
SLOT_CHUNK = 256
N_CHUNK_STEPS = N_TOK * TOP_K // SLOT_CHUNK + N_EXPERTS
SC_ROW = 256
SC_SPLIT = D_MODEL // (2 * SC_ROW)
SC_WIN = 128

ZA_W = 4 * ML_WIDTH
ZS_W = 512
ZC_W = 2 * SG_WIDTH
ZB_W = N_BRANCH * D_MODEL
ZIN_W = ZA_W + ZS_W + ZC_W + ZB_W
GATE_LANE0 = MLA_ROPE


def _cparams(sem):
    return pltpu.CompilerParams(dimension_semantics=sem, vmem_limit_bytes=VMEM_LIMIT)


def _mod_row(i):
    return jnp.where(i < N_TILES_CTX, 0, 1 + (i - N_TILES_CTX) // TILES_PER_SMP_SEQ)


def _rms(x, g, n=None):
    ms = jnp.sum(x * x, axis=-1, keepdims=True) * (1.0 / (n or x.shape[-1]))
    return x * lax.rsqrt(ms + EPS) * g


def _gelu(x):
    return 0.5 * x * (1.0 + jnp.tanh(0.7978845608028654 * (x + 0.044715 * (x * x * x))))


def _pack_pairs(lo, hi):
    lo_bits = lax.bitcast_convert_type(lo.astype(BF16).astype(F32), jnp.uint32)
    hi_bits = lax.bitcast_convert_type(hi.astype(BF16).astype(F32), jnp.uint32)
    return (lo_bits >> 16) | (hi_bits & jnp.uint32(0xFFFF0000))


def _unpack_pairs(u):
    return (lax.bitcast_convert_type(u << 16, F32), lax.bitcast_convert_type(u & jnp.uint32(0xFFFF0000), F32))


def _dot(a, b):
    return jnp.dot(a, b, preferred_element_type=F32)


def _dot_nt(a, b):
    return lax.dot_general(a, b, (((1,), (1,)), ((), ())), preferred_element_type=F32)


def _adaln_kernel(c_ref, w_ref, b_ref, o_ref):
    c = c_ref[...]
    s = c * jax.nn.sigmoid(c)
    o_ref[0] = jnp.dot(s, w_ref[0], precision=HI, preferred_element_type=F32) + b_ref[0]


def _adaln(cvec, w_ada, b_ada):
    nchunk = 4
    cw = 6 * D_MODEL // nchunk
    return pl.pallas_call(
        _adaln_kernel,
        grid=(DEPTH, nchunk),
        in_specs=[
            pl.BlockSpec((SUBLANES, D_MODEL), lambda l, j: (0, 0)),
            pl.BlockSpec((1, D_MODEL, cw), lambda l, j: (l, 0, j)),
            pl.BlockSpec((1, 1, cw), lambda l, j: (l, 0, j)),
        ],
        out_specs=pl.BlockSpec((1, SUBLANES, cw), lambda l, j: (l, 0, j)),
        out_shape=jax.ShapeDtypeStruct((DEPTH, SUBLANES, 6 * D_MODEL), F32),
        compiler_params=_cparams(("arbitrary", "arbitrary")),
        name="adaln",
    )(cvec, w_ada, b_ada.reshape(DEPTH, 1, 6 * D_MODEL))


IN_SPLITS = (ML_WIDTH, ML_WIDTH, ML_WIDTH, ML_WIDTH, 4 * ML_HEADS, MLA_Q_RANK, MLA_KV_RANK, MLA_ROPE, SG_WIDTH, SG_WIDTH,
             N_BRANCH * D_MODEL)
IN_OFFS = tuple(int(v) for v in np.cumsum((0,) + IN_SPLITS))
D_IN = IN_OFFS[-1]
W_PREP_ROWS = 256


def _w_in_prep_kernel(w_ref, o_ref):
    w = w_ref[0]
    o = IN_OFFS
    o_ref[0, :, 0:o[1]] = w[:, 0:o[1]].astype(BF16)
    o_ref[0, :, o[1]:o[2]] = (w[:, o[1]:o[2]] * (ML_DIM ** -0.5)).astype(BF16)
    o_ref[0, :, o[2]:o[4]] = w[:, o[2]:o[4]].astype(BF16)
    small = jnp.concatenate(
        [w[:, o[5]:o[8]], w[:, o[4]:o[5]], jnp.zeros((W_PREP_ROWS, ZS_W - (o[8] - o[4])), F32)], axis=1)
    o_ref[0, :, ZA_W:ZA_W + ZS_W] = small.astype(BF16)
    o_ref[0, :, ZA_W + ZS_W:ZIN_W] = w[:, o[8]:o[11]].astype(BF16)


def _w_in_prep(w_in):
    return pl.pallas_call(
        _w_in_prep_kernel,
        grid=(DEPTH, D_MODEL // W_PREP_ROWS),
        in_specs=[pl.BlockSpec((1, W_PREP_ROWS, D_IN), lambda l, r: (l, r, 0))],
        out_specs=pl.BlockSpec((1, W_PREP_ROWS, ZIN_W), lambda l, r: (l, r, 0)),
        out_shape=jax.ShapeDtypeStruct((DEPTH, D_MODEL, ZIN_W), BF16),
        compiler_params=_cparams(("arbitrary", "arbitrary")),
        name="w_in_prep",
    )(w_in)


def _inproj_kernel(x_ref, g_ref, sh_ref, sc_ref, w_ref, za_ref, zs_ref, zc_ref, zb_ref):
    h = _rms(x_ref[...], g_ref[0]) * (1.0 + sc_ref[0, 0]) + sh_ref[0, 0]
    hb = h.astype(BF16)
    za_ref[...] = _dot(hb, w_ref[0, :, 0:ZA_W]).astype(BF16)
    zs_ref[...] = _dot(hb, w_ref[0, :, ZA_W:ZA_W + ZS_W])
    zc_ref[...] = _dot(hb, w_ref[0, :, ZA_W + ZS_W:ZA_W + ZS_W + ZC_W]).astype(BF16)
    zb_ref[...] = _dot(hb, w_ref[0, :, ZA_W + ZS_W + ZC_W:ZIN_W]).astype(BF16)


def _mod_spec(layer, k):
    return pl.BlockSpec((1, 1, 1, D_MODEL), lambda i: (layer, _mod_row(i), 0, k))


def _inproj(layer, x, norm1, mod, w_in_r):
    tok = lambda w: pl.BlockSpec((TW, w), lambda i: (i, 0))
    return pl.pallas_call(
        _inproj_kernel,
        grid=(N_TILES,),
        in_specs=[
            tok(D_MODEL),
            pl.BlockSpec((1, 1, D_MODEL), lambda i: (layer, 0, 0)),
            _mod_spec(layer, 0),
            _mod_spec(layer, 1),
            pl.BlockSpec((1, D_MODEL, ZIN_W), lambda i: (layer, 0, 0)),
        ],
        out_specs=[tok(ZA_W), tok(ZS_W), tok(ZC_W), tok(ZB_W)],
        out_shape=[
            jax.ShapeDtypeStruct((N_TOK, ZA_W), BF16),
            jax.ShapeDtypeStruct((N_TOK, ZS_W), F32),
            jax.ShapeDtypeStruct((N_TOK, ZC_W), BF16),
            jax.ShapeDtypeStruct((N_TOK, ZB_W), BF16),
        ],
        compiler_params=_cparams(("arbitrary",)),
        name="inproj",
    )(x, norm1, mod, mod, w_in_r)


def _make_mlstm_kernel(seq, layer, has_init):
    nq = seq // TQ
    lane_if, lane_ff, lane_ib, lane_fb = (GATE_LANE0 + ML_HEADS * j for j in range(4))

    def kern(*refs):
        if has_init:
            body(*refs)
            return
        out = refs[10]
        b = pl.program_id(0)

        @pl.when(b < N_SEQ_CTX)
        def _():
            body(*refs)

        @pl.when(b >= N_SEQ_CTX)
        def _():
            out[...] = jnp.zeros_like(out)

    def body(*refs):
        if has_init:
            m0_ref, zq, zk, zv, zo, gz, bg, nrm, c0_ref, n0_ref, _, out, bp_scr, bs_scr = refs
        else:
            zq, zk, zv, zo, gz, bg, nrm, _, _, _, out, cf_ref, nf_ref, mf_ref, bp_scr, bs_scr = refs
        b = pl.program_id(0)
        g = gz[...] + bg[0]
        lane = lax.broadcasted_iota(jnp.int32, g.shape, 1)
        is_forget = ((lane >= lane_ff) & (lane < lane_ib)) | ((lane >= lane_fb) & (lane < lane_fb + ML_HEADS))
        log_sig = jnp.minimum(g, 0.0) - jnp.log1p(jnp.exp(-jnp.abs(g)))
        a = jnp.where(is_forget, log_sig, g)
        r_i = lax.broadcasted_iota(jnp.int32, (seq, seq), 0)
        c_i = lax.broadcasted_iota(jnp.int32, (seq, seq), 1)
        ltri = (c_i <= r_i).astype(F32)
        bp = jnp.dot(ltri, a, precision=HI, preferred_element_type=F32)
        bs = bp[seq - 1:seq, :] - bp + a
        bp_scr[...] = bp
        bs_scr[...] = bs
        a_t, bp_t, bs_t = a.T, bp.T, bs.T

        for h in range(ML_HEADS):
            hs = slice(h * ML_DIM, (h + 1) * ML_DIM)
            k = zk[:, hs]
            v = zv[:, hs]
            rows = (
                a_t[lane_if + h:lane_if + h + 1, :] - bp_t[lane_ff + h:lane_ff + h + 1, :],
                a_t[lane_ib + h:lane_ib + h + 1, :] - bs_t[lane_fb + h:lane_fb + h + 1, :],
            )
            col_refs = ((bp_scr, lane_ff + h), (bs_scr, lane_fb + h))
            if has_init:
                m0 = tuple(m0_ref[((b * DEPTH + layer) * 2 + dr) * ML_HEADS + h] for dr in range(2))
                c0 = tuple(c0_ref[0, 0, dr, h].astype(BF16) for dr in range(2))
                n0 = tuple(n0_ref[0, 0, dr, h:h + 1, :] for dr in range(2))
            else:
                m0 = (0.0, 0.0)

            def qblock(qi, carry):
                q0 = pl.multiple_of(qi * TQ, TQ)
                qb = zq[pl.ds(q0, TQ), hs]
                sc = _dot_nt(qb, k)
                t_idx = q0 + lax.broadcasted_iota(jnp.int32, (TQ, seq), 0)
                s_idx = lax.broadcasted_iota(jnp.int32, (TQ, seq), 1)
                hsum = jnp.zeros((TQ, ML_DIM), F32)
                for dr in range(2):
                    cref, cl = col_refs[dr]
                    col = cref[pl.ds(q0, TQ), cl:cl + 1]
                    mask = (s_idx <= t_idx) if dr == 0 else (s_idx >= t_idx)
                    d = jnp.where(mask, col + rows[dr], -jnp.inf)
                    a0 = col + m0[dr]
                    m_t = jnp.maximum(a0, jnp.max(d, axis=1, keepdims=True))
                    s = sc * jnp.exp(d - m_t)
                    num = _dot(s.astype(BF16), v)
                    den = jnp.sum(s, axis=1, keepdims=True)
                    if has_init:
                        w_c = jnp.exp(a0 - m_t)
                        num = num + w_c * _dot(qb, c0[dr])
                        den = den + w_c * jnp.sum(qb.astype(F32) * n0[dr], axis=1, keepdims=True)
                    hsum = hsum + num / jnp.maximum(jnp.abs(den), jnp.exp(-m_t))
                hn = _rms(hsum, nrm[0][:, hs])
                og = zo[pl.ds(q0, TQ), hs].astype(F32)
                out[pl.ds(q0, TQ), hs] = (hn * jax.nn.sigmoid(og)).astype(out.dtype)
                return carry

            if nq == 1:
                qblock(0, 0)
            else:
                lax.fori_loop(0, nq, qblock, 0)

            if not has_init:
                k_t = k.astype(F32).T
                kf = k.astype(F32)
                tot = (bp_t[lane_ff + h:lane_ff + h + 1, seq - 1:seq], bp_t[lane_fb + h:lane_fb + h + 1, seq - 1:seq])
                gl = (
                    tot[0] + rows[0],
                    bp_t[lane_fb + h:lane_fb + h + 1, :] - a_t[lane_fb + h:lane_fb + h + 1, :]
                    + a_t[lane_ib + h:lane_ib + h + 1, :],
                )
                for dr in range(2):
                    m_new = jnp.maximum(tot[dr] + m0[dr], jnp.max(gl[dr], axis=1, keepdims=True))
                    w_s = jnp.exp(gl[dr] - m_new)
                    cf_ref[0, 0, dr, h] = _dot((k_t * w_s).astype(BF16), v)
                    n_new = jnp.dot(jnp.broadcast_to(w_s, (SUBLANES, seq)), kf, precision=HI, preferred_element_type=F32)
                    nf_ref[0, 0, dr, h:h + 1, :] = n_new[0:1, :]
                    mf_ref[0, 0, dr * ML_HEADS + h:dr * ML_HEADS + h + 1, :] = jnp.broadcast_to(m_new, (1, LANES))

    return kern


def _mlstm(layer, za, zs, b_gates, mlstm_norm, init=None, ctx_out=None, states=None):
    has_init = init is not None
    seq, nseq, row0 = (S_SMP, N_SEQ_SMP, N_CTX // S_SMP) if has_init else (S_CTX, N_SEQ_CTX, 0)
    qkvo = [pl.BlockSpec((seq, ML_WIDTH), functools.partial(lambda j, b: (row0 + b, j), j)) for j in range(4)]
    in_specs = qkvo + [
        pl.BlockSpec((seq, LANES), lambda b: (row0 + b, ZS_W // LANES - 1)),
        pl.BlockSpec((1, 1, LANES), lambda b: (layer, 0, 0)),
        pl.BlockSpec((1, 1, ML_WIDTH), lambda b: (layer, 0, 0)),
    ]
    args = [za, za, za, za, zs, b_gates, mlstm_norm]
    out_specs = [pl.BlockSpec((seq, ML_WIDTH), lambda b: (row0 + b, 0))]
    out_shape = [jax.ShapeDtypeStruct((N_TOK, ML_WIDTH), BF16)]
    aliases = {}
    if has_init:
        st_c, st_n, st_m = init
        in_specs = [pl.BlockSpec(memory_space=pltpu.SMEM)] + in_specs + [
            pl.BlockSpec((1, 1, 2, ML_HEADS, ML_DIM, ML_DIM), lambda b: (b, layer, 0, 0, 0, 0)),
            pl.BlockSpec((1, 1, 2, ML_HEADS, ML_DIM), lambda b: (b, layer, 0, 0, 0)),
            pl.BlockSpec(memory_space=pl.ANY),
        ]
        args = [st_m.reshape(-1)] + args + [st_c, st_n, ctx_out]
        aliases = {len(args) - 1: 0}
    else:
        seq_blk = lambda b: jnp.minimum(b, nseq - 1)
        in_specs += [pl.BlockSpec(memory_space=pl.ANY)] * 3
        args += list(states)
        aliases = {len(args) - 3 + j: 1 + j for j in range(3)}
        out_specs += [
            pl.BlockSpec((1, 1, 2, ML_HEADS, ML_DIM, ML_DIM), lambda b: (seq_blk(b), layer, 0, 0, 0, 0)),
            pl.BlockSpec((1, 1, 2, ML_HEADS, ML_DIM), lambda b: (seq_blk(b), layer, 0, 0, 0)),
            pl.BlockSpec((1, 1, 2 * ML_HEADS, LANES), lambda b: (seq_blk(b), layer, 0, 0)),
        ]
        out_shape += [jax.ShapeDtypeStruct(s.shape, s.dtype) for s in states]
    return pl.pallas_call(
        _make_mlstm_kernel(seq, layer, has_init),
        grid=(nseq if has_init else N_TILES,),
        in_specs=in_specs,
        out_specs=out_specs,
        out_shape=out_shape,
        scratch_shapes=[pltpu.VMEM((seq, LANES), F32), pltpu.VMEM((seq, LANES), F32)],
        input_output_aliases=aliases,
        compiler_params=_cparams(("arbitrary",)),
        name="mlstm_smp" if has_init else "mlstm_ctx",
    )(*args)


def _rope(x, cos, sin_a, sin_b):
    return x * cos + pltpu.roll(x, LANES - 8, 1) * sin_a + pltpu.roll(x, 8, 1) * sin_b


def _mla_prep_kernel(zs_ref, qa_ref, kva_ref, wuq_ref, wk_ref, wv_ref, qn_ref, kn_ref, cos_ref, sa_ref, sb_ref, _, __,
                     q_ref, kk_ref, v_ref, ckv_ref, kr_ref, qf_scr, kf_scr):
    cq = zs_ref[:, 0:MLA_Q_RANK]
    ckv = zs_ref[:, MLA_Q_RANK:MLA_Q_RANK + MLA_KV_RANK]
    last = zs_ref[:, ZS_W - LANES:ZS_W]
    qf_scr[...] = _dot(_rms(cq, qa_ref[0]).astype(BF16), wuq_ref[0])
    ckvn = _rms(ckv, kva_ref[0])

    @pl.when(pl.program_id(0) < N_TILES_CTX)
    def _():
        ckv_ref[0, 0] = ckvn
        kr_ref[0, 0] = last[:, 0:MLA_ROPE]

    cb = ckvn.astype(BF16)
    kf_scr[...] = _dot(cb, wk_ref[0])
    v_ref[...] = _dot(cb, wv_ref[0]).astype(BF16)
    lane = lax.broadcasted_iota(jnp.int32, last.shape, 1)
    kr = jnp.where((lane >= MLA_NOPE) & (lane < MLA_QK), pltpu.roll(last, MLA_NOPE, 1), 0.0)
    is_latent = pl.program_id(0) >= N_TILES_CTX

    def heads(rotate):
        for h in range(MLA_HEADS):
            hs = slice(h * HEAD_PAD, (h + 1) * HEAD_PAD)
            q_ref[:, hs] = rotate(_rms(qf_scr[:, hs], qn_ref[0], n=MLA_QK)).astype(BF16)
            kk_ref[:, hs] = rotate(_rms(kf_scr[:, hs] + kr, kn_ref[0], n=MLA_QK)).astype(BF16)

    @pl.when(is_latent)
    def _():
        cos, sa, sb = cos_ref[...], sa_ref[...], sb_ref[...]
        heads(lambda x: _rope(x, cos, sa, sb))

    @pl.when(jnp.logical_not(is_latent))
    def _():
        heads(lambda x: x)


def _mla_prep(layer, zs, q_a_norm, kv_a_norm, w_uq_r, w_k_r, w_v_r, q_norm_p, k_norm_p, rope_tabs, new_ckv, new_kr):
    seq_blk = lambda w: pl.BlockSpec((1, 1, S_CTX, w), lambda i: (jnp.minimum(i, N_SEQ_CTX - 1), layer, 0, 0))
    lw = lambda shape: pl.BlockSpec((1,) + shape, lambda i: (layer,) + (0,) * len(shape))
    tab = pl.BlockSpec((TW, LANES), lambda i: (jnp.where(i < N_TILES_CTX, 0, 1 + (i - N_TILES_CTX) % TILES_PER_SMP_SEQ), 0))
    tok = lambda w: pl.BlockSpec((TW, w), lambda i: (i, 0))
    return pl.pallas_call(
        _mla_prep_kernel,
        grid=(N_TILES,),
        in_specs=[
            tok(ZS_W), lw((1, MLA_Q_RANK)), lw((1, MLA_KV_RANK)),
            lw((MLA_Q_RANK, MLA_HEADS * HEAD_PAD)), lw((MLA_KV_RANK, MLA_HEADS * HEAD_PAD)),
            lw((MLA_KV_RANK, MLA_HEADS * MLA_V)), lw((1, HEAD_PAD)), lw((1, HEAD_PAD)), tab, tab, tab,
            pl.BlockSpec(memory_space=pl.ANY), pl.BlockSpec(memory_space=pl.ANY),
        ],
        out_specs=[tok(MLA_HEADS * HEAD_PAD), tok(MLA_HEADS * HEAD_PAD), tok(MLA_HEADS * MLA_V),
                   seq_blk(MLA_KV_RANK), seq_blk(MLA_ROPE)],
        input_output_aliases={11: 3, 12: 4},
        out_shape=[
            jax.ShapeDtypeStruct((N_TOK, MLA_HEADS * HEAD_PAD), BF16),
            jax.ShapeDtypeStruct((N_TOK, MLA_HEADS * HEAD_PAD), BF16),
            jax.ShapeDtypeStruct((N_TOK, MLA_HEADS * MLA_V), BF16),
            jax.ShapeDtypeStruct(new_ckv.shape, F32),
            jax.ShapeDtypeStruct(new_kr.shape, F32),
        ],
        scratch_shapes=[pltpu.VMEM((TW, MLA_HEADS * HEAD_PAD), F32), pltpu.VMEM((TW, MLA_HEADS * HEAD_PAD), F32)],
        compiler_params=_cparams(("arbitrary",)),
        name="mla_prep",
    )(zs, q_a_norm, kv_a_norm, w_uq_r, w_k_r, w_v_r, q_norm_p, k_norm_p, *rope_tabs, new_ckv, new_kr)


def _cache_kv_kernel(ckv_ref, kr_ref, wk_ref, wv_ref, kn_ref, kk_ref, v_ref):
    cb = ckv_ref[...].astype(BF16)
    kf = _dot(cb, wk_ref[0])
    v_ref[...] = _dot(cb, wv_ref[0]).astype(BF16)
    kr = kr_ref[...]
    for h in range(MLA_HEADS):
        hs = slice(h * HEAD_PAD, (h + 1) * HEAD_PAD)
        kk_ref[:, hs] = _rms(kf[:, hs] + kr, kn_ref[0], n=MLA_QK).astype(BF16)


def _cache_kv(cache_ckv, cache_kr_pad, w_k_r, w_v_r, k_norm_p):
    lw = lambda shape: pl.BlockSpec((1,) + shape, lambda b, l: (l,) + (0,) * len(shape))
    blk = lambda w: pl.BlockSpec((None, None, PAST_LEN, w), lambda b, l: (b, l, 0, 0))
    return pl.pallas_call(
        _cache_kv_kernel,
        grid=(N_SEQ_SMP, DEPTH),
        in_specs=[blk(MLA_KV_RANK), blk(LANES), lw((MLA_KV_RANK, MLA_HEADS * HEAD_PAD)),
                  lw((MLA_KV_RANK, MLA_HEADS * MLA_V)), lw((1, HEAD_PAD))],
        out_specs=[blk(MLA_HEADS * HEAD_PAD), blk(MLA_HEADS * MLA_V)],
        out_shape=[
            jax.ShapeDtypeStruct((N_SEQ_SMP, DEPTH, PAST_LEN, MLA_HEADS * HEAD_PAD), BF16),
            jax.ShapeDtypeStruct((N_SEQ_SMP, DEPTH, PAST_LEN, MLA_HEADS * MLA_V), BF16),
        ],
        compiler_params=_cparams(("arbitrary", "arbitrary")),
        name="cache_kv",
    )(cache_ckv, cache_kr_pad, w_k_r, w_v_r, k_norm_p)


def _make_attn_kernel(n_src):
    scale = MLA_QK ** -0.5

    def kern(q_ref, *refs):
        o_ref = refs[-1]
        if n_src > 1:
            body(q_ref, *refs)
            return

        @pl.when(pl.program_id(0) < N_SEQ_CTX)
        def _():
            body(q_ref, *refs)

        @pl.when(pl.program_id(0) >= N_SEQ_CTX)
        def _():
            o_ref[...] = jnp.zeros_like(o_ref)

    def body(q_ref, *refs):
        o_ref = refs[-1]
        for h in range(MLA_HEADS):
            hs = slice(h * HEAD_PAD, (h + 1) * HEAD_PAD)
            vs = slice(h * MLA_V, (h + 1) * MLA_V)
            q = q_ref[:, hs]
            ss = [_dot_nt(q, refs[2 * j][:, hs]) * scale for j in range(n_src)]
            m = functools.reduce(jnp.maximum, [jnp.max(s, axis=1, keepdims=True) for s in ss])
            ps = [jnp.exp(s - m) for s in ss]
            l = functools.reduce(jnp.add, [jnp.sum(p, axis=1, keepdims=True) for p in ps])
            o = functools.reduce(jnp.add, [_dot(ps[j].astype(BF16), refs[2 * j + 1][:, vs]) for j in range(n_src)])
            o_ref[:, vs] = (o / l).astype(o_ref.dtype)

    return kern


def _attn_ctx(q, kk, v):
    blk = lambda w: pl.BlockSpec((S_CTX, w), lambda b: (b, 0))
    return pl.pallas_call(
        _make_attn_kernel(1),
        grid=(N_TILES,),
        in_specs=[blk(MLA_HEADS * HEAD_PAD), blk(MLA_HEADS * HEAD_PAD), blk(MLA_HEADS * MLA_V)],
        out_specs=blk(MLA_HEADS * MLA_V),
        out_shape=jax.ShapeDtypeStruct((N_TOK, MLA_HEADS * MLA_V), BF16),
        compiler_params=_cparams(("arbitrary",)),
        name="attn_ctx",
    )(q, kk, v)


def _attn_smp(layer, q, kk, v, kk_cache, v_cache, ctx_out):
    row0 = N_CTX // S_SMP
    nq = S_SMP // TQ
    seqb = lambda w: pl.BlockSpec((S_SMP, w), lambda b, i: (row0 + b, 0))
    cache = lambda w: pl.BlockSpec((None, None, PAST_LEN, w), lambda b, i: (b, layer, 0, 0))
    return pl.pallas_call(
        _make_attn_kernel(2),
        grid=(N_SEQ_SMP, nq),
        in_specs=[
            pl.BlockSpec((TQ, MLA_HEADS * HEAD_PAD), lambda b, i: (N_CTX // TQ + b * nq + i, 0)),
            seqb(MLA_HEADS * HEAD_PAD), seqb(MLA_HEADS * MLA_V),
            cache(MLA_HEADS * HEAD_PAD), cache(MLA_HEADS * MLA_V),
            pl.BlockSpec(memory_space=pl.ANY),
        ],
        out_specs=pl.BlockSpec((TQ, MLA_HEADS * MLA_V), lambda b, i: (N_CTX // TQ + b * nq + i, 0)),
        out_shape=jax.ShapeDtypeStruct((N_TOK, MLA_HEADS * MLA_V), BF16),
        input_output_aliases={5: 0},
        compiler_params=_cparams(("arbitrary", "arbitrary")),
        name="attn_smp",
    )(q, kk, v, kk_cache, v_cache, ctx_out)


def _merge_kernel(x_ref, oa_ref, ob_ref, zc_ref, zb_ref, sgn_ref, ws_ref, bs_ref, wb_ref, wo_ref, g1_ref, n2_ref,
                  sh2_ref, sc2_ref, wr_ref, br_ref,
                  xmid_ref, h2_ref, dest_ref, wsel_ref, cnt_ref, oc_scr, carry_scr):
    i = pl.program_id(0)

    @pl.when(i == 0)
    def _():
        carry_scr[...] = jnp.zeros_like(carry_scr)

    u = _gelu(zc_ref[:, 0:SG_WIDTH].astype(F32))
    vg = _gelu(zc_ref[:, SG_WIDTH:2 * SG_WIDTH].astype(F32))
    for g in range(SG_GROUPS):
        gs = slice(g * SG_DIM, (g + 1) * SG_DIM)
        vn = _rms(vg[:, gs], sgn_ref[0][:, gs]).astype(BF16)
        for c in range(TW // SG_CHUNK):
            cs = slice(c * SG_CHUNK, (c + 1) * SG_CHUNK)
            mixed = _dot(ws_ref[0, g], vn[cs, :]) + bs_ref[0][:, g:g + 1]
            oc_scr[cs, gs] = (u[cs, gs] * mixed).astype(BF16)

    acc = jnp.zeros((TW, D_MODEL), F32)
    for j, src in enumerate((oa_ref, ob_ref, oc_scr)):
        gate = jax.nn.sigmoid(zb_ref[:, j * D_MODEL:(j + 1) * D_MODEL].astype(F32))
        acc = acc + gate * _dot(src[...], wb_ref[0, j])
    xm = x_ref[...] + g1_ref[0, 0] * _dot(acc.astype(BF16), wo_ref[0])
    xmid_ref[...] = xm
    h2 = _rms(xm, n2_ref[0]) * (1.0 + sc2_ref[0, 0]) + sh2_ref[0, 0]
    for c in range(SC_SPLIT):
        h2_ref[c] = _pack_pairs(h2[:, 2 * c * SC_ROW:(2 * c + 1) * SC_ROW], h2[:, (2 * c + 1) * SC_ROW:(2 * c + 2) * SC_ROW])

    h_hi = h2.astype(BF16)
    h_lo = (h2 - h_hi.astype(F32)).astype(BF16)
    p_hi = _dot(h_hi, wr_ref[0])
    logits = p_hi[:, 0:LANES] + p_hi[:, LANES:2 * LANES] + _dot(h_lo, wr_ref[0, :, 0:LANES]) + br_ref[0]
    lane = lax.broadcasted_iota(jnp.int32, logits.shape, 1)
    hits, exps = [], []
    sel = jnp.zeros(logits.shape, F32)
    denom = jnp.zeros((TW, 1), F32)
    top = None
    for _ in range(TOP_K):
        m = jnp.max(logits, axis=1, keepdims=True)
        idx = jnp.min(jnp.where(logits == m, lane, LANES), axis=1, keepdims=True)
        hit = lane == idx
        top = m if top is None else top
        hits.append(hit)
        exps.append(jnp.exp(m - top))
        sel = jnp.where(hit, 1.0, sel)
        denom = denom + exps[-1]
        logits = jnp.where(hit, -jnp.inf, logits)

    r_i = lax.broadcasted_iota(jnp.int32, (TW, TW), 0)
    c_i = lax.broadcasted_iota(jnp.int32, (TW, TW), 1)
    carry = carry_scr[0:1, :]
    rank = _dot(jnp.where(c_i < r_i, 1.0, 0.0).astype(BF16), sel.astype(BF16)) + carry
    new_carry = carry + jnp.sum(sel, axis=0, keepdims=True)
    carry_scr[...] = jnp.broadcast_to(new_carry, (SUBLANES, LANES))
    cnt_ref[...] = jnp.broadcast_to(new_carry, (SUBLANES, LANES))
    slot = rank + lane.astype(F32) * float(EXPERT_CAP)
    dmat = jnp.zeros(logits.shape, F32)
    wmat = jnp.zeros(logits.shape, F32)
    for k in range(TOP_K):
        dk = jnp.sum(jnp.where(hits[k], slot, 0.0), axis=1, keepdims=True)
        dmat = jnp.where(lane == k, dk, dmat)
        wmat = jnp.where(lane == k, exps[k] / denom, wmat)
    dest_ref[...] = dmat.T[0:SUBLANES, :].astype(jnp.int32)
    wsel_ref[...] = wmat


def _merge(layer, x, oa, ob, zc, zb, sg_norm, w_sp, b_sp, w_branch, w_out, mod, norm2, w_router_p, b_router_p):
    lw = lambda shape: pl.BlockSpec((1,) + shape, lambda i: (layer,) + (0,) * len(shape))
    tok = lambda w: pl.BlockSpec((TW, w), lambda i: (i, 0))
    return pl.pallas_call(
        _merge_kernel,
        grid=(N_TILES,),
        in_specs=[
            tok(D_MODEL), tok(ML_WIDTH), tok(MLA_HEADS * MLA_V), tok(ZC_W), tok(ZB_W),
            lw((1, SG_WIDTH)), lw((SG_GROUPS, SG_CHUNK, SG_CHUNK)), lw((SG_CHUNK, LANES)),
            lw((N_BRANCH, ML_WIDTH, D_MODEL)), lw((D_MODEL, D_MODEL)),
            _mod_spec(layer, 2), lw((1, D_MODEL)), _mod_spec(layer, 3), _mod_spec(layer, 4),
            lw((D_MODEL, 2 * LANES)), lw((1, LANES)),
        ],
        out_specs=[tok(D_MODEL), pl.BlockSpec((SC_SPLIT, TW, SC_ROW), lambda i: (0, i, 0)),
                   pl.BlockSpec((SUBLANES, TW), lambda i: (0, i)), tok(LANES),
                   pl.BlockSpec((SUBLANES, LANES), lambda i: (0, 0))],
        out_shape=[
            jax.ShapeDtypeStruct((N_TOK, D_MODEL), F32),
            jax.ShapeDtypeStruct((SC_SPLIT, N_TOK, SC_ROW), jnp.uint32),
            jax.ShapeDtypeStruct((SUBLANES, N_TOK), jnp.int32),
            jax.ShapeDtypeStruct((N_TOK, LANES), F32),
            jax.ShapeDtypeStruct((SUBLANES, LANES), F32),
        ],
        scratch_shapes=[pltpu.VMEM((TW, SG_WIDTH), BF16), pltpu.VMEM((SUBLANES, LANES), F32)],
        compiler_params=_cparams(("arbitrary",)),
        name="merge_router",
    )(x, oa, ob, zc, zb, sg_norm, w_sp, b_sp, w_branch, w_out, mod, norm2, mod, mod, w_router_p, b_router_p)


def _sc_mesh():
    return plsc.VectorSubcoreMesh(core_axis_name="core", subcore_axis_name="subcore")


def _sc_scatter_rows(x, idxs, n_rows):
    @pl.kernel(out_type=jax.ShapeDtypeStruct((n_rows, SC_ROW), x.dtype), mesh=_sc_mesh(), scratch_types=[])
    def scatter(x_hbm, *refs):
        o_hbm = refs[-1]

        def body(x_vmem, *i_vmems):
            for i_vmem in i_vmems:
                pltpu.sync_copy(x_vmem, o_hbm.at[i_vmem.at[0]])

        pltpu.emit_pipeline(
            body,
            grid=(x.shape[0] // SC_WIN,),
            in_specs=[pl.BlockSpec((SC_WIN, SC_ROW), lambda i: (i, 0))]
            + [pl.BlockSpec((1, SC_WIN), lambda i: (0, i))] * len(idxs),
            out_specs=[],
            core_axis_name=("core", "subcore"),
            dimension_semantics=(pltpu.PARALLEL,),
        )(x_hbm, *refs[:-1])

    return scatter(x, *idxs)


def _sc_gather_rows(x, idx):
    m = idx.shape[1]

    @pl.kernel(out_type=jax.ShapeDtypeStruct((m, SC_ROW), x.dtype), mesh=_sc_mesh())
    def gather(x_hbm, i_hbm, o_hbm):
        def body(i_vmem, o_vmem):
            pltpu.sync_copy(x_hbm.at[i_vmem.at[0]], o_vmem)

        pltpu.emit_pipeline(
            body,
            grid=(m // SC_WIN,),
            in_specs=[pl.BlockSpec((1, SC_WIN), lambda i: (0, i))],
            out_specs=[pl.BlockSpec((SC_WIN, SC_ROW), lambda i: (i, 0))],
            core_axis_name=("core", "subcore"),
            dimension_semantics=(pltpu.PARALLEL,),
        )(i_hbm, o_hbm)

    return gather(x, idx)


STEP_VALID, STEP_FIRST, STEP_HAS_NEXT = 1, 2, 4


def _moe_ffn_kernel(layer, be_ref, nx_ref, br_ref, fl_ref, xs_ref, b1_ref, b2_ref, w1_hbm, w2_hbm, y_ref,
                    w1f, w2f, w1b, w2b, sem):
    g = pl.program_id(0)
    flags = fl_ref[g]

    def weight_copies(e):
        return (pltpu.make_async_copy(w1_hbm.at[layer, e], w1f, sem.at[0]),
                pltpu.make_async_copy(w2_hbm.at[layer, e], w2f, sem.at[1]))

    @pl.when(g == 0)
    def _():
        for cp in weight_copies(be_ref[0]):
            cp.start()

    @pl.when((flags & STEP_FIRST) != 0)
    def _():
        for cp in weight_copies(be_ref[g]):
            cp.wait()
        w1b[...] = w1f[...].astype(BF16)
        w2b[...] = w2f[...].astype(BF16)

        @pl.when((flags & STEP_HAS_NEXT) != 0)
        def _():
            for cp in weight_copies(nx_ref[g]):
                cp.start()

    @pl.when((flags & STEP_VALID) != 0)
    def _():
        halves = [h.astype(BF16) for c in range(SC_SPLIT) for h in _unpack_pairs(xs_ref[c])]
        g1 = _dot(jnp.concatenate(halves, axis=1), w1b[...]) + b1_ref[0, 0]
        gate = jnp.minimum(g1[:, :D_EXPERT], SWIGLU_LIMIT)
        up = jnp.clip(g1[:, D_EXPERT:], -SWIGLU_LIMIT, SWIGLU_LIMIT)
        act = gate * jax.nn.sigmoid(SWIGLU_ALPHA * gate) * (up + 1.0)
        y = _dot(act.astype(BF16), w2b[...]) + b2_ref[0, 0]
        for c in range(SC_SPLIT):
            y_ref[c] = _pack_pairs(y[:, 2 * c * SC_ROW:(2 * c + 1) * SC_ROW], y[:, (2 * c + 1) * SC_ROW:(2 * c + 2) * SC_ROW])


def _moe_ffn(layer, xs, plan, w1, b1, w2, b2):
    eb = lambda c: pl.BlockSpec((1, 1, 1, c), lambda g, be, nx, br, fl: (layer, be[g], 0, 0))
    rows = pl.BlockSpec((SC_SPLIT, SLOT_CHUNK, SC_ROW), lambda g, be, nx, br, fl: (0, br[g], 0))
    hbm = pl.BlockSpec(memory_space=pl.ANY)
    grid_spec = pltpu.PrefetchScalarGridSpec(
        num_scalar_prefetch=4,
        grid=(N_CHUNK_STEPS,),
        in_specs=[rows, eb(2 * D_EXPERT), eb(D_MODEL), hbm, hbm],
        out_specs=rows,
        scratch_shapes=[
            pltpu.VMEM((D_MODEL, 2 * D_EXPERT), F32), pltpu.VMEM((D_EXPERT, D_MODEL), F32),
            pltpu.VMEM((D_MODEL, 2 * D_EXPERT), BF16), pltpu.VMEM((D_EXPERT, D_MODEL), BF16),
            pltpu.SemaphoreType.DMA((2,)),
        ],
    )
    return pl.pallas_call(
        functools.partial(_moe_ffn_kernel, layer),
        grid_spec=grid_spec,
        out_shape=jax.ShapeDtypeStruct(xs.shape, xs.dtype),
        compiler_params=_cparams(("arbitrary",)),
        name="moe_ffn",
    )(*plan, xs, b1, b2, w1, w2)


def _chunk_plan(cnt):
    nch = (cnt + SLOT_CHUNK - 1) // SLOT_CHUNK
    cum = jnp.cumsum(nch)
    expert_at = lambda step: jnp.minimum(
        jnp.sum((cum[None, :] <= step[:, None]).astype(jnp.int32), axis=1), N_EXPERTS - 1)
    steps = jnp.arange(N_CHUNK_STEPS, dtype=jnp.int32)
    g = jnp.minimum(steps, cum[-1] - 1)
    e = expert_at(g)
    is_e = jnp.arange(N_EXPERTS, dtype=jnp.int32)[None, :] == e[:, None]
    end_e = jnp.sum(jnp.where(is_e, cum[None, :], 0), axis=1)
    j = g - jnp.sum(jnp.where(is_e, (cum - nch)[None, :], 0), axis=1)
    valid = steps < cum[-1]
    first = valid & (j == 0)
    flags = (STEP_VALID * valid.astype(jnp.int32) + STEP_FIRST * first.astype(jnp.int32)
             + STEP_HAS_NEXT * (first & (end_e < cum[-1])).astype(jnp.int32))
    return e, expert_at(end_e), (e * (EXPERT_CAP // SLOT_CHUNK) + j).astype(jnp.int32), flags


def _combine_kernel(x_ref, yg_ref, w_ref, g_ref, *o_refs):
    def emit(o_ref):
        w = w_ref[...]
        for c in range(SC_SPLIT):
            parts = [_unpack_pairs(yg_ref[k, c]) for k in range(TOP_K)]
            for half in range(2):
                cs = slice((2 * c + half) * SC_ROW, (2 * c + half + 1) * SC_ROW)
                acc = w[:, 0:1] * parts[0][half]
                for k in range(1, TOP_K):
                    acc = acc + w[:, k:k + 1] * parts[k][half]
                o_ref[:, cs] = x_ref[:, cs] + g_ref[0, 0][:, cs] * acc

    if len(o_refs) == 1:
        emit(o_refs[0])
    else:
        pl.when(pl.program_id(0) < N_TILES_CTX)(lambda: emit(o_refs[0]))
        pl.when(pl.program_id(0) >= N_TILES_CTX)(lambda: emit(o_refs[1]))


def _combine(layer, xmid, yg, wsel, mod, split_out):
    tok = lambda w: pl.BlockSpec((TW, w), lambda i: (i, 0))
    if split_out:
        out_specs = [pl.BlockSpec((TW, D_MODEL), lambda i: (jnp.minimum(i, N_TILES_CTX - 1), 0)),
                     pl.BlockSpec((TW, D_MODEL), lambda i: (jnp.maximum(i - N_TILES_CTX, 0), 0))]
        out_shape = [jax.ShapeDtypeStruct((N_CTX, D_MODEL), F32), jax.ShapeDtypeStruct((N_SMP, D_MODEL), F32)]
    else:
        out_specs, out_shape = tok(D_MODEL), jax.ShapeDtypeStruct((N_TOK, D_MODEL), F32)
    return pl.pallas_call(
        _combine_kernel,
        grid=(N_TILES,),
        in_specs=[tok(D_MODEL), pl.BlockSpec((TOP_K, SC_SPLIT, TW, SC_ROW), lambda i: (0, 0, i, 0)), tok(LANES),
                  _mod_spec(layer, 5)],
        out_specs=out_specs,
        out_shape=out_shape,
        compiler_params=_cparams(("arbitrary",)),
        name="combine",
    )(xmid, yg, wsel, mod)


def _moe(layer, xmid, h2, dest, wsel, cnt, mod, w1, b1, w2, b2, split_out):
    n_slots = N_EXPERTS * EXPERT_CAP
    idx = dest[0:TOP_K][:, None, :] + (jnp.arange(SC_SPLIT, dtype=jnp.int32) * n_slots)[None, :, None]
    idx = idx.reshape(TOP_K, 1, SC_SPLIT * N_TOK)
    xs = _sc_scatter_rows(h2.reshape(SC_SPLIT * N_TOK, SC_ROW), [idx[k] for k in range(TOP_K)], SC_SPLIT * n_slots)
    plan = _chunk_plan(cnt[0, :N_EXPERTS].astype(jnp.int32))
    y = _moe_ffn(layer, xs.reshape(SC_SPLIT, n_slots, SC_ROW), plan, w1, b1, w2, b2)
    yg = _sc_gather_rows(y.reshape(SC_SPLIT * n_slots, SC_ROW), idx.reshape(1, TOP_K * SC_SPLIT * N_TOK))
    return _combine(layer, xmid, yg.reshape(TOP_K, SC_SPLIT, N_TOK, SC_ROW), wsel, mod, split_out)


def _rope_tables():
    pos = np.arange(S_SMP)
    half = MLA_ROPE // 2
    inv_freq = (ROPE_THETA ** (-(np.arange(0, half, 2, dtype=np.float32) / np.float32(half)))).astype(np.float32)
    angs = [((pos // GRID_W).astype(np.float32)[:, None] * inv_freq[None, :]).astype(np.float32),
            ((pos % GRID_W).astype(np.float32)[:, None] * inv_freq[None, :]).astype(np.float32)]
    nf = half // 2
    cos = np.ones((TW + S_SMP, LANES), np.float32)
    sin_a = np.zeros((TW + S_SMP, LANES), np.float32)
    sin_b = np.zeros((TW + S_SMP, LANES), np.float32)
    for axis, ang in enumerate(angs):
        base = MLA_NOPE + axis * half
        c, s = np.cos(ang.astype(np.float64)), np.sin(ang.astype(np.float64))
        cos[TW:, base:base + nf] = c
        cos[TW:, base + nf:base + half] = c
        sin_a[TW:, base:base + nf] = -s
        sin_b[TW:, base + nf:base + half] = s
    return jnp.asarray(cos), jnp.asarray(sin_a), jnp.asarray(sin_b)


def _pad_last(a, width):
    return jnp.pad(a, [(0, 0)] * (a.ndim - 1) + [(0, width - a.shape[-1])])


def kernel(x_prompt, x_sample, cache_mla_ckv, cache_mla_krope, state_mlstm_C, state_mlstm_n, state_mlstm_m, c, c_ctx, norm1, norm2, w_ada, b_ada, w_in, b_mlstm_gates, mlstm_norm, mla_q_a_norm, mla_kv_a_norm, w_uq, w_ukv, mla_q_norm, mla_k_norm, sg_norm, w_spatial, b_spatial, w_branch, w_out, w_router, b_router, w_exp1, b_exp1, w_exp2, b_exp2):
    w_in_r = _w_in_prep(w_in)
    w_uq_r = _pad_last(w_uq.reshape(DEPTH, MLA_Q_RANK, MLA_HEADS, MLA_QK), HEAD_PAD).reshape(
        DEPTH, MLA_Q_RANK, MLA_HEADS * HEAD_PAD).astype(BF16)
    w_ukv4 = w_ukv.reshape(DEPTH, MLA_KV_RANK, MLA_HEADS, MLA_NOPE + MLA_V)
    w_k_r = _pad_last(w_ukv4[..., :MLA_NOPE], HEAD_PAD).reshape(DEPTH, MLA_KV_RANK, MLA_HEADS * HEAD_PAD).astype(BF16)
    w_v_r = w_ukv4[..., MLA_NOPE:].reshape(DEPTH, MLA_KV_RANK, MLA_HEADS * MLA_V).astype(BF16)
    q_norm_p = _pad_last(mla_q_norm, HEAD_PAD).reshape(DEPTH, 1, HEAD_PAD)
    k_norm_p = _pad_last(mla_k_norm, HEAD_PAD).reshape(DEPTH, 1, HEAD_PAD)
    b_gates_p = jnp.pad(b_mlstm_gates, ((0, 0), (GATE_LANE0, LANES - GATE_LANE0 - 4 * ML_HEADS))).reshape(DEPTH, 1, LANES)
    b_sp = _pad_last(jnp.swapaxes(b_spatial, 1, 2), LANES)
    w_router_p = _pad_last(w_router, LANES)
    w_router_hi = w_router_p.astype(BF16)
    w_router_p = jnp.concatenate([w_router_hi, (w_router_p - w_router_hi.astype(F32)).astype(BF16)], axis=-1)
    b_router_p = jnp.pad(b_router, ((0, 0), (0, LANES - N_EXPERTS)), constant_values=-1e30).reshape(DEPTH, 1, LANES)
    r3 = lambda a: a.reshape(DEPTH, 1, a.shape[-1])
    cache_kr_pad = jnp.pad(cache_mla_krope, ((0, 0), (0, 0), (0, 0), (MLA_NOPE, LANES - MLA_QK)))
    rope_tabs = _rope_tables()

    cvec = jnp.concatenate([c_ctx[None, :], c, jnp.zeros((SUBLANES - 1 - N_SEQ_SMP, D_MODEL), F32)], axis=0)
    mod = _adaln(cvec, w_ada, b_ada).reshape(DEPTH, SUBLANES, 1, 6 * D_MODEL)
    b1 = b_exp1.reshape(DEPTH, N_EXPERTS, 1, 2 * D_EXPERT)
    b2 = b_exp2.reshape(DEPTH, N_EXPERTS, 1, D_MODEL)
    kk_cache, v_cache = _cache_kv(cache_mla_ckv, cache_kr_pad, w_k_r, w_v_r, k_norm_p)

    x = jnp.concatenate([x_prompt.reshape(N_CTX, D_MODEL), x_sample.reshape(N_SMP, D_MODEL)], axis=0)
    new_ckv = jnp.zeros((N_SEQ_CTX, DEPTH, S_CTX, MLA_KV_RANK), F32)
    new_kr = jnp.zeros((N_SEQ_CTX, DEPTH, S_CTX, MLA_ROPE), F32)
    states = (jnp.zeros((N_SEQ_CTX, DEPTH, 2, ML_HEADS, ML_DIM, ML_DIM), F32),
              jnp.zeros((N_SEQ_CTX, DEPTH, 2, ML_HEADS, ML_DIM), F32),
              jnp.zeros((N_SEQ_CTX, DEPTH, 2 * ML_HEADS, LANES), F32))
    for l in range(DEPTH):
        za, zs, zc, zb = _inproj(l, x, r3(norm1), mod, w_in_r)
        oa, *states = _mlstm(l, za, zs, b_gates_p, r3(mlstm_norm), states=states)
        (oa,) = _mlstm(l, za, zs, b_gates_p, r3(mlstm_norm), init=(state_mlstm_C, state_mlstm_n, state_mlstm_m), ctx_out=oa)
        q, kk, v, new_ckv, new_kr = _mla_prep(l, zs, r3(mla_q_a_norm), r3(mla_kv_a_norm), w_uq_r, w_k_r, w_v_r,
                                              q_norm_p, k_norm_p, rope_tabs, new_ckv, new_kr)
        ob = _attn_smp(l, q, kk, v, kk_cache, v_cache, _attn_ctx(q, kk, v))
        xmid, h2, dest, wsel, cnt = _merge(
            l, x, oa, ob, zc, zb, r3(sg_norm), w_spatial.astype(BF16), b_sp, w_branch.astype(BF16), w_out.astype(BF16),
            mod, r3(norm2), w_router_p, b_router_p)
        x = _moe(l, xmid, h2, dest, wsel, cnt, mod, w_exp1, b1, w_exp2, b2, split_out=l == DEPTH - 1)
    y_ctx, y_smp = x
    return (
        y_ctx.reshape(N_SEQ_CTX, S_CTX, D_MODEL),
        y_smp.reshape(N_SEQ_SMP, S_SMP, D_MODEL),
        new_ckv,
        new_kr,
        states[0],
        states[1],
        states[2][:, :, :, 0].reshape(N_SEQ_CTX, DEPTH, 2, ML_HEADS),
    )
```

```python
import functools

import numpy as np
import jax
import jax.numpy as jnp
from jax import lax
from jax.experimental import pallas as pl
from jax.experimental.pallas import tpu as pltpu
from jax.experimental.pallas import tpu_sc as plsc

F32 = jnp.float32
BF16 = jnp.bfloat16
HI = lax.Precision.HIGHEST

D_MODEL = 1024
N_SEQ_CTX, S_CTX = 32, 256
N_SEQ_SMP, S_SMP = 2, 1024
DEPTH = 4
PAST_LEN = 512
GRID_W = 64
EPS = 1e-6
ML_HEADS, ML_DIM = 4, 128
ML_WIDTH = ML_HEADS * ML_DIM
MLA_HEADS, MLA_NOPE, MLA_ROPE, MLA_V = 8, 64, 32, 64
MLA_QK = MLA_NOPE + MLA_ROPE
MLA_Q_RANK, MLA_KV_RANK = 256, 128
ROPE_THETA = 10000.0
SG_GROUPS, SG_DIM, SG_CHUNK = 4, 128, 128
SG_WIDTH = SG_GROUPS * SG_DIM
N_BRANCH = 3
N_EXPERTS, TOP_K, D_EXPERT = 32, 4, 1024
SWIGLU_LIMIT, SWIGLU_ALPHA = 7.0, 1.702

N_CTX = N_SEQ_CTX * S_CTX
N_SMP = N_SEQ_SMP * S_SMP
N_TOK = N_CTX + N_SMP

LANES = 128
SUBLANES = 8
VMEM_LIMIT = 56 * 1024 * 1024

TW = 256
N_TILES = N_TOK // TW
N_TILES_CTX = N_CTX // TW
TILES_PER_SMP_SEQ = S_SMP // TW
HEAD_PAD = LANES
TQ = 256
EXPERT_CAP = N_TOK
SLOT_CHUNK = 256
FFN_BLOCK = 2 * SLOT_CHUNK
N_CHUNK_STEPS = N_TOK * TOP_K // FFN_BLOCK + N_EXPERTS
SC_ROW = 256
SC_SPLIT = D_MODEL // (2 * SC_ROW)
SC_WIN = 128

ZA_W = 4 * ML_WIDTH
ZS_W = 512
ZC_W = 2 * SG_WIDTH
ZB_W = N_BRANCH * D_MODEL
ZIN_W = ZA_W + ZS_W + ZC_W + ZB_W
GATE_LANE0 = MLA_ROPE


def _cparams(sem):
    return pltpu.CompilerParams(dimension_semantics=sem, vmem_limit_bytes=VMEM_LIMIT)


def _mod_row(i):
    return jnp.where(i < N_TILES_CTX, 0, 1 + (i - N_TILES_CTX) // TILES_PER_SMP_SEQ)


def _rms(x, g, n=None):
    ms = jnp.sum(x * x, axis=-1, keepdims=True) * (1.0 / (n or x.shape[-1]))
    return x * lax.rsqrt(ms + EPS) * g


def _gelu(x):
    return 0.5 * x * (1.0 + jnp.tanh(0.7978845608028654 * (x + 0.044715 * (x * x * x))))


def _pack_pairs(lo, hi):
    lo_bits = lax.bitcast_convert_type(lo.astype(BF16).astype(F32), jnp.uint32)
    hi_bits = lax.bitcast_convert_type(hi.astype(BF16).astype(F32), jnp.uint32)
    return (lo_bits >> 16) | (hi_bits & jnp.uint32(0xFFFF0000))


def _unpack_pairs(u):
    return (lax.bitcast_convert_type(u << 16, F32), lax.bitcast_convert_type(u & jnp.uint32(0xFFFF0000), F32))


def _dot(a, b):
    return jnp.dot(a, b, preferred_element_type=F32)


def _dot_nt(a, b):
    return lax.dot_general(a, b, (((1,), (1,)), ((), ())), preferred_element_type=F32)


def _adaln_kernel(c_ref, w_ref, b_ref, o_ref):
    c = c_ref[...]
    s = c * jax.nn.sigmoid(c)
    o_ref[0] = jnp.dot(s, w_ref[0], precision=HI, preferred_element_type=F32) + b_ref[0]


def _adaln(cvec, w_ada, b_ada):
    nchunk = 4
    cw = 6 * D_MODEL // nchunk
    return pl.pallas_call(
        _adaln_kernel,
        grid=(DEPTH, nchunk),
        in_specs=[
            pl.BlockSpec((SUBLANES, D_MODEL), lambda l, j: (0, 0)),
            pl.BlockSpec((1, D_MODEL, cw), lambda l, j: (l, 0, j)),
            pl.BlockSpec((1, 1, cw), lambda l, j: (l, 0, j)),
        ],
        out_specs=pl.BlockSpec((1, SUBLANES, cw), lambda l, j: (l, 0, j)),
        out_shape=jax.ShapeDtypeStruct((DEPTH, SUBLANES, 6 * D_MODEL), F32),
        compiler_params=_cparams(("arbitrary", "arbitrary")),
        name="adaln",
    )(cvec, w_ada, b_ada.reshape(DEPTH, 1, 6 * D_MODEL))


IN_SPLITS = (ML_WIDTH, ML_WIDTH, ML_WIDTH, ML_WIDTH, 4 * ML_HEADS, MLA_Q_RANK, MLA_KV_RANK, MLA_ROPE, SG_WIDTH, SG_WIDTH,
             N_BRANCH * D_MODEL)
IN_OFFS = tuple(int(v) for v in np.cumsum((0,) + IN_SPLITS))
D_IN = IN_OFFS[-1]
W_PREP_ROWS = 256
W_PREP_COLS = 512


def _w_in_prep_kernel(wt_ref, o_ref):
    o = IN_OFFS

    def put(c0, rows):
        o_ref[0, :, c0:c0 + W_PREP_COLS] = rows.T.astype(BF16)

    for c0 in range(0, ZA_W, W_PREP_COLS):
        scale = ML_DIM ** -0.5 if o[1] <= c0 < o[2] else 1.0
        put(c0, wt_ref[0, c0:c0 + W_PREP_COLS, :] * scale)
    pad = jnp.zeros((ZS_W - (o[8] - o[4]), W_PREP_ROWS), F32)
    put(ZA_W, jnp.concatenate([wt_ref[0, o[5]:o[8], :], wt_ref[0, o[4]:o[5], :], pad], axis=0))
    for c0 in range(ZA_W + ZS_W, ZIN_W, W_PREP_COLS):
        src = c0 - (ZA_W + ZS_W) + o[8]
        put(c0, wt_ref[0, src:src + W_PREP_COLS, :])


def _w_in_prep(w_in):
    return pl.pallas_call(
        _w_in_prep_kernel,
        grid=(DEPTH, D_MODEL // W_PREP_ROWS),
        in_specs=[pl.BlockSpec((1, D_IN, W_PREP_ROWS), lambda l, r: (l, 0, r))],
        out_specs=pl.BlockSpec((1, W_PREP_ROWS, ZIN_W), lambda l, r: (l, r, 0)),
        out_shape=jax.ShapeDtypeStruct((DEPTH, D_MODEL, ZIN_W), BF16),
        compiler_params=_cparams(("arbitrary", "arbitrary")),
        name="w_in_prep",
    )(jnp.swapaxes(w_in, 1, 2))


def _inproj_kernel(x_ref, g_ref, sh_ref, sc_ref, w_ref, za_ref, zs_ref, zc_ref, zb_ref):
    h = _rms(x_ref[...], g_ref[0]) * (1.0 + sc_ref[0, 0]) + sh_ref[0, 0]
    hb = h.astype(BF16)
    za_ref[...] = _dot(hb, w_ref[0, :, 0:ZA_W]).astype(BF16)
    zs_ref[...] = _dot(hb, w_ref[0, :, ZA_W:ZA_W + ZS_W])
    zc_ref[...] = _dot(hb, w_ref[0, :, ZA_W + ZS_W:ZA_W + ZS_W + ZC_W]).astype(BF16)
    zb_ref[...] = _dot(hb, w_ref[0, :, ZA_W + ZS_W + ZC_W:ZIN_W]).astype(BF16)


def _mod_spec(layer, k):
    return pl.BlockSpec((1, 1, 1, D_MODEL), lambda i: (layer, _mod_row(i), 0, k))


def _inproj(layer, x, norm1, mod, w_in_r):
    tok = lambda w: pl.BlockSpec((TW, w), lambda i: (i, 0))
    return pl.pallas_call(
        _inproj_kernel,
        grid=(N_TILES,),
        in_specs=[
            tok(D_MODEL),
            pl.BlockSpec((1, 1, D_MODEL), lambda i: (layer, 0, 0)),
            _mod_spec(layer, 0),
            _mod_spec(layer, 1),
            pl.BlockSpec((1, D_MODEL, ZIN_W), lambda i: (layer, 0, 0)),
        ],
        out_specs=[tok(ZA_W), tok(ZS_W), tok(ZC_W), tok(ZB_W)],
        out_shape=[
            jax.ShapeDtypeStruct((N_TOK, ZA_W), BF16),
            jax.ShapeDtypeStruct((N_TOK, ZS_W), F32),
            jax.ShapeDtypeStruct((N_TOK, ZC_W), BF16),
            jax.ShapeDtypeStruct((N_TOK, ZB_W), BF16),
        ],
        compiler_params=_cparams(("arbitrary",)),
        name="inproj",
    )(x, norm1, mod, mod, w_in_r)


def _make_mlstm_kernel(seq, layer, has_init):
    nq = seq // TQ
    lane_if, lane_ff, lane_ib, lane_fb = (GATE_LANE0 + ML_HEADS * j for j in range(4))

    def kern(*refs):
        if has_init:
            body(*refs)
            return
        out = refs[10]
        b = pl.program_id(0)

        @pl.when(b < N_SEQ_CTX)
        def _():
            body(*refs)

        @pl.when(b >= N_SEQ_CTX)
        def _():
            out[...] = jnp.zeros_like(out)

    def body(*refs):
        if has_init:
            m0_ref, zq, zk, zv, zo, gz, bg, nrm, c0_ref, n0_ref, _, out, bp_scr, bs_scr = refs
        else:
            zq, zk, zv, zo, gz, bg, nrm, _, _, _, out, cf_ref, nf_ref, mf_ref, bp_scr, bs_scr = refs
        b = pl.program_id(0)
        g = gz[...] + bg[0]
        lane = lax.broadcasted_iota(jnp.int32, g.shape, 1)
        is_forget = ((lane >= lane_ff) & (lane < lane_ib)) | ((lane >= lane_fb) & (lane < lane_fb + ML_HEADS))
        log_sig = jnp.minimum(g, 0.0) - jnp.log1p(jnp.exp(-jnp.abs(g)))
        a = jnp.where(is_forget, log_sig, g)
        r_i = lax.broadcasted_iota(jnp.int32, (seq, seq), 0)
        c_i = lax.broadcasted_iota(jnp.int32, (seq, seq), 1)
        ltri = (c_i <= r_i).astype(F32)
        bp = jnp.dot(ltri, a, precision=HI, preferred_element_type=F32)
        bs = bp[seq - 1:seq, :] - bp + a
        bp_scr[...] = bp
        bs_scr[...] = bs
        eye = (lax.broadcasted_iota(jnp.int32, (LANES, LANES), 0)
               == lax.broadcasted_iota(jnp.int32, (LANES, LANES), 1)).astype(F32)
        tr = lambda x: lax.dot_general(eye, x, (((1,), (1,)), ((), ())), precision=HI, preferred_element_type=F32)
        a_t, bp_t, bs_t = tr(a), tr(bp), tr(bs)

        for h in range(ML_HEADS):
            hs = slice(h * ML_DIM, (h + 1) * ML_DIM)
            k = zk[:, hs]
            v = zv[:, hs]
            rows = (
                a_t[lane_if + h:lane_if + h + 1, :] - bp_t[lane_ff + h:lane_ff + h + 1, :],
                a_t[lane_ib + h:lane_ib + h + 1, :] - bs_t[lane_fb + h:lane_fb + h + 1, :],
            )
            col_refs = ((bp_scr, lane_ff + h), (bs_scr, lane_fb + h))
            if has_init:
                m0 = tuple(m0_ref[((b * DEPTH + layer) * 2 + dr) * ML_HEADS + h] for dr in range(2))
                c0 = tuple(c0_ref[0, 0, dr, h].astype(BF16) for dr in range(2))
                n0 = tuple(n0_ref[0, 0, dr, h:h + 1, :] for dr in range(2))
            else:
                m0 = (0.0, 0.0)

            def qblock(qi, carry):
                q0 = pl.multiple_of(qi * TQ, TQ)
                qb = zq[pl.ds(q0, TQ), hs]
                sc = _dot_nt(qb, k)
                t_idx = q0 + lax.broadcasted_iota(jnp.int32, (TQ, seq), 0)
                s_idx = lax.broadcasted_iota(jnp.int32, (TQ, seq), 1)
                hsum = jnp.zeros((TQ, ML_DIM), F32)
                for dr in range(2):
                    cref, cl = col_refs[dr]
                    col = cref[pl.ds(q0, TQ), cl:cl + 1]
                    mask = (s_idx <= t_idx) if dr == 0 else (s_idx >= t_idx)
                    d = jnp.where(mask, col + rows[dr], -jnp.inf)
                    a0 = col + m0[dr]
                    m_t = jnp.maximum(a0, jnp.max(d, axis=1, keepdims=True))
                    s = sc * jnp.exp(d - m_t)
                    num = _dot(s.astype(BF16), v)
                    den = jnp.sum(s, axis=1, keepdims=True)
                    if has_init:
                        w_c = jnp.exp(a0 - m_t)
                        num = num + w_c * _dot(qb, c0[dr])
                        den = den + w_c * jnp.sum(qb.astype(F32) * n0[dr], axis=1, keepdims=True)
                    hsum = hsum + num / jnp.maximum(jnp.abs(den), jnp.exp(-m_t))
                hn = _rms(hsum, nrm[0][:, hs])
                og = zo[pl.ds(q0, TQ), hs].astype(F32)
                out[pl.ds(q0, TQ), hs] = (hn * jax.nn.sigmoid(og)).astype(out.dtype)
                return carry

            if nq == 1:
                qblock(0, 0)
            else:
                lax.fori_loop(0, nq, qblock, 0)

            if not has_init:
                k_t = _dot_nt(eye.astype(BF16), k)
                kf = k.astype(F32)
                tot = (bp_t[lane_ff + h:lane_ff + h + 1, seq - 1:seq], bp_t[lane_fb + h:lane_fb + h + 1, seq - 1:seq])
                gl = (
                    tot[0] + rows[0],
                    bp_t[lane_fb + h:lane_fb + h + 1, :] - a_t[lane_fb + h:lane_fb + h + 1, :]
                    + a_t[lane_ib + h:lane_ib + h + 1, :],
                )
                for dr in range(2):
                    m_new = jnp.maximum(tot[dr] + m0[dr], jnp.max(gl[dr], axis=1, keepdims=True))
                    w_s = jnp.exp(gl[dr] - m_new)
                    cf_ref[0, 0, dr, h] = _dot((k_t * w_s).astype(BF16), v)
                    n_new = jnp.dot(jnp.broadcast_to(w_s, (SUBLANES, seq)), kf, precision=HI, preferred_element_type=F32)
                    nf_ref[0, 0, dr, h:h + 1, :] = n_new[0:1, :]
                    mf_ref[0, 0, dr * ML_HEADS + h:dr * ML_HEADS + h + 1, :] = jnp.broadcast_to(m_new, (1, LANES))

    return kern


def _mlstm(layer, za, zs, b_gates, mlstm_norm, init=None, ctx_out=None, states=None):
    has_init = init is not None
    seq, nseq, row0 = (S_SMP, N_SEQ_SMP, N_CTX // S_SMP) if has_init else (S_CTX, N_SEQ_CTX, 0)
    qkvo = [pl.BlockSpec((seq, ML_WIDTH), functools.partial(lambda j, b: (row0 + b, j), j)) for j in range(4)]
    in_specs = qkvo + [
        pl.BlockSpec((seq, LANES), lambda b: (row0 + b, ZS_W // LANES - 1)),
        pl.BlockSpec((1, 1, LANES), lambda b: (layer, 0, 0)),
        pl.BlockSpec((1, 1, ML_WIDTH), lambda b: (layer, 0, 0)),
    ]
    args = [za, za, za, za, zs, b_gates, mlstm_norm]
    out_specs = [pl.BlockSpec((seq, ML_WIDTH), lambda b: (row0 + b, 0))]
    out_shape = [jax.ShapeDtypeStruct((N_TOK, ML_WIDTH), BF16)]
    aliases = {}
    if has_init:
        st_c, st_n, st_m = init
        in_specs = [pl.BlockSpec(memory_space=pltpu.SMEM)] + in_specs + [
            pl.BlockSpec((1, 1, 2, ML_HEADS, ML_DIM, ML_DIM), lambda b: (b, layer, 0, 0, 0, 0)),
            pl.BlockSpec((1, 1, 2, ML_HEADS, ML_DIM), lambda b: (b, layer, 0, 0, 0)),
            pl.BlockSpec(memory_space=pl.ANY),
        ]
        args = [st_m.reshape(-1)] + args + [st_c, st_n, ctx_out]
        aliases = {len(args) - 1: 0}
    else:
        seq_blk = lambda b: jnp.minimum(b, nseq - 1)
        in_specs += [pl.BlockSpec(memory_space=pl.ANY)] * 3
        args += list(states)
        aliases = {len(args) - 3 + j: 1 + j for j in range(3)}
        out_specs += [
            pl.BlockSpec((1, 1, 2, ML_HEADS, ML_DIM, ML_DIM), lambda b: (seq_blk(b), layer, 0, 0, 0, 0)),
            pl.BlockSpec((1, 1, 2, ML_HEADS, ML_DIM), lambda b: (seq_blk(b), layer, 0, 0, 0)),
            pl.BlockSpec((1, 1, 2 * ML_HEADS, LANES), lambda b: (seq_blk(b), layer, 0, 0)),
        ]
        out_shape += [jax.ShapeDtypeStruct(s.shape, s.dtype) for s in states]
    return pl.pallas_call(
        _make_mlstm_kernel(seq, layer, has_init),
        grid=(nseq if has_init else N_TILES,),
        in_specs=in_specs,
        out_specs=out_specs,
        out_shape=out_shape,
        scratch_shapes=[pltpu.VMEM((seq, LANES), F32), pltpu.VMEM((seq, LANES), F32)],
        input_output_aliases=aliases,
        compiler_params=_cparams(("arbitrary",)),
        name="mlstm_smp" if has_init else "mlstm_ctx",
    )(*args)


def _rope(x, cos, sin_a, sin_b):
    return x * cos + pltpu.roll(x, LANES - 8, 1) * sin_a + pltpu.roll(x, 8, 1) * sin_b


def _mla_prep_kernel(zs_ref, qa_ref, kva_ref, wuq_ref, wk_ref, wv_ref, qn_ref, kn_ref, cos_ref, sa_ref, sb_ref, _, __,
                     q_ref, kk_ref, v_ref, ckv_ref, kr_ref, qf_scr, kf_scr):
    cq = zs_ref[:, 0:MLA_Q_RANK]
    ckv = zs_ref[:, MLA_Q_RANK:MLA_Q_RANK + MLA_KV_RANK]
    last = zs_ref[:, ZS_W - LANES:ZS_W]
    qf_scr[...] = _dot(_rms(cq, qa_ref[0]).astype(BF16), wuq_ref[0])
    ckvn = _rms(ckv, kva_ref[0])

    @pl.when(pl.program_id(0) < N_TILES_CTX)
    def _():
        ckv_ref[0, 0] = ckvn
        kr_ref[0, 0] = last[:, 0:MLA_ROPE]

    cb = ckvn.astype(BF16)
    kf_scr[...] = _dot(cb, wk_ref[0])
    v_ref[...] = _dot(cb, wv_ref[0]).astype(BF16)
    lane = lax.broadcasted_iota(jnp.int32, last.shape, 1)
    kr = jnp.where((lane >= MLA_NOPE) & (lane < MLA_QK), pltpu.roll(last, MLA_NOPE, 1), 0.0)
    is_latent = pl.program_id(0) >= N_TILES_CTX

    def heads(rotate):
        for h in range(MLA_HEADS):
            hs = slice(h * HEAD_PAD, (h + 1) * HEAD_PAD)
            q_ref[:, hs] = rotate(_rms(qf_scr[:, hs], qn_ref[0], n=MLA_QK)).astype(BF16)
            kk_ref[:, hs] = rotate(_rms(kf_scr[:, hs] + kr, kn_ref[0], n=MLA_QK)).astype(BF16)

    @pl.when(is_latent)
    def _():
        cos, sa, sb = cos_ref[...], sa_ref[...], sb_ref[...]
        heads(lambda x: _rope(x, cos, sa, sb))

    @pl.when(jnp.logical_not(is_latent))
    def _():
        heads(lambda x: x)


def _mla_prep(layer, zs, q_a_norm, kv_a_norm, w_uq_r, w_k_r, w_v_r, q_norm_p, k_norm_p, rope_tabs, new_ckv, new_kr):
    seq_blk = lambda w: pl.BlockSpec((1, 1, S_CTX, w), lambda i: (jnp.minimum(i, N_SEQ_CTX - 1), layer, 0, 0))
    lw = lambda shape: pl.BlockSpec((1,) + shape, lambda i: (layer,) + (0,) * len(shape))
    tab = pl.BlockSpec((TW, LANES), lambda i: (jnp.where(i < N_TILES_CTX, 0, 1 + (i - N_TILES_CTX) % TILES_PER_SMP_SEQ), 0))
    tok = lambda w: pl.BlockSpec((TW, w), lambda i: (i, 0))
    return pl.pallas_call(
        _mla_prep_kernel,
        grid=(N_TILES,),
        in_specs=[
            tok(ZS_W), lw((1, MLA_Q_RANK)), lw((1, MLA_KV_RANK)),
            lw((MLA_Q_RANK, MLA_HEADS * HEAD_PAD)), lw((MLA_KV_RANK, MLA_HEADS * HEAD_PAD)),
            lw((MLA_KV_RANK, MLA_HEADS * MLA_V)), lw((1, HEAD_PAD)), lw((1, HEAD_PAD)), tab, tab, tab,
            pl.BlockSpec(memory_space=pl.ANY), pl.BlockSpec(memory_space=pl.ANY),
        ],
        out_specs=[tok(MLA_HEADS * HEAD_PAD), tok(MLA_HEADS * HEAD_PAD), tok(MLA_HEADS * MLA_V),
                   seq_blk(MLA_KV_RANK), seq_blk(MLA_ROPE)],
        input_output_aliases={11: 3, 12: 4},
        out_shape=[
            jax.ShapeDtypeStruct((N_TOK, MLA_HEADS * HEAD_PAD), BF16),
            jax.ShapeDtypeStruct((N_TOK, MLA_HEADS * HEAD_PAD), BF16),
            jax.ShapeDtypeStruct((N_TOK, MLA_HEADS * MLA_V), BF16),
            jax.ShapeDtypeStruct(new_ckv.shape, F32),
            jax.ShapeDtypeStruct(new_kr.shape, F32),
        ],
        scratch_shapes=[pltpu.VMEM((TW, MLA_HEADS * HEAD_PAD), F32), pltpu.VMEM((TW, MLA_HEADS * HEAD_PAD), F32)],
        compiler_params=_cparams(("arbitrary",)),
        name="mla_prep",
    )(zs, q_a_norm, kv_a_norm, w_uq_r, w_k_r, w_v_r, q_norm_p, k_norm_p, *rope_tabs, new_ckv, new_kr)


def _cache_kv_kernel(ckv_ref, kr_ref, wk_ref, wv_ref, kn_ref, kk_ref, v_ref):
    cb = ckv_ref[...].astype(BF16)
    kf = _dot(cb, wk_ref[0])
    v_ref[...] = _dot(cb, wv_ref[0]).astype(BF16)
    kr = kr_ref[...]
    for h in range(MLA_HEADS):
        hs = slice(h * HEAD_PAD, (h + 1) * HEAD_PAD)
        kk_ref[:, hs] = _rms(kf[:, hs] + kr, kn_ref[0], n=MLA_QK).astype(BF16)


def _cache_kv(cache_ckv, cache_kr_pad, w_k_r, w_v_r, k_norm_p):
    lw = lambda shape: pl.BlockSpec((1,) + shape, lambda b, l: (l,) + (0,) * len(shape))
    blk = lambda w: pl.BlockSpec((None, None, PAST_LEN, w), lambda b, l: (b, l, 0, 0))
    return pl.pallas_call(
        _cache_kv_kernel,
        grid=(N_SEQ_SMP, DEPTH),
        in_specs=[blk(MLA_KV_RANK), blk(LANES), lw((MLA_KV_RANK, MLA_HEADS * HEAD_PAD)),
                  lw((MLA_KV_RANK, MLA_HEADS * MLA_V)), lw((1, HEAD_PAD))],
        out_specs=[blk(MLA_HEADS * HEAD_PAD), blk(MLA_HEADS * MLA_V)],
        out_shape=[
            jax.ShapeDtypeStruct((N_SEQ_SMP, DEPTH, PAST_LEN, MLA_HEADS * HEAD_PAD), BF16),
            jax.ShapeDtypeStruct((N_SEQ_SMP, DEPTH, PAST_LEN, MLA_HEADS * MLA_V), BF16),
        ],
        compiler_params=_cparams(("arbitrary", "arbitrary")),
        name="cache_kv",
    )(cache_ckv, cache_kr_pad, w_k_r, w_v_r, k_norm_p)


def _make_attn_kernel(n_src):
    scale = MLA_QK ** -0.5

    def kern(q_ref, *refs):
        o_ref = refs[-1]
        if n_src > 1:
            body(q_ref, *refs)
            return

        @pl.when(pl.program_id(0) < N_SEQ_CTX)
        def _():
            body(q_ref, *refs)

        @pl.when(pl.program_id(0) >= N_SEQ_CTX)
        def _():
            o_ref[...] = jnp.zeros_like(o_ref)

    def body(q_ref, *refs):
        o_ref = refs[-1]
        for h in range(MLA_HEADS):
            hs = slice(h * HEAD_PAD, (h + 1) * HEAD_PAD)
            vs = slice(h * MLA_V, (h + 1) * MLA_V)
            q = q_ref[:, hs]
            ss = [_dot_nt(q, refs[2 * j][:, hs]) * scale for j in range(n_src)]
            m = functools.reduce(jnp.maximum, [jnp.max(s, axis=1, keepdims=True) for s in ss])
            ps = [jnp.exp(s - m) for s in ss]
            l = functools.reduce(jnp.add, [jnp.sum(p, axis=1, keepdims=True) for p in ps])
            o = functools.reduce(jnp.add, [_dot(ps[j].astype(BF16), refs[2 * j + 1][:, vs]) for j in range(n_src)])
            o_ref[:, vs] = (o / l).astype(o_ref.dtype)

    return kern


def _attn_ctx(q, kk, v):
    blk = lambda w: pl.BlockSpec((S_CTX, w), lambda b: (b, 0))
    return pl.pallas_call(
        _make_attn_kernel(1),
        grid=(N_TILES,),
        in_specs=[blk(MLA_HEADS * HEAD_PAD), blk(MLA_HEADS * HEAD_PAD), blk(MLA_HEADS * MLA_V)],
        out_specs=blk(MLA_HEADS * MLA_V),
        out_shape=jax.ShapeDtypeStruct((N_TOK, MLA_HEADS * MLA_V), BF16),
        compiler_params=_cparams(("arbitrary",)),
        name="attn_ctx",
    )(q, kk, v)


def _attn_smp(layer, q, kk, v, kk_cache, v_cache, ctx_out):
    row0 = N_CTX // S_SMP
    nq = S_SMP // TQ
    seqb = lambda w: pl.BlockSpec((S_SMP, w), lambda b, i: (row0 + b, 0))
    cache = lambda w: pl.BlockSpec((None, None, PAST_LEN, w), lambda b, i: (b, layer, 0, 0))
    return pl.pallas_call(
        _make_attn_kernel(2),
        grid=(N_SEQ_SMP, nq),
        in_specs=[
            pl.BlockSpec((TQ, MLA_HEADS * HEAD_PAD), lambda b, i: (N_CTX // TQ + b * nq + i, 0)),
            seqb(MLA_HEADS * HEAD_PAD), seqb(MLA_HEADS * MLA_V),
            cache(MLA_HEADS * HEAD_PAD), cache(MLA_HEADS * MLA_V),
            pl.BlockSpec(memory_space=pl.ANY),
        ],
        out_specs=pl.BlockSpec((TQ, MLA_HEADS * MLA_V), lambda b, i: (N_CTX // TQ + b * nq + i, 0)),
        out_shape=jax.ShapeDtypeStruct((N_TOK, MLA_HEADS * MLA_V), BF16),
        input_output_aliases={5: 0},
        compiler_params=_cparams(("arbitrary", "arbitrary")),
        name="attn_smp",
    )(q, kk, v, kk_cache, v_cache, ctx_out)


def _merge_kernel(x_ref, oa_ref, ob_ref, zc_ref, zb_ref, sgn_ref, ws_ref, bs_ref, wb_ref, wo_ref, g1_ref, n2_ref,
                  sh2_ref, sc2_ref, wr_ref, br_ref,
                  xmid_ref, h2_ref, dest_ref, wsel_ref, cnt_ref, oc_scr, carry_scr):
    i = pl.program_id(0)

    @pl.when(i == 0)
    def _():
        carry_scr[...] = jnp.zeros_like(carry_scr)

    u = _gelu(zc_ref[:, 0:SG_WIDTH].astype(F32))
    vg = _gelu(zc_ref[:, SG_WIDTH:2 * SG_WIDTH].astype(F32))
    for g in range(SG_GROUPS):
        gs = slice(g * SG_DIM, (g + 1) * SG_DIM)
        vn = _rms(vg[:, gs], sgn_ref[0][:, gs]).astype(BF16)
        for c in range(TW // SG_CHUNK):
            cs = slice(c * SG_CHUNK, (c + 1) * SG_CHUNK)
            mixed = _dot(ws_ref[0, g], vn[cs, :]) + bs_ref[0][:, g:g + 1]
            oc_scr[cs, gs] = (u[cs, gs] * mixed).astype(BF16)

    acc = jnp.zeros((TW, D_MODEL), F32)
    for j, src in enumerate((oa_ref, ob_ref, oc_scr)):
        gate = jax.nn.sigmoid(zb_ref[:, j * D_MODEL:(j + 1) * D_MODEL].astype(F32))
        acc = acc + gate * _dot(src[...], wb_ref[0, j])
    xm = x_ref[...] + g1_ref[0, 0] * _dot(acc.astype(BF16), wo_ref[0])
    xmid_ref[...] = xm
    h2 = _rms(xm, n2_ref[0]) * (1.0 + sc2_ref[0, 0]) + sh2_ref[0, 0]
    for c in range(SC_SPLIT):
        h2_ref[c] = _pack_pairs(h2[:, 2 * c * SC_ROW:(2 * c + 1) * SC_ROW], h2[:, (2 * c + 1) * SC_ROW:(2 * c + 2) * SC_ROW])

    h_hi = h2.astype(BF16)
    h_lo = (h2 - h_hi.astype(F32)).astype(BF16)
    p_hi = _dot(h_hi, wr_ref[0])
    logits = p_hi[:, 0:LANES] + p_hi[:, LANES:2 * LANES] + _dot(h_lo, wr_ref[0, :, 0:LANES]) + br_ref[0]
    lane = lax.broadcasted_iota(jnp.int32, logits.shape, 1)
    hits, exps = [], []
    sel = jnp.zeros(logits.shape, F32)
    denom = jnp.zeros((TW, 1), F32)
    top = None
    for _ in range(TOP_K):
        m = jnp.max(logits, axis=1, keepdims=True)
        idx = jnp.min(jnp.where(logits == m, lane, LANES), axis=1, keepdims=True)
        hit = lane == idx
        top = m if top is None else top
        hits.append(hit)
        exps.append(jnp.exp(m - top))
        sel = jnp.where(hit, 1.0, sel)
        denom = denom + exps[-1]
        logits = jnp.where(hit, -jnp.inf, logits)

    r_i = lax.broadcasted_iota(jnp.int32, (TW, TW), 0)
    c_i = lax.broadcasted_iota(jnp.int32, (TW, TW), 1)
    carry = carry_scr[0:1, :]
    rank = _dot(jnp.where(c_i < r_i, 1.0, 0.0).astype(BF16), sel.astype(BF16)) + carry
    new_carry = carry + jnp.sum(sel, axis=0, keepdims=True)
    carry_scr[...] = jnp.broadcast_to(new_carry, (SUBLANES, LANES))
    cnt_ref[...] = jnp.broadcast_to(new_carry, (SUBLANES, LANES))
    slot = rank + lane.astype(F32) * float(EXPERT_CAP)
    dmat = jnp.zeros(logits.shape, F32)
    wmat = jnp.zeros(logits.shape, F32)
    for k in range(TOP_K):
        dk = jnp.sum(jnp.where(hits[k], slot, 0.0), axis=1, keepdims=True)
        dmat = jnp.where(lane == k, dk, dmat)
        wmat = jnp.where(lane == k, exps[k] / denom, wmat)
    dest_ref[...] = dmat.T[0:SUBLANES, :].astype(jnp.int32)
    wsel_ref[...] = wmat


def _merge(layer, x, oa, ob, zc, zb, sg_norm, w_sp, b_sp, w_branch, w_out, mod, norm2, w_router_p, b_router_p):
    lw = lambda shape: pl.BlockSpec((1,) + shape, lambda i: (layer,) + (0,) * len(shape))
    tok = lambda w: pl.BlockSpec((TW, w), lambda i: (i, 0))
    return pl.pallas_call(
        _merge_kernel,
        grid=(N_TILES,),
        in_specs=[
            tok(D_MODEL), tok(ML_WIDTH), tok(MLA_HEADS * MLA_V), tok(ZC_W), tok(ZB_W),
            lw((1, SG_WIDTH)), lw((SG_GROUPS, SG_CHUNK, SG_CHUNK)), lw((SG_CHUNK, LANES)),
            lw((N_BRANCH, ML_WIDTH, D_MODEL)), lw((D_MODEL, D_MODEL)),
            _mod_spec(layer, 2), lw((1, D_MODEL)), _mod_spec(layer, 3), _mod_spec(layer, 4),
            lw((D_MODEL, 2 * LANES)), lw((1, LANES)),
        ],
        out_specs=[tok(D_MODEL), pl.BlockSpec((SC_SPLIT, TW, SC_ROW), lambda i: (0, i, 0)),
                   pl.BlockSpec((SUBLANES, TW), lambda i: (0, i)), tok(LANES),
                   pl.BlockSpec((SUBLANES, LANES), lambda i: (0, 0))],
        out_shape=[
            jax.ShapeDtypeStruct((N_TOK, D_MODEL), F32),
            jax.ShapeDtypeStruct((SC_SPLIT, N_TOK, SC_ROW), jnp.uint32),
            jax.ShapeDtypeStruct((SUBLANES, N_TOK), jnp.int32),
            jax.ShapeDtypeStruct((N_TOK, LANES), F32),
            jax.ShapeDtypeStruct((SUBLANES, LANES), F32),
        ],
        scratch_shapes=[pltpu.VMEM((TW, SG_WIDTH), BF16), pltpu.VMEM((SUBLANES, LANES), F32)],
        compiler_params=_cparams(("arbitrary",)),
        name="merge_router",
    )(x, oa, ob, zc, zb, sg_norm, w_sp, b_sp, w_branch, w_out, mod, norm2, mod, mod, w_router_p, b_router_p)


def _sc_mesh():
    return plsc.VectorSubcoreMesh(core_axis_name="core", subcore_axis_name="subcore")


def _sc_scatter_rows(x, idxs, n_rows):
    @pl.kernel(out_type=jax.ShapeDtypeStruct((n_rows, SC_ROW), x.dtype), mesh=_sc_mesh(), scratch_types=[])
    def scatter(x_hbm, *refs):
        o_hbm = refs[-1]

        def body(x_vmem, *i_vmems):
            for i_vmem in i_vmems:
                pltpu.sync_copy(x_vmem, o_hbm.at[i_vmem.at[0]])

        pltpu.emit_pipeline(
            body,
            grid=(x.shape[0] // SC_WIN,),
            in_specs=[pl.BlockSpec((SC_WIN, SC_ROW), lambda i: (i, 0))]
            + [pl.BlockSpec((1, SC_WIN), lambda i: (0, i))] * len(idxs),
            out_specs=[],
            core_axis_name=("core", "subcore"),
            dimension_semantics=(pltpu.PARALLEL,),
        )(x_hbm, *refs[:-1])

    return scatter(x, *idxs)


def _sc_gather_rows(x, idx):
    m = idx.shape[1]

    @pl.kernel(out_type=jax.ShapeDtypeStruct((m, SC_ROW), x.dtype), mesh=_sc_mesh())
    def gather(x_hbm, i_hbm, o_hbm):
        def body(i_vmem, o_vmem):
            pltpu.sync_copy(x_hbm.at[i_vmem.at[0]], o_vmem)

        pltpu.emit_pipeline(
            body,
            grid=(m // SC_WIN,),
            in_specs=[pl.BlockSpec((1, SC_WIN), lambda i: (0, i))],
            out_specs=[pl.BlockSpec((SC_WIN, SC_ROW), lambda i: (i, 0))],
            core_axis_name=("core", "subcore"),
            dimension_semantics=(pltpu.PARALLEL,),
        )(i_hbm, o_hbm)

    return gather(x, idx)


STEP_VALID, STEP_FIRST, STEP_HAS_NEXT, STEP_FULL = 1, 2, 4, 8


def _moe_ffn_kernel(layer, be_ref, nx_ref, br_ref, fl_ref, xs_ref, b1_ref, b2_ref, w1_hbm, w2_hbm, y_ref,
                    w1f, w2f, w1b, w2b, sem):
    g = pl.program_id(0)
    flags = fl_ref[g]

    def weight_copies(e):
        return (pltpu.make_async_copy(w1_hbm.at[layer, e], w1f, sem.at[0]),
                pltpu.make_async_copy(w2_hbm.at[layer, e], w2f, sem.at[1]))

    @pl.when(g == 0)
    def _():
        for cp in weight_copies(be_ref[0]):
            cp.start()

    @pl.when((flags & STEP_FIRST) != 0)
    def _():
        for cp in weight_copies(be_ref[g]):
            cp.wait()
        w1b[...] = w1f[...].astype(BF16)
        w2b[...] = w2f[...].astype(BF16)

        @pl.when((flags & STEP_HAS_NEXT) != 0)
        def _():
            for cp in weight_copies(nx_ref[g]):
                cp.start()

    def ffn(n_rows):
        halves = [h.astype(BF16) for c in range(SC_SPLIT) for h in _unpack_pairs(xs_ref[c, 0:n_rows, :])]
        g1 = _dot(jnp.concatenate(halves, axis=1), w1b[...]) + b1_ref[0, 0]
        gate = jnp.minimum(g1[:, :D_EXPERT], SWIGLU_LIMIT)
        up = jnp.clip(g1[:, D_EXPERT:], -SWIGLU_LIMIT, SWIGLU_LIMIT)
        act = gate * jax.nn.sigmoid(SWIGLU_ALPHA * gate) * (up + 1.0)
        y = _dot(act.astype(BF16), w2b[...]) + b2_ref[0, 0]
        for c in range(SC_SPLIT):
            y_ref[c, 0:n_rows, :] = _pack_pairs(
                y[:, 2 * c * SC_ROW:(2 * c + 1) * SC_ROW], y[:, (2 * c + 1) * SC_ROW:(2 * c + 2) * SC_ROW])

    pl.when((flags & (STEP_VALID | STEP_FULL)) == STEP_VALID)(lambda: ffn(SLOT_CHUNK))
    pl.when((flags & STEP_FULL) != 0)(lambda: ffn(FFN_BLOCK))


def _moe_ffn(layer, xs, plan, w1, b1, w2, b2):
    eb = lambda c: pl.BlockSpec((1, 1, 1, c), lambda g, be, nx, br, fl: (layer, be[g], 0, 0))
    rows = pl.BlockSpec((SC_SPLIT, FFN_BLOCK, SC_ROW), lambda g, be, nx, br, fl: (0, br[g], 0))
    hbm = pl.BlockSpec(memory_space=pl.ANY)
    grid_spec = pltpu.PrefetchScalarGridSpec(
        num_scalar_prefetch=4,
        grid=(N_CHUNK_STEPS,),
        in_specs=[rows, eb(2 * D_EXPERT), eb(D_MODEL), hbm, hbm],
        out_specs=rows,
        scratch_shapes=[
            pltpu.VMEM((D_MODEL, 2 * D_EXPERT), F32), pltpu.VMEM((D_EXPERT, D_MODEL), F32),
            pltpu.VMEM((D_MODEL, 2 * D_EXPERT), BF16), pltpu.VMEM((D_EXPERT, D_MODEL), BF16),
            pltpu.SemaphoreType.DMA((2,)),
        ],
    )
    return pl.pallas_call(
        functools.partial(_moe_ffn_kernel, layer),
        grid_spec=grid_spec,
        out_shape=jax.ShapeDtypeStruct(xs.shape, xs.dtype),
        compiler_params=_cparams(("arbitrary",)),
        name="moe_ffn",
    )(*plan, xs, b1, b2, w1, w2)


def _chunk_plan(cnt):
    nch = (cnt + FFN_BLOCK - 1) // FFN_BLOCK
    cum = jnp.cumsum(nch)
    expert_at = lambda step: jnp.minimum(
        jnp.sum((cum[None, :] <= step[:, None]).astype(jnp.int32), axis=1), N_EXPERTS - 1)
    steps = jnp.arange(N_CHUNK_STEPS, dtype=jnp.int32)
    g = jnp.minimum(steps, cum[-1] - 1)
    e = expert_at(g)
    is_e = jnp.arange(N_EXPERTS, dtype=jnp.int32)[None, :] == e[:, None]
    end_e = jnp.sum(jnp.where(is_e, cum[None, :], 0), axis=1)
    j = g - jnp.sum(jnp.where(is_e, (cum - nch)[None, :], 0), axis=1)
    valid = steps < cum[-1]
    first = valid & (j == 0)
    cnt_e = jnp.sum(jnp.where(is_e, cnt[None, :], 0), axis=1)
    full = valid & (cnt_e - j * FFN_BLOCK > SLOT_CHUNK)
    flags = (STEP_VALID * valid.astype(jnp.int32) + STEP_FIRST * first.astype(jnp.int32)
             + STEP_HAS_NEXT * (first & (end_e < cum[-1])).astype(jnp.int32) + STEP_FULL * full.astype(jnp.int32))
    return e, expert_at(end_e), (e * (EXPERT_CAP // FFN_BLOCK) + j).astype(jnp.int32), flags


def _combine_kernel(x_ref, yg_ref, w_ref, g_ref, *o_refs):
    def emit(o_ref):
        w = w_ref[...]
        for c in range(SC_SPLIT):
            parts = [_unpack_pairs(yg_ref[k, c]) for k in range(TOP_K)]
            for half in range(2):
                cs = slice((2 * c + half) * SC_ROW, (2 * c + half + 1) * SC_ROW)
                acc = w[:, 0:1] * parts[0][half]
                for k in range(1, TOP_K):
                    acc = acc + w[:, k:k + 1] * parts[k][half]
                o_ref[:, cs] = x_ref[:, cs] + g_ref[0, 0][:, cs] * acc

    if len(o_refs) == 1:
        emit(o_refs[0])
    else:
        pl.when(pl.program_id(0) < N_TILES_CTX)(lambda: emit(o_refs[0]))
        pl.when(pl.program_id(0) >= N_TILES_CTX)(lambda: emit(o_refs[1]))


def _combine(layer, xmid, yg, wsel, mod, split_out):
    tok = lambda w: pl.BlockSpec((TW, w), lambda i: (i, 0))
    if split_out:
        out_specs = [pl.BlockSpec((TW, D_MODEL), lambda i: (jnp.minimum(i, N_TILES_CTX - 1), 0)),
                     pl.BlockSpec((TW, D_MODEL), lambda i: (jnp.maximum(i - N_TILES_CTX, 0), 0))]
        out_shape = [jax.ShapeDtypeStruct((N_CTX, D_MODEL), F32), jax.ShapeDtypeStruct((N_SMP, D_MODEL), F32)]
    else:
        out_specs, out_shape = tok(D_MODEL), jax.ShapeDtypeStruct((N_TOK, D_MODEL), F32)
    return pl.pallas_call(
        _combine_kernel,
        grid=(N_TILES,),
        in_specs=[tok(D_MODEL), pl.BlockSpec((TOP_K, SC_SPLIT, TW, SC_ROW), lambda i: (0, 0, i, 0)), tok(LANES),
                  _mod_spec(layer, 5)],
        out_specs=out_specs,
        out_shape=out_shape,
        compiler_params=_cparams(("arbitrary",)),
        name="combine",
    )(xmid, yg, wsel, mod)


def _moe(layer, xmid, h2, dest, wsel, cnt, mod, w1, b1, w2, b2, split_out):
    n_slots = N_EXPERTS * EXPERT_CAP
    idx = dest[0:TOP_K][:, None, :] + (jnp.arange(SC_SPLIT, dtype=jnp.int32) * n_slots)[None, :, None]
    idx = idx.reshape(TOP_K, 1, SC_SPLIT * N_TOK)
    xs = _sc_scatter_rows(h2.reshape(SC_SPLIT * N_TOK, SC_ROW), [idx[k] for k in range(TOP_K)], SC_SPLIT * n_slots)
    plan = _chunk_plan(cnt[0, :N_EXPERTS].astype(jnp.int32))
    y = _moe_ffn(layer, xs.reshape(SC_SPLIT, n_slots, SC_ROW), plan, w1, b1, w2, b2)
    yg = _sc_gather_rows(y.reshape(SC_SPLIT * n_slots, SC_ROW), idx.reshape(1, TOP_K * SC_SPLIT * N_TOK))
    return _combine(layer, xmid, yg.reshape(TOP_K, SC_SPLIT, N_TOK, SC_ROW), wsel, mod, split_out)


def _rope_tables():
    pos = np.arange(S_SMP)
    half = MLA_ROPE // 2
    inv_freq = (ROPE_THETA ** (-(np.arange(0, half, 2, dtype=np.float32) / np.float32(half)))).astype(np.float32)
    angs = [((pos // GRID_W).astype(np.float32)[:, None] * inv_freq[None, :]).astype(np.float32),
            ((pos % GRID_W).astype(np.float32)[:, None] * inv_freq[None, :]).astype(np.float32)]
    nf = half // 2
    cos = np.ones((TW + S_SMP, LANES), np.float32)
    sin_a = np.zeros((TW + S_SMP, LANES), np.float32)
    sin_b = np.zeros((TW + S_SMP, LANES), np.float32)
    for axis, ang in enumerate(angs):
        base = MLA_NOPE + axis * half
        c, s = np.cos(ang.astype(np.float64)), np.sin(ang.astype(np.float64))
        cos[TW:, base:base + nf] = c
        cos[TW:, base + nf:base + half] = c
        sin_a[TW:, base:base + nf] = -s
        sin_b[TW:, base + nf:base + half] = s
    return jnp.asarray(cos), jnp.asarray(sin_a), jnp.asarray(sin_b)


def _pad_last(a, width):
    return jnp.pad(a, [(0, 0)] * (a.ndim - 1) + [(0, width - a.shape[-1])])


def kernel(x_prompt, x_sample, cache_mla_ckv, cache_mla_krope, state_mlstm_C, state_mlstm_n, state_mlstm_m, c, c_ctx, norm1, norm2, w_ada, b_ada, w_in, b_mlstm_gates, mlstm_norm, mla_q_a_norm, mla_kv_a_norm, w_uq, w_ukv, mla_q_norm, mla_k_norm, sg_norm, w_spatial, b_spatial, w_branch, w_out, w_router, b_router, w_exp1, b_exp1, w_exp2, b_exp2):
    w_in_r = _w_in_prep(w_in)
    w_uq_r = _pad_last(w_uq.reshape(DEPTH, MLA_Q_RANK, MLA_HEADS, MLA_QK), HEAD_PAD).reshape(
        DEPTH, MLA_Q_RANK, MLA_HEADS * HEAD_PAD).astype(BF16)
    w_ukv4 = w_ukv.reshape(DEPTH, MLA_KV_RANK, MLA_HEADS, MLA_NOPE + MLA_V)
    w_k_r = _pad_last(w_ukv4[..., :MLA_NOPE], HEAD_PAD).reshape(DEPTH, MLA_KV_RANK, MLA_HEADS * HEAD_PAD).astype(BF16)
    w_v_r = w_ukv4[..., MLA_NOPE:].reshape(DEPTH, MLA_KV_RANK, MLA_HEADS * MLA_V).astype(BF16)
    q_norm_p = _pad_last(mla_q_norm, HEAD_PAD).reshape(DEPTH, 1, HEAD_PAD)
    k_norm_p = _pad_last(mla_k_norm, HEAD_PAD).reshape(DEPTH, 1, HEAD_PAD)
    b_gates_p = jnp.pad(b_mlstm_gates, ((0, 0), (GATE_LANE0, LANES - GATE_LANE0 - 4 * ML_HEADS))).reshape(DEPTH, 1, LANES)
    b_sp = _pad_last(jnp.swapaxes(b_spatial, 1, 2), LANES)
    w_router_p = _pad_last(w_router, LANES)
    w_router_hi = w_router_p.astype(BF16)
    w_router_p = jnp.concatenate([w_router_hi, (w_router_p - w_router_hi.astype(F32)).astype(BF16)], axis=-1)
    b_router_p = jnp.pad(b_router, ((0, 0), (0, LANES - N_EXPERTS)), constant_values=-1e30).reshape(DEPTH, 1, LANES)
    r3 = lambda a: a.reshape(DEPTH, 1, a.shape[-1])
    cache_kr_pad = jnp.pad(cache_mla_krope, ((0, 0), (0, 0), (0, 0), (MLA_NOPE, LANES - MLA_QK)))
    rope_tabs = _rope_tables()

    cvec = jnp.concatenate([c_ctx[None, :], c, jnp.zeros((SUBLANES - 1 - N_SEQ_SMP, D_MODEL), F32)], axis=0)
    mod = _adaln(cvec, w_ada, b_ada).reshape(DEPTH, SUBLANES, 1, 6 * D_MODEL)
    b1 = b_exp1.reshape(DEPTH, N_EXPERTS, 1, 2 * D_EXPERT)
    b2 = b_exp2.reshape(DEPTH, N_EXPERTS, 1, D_MODEL)
    kk_cache, v_cache = _cache_kv(cache_mla_ckv, cache_kr_pad, w_k_r, w_v_r, k_norm_p)

    x = jnp.concatenate([x_prompt.reshape(N_CTX, D_MODEL), x_sample.reshape(N_SMP, D_MODEL)], axis=0)
    new_ckv = jnp.zeros((N_SEQ_CTX, DEPTH, S_CTX, MLA_KV_RANK), F32)
    new_kr = jnp.zeros((N_SEQ_CTX, DEPTH, S_CTX, MLA_ROPE), F32)
    states = (jnp.zeros((N_SEQ_CTX, DEPTH, 2, ML_HEADS, ML_DIM, ML_DIM), F32),
              jnp.zeros((N_SEQ_CTX, DEPTH, 2, ML_HEADS, ML_DIM), F32),
              jnp.zeros((N_SEQ_CTX, DEPTH, 2 * ML_HEADS, LANES), F32))
    for l in range(DEPTH):
        za, zs, zc, zb = _inproj(l, x, r3(norm1), mod, w_in_r)
        oa, *states = _mlstm(l, za, zs, b_gates_p, r3(mlstm_norm), states=states)
        (oa,) = _mlstm(l, za, zs, b_gates_p, r3(mlstm_norm), init=(state_mlstm_C, state_mlstm_n, state_mlstm_m), ctx_out=oa)
        q, kk, v, new_ckv, new_kr = _mla_prep(l, zs, r3(mla_q_a_norm), r3(mla_kv_a_norm), w_uq_r, w_k_r, w_v_r,
                                              q_norm_p, k_norm_p, rope_tabs, new_ckv, new_kr)
        ob = _attn_smp(l, q, kk, v, kk_cache, v_cache, _attn_ctx(q, kk, v))
        xmid, h2, dest, wsel, cnt = _merge(
            l, x, oa, ob, zc, zb, r3(sg_norm), w_spatial.astype(BF16), b_sp, w_branch.astype(BF16), w_out.astype(BF16),
            mod, r3(norm2), w_router_p, b_router_p)
        x = _moe(l, xmid, h2, dest, wsel, cnt, mod, w_exp1, b1, w_exp2, b2, split_out=l == DEPTH - 1)
    y_ctx, y_smp = x
    return (
        y_ctx.reshape(N_SEQ_CTX, S_CTX, D_MODEL),
        y_smp.reshape(N_SEQ_SMP, S_SMP, D_MODEL),
        new_ckv,
        new_kr,
        states[0],
        states[1],
        states[2][:, :, :, 0].reshape(N_SEQ_CTX, DEPTH, 2, ML_HEADS),
    )
```

```python
import functools

import numpy as np
import jax
import jax.numpy as jnp
from jax import lax
from jax.experimental import pallas as pl
from jax.experimental.pallas import tpu as pltpu
from jax.experimental.pallas import tpu_sc as plsc

F32 = jnp.float32
BF16 = jnp.bfloat16
HI = lax.Precision.HIGHEST

D_MODEL = 1024
N_SEQ_CTX, S_CTX = 32, 256
N_SEQ_SMP, S_SMP = 2, 1024
DEPTH = 4
PAST_LEN = 512
GRID_W = 64
EPS = 1e-6
ML_HEADS, ML_DIM = 4, 128
ML_WIDTH = ML_HEADS * ML_DIM
MLA_HEADS, MLA_NOPE, MLA_ROPE, MLA_V = 8, 64, 32, 64
MLA_QK = MLA_NOPE + MLA_ROPE
MLA_Q_RANK, MLA_KV_RANK = 256, 128
ROPE_THETA = 10000.0
SG_GROUPS, SG_DIM, SG_CHUNK = 4, 128, 128
SG_WIDTH = SG_GROUPS * SG_DIM
N_BRANCH = 3
N_EXPERTS, TOP_K, D_EXPERT = 32, 4, 1024
SWIGLU_LIMIT, SWIGLU_ALPHA = 7.0, 1.702

N_CTX = N_SEQ_CTX * S_CTX
N_SMP = N_SEQ_SMP * S_SMP
N_TOK = N_CTX + N_SMP

LANES = 128
SUBLANES = 8
VMEM_LIMIT = 56 * 1024 * 1024

TW = 256
N_TILES = N_TOK // TW
N_TILES_CTX = N_CTX // TW
TILES_PER_SMP_SEQ = S_SMP // TW
HEAD_PAD = LANES
TQ = 256
EXPERT_CAP = N_TOK
SLOT_CHUNK = 256
FFN_BLOCK = 2 * SLOT_CHUNK
N_CHUNK_STEPS = N_TOK * TOP_K // FFN_BLOCK + N_EXPERTS
SC_ROW = 256
SC_SPLIT = D_MODEL // (2 * SC_ROW)
SC_WIN = 128

ZA_W = 4 * ML_WIDTH
ZS_W = 512
ZC_W = 2 * SG_WIDTH
ZB_W = N_BRANCH * D_MODEL
ZIN_W = ZA_W + ZS_W + ZC_W + ZB_W
GATE_LANE0 = MLA_ROPE


def _cparams(sem):
    return pltpu.CompilerParams(dimension_semantics=sem, vmem_limit_bytes=VMEM_LIMIT)


def _mod_row(i):
    return jnp.where(i < N_TILES_CTX, 0, 1 + (i - N_TILES_CTX) // TILES_PER_SMP_SEQ)


def _rms(x, g, n=None):
    ms = jnp.sum(x * x, axis=-1, keepdims=True) * (1.0 / (n or x.shape[-1]))
    return x * lax.rsqrt(ms + EPS) * g


def _gelu(x):
    return 0.5 * x * (1.0 + jnp.tanh(0.7978845608028654 * (x + 0.044715 * (x * x * x))))


def _pack_pairs(lo, hi):
    lo_bits = lax.bitcast_convert_type(lo.astype(BF16).astype(F32), jnp.uint32)
    hi_bits = lax.bitcast_convert_type(hi.astype(BF16).astype(F32), jnp.uint32)
    return (lo_bits >> 16) | (hi_bits & jnp.uint32(0xFFFF0000))


def _unpack_pairs(u):
    return (lax.bitcast_convert_type(u << 16, F32), lax.bitcast_convert_type(u & jnp.uint32(0xFFFF0000), F32))


def _dot(a, b):
    return jnp.dot(a, b, preferred_element_type=F32)


def _dot_nt(a, b):
    return lax.dot_general(a, b, (((1,), (1,)), ((), ())), preferred_element_type=F32)


def _adaln_kernel(c_ref, w_ref, b_ref, o_ref):
    c = c_ref[...]
    s = c * jax.nn.sigmoid(c)
    o_ref[0] = jnp.dot(s, w_ref[0], precision=HI, preferred_element_type=F32) + b_ref[0]


def _adaln(cvec, w_ada, b_ada):
    nchunk = 4
    cw = 6 * D_MODEL // nchunk
    return pl.pallas_call(
        _adaln_kernel,
        grid=(DEPTH, nchunk),
        in_specs=[
            pl.BlockSpec((SUBLANES, D_MODEL), lambda l, j: (0, 0)),
            pl.BlockSpec((1, D_MODEL, cw), lambda l, j: (l, 0, j)),
            pl.BlockSpec((1, 1, cw), lambda l, j: (l, 0, j)),
        ],
        out_specs=pl.BlockSpec((1, SUBLANES, cw), lambda l, j: (l, 0, j)),
        out_shape=jax.ShapeDtypeStruct((DEPTH, SUBLANES, 6 * D_MODEL), F32),
        compiler_params=_cparams(("arbitrary", "arbitrary")),
        name="adaln",
    )(cvec, w_ada, b_ada.reshape(DEPTH, 1, 6 * D_MODEL))


IN_SPLITS = (ML_WIDTH, ML_WIDTH, ML_WIDTH, ML_WIDTH, 4 * ML_HEADS, MLA_Q_RANK, MLA_KV_RANK, MLA_ROPE, SG_WIDTH, SG_WIDTH,
             N_BRANCH * D_MODEL)
IN_OFFS = tuple(int(v) for v in np.cumsum((0,) + IN_SPLITS))
D_IN = IN_OFFS[-1]
W_PREP_ROWS = 256
W_PREP_COLS = 512


def _w_in_prep_kernel(wt_ref, o_ref):
    o = IN_OFFS

    def put(c0, rows):
        o_ref[0, :, c0:c0 + W_PREP_COLS] = rows.T.astype(BF16)

    for c0 in range(0, ZA_W, W_PREP_COLS):
        scale = ML_DIM ** -0.5 if o[1] <= c0 < o[2] else 1.0
        put(c0, wt_ref[0, c0:c0 + W_PREP_COLS, :] * scale)
    pad = jnp.zeros((ZS_W - (o[8] - o[4]), W_PREP_ROWS), F32)
    put(ZA_W, jnp.concatenate([wt_ref[0, o[5]:o[8], :], wt_ref[0, o[4]:o[5], :], pad], axis=0))
    for c0 in range(ZA_W + ZS_W, ZIN_W, W_PREP_COLS):
        src = c0 - (ZA_W + ZS_W) + o[8]
        put(c0, wt_ref[0, src:src + W_PREP_COLS, :])


def _w_in_prep(w_in):
    return pl.pallas_call(
        _w_in_prep_kernel,
        grid=(DEPTH, D_MODEL // W_PREP_ROWS),
        in_specs=[pl.BlockSpec((1, D_IN, W_PREP_ROWS), lambda l, r: (l, 0, r))],
        out_specs=pl.BlockSpec((1, W_PREP_ROWS, ZIN_W), lambda l, r: (l, r, 0)),
        out_shape=jax.ShapeDtypeStruct((DEPTH, D_MODEL, ZIN_W), BF16),
        compiler_params=_cparams(("arbitrary", "arbitrary")),
        name="w_in_prep",
    )(jnp.swapaxes(w_in, 1, 2))


def _inproj_kernel(x_ref, g_ref, sh_ref, sc_ref, w_ref, za_ref, zs_ref, zc_ref, zb_ref):
    h = _rms(x_ref[...], g_ref[0]) * (1.0 + sc_ref[0, 0]) + sh_ref[0, 0]
    hb = h.astype(BF16)
    za_ref[...] = _dot(hb, w_ref[0, :, 0:ZA_W]).astype(BF16)
    zs_ref[...] = _dot(hb, w_ref[0, :, ZA_W:ZA_W + ZS_W])
    zc_ref[...] = _dot(hb, w_ref[0, :, ZA_W + ZS_W:ZA_W + ZS_W + ZC_W]).astype(BF16)
    zb_ref[...] = _dot(hb, w_ref[0, :, ZA_W + ZS_W + ZC_W:ZIN_W]).astype(BF16)


def _mod_spec(layer, k):
    return pl.BlockSpec((1, 1, 1, D_MODEL), lambda i: (layer, _mod_row(i), 0, k))


def _inproj(layer, x, norm1, mod, w_in_r):
    tok = lambda w: pl.BlockSpec((TW, w), lambda i: (i, 0))
    return pl.pallas_call(
        _inproj_kernel,
        grid=(N_TILES,),
        in_specs=[
            tok(D_MODEL),
            pl.BlockSpec((1, 1, D_MODEL), lambda i: (layer, 0, 0)),
            _mod_spec(layer, 0),
            _mod_spec(layer, 1),
            pl.BlockSpec((1, D_MODEL, ZIN_W), lambda i: (layer, 0, 0)),
        ],
        out_specs=[tok(ZA_W), tok(ZS_W), tok(ZC_W), tok(ZB_W)],
        out_shape=[
            jax.ShapeDtypeStruct((N_TOK, ZA_W), BF16),
            jax.ShapeDtypeStruct((N_TOK, ZS_W), F32),
            jax.ShapeDtypeStruct((N_TOK, ZC_W), BF16),
            jax.ShapeDtypeStruct((N_TOK, ZB_W), BF16),
        ],
        compiler_params=_cparams(("arbitrary",)),
        name="inproj",
    )(x, norm1, mod, mod, w_in_r)


def _make_mlstm_kernel(seq, layer, has_init):
    nq = seq // TQ
    lane_if, lane_ff, lane_ib, lane_fb = (GATE_LANE0 + ML_HEADS * j for j in range(4))

    def kern(*refs):
        if has_init:
            body(*refs)
            return
        out = refs[10]
        b = pl.program_id(0)

        @pl.when(b < N_SEQ_CTX)
        def _():
            body(*refs)

        @pl.when(b >= N_SEQ_CTX)
        def _():
            out[...] = jnp.zeros_like(out)

    def body(*refs):
        if has_init:
            m0_ref, zq, zk, zv, zo, gz, bg, nrm, c0_ref, n0_ref, _, out, bp_scr, bs_scr = refs
        else:
            zq, zk, zv, zo, gz, bg, nrm, _, _, _, out, cf_ref, nf_ref, mf_ref, bp_scr, bs_scr = refs
        b = pl.program_id(0)
        g = gz[...] + bg[0]
        lane = lax.broadcasted_iota(jnp.int32, g.shape, 1)
        is_forget = ((lane >= lane_ff) & (lane < lane_ib)) | ((lane >= lane_fb) & (lane < lane_fb + ML_HEADS))
        log_sig = jnp.minimum(g, 0.0) - jnp.log1p(jnp.exp(-jnp.abs(g)))
        a = jnp.where(is_forget, log_sig, g)
        r_i = lax.broadcasted_iota(jnp.int32, (seq, seq), 0)
        c_i = lax.broadcasted_iota(jnp.int32, (seq, seq), 1)
        ltri = (c_i <= r_i).astype(F32)
        bp = jnp.dot(ltri, a, precision=HI, preferred_element_type=F32)
        bs = bp[seq - 1:seq, :] - bp + a
        bp_scr[...] = bp
        bs_scr[...] = bs
        eye = (lax.broadcasted_iota(jnp.int32, (LANES, LANES), 0)
               == lax.broadcasted_iota(jnp.int32, (LANES, LANES), 1)).astype(F32)
        tr = lambda x: lax.dot_general(eye, x, (((1,), (1,)), ((), ())), precision=HI, preferred_element_type=F32)
        if has_init:
            tr = lambda x: x.T
        a_t, bp_t, bs_t = tr(a), tr(bp), tr(bs)

        for h in range(ML_HEADS):
            hs = slice(h * ML_DIM, (h + 1) * ML_DIM)
            k = zk[:, hs]
            v = zv[:, hs]
            first_lane = lax.broadcasted_iota(jnp.int32, (seq, ML_DIM), 1) == 0
            v_aug = jnp.concatenate([v, jnp.where(first_lane, 1.0, 0.0).astype(BF16)], axis=1)
            rows = (
                a_t[lane_if + h:lane_if + h + 1, :] - bp_t[lane_ff + h:lane_ff + h + 1, :],
                a_t[lane_ib + h:lane_ib + h + 1, :] - bs_t[lane_fb + h:lane_fb + h + 1, :],
            )
            col_refs = ((bp_scr, lane_ff + h), (bs_scr, lane_fb + h))
            if has_init:
                m0 = tuple(m0_ref[((b * DEPTH + layer) * 2 + dr) * ML_HEADS + h] for dr in range(2))
                c0 = tuple(c0_ref[0, 0, dr, h].astype(BF16) for dr in range(2))
                n0 = tuple(jnp.broadcast_to(n0_ref[0, 0, dr, h:h + 1, :], (ML_DIM, ML_DIM)).astype(BF16) for dr in range(2))
            else:
                m0 = (0.0, 0.0)

            def qblock(qi, carry):
                q0 = pl.multiple_of(qi * TQ, TQ)
                qb = zq[pl.ds(q0, TQ), hs]
                sc = _dot_nt(qb, k)
                t_idx = q0 + lax.broadcasted_iota(jnp.int32, (TQ, seq), 0)
                s_idx = lax.broadcasted_iota(jnp.int32, (TQ, seq), 1)
                hsum = jnp.zeros((TQ, ML_DIM), F32)
                for dr in range(2):
                    cref, cl = col_refs[dr]
                    col = cref[pl.ds(q0, TQ), cl:cl + 1]
                    mask = (s_idx <= t_idx) if dr == 0 else (s_idx >= t_idx)
                    drow = jnp.where(mask, rows[dr], -jnp.inf)
                    c_t = jnp.maximum(m0[dr], jnp.max(drow, axis=1, keepdims=True))
                    s = sc * jnp.exp(drow - c_t)
                    na = _dot(s.astype(BF16), v_aug)
                    num, den = na[:, 0:ML_DIM], na[:, ML_DIM:ML_DIM + 1]
                    if has_init:
                        w_c = jnp.exp(m0[dr] - c_t)
                        num = num + w_c * _dot(qb, c0[dr])
                        den = den + w_c * _dot_nt(qb, n0[dr])[:, 0:1]
                    hsum = hsum + num / jnp.maximum(jnp.abs(den), jnp.exp(-(col + c_t)))
                hn = _rms(hsum, nrm[0][:, hs])
                og = zo[pl.ds(q0, TQ), hs].astype(F32)
                out[pl.ds(q0, TQ), hs] = (hn * jax.nn.sigmoid(og)).astype(out.dtype)
                return carry

            if nq == 1:
                qblock(0, 0)
            else:
                lax.fori_loop(0, nq, qblock, 0)

            if not has_init:
                k_t = _dot_nt(eye.astype(BF16), k)
                kf = k.astype(F32)
                tot = (bp_t[lane_ff + h:lane_ff + h + 1, seq - 1:seq], bp_t[lane_fb + h:lane_fb + h + 1, seq - 1:seq])
                gl = (
                    tot[0] + rows[0],
                    bp_t[lane_fb + h:lane_fb + h + 1, :] - a_t[lane_fb + h:lane_fb + h + 1, :]
                    + a_t[lane_ib + h:lane_ib + h + 1, :],
                )
                for dr in range(2):
                    m_new = jnp.maximum(tot[dr] + m0[dr], jnp.max(gl[dr], axis=1, keepdims=True))
                    w_s = jnp.exp(gl[dr] - m_new)
                    cf_ref[0, 0, dr, h] = _dot((k_t * w_s).astype(BF16), v)
                    n_new = jnp.dot(jnp.broadcast_to(w_s, (SUBLANES, seq)), kf, precision=HI, preferred_element_type=F32)
                    nf_ref[0, 0, dr, h:h + 1, :] = n_new[0:1, :]
                    mf_ref[0, 0, dr * ML_HEADS + h:dr * ML_HEADS + h + 1, :] = jnp.broadcast_to(m_new, (1, LANES))

    return kern


def _mlstm(layer, za, zs, b_gates, mlstm_norm, init=None, ctx_out=None, states=None):
    has_init = init is not None
    seq, nseq, row0 = (S_SMP, N_SEQ_SMP, N_CTX // S_SMP) if has_init else (S_CTX, N_SEQ_CTX, 0)
    qkvo = [pl.BlockSpec((seq, ML_WIDTH), functools.partial(lambda j, b: (row0 + b, j), j)) for j in range(4)]
    in_specs = qkvo + [
        pl.BlockSpec((seq, LANES), lambda b: (row0 + b, ZS_W // LANES - 1)),
        pl.BlockSpec((1, 1, LANES), lambda b: (layer, 0, 0)),
        pl.BlockSpec((1, 1, ML_WIDTH), lambda b: (layer, 0, 0)),
    ]
    args = [za, za, za, za, zs, b_gates, mlstm_norm]
    out_specs = [pl.BlockSpec((seq, ML_WIDTH), lambda b: (row0 + b, 0))]
    out_shape = [jax.ShapeDtypeStruct((N_TOK, ML_WIDTH), BF16)]
    aliases = {}
    if has_init:
        st_c, st_n, st_m = init
        in_specs = [pl.BlockSpec(memory_space=pltpu.SMEM)] + in_specs + [
            pl.BlockSpec((1, 1, 2, ML_HEADS, ML_DIM, ML_DIM), lambda b: (b, layer, 0, 0, 0, 0)),
            pl.BlockSpec((1, 1, 2, ML_HEADS, ML_DIM), lambda b: (b, layer, 0, 0, 0)),
            pl.BlockSpec(memory_space=pl.ANY),
        ]
        args = [st_m.reshape(-1)] + args + [st_c, st_n, ctx_out]
        aliases = {len(args) - 1: 0}
    else:
        seq_blk = lambda b: jnp.minimum(b, nseq - 1)
        in_specs += [pl.BlockSpec(memory_space=pl.ANY)] * 3
        args += list(states)
        aliases = {len(args) - 3 + j: 1 + j for j in range(3)}
        out_specs += [
            pl.BlockSpec((1, 1, 2, ML_HEADS, ML_DIM, ML_DIM), lambda b: (seq_blk(b), layer, 0, 0, 0, 0)),
            pl.BlockSpec((1, 1, 2, ML_HEADS, ML_DIM), lambda b: (seq_blk(b), layer, 0, 0, 0)),
            pl.BlockSpec((1, 1, 2 * ML_HEADS, LANES), lambda b: (seq_blk(b), layer, 0, 0)),
        ]
        out_shape += [jax.ShapeDtypeStruct(s.shape, s.dtype) for s in states]
    return pl.pallas_call(
        _make_mlstm_kernel(seq, layer, has_init),
        grid=(nseq if has_init else N_TILES,),
        in_specs=in_specs,
        out_specs=out_specs,
        out_shape=out_shape,
        scratch_shapes=[pltpu.VMEM((seq, LANES), F32), pltpu.VMEM((seq, LANES), F32)],
        input_output_aliases=aliases,
        compiler_params=_cparams(("arbitrary",)),
        name="mlstm_smp" if has_init else "mlstm_ctx",
    )(*args)


def _rope(x, cos, sin_a, sin_b):
    return x * cos + pltpu.roll(x, LANES - 8, 1) * sin_a + pltpu.roll(x, 8, 1) * sin_b


def _mla_prep_kernel(zs_ref, qa_ref, kva_ref, wuq_ref, wk_ref, wv_ref, qn_ref, kn_ref, cos_ref, sa_ref, sb_ref, _, __,
                     q_ref, kk_ref, v_ref, ckv_ref, kr_ref, qf_scr, kf_scr):
    cq = zs_ref[:, 0:MLA_Q_RANK]
    ckv = zs_ref[:, MLA_Q_RANK:MLA_Q_RANK + MLA_KV_RANK]
    last = zs_ref[:, ZS_W - LANES:ZS_W]
    qf_scr[...] = _dot(_rms(cq, qa_ref[0]).astype(BF16), wuq_ref[0])
    ckvn = _rms(ckv, kva_ref[0])

    @pl.when(pl.program_id(0) < N_TILES_CTX)
    def _():
        ckv_ref[0, 0] = ckvn
        kr_ref[0, 0] = last[:, 0:MLA_ROPE]

    cb = ckvn.astype(BF16)
    kf_scr[...] = _dot(cb, wk_ref[0])
    v_ref[...] = _dot(cb, wv_ref[0]).astype(BF16)
    lane = lax.broadcasted_iota(jnp.int32, last.shape, 1)
    kr = jnp.where((lane >= MLA_NOPE) & (lane < MLA_QK), pltpu.roll(last, MLA_NOPE, 1), 0.0)
    is_latent = pl.program_id(0) >= N_TILES_CTX

    def heads(rotate):
        for h in range(MLA_HEADS):
            hs = slice(h * HEAD_PAD, (h + 1) * HEAD_PAD)
            q_ref[:, hs] = rotate(_rms(qf_scr[:, hs], qn_ref[0], n=MLA_QK)).astype(BF16)
            kk_ref[:, hs] = rotate(_rms(kf_scr[:, hs] + kr, kn_ref[0], n=MLA_QK)).astype(BF16)

    @pl.when(is_latent)
    def _():
        cos, sa, sb = cos_ref[...], sa_ref[...], sb_ref[...]
        heads(lambda x: _rope(x, cos, sa, sb))

    @pl.when(jnp.logical_not(is_latent))
    def _():
        heads(lambda x: x)


def _mla_prep(layer, zs, q_a_norm, kv_a_norm, w_uq_r, w_k_r, w_v_r, q_norm_p, k_norm_p, rope_tabs, new_ckv, new_kr):
    seq_blk = lambda w: pl.BlockSpec((1, 1, S_CTX, w), lambda i: (jnp.minimum(i, N_SEQ_CTX - 1), layer, 0, 0))
    lw = lambda shape: pl.BlockSpec((1,) + shape, lambda i: (layer,) + (0,) * len(shape))
    tab = pl.BlockSpec((TW, LANES), lambda i: (jnp.where(i < N_TILES_CTX, 0, 1 + (i - N_TILES_CTX) % TILES_PER_SMP_SEQ), 0))
    tok = lambda w: pl.BlockSpec((TW, w), lambda i: (i, 0))
    return pl.pallas_call(
        _mla_prep_kernel,
        grid=(N_TILES,),
        in_specs=[
            tok(ZS_W), lw((1, MLA_Q_RANK)), lw((1, MLA_KV_RANK)),
            lw((MLA_Q_RANK, MLA_HEADS * HEAD_PAD)), lw((MLA_KV_RANK, MLA_HEADS * HEAD_PAD)),
            lw((MLA_KV_RANK, MLA_HEADS * MLA_V)), lw((1, HEAD_PAD)), lw((1, HEAD_PAD)), tab, tab, tab,
            pl.BlockSpec(memory_space=pl.ANY), pl.BlockSpec(memory_space=pl.ANY),
        ],
        out_specs=[tok(MLA_HEADS * HEAD_PAD), tok(MLA_HEADS * HEAD_PAD), tok(MLA_HEADS * MLA_V),
                   seq_blk(MLA_KV_RANK), seq_blk(MLA_ROPE)],
        input_output_aliases={11: 3, 12: 4},
        out_shape=[
            jax.ShapeDtypeStruct((N_TOK, MLA_HEADS * HEAD_PAD), BF16),
            jax.ShapeDtypeStruct((N_TOK, MLA_HEADS * HEAD_PAD), BF16),
            jax.ShapeDtypeStruct((N_TOK, MLA_HEADS * MLA_V), BF16),
            jax.ShapeDtypeStruct(new_ckv.shape, F32),
            jax.ShapeDtypeStruct(new_kr.shape, F32),
        ],
        scratch_shapes=[pltpu.VMEM((TW, MLA_HEADS * HEAD_PAD), F32), pltpu.VMEM((TW, MLA_HEADS * HEAD_PAD), F32)],
        compiler_params=_cparams(("arbitrary",)),
        name="mla_prep",
    )(zs, q_a_norm, kv_a_norm, w_uq_r, w_k_r, w_v_r, q_norm_p, k_norm_p, *rope_tabs, new_ckv, new_kr)


def _cache_kv_kernel(ckv_ref, kr_ref, wk_ref, wv_ref, kn_ref, kk_ref, v_ref):
    cb = ckv_ref[...].astype(BF16)
    kf = _dot(cb, wk_ref[0])
    v_ref[...] = _dot(cb, wv_ref[0]).astype(BF16)
    kr = kr_ref[...]
    for h in range(MLA_HEADS):
        hs = slice(h * HEAD_PAD, (h + 1) * HEAD_PAD)
        kk_ref[:, hs] = _rms(kf[:, hs] + kr, kn_ref[0], n=MLA_QK).astype(BF16)


def _cache_kv(cache_ckv, cache_kr_pad, w_k_r, w_v_r, k_norm_p):
    lw = lambda shape: pl.BlockSpec((1,) + shape, lambda b, l: (l,) + (0,) * len(shape))
    blk = lambda w: pl.BlockSpec((None, None, PAST_LEN, w), lambda b, l: (b, l, 0, 0))
    return pl.pallas_call(
        _cache_kv_kernel,
        grid=(N_SEQ_SMP, DEPTH),
        in_specs=[blk(MLA_KV_RANK), blk(LANES), lw((MLA_KV_RANK, MLA_HEADS * HEAD_PAD)),
                  lw((MLA_KV_RANK, MLA_HEADS * MLA_V)), lw((1, HEAD_PAD))],
        out_specs=[blk(MLA_HEADS * HEAD_PAD), blk(MLA_HEADS * MLA_V)],
        out_shape=[
            jax.ShapeDtypeStruct((N_SEQ_SMP, DEPTH, PAST_LEN, MLA_HEADS * HEAD_PAD), BF16),
            jax.ShapeDtypeStruct((N_SEQ_SMP, DEPTH, PAST_LEN, MLA_HEADS * MLA_V), BF16),
        ],
        compiler_params=_cparams(("arbitrary", "arbitrary")),
        name="cache_kv",
    )(cache_ckv, cache_kr_pad, w_k_r, w_v_r, k_norm_p)


def _make_attn_kernel(n_src):
    scale = MLA_QK ** -0.5

    def kern(q_ref, *refs):
        o_ref = refs[-1]
        if n_src > 1:
            body(q_ref, *refs)
            return

        @pl.when(pl.program_id(0) < N_SEQ_CTX)
        def _():
            body(q_ref, *refs)

        @pl.when(pl.program_id(0) >= N_SEQ_CTX)
        def _():
            o_ref[...] = jnp.zeros_like(o_ref)

    def body(q_ref, *refs):
        o_ref = refs[-1]
        for h in range(MLA_HEADS):
            hs = slice(h * HEAD_PAD, (h + 1) * HEAD_PAD)
            vs = slice(h * MLA_V, (h + 1) * MLA_V)
            q = q_ref[:, hs]
            ss = [_dot_nt(q, refs[2 * j][:, hs]) * scale for j in range(n_src)]
            m = functools.reduce(jnp.maximum, [jnp.max(s, axis=1, keepdims=True) for s in ss])
            ps = [jnp.exp(s - m) for s in ss]
            l = functools.reduce(jnp.add, [jnp.sum(p, axis=1, keepdims=True) for p in ps])
            o = functools.reduce(jnp.add, [_dot(ps[j].astype(BF16), refs[2 * j + 1][:, vs]) for j in range(n_src)])
            o_ref[:, vs] = (o / l).astype(o_ref.dtype)

    return kern


def _attn_ctx(q, kk, v):
    blk = lambda w: pl.BlockSpec((S_CTX, w), lambda b: (b, 0))
    return pl.pallas_call(
        _make_attn_kernel(1),
        grid=(N_TILES,),
        in_specs=[blk(MLA_HEADS * HEAD_PAD), blk(MLA_HEADS * HEAD_PAD), blk(MLA_HEADS * MLA_V)],
        out_specs=blk(MLA_HEADS * MLA_V),
        out_shape=jax.ShapeDtypeStruct((N_TOK, MLA_HEADS * MLA_V), BF16),
        compiler_params=_cparams(("arbitrary",)),
        name="attn_ctx",
    )(q, kk, v)


def _attn_smp(layer, q, kk, v, kk_cache, v_cache, ctx_out):
    row0 = N_CTX // S_SMP
    nq = S_SMP // TQ
    seqb = lambda w: pl.BlockSpec((S_SMP, w), lambda b, i: (row0 + b, 0))
    cache = lambda w: pl.BlockSpec((None, None, PAST_LEN, w), lambda b, i: (b, layer, 0, 0))
    return pl.pallas_call(
        _make_attn_kernel(2),
        grid=(N_SEQ_SMP, nq),
        in_specs=[
            pl.BlockSpec((TQ, MLA_HEADS * HEAD_PAD), lambda b, i: (N_CTX // TQ + b * nq + i, 0)),
            seqb(MLA_HEADS * HEAD_PAD), seqb(MLA_HEADS * MLA_V),
            cache(MLA_HEADS * HEAD_PAD), cache(MLA_HEADS * MLA_V),
            pl.BlockSpec(memory_space=pl.ANY),
        ],
        out_specs=pl.BlockSpec((TQ, MLA_HEADS * MLA_V), lambda b, i: (N_CTX // TQ + b * nq + i, 0)),
        out_shape=jax.ShapeDtypeStruct((N_TOK, MLA_HEADS * MLA_V), BF16),
        input_output_aliases={5: 0},
        compiler_params=_cparams(("arbitrary", "arbitrary")),
        name="attn_smp",
    )(q, kk, v, kk_cache, v_cache, ctx_out)


def _merge_kernel(x_ref, oa_ref, ob_ref, zc_ref, zb_ref, sgn_ref, ws_ref, bs_ref, wb_ref, wo_ref, g1_ref, n2_ref,
                  sh2_ref, sc2_ref, wr_ref, br_ref,
                  xmid_ref, h2_ref, dest_ref, wsel_ref, cnt_ref, oc_scr, carry_scr):
    i = pl.program_id(0)

    @pl.when(i == 0)
    def _():
        carry_scr[...] = jnp.zeros_like(carry_scr)

    u = _gelu(zc_ref[:, 0:SG_WIDTH].astype(F32))
    vg = _gelu(zc_ref[:, SG_WIDTH:2 * SG_WIDTH].astype(F32))
    for g in range(SG_GROUPS):
        gs = slice(g * SG_DIM, (g + 1) * SG_DIM)
        vn = _rms(vg[:, gs], sgn_ref[0][:, gs]).astype(BF16)
        for c in range(TW // SG_CHUNK):
            cs = slice(c * SG_CHUNK, (c + 1) * SG_CHUNK)
            mixed = _dot(ws_ref[0, g], vn[cs, :]) + bs_ref[0][:, g:g + 1]
            oc_scr[cs, gs] = (u[cs, gs] * mixed).astype(BF16)

    acc = jnp.zeros((TW, D_MODEL), F32)
    for j, src in enumerate((oa_ref, ob_ref, oc_scr)):
        gate = jax.nn.sigmoid(zb_ref[:, j * D_MODEL:(j + 1) * D_MODEL].astype(F32))
        acc = acc + gate * _dot(src[...], wb_ref[0, j])
    xm = x_ref[...] + g1_ref[0, 0] * _dot(acc.astype(BF16), wo_ref[0])
    xmid_ref[...] = xm
    h2 = _rms(xm, n2_ref[0]) * (1.0 + sc2_ref[0, 0]) + sh2_ref[0, 0]
    for c in range(SC_SPLIT):
        h2_ref[c] = _pack_pairs(h2[:, 2 * c * SC_ROW:(2 * c + 1) * SC_ROW], h2[:, (2 * c + 1) * SC_ROW:(2 * c + 2) * SC_ROW])

    h_hi = h2.astype(BF16)
    h_lo = (h2 - h_hi.astype(F32)).astype(BF16)
    p_hi = _dot(h_hi, wr_ref[0])
    logits = p_hi[:, 0:LANES] + p_hi[:, LANES:2 * LANES] + _dot(h_lo, wr_ref[0, :, 0:LANES]) + br_ref[0]
    lane = lax.broadcasted_iota(jnp.int32, logits.shape, 1)
    hits, exps = [], []
    sel = jnp.zeros(logits.shape, F32)
    denom = jnp.zeros((TW, 1), F32)
    top = None
    for _ in range(TOP_K):
        m = jnp.max(logits, axis=1, keepdims=True)
        idx = jnp.min(jnp.where(logits == m, lane, LANES), axis=1, keepdims=True)
        hit = lane == idx
        top = m if top is None else top
        hits.append(hit)
        exps.append(jnp.exp(m - top))
        sel = jnp.where(hit, 1.0, sel)
        denom = denom + exps[-1]
        logits = jnp.where(hit, -jnp.inf, logits)

    r_i = lax.broadcasted_iota(jnp.int32, (TW, TW), 0)
    c_i = lax.broadcasted_iota(jnp.int32, (TW, TW), 1)
    carry = carry_scr[0:1, :]
    rank = _dot(jnp.where(c_i < r_i, 1.0, 0.0).astype(BF16), sel.astype(BF16)) + carry
    new_carry = carry + jnp.sum(sel, axis=0, keepdims=True)
    carry_scr[...] = jnp.broadcast_to(new_carry, (SUBLANES, LANES))
    cnt_ref[...] = jnp.broadcast_to(new_carry, (SUBLANES, LANES))
    slot = rank + lane.astype(F32) * float(EXPERT_CAP)
    dmat = jnp.zeros(logits.shape, F32)
    wmat = jnp.zeros(logits.shape, F32)
    for k in range(TOP_K):
        dk = jnp.sum(jnp.where(hits[k], slot, 0.0), axis=1, keepdims=True)
        dmat = jnp.where(lane == k, dk, dmat)
        wmat = jnp.where(lane == k, exps[k] / denom, wmat)
    dest_ref[...] = dmat.T[0:SUBLANES, :].astype(jnp.int32)
    wsel_ref[...] = wmat


def _merge(layer, x, oa, ob, zc, zb, sg_norm, w_sp, b_sp, w_branch, w_out, mod, norm2, w_router_p, b_router_p):
    lw = lambda shape: pl.BlockSpec((1,) + shape, lambda i: (layer,) + (0,) * len(shape))
    tok = lambda w: pl.BlockSpec((TW, w), lambda i: (i, 0))
    return pl.pallas_call(
        _merge_kernel,
        grid=(N_TILES,),
        in_specs=[
            tok(D_MODEL), tok(ML_WIDTH), tok(MLA_HEADS * MLA_V), tok(ZC_W), tok(ZB_W),
            lw((1, SG_WIDTH)), lw((SG_GROUPS, SG_CHUNK, SG_CHUNK)), lw((SG_CHUNK, LANES)),
            lw((N_BRANCH, ML_WIDTH, D_MODEL)), lw((D_MODEL, D_MODEL)),
            _mod_spec(layer, 2), lw((1, D_MODEL)), _mod_spec(layer, 3), _mod_spec(layer, 4),
            lw((D_MODEL, 2 * LANES)), lw((1, LANES)),
        ],
        out_specs=[tok(D_MODEL), pl.BlockSpec((SC_SPLIT, TW, SC_ROW), lambda i: (0, i, 0)),
                   pl.BlockSpec((SUBLANES, TW), lambda i: (0, i)), tok(LANES),
                   pl.BlockSpec((SUBLANES, LANES), lambda i: (0, 0))],
        out_shape=[
            jax.ShapeDtypeStruct((N_TOK, D_MODEL), F32),
            jax.ShapeDtypeStruct((SC_SPLIT, N_TOK, SC_ROW), jnp.uint32),
            jax.ShapeDtypeStruct((SUBLANES, N_TOK), jnp.int32),
            jax.ShapeDtypeStruct((N_TOK, LANES), F32),
            jax.ShapeDtypeStruct((SUBLANES, LANES), F32),
        ],
        scratch_shapes=[pltpu.VMEM((TW, SG_WIDTH), BF16), pltpu.VMEM((SUBLANES, LANES), F32)],
        compiler_params=_cparams(("arbitrary",)),
        name="merge_router",
    )(x, oa, ob, zc, zb, sg_norm, w_sp, b_sp, w_branch, w_out, mod, norm2, mod, mod, w_router_p, b_router_p)


def _sc_mesh():
    return plsc.VectorSubcoreMesh(core_axis_name="core", subcore_axis_name="subcore")


def _sc_scatter_rows(x, idxs, n_rows):
    @pl.kernel(out_type=jax.ShapeDtypeStruct((n_rows, SC_ROW), x.dtype), mesh=_sc_mesh(), scratch_types=[])
    def scatter(x_hbm, *refs):
        o_hbm = refs[-1]

        def body(x_vmem, *i_vmems):
            for i_vmem in i_vmems:
                pltpu.sync_copy(x_vmem, o_hbm.at[i_vmem.at[0]])

        pltpu.emit_pipeline(
            body,
            grid=(x.shape[0] // SC_WIN,),
            in_specs=[pl.BlockSpec((SC_WIN, SC_ROW), lambda i: (i, 0))]
            + [pl.BlockSpec((1, SC_WIN), lambda i: (0, i))] * len(idxs),
            out_specs=[],
            core_axis_name=("core", "subcore"),
            dimension_semantics=(pltpu.PARALLEL,),
        )(x_hbm, *refs[:-1])

    return scatter(x, *idxs)


def _sc_gather_rows(x, idx):
    m = idx.shape[1]

    @pl.kernel(out_type=jax.ShapeDtypeStruct((m, SC_ROW), x.dtype), mesh=_sc_mesh())
    def gather(x_hbm, i_hbm, o_hbm):
        def body(i_vmem, o_vmem):
            pltpu.sync_copy(x_hbm.at[i_vmem.at[0]], o_vmem)

        pltpu.emit_pipeline(
            body,
            grid=(m // SC_WIN,),
            in_specs=[pl.BlockSpec((1, SC_WIN), lambda i: (0, i))],
            out_specs=[pl.BlockSpec((SC_WIN, SC_ROW), lambda i: (i, 0))],
            core_axis_name=("core", "subcore"),
            dimension_semantics=(pltpu.PARALLEL,),
        )(i_hbm, o_hbm)

    return gather(x, idx)


STEP_VALID, STEP_FIRST, STEP_HAS_NEXT, STEP_FULL = 1, 2, 4, 8


def _moe_ffn_kernel(layer, be_ref, nx_ref, br_ref, fl_ref, xs_ref, b1_ref, b2_ref, w1_hbm, w2_hbm, y_ref,
                    w1f, w2f, w1b, w2b, sem):
    g = pl.program_id(0)
    flags = fl_ref[g]

    def weight_copies(e):
        return (pltpu.make_async_copy(w1_hbm.at[layer, e], w1f, sem.at[0]),
                pltpu.make_async_copy(w2_hbm.at[layer, e], w2f, sem.at[1]))

    @pl.when(g == 0)
    def _():
        for cp in weight_copies(be_ref[0]):
            cp.start()

    @pl.when((flags & STEP_FIRST) != 0)
    def _():
        for cp in weight_copies(be_ref[g]):
            cp.wait()
        w1b[...] = w1f[...].astype(BF16)
        w2b[...] = w2f[...].astype(BF16)

        @pl.when((flags & STEP_HAS_NEXT) != 0)
        def _():
            for cp in weight_copies(nx_ref[g]):
                cp.start()

    def ffn(n_rows):
        halves = [h.astype(BF16) for c in range(SC_SPLIT) for h in _unpack_pairs(xs_ref[c, 0:n_rows, :])]
        g1 = _dot(jnp.concatenate(halves, axis=1), w1b[...]) + b1_ref[0, 0]
        gate = jnp.minimum(g1[:, :D_EXPERT], SWIGLU_LIMIT)
        up = jnp.clip(g1[:, D_EXPERT:], -SWIGLU_LIMIT, SWIGLU_LIMIT)
        act = gate * jax.nn.sigmoid(SWIGLU_ALPHA * gate) * (up + 1.0)
        y = _dot(act.astype(BF16), w2b[...]) + b2_ref[0, 0]
        for c in range(SC_SPLIT):
            y_ref[c, 0:n_rows, :] = _pack_pairs(
                y[:, 2 * c * SC_ROW:(2 * c + 1) * SC_ROW], y[:, (2 * c + 1) * SC_ROW:(2 * c + 2) * SC_ROW])

    pl.when((flags & (STEP_VALID | STEP_FULL)) == STEP_VALID)(lambda: ffn(SLOT_CHUNK))
    pl.when((flags & STEP_FULL) != 0)(lambda: ffn(FFN_BLOCK))


def _moe_ffn(layer, xs, plan, w1, b1, w2, b2):
    eb = lambda c: pl.BlockSpec((1, 1, 1, c), lambda g, be, nx, br, fl: (layer, be[g], 0, 0))
    rows = pl.BlockSpec((SC_SPLIT, FFN_BLOCK, SC_ROW), lambda g, be, nx, br, fl: (0, br[g], 0))
    hbm = pl.BlockSpec(memory_space=pl.ANY)
    grid_spec = pltpu.PrefetchScalarGridSpec(
        num_scalar_prefetch=4,
        grid=(N_CHUNK_STEPS,),
        in_specs=[rows, eb(2 * D_EXPERT), eb(D_MODEL), hbm, hbm],
        out_specs=rows,
        scratch_shapes=[
            pltpu.VMEM((D_MODEL, 2 * D_EXPERT), F32), pltpu.VMEM((D_EXPERT, D_MODEL), F32),
            pltpu.VMEM((D_MODEL, 2 * D_EXPERT), BF16), pltpu.VMEM((D_EXPERT, D_MODEL), BF16),
            pltpu.SemaphoreType.DMA((2,)),
        ],
    )
    return pl.pallas_call(
        functools.partial(_moe_ffn_kernel, layer),
        grid_spec=grid_spec,
        out_shape=jax.ShapeDtypeStruct(xs.shape, xs.dtype),
        compiler_params=_cparams(("arbitrary",)),
        name="moe_ffn",
    )(*plan, xs, b1, b2, w1, w2)


def _chunk_plan_kernel(cnt_ref, be_ref, nx_ref, br_ref, fl_ref):
    def expert(e, carry):
        step0, prev_first = carry
        c = cnt_ref[e]
        n_blk = (c + FFN_BLOCK - 1) // FFN_BLOCK

        def block(j, _):
            s = step0 + j
            be_ref[s] = e
            nx_ref[s] = e
            br_ref[s] = e * (EXPERT_CAP // FFN_BLOCK) + j
            fl_ref[s] = (STEP_VALID + jnp.where(j == 0, STEP_FIRST, 0)
                         + jnp.where(c - j * FFN_BLOCK > SLOT_CHUNK, STEP_FULL, 0))
            return 0

        lax.fori_loop(0, n_blk, block, 0)

        @pl.when((n_blk > 0) & (prev_first >= 0))
        def _():
            nx_ref[prev_first] = e
            fl_ref[prev_first] = fl_ref[prev_first] + STEP_HAS_NEXT

        return step0 + n_blk, jnp.where(n_blk > 0, step0, prev_first)

    used, _ = lax.fori_loop(0, N_EXPERTS, expert, (jnp.int32(0), jnp.int32(-1)))

    def idle(s, _):
        be_ref[s] = be_ref[used - 1]
        nx_ref[s] = be_ref[used - 1]
        br_ref[s] = br_ref[used - 1]
        fl_ref[s] = 0
        return 0

    lax.fori_loop(used, N_CHUNK_STEPS, idle, 0)


def _chunk_plan(cnt):
    smem = pl.BlockSpec(memory_space=pltpu.SMEM)
    return pl.pallas_call(
        _chunk_plan_kernel,
        in_specs=[smem],
        out_specs=[smem] * 4,
        out_shape=[jax.ShapeDtypeStruct((N_CHUNK_STEPS,), jnp.int32)] * 4,
        name="chunk_plan",
    )(cnt)


def _combine_kernel(x_ref, yg_ref, w_ref, g_ref, *o_refs):
    def emit(o_ref):
        w = w_ref[...]
        for c in range(SC_SPLIT):
            parts = [_unpack_pairs(yg_ref[k, c]) for k in range(TOP_K)]
            for half in range(2):
                cs = slice((2 * c + half) * SC_ROW, (2 * c + half + 1) * SC_ROW)
                acc = w[:, 0:1] * parts[0][half]
                for k in range(1, TOP_K):
                    acc = acc + w[:, k:k + 1] * parts[k][half]
                o_ref[:, cs] = x_ref[:, cs] + g_ref[0, 0][:, cs] * acc

    if len(o_refs) == 1:
        emit(o_refs[0])
    else:
        pl.when(pl.program_id(0) < N_TILES_CTX)(lambda: emit(o_refs[0]))
        pl.when(pl.program_id(0) >= N_TILES_CTX)(lambda: emit(o_refs[1]))


def _combine(layer, xmid, yg, wsel, mod, split_out):
    tok = lambda w: pl.BlockSpec((TW, w), lambda i: (i, 0))
    if split_out:
        out_specs = [pl.BlockSpec((TW, D_MODEL), lambda i: (jnp.minimum(i, N_TILES_CTX - 1), 0)),
                     pl.BlockSpec((TW, D_MODEL), lambda i: (jnp.maximum(i - N_TILES_CTX, 0), 0))]
        out_shape = [jax.ShapeDtypeStruct((N_CTX, D_MODEL), F32), jax.ShapeDtypeStruct((N_SMP, D_MODEL), F32)]
    else:
        out_specs, out_shape = tok(D_MODEL), jax.ShapeDtypeStruct((N_TOK, D_MODEL), F32)
    return pl.pallas_call(
        _combine_kernel,
        grid=(N_TILES,),
        in_specs=[tok(D_MODEL), pl.BlockSpec((TOP_K, SC_SPLIT, TW, SC_ROW), lambda i: (0, 0, i, 0)), tok(LANES),
                  _mod_spec(layer, 5)],
        out_specs=out_specs,
        out_shape=out_shape,
        compiler_params=_cparams(("arbitrary",)),
        name="combine",
    )(xmid, yg, wsel, mod)


def _moe(layer, xmid, h2, dest, wsel, cnt, mod, w1, b1, w2, b2, split_out):
    n_slots = N_EXPERTS * EXPERT_CAP
    idx = dest[0:TOP_K][:, None, :] + (jnp.arange(SC_SPLIT, dtype=jnp.int32) * n_slots)[None, :, None]
    idx = idx.reshape(TOP_K, 1, SC_SPLIT * N_TOK)
    xs = _sc_scatter_rows(h2.reshape(SC_SPLIT * N_TOK, SC_ROW), [idx[k] for k in range(TOP_K)], SC_SPLIT * n_slots)
    plan = _chunk_plan(cnt[0, :N_EXPERTS].astype(jnp.int32))
    y = _moe_ffn(layer, xs.reshape(SC_SPLIT, n_slots, SC_ROW), plan, w1, b1, w2, b2)
    yg = _sc_gather_rows(y.reshape(SC_SPLIT * n_slots, SC_ROW), idx.reshape(1, TOP_K * SC_SPLIT * N_TOK))
    return _combine(layer, xmid, yg.reshape(TOP_K, SC_SPLIT, N_TOK, SC_ROW), wsel, mod, split_out)


def _rope_tables():
    pos = np.arange(S_SMP)
    half = MLA_ROPE // 2
    inv_freq = (ROPE_THETA ** (-(np.arange(0, half, 2, dtype=np.float32) / np.float32(half)))).astype(np.float32)
    angs = [((pos // GRID_W).astype(np.float32)[:, None] * inv_freq[None, :]).astype(np.float32),
            ((pos % GRID_W).astype(np.float32)[:, None] * inv_freq[None, :]).astype(np.float32)]
    nf = half // 2
    cos = np.ones((TW + S_SMP, LANES), np.float32)
    sin_a = np.zeros((TW + S_SMP, LANES), np.float32)
    sin_b = np.zeros((TW + S_SMP, LANES), np.float32)
    for axis, ang in enumerate(angs):
        base = MLA_NOPE + axis * half
        c, s = np.cos(ang.astype(np.float64)), np.sin(ang.astype(np.float64))
        cos[TW:, base:base + nf] = c
        cos[TW:, base + nf:base + half] = c
        sin_a[TW:, base:base + nf] = -s
        sin_b[TW:, base + nf:base + half] = s
    return jnp.asarray(cos), jnp.asarray(sin_a), jnp.asarray(sin_b)


def _pad_last(a, width):
    return jnp.pad(a, [(0, 0)] * (a.ndim - 1) + [(0, width - a.shape[-1])])


def kernel(x_prompt, x_sample, cache_mla_ckv, cache_mla_krope, state_mlstm_C, state_mlstm_n, state_mlstm_m, c, c_ctx, norm1, norm2, w_ada, b_ada, w_in, b_mlstm_gates, mlstm_norm, mla_q_a_norm, mla_kv_a_norm, w_uq, w_ukv, mla_q_norm, mla_k_norm, sg_norm, w_spatial, b_spatial, w_branch, w_out, w_router, b_router, w_exp1, b_exp1, w_exp2, b_exp2):
    w_in_r = _w_in_prep(w_in)
    w_uq_r = _pad_last(w_uq.reshape(DEPTH, MLA_Q_RANK, MLA_HEADS, MLA_QK), HEAD_PAD).reshape(
        DEPTH, MLA_Q_RANK, MLA_HEADS * HEAD_PAD).astype(BF16)
    w_ukv4 = w_ukv.reshape(DEPTH, MLA_KV_RANK, MLA_HEADS, MLA_NOPE + MLA_V)
    w_k_r = _pad_last(w_ukv4[..., :MLA_NOPE], HEAD_PAD).reshape(DEPTH, MLA_KV_RANK, MLA_HEADS * HEAD_PAD).astype(BF16)
    w_v_r = w_ukv4[..., MLA_NOPE:].reshape(DEPTH, MLA_KV_RANK, MLA_HEADS * MLA_V).astype(BF16)
    q_norm_p = _pad_last(mla_q_norm, HEAD_PAD).reshape(DEPTH, 1, HEAD_PAD)
    k_norm_p = _pad_last(mla_k_norm, HEAD_PAD).reshape(DEPTH, 1, HEAD_PAD)
    b_gates_p = jnp.pad(b_mlstm_gates, ((0, 0), (GATE_LANE0, LANES - GATE_LANE0 - 4 * ML_HEADS))).reshape(DEPTH, 1, LANES)
    b_sp = _pad_last(jnp.swapaxes(b_spatial, 1, 2), LANES)
    w_router_p = _pad_last(w_router, LANES)
    w_router_hi = w_router_p.astype(BF16)
    w_router_p = jnp.concatenate([w_router_hi, (w_router_p - w_router_hi.astype(F32)).astype(BF16)], axis=-1)
    b_router_p = jnp.pad(b_router, ((0, 0), (0, LANES - N_EXPERTS)), constant_values=-1e30).reshape(DEPTH, 1, LANES)
    r3 = lambda a: a.reshape(DEPTH, 1, a.shape[-1])
    cache_kr_pad = jnp.pad(cache_mla_krope, ((0, 0), (0, 0), (0, 0), (MLA_NOPE, LANES - MLA_QK)))
    rope_tabs = _rope_tables()

    cvec = jnp.concatenate([c_ctx[None, :], c, jnp.zeros((SUBLANES - 1 - N_SEQ_SMP, D_MODEL), F32)], axis=0)
    mod = _adaln(cvec, w_ada, b_ada).reshape(DEPTH, SUBLANES, 1, 6 * D_MODEL)
    b1 = b_exp1.reshape(DEPTH, N_EXPERTS, 1, 2 * D_EXPERT)
    b2 = b_exp2.reshape(DEPTH, N_EXPERTS, 1, D_MODEL)
    kk_cache, v_cache = _cache_kv(cache_mla_ckv, cache_kr_pad, w_k_r, w_v_r, k_norm_p)

    x = jnp.concatenate([x_prompt.reshape(N_CTX, D_MODEL), x_sample.reshape(N_SMP, D_MODEL)], axis=0)
    new_ckv = jnp.zeros((N_SEQ_CTX, DEPTH, S_CTX, MLA_KV_RANK), F32)
    new_kr = jnp.zeros((N_SEQ_CTX, DEPTH, S_CTX, MLA_ROPE), F32)
    states = (jnp.zeros((N_SEQ_CTX, DEPTH, 2, ML_HEADS, ML_DIM, ML_DIM), F32),
              jnp.zeros((N_SEQ_CTX, DEPTH, 2, ML_HEADS, ML_DIM), F32),
              jnp.zeros((N_SEQ_CTX, DEPTH, 2 * ML_HEADS, LANES), F32))
    for l in range(DEPTH):
        za, zs, zc, zb = _inproj(l, x, r3(norm1), mod, w_in_r)
        oa, *states = _mlstm(l, za, zs, b_gates_p, r3(mlstm_norm), states=states)
        (oa,) = _mlstm(l, za, zs, b_gates_p, r3(mlstm_norm), init=(state_mlstm_C, state_mlstm_n, state_mlstm_m), ctx_out=oa)
        q, kk, v, new_ckv, new_kr = _mla_prep(l, zs, r3(mla_q_a_norm), r3(mla_kv_a_norm), w_uq_r, w_k_r, w_v_r,
                                              q_norm_p, k_norm_p, rope_tabs, new_ckv, new_kr)
        ob = _attn_smp(l, q, kk, v, kk_cache, v_cache, _attn_ctx(q, kk, v))
        xmid, h2, dest, wsel, cnt = _merge(
            l, x, oa, ob, zc, zb, r3(sg_norm), w_spatial.astype(BF16), b_sp, w_branch.astype(BF16), w_out.astype(BF16),
            mod, r3(norm2), w_router_p, b_router_p)
        x = _moe(l, xmid, h2, dest, wsel, cnt, mod, w_exp1, b1, w_exp2, b2, split_out=l == DEPTH - 1)
    y_ctx, y_smp = x
    return (
        y_ctx.reshape(N_SEQ_CTX, S_CTX, D_MODEL),
        y_smp.reshape(N_SEQ_SMP, S_SMP, D_MODEL),
        new_ckv,
        new_kr,
        states[0],
        states[1],
        states[2][:, :, :, 0].reshape(N_SEQ_CTX, DEPTH, 2, ML_HEADS),
    )
```

```python
import functools

import numpy as np
import jax
import jax.numpy as jnp
from jax import lax
from jax.experimental import pallas as pl
from jax.experimental.pallas import tpu as pltpu
from jax.experimental.pallas import tpu_sc as plsc

F32 = jnp.float32
BF16 = jnp.bfloat16
HI = lax.Precision.HIGHEST

D_MODEL = 1024
N_SEQ_CTX, S_CTX = 32, 256
N_SEQ_SMP, S_SMP = 2, 1024
DEPTH = 4
PAST_LEN = 512
GRID_W = 64
EPS = 1e-6
ML_HEADS, ML_DIM = 4, 128
ML_WIDTH = ML_HEADS * ML_DIM
MLA_HEADS, MLA_NOPE, MLA_ROPE, MLA_V = 8, 64, 32, 64
MLA_QK = MLA_NOPE + MLA_ROPE
MLA_Q_RANK, MLA_KV_RANK = 256, 128
ROPE_THETA = 10000.0
SG_GROUPS, SG_DIM, SG_CHUNK = 4, 128, 128
SG_WIDTH = SG_GROUPS * SG_DIM
N_BRANCH = 3
N_EXPERTS, TOP_K, D_EXPERT = 32, 4, 1024
SWIGLU_LIMIT, SWIGLU_ALPHA = 7.0, 1.702

N_CTX = N_SEQ_CTX * S_CTX
N_SMP = N_SEQ_SMP * S_SMP
N_TOK = N_CTX + N_SMP

LANES = 128
SUBLANES = 8
VMEM_LIMIT = 56 * 1024 * 1024

TW = 512
N_TILES = N_TOK // TW
N_TILES_CTX = N_CTX // TW
TILES_PER_SMP_SEQ = S_SMP // TW
N_SEQ_BLOCKS = N_TOK // S_CTX
HEAD_PAD = LANES
TQ = 256
EXPERT_CAP = N_TOK
SLOT_CHUNK = 256
FFN_BLOCK = 2 * SLOT_CHUNK
N_CHUNK_STEPS = N_TOK * TOP_K // FFN_BLOCK + N_EXPERTS
SC_ROW = 256
SC_SPLIT = D_MODEL // (2 * SC_ROW)
SC_WIN = 128

ZA_W = 4 * ML_WIDTH
ZS_W = 512
ZC_W = 2 * SG_WIDTH
ZB_W = N_BRANCH * D_MODEL
ZIN_W = ZA_W + ZS_W + ZC_W + ZB_W
GATE_LANE0 = MLA_ROPE


def _cparams(sem):
    return pltpu.CompilerParams(dimension_semantics=sem, vmem_limit_bytes=VMEM_LIMIT)


def _mod_row(i):
    return jnp.where(i < N_TILES_CTX, 0, 1 + (i - N_TILES_CTX) // TILES_PER_SMP_SEQ)


def _rms(x, g, n=None):
    ms = jnp.sum(x * x, axis=-1, keepdims=True) * (1.0 / (n or x.shape[-1]))
    return x * lax.rsqrt(ms + EPS) * g


def _gelu(x):
    return 0.5 * x * (1.0 + jnp.tanh(0.7978845608028654 * (x + 0.044715 * (x * x * x))))


def _pack_pairs(lo, hi):
    lo_bits = lax.bitcast_convert_type(lo.astype(BF16).astype(F32), jnp.uint32)
    hi_bits = lax.bitcast_convert_type(hi.astype(BF16).astype(F32), jnp.uint32)
    return (lo_bits >> 16) | (hi_bits & jnp.uint32(0xFFFF0000))


def _unpack_pairs(u):
    return (lax.bitcast_convert_type(u << 16, F32), lax.bitcast_convert_type(u & jnp.uint32(0xFFFF0000), F32))


def _dot(a, b):
    return jnp.dot(a, b, preferred_element_type=F32)


def _dot_nt(a, b):
    return lax.dot_general(a, b, (((1,), (1,)), ((), ())), preferred_element_type=F32)


def _adaln_kernel(c_ref, w_ref, b_ref, o_ref):
    c = c_ref[...]
    s = c * jax.nn.sigmoid(c)
    o_ref[0] = jnp.dot(s, w_ref[0], precision=HI, preferred_element_type=F32) + b_ref[0]


def _adaln(cvec, w_ada, b_ada):
    nchunk = 4
    cw = 6 * D_MODEL // nchunk
    return pl.pallas_call(
        _adaln_kernel,
        grid=(DEPTH, nchunk),
        in_specs=[
            pl.BlockSpec((SUBLANES, D_MODEL), lambda l, j: (0, 0)),
            pl.BlockSpec((1, D_MODEL, cw), lambda l, j: (l, 0, j)),
            pl.BlockSpec((1, 1, cw), lambda l, j: (l, 0, j)),
        ],
        out_specs=pl.BlockSpec((1, SUBLANES, cw), lambda l, j: (l, 0, j)),
        out_shape=jax.ShapeDtypeStruct((DEPTH, SUBLANES, 6 * D_MODEL), F32),
        compiler_params=_cparams(("arbitrary", "arbitrary")),
        name="adaln",
    )(cvec, w_ada, b_ada.reshape(DEPTH, 1, 6 * D_MODEL))


IN_SPLITS = (ML_WIDTH, ML_WIDTH, ML_WIDTH, ML_WIDTH, 4 * ML_HEADS, MLA_Q_RANK, MLA_KV_RANK, MLA_ROPE, SG_WIDTH, SG_WIDTH,
             N_BRANCH * D_MODEL)
IN_OFFS = tuple(int(v) for v in np.cumsum((0,) + IN_SPLITS))
D_IN = IN_OFFS[-1]
W_PREP_ROWS = 256
W_PREP_COLS = 512


def _w_in_prep_kernel(wt_ref, o_ref):
    o = IN_OFFS

    def put(c0, rows):
        o_ref[0, :, c0:c0 + W_PREP_COLS] = rows.T.astype(BF16)

    for c0 in range(0, ZA_W, W_PREP_COLS):
        scale = ML_DIM ** -0.5 if o[1] <= c0 < o[2] else 1.0
        put(c0, wt_ref[0, c0:c0 + W_PREP_COLS, :] * scale)
    pad = jnp.zeros((ZS_W - (o[8] - o[4]), W_PREP_ROWS), F32)
    put(ZA_W, jnp.concatenate([wt_ref[0, o[5]:o[8], :], wt_ref[0, o[4]:o[5], :], pad], axis=0))
    for c0 in range(ZA_W + ZS_W, ZIN_W, W_PREP_COLS):
        src = c0 - (ZA_W + ZS_W) + o[8]
        put(c0, wt_ref[0, src:src + W_PREP_COLS, :])


def _w_in_prep(w_in):
    return pl.pallas_call(
        _w_in_prep_kernel,
        grid=(DEPTH, D_MODEL // W_PREP_ROWS),
        in_specs=[pl.BlockSpec((1, D_IN, W_PREP_ROWS), lambda l, r: (l, 0, r))],
        out_specs=pl.BlockSpec((1, W_PREP_ROWS, ZIN_W), lambda l, r: (l, r, 0)),
        out_shape=jax.ShapeDtypeStruct((DEPTH, D_MODEL, ZIN_W), BF16),
        compiler_params=_cparams(("arbitrary", "arbitrary")),
        name="w_in_prep",
    )(jnp.swapaxes(w_in, 1, 2))


def _inproj_kernel(x_ref, g_ref, sh_ref, sc_ref, w_ref, za_ref, zs_ref, zc_ref, zb_ref):
    h = _rms(x_ref[...], g_ref[0]) * (1.0 + sc_ref[0, 0]) + sh_ref[0, 0]
    hb = h.astype(BF16)
    za_ref[...] = _dot(hb, w_ref[0, :, 0:ZA_W]).astype(BF16)
    zs_ref[...] = _dot(hb, w_ref[0, :, ZA_W:ZA_W + ZS_W])
    zc_ref[...] = _dot(hb, w_ref[0, :, ZA_W + ZS_W:ZA_W + ZS_W + ZC_W]).astype(BF16)
    zb_ref[...] = _dot(hb, w_ref[0, :, ZA_W + ZS_W + ZC_W:ZIN_W]).astype(BF16)


def _mod_spec(layer, k):
    return pl.BlockSpec((1, 1, 1, D_MODEL), lambda i: (layer, _mod_row(i), 0, k))


def _inproj(layer, x, norm1, mod, w_in_r):
    tok = lambda w: pl.BlockSpec((TW, w), lambda i: (i, 0))
    return pl.pallas_call(
        _inproj_kernel,
        grid=(N_TILES,),
        in_specs=[
            tok(D_MODEL),
            pl.BlockSpec((1, 1, D_MODEL), lambda i: (layer, 0, 0)),
            _mod_spec(layer, 0),
            _mod_spec(layer, 1),
            pl.BlockSpec((1, D_MODEL, ZIN_W), lambda i: (layer, 0, 0)),
        ],
        out_specs=[tok(ZA_W), tok(ZS_W), tok(ZC_W), tok(ZB_W)],
        out_shape=[
            jax.ShapeDtypeStruct((N_TOK, ZA_W), BF16),
            jax.ShapeDtypeStruct((N_TOK, ZS_W), F32),
            jax.ShapeDtypeStruct((N_TOK, ZC_W), BF16),
            jax.ShapeDtypeStruct((N_TOK, ZB_W), BF16),
        ],
        compiler_params=_cparams(("arbitrary",)),
        name="inproj",
    )(x, norm1, mod, mod, w_in_r)


def _make_mlstm_kernel(seq, layer, has_init):
    nq = seq // TQ
    lane_if, lane_ff, lane_ib, lane_fb = (GATE_LANE0 + ML_HEADS * j for j in range(4))

    def kern(*refs):
        if has_init:
            body(*refs)
            return
        out = refs[10]
        b = pl.program_id(0)

        @pl.when(b < N_SEQ_CTX)
        def _():
            body(*refs)

        @pl.when(b >= N_SEQ_CTX)
        def _():
            out[...] = jnp.zeros_like(out)

    def body(*refs):
        if has_init:
            m0_ref, zq, zk, zv, zo, gz, bg, nrm, c0_ref, n0_ref, _, out, bp_scr, bs_scr = refs
        else:
            zq, zk, zv, zo, gz, bg, nrm, _, _, _, out, cf_ref, nf_ref, mf_ref, bp_scr, bs_scr = refs
        b = pl.program_id(0)
        g = gz[...] + bg[0]
        lane = lax.broadcasted_iota(jnp.int32, g.shape, 1)
        is_forget = ((lane >= lane_ff) & (lane < lane_ib)) | ((lane >= lane_fb) & (lane < lane_fb + ML_HEADS))
        log_sig = jnp.minimum(g, 0.0) - jnp.log1p(jnp.exp(-jnp.abs(g)))
        a = jnp.where(is_forget, log_sig, g)
        r_i = lax.broadcasted_iota(jnp.int32, (seq, seq), 0)
        c_i = lax.broadcasted_iota(jnp.int32, (seq, seq), 1)
        ltri = (c_i <= r_i).astype(F32)
        bp = jnp.dot(ltri, a, precision=HI, preferred_element_type=F32)
        bs = bp[seq - 1:seq, :] - bp + a
        bp_scr[...] = bp
        bs_scr[...] = bs
        eye = (lax.broadcasted_iota(jnp.int32, (LANES, LANES), 0)
               == lax.broadcasted_iota(jnp.int32, (LANES, LANES), 1)).astype(F32)
        tr = lambda x: lax.dot_general(eye, x, (((1,), (1,)), ((), ())), precision=HI, preferred_element_type=F32)
        if has_init:
            tr = lambda x: x.T
        a_t, bp_t, bs_t = tr(a), tr(bp), tr(bs)

        for h in range(ML_HEADS):
            hs = slice(h * ML_DIM, (h + 1) * ML_DIM)
            k = zk[:, hs]
            v = zv[:, hs]
            first_lane = lax.broadcasted_iota(jnp.int32, (seq, ML_DIM), 1) == 0
            v_aug = jnp.concatenate([v, jnp.where(first_lane, 1.0, 0.0).astype(BF16)], axis=1)
            rows = (
                a_t[lane_if + h:lane_if + h + 1, :] - bp_t[lane_ff + h:lane_ff + h + 1, :],
                a_t[lane_ib + h:lane_ib + h + 1, :] - bs_t[lane_fb + h:lane_fb + h + 1, :],
            )
            col_refs = ((bp_scr, lane_ff + h), (bs_scr, lane_fb + h))
            if has_init:
                m0 = tuple(m0_ref[((b * DEPTH + layer) * 2 + dr) * ML_HEADS + h] for dr in range(2))
                c0 = tuple(c0_ref[0, 0, dr, h].astype(BF16) for dr in range(2))
                n0 = tuple(jnp.broadcast_to(n0_ref[0, 0, dr, h:h + 1, :], (ML_DIM, ML_DIM)).astype(BF16) for dr in range(2))
            else:
                m0 = (0.0, 0.0)

            def qblock(qi, carry):
                q0 = pl.multiple_of(qi * TQ, TQ)
                qb = zq[pl.ds(q0, TQ), hs]
                sc = _dot_nt(qb, k)
                t_idx = q0 + lax.broadcasted_iota(jnp.int32, (TQ, seq), 0)
                s_idx = lax.broadcasted_iota(jnp.int32, (TQ, seq), 1)
                hsum = jnp.zeros((TQ, ML_DIM), F32)
                for dr in range(2):
                    cref, cl = col_refs[dr]
                    col = cref[pl.ds(q0, TQ), cl:cl + 1]
                    mask = (s_idx <= t_idx) if dr == 0 else (s_idx >= t_idx)
                    drow = jnp.where(mask, rows[dr], -jnp.inf)
                    c_t = jnp.maximum(m0[dr], jnp.max(drow, axis=1, keepdims=True))
                    s = sc * jnp.exp(drow - c_t)
                    na = _dot(s.astype(BF16), v_aug)
                    num, den = na[:, 0:ML_DIM], na[:, ML_DIM:ML_DIM + 1]
                    if has_init:
                        w_c = jnp.exp(m0[dr] - c_t)
                        num = num + w_c * _dot(qb, c0[dr])
                        den = den + w_c * _dot_nt(qb, n0[dr])[:, 0:1]
                    hsum = hsum + num / jnp.maximum(jnp.abs(den), jnp.exp(-(col + c_t)))
                hn = _rms(hsum, nrm[0][:, hs])
                og = zo[pl.ds(q0, TQ), hs].astype(F32)
                out[pl.ds(q0, TQ), hs] = (hn * jax.nn.sigmoid(og)).astype(out.dtype)
                return carry

            if nq == 1:
                qblock(0, 0)
            else:
                lax.fori_loop(0, nq, qblock, 0)

            if not has_init:
                k_t = _dot_nt(eye.astype(BF16), k)
                kf = k.astype(F32)
                tot = (bp_t[lane_ff + h:lane_ff + h + 1, seq - 1:seq], bp_t[lane_fb + h:lane_fb + h + 1, seq - 1:seq])
                gl = (
                    tot[0] + rows[0],
                    bp_t[lane_fb + h:lane_fb + h + 1, :] - a_t[lane_fb + h:lane_fb + h + 1, :]
                    + a_t[lane_ib + h:lane_ib + h + 1, :],
                )
                for dr in range(2):
                    m_new = jnp.maximum(tot[dr] + m0[dr], jnp.max(gl[dr], axis=1, keepdims=True))
                    w_s = jnp.exp(gl[dr] - m_new)
                    cf_ref[0, 0, dr, h] = _dot((k_t * w_s).astype(BF16), v)
                    n_new = jnp.dot(jnp.broadcast_to(w_s, (SUBLANES, seq)), kf, precision=HI, preferred_element_type=F32)
                    nf_ref[0, 0, dr, h:h + 1, :] = n_new[0:1, :]
                    mf_ref[0, 0, dr * ML_HEADS + h:dr * ML_HEADS + h + 1, :] = jnp.broadcast_to(m_new, (1, LANES))

    return kern


def _mlstm(layer, za, zs, b_gates, mlstm_norm, init=None, ctx_out=None, states=None):
    has_init = init is not None
    seq, nseq, row0 = (S_SMP, N_SEQ_SMP, N_CTX // S_SMP) if has_init else (S_CTX, N_SEQ_CTX, 0)
    qkvo = [pl.BlockSpec((seq, ML_WIDTH), functools.partial(lambda j, b: (row0 + b, j), j)) for j in range(4)]
    in_specs = qkvo + [
        pl.BlockSpec((seq, LANES), lambda b: (row0 + b, ZS_W // LANES - 1)),
        pl.BlockSpec((1, 1, LANES), lambda b: (layer, 0, 0)),
        pl.BlockSpec((1, 1, ML_WIDTH), lambda b: (layer, 0, 0)),
    ]
    args = [za, za, za, za, zs, b_gates, mlstm_norm]
    out_specs = [pl.BlockSpec((seq, ML_WIDTH), lambda b: (row0 + b, 0))]
    out_shape = [jax.ShapeDtypeStruct((N_TOK, ML_WIDTH), BF16)]
    aliases = {}
    if has_init:
        st_c, st_n, st_m = init
        in_specs = [pl.BlockSpec(memory_space=pltpu.SMEM)] + in_specs + [
            pl.BlockSpec((1, 1, 2, ML_HEADS, ML_DIM, ML_DIM), lambda b: (b, layer, 0, 0, 0, 0)),
            pl.BlockSpec((1, 1, 2, ML_HEADS, ML_DIM), lambda b: (b, layer, 0, 0, 0)),
            pl.BlockSpec(memory_space=pl.ANY),
        ]
        args = [st_m.reshape(-1)] + args + [st_c, st_n, ctx_out]
        aliases = {len(args) - 1: 0}
    else:
        seq_blk = lambda b: jnp.minimum(b, nseq - 1)
        in_specs += [pl.BlockSpec(memory_space=pl.ANY)] * 3
        args += list(states)
        aliases = {len(args) - 3 + j: 1 + j for j in range(3)}
        out_specs += [
            pl.BlockSpec((1, 1, 2, ML_HEADS, ML_DIM, ML_DIM), lambda b: (seq_blk(b), layer, 0, 0, 0, 0)),
            pl.BlockSpec((1, 1, 2, ML_HEADS, ML_DIM), lambda b: (seq_blk(b), layer, 0, 0, 0)),
            pl.BlockSpec((1, 1, 2 * ML_HEADS, LANES), lambda b: (seq_blk(b), layer, 0, 0)),
        ]
        out_shape += [jax.ShapeDtypeStruct(s.shape, s.dtype) for s in states]
    return pl.pallas_call(
        _make_mlstm_kernel(seq, layer, has_init),
        grid=(nseq if has_init else N_SEQ_BLOCKS,),
        in_specs=in_specs,
        out_specs=out_specs,
        out_shape=out_shape,
        scratch_shapes=[pltpu.VMEM((seq, LANES), F32), pltpu.VMEM((seq, LANES), F32)],
        input_output_aliases=aliases,
        compiler_params=_cparams(("arbitrary",)),
        name="mlstm_smp" if has_init else "mlstm_ctx",
    )(*args)


def _rope(x, cos, sin_a, sin_b):
    return x * cos + pltpu.roll(x, LANES - 8, 1) * sin_a + pltpu.roll(x, 8, 1) * sin_b


def _mla_prep_kernel(zs_ref, qa_ref, kva_ref, wuq_ref, wk_ref, wv_ref, qn_ref, kn_ref, cos_ref, sa_ref, sb_ref, _, __,
                     q_ref, kk_ref, v_ref, ckv_ref, kr_ref, qf_scr, kf_scr):
    cq = zs_ref[:, 0:MLA_Q_RANK]
    ckv = zs_ref[:, MLA_Q_RANK:MLA_Q_RANK + MLA_KV_RANK]
    last = zs_ref[:, ZS_W - LANES:ZS_W]
    qf_scr[...] = _dot(_rms(cq, qa_ref[0]).astype(BF16), wuq_ref[0])
    ckvn = _rms(ckv, kva_ref[0])

    @pl.when(pl.program_id(0) < N_TILES_CTX)
    def _():
        for j in range(TW // S_CTX):
            ckv_ref[j, 0] = ckvn[j * S_CTX:(j + 1) * S_CTX, :]
            kr_ref[j, 0] = last[j * S_CTX:(j + 1) * S_CTX, 0:MLA_ROPE]

    cb = ckvn.astype(BF16)
    kf_scr[...] = _dot(cb, wk_ref[0])
    v_ref[...] = _dot(cb, wv_ref[0]).astype(BF16)
    lane = lax.broadcasted_iota(jnp.int32, last.shape, 1)
    kr = jnp.where((lane >= MLA_NOPE) & (lane < MLA_QK), pltpu.roll(last, MLA_NOPE, 1), 0.0)
    is_latent = pl.program_id(0) >= N_TILES_CTX

    def heads(rotate):
        for h in range(MLA_HEADS):
            hs = slice(h * HEAD_PAD, (h + 1) * HEAD_PAD)
            q_ref[:, hs] = rotate(_rms(qf_scr[:, hs], qn_ref[0], n=MLA_QK)).astype(BF16)
            kk_ref[:, hs] = rotate(_rms(kf_scr[:, hs] + kr, kn_ref[0], n=MLA_QK)).astype(BF16)

    @pl.when(is_latent)
    def _():
        cos, sa, sb = cos_ref[...], sa_ref[...], sb_ref[...]
        heads(lambda x: _rope(x, cos, sa, sb))

    @pl.when(jnp.logical_not(is_latent))
    def _():
        heads(lambda x: x)


def _mla_prep(layer, zs, q_a_norm, kv_a_norm, w_uq_r, w_k_r, w_v_r, q_norm_p, k_norm_p, rope_tabs, new_ckv, new_kr):
    seq_blk = lambda w: pl.BlockSpec((TW // S_CTX, 1, S_CTX, w), lambda i: (jnp.minimum(i, N_TILES_CTX - 1), layer, 0, 0))
    lw = lambda shape: pl.BlockSpec((1,) + shape, lambda i: (layer,) + (0,) * len(shape))
    tab = pl.BlockSpec((TW, LANES), lambda i: (jnp.where(i < N_TILES_CTX, 0, 1 + (i - N_TILES_CTX) % TILES_PER_SMP_SEQ), 0))
    tok = lambda w: pl.BlockSpec((TW, w), lambda i: (i, 0))
    return pl.pallas_call(
        _mla_prep_kernel,
        grid=(N_TILES,),
        in_specs=[
            tok(ZS_W), lw((1, MLA_Q_RANK)), lw((1, MLA_KV_RANK)),
            lw((MLA_Q_RANK, MLA_HEADS * HEAD_PAD)), lw((MLA_KV_RANK, MLA_HEADS * HEAD_PAD)),
            lw((MLA_KV_RANK, MLA_HEADS * MLA_V)), lw((1, HEAD_PAD)), lw((1, HEAD_PAD)), tab, tab, tab,
            pl.BlockSpec(memory_space=pl.ANY), pl.BlockSpec(memory_space=pl.ANY),
        ],
        out_specs=[tok(MLA_HEADS * HEAD_PAD), tok(MLA_HEADS * HEAD_PAD), tok(MLA_HEADS * MLA_V),
                   seq_blk(MLA_KV_RANK), seq_blk(MLA_ROPE)],
        input_output_aliases={11: 3, 12: 4},
        out_shape=[
            jax.ShapeDtypeStruct((N_TOK, MLA_HEADS * HEAD_PAD), BF16),
            jax.ShapeDtypeStruct((N_TOK, MLA_HEADS * HEAD_PAD), BF16),
            jax.ShapeDtypeStruct((N_TOK, MLA_HEADS * MLA_V), BF16),
            jax.ShapeDtypeStruct(new_ckv.shape, F32),
            jax.ShapeDtypeStruct(new_kr.shape, F32),
        ],
        scratch_shapes=[pltpu.VMEM((TW, MLA_HEADS * HEAD_PAD), F32), pltpu.VMEM((TW, MLA_HEADS * HEAD_PAD), F32)],
        compiler_params=_cparams(("arbitrary",)),
        name="mla_prep",
    )(zs, q_a_norm, kv_a_norm, w_uq_r, w_k_r, w_v_r, q_norm_p, k_norm_p, *rope_tabs, new_ckv, new_kr)


def _cache_kv_kernel(ckv_ref, kr_ref, wk_ref, wv_ref, kn_ref, kk_ref, v_ref):
    cb = ckv_ref[...].astype(BF16)
    kf = _dot(cb, wk_ref[0])
    v_ref[...] = _dot(cb, wv_ref[0]).astype(BF16)
    kr = kr_ref[...]
    for h in range(MLA_HEADS):
        hs = slice(h * HEAD_PAD, (h + 1) * HEAD_PAD)
        kk_ref[:, hs] = _rms(kf[:, hs] + kr, kn_ref[0], n=MLA_QK).astype(BF16)


def _cache_kv(cache_ckv, cache_kr_pad, w_k_r, w_v_r, k_norm_p):
    lw = lambda shape: pl.BlockSpec((1,) + shape, lambda b, l: (l,) + (0,) * len(shape))
    blk = lambda w: pl.BlockSpec((None, None, PAST_LEN, w), lambda b, l: (b, l, 0, 0))
    return pl.pallas_call(
        _cache_kv_kernel,
        grid=(N_SEQ_SMP, DEPTH),
        in_specs=[blk(MLA_KV_RANK), blk(LANES), lw((MLA_KV_RANK, MLA_HEADS * HEAD_PAD)),
                  lw((MLA_KV_RANK, MLA_HEADS * MLA_V)), lw((1, HEAD_PAD))],
        out_specs=[blk(MLA_HEADS * HEAD_PAD), blk(MLA_HEADS * MLA_V)],
        out_shape=[
            jax.ShapeDtypeStruct((N_SEQ_SMP, DEPTH, PAST_LEN, MLA_HEADS * HEAD_PAD), BF16),
            jax.ShapeDtypeStruct((N_SEQ_SMP, DEPTH, PAST_LEN, MLA_HEADS * MLA_V), BF16),
        ],
        compiler_params=_cparams(("arbitrary", "arbitrary")),
        name="cache_kv",
    )(cache_ckv, cache_kr_pad, w_k_r, w_v_r, k_norm_p)


def _make_attn_kernel(n_src):
    scale = MLA_QK ** -0.5

    def kern(q_ref, *refs):
        o_ref = refs[-1]
        if n_src > 1:
            body(q_ref, *refs)
            return

        @pl.when(pl.program_id(0) < N_SEQ_CTX)
        def _():
            body(q_ref, *refs)

        @pl.when(pl.program_id(0) >= N_SEQ_CTX)
        def _():
            o_ref[...] = jnp.zeros_like(o_ref)

    def body(q_ref, *refs):
        o_ref = refs[-1]
        for h in range(MLA_HEADS):
            hs = slice(h * HEAD_PAD, (h + 1) * HEAD_PAD)
            vs = slice(h * MLA_V, (h + 1) * MLA_V)
            q = q_ref[:, hs]
            ss = [_dot_nt(q, refs[2 * j][:, hs]) * scale for j in range(n_src)]
            m = functools.reduce(jnp.maximum, [jnp.max(s, axis=1, keepdims=True) for s in ss])
            ps = [jnp.exp(s - m) for s in ss]
            l = functools.reduce(jnp.add, [jnp.sum(p, axis=1, keepdims=True) for p in ps])
            o = functools.reduce(jnp.add, [_dot(ps[j].astype(BF16), refs[2 * j + 1][:, vs]) for j in range(n_src)])
            o_ref[:, vs] = (o / l).astype(o_ref.dtype)

    return kern


def _attn_ctx(q, kk, v):
    blk = lambda w: pl.BlockSpec((S_CTX, w), lambda b: (b, 0))
    return pl.pallas_call(
        _make_attn_kernel(1),
        grid=(N_SEQ_BLOCKS,),
        in_specs=[blk(MLA_HEADS * HEAD_PAD), blk(MLA_HEADS * HEAD_PAD), blk(MLA_HEADS * MLA_V)],
        out_specs=blk(MLA_HEADS * MLA_V),
        out_shape=jax.ShapeDtypeStruct((N_TOK, MLA_HEADS * MLA_V), BF16),
        compiler_params=_cparams(("arbitrary",)),
        name="attn_ctx",
    )(q, kk, v)


def _attn_smp(layer, q, kk, v, kk_cache, v_cache, ctx_out):
    row0 = N_CTX // S_SMP
    nq = S_SMP // TQ
    seqb = lambda w: pl.BlockSpec((S_SMP, w), lambda b, i: (row0 + b, 0))
    cache = lambda w: pl.BlockSpec((None, None, PAST_LEN, w), lambda b, i: (b, layer, 0, 0))
    return pl.pallas_call(
        _make_attn_kernel(2),
        grid=(N_SEQ_SMP, nq),
        in_specs=[
            pl.BlockSpec((TQ, MLA_HEADS * HEAD_PAD), lambda b, i: (N_CTX // TQ + b * nq + i, 0)),
            seqb(MLA_HEADS * HEAD_PAD), seqb(MLA_HEADS * MLA_V),
            cache(MLA_HEADS * HEAD_PAD), cache(MLA_HEADS * MLA_V),
            pl.BlockSpec(memory_space=pl.ANY),
        ],
        out_specs=pl.BlockSpec((TQ, MLA_HEADS * MLA_V), lambda b, i: (N_CTX // TQ + b * nq + i, 0)),
        out_shape=jax.ShapeDtypeStruct((N_TOK, MLA_HEADS * MLA_V), BF16),
        input_output_aliases={5: 0},
        compiler_params=_cparams(("arbitrary", "arbitrary")),
        name="attn_smp",
    )(q, kk, v, kk_cache, v_cache, ctx_out)


def _merge_kernel(x_ref, oa_ref, ob_ref, zc_ref, zb_ref, sgn_ref, ws_ref, bs_ref, wb_ref, wo_ref, g1_ref, n2_ref,
                  sh2_ref, sc2_ref, wr_ref, br_ref,
                  xmid_ref, h2_ref, dest_ref, wsel_ref, cnt_ref, oc_scr, carry_scr):
    i = pl.program_id(0)

    @pl.when(i == 0)
    def _():
        carry_scr[...] = jnp.zeros_like(carry_scr)

    u = _gelu(zc_ref[:, 0:SG_WIDTH].astype(F32))
    vg = _gelu(zc_ref[:, SG_WIDTH:2 * SG_WIDTH].astype(F32))
    for g in range(SG_GROUPS):
        gs = slice(g * SG_DIM, (g + 1) * SG_DIM)
        vn = _rms(vg[:, gs], sgn_ref[0][:, gs]).astype(BF16)
        for c in range(TW // SG_CHUNK):
            cs = slice(c * SG_CHUNK, (c + 1) * SG_CHUNK)
            mixed = _dot(ws_ref[0, g], vn[cs, :]) + bs_ref[0][:, g:g + 1]
            oc_scr[cs, gs] = (u[cs, gs] * mixed).astype(BF16)

    acc = jnp.zeros((TW, D_MODEL), F32)
    for j, src in enumerate((oa_ref, ob_ref, oc_scr)):
        gate = jax.nn.sigmoid(zb_ref[:, j * D_MODEL:(j + 1) * D_MODEL].astype(F32))
        acc = acc + gate * _dot(src[...], wb_ref[0, j])
    xm = x_ref[...] + g1_ref[0, 0] * _dot(acc.astype(BF16), wo_ref[0])
    xmid_ref[...] = xm
    h2 = _rms(xm, n2_ref[0]) * (1.0 + sc2_ref[0, 0]) + sh2_ref[0, 0]
    for c in range(SC_SPLIT):
        h2_ref[c] = _pack_pairs(h2[:, 2 * c * SC_ROW:(2 * c + 1) * SC_ROW], h2[:, (2 * c + 1) * SC_ROW:(2 * c + 2) * SC_ROW])

    h_hi = h2.astype(BF16)
    h_lo = (h2 - h_hi.astype(F32)).astype(BF16)
    p_hi = _dot(h_hi, wr_ref[0])
    logits = p_hi[:, 0:LANES] + p_hi[:, LANES:2 * LANES] + _dot(h_lo, wr_ref[0, :, 0:LANES]) + br_ref[0]
    lane = lax.broadcasted_iota(jnp.int32, logits.shape, 1)
    hits, exps = [], []
    sel = jnp.zeros(logits.shape, F32)
    denom = jnp.zeros((TW, 1), F32)
    top = None
    for _ in range(TOP_K):
        m = jnp.max(logits, axis=1, keepdims=True)
        idx = jnp.min(jnp.where(logits == m, lane, LANES), axis=1, keepdims=True)
        hit = lane == idx
        top = m if top is None else top
        hits.append(hit)
        exps.append(jnp.exp(m - top))
        sel = jnp.where(hit, 1.0, sel)
        denom = denom + exps[-1]
        logits = jnp.where(hit, -jnp.inf, logits)

    r_i = lax.broadcasted_iota(jnp.int32, (TW, TW), 0)
    c_i = lax.broadcasted_iota(jnp.int32, (TW, TW), 1)
    carry = carry_scr[0:1, :]
    rank = _dot(jnp.where(c_i < r_i, 1.0, 0.0).astype(BF16), sel.astype(BF16)) + carry
    new_carry = carry + jnp.sum(sel, axis=0, keepdims=True)
    carry_scr[...] = jnp.broadcast_to(new_carry, (SUBLANES, LANES))
    cnt_ref[...] = jnp.broadcast_to(new_carry, (SUBLANES, LANES))
    slot = rank + lane.astype(F32) * float(EXPERT_CAP)
    dmat = jnp.zeros(logits.shape, F32)
    wmat = jnp.zeros(logits.shape, F32)
    for k in range(TOP_K):
        dk = jnp.sum(jnp.where(hits[k], slot, 0.0), axis=1, keepdims=True)
        dmat = jnp.where(lane == k, dk, dmat)
        wmat = jnp.where(lane == k, exps[k] / denom, wmat)
    dest_ref[...] = dmat.T[0:SUBLANES, :].astype(jnp.int32)
    wsel_ref[...] = wmat


def _merge(layer, x, oa, ob, zc, zb, sg_norm, w_sp, b_sp, w_branch, w_out, mod, norm2, w_router_p, b_router_p):
    lw = lambda shape: pl.BlockSpec((1,) + shape, lambda i: (layer,) + (0,) * len(shape))
    tok = lambda w: pl.BlockSpec((TW, w), lambda i: (i, 0))
    return pl.pallas_call(
        _merge_kernel,
        grid=(N_TILES,),
        in_specs=[
            tok(D_MODEL), tok(ML_WIDTH), tok(MLA_HEADS * MLA_V), tok(ZC_W), tok(ZB_W),
            lw((1, SG_WIDTH)), lw((SG_GROUPS, SG_CHUNK, SG_CHUNK)), lw((SG_CHUNK, LANES)),
            lw((N_BRANCH, ML_WIDTH, D_MODEL)), lw((D_MODEL, D_MODEL)),
            _mod_spec(layer, 2), lw((1, D_MODEL)), _mod_spec(layer, 3), _mod_spec(layer, 4),
            lw((D_MODEL, 2 * LANES)), lw((1, LANES)),
        ],
        out_specs=[tok(D_MODEL), pl.BlockSpec((SC_SPLIT, TW, SC_ROW), lambda i: (0, i, 0)),
                   pl.BlockSpec((SUBLANES, TW), lambda i: (0, i)), tok(LANES),
                   pl.BlockSpec((SUBLANES, LANES), lambda i: (0, 0))],
        out_shape=[
            jax.ShapeDtypeStruct((N_TOK, D_MODEL), F32),
            jax.ShapeDtypeStruct((SC_SPLIT, N_TOK, SC_ROW), jnp.uint32),
            jax.ShapeDtypeStruct((SUBLANES, N_TOK), jnp.int32),
            jax.ShapeDtypeStruct((N_TOK, LANES), F32),
            jax.ShapeDtypeStruct((SUBLANES, LANES), F32),
        ],
        scratch_shapes=[pltpu.VMEM((TW, SG_WIDTH), BF16), pltpu.VMEM((SUBLANES, LANES), F32)],
        compiler_params=_cparams(("arbitrary",)),
        name="merge_router",
    )(x, oa, ob, zc, zb, sg_norm, w_sp, b_sp, w_branch, w_out, mod, norm2, mod, mod, w_router_p, b_router_p)


def _sc_mesh():
    return plsc.VectorSubcoreMesh(core_axis_name="core", subcore_axis_name="subcore")


def _sc_scatter_rows(x, idxs, n_rows):
    @pl.kernel(out_type=jax.ShapeDtypeStruct((n_rows, SC_ROW), x.dtype), mesh=_sc_mesh(), scratch_types=[])
    def scatter(x_hbm, *refs):
        o_hbm = refs[-1]

        def body(x_vmem, *i_vmems):
            for i_vmem in i_vmems:
                pltpu.sync_copy(x_vmem, o_hbm.at[i_vmem.at[0]])

        pltpu.emit_pipeline(
            body,
            grid=(x.shape[0] // SC_WIN,),
            in_specs=[pl.BlockSpec((SC_WIN, SC_ROW), lambda i: (i, 0))]
            + [pl.BlockSpec((1, SC_WIN), lambda i: (0, i))] * len(idxs),
            out_specs=[],
            core_axis_name=("core", "subcore"),
            dimension_semantics=(pltpu.PARALLEL,),
        )(x_hbm, *refs[:-1])

    return scatter(x, *idxs)


def _sc_gather_rows(x, idx):
    m = idx.shape[1]

    @pl.kernel(out_type=jax.ShapeDtypeStruct((m, SC_ROW), x.dtype), mesh=_sc_mesh())
    def gather(x_hbm, i_hbm, o_hbm):
        def body(i_vmem, o_vmem):
            pltpu.sync_copy(x_hbm.at[i_vmem.at[0]], o_vmem)

        pltpu.emit_pipeline(
            body,
            grid=(m // SC_WIN,),
            in_specs=[pl.BlockSpec((1, SC_WIN), lambda i: (0, i))],
            out_specs=[pl.BlockSpec((SC_WIN, SC_ROW), lambda i: (i, 0))],
            core_axis_name=("core", "subcore"),
            dimension_semantics=(pltpu.PARALLEL,),
        )(i_hbm, o_hbm)

    return gather(x, idx)


STEP_VALID, STEP_FIRST, STEP_HAS_NEXT, STEP_FULL = 1, 2, 4, 8


def _moe_ffn_kernel(layer, be_ref, nx_ref, br_ref, fl_ref, xs_ref, b1_ref, b2_ref, w1_hbm, w2_hbm, y_ref,
                    w1f, w2f, w1b, w2b, sem):
    g = pl.program_id(0)
    flags = fl_ref[g]

    def weight_copies(e):
        return (pltpu.make_async_copy(w1_hbm.at[layer, e], w1f, sem.at[0]),
                pltpu.make_async_copy(w2_hbm.at[layer, e], w2f, sem.at[1]))

    @pl.when(g == 0)
    def _():
        for cp in weight_copies(be_ref[0]):
            cp.start()

    @pl.when((flags & STEP_FIRST) != 0)
    def _():
        for cp in weight_copies(be_ref[g]):
            cp.wait()
        w1b[...] = w1f[...].astype(BF16)
        w2b[...] = w2f[...].astype(BF16)

        @pl.when((flags & STEP_HAS_NEXT) != 0)
        def _():
            for cp in weight_copies(nx_ref[g]):
                cp.start()

    def ffn(n_rows):
        halves = [h.astype(BF16) for c in range(SC_SPLIT) for h in _unpack_pairs(xs_ref[c, 0:n_rows, :])]
        g1 = _dot(jnp.concatenate(halves, axis=1), w1b[...]) + b1_ref[0, 0]
        gate = jnp.minimum(g1[:, :D_EXPERT], SWIGLU_LIMIT)
        up = jnp.clip(g1[:, D_EXPERT:], -SWIGLU_LIMIT, SWIGLU_LIMIT)
        act = gate * jax.nn.sigmoid(SWIGLU_ALPHA * gate) * (up + 1.0)
        y = _dot(act.astype(BF16), w2b[...]) + b2_ref[0, 0]
        for c in range(SC_SPLIT):
            y_ref[c, 0:n_rows, :] = _pack_pairs(
                y[:, 2 * c * SC_ROW:(2 * c + 1) * SC_ROW], y[:, (2 * c + 1) * SC_ROW:(2 * c + 2) * SC_ROW])

    pl.when((flags & (STEP_VALID | STEP_FULL)) == STEP_VALID)(lambda: ffn(SLOT_CHUNK))
    pl.when((flags & STEP_FULL) != 0)(lambda: ffn(FFN_BLOCK))


def _moe_ffn(layer, xs, plan, w1, b1, w2, b2):
    eb = lambda c: pl.BlockSpec((1, 1, 1, c), lambda g, be, nx, br, fl: (layer, be[g], 0, 0))
    rows = pl.BlockSpec((SC_SPLIT, FFN_BLOCK, SC_ROW), lambda g, be, nx, br, fl: (0, br[g], 0))
    hbm = pl.BlockSpec(memory_space=pl.ANY)
    grid_spec = pltpu.PrefetchScalarGridSpec(
        num_scalar_prefetch=4,
        grid=(N_CHUNK_STEPS,),
        in_specs=[rows, eb(2 * D_EXPERT), eb(D_MODEL), hbm, hbm],
        out_specs=rows,
        scratch_shapes=[
            pltpu.VMEM((D_MODEL, 2 * D_EXPERT), F32), pltpu.VMEM((D_EXPERT, D_MODEL), F32),
            pltpu.VMEM((D_MODEL, 2 * D_EXPERT), BF16), pltpu.VMEM((D_EXPERT, D_MODEL), BF16),
            pltpu.SemaphoreType.DMA((2,)),
        ],
    )
    return pl.pallas_call(
        functools.partial(_moe_ffn_kernel, layer),
        grid_spec=grid_spec,
        out_shape=jax.ShapeDtypeStruct(xs.shape, xs.dtype),
        compiler_params=_cparams(("arbitrary",)),
        name="moe_ffn",
    )(*plan, xs, b1, b2, w1, w2)


def _chunk_plan_kernel(cnt_ref, be_ref, nx_ref, br_ref, fl_ref):
    def expert(e, carry):
        step0, prev_first = carry
        c = cnt_ref[e]
        n_blk = (c + FFN_BLOCK - 1) // FFN_BLOCK

        def block(j, _):
            s = step0 + j
            be_ref[s] = e
            nx_ref[s] = e
            br_ref[s] = e * (EXPERT_CAP // FFN_BLOCK) + j
            fl_ref[s] = (STEP_VALID + jnp.where(j == 0, STEP_FIRST, 0)
                         + jnp.where(c - j * FFN_BLOCK > SLOT_CHUNK, STEP_FULL, 0))
            return 0

        lax.fori_loop(0, n_blk, block, 0)

        @pl.when((n_blk > 0) & (prev_first >= 0))
        def _():
            nx_ref[prev_first] = e
            fl_ref[prev_first] = fl_ref[prev_first] + STEP_HAS_NEXT

        return step0 + n_blk, jnp.where(n_blk > 0, step0, prev_first)

    used, _ = lax.fori_loop(0, N_EXPERTS, expert, (jnp.int32(0), jnp.int32(-1)))

    def idle(s, _):
        be_ref[s] = be_ref[used - 1]
        nx_ref[s] = be_ref[used - 1]
        br_ref[s] = br_ref[used - 1]
        fl_ref[s] = 0
        return 0

    lax.fori_loop(used, N_CHUNK_STEPS, idle, 0)


def _chunk_plan(cnt):
    smem = pl.BlockSpec(memory_space=pltpu.SMEM)
    return pl.pallas_call(
        _chunk_plan_kernel,
        in_specs=[smem],
        out_specs=[smem] * 4,
        out_shape=[jax.ShapeDtypeStruct((N_CHUNK_STEPS,), jnp.int32)] * 4,
        name="chunk_plan",
    )(cnt)


def _combine_kernel(x_ref, yg_ref, w_ref, g_ref, *o_refs):
    def emit(o_ref):
        w = w_ref[...]
        for c in range(SC_SPLIT):
            parts = [_unpack_pairs(yg_ref[k, c]) for k in range(TOP_K)]
            for half in range(2):
                cs = slice((2 * c + half) * SC_ROW, (2 * c + half + 1) * SC_ROW)
                acc = w[:, 0:1] * parts[0][half]
                for k in range(1, TOP_K):
                    acc = acc + w[:, k:k + 1] * parts[k][half]
                o_ref[:, cs] = x_ref[:, cs] + g_ref[0, 0][:, cs] * acc

    if len(o_refs) == 1:
        emit(o_refs[0])
    else:
        pl.when(pl.program_id(0) < N_TILES_CTX)(lambda: emit(o_refs[0]))
        pl.when(pl.program_id(0) >= N_TILES_CTX)(lambda: emit(o_refs[1]))


def _combine(layer, xmid, yg, wsel, mod, split_out):
    tok = lambda w: pl.BlockSpec((TW, w), lambda i: (i, 0))
    if split_out:
        out_specs = [pl.BlockSpec((TW, D_MODEL), lambda i: (jnp.minimum(i, N_TILES_CTX - 1), 0)),
                     pl.BlockSpec((TW, D_MODEL), lambda i: (jnp.maximum(i - N_TILES_CTX, 0), 0))]
        out_shape = [jax.ShapeDtypeStruct((N_CTX, D_MODEL), F32), jax.ShapeDtypeStruct((N_SMP, D_MODEL), F32)]
    else:
        out_specs, out_shape = tok(D_MODEL), jax.ShapeDtypeStruct((N_TOK, D_MODEL), F32)
    return pl.pallas_call(
        _combine_kernel,
        grid=(N_TILES,),
        in_specs=[tok(D_MODEL), pl.BlockSpec((TOP_K, SC_SPLIT, TW, SC_ROW), lambda i: (0, 0, i, 0)), tok(LANES),
                  _mod_spec(layer, 5)],
        out_specs=out_specs,
        out_shape=out_shape,
        compiler_params=_cparams(("arbitrary",)),
        name="combine",
    )(xmid, yg, wsel, mod)


def _moe(layer, xmid, h2, dest, wsel, cnt, mod, w1, b1, w2, b2, split_out):
    n_slots = N_EXPERTS * EXPERT_CAP
    idx = dest[0:TOP_K][:, None, :] + (jnp.arange(SC_SPLIT, dtype=jnp.int32) * n_slots)[None, :, None]
    idx = idx.reshape(TOP_K, 1, SC_SPLIT * N_TOK)
    xs = _sc_scatter_rows(h2.reshape(SC_SPLIT * N_TOK, SC_ROW), [idx[k] for k in range(TOP_K)], SC_SPLIT * n_slots)
    plan = _chunk_plan(cnt[0, :N_EXPERTS].astype(jnp.int32))
    y = _moe_ffn(layer, xs.reshape(SC_SPLIT, n_slots, SC_ROW), plan, w1, b1, w2, b2)
    yg = _sc_gather_rows(y.reshape(SC_SPLIT * n_slots, SC_ROW), idx.reshape(1, TOP_K * SC_SPLIT * N_TOK))
    return _combine(layer, xmid, yg.reshape(TOP_K, SC_SPLIT, N_TOK, SC_ROW), wsel, mod, split_out)


def _rope_tables():
    pos = np.arange(S_SMP)
    half = MLA_ROPE // 2
    inv_freq = (ROPE_THETA ** (-(np.arange(0, half, 2, dtype=np.float32) / np.float32(half)))).astype(np.float32)
    angs = [((pos // GRID_W).astype(np.float32)[:, None] * inv_freq[None, :]).astype(np.float32),
            ((pos % GRID_W).astype(np.float32)[:, None] * inv_freq[None, :]).astype(np.float32)]
    nf = half // 2
    cos = np.ones((TW + S_SMP, LANES), np.float32)
    sin_a = np.zeros((TW + S_SMP, LANES), np.float32)
    sin_b = np.zeros((TW + S_SMP, LANES), np.float32)
    for axis, ang in enumerate(angs):
        base = MLA_NOPE + axis * half
        c, s = np.cos(ang.astype(np.float64)), np.sin(ang.astype(np.float64))
        cos[TW:, base:base + nf] = c
        cos[TW:, base + nf:base + half] = c
        sin_a[TW:, base:base + nf] = -s
        sin_b[TW:, base + nf:base + half] = s
    return jnp.asarray(cos), jnp.asarray(sin_a), jnp.asarray(sin_b)


def _pad_last(a, width):
    return jnp.pad(a, [(0, 0)] * (a.ndim - 1) + [(0, width - a.shape[-1])])


def kernel(x_prompt, x_sample, cache_mla_ckv, cache_mla_krope, state_mlstm_C, state_mlstm_n, state_mlstm_m, c, c_ctx, norm1, norm2, w_ada, b_ada, w_in, b_mlstm_gates, mlstm_norm, mla_q_a_norm, mla_kv_a_norm, w_uq, w_ukv, mla_q_norm, mla_k_norm, sg_norm, w_spatial, b_spatial, w_branch, w_out, w_router, b_router, w_exp1, b_exp1, w_exp2, b_exp2):
    w_in_r = _w_in_prep(w_in)
    w_uq_r = _pad_last(w_uq.reshape(DEPTH, MLA_Q_RANK, MLA_HEADS, MLA_QK), HEAD_PAD).reshape(
        DEPTH, MLA_Q_RANK, MLA_HEADS * HEAD_PAD).astype(BF16)
    w_ukv4 = w_ukv.reshape(DEPTH, MLA_KV_RANK, MLA_HEADS, MLA_NOPE + MLA_V)
    w_k_r = _pad_last(w_ukv4[..., :MLA_NOPE], HEAD_PAD).reshape(DEPTH, MLA_KV_RANK, MLA_HEADS * HEAD_PAD).astype(BF16)
    w_v_r = w_ukv4[..., MLA_NOPE:].reshape(DEPTH, MLA_KV_RANK, MLA_HEADS * MLA_V).astype(BF16)
    q_norm_p = _pad_last(mla_q_norm, HEAD_PAD).reshape(DEPTH, 1, HEAD_PAD)
    k_norm_p = _pad_last(mla_k_norm, HEAD_PAD).reshape(DEPTH, 1, HEAD_PAD)
    b_gates_p = jnp.pad(b_mlstm_gates, ((0, 0), (GATE_LANE0, LANES - GATE_LANE0 - 4 * ML_HEADS))).reshape(DEPTH, 1, LANES)
    b_sp = _pad_last(jnp.swapaxes(b_spatial, 1, 2), LANES)
    w_router_p = _pad_last(w_router, LANES)
    w_router_hi = w_router_p.astype(BF16)
    w_router_p = jnp.concatenate([w_router_hi, (w_router_p - w_router_hi.astype(F32)).astype(BF16)], axis=-1)
    b_router_p = jnp.pad(b_router, ((0, 0), (0, LANES - N_EXPERTS)), constant_values=-1e30).reshape(DEPTH, 1, LANES)
    r3 = lambda a: a.reshape(DEPTH, 1, a.shape[-1])
    cache_kr_pad = jnp.pad(cache_mla_krope, ((0, 0), (0, 0), (0, 0), (MLA_NOPE, LANES - MLA_QK)))
    rope_tabs = _rope_tables()

    cvec = jnp.concatenate([c_ctx[None, :], c, jnp.zeros((SUBLANES - 1 - N_SEQ_SMP, D_MODEL), F32)], axis=0)
    mod = _adaln(cvec, w_ada, b_ada).reshape(DEPTH, SUBLANES, 1, 6 * D_MODEL)
    b1 = b_exp1.reshape(DEPTH, N_EXPERTS, 1, 2 * D_EXPERT)
    b2 = b_exp2.reshape(DEPTH, N_EXPERTS, 1, D_MODEL)
    kk_cache, v_cache = _cache_kv(cache_mla_ckv, cache_kr_pad, w_k_r, w_v_r, k_norm_p)

    x = jnp.concatenate([x_prompt.reshape(N_CTX, D_MODEL), x_sample.reshape(N_SMP, D_MODEL)], axis=0)
    new_ckv = jnp.zeros((N_SEQ_CTX, DEPTH, S_CTX, MLA_KV_RANK), F32)
    new_kr = jnp.zeros((N_SEQ_CTX, DEPTH, S_CTX, MLA_ROPE), F32)
    states = (jnp.zeros((N_SEQ_CTX, DEPTH, 2, ML_HEADS, ML_DIM, ML_DIM), F32),
              jnp.zeros((N_SEQ_CTX, DEPTH, 2, ML_HEADS, ML_DIM), F32),
              jnp.zeros((N_SEQ_CTX, DEPTH, 2 * ML_HEADS, LANES), F32))
    for l in range(DEPTH):
        za, zs, zc, zb = _inproj(l, x, r3(norm1), mod, w_in_r)
        oa, *states = _mlstm(l, za, zs, b_gates_p, r3(mlstm_norm), states=states)
        (oa,) = _mlstm(l, za, zs, b_gates_p, r3(mlstm_norm), init=(state_mlstm_C, state_mlstm_n, state_mlstm_m), ctx_out=oa)
        q, kk, v, new_ckv, new_kr = _mla_prep(l, zs, r3(mla_q_a_norm), r3(mla_kv_a_norm), w_uq_r, w_k_r, w_v_r,
                                              q_norm_p, k_norm_p, rope_tabs, new_ckv, new_kr)
        ob = _attn_smp(l, q, kk, v, kk_cache, v_cache, _attn_ctx(q, kk, v))
        xmid, h2, dest, wsel, cnt = _merge(
            l, x, oa, ob, zc, zb, r3(sg_norm), w_spatial.astype(BF16), b_sp, w_branch.astype(BF16), w_out.astype(BF16),
            mod, r3(norm2), w_router_p, b_router_p)
        x = _moe(l, xmid, h2, dest, wsel, cnt, mod, w_exp1, b1, w_exp2, b2, split_out=l == DEPTH - 1)
    y_ctx, y_smp = x
    return (
        y_ctx.reshape(N_SEQ_CTX, S_CTX, D_MODEL),
        y_smp.reshape(N_SEQ_SMP, S_SMP, D_MODEL),
        new_ckv,
        new_kr,
        states[0],
        states[1],
        states[2][:, :, :, 0].reshape(N_SEQ_CTX, DEPTH, 2, ML_HEADS),
    )
```

```python
import functools

import numpy as np
import jax
import jax.numpy as jnp
from jax import lax
from jax.experimental import pallas as pl
from jax.experimental.pallas import tpu as pltpu
from jax.experimental.pallas import tpu_sc as plsc

F32 = jnp.float32
BF16 = jnp.bfloat16
HI = lax.Precision.HIGHEST

D_MODEL = 1024
N_SEQ_CTX, S_CTX = 32, 256
N_SEQ_SMP, S_SMP = 2, 1024
DEPTH = 4
PAST_LEN = 512
GRID_W = 64
EPS = 1e-6
ML_HEADS, ML_DIM = 4, 128
ML_WIDTH = ML_HEADS * ML_DIM
MLA_HEADS, MLA_NOPE, MLA_ROPE, MLA_V = 8, 64, 32, 64
MLA_QK = MLA_NOPE + MLA_ROPE
MLA_Q_RANK, MLA_KV_RANK = 256, 128
ROPE_THETA = 10000.0
SG_GROUPS, SG_DIM, SG_CHUNK = 4, 128, 128
SG_WIDTH = SG_GROUPS * SG_DIM
N_BRANCH = 3
N_EXPERTS, TOP_K, D_EXPERT = 32, 4, 1024
SWIGLU_LIMIT, SWIGLU_ALPHA = 7.0, 1.702

N_CTX = N_SEQ_CTX * S_CTX
N_SMP = N_SEQ_SMP * S_SMP
N_TOK = N_CTX + N_SMP

LANES = 128
SUBLANES = 8
VMEM_LIMIT = 56 * 1024 * 1024

TW = 1024
TW_IN = 512
N_TILES = N_TOK // TW
N_TILES_CTX = N_CTX // TW
TILES_PER_SMP_SEQ = S_SMP // TW
N_SEQ_BLOCKS = N_TOK // S_CTX
HEAD_PAD = LANES
TQ = 256
EXPERT_CAP = N_TOK
SLOT_CHUNK = 256
FFN_BLOCK = 2 * SLOT_CHUNK
N_CHUNK_STEPS = N_TOK * TOP_K // FFN_BLOCK + N_EXPERTS
SC_ROW = 256
SC_SPLIT = D_MODEL // (2 * SC_ROW)
SC_WIN = 128

ZA_W = 4 * ML_WIDTH
ZS_W = 512
ZC_W = 2 * SG_WIDTH
ZB_W = N_BRANCH * D_MODEL
ZIN_W = ZA_W + ZS_W + ZC_W + ZB_W
GATE_LANE0 = MLA_ROPE


def _cparams(sem):
    return pltpu.CompilerParams(dimension_semantics=sem, vmem_limit_bytes=VMEM_LIMIT)


def _mod_row(i, tile=None):
    tile = tile or TW
    return jnp.where(i < N_CTX // tile, 0, 1 + (i - N_CTX // tile) // (S_SMP // tile))


def _rms(x, g, n=None):
    ms = jnp.sum(x * x, axis=-1, keepdims=True) * (1.0 / (n or x.shape[-1]))
    return x * lax.rsqrt(ms + EPS) * g


def _gelu(x):
    return 0.5 * x * (1.0 + jnp.tanh(0.7978845608028654 * (x + 0.044715 * (x * x * x))))


def _pack_pairs(lo, hi):
    lo_bits = lax.bitcast_convert_type(lo.astype(BF16).astype(F32), jnp.uint32)
    hi_bits = lax.bitcast_convert_type(hi.astype(BF16).astype(F32), jnp.uint32)
    return (lo_bits >> 16) | (hi_bits & jnp.uint32(0xFFFF0000))


def _unpack_pairs(u):
    return (lax.bitcast_convert_type(u << 16, F32), lax.bitcast_convert_type(u & jnp.uint32(0xFFFF0000), F32))


def _dot(a, b):
    return jnp.dot(a, b, preferred_element_type=F32)


def _dot_nt(a, b):
    return lax.dot_general(a, b, (((1,), (1,)), ((), ())), preferred_element_type=F32)


def _adaln_kernel(c_ref, w_ref, b_ref, o_ref):
    c = c_ref[...]
    s = c * jax.nn.sigmoid(c)
    o_ref[0] = jnp.dot(s, w_ref[0], precision=HI, preferred_element_type=F32) + b_ref[0]


def _adaln(cvec, w_ada, b_ada):
    nchunk = 4
    cw = 6 * D_MODEL // nchunk
    return pl.pallas_call(
        _adaln_kernel,
        grid=(DEPTH, nchunk),
        in_specs=[
            pl.BlockSpec((SUBLANES, D_MODEL), lambda l, j: (0, 0)),
            pl.BlockSpec((1, D_MODEL, cw), lambda l, j: (l, 0, j)),
            pl.BlockSpec((1, 1, cw), lambda l, j: (l, 0, j)),
        ],
        out_specs=pl.BlockSpec((1, SUBLANES, cw), lambda l, j: (l, 0, j)),
        out_shape=jax.ShapeDtypeStruct((DEPTH, SUBLANES, 6 * D_MODEL), F32),
        compiler_params=_cparams(("arbitrary", "arbitrary")),
        name="adaln",
    )(cvec, w_ada, b_ada.reshape(DEPTH, 1, 6 * D_MODEL))


IN_SPLITS = (ML_WIDTH, ML_WIDTH, ML_WIDTH, ML_WIDTH, 4 * ML_HEADS, MLA_Q_RANK, MLA_KV_RANK, MLA_ROPE, SG_WIDTH, SG_WIDTH,
             N_BRANCH * D_MODEL)
IN_OFFS = tuple(int(v) for v in np.cumsum((0,) + IN_SPLITS))
D_IN = IN_OFFS[-1]
W_PREP_ROWS = 256
W_PREP_COLS = 512


def _w_in_prep_kernel(wt_ref, o_ref):
    o = IN_OFFS

    def put(c0, rows):
        o_ref[0, :, c0:c0 + W_PREP_COLS] = rows.T.astype(BF16)

    for c0 in range(0, ZA_W, W_PREP_COLS):
        scale = ML_DIM ** -0.5 if o[1] <= c0 < o[2] else 1.0
        put(c0, wt_ref[0, c0:c0 + W_PREP_COLS, :] * scale)
    pad = jnp.zeros((ZS_W - (o[8] - o[4]), W_PREP_ROWS), F32)
    put(ZA_W, jnp.concatenate([wt_ref[0, o[5]:o[8], :], wt_ref[0, o[4]:o[5], :], pad], axis=0))
    for c0 in range(ZA_W + ZS_W, ZIN_W, W_PREP_COLS):
        src = c0 - (ZA_W + ZS_W) + o[8]
        put(c0, wt_ref[0, src:src + W_PREP_COLS, :])


def _w_in_prep(w_in):
    return pl.pallas_call(
        _w_in_prep_kernel,
        grid=(DEPTH, D_MODEL // W_PREP_ROWS),
        in_specs=[pl.BlockSpec((1, D_IN, W_PREP_ROWS), lambda l, r: (l, 0, r))],
        out_specs=pl.BlockSpec((1, W_PREP_ROWS, ZIN_W), lambda l, r: (l, r, 0)),
        out_shape=jax.ShapeDtypeStruct((DEPTH, D_MODEL, ZIN_W), BF16),
        compiler_params=_cparams(("arbitrary", "arbitrary")),
        name="w_in_prep",
    )(jnp.swapaxes(w_in, 1, 2))


def _inproj_kernel(x_ref, g_ref, sh_ref, sc_ref, w_ref, za_ref, zs_ref, zc_ref, zb_ref):
    h = _rms(x_ref[...], g_ref[0]) * (1.0 + sc_ref[0, 0]) + sh_ref[0, 0]
    hb = h.astype(BF16)
    za_ref[...] = _dot(hb, w_ref[0, :, 0:ZA_W]).astype(BF16)
    zs_ref[...] = _dot(hb, w_ref[0, :, ZA_W:ZA_W + ZS_W])
    zc_ref[...] = _dot(hb, w_ref[0, :, ZA_W + ZS_W:ZA_W + ZS_W + ZC_W]).astype(BF16)
    zb_ref[...] = _dot(hb, w_ref[0, :, ZA_W + ZS_W + ZC_W:ZIN_W]).astype(BF16)


def _mod_spec(layer, k, tile=None):
    return pl.BlockSpec((1, 1, 1, D_MODEL), lambda i: (layer, _mod_row(i, tile), 0, k))


def _inproj(layer, x, norm1, mod, w_in_r):
    tok = lambda w: pl.BlockSpec((TW_IN, w), lambda i: (i, 0))
    return pl.pallas_call(
        _inproj_kernel,
        grid=(N_TOK // TW_IN,),
        in_specs=[
            tok(D_MODEL),
            pl.BlockSpec((1, 1, D_MODEL), lambda i: (layer, 0, 0)),
            _mod_spec(layer, 0, TW_IN),
            _mod_spec(layer, 1, TW_IN),
            pl.BlockSpec((1, D_MODEL, ZIN_W), lambda i: (layer, 0, 0)),
        ],
        out_specs=[tok(ZA_W), tok(ZS_W), tok(ZC_W), tok(ZB_W)],
        out_shape=[
            jax.ShapeDtypeStruct((N_TOK, ZA_W), BF16),
            jax.ShapeDtypeStruct((N_TOK, ZS_W), F32),
            jax.ShapeDtypeStruct((N_TOK, ZC_W), BF16),
            jax.ShapeDtypeStruct((N_TOK, ZB_W), BF16),
        ],
        compiler_params=_cparams(("arbitrary",)),
        name="inproj",
    )(x, norm1, mod, mod, w_in_r)


def _make_mlstm_kernel(seq, layer, has_init):
    nq = seq // TQ
    lane_if, lane_ff, lane_ib, lane_fb = (GATE_LANE0 + ML_HEADS * j for j in range(4))

    def kern(*refs):
        if has_init:
            body(*refs)
            return
        out = refs[10]
        b = pl.program_id(0)

        @pl.when(b < N_SEQ_CTX)
        def _():
            body(*refs)

        @pl.when(b >= N_SEQ_CTX)
        def _():
            out[...] = jnp.zeros_like(out)

    def body(*refs):
        if has_init:
            m0_ref, zq, zk, zv, zo, gz, bg, nrm, c0_ref, n0_ref, _, out, bp_scr, bs_scr = refs
        else:
            zq, zk, zv, zo, gz, bg, nrm, _, _, _, out, cf_ref, nf_ref, mf_ref, bp_scr, bs_scr = refs
        b = pl.program_id(0)
        g = gz[...] + bg[0]
        lane = lax.broadcasted_iota(jnp.int32, g.shape, 1)
        is_forget = ((lane >= lane_ff) & (lane < lane_ib)) | ((lane >= lane_fb) & (lane < lane_fb + ML_HEADS))
        log_sig = jnp.minimum(g, 0.0) - jnp.log1p(jnp.exp(-jnp.abs(g)))
        a = jnp.where(is_forget, log_sig, g)
        r_i = lax.broadcasted_iota(jnp.int32, (seq, seq), 0)
        c_i = lax.broadcasted_iota(jnp.int32, (seq, seq), 1)
        ltri = (c_i <= r_i).astype(F32)
        bp = jnp.dot(ltri, a, precision=HI, preferred_element_type=F32)
        bs = bp[seq - 1:seq, :] - bp + a
        bp_scr[...] = bp
        bs_scr[...] = bs
        eye = (lax.broadcasted_iota(jnp.int32, (LANES, LANES), 0)
               == lax.broadcasted_iota(jnp.int32, (LANES, LANES), 1)).astype(F32)
        tr = lambda x: lax.dot_general(eye, x, (((1,), (1,)), ((), ())), precision=HI, preferred_element_type=F32)
        if has_init:
            tr = lambda x: x.T
        a_t, bp_t, bs_t = tr(a), tr(bp), tr(bs)

        for h in range(ML_HEADS):
            hs = slice(h * ML_DIM, (h + 1) * ML_DIM)
            k = zk[:, hs]
            v = zv[:, hs]
            first_lane = lax.broadcasted_iota(jnp.int32, (seq, ML_DIM), 1) == 0
            v_aug = jnp.concatenate([v, jnp.where(first_lane, 1.0, 0.0).astype(BF16)], axis=1)
            rows = (
                a_t[lane_if + h:lane_if + h + 1, :] - bp_t[lane_ff + h:lane_ff + h + 1, :],
                a_t[lane_ib + h:lane_ib + h + 1, :] - bs_t[lane_fb + h:lane_fb + h + 1, :],
            )
            col_refs = ((bp_scr, lane_ff + h), (bs_scr, lane_fb + h))
            if has_init:
                m0 = tuple(m0_ref[((b * DEPTH + layer) * 2 + dr) * ML_HEADS + h] for dr in range(2))
                c0 = tuple(c0_ref[0, 0, dr, h].astype(BF16) for dr in range(2))
                n0 = tuple(jnp.broadcast_to(n0_ref[0, 0, dr, h:h + 1, :], (ML_DIM, ML_DIM)).astype(BF16) for dr in range(2))
            else:
                m0 = (0.0, 0.0)

            def qblock(qi, carry):
                q0 = pl.multiple_of(qi * TQ, TQ)
                qb = zq[pl.ds(q0, TQ), hs]
                sc = _dot_nt(qb, k)
                t_idx = q0 + lax.broadcasted_iota(jnp.int32, (TQ, seq), 0)
                s_idx = lax.broadcasted_iota(jnp.int32, (TQ, seq), 1)
                hsum = jnp.zeros((TQ, ML_DIM), F32)
                for dr in range(2):
                    cref, cl = col_refs[dr]
                    col = cref[pl.ds(q0, TQ), cl:cl + 1]
                    mask = (s_idx <= t_idx) if dr == 0 else (s_idx >= t_idx)
                    drow = jnp.where(mask, rows[dr], -jnp.inf)
                    c_t = jnp.maximum(m0[dr], jnp.max(drow, axis=1, keepdims=True))
                    s = sc * jnp.exp(drow - c_t)
                    na = _dot(s.astype(BF16), v_aug)
                    num, den = na[:, 0:ML_DIM], na[:, ML_DIM:ML_DIM + 1]
                    if has_init:
                        w_c = jnp.exp(m0[dr] - c_t)
                        num = num + w_c * _dot(qb, c0[dr])
                        den = den + w_c * _dot_nt(qb, n0[dr])[:, 0:1]
                    hsum = hsum + num / jnp.maximum(jnp.abs(den), jnp.exp(-(col + c_t)))
                hn = _rms(hsum, nrm[0][:, hs])
                og = zo[pl.ds(q0, TQ), hs].astype(F32)
                out[pl.ds(q0, TQ), hs] = (hn * jax.nn.sigmoid(og)).astype(out.dtype)
                return carry

            if nq == 1:
                qblock(0, 0)
            else:
                lax.fori_loop(0, nq, qblock, 0)

            if not has_init:
                k_t = _dot_nt(eye.astype(BF16), k)
                kf = k.astype(F32)
                tot = (bp_t[lane_ff + h:lane_ff + h + 1, seq - 1:seq], bp_t[lane_fb + h:lane_fb + h + 1, seq - 1:seq])
                gl = (
                    tot[0] + rows[0],
                    bp_t[lane_fb + h:lane_fb + h + 1, :] - a_t[lane_fb + h:lane_fb + h + 1, :]
                    + a_t[lane_ib + h:lane_ib + h + 1, :],
                )
                for dr in range(2):
                    m_new = jnp.maximum(tot[dr] + m0[dr], jnp.max(gl[dr], axis=1, keepdims=True))
                    w_s = jnp.exp(gl[dr] - m_new)
                    cf_ref[0, 0, dr, h] = _dot((k_t * w_s).astype(BF16), v)
                    n_new = jnp.dot(jnp.broadcast_to(w_s, (SUBLANES, seq)), kf, precision=HI, preferred_element_type=F32)
                    nf_ref[0, 0, dr, h:h + 1, :] = n_new[0:1, :]
                    mf_ref[0, 0, dr * ML_HEADS + h:dr * ML_HEADS + h + 1, :] = jnp.broadcast_to(m_new, (1, LANES))

    return kern


def _mlstm(layer, za, zs, b_gates, mlstm_norm, init=None, ctx_out=None, states=None):
    has_init = init is not None
    seq, nseq, row0 = (S_SMP, N_SEQ_SMP, N_CTX // S_SMP) if has_init else (S_CTX, N_SEQ_CTX, 0)
    qkvo = [pl.BlockSpec((seq, ML_WIDTH), functools.partial(lambda j, b: (row0 + b, j), j)) for j in range(4)]
    in_specs = qkvo + [
        pl.BlockSpec((seq, LANES), lambda b: (row0 + b, ZS_W // LANES - 1)),
        pl.BlockSpec((1, 1, LANES), lambda b: (layer, 0, 0)),
        pl.BlockSpec((1, 1, ML_WIDTH), lambda b: (layer, 0, 0)),
    ]
    args = [za, za, za, za, zs, b_gates, mlstm_norm]
    out_specs = [pl.BlockSpec((seq, ML_WIDTH), lambda b: (row0 + b, 0))]
    out_shape = [jax.ShapeDtypeStruct((N_TOK, ML_WIDTH), BF16)]
    aliases = {}
    if has_init:
        st_c, st_n, st_m = init
        in_specs = [pl.BlockSpec(memory_space=pltpu.SMEM)] + in_specs + [
            pl.BlockSpec((1, 1, 2, ML_HEADS, ML_DIM, ML_DIM), lambda b: (b, layer, 0, 0, 0, 0)),
            pl.BlockSpec((1, 1, 2, ML_HEADS, ML_DIM), lambda b: (b, layer, 0, 0, 0)),
            pl.BlockSpec(memory_space=pl.ANY),
        ]
        args = [st_m.reshape(-1)] + args + [st_c, st_n, ctx_out]
        aliases = {len(args) - 1: 0}
    else:
        seq_blk = lambda b: jnp.minimum(b, nseq - 1)
        in_specs += [pl.BlockSpec(memory_space=pl.ANY)] * 3
        args += list(states)
        aliases = {len(args) - 3 + j: 1 + j for j in range(3)}
        out_specs += [
            pl.BlockSpec((1, 1, 2, ML_HEADS, ML_DIM, ML_DIM), lambda b: (seq_blk(b), layer, 0, 0, 0, 0)),
            pl.BlockSpec((1, 1, 2, ML_HEADS, ML_DIM), lambda b: (seq_blk(b), layer, 0, 0, 0)),
            pl.BlockSpec((1, 1, 2 * ML_HEADS, LANES), lambda b: (seq_blk(b), layer, 0, 0)),
        ]
        out_shape += [jax.ShapeDtypeStruct(s.shape, s.dtype) for s in states]
    return pl.pallas_call(
        _make_mlstm_kernel(seq, layer, has_init),
        grid=(nseq if has_init else N_SEQ_BLOCKS,),
        in_specs=in_specs,
        out_specs=out_specs,
        out_shape=out_shape,
        scratch_shapes=[pltpu.VMEM((seq, LANES), F32), pltpu.VMEM((seq, LANES), F32)],
        input_output_aliases=aliases,
        compiler_params=_cparams(("arbitrary",)),
        name="mlstm_smp" if has_init else "mlstm_ctx",
    )(*args)


def _rope(x, cos, sin_a, sin_b):
    return x * cos + pltpu.roll(x, LANES - 8, 1) * sin_a + pltpu.roll(x, 8, 1) * sin_b


def _mla_prep_kernel(zs_ref, qa_ref, kva_ref, wuq_ref, wk_ref, wv_ref, qn_ref, kn_ref, cos_ref, sa_ref, sb_ref, _, __,
                     q_ref, kk_ref, v_ref, ckv_ref, kr_ref, qf_scr, kf_scr):
    cq = zs_ref[:, 0:MLA_Q_RANK]
    ckv = zs_ref[:, MLA_Q_RANK:MLA_Q_RANK + MLA_KV_RANK]
    last = zs_ref[:, ZS_W - LANES:ZS_W]
    qf_scr[...] = _dot(_rms(cq, qa_ref[0]).astype(BF16), wuq_ref[0])
    ckvn = _rms(ckv, kva_ref[0])

    @pl.when(pl.program_id(0) < N_TILES_CTX)
    def _():
        for j in range(TW // S_CTX):
            ckv_ref[j, 0] = ckvn[j * S_CTX:(j + 1) * S_CTX, :]
            kr_ref[j, 0] = last[j * S_CTX:(j + 1) * S_CTX, 0:MLA_ROPE]

    cb = ckvn.astype(BF16)
    kf_scr[...] = _dot(cb, wk_ref[0])
    v_ref[...] = _dot(cb, wv_ref[0]).astype(BF16)
    lane = lax.broadcasted_iota(jnp.int32, last.shape, 1)
    kr = jnp.where((lane >= MLA_NOPE) & (lane < MLA_QK), pltpu.roll(last, MLA_NOPE, 1), 0.0)
    is_latent = pl.program_id(0) >= N_TILES_CTX

    def heads(rotate):
        for h in range(MLA_HEADS):
            hs = slice(h * HEAD_PAD, (h + 1) * HEAD_PAD)
            q_ref[:, hs] = rotate(_rms(qf_scr[:, hs], qn_ref[0], n=MLA_QK)).astype(BF16)
            kk_ref[:, hs] = rotate(_rms(kf_scr[:, hs] + kr, kn_ref[0], n=MLA_QK)).astype(BF16)

    @pl.when(is_latent)
    def _():
        cos, sa, sb = cos_ref[...], sa_ref[...], sb_ref[...]
        heads(lambda x: _rope(x, cos, sa, sb))

    @pl.when(jnp.logical_not(is_latent))
    def _():
        heads(lambda x: x)


def _mla_prep(layer, zs, q_a_norm, kv_a_norm, w_uq_r, w_k_r, w_v_r, q_norm_p, k_norm_p, rope_tabs, new_ckv, new_kr):
    seq_blk = lambda w: pl.BlockSpec((TW // S_CTX, 1, S_CTX, w), lambda i: (jnp.minimum(i, N_TILES_CTX - 1), layer, 0, 0))
    lw = lambda shape: pl.BlockSpec((1,) + shape, lambda i: (layer,) + (0,) * len(shape))
    tab = pl.BlockSpec((TW, LANES), lambda i: (jnp.where(i < N_TILES_CTX, 0, 1 + (i - N_TILES_CTX) % TILES_PER_SMP_SEQ), 0))
    tok = lambda w: pl.BlockSpec((TW, w), lambda i: (i, 0))
    return pl.pallas_call(
        _mla_prep_kernel,
        grid=(N_TILES,),
        in_specs=[
            tok(ZS_W), lw((1, MLA_Q_RANK)), lw((1, MLA_KV_RANK)),
            lw((MLA_Q_RANK, MLA_HEADS * HEAD_PAD)), lw((MLA_KV_RANK, MLA_HEADS * HEAD_PAD)),
            lw((MLA_KV_RANK, MLA_HEADS * MLA_V)), lw((1, HEAD_PAD)), lw((1, HEAD_PAD)), tab, tab, tab,
            pl.BlockSpec(memory_space=pl.ANY), pl.BlockSpec(memory_space=pl.ANY),
        ],
        out_specs=[tok(MLA_HEADS * HEAD_PAD), tok(MLA_HEADS * HEAD_PAD), tok(MLA_HEADS * MLA_V),
                   seq_blk(MLA_KV_RANK), seq_blk(MLA_ROPE)],
        input_output_aliases={11: 3, 12: 4},
        out_shape=[
            jax.ShapeDtypeStruct((N_TOK, MLA_HEADS * HEAD_PAD), BF16),
            jax.ShapeDtypeStruct((N_TOK, MLA_HEADS * HEAD_PAD), BF16),
            jax.ShapeDtypeStruct((N_TOK, MLA_HEADS * MLA_V), BF16),
            jax.ShapeDtypeStruct(new_ckv.shape, F32),
            jax.ShapeDtypeStruct(new_kr.shape, F32),
        ],
        scratch_shapes=[pltpu.VMEM((TW, MLA_HEADS * HEAD_PAD), F32), pltpu.VMEM((TW, MLA_HEADS * HEAD_PAD), F32)],
        compiler_params=_cparams(("arbitrary",)),
        name="mla_prep",
    )(zs, q_a_norm, kv_a_norm, w_uq_r, w_k_r, w_v_r, q_norm_p, k_norm_p, *rope_tabs, new_ckv, new_kr)


def _cache_kv_kernel(ckv_ref, kr_ref, wk_ref, wv_ref, kn_ref, kk_ref, v_ref):
    cb = ckv_ref[...].astype(BF16)
    kf = _dot(cb, wk_ref[0])
    v_ref[...] = _dot(cb, wv_ref[0]).astype(BF16)
    kr = kr_ref[...]
    for h in range(MLA_HEADS):
        hs = slice(h * HEAD_PAD, (h + 1) * HEAD_PAD)
        kk_ref[:, hs] = _rms(kf[:, hs] + kr, kn_ref[0], n=MLA_QK).astype(BF16)


def _cache_kv(cache_ckv, cache_kr_pad, w_k_r, w_v_r, k_norm_p):
    lw = lambda shape: pl.BlockSpec((1,) + shape, lambda b, l: (l,) + (0,) * len(shape))
    blk = lambda w: pl.BlockSpec((None, None, PAST_LEN, w), lambda b, l: (b, l, 0, 0))
    return pl.pallas_call(
        _cache_kv_kernel,
        grid=(N_SEQ_SMP, DEPTH),
        in_specs=[blk(MLA_KV_RANK), blk(LANES), lw((MLA_KV_RANK, MLA_HEADS * HEAD_PAD)),
                  lw((MLA_KV_RANK, MLA_HEADS * MLA_V)), lw((1, HEAD_PAD))],
        out_specs=[blk(MLA_HEADS * HEAD_PAD), blk(MLA_HEADS * MLA_V)],
        out_shape=[
            jax.ShapeDtypeStruct((N_SEQ_SMP, DEPTH, PAST_LEN, MLA_HEADS * HEAD_PAD), BF16),
            jax.ShapeDtypeStruct((N_SEQ_SMP, DEPTH, PAST_LEN, MLA_HEADS * MLA_V), BF16),
        ],
        compiler_params=_cparams(("arbitrary", "arbitrary")),
        name="cache_kv",
    )(cache_ckv, cache_kr_pad, w_k_r, w_v_r, k_norm_p)


def _make_attn_kernel(n_src):
    scale = MLA_QK ** -0.5

    def kern(q_ref, *refs):
        o_ref = refs[-1]
        if n_src > 1:
            body(q_ref, *refs)
            return

        @pl.when(pl.program_id(0) < N_SEQ_CTX)
        def _():
            body(q_ref, *refs)

        @pl.when(pl.program_id(0) >= N_SEQ_CTX)
        def _():
            o_ref[...] = jnp.zeros_like(o_ref)

    def body(q_ref, *refs):
        o_ref = refs[-1]
        for h in range(MLA_HEADS):
            hs = slice(h * HEAD_PAD, (h + 1) * HEAD_PAD)
            vs = slice(h * MLA_V, (h + 1) * MLA_V)
            q = q_ref[:, hs]
            ss = [_dot_nt(q, refs[2 * j][:, hs]) * scale for j in range(n_src)]
            m = functools.reduce(jnp.maximum, [jnp.max(s, axis=1, keepdims=True) for s in ss])
            ps = [jnp.exp(s - m) for s in ss]
            l = functools.reduce(jnp.add, [jnp.sum(p, axis=1, keepdims=True) for p in ps])
            o = functools.reduce(jnp.add, [_dot(ps[j].astype(BF16), refs[2 * j + 1][:, vs]) for j in range(n_src)])
            o_ref[:, vs] = (o / l).astype(o_ref.dtype)

    return kern


def _attn_ctx(q, kk, v):
    blk = lambda w: pl.BlockSpec((S_CTX, w), lambda b: (b, 0))
    return pl.pallas_call(
        _make_attn_kernel(1),
        grid=(N_SEQ_BLOCKS,),
        in_specs=[blk(MLA_HEADS * HEAD_PAD), blk(MLA_HEADS * HEAD_PAD), blk(MLA_HEADS * MLA_V)],
        out_specs=blk(MLA_HEADS * MLA_V),
        out_shape=jax.ShapeDtypeStruct((N_TOK, MLA_HEADS * MLA_V), BF16),
        compiler_params=_cparams(("arbitrary",)),
        name="attn_ctx",
    )(q, kk, v)


def _attn_smp(layer, q, kk, v, kk_cache, v_cache, ctx_out):
    row0 = N_CTX // S_SMP
    nq = S_SMP // TQ
    seqb = lambda w: pl.BlockSpec((S_SMP, w), lambda b, i: (row0 + b, 0))
    cache = lambda w: pl.BlockSpec((None, None, PAST_LEN, w), lambda b, i: (b, layer, 0, 0))
    return pl.pallas_call(
        _make_attn_kernel(2),
        grid=(N_SEQ_SMP, nq),
        in_specs=[
            pl.BlockSpec((TQ, MLA_HEADS * HEAD_PAD), lambda b, i: (N_CTX // TQ + b * nq + i, 0)),
            seqb(MLA_HEADS * HEAD_PAD), seqb(MLA_HEADS * MLA_V),
            cache(MLA_HEADS * HEAD_PAD), cache(MLA_HEADS * MLA_V),
            pl.BlockSpec(memory_space=pl.ANY),
        ],
        out_specs=pl.BlockSpec((TQ, MLA_HEADS * MLA_V), lambda b, i: (N_CTX // TQ + b * nq + i, 0)),
        out_shape=jax.ShapeDtypeStruct((N_TOK, MLA_HEADS * MLA_V), BF16),
        input_output_aliases={5: 0},
        compiler_params=_cparams(("arbitrary", "arbitrary")),
        name="attn_smp",
    )(q, kk, v, kk_cache, v_cache, ctx_out)


def _merge_kernel(x_ref, oa_ref, ob_ref, zc_ref, zb_ref, sgn_ref, ws_ref, bs_ref, wb_ref, wo_ref, g1_ref, n2_ref,
                  sh2_ref, sc2_ref, wr_ref, br_ref,
                  xmid_ref, h2_ref, dest_ref, wsel_ref, cnt_ref, oc_scr, carry_scr):
    i = pl.program_id(0)

    @pl.when(i == 0)
    def _():
        carry_scr[...] = jnp.zeros_like(carry_scr)

    u = _gelu(zc_ref[:, 0:SG_WIDTH].astype(F32))
    vg = _gelu(zc_ref[:, SG_WIDTH:2 * SG_WIDTH].astype(F32))
    for g in range(SG_GROUPS):
        gs = slice(g * SG_DIM, (g + 1) * SG_DIM)
        vn = _rms(vg[:, gs], sgn_ref[0][:, gs]).astype(BF16)
        for c in range(TW // SG_CHUNK):
            cs = slice(c * SG_CHUNK, (c + 1) * SG_CHUNK)
            mixed = _dot(ws_ref[0, g], vn[cs, :]) + bs_ref[0][:, g:g + 1]
            oc_scr[cs, gs] = (u[cs, gs] * mixed).astype(BF16)

    acc = jnp.zeros((TW, D_MODEL), F32)
    for j, src in enumerate((oa_ref, ob_ref, oc_scr)):
        gate = jax.nn.sigmoid(zb_ref[:, j * D_MODEL:(j + 1) * D_MODEL].astype(F32))
        acc = acc + gate * _dot(src[...], wb_ref[0, j])
    xm = x_ref[...] + g1_ref[0, 0] * _dot(acc.astype(BF16), wo_ref[0])
    xmid_ref[...] = xm
    h2 = _rms(xm, n2_ref[0]) * (1.0 + sc2_ref[0, 0]) + sh2_ref[0, 0]
    for c in range(SC_SPLIT):
        h2_ref[c] = _pack_pairs(h2[:, 2 * c * SC_ROW:(2 * c + 1) * SC_ROW], h2[:, (2 * c + 1) * SC_ROW:(2 * c + 2) * SC_ROW])

    h_hi = h2.astype(BF16)
    h_lo = (h2 - h_hi.astype(F32)).astype(BF16)
    p_hi = _dot(h_hi, wr_ref[0])
    logits = p_hi[:, 0:LANES] + p_hi[:, LANES:2 * LANES] + _dot(h_lo, wr_ref[0, :, 0:LANES]) + br_ref[0]
    lane = lax.broadcasted_iota(jnp.int32, logits.shape, 1)
    hits, exps = [], []
    sel = jnp.zeros(logits.shape, F32)
    denom = jnp.zeros((TW, 1), F32)
    top = None
    for _ in range(TOP_K):
        m = jnp.max(logits, axis=1, keepdims=True)
        idx = jnp.min(jnp.where(logits == m, lane, LANES), axis=1, keepdims=True)
        hit = lane == idx
        top = m if top is None else top
        hits.append(hit)
        exps.append(jnp.exp(m - top))
        sel = jnp.where(hit, 1.0, sel)
        denom = denom + exps[-1]
        logits = jnp.where(hit, -jnp.inf, logits)

    r_i = lax.broadcasted_iota(jnp.int32, (TW, TW), 0)
    c_i = lax.broadcasted_iota(jnp.int32, (TW, TW), 1)
    carry = carry_scr[0:1, :]
    rank = _dot(jnp.where(c_i < r_i, 1.0, 0.0).astype(BF16), sel.astype(BF16)) + carry
    new_carry = carry + jnp.sum(sel, axis=0, keepdims=True)
    carry_scr[...] = jnp.broadcast_to(new_carry, (SUBLANES, LANES))
    cnt_ref[...] = jnp.broadcast_to(new_carry, (SUBLANES, LANES))
    slot = rank + lane.astype(F32) * float(EXPERT_CAP)
    dmat = jnp.zeros(logits.shape, F32)
    wmat = jnp.zeros(logits.shape, F32)
    for k in range(TOP_K):
        dk = jnp.sum(jnp.where(hits[k], slot, 0.0), axis=1, keepdims=True)
        dmat = jnp.where(lane == k, dk, dmat)
        wmat = jnp.where(lane == k, exps[k] / denom, wmat)
    dest_ref[...] = dmat.T[0:SUBLANES, :].astype(jnp.int32)
    wsel_ref[...] = wmat


def _merge(layer, x, oa, ob, zc, zb, sg_norm, w_sp, b_sp, w_branch, w_out, mod, norm2, w_router_p, b_router_p):
    lw = lambda shape: pl.BlockSpec((1,) + shape, lambda i: (layer,) + (0,) * len(shape))
    tok = lambda w: pl.BlockSpec((TW, w), lambda i: (i, 0))
    return pl.pallas_call(
        _merge_kernel,
        grid=(N_TILES,),
        in_specs=[
            tok(D_MODEL), tok(ML_WIDTH), tok(MLA_HEADS * MLA_V), tok(ZC_W), tok(ZB_W),
            lw((1, SG_WIDTH)), lw((SG_GROUPS, SG_CHUNK, SG_CHUNK)), lw((SG_CHUNK, LANES)),
            lw((N_BRANCH, ML_WIDTH, D_MODEL)), lw((D_MODEL, D_MODEL)),
            _mod_spec(layer, 2), lw((1, D_MODEL)), _mod_spec(layer, 3), _mod_spec(layer, 4),
            lw((D_MODEL, 2 * LANES)), lw((1, LANES)),
        ],
        out_specs=[tok(D_MODEL), pl.BlockSpec((SC_SPLIT, TW, SC_ROW), lambda i: (0, i, 0)),
                   pl.BlockSpec((SUBLANES, TW), lambda i: (0, i)), tok(LANES),
                   pl.BlockSpec((SUBLANES, LANES), lambda i: (0, 0))],
        out_shape=[
            jax.ShapeDtypeStruct((N_TOK, D_MODEL), F32),
            jax.ShapeDtypeStruct((SC_SPLIT, N_TOK, SC_ROW), jnp.uint32),
            jax.ShapeDtypeStruct((SUBLANES, N_TOK), jnp.int32),
            jax.ShapeDtypeStruct((N_TOK, LANES), F32),
            jax.ShapeDtypeStruct((SUBLANES, LANES), F32),
        ],
        scratch_shapes=[pltpu.VMEM((TW, SG_WIDTH), BF16), pltpu.VMEM((SUBLANES, LANES), F32)],
        compiler_params=_cparams(("arbitrary",)),
        name="merge_router",
    )(x, oa, ob, zc, zb, sg_norm, w_sp, b_sp, w_branch, w_out, mod, norm2, mod, mod, w_router_p, b_router_p)


def _sc_mesh():
    return plsc.VectorSubcoreMesh(core_axis_name="core", subcore_axis_name="subcore")


def _sc_scatter_rows(x, idxs, n_rows):
    @pl.kernel(out_type=jax.ShapeDtypeStruct((n_rows, SC_ROW), x.dtype), mesh=_sc_mesh(), scratch_types=[])
    def scatter(x_hbm, *refs):
        o_hbm = refs[-1]

        def body(x_vmem, *i_vmems):
            for i_vmem in i_vmems:
                pltpu.sync_copy(x_vmem, o_hbm.at[i_vmem.at[0]])

        pltpu.emit_pipeline(
            body,
            grid=(x.shape[0] // SC_WIN,),
            in_specs=[pl.BlockSpec((SC_WIN, SC_ROW), lambda i: (i, 0))]
            + [pl.BlockSpec((1, SC_WIN), lambda i: (0, i))] * len(idxs),
            out_specs=[],
            core_axis_name=("core", "subcore"),
            dimension_semantics=(pltpu.PARALLEL,),
        )(x_hbm, *refs[:-1])

    return scatter(x, *idxs)


def _sc_gather_rows(x, idx):
    m = idx.shape[1]

    @pl.kernel(out_type=jax.ShapeDtypeStruct((m, SC_ROW), x.dtype), mesh=_sc_mesh())
    def gather(x_hbm, i_hbm, o_hbm):
        def body(i_vmem, o_vmem):
            pltpu.sync_copy(x_hbm.at[i_vmem.at[0]], o_vmem)

        pltpu.emit_pipeline(
            body,
            grid=(m // SC_WIN,),
            in_specs=[pl.BlockSpec((1, SC_WIN), lambda i: (0, i))],
            out_specs=[pl.BlockSpec((SC_WIN, SC_ROW), lambda i: (i, 0))],
            core_axis_name=("core", "subcore"),
            dimension_semantics=(pltpu.PARALLEL,),
        )(i_hbm, o_hbm)

    return gather(x, idx)


STEP_VALID, STEP_FIRST, STEP_HAS_NEXT, STEP_FULL = 1, 2, 4, 8


def _moe_ffn_kernel(layer, be_ref, nx_ref, br_ref, fl_ref, xs_ref, b1_ref, b2_ref, w1_hbm, w2_hbm, y_ref,
                    w1f, w2f, w1b, w2b, sem):
    g = pl.program_id(0)
    flags = fl_ref[g]

    def weight_copies(e):
        return (pltpu.make_async_copy(w1_hbm.at[layer, e], w1f, sem.at[0]),
                pltpu.make_async_copy(w2_hbm.at[layer, e], w2f, sem.at[1]))

    @pl.when(g == 0)
    def _():
        for cp in weight_copies(be_ref[0]):
            cp.start()

    @pl.when((flags & STEP_FIRST) != 0)
    def _():
        for cp in weight_copies(be_ref[g]):
            cp.wait()
        w1b[...] = w1f[...].astype(BF16)
        w2b[...] = w2f[...].astype(BF16)

        @pl.when((flags & STEP_HAS_NEXT) != 0)
        def _():
            for cp in weight_copies(nx_ref[g]):
                cp.start()

    def ffn(n_rows):
        halves = [h.astype(BF16) for c in range(SC_SPLIT) for h in _unpack_pairs(xs_ref[c, 0:n_rows, :])]
        g1 = _dot(jnp.concatenate(halves, axis=1), w1b[...]) + b1_ref[0, 0]
        gate = jnp.minimum(g1[:, :D_EXPERT], SWIGLU_LIMIT)
        up = jnp.clip(g1[:, D_EXPERT:], -SWIGLU_LIMIT, SWIGLU_LIMIT)
        act = gate * jax.nn.sigmoid(SWIGLU_ALPHA * gate) * (up + 1.0)
        y = _dot(act.astype(BF16), w2b[...]) + b2_ref[0, 0]
        for c in range(SC_SPLIT):
            y_ref[c, 0:n_rows, :] = _pack_pairs(
                y[:, 2 * c * SC_ROW:(2 * c + 1) * SC_ROW], y[:, (2 * c + 1) * SC_ROW:(2 * c + 2) * SC_ROW])

    pl.when((flags & (STEP_VALID | STEP_FULL)) == STEP_VALID)(lambda: ffn(SLOT_CHUNK))
    pl.when((flags & STEP_FULL) != 0)(lambda: ffn(FFN_BLOCK))


def _moe_ffn(layer, xs, plan, w1, b1, w2, b2):
    eb = lambda c: pl.BlockSpec((1, 1, 1, c), lambda g, be, nx, br, fl: (layer, be[g], 0, 0))
    rows = pl.BlockSpec((SC_SPLIT, FFN_BLOCK, SC_ROW), lambda g, be, nx, br, fl: (0, br[g], 0))
    hbm = pl.BlockSpec(memory_space=pl.ANY)
    grid_spec = pltpu.PrefetchScalarGridSpec(
        num_scalar_prefetch=4,
        grid=(N_CHUNK_STEPS,),
        in_specs=[rows, eb(2 * D_EXPERT), eb(D_MODEL), hbm, hbm],
        out_specs=rows,
        scratch_shapes=[
            pltpu.VMEM((D_MODEL, 2 * D_EXPERT), F32), pltpu.VMEM((D_EXPERT, D_MODEL), F32),
            pltpu.VMEM((D_MODEL, 2 * D_EXPERT), BF16), pltpu.VMEM((D_EXPERT, D_MODEL), BF16),
            pltpu.SemaphoreType.DMA((2,)),
        ],
    )
    return pl.pallas_call(
        functools.partial(_moe_ffn_kernel, layer),
        grid_spec=grid_spec,
        out_shape=jax.ShapeDtypeStruct(xs.shape, xs.dtype),
        compiler_params=_cparams(("arbitrary",)),
        name="moe_ffn",
    )(*plan, xs, b1, b2, w1, w2)


def _chunk_plan_kernel(cnt_ref, be_ref, nx_ref, br_ref, fl_ref):
    def expert(e, carry):
        step0, prev_first = carry
        c = cnt_ref[e]
        n_blk = (c + FFN_BLOCK - 1) // FFN_BLOCK

        def block(j, _):
            s = step0 + j
            be_ref[s] = e
            nx_ref[s] = e
            br_ref[s] = e * (EXPERT_CAP // FFN_BLOCK) + j
            fl_ref[s] = (STEP_VALID + jnp.where(j == 0, STEP_FIRST, 0)
                         + jnp.where(c - j * FFN_BLOCK > SLOT_CHUNK, STEP_FULL, 0))
            return 0

        lax.fori_loop(0, n_blk, block, 0)

        @pl.when((n_blk > 0) & (prev_first >= 0))
        def _():
            nx_ref[prev_first] = e
            fl_ref[prev_first] = fl_ref[prev_first] + STEP_HAS_NEXT

        return step0 + n_blk, jnp.where(n_blk > 0, step0, prev_first)

    used, _ = lax.fori_loop(0, N_EXPERTS, expert, (jnp.int32(0), jnp.int32(-1)))

    def idle(s, _):
        be_ref[s] = be_ref[used - 1]
        nx_ref[s] = be_ref[used - 1]
        br_ref[s] = br_ref[used - 1]
        fl_ref[s] = 0
        return 0

    lax.fori_loop(used, N_CHUNK_STEPS, idle, 0)


def _chunk_plan(cnt):
    smem = pl.BlockSpec(memory_space=pltpu.SMEM)
    return pl.pallas_call(
        _chunk_plan_kernel,
        in_specs=[smem],
        out_specs=[smem] * 4,
        out_shape=[jax.ShapeDtypeStruct((N_CHUNK_STEPS,), jnp.int32)] * 4,
        name="chunk_plan",
    )(cnt)


def _combine_kernel(x_ref, yg_ref, w_ref, g_ref, *o_refs):
    def emit(o_ref):
        w = w_ref[...]
        for c in range(SC_SPLIT):
            parts = [_unpack_pairs(yg_ref[k, c]) for k in range(TOP_K)]
            for half in range(2):
                cs = slice((2 * c + half) * SC_ROW, (2 * c + half + 1) * SC_ROW)
                acc = w[:, 0:1] * parts[0][half]
                for k in range(1, TOP_K):
                    acc = acc + w[:, k:k + 1] * parts[k][half]
                o_ref[:, cs] = x_ref[:, cs] + g_ref[0, 0][:, cs] * acc

    if len(o_refs) == 1:
        emit(o_refs[0])
    else:
        pl.when(pl.program_id(0) < N_TILES_CTX)(lambda: emit(o_refs[0]))
        pl.when(pl.program_id(0) >= N_TILES_CTX)(lambda: emit(o_refs[1]))


def _combine(layer, xmid, yg, wsel, mod, split_out):
    tok = lambda w: pl.BlockSpec((TW, w), lambda i: (i, 0))
    if split_out:
        out_specs = [pl.BlockSpec((TW, D_MODEL), lambda i: (jnp.minimum(i, N_TILES_CTX - 1), 0)),
                     pl.BlockSpec((TW, D_MODEL), lambda i: (jnp.maximum(i - N_TILES_CTX, 0), 0))]
        out_shape = [jax.ShapeDtypeStruct((N_CTX, D_MODEL), F32), jax.ShapeDtypeStruct((N_SMP, D_MODEL), F32)]
    else:
        out_specs, out_shape = tok(D_MODEL), jax.ShapeDtypeStruct((N_TOK, D_MODEL), F32)
    return pl.pallas_call(
        _combine_kernel,
        grid=(N_TILES,),
        in_specs=[tok(D_MODEL), pl.BlockSpec((TOP_K, SC_SPLIT, TW, SC_ROW), lambda i: (0, 0, i, 0)), tok(LANES),
                  _mod_spec(layer, 5)],
        out_specs=out_specs,
        out_shape=out_shape,
        compiler_params=_cparams(("arbitrary",)),
        name="combine",
    )(xmid, yg, wsel, mod)


def _moe(layer, xmid, h2, dest, wsel, cnt, mod, w1, b1, w2, b2, split_out):
    n_slots = N_EXPERTS * EXPERT_CAP
    idx = dest[0:TOP_K][:, None, :] + (jnp.arange(SC_SPLIT, dtype=jnp.int32) * n_slots)[None, :, None]
    idx = idx.reshape(TOP_K, 1, SC_SPLIT * N_TOK)
    xs = _sc_scatter_rows(h2.reshape(SC_SPLIT * N_TOK, SC_ROW), [idx[k] for k in range(TOP_K)], SC_SPLIT * n_slots)
    plan = _chunk_plan(cnt[0, :N_EXPERTS].astype(jnp.int32))
    y = _moe_ffn(layer, xs.reshape(SC_SPLIT, n_slots, SC_ROW), plan, w1, b1, w2, b2)
    yg = _sc_gather_rows(y.reshape(SC_SPLIT * n_slots, SC_ROW), idx.reshape(1, TOP_K * SC_SPLIT * N_TOK))
    return _combine(layer, xmid, yg.reshape(TOP_K, SC_SPLIT, N_TOK, SC_ROW), wsel, mod, split_out)


def _rope_tables():
    pos = np.arange(S_SMP)
    half = MLA_ROPE // 2
    inv_freq = (ROPE_THETA ** (-(np.arange(0, half, 2, dtype=np.float32) / np.float32(half)))).astype(np.float32)
    angs = [((pos // GRID_W).astype(np.float32)[:, None] * inv_freq[None, :]).astype(np.float32),
            ((pos % GRID_W).astype(np.float32)[:, None] * inv_freq[None, :]).astype(np.float32)]
    nf = half // 2
    cos = np.ones((TW + S_SMP, LANES), np.float32)
    sin_a = np.zeros((TW + S_SMP, LANES), np.float32)
    sin_b = np.zeros((TW + S_SMP, LANES), np.float32)
    for axis, ang in enumerate(angs):
        base = MLA_NOPE + axis * half
        c, s = np.cos(ang.astype(np.float64)), np.sin(ang.astype(np.float64))
        cos[TW:, base:base + nf] = c
        cos[TW:, base + nf:base + half] = c
        sin_a[TW:, base:base + nf] = -s
        sin_b[TW:, base + nf:base + half] = s
    return jnp.asarray(cos), jnp.asarray(sin_a), jnp.asarray(sin_b)


def _pad_last(a, width):
    return jnp.pad(a, [(0, 0)] * (a.ndim - 1) + [(0, width - a.shape[-1])])


def kernel(x_prompt, x_sample, cache_mla_ckv, cache_mla_krope, state_mlstm_C, state_mlstm_n, state_mlstm_m, c, c_ctx, norm1, norm2, w_ada, b_ada, w_in, b_mlstm_gates, mlstm_norm, mla_q_a_norm, mla_kv_a_norm, w_uq, w_ukv, mla_q_norm, mla_k_norm, sg_norm, w_spatial, b_spatial, w_branch, w_out, w_router, b_router, w_exp1, b_exp1, w_exp2, b_exp2):
    w_in_r = _w_in_prep(w_in)
    w_uq_r = _pad_last(w_uq.reshape(DEPTH, MLA_Q_RANK, MLA_HEADS, MLA_QK), HEAD_PAD).reshape(
        DEPTH, MLA_Q_RANK, MLA_HEADS * HEAD_PAD).astype(BF16)
    w_ukv4 = w_ukv.reshape(DEPTH, MLA_KV_RANK, MLA_HEADS, MLA_NOPE + MLA_V)
    w_k_r = _pad_last(w_ukv4[..., :MLA_NOPE], HEAD_PAD).reshape(DEPTH, MLA_KV_RANK, MLA_HEADS * HEAD_PAD).astype(BF16)
    w_v_r = w_ukv4[..., MLA_NOPE:].reshape(DEPTH, MLA_KV_RANK, MLA_HEADS * MLA_V).astype(BF16)
    q_norm_p = _pad_last(mla_q_norm, HEAD_PAD).reshape(DEPTH, 1, HEAD_PAD)
    k_norm_p = _pad_last(mla_k_norm, HEAD_PAD).reshape(DEPTH, 1, HEAD_PAD)
    b_gates_p = jnp.pad(b_mlstm_gates, ((0, 0), (GATE_LANE0, LANES - GATE_LANE0 - 4 * ML_HEADS))).reshape(DEPTH, 1, LANES)
    b_sp = _pad_last(jnp.swapaxes(b_spatial, 1, 2), LANES)
    w_router_p = _pad_last(w_router, LANES)
    w_router_hi = w_router_p.astype(BF16)
    w_router_p = jnp.concatenate([w_router_hi, (w_router_p - w_router_hi.astype(F32)).astype(BF16)], axis=-1)
    b_router_p = jnp.pad(b_router, ((0, 0), (0, LANES - N_EXPERTS)), constant_values=-1e30).reshape(DEPTH, 1, LANES)
    r3 = lambda a: a.reshape(DEPTH, 1, a.shape[-1])
    cache_kr_pad = jnp.pad(cache_mla_krope, ((0, 0), (0, 0), (0, 0), (MLA_NOPE, LANES - MLA_QK)))
    rope_tabs = _rope_tables()

    cvec = jnp.concatenate([c_ctx[None, :], c, jnp.zeros((SUBLANES - 1 - N_SEQ_SMP, D_MODEL), F32)], axis=0)
    mod = _adaln(cvec, w_ada, b_ada).reshape(DEPTH, SUBLANES, 1, 6 * D_MODEL)
    b1 = b_exp1.reshape(DEPTH, N_EXPERTS, 1, 2 * D_EXPERT)
    b2 = b_exp2.reshape(DEPTH, N_EXPERTS, 1, D_MODEL)
    kk_cache, v_cache = _cache_kv(cache_mla_ckv, cache_kr_pad, w_k_r, w_v_r, k_norm_p)

    x = jnp.concatenate([x_prompt.reshape(N_CTX, D_MODEL), x_sample.reshape(N_SMP, D_MODEL)], axis=0)
    new_ckv = jnp.zeros((N_SEQ_CTX, DEPTH, S_CTX, MLA_KV_RANK), F32)
    new_kr = jnp.zeros((N_SEQ_CTX, DEPTH, S_CTX, MLA_ROPE), F32)
    states = (jnp.zeros((N_SEQ_CTX, DEPTH, 2, ML_HEADS, ML_DIM, ML_DIM), F32),
              jnp.zeros((N_SEQ_CTX, DEPTH, 2, ML_HEADS, ML_DIM), F32),
              jnp.zeros((N_SEQ_CTX, DEPTH, 2 * ML_HEADS, LANES), F32))
    for l in range(DEPTH):
        za, zs, zc, zb = _inproj(l, x, r3(norm1), mod, w_in_r)
        oa, *states = _mlstm(l, za, zs, b_gates_p, r3(mlstm_norm), states=states)
        (oa,) = _mlstm(l, za, zs, b_gates_p, r3(mlstm_norm), init=(state_mlstm_C, state_mlstm_n, state_mlstm_m), ctx_out=oa)
        q, kk, v, new_ckv, new_kr = _mla_prep(l, zs, r3(mla_q_a_norm), r3(mla_kv_a_norm), w_uq_r, w_k_r, w_v_r,
                                              q_norm_p, k_norm_p, rope_tabs, new_ckv, new_kr)
        ob = _attn_smp(l, q, kk, v, kk_cache, v_cache, _attn_ctx(q, kk, v))
        xmid, h2, dest, wsel, cnt = _merge(
            l, x, oa, ob, zc, zb, r3(sg_norm), w_spatial.astype(BF16), b_sp, w_branch.astype(BF16), w_out.astype(BF16),
            mod, r3(norm2), w_router_p, b_router_p)
        x = _moe(l, xmid, h2, dest, wsel, cnt, mod, w_exp1, b1, w_exp2, b2, split_out=l == DEPTH - 1)
    y_ctx, y_smp = x
    return (
        y_ctx.reshape(N_SEQ_CTX, S_CTX, D_MODEL),
        y_smp.reshape(N_SEQ_SMP, S_SMP, D_MODEL),
        new_ckv,
        new_kr,
        states[0],
        states[1],
        states[2][:, :, :, 0].reshape(N_SEQ_CTX, DEPTH, 2, ML_HEADS),
    )
```

```python
import functools

import numpy as np
import jax
import jax.numpy as jnp
from jax import lax
from jax.experimental import pallas as pl
from jax.experimental.pallas import tpu as pltpu
from jax.experimental.pallas import tpu_sc as plsc

F32 = jnp.float32
BF16 = jnp.bfloat16
HI = lax.Precision.HIGHEST

D_MODEL = 1024
N_SEQ_CTX, S_CTX = 32, 256
N_SEQ_SMP, S_SMP = 2, 1024
DEPTH = 4
PAST_LEN = 512
GRID_W = 64
EPS = 1e-6
ML_HEADS, ML_DIM = 4, 128
ML_WIDTH = ML_HEADS * ML_DIM
MLA_HEADS, MLA_NOPE, MLA_ROPE, MLA_V = 8, 64, 32, 64
MLA_QK = MLA_NOPE + MLA_ROPE
MLA_Q_RANK, MLA_KV_RANK = 256, 128
ROPE_THETA = 10000.0
SG_GROUPS, SG_DIM, SG_CHUNK = 4, 128, 128
SG_WIDTH = SG_GROUPS * SG_DIM
N_BRANCH = 3
N_EXPERTS, TOP_K, D_EXPERT = 32, 4, 1024
SWIGLU_LIMIT, SWIGLU_ALPHA = 7.0, 1.702

N_CTX = N_SEQ_CTX * S_CTX
N_SMP = N_SEQ_SMP * S_SMP
N_TOK = N_CTX + N_SMP

LANES = 128
SUBLANES = 8
VMEM_LIMIT = 56 * 1024 * 1024

TW = 1024
TW_IN = 512
N_TILES = N_TOK // TW
N_TILES_CTX = N_CTX // TW
TILES_PER_SMP_SEQ = S_SMP // TW
N_SEQ_BLOCKS = N_TOK // S_CTX
HEAD_PAD = LANES
TQ = 256
EXPERT_CAP = N_TOK
SLOT_CHUNK = 256
FFN_BLOCK = 2 * SLOT_CHUNK
N_CHUNK_STEPS = N_TOK * TOP_K // FFN_BLOCK + N_EXPERTS
SC_ROW = 256
SC_SPLIT = D_MODEL // (2 * SC_ROW)
SC_WIN = 128

ZA_W = 4 * ML_WIDTH
ZS_W = 512
ZC_W = 2 * SG_WIDTH
ZB_W = N_BRANCH * D_MODEL
ZIN_W = ZA_W + ZS_W + ZC_W + ZB_W
GATE_LANE0 = MLA_ROPE


def _cparams(sem):
    return pltpu.CompilerParams(dimension_semantics=sem, vmem_limit_bytes=VMEM_LIMIT)


def _mod_row(i, tile=None):
    tile = tile or TW
    return jnp.where(i < N_CTX // tile, 0, 1 + (i - N_CTX // tile) // (S_SMP // tile))


def _rms(x, g, n=None):
    ms = jnp.sum(x * x, axis=-1, keepdims=True) * (1.0 / (n or x.shape[-1]))
    return x * lax.rsqrt(ms + EPS) * g


def _gelu(x):
    return 0.5 * x * (1.0 + jnp.tanh(0.7978845608028654 * (x + 0.044715 * (x * x * x))))


def _pack_pairs(lo, hi):
    lo_bits = lax.bitcast_convert_type(lo.astype(BF16).astype(F32), jnp.uint32)
    hi_bits = lax.bitcast_convert_type(hi.astype(BF16).astype(F32), jnp.uint32)
    return (lo_bits >> 16) | (hi_bits & jnp.uint32(0xFFFF0000))


def _unpack_pairs(u):
    return (lax.bitcast_convert_type(u << 16, F32), lax.bitcast_convert_type(u & jnp.uint32(0xFFFF0000), F32))


def _dot(a, b):
    return jnp.dot(a, b, preferred_element_type=F32)


def _dot_nt(a, b):
    return lax.dot_general(a, b, (((1,), (1,)), ((), ())), preferred_element_type=F32)


def _adaln_kernel(c_ref, w_ref, b_ref, o_ref):
    c = c_ref[...]
    s = c * jax.nn.sigmoid(c)
    w = w_ref[0]
    s_hi, w_hi = s.astype(BF16), w.astype(BF16)
    s_lo, w_lo = (s - s_hi.astype(F32)).astype(BF16), (w - w_hi.astype(F32)).astype(BF16)
    o_ref[0] = _dot(s_hi, w_hi) + _dot(s_hi, w_lo) + _dot(s_lo, w_hi) + b_ref[0]


def _adaln(cvec, w_ada, b_ada):
    nchunk = 4
    cw = 6 * D_MODEL // nchunk
    return pl.pallas_call(
        _adaln_kernel,
        grid=(DEPTH, nchunk),
        in_specs=[
            pl.BlockSpec((SUBLANES, D_MODEL), lambda l, j: (0, 0)),
            pl.BlockSpec((1, D_MODEL, cw), lambda l, j: (l, 0, j)),
            pl.BlockSpec((1, 1, cw), lambda l, j: (l, 0, j)),
        ],
        out_specs=pl.BlockSpec((1, SUBLANES, cw), lambda l, j: (l, 0, j)),
        out_shape=jax.ShapeDtypeStruct((DEPTH, SUBLANES, 6 * D_MODEL), F32),
        compiler_params=_cparams(("arbitrary", "arbitrary")),
        name="adaln",
    )(cvec, w_ada, b_ada.reshape(DEPTH, 1, 6 * D_MODEL))


IN_SPLITS = (ML_WIDTH, ML_WIDTH, ML_WIDTH, ML_WIDTH, 4 * ML_HEADS, MLA_Q_RANK, MLA_KV_RANK, MLA_ROPE, SG_WIDTH, SG_WIDTH,
             N_BRANCH * D_MODEL)
IN_OFFS = tuple(int(v) for v in np.cumsum((0,) + IN_SPLITS))
D_IN = IN_OFFS[-1]
W_PREP_ROWS = 256
W_PREP_COLS = 512


def _w_in_prep_kernel(wt_ref, o_ref):
    o = IN_OFFS

    def put(c0, rows):
        o_ref[0, :, c0:c0 + W_PREP_COLS] = rows.T.astype(BF16)

    for c0 in range(0, ZA_W, W_PREP_COLS):
        scale = ML_DIM ** -0.5 if o[1] <= c0 < o[2] else 1.0
        put(c0, wt_ref[0, c0:c0 + W_PREP_COLS, :] * scale)
    pad = jnp.zeros((ZS_W - (o[8] - o[4]), W_PREP_ROWS), F32)
    put(ZA_W, jnp.concatenate([wt_ref[0, o[5]:o[8], :], wt_ref[0, o[4]:o[5], :], pad], axis=0))
    for c0 in range(ZA_W + ZS_W, ZIN_W, W_PREP_COLS):
        src = c0 - (ZA_W + ZS_W) + o[8]
        put(c0, wt_ref[0, src:src + W_PREP_COLS, :])


def _w_in_prep(w_in):
    return pl.pallas_call(
        _w_in_prep_kernel,
        grid=(DEPTH, D_MODEL // W_PREP_ROWS),
        in_specs=[pl.BlockSpec((1, D_IN, W_PREP_ROWS), lambda l, r: (l, 0, r))],
        out_specs=pl.BlockSpec((1, W_PREP_ROWS, ZIN_W), lambda l, r: (l, r, 0)),
        out_shape=jax.ShapeDtypeStruct((DEPTH, D_MODEL, ZIN_W), BF16),
        compiler_params=_cparams(("arbitrary", "arbitrary")),
        name="w_in_prep",
    )(jnp.swapaxes(w_in, 1, 2))


def _inproj_kernel(x_ref, g_ref, sh_ref, sc_ref, w_ref, za_ref, zs_ref, zc_ref, zb_ref):
    h = _rms(x_ref[...], g_ref[0]) * (1.0 + sc_ref[0, 0]) + sh_ref[0, 0]
    hb = h.astype(BF16)
    za_ref[...] = _dot(hb, w_ref[0, :, 0:ZA_W]).astype(BF16)
    zs_ref[...] = _dot(hb, w_ref[0, :, ZA_W:ZA_W + ZS_W])
    zc_ref[...] = _dot(hb, w_ref[0, :, ZA_W + ZS_W:ZA_W + ZS_W + ZC_W]).astype(BF16)
    zb_ref[...] = _dot(hb, w_ref[0, :, ZA_W + ZS_W + ZC_W:ZIN_W]).astype(BF16)


def _mod_spec(layer, k, tile=None):
    return pl.BlockSpec((1, 1, 1, D_MODEL), lambda i: (layer, _mod_row(i, tile), 0, k))


def _inproj(layer, x, norm1, mod, w_in_r):
    tok = lambda w: pl.BlockSpec((TW_IN, w), lambda i: (i, 0))
    return pl.pallas_call(
        _inproj_kernel,
        grid=(N_TOK // TW_IN,),
        in_specs=[
            tok(D_MODEL),
            pl.BlockSpec((1, 1, D_MODEL), lambda i: (layer, 0, 0)),
            _mod_spec(layer, 0, TW_IN),
            _mod_spec(layer, 1, TW_IN),
            pl.BlockSpec((1, D_MODEL, ZIN_W), lambda i: (layer, 0, 0)),
        ],
        out_specs=[tok(ZA_W), tok(ZS_W), tok(ZC_W), tok(ZB_W)],
        out_shape=[
            jax.ShapeDtypeStruct((N_TOK, ZA_W), BF16),
            jax.ShapeDtypeStruct((N_TOK, ZS_W), F32),
            jax.ShapeDtypeStruct((N_TOK, ZC_W), BF16),
            jax.ShapeDtypeStruct((N_TOK, ZB_W), BF16),
        ],
        compiler_params=_cparams(("arbitrary",)),
        name="inproj",
    )(x, norm1, mod, mod, w_in_r)


def _make_mlstm_kernel(seq, layer, has_init):
    nq = seq // TQ
    lane_if, lane_ff, lane_ib, lane_fb = (GATE_LANE0 + ML_HEADS * j for j in range(4))

    def kern(*refs):
        if has_init:
            body(*refs)
            return
        out = refs[10]
        b = pl.program_id(0)

        @pl.when(b < N_SEQ_CTX)
        def _():
            body(*refs)

        @pl.when(b >= N_SEQ_CTX)
        def _():
            out[...] = jnp.zeros_like(out)

    def body(*refs):
        if has_init:
            m0_ref, zq, zk, zv, zo, gz, bg, nrm, c0_ref, n0_ref, _, out, bp_scr, bs_scr = refs
        else:
            zq, zk, zv, zo, gz, bg, nrm, _, _, _, out, cf_ref, nf_ref, mf_ref, bp_scr, bs_scr = refs
        b = pl.program_id(0)
        g = gz[...] + bg[0]
        lane = lax.broadcasted_iota(jnp.int32, g.shape, 1)
        is_forget = ((lane >= lane_ff) & (lane < lane_ib)) | ((lane >= lane_fb) & (lane < lane_fb + ML_HEADS))
        log_sig = jnp.minimum(g, 0.0) - jnp.log1p(jnp.exp(-jnp.abs(g)))
        a = jnp.where(is_forget, log_sig, g)
        r_i = lax.broadcasted_iota(jnp.int32, (seq, seq), 0)
        c_i = lax.broadcasted_iota(jnp.int32, (seq, seq), 1)
        ltri = (c_i <= r_i).astype(F32)
        bp = jnp.dot(ltri, a, precision=HI, preferred_element_type=F32)
        bs = bp[seq - 1:seq, :] - bp + a
        bp_scr[...] = bp
        bs_scr[...] = bs
        eye = (lax.broadcasted_iota(jnp.int32, (LANES, LANES), 0)
               == lax.broadcasted_iota(jnp.int32, (LANES, LANES), 1)).astype(F32)
        tr = lambda x: lax.dot_general(eye, x, (((1,), (1,)), ((), ())), precision=HI, preferred_element_type=F32)
        if has_init:
            tr = lambda x: x.T
        a_t, bp_t, bs_t = tr(a), tr(bp), tr(bs)

        for h in range(ML_HEADS):
            hs = slice(h * ML_DIM, (h + 1) * ML_DIM)
            k = zk[:, hs]
            v = zv[:, hs]
            first_lane = lax.broadcasted_iota(jnp.int32, (seq, ML_DIM), 1) == 0
            v_aug = jnp.concatenate([v, jnp.where(first_lane, 1.0, 0.0).astype(BF16)], axis=1)
            rows = (
                a_t[lane_if + h:lane_if + h + 1, :] - bp_t[lane_ff + h:lane_ff + h + 1, :],
                a_t[lane_ib + h:lane_ib + h + 1, :] - bs_t[lane_fb + h:lane_fb + h + 1, :],
            )
            col_refs = ((bp_scr, lane_ff + h), (bs_scr, lane_fb + h))
            if has_init:
                m0 = tuple(m0_ref[((b * DEPTH + layer) * 2 + dr) * ML_HEADS + h] for dr in range(2))
                c0 = tuple(c0_ref[0, 0, dr, h].astype(BF16) for dr in range(2))
                n0 = tuple(jnp.broadcast_to(n0_ref[0, 0, dr, h:h + 1, :], (ML_DIM, ML_DIM)).astype(BF16) for dr in range(2))
            else:
                m0 = (0.0, 0.0)

            def qblock(qi, carry):
                q0 = pl.multiple_of(qi * TQ, TQ)
                qb = zq[pl.ds(q0, TQ), hs]
                sc = _dot_nt(qb, k)
                t_idx = q0 + lax.broadcasted_iota(jnp.int32, (TQ, seq), 0)
                s_idx = lax.broadcasted_iota(jnp.int32, (TQ, seq), 1)
                hsum = jnp.zeros((TQ, ML_DIM), F32)
                for dr in range(2):
                    cref, cl = col_refs[dr]
                    col = cref[pl.ds(q0, TQ), cl:cl + 1]
                    mask = (s_idx <= t_idx) if dr == 0 else (s_idx >= t_idx)
                    drow = jnp.where(mask, rows[dr], -jnp.inf)
                    c_t = jnp.maximum(m0[dr], jnp.max(drow, axis=1, keepdims=True))
                    s = sc * jnp.exp(drow - c_t)
                    na = _dot(s.astype(BF16), v_aug)
                    num, den = na[:, 0:ML_DIM], na[:, ML_DIM:ML_DIM + 1]
                    if has_init:
                        w_c = jnp.exp(m0[dr] - c_t)
                        num = num + w_c * _dot(qb, c0[dr])
                        den = den + w_c * _dot_nt(qb, n0[dr])[:, 0:1]
                    hsum = hsum + num / jnp.maximum(jnp.abs(den), jnp.exp(-(col + c_t)))
                hn = _rms(hsum, nrm[0][:, hs])
                og = zo[pl.ds(q0, TQ), hs].astype(F32)
                out[pl.ds(q0, TQ), hs] = (hn * jax.nn.sigmoid(og)).astype(out.dtype)
                return carry

            if nq == 1:
                qblock(0, 0)
            else:
                lax.fori_loop(0, nq, qblock, 0)

            if not has_init:
                k_t = _dot_nt(eye.astype(BF16), k)
                kf = k.astype(F32)
                tot = (bp_t[lane_ff + h:lane_ff + h + 1, seq - 1:seq], bp_t[lane_fb + h:lane_fb + h + 1, seq - 1:seq])
                gl = (
                    tot[0] + rows[0],
                    bp_t[lane_fb + h:lane_fb + h + 1, :] - a_t[lane_fb + h:lane_fb + h + 1, :]
                    + a_t[lane_ib + h:lane_ib + h + 1, :],
                )
                for dr in range(2):
                    m_new = jnp.maximum(tot[dr] + m0[dr], jnp.max(gl[dr], axis=1, keepdims=True))
                    w_s = jnp.exp(gl[dr] - m_new)
                    cf_ref[0, 0, dr, h] = _dot((k_t * w_s).astype(BF16), v)
                    n_new = jnp.dot(jnp.broadcast_to(w_s, (SUBLANES, seq)), kf, precision=HI, preferred_element_type=F32)
                    nf_ref[0, 0, dr, h:h + 1, :] = n_new[0:1, :]
                    mf_ref[0, 0, dr * ML_HEADS + h:dr * ML_HEADS + h + 1, :] = jnp.broadcast_to(m_new, (1, LANES))

    return kern


def _mlstm(layer, za, zs, b_gates, mlstm_norm, init=None, ctx_out=None, states=None):
    has_init = init is not None
    seq, nseq, row0 = (S_SMP, N_SEQ_SMP, N_CTX // S_SMP) if has_init else (S_CTX, N_SEQ_CTX, 0)
    qkvo = [pl.BlockSpec((seq, ML_WIDTH), functools.partial(lambda j, b: (row0 + b, j), j)) for j in range(4)]
    in_specs = qkvo + [
        pl.BlockSpec((seq, LANES), lambda b: (row0 + b, ZS_W // LANES - 1)),
        pl.BlockSpec((1, 1, LANES), lambda b: (layer, 0, 0)),
        pl.BlockSpec((1, 1, ML_WIDTH), lambda b: (layer, 0, 0)),
    ]
    args = [za, za, za, za, zs, b_gates, mlstm_norm]
    out_specs = [pl.BlockSpec((seq, ML_WIDTH), lambda b: (row0 + b, 0))]
    out_shape = [jax.ShapeDtypeStruct((N_TOK, ML_WIDTH), BF16)]
    aliases = {}
    if has_init:
        st_c, st_n, st_m = init
        in_specs = [pl.BlockSpec(memory_space=pltpu.SMEM)] + in_specs + [
            pl.BlockSpec((1, 1, 2, ML_HEADS, ML_DIM, ML_DIM), lambda b: (b, layer, 0, 0, 0, 0)),
            pl.BlockSpec((1, 1, 2, ML_HEADS, ML_DIM), lambda b: (b, layer, 0, 0, 0)),
            pl.BlockSpec(memory_space=pl.ANY),
        ]
        args = [st_m.reshape(-1)] + args + [st_c, st_n, ctx_out]
        aliases = {len(args) - 1: 0}
    else:
        seq_blk = lambda b: jnp.minimum(b, nseq - 1)
        in_specs += [pl.BlockSpec(memory_space=pl.ANY)] * 3
        args += list(states)
        aliases = {len(args) - 3 + j: 1 + j for j in range(3)}
        out_specs += [
            pl.BlockSpec((1, 1, 2, ML_HEADS, ML_DIM, ML_DIM), lambda b: (seq_blk(b), layer, 0, 0, 0, 0)),
            pl.BlockSpec((1, 1, 2, ML_HEADS, ML_DIM), lambda b: (seq_blk(b), layer, 0, 0, 0)),
            pl.BlockSpec((1, 1, 2 * ML_HEADS, LANES), lambda b: (seq_blk(b), layer, 0, 0)),
        ]
        out_shape += [jax.ShapeDtypeStruct(s.shape, s.dtype) for s in states]
    return pl.pallas_call(
        _make_mlstm_kernel(seq, layer, has_init),
        grid=(nseq if has_init else N_SEQ_BLOCKS,),
        in_specs=in_specs,
        out_specs=out_specs,
        out_shape=out_shape,
        scratch_shapes=[pltpu.VMEM((seq, LANES), F32), pltpu.VMEM((seq, LANES), F32)],
        input_output_aliases=aliases,
        compiler_params=_cparams(("arbitrary",)),
        name="mlstm_smp" if has_init else "mlstm_ctx",
    )(*args)


def _rope(x, cos, sin_a, sin_b):
    return x * cos + pltpu.roll(x, LANES - 8, 1) * sin_a + pltpu.roll(x, 8, 1) * sin_b


def _mla_prep_kernel(zs_ref, qa_ref, kva_ref, wuq_ref, wk_ref, wv_ref, qn_ref, kn_ref, cos_ref, sa_ref, sb_ref, _, __,
                     q_ref, kk_ref, v_ref, ckv_ref, kr_ref, qf_scr, kf_scr):
    cq = zs_ref[:, 0:MLA_Q_RANK]
    ckv = zs_ref[:, MLA_Q_RANK:MLA_Q_RANK + MLA_KV_RANK]
    last = zs_ref[:, ZS_W - LANES:ZS_W]
    qf_scr[...] = _dot(_rms(cq, qa_ref[0]).astype(BF16), wuq_ref[0])
    ckvn = _rms(ckv, kva_ref[0])

    @pl.when(pl.program_id(0) < N_TILES_CTX)
    def _():
        for j in range(TW // S_CTX):
            ckv_ref[j, 0] = ckvn[j * S_CTX:(j + 1) * S_CTX, :]
            kr_ref[j, 0] = last[j * S_CTX:(j + 1) * S_CTX, 0:MLA_ROPE]

    cb = ckvn.astype(BF16)
    kf_scr[...] = _dot(cb, wk_ref[0])
    v_ref[...] = _dot(cb, wv_ref[0]).astype(BF16)
    lane = lax.broadcasted_iota(jnp.int32, last.shape, 1)
    kr = jnp.where((lane >= MLA_NOPE) & (lane < MLA_QK), pltpu.roll(last, MLA_NOPE, 1), 0.0)
    is_latent = pl.program_id(0) >= N_TILES_CTX

    def heads(rotate):
        for h in range(MLA_HEADS):
            hs = slice(h * HEAD_PAD, (h + 1) * HEAD_PAD)
            q_ref[:, hs] = rotate(_rms(qf_scr[:, hs], qn_ref[0], n=MLA_QK)).astype(BF16)
            kk_ref[:, hs] = rotate(_rms(kf_scr[:, hs] + kr, kn_ref[0], n=MLA_QK)).astype(BF16)

    @pl.when(is_latent)
    def _():
        cos, sa, sb = cos_ref[...], sa_ref[...], sb_ref[...]
        heads(lambda x: _rope(x, cos, sa, sb))

    @pl.when(jnp.logical_not(is_latent))
    def _():
        heads(lambda x: x)


def _mla_prep(layer, zs, q_a_norm, kv_a_norm, w_uq_r, w_k_r, w_v_r, q_norm_p, k_norm_p, rope_tabs, new_ckv, new_kr):
    seq_blk = lambda w: pl.BlockSpec((TW // S_CTX, 1, S_CTX, w), lambda i: (jnp.minimum(i, N_TILES_CTX - 1), layer, 0, 0))
    lw = lambda shape: pl.BlockSpec((1,) + shape, lambda i: (layer,) + (0,) * len(shape))
    tab = pl.BlockSpec((TW, LANES), lambda i: (jnp.where(i < N_TILES_CTX, 0, 1 + (i - N_TILES_CTX) % TILES_PER_SMP_SEQ), 0))
    tok = lambda w: pl.BlockSpec((TW, w), lambda i: (i, 0))
    return pl.pallas_call(
        _mla_prep_kernel,
        grid=(N_TILES,),
        in_specs=[
            tok(ZS_W), lw((1, MLA_Q_RANK)), lw((1, MLA_KV_RANK)),
            lw((MLA_Q_RANK, MLA_HEADS * HEAD_PAD)), lw((MLA_KV_RANK, MLA_HEADS * HEAD_PAD)),
            lw((MLA_KV_RANK, MLA_HEADS * MLA_V)), lw((1, HEAD_PAD)), lw((1, HEAD_PAD)), tab, tab, tab,
            pl.BlockSpec(memory_space=pl.ANY), pl.BlockSpec(memory_space=pl.ANY),
        ],
        out_specs=[tok(MLA_HEADS * HEAD_PAD), tok(MLA_HEADS * HEAD_PAD), tok(MLA_HEADS * MLA_V),
                   seq_blk(MLA_KV_RANK), seq_blk(MLA_ROPE)],
        input_output_aliases={11: 3, 12: 4},
        out_shape=[
            jax.ShapeDtypeStruct((N_TOK, MLA_HEADS * HEAD_PAD), BF16),
            jax.ShapeDtypeStruct((N_TOK, MLA_HEADS * HEAD_PAD), BF16),
            jax.ShapeDtypeStruct((N_TOK, MLA_HEADS * MLA_V), BF16),
            jax.ShapeDtypeStruct(new_ckv.shape, F32),
            jax.ShapeDtypeStruct(new_kr.shape, F32),
        ],
        scratch_shapes=[pltpu.VMEM((TW, MLA_HEADS * HEAD_PAD), F32), pltpu.VMEM((TW, MLA_HEADS * HEAD_PAD), F32)],
        compiler_params=_cparams(("arbitrary",)),
        name="mla_prep",
    )(zs, q_a_norm, kv_a_norm, w_uq_r, w_k_r, w_v_r, q_norm_p, k_norm_p, *rope_tabs, new_ckv, new_kr)


def _cache_kv_kernel(ckv_ref, kr_ref, wk_ref, wv_ref, kn_ref, kk_ref, v_ref):
    cb = ckv_ref[...].astype(BF16)
    kf = _dot(cb, wk_ref[0])
    v_ref[...] = _dot(cb, wv_ref[0]).astype(BF16)
    kr = kr_ref[...]
    for h in range(MLA_HEADS):
        hs = slice(h * HEAD_PAD, (h + 1) * HEAD_PAD)
        kk_ref[:, hs] = _rms(kf[:, hs] + kr, kn_ref[0], n=MLA_QK).astype(BF16)


def _cache_kv(cache_ckv, cache_kr_pad, w_k_r, w_v_r, k_norm_p):
    lw = lambda shape: pl.BlockSpec((1,) + shape, lambda b, l: (l,) + (0,) * len(shape))
    blk = lambda w: pl.BlockSpec((None, None, PAST_LEN, w), lambda b, l: (b, l, 0, 0))
    return pl.pallas_call(
        _cache_kv_kernel,
        grid=(N_SEQ_SMP, DEPTH),
        in_specs=[blk(MLA_KV_RANK), blk(LANES), lw((MLA_KV_RANK, MLA_HEADS * HEAD_PAD)),
                  lw((MLA_KV_RANK, MLA_HEADS * MLA_V)), lw((1, HEAD_PAD))],
        out_specs=[blk(MLA_HEADS * HEAD_PAD), blk(MLA_HEADS * MLA_V)],
        out_shape=[
            jax.ShapeDtypeStruct((N_SEQ_SMP, DEPTH, PAST_LEN, MLA_HEADS * HEAD_PAD), BF16),
            jax.ShapeDtypeStruct((N_SEQ_SMP, DEPTH, PAST_LEN, MLA_HEADS * MLA_V), BF16),
        ],
        compiler_params=_cparams(("arbitrary", "arbitrary")),
        name="cache_kv",
    )(cache_ckv, cache_kr_pad, w_k_r, w_v_r, k_norm_p)


def _make_attn_kernel(n_src):
    scale = MLA_QK ** -0.5

    def kern(q_ref, *refs):
        o_ref = refs[-1]
        if n_src > 1:
            body(q_ref, *refs)
            return

        @pl.when(pl.program_id(0) < N_SEQ_CTX)
        def _():
            body(q_ref, *refs)

        @pl.when(pl.program_id(0) >= N_SEQ_CTX)
        def _():
            o_ref[...] = jnp.zeros_like(o_ref)

    def body(q_ref, *refs):
        o_ref = refs[-1]
        for h in range(MLA_HEADS):
            hs = slice(h * HEAD_PAD, (h + 1) * HEAD_PAD)
            vs = slice(h * MLA_V, (h + 1) * MLA_V)
            q = q_ref[:, hs]
            ss = [_dot_nt(q, refs[2 * j][:, hs]) * scale for j in range(n_src)]
            m = functools.reduce(jnp.maximum, [jnp.max(s, axis=1, keepdims=True) for s in ss])
            ps = [jnp.exp(s - m) for s in ss]
            l = functools.reduce(jnp.add, [jnp.sum(p, axis=1, keepdims=True) for p in ps])
            o = functools.reduce(jnp.add, [_dot(ps[j].astype(BF16), refs[2 * j + 1][:, vs]) for j in range(n_src)])
            o_ref[:, vs] = (o / l).astype(o_ref.dtype)

    return kern


def _attn_ctx(q, kk, v):
    blk = lambda w: pl.BlockSpec((S_CTX, w), lambda b: (b, 0))
    return pl.pallas_call(
        _make_attn_kernel(1),
        grid=(N_SEQ_BLOCKS,),
        in_specs=[blk(MLA_HEADS * HEAD_PAD), blk(MLA_HEADS * HEAD_PAD), blk(MLA_HEADS * MLA_V)],
        out_specs=blk(MLA_HEADS * MLA_V),
        out_shape=jax.ShapeDtypeStruct((N_TOK, MLA_HEADS * MLA_V), BF16),
        compiler_params=_cparams(("arbitrary",)),
        name="attn_ctx",
    )(q, kk, v)


def _attn_smp(layer, q, kk, v, kk_cache, v_cache, ctx_out):
    row0 = N_CTX // S_SMP
    nq = S_SMP // TQ
    seqb = lambda w: pl.BlockSpec((S_SMP, w), lambda b, i: (row0 + b, 0))
    cache = lambda w: pl.BlockSpec((None, None, PAST_LEN, w), lambda b, i: (b, layer, 0, 0))
    return pl.pallas_call(
        _make_attn_kernel(2),
        grid=(N_SEQ_SMP, nq),
        in_specs=[
            pl.BlockSpec((TQ, MLA_HEADS * HEAD_PAD), lambda b, i: (N_CTX // TQ + b * nq + i, 0)),
            seqb(MLA_HEADS * HEAD_PAD), seqb(MLA_HEADS * MLA_V),
            cache(MLA_HEADS * HEAD_PAD), cache(MLA_HEADS * MLA_V),
            pl.BlockSpec(memory_space=pl.ANY),
        ],
        out_specs=pl.BlockSpec((TQ, MLA_HEADS * MLA_V), lambda b, i: (N_CTX // TQ + b * nq + i, 0)),
        out_shape=jax.ShapeDtypeStruct((N_TOK, MLA_HEADS * MLA_V), BF16),
        input_output_aliases={5: 0},
        compiler_params=_cparams(("arbitrary", "arbitrary")),
        name="attn_smp",
    )(q, kk, v, kk_cache, v_cache, ctx_out)


def _merge_kernel(x_ref, oa_ref, ob_ref, zc_ref, zb_ref, sgn_ref, ws_ref, bs_ref, wb_ref, wo_ref, g1_ref, n2_ref,
                  sh2_ref, sc2_ref, wr_ref, br_ref,
                  xmid_ref, h2_ref, dest_ref, wsel_ref, cnt_ref, oc_scr, carry_scr):
    i = pl.program_id(0)

    @pl.when(i == 0)
    def _():
        carry_scr[...] = jnp.zeros_like(carry_scr)

    u = _gelu(zc_ref[:, 0:SG_WIDTH].astype(F32))
    vg = _gelu(zc_ref[:, SG_WIDTH:2 * SG_WIDTH].astype(F32))
    for g in range(SG_GROUPS):
        gs = slice(g * SG_DIM, (g + 1) * SG_DIM)
        vn = _rms(vg[:, gs], sgn_ref[0][:, gs]).astype(BF16)
        for c in range(TW // SG_CHUNK):
            cs = slice(c * SG_CHUNK, (c + 1) * SG_CHUNK)
            mixed = _dot(ws_ref[0, g], vn[cs, :]) + bs_ref[0][:, g:g + 1]
            oc_scr[cs, gs] = (u[cs, gs] * mixed).astype(BF16)

    acc = jnp.zeros((TW, D_MODEL), F32)
    for j, src in enumerate((oa_ref, ob_ref, oc_scr)):
        gate = jax.nn.sigmoid(zb_ref[:, j * D_MODEL:(j + 1) * D_MODEL].astype(F32))
        acc = acc + gate * _dot(src[...], wb_ref[0, j])
    xm = x_ref[...] + g1_ref[0, 0] * _dot(acc.astype(BF16), wo_ref[0])
    xmid_ref[...] = xm
    h2 = _rms(xm, n2_ref[0]) * (1.0 + sc2_ref[0, 0]) + sh2_ref[0, 0]
    for c in range(SC_SPLIT):
        h2_ref[c] = _pack_pairs(h2[:, 2 * c * SC_ROW:(2 * c + 1) * SC_ROW], h2[:, (2 * c + 1) * SC_ROW:(2 * c + 2) * SC_ROW])

    h_hi = h2.astype(BF16)
    h_lo = (h2 - h_hi.astype(F32)).astype(BF16)
    p_hi = _dot(h_hi, wr_ref[0])
    logits = p_hi[:, 0:LANES] + p_hi[:, LANES:2 * LANES] + _dot(h_lo, wr_ref[0, :, 0:LANES]) + br_ref[0]
    lane = lax.broadcasted_iota(jnp.int32, logits.shape, 1)
    hits, exps = [], []
    sel = jnp.zeros(logits.shape, F32)
    denom = jnp.zeros((TW, 1), F32)
    top = None
    for _ in range(TOP_K):
        m = jnp.max(logits, axis=1, keepdims=True)
        idx = jnp.min(jnp.where(logits == m, lane, LANES), axis=1, keepdims=True)
        hit = lane == idx
        top = m if top is None else top
        hits.append(hit)
        exps.append(jnp.exp(m - top))
        sel = jnp.where(hit, 1.0, sel)
        denom = denom + exps[-1]
        logits = jnp.where(hit, -jnp.inf, logits)

    r_i = lax.broadcasted_iota(jnp.int32, (TW, TW), 0)
    c_i = lax.broadcasted_iota(jnp.int32, (TW, TW), 1)
    carry = carry_scr[0:1, :]
    rank = _dot(jnp.where(c_i < r_i, 1.0, 0.0).astype(BF16), sel.astype(BF16)) + carry
    new_carry = carry + jnp.sum(sel, axis=0, keepdims=True)
    carry_scr[...] = jnp.broadcast_to(new_carry, (SUBLANES, LANES))
    cnt_ref[...] = jnp.broadcast_to(new_carry, (SUBLANES, LANES))
    slot = rank + lane.astype(F32) * float(EXPERT_CAP)
    dmat = jnp.zeros(logits.shape, F32)
    wmat = jnp.zeros(logits.shape, F32)
    for k in range(TOP_K):
        dk = jnp.sum(jnp.where(hits[k], slot, 0.0), axis=1, keepdims=True)
        dmat = jnp.where(lane == k, dk, dmat)
        wmat = jnp.where(lane == k, exps[k] / denom, wmat)
    dest_ref[...] = dmat.T[0:SUBLANES, :].astype(jnp.int32)
    wsel_ref[...] = wmat


def _merge(layer, x, oa, ob, zc, zb, sg_norm, w_sp, b_sp, w_branch, w_out, mod, norm2, w_router_p, b_router_p):
    lw = lambda shape: pl.BlockSpec((1,) + shape, lambda i: (layer,) + (0,) * len(shape))
    tok = lambda w: pl.BlockSpec((TW, w), lambda i: (i, 0))
    return pl.pallas_call(
        _merge_kernel,
        grid=(N_TILES,),
        in_specs=[
            tok(D_MODEL), tok(ML_WIDTH), tok(MLA_HEADS * MLA_V), tok(ZC_W), tok(ZB_W),
            lw((1, SG_WIDTH)), lw((SG_GROUPS, SG_CHUNK, SG_CHUNK)), lw((SG_CHUNK, LANES)),
            lw((N_BRANCH, ML_WIDTH, D_MODEL)), lw((D_MODEL, D_MODEL)),
            _mod_spec(layer, 2), lw((1, D_MODEL)), _mod_spec(layer, 3), _mod_spec(layer, 4),
            lw((D_MODEL, 2 * LANES)), lw((1, LANES)),
        ],
        out_specs=[tok(D_MODEL), pl.BlockSpec((SC_SPLIT, TW, SC_ROW), lambda i: (0, i, 0)),
                   pl.BlockSpec((SUBLANES, TW), lambda i: (0, i)), tok(LANES),
                   pl.BlockSpec((SUBLANES, LANES), lambda i: (0, 0))],
        out_shape=[
            jax.ShapeDtypeStruct((N_TOK, D_MODEL), F32),
            jax.ShapeDtypeStruct((SC_SPLIT, N_TOK, SC_ROW), jnp.uint32),
            jax.ShapeDtypeStruct((SUBLANES, N_TOK), jnp.int32),
            jax.ShapeDtypeStruct((N_TOK, LANES), F32),
            jax.ShapeDtypeStruct((SUBLANES, LANES), F32),
        ],
        scratch_shapes=[pltpu.VMEM((TW, SG_WIDTH), BF16), pltpu.VMEM((SUBLANES, LANES), F32)],
        compiler_params=_cparams(("arbitrary",)),
        name="merge_router",
    )(x, oa, ob, zc, zb, sg_norm, w_sp, b_sp, w_branch, w_out, mod, norm2, mod, mod, w_router_p, b_router_p)


def _sc_mesh():
    return plsc.VectorSubcoreMesh(core_axis_name="core", subcore_axis_name="subcore")


def _sc_scatter_rows(x, idxs, n_rows):
    @pl.kernel(out_type=jax.ShapeDtypeStruct((n_rows, SC_ROW), x.dtype), mesh=_sc_mesh(), scratch_types=[])
    def scatter(x_hbm, *refs):
        o_hbm = refs[-1]

        def body(x_vmem, *i_vmems):
            for i_vmem in i_vmems:
                pltpu.sync_copy(x_vmem, o_hbm.at[i_vmem.at[0]])

        pltpu.emit_pipeline(
            body,
            grid=(x.shape[0] // SC_WIN,),
            in_specs=[pl.BlockSpec((SC_WIN, SC_ROW), lambda i: (i, 0))]
            + [pl.BlockSpec((1, SC_WIN), lambda i: (0, i))] * len(idxs),
            out_specs=[],
            core_axis_name=("core", "subcore"),
            dimension_semantics=(pltpu.PARALLEL,),
        )(x_hbm, *refs[:-1])

    return scatter(x, *idxs)


def _sc_gather_rows(x, idx):
    m = idx.shape[1]

    @pl.kernel(out_type=jax.ShapeDtypeStruct((m, SC_ROW), x.dtype), mesh=_sc_mesh())
    def gather(x_hbm, i_hbm, o_hbm):
        def body(i_vmem, o_vmem):
            pltpu.sync_copy(x_hbm.at[i_vmem.at[0]], o_vmem)

        pltpu.emit_pipeline(
            body,
            grid=(m // SC_WIN,),
            in_specs=[pl.BlockSpec((1, SC_WIN), lambda i: (0, i))],
            out_specs=[pl.BlockSpec((SC_WIN, SC_ROW), lambda i: (i, 0))],
            core_axis_name=("core", "subcore"),
            dimension_semantics=(pltpu.PARALLEL,),
        )(i_hbm, o_hbm)

    return gather(x, idx)


STEP_VALID, STEP_FIRST, STEP_HAS_NEXT, STEP_FULL = 1, 2, 4, 8


def _moe_ffn_kernel(layer, be_ref, nx_ref, br_ref, fl_ref, xs_ref, b1_ref, b2_ref, w1_hbm, w2_hbm, y_ref,
                    w1f, w2f, w1b, w2b, sem):
    g = pl.program_id(0)
    flags = fl_ref[g]

    def weight_copies(e):
        return (pltpu.make_async_copy(w1_hbm.at[layer, e], w1f, sem.at[0]),
                pltpu.make_async_copy(w2_hbm.at[layer, e], w2f, sem.at[1]))

    @pl.when(g == 0)
    def _():
        for cp in weight_copies(be_ref[0]):
            cp.start()

    @pl.when((flags & STEP_FIRST) != 0)
    def _():
        for cp in weight_copies(be_ref[g]):
            cp.wait()
        w1b[...] = w1f[...].astype(BF16)
        w2b[...] = w2f[...].astype(BF16)

        @pl.when((flags & STEP_HAS_NEXT) != 0)
        def _():
            for cp in weight_copies(nx_ref[g]):
                cp.start()

    def ffn(n_rows):
        halves = [h.astype(BF16) for c in range(SC_SPLIT) for h in _unpack_pairs(xs_ref[c, 0:n_rows, :])]
        g1 = _dot(jnp.concatenate(halves, axis=1), w1b[...]) + b1_ref[0, 0]
        gate = jnp.minimum(g1[:, :D_EXPERT], SWIGLU_LIMIT)
        up = jnp.clip(g1[:, D_EXPERT:], -SWIGLU_LIMIT, SWIGLU_LIMIT)
        act = gate * jax.nn.sigmoid(SWIGLU_ALPHA * gate) * (up + 1.0)
        y = _dot(act.astype(BF16), w2b[...]) + b2_ref[0, 0]
        for c in range(SC_SPLIT):
            y_ref[c, 0:n_rows, :] = _pack_pairs(
                y[:, 2 * c * SC_ROW:(2 * c + 1) * SC_ROW], y[:, (2 * c + 1) * SC_ROW:(2 * c + 2) * SC_ROW])

    pl.when((flags & (STEP_VALID | STEP_FULL)) == STEP_VALID)(lambda: ffn(SLOT_CHUNK))
    pl.when((flags & STEP_FULL) != 0)(lambda: ffn(FFN_BLOCK))


def _moe_ffn(layer, xs, plan, w1, b1, w2, b2):
    eb = lambda c: pl.BlockSpec((1, 1, 1, c), lambda g, be, nx, br, fl: (layer, be[g], 0, 0))
    rows = pl.BlockSpec((SC_SPLIT, FFN_BLOCK, SC_ROW), lambda g, be, nx, br, fl: (0, br[g], 0))
    hbm = pl.BlockSpec(memory_space=pl.ANY)
    grid_spec = pltpu.PrefetchScalarGridSpec(
        num_scalar_prefetch=4,
        grid=(N_CHUNK_STEPS,),
        in_specs=[rows, eb(2 * D_EXPERT), eb(D_MODEL), hbm, hbm],
        out_specs=rows,
        scratch_shapes=[
            pltpu.VMEM((D_MODEL, 2 * D_EXPERT), F32), pltpu.VMEM((D_EXPERT, D_MODEL), F32),
            pltpu.VMEM((D_MODEL, 2 * D_EXPERT), BF16), pltpu.VMEM((D_EXPERT, D_MODEL), BF16),
            pltpu.SemaphoreType.DMA((2,)),
        ],
    )
    return pl.pallas_call(
        functools.partial(_moe_ffn_kernel, layer),
        grid_spec=grid_spec,
        out_shape=jax.ShapeDtypeStruct(xs.shape, xs.dtype),
        compiler_params=_cparams(("arbitrary",)),
        name="moe_ffn",
    )(*plan, xs, b1, b2, w1, w2)


def _chunk_plan_kernel(cnt_ref, be_ref, nx_ref, br_ref, fl_ref):
    def expert(e, carry):
        step0, prev_first = carry
        c = cnt_ref[e]
        n_blk = (c + FFN_BLOCK - 1) // FFN_BLOCK

        def block(j, _):
            s = step0 + j
            be_ref[s] = e
            nx_ref[s] = e
            br_ref[s] = e * (EXPERT_CAP // FFN_BLOCK) + j
            fl_ref[s] = (STEP_VALID + jnp.where(j == 0, STEP_FIRST, 0)
                         + jnp.where(c - j * FFN_BLOCK > SLOT_CHUNK, STEP_FULL, 0))
            return 0

        lax.fori_loop(0, n_blk, block, 0)

        @pl.when((n_blk > 0) & (prev_first >= 0))
        def _():
            nx_ref[prev_first] = e
            fl_ref[prev_first] = fl_ref[prev_first] + STEP_HAS_NEXT

        return step0 + n_blk, jnp.where(n_blk > 0, step0, prev_first)

    used, _ = lax.fori_loop(0, N_EXPERTS, expert, (jnp.int32(0), jnp.int32(-1)))

    def idle(s, _):
        be_ref[s] = be_ref[used - 1]
        nx_ref[s] = be_ref[used - 1]
        br_ref[s] = br_ref[used - 1]
        fl_ref[s] = 0
        return 0

    lax.fori_loop(used, N_CHUNK_STEPS, idle, 0)


def _chunk_plan(cnt):
    smem = pl.BlockSpec(memory_space=pltpu.SMEM)
    return pl.pallas_call(
        _chunk_plan_kernel,
        in_specs=[smem],
        out_specs=[smem] * 4,
        out_shape=[jax.ShapeDtypeStruct((N_CHUNK_STEPS,), jnp.int32)] * 4,
        name="chunk_plan",
    )(cnt)


def _combine_kernel(x_ref, yg_ref, w_ref, g_ref, *o_refs):
    def emit(o_ref):
        w = w_ref[...]
        for c in range(SC_SPLIT):
            parts = [_unpack_pairs(yg_ref[k, c]) for k in range(TOP_K)]
            for half in range(2):
                cs = slice((2 * c + half) * SC_ROW, (2 * c + half + 1) * SC_ROW)
                acc = w[:, 0:1] * parts[0][half]
                for k in range(1, TOP_K):
                    acc = acc + w[:, k:k + 1] * parts[k][half]
                o_ref[:, cs] = x_ref[:, cs] + g_ref[0, 0][:, cs] * acc

    if len(o_refs) == 1:
        emit(o_refs[0])
    else:
        pl.when(pl.program_id(0) < N_TILES_CTX)(lambda: emit(o_refs[0]))
        pl.when(pl.program_id(0) >= N_TILES_CTX)(lambda: emit(o_refs[1]))


def _combine(layer, xmid, yg, wsel, mod, split_out):
    tok = lambda w: pl.BlockSpec((TW, w), lambda i: (i, 0))
    if split_out:
        out_specs = [pl.BlockSpec((TW, D_MODEL), lambda i: (jnp.minimum(i, N_TILES_CTX - 1), 0)),
                     pl.BlockSpec((TW, D_MODEL), lambda i: (jnp.maximum(i - N_TILES_CTX, 0), 0))]
        out_shape = [jax.ShapeDtypeStruct((N_CTX, D_MODEL), F32), jax.ShapeDtypeStruct((N_SMP, D_MODEL), F32)]
    else:
        out_specs, out_shape = tok(D_MODEL), jax.ShapeDtypeStruct((N_TOK, D_MODEL), F32)
    return pl.pallas_call(
        _combine_kernel,
        grid=(N_TILES,),
        in_specs=[tok(D_MODEL), pl.BlockSpec((TOP_K, SC_SPLIT, TW, SC_ROW), lambda i: (0, 0, i, 0)), tok(LANES),
                  _mod_spec(layer, 5)],
        out_specs=out_specs,
        out_shape=out_shape,
        compiler_params=_cparams(("arbitrary",)),
        name="combine",
    )(xmid, yg, wsel, mod)


def _moe(layer, xmid, h2, dest, wsel, cnt, mod, w1, b1, w2, b2, split_out):
    n_slots = N_EXPERTS * EXPERT_CAP
    idx = dest[0:TOP_K][:, None, :] + (jnp.arange(SC_SPLIT, dtype=jnp.int32) * n_slots)[None, :, None]
    idx = idx.reshape(TOP_K, 1, SC_SPLIT * N_TOK)
    xs = _sc_scatter_rows(h2.reshape(SC_SPLIT * N_TOK, SC_ROW), [idx[k] for k in range(TOP_K)], SC_SPLIT * n_slots)
    plan = _chunk_plan(cnt[0, :N_EXPERTS].astype(jnp.int32))
    y = _moe_ffn(layer, xs.reshape(SC_SPLIT, n_slots, SC_ROW), plan, w1, b1, w2, b2)
    yg = _sc_gather_rows(y.reshape(SC_SPLIT * n_slots, SC_ROW), idx.reshape(1, TOP_K * SC_SPLIT * N_TOK))
    return _combine(layer, xmid, yg.reshape(TOP_K, SC_SPLIT, N_TOK, SC_ROW), wsel, mod, split_out)


def _rope_tables():
    pos = np.arange(S_SMP)
    half = MLA_ROPE // 2
    inv_freq = (ROPE_THETA ** (-(np.arange(0, half, 2, dtype=np.float32) / np.float32(half)))).astype(np.float32)
    angs = [((pos // GRID_W).astype(np.float32)[:, None] * inv_freq[None, :]).astype(np.float32),
            ((pos % GRID_W).astype(np.float32)[:, None] * inv_freq[None, :]).astype(np.float32)]
    nf = half // 2
    cos = np.ones((TW + S_SMP, LANES), np.float32)
    sin_a = np.zeros((TW + S_SMP, LANES), np.float32)
    sin_b = np.zeros((TW + S_SMP, LANES), np.float32)
    for axis, ang in enumerate(angs):
        base = MLA_NOPE + axis * half
        c, s = np.cos(ang.astype(np.float64)), np.sin(ang.astype(np.float64))
        cos[TW:, base:base + nf] = c
        cos[TW:, base + nf:base + half] = c
        sin_a[TW:, base:base + nf] = -s
        sin_b[TW:, base + nf:base + half] = s
    return jnp.asarray(cos), jnp.asarray(sin_a), jnp.asarray(sin_b)


def _pad_last(a, width):
    return jnp.pad(a, [(0, 0)] * (a.ndim - 1) + [(0, width - a.shape[-1])])


def kernel(x_prompt, x_sample, cache_mla_ckv, cache_mla_krope, state_mlstm_C, state_mlstm_n, state_mlstm_m, c, c_ctx, norm1, norm2, w_ada, b_ada, w_in, b_mlstm_gates, mlstm_norm, mla_q_a_norm, mla_kv_a_norm, w_uq, w_ukv, mla_q_norm, mla_k_norm, sg_norm, w_spatial, b_spatial, w_branch, w_out, w_router, b_router, w_exp1, b_exp1, w_exp2, b_exp2):
    w_in_r = _w_in_prep(w_in)
    w_uq_r = _pad_last(w_uq.reshape(DEPTH, MLA_Q_RANK, MLA_HEADS, MLA_QK), HEAD_PAD).reshape(
        DEPTH, MLA_Q_RANK, MLA_HEADS * HEAD_PAD).astype(BF16)
    w_ukv4 = w_ukv.reshape(DEPTH, MLA_KV_RANK, MLA_HEADS, MLA_NOPE + MLA_V)
    w_k_r = _pad_last(w_ukv4[..., :MLA_NOPE], HEAD_PAD).reshape(DEPTH, MLA_KV_RANK, MLA_HEADS * HEAD_PAD).astype(BF16)
    w_v_r = w_ukv4[..., MLA_NOPE:].reshape(DEPTH, MLA_KV_RANK, MLA_HEADS * MLA_V).astype(BF16)
    q_norm_p = _pad_last(mla_q_norm, HEAD_PAD).reshape(DEPTH, 1, HEAD_PAD)
    k_norm_p = _pad_last(mla_k_norm, HEAD_PAD).reshape(DEPTH, 1, HEAD_PAD)
    b_gates_p = jnp.pad(b_mlstm_gates, ((0, 0), (GATE_LANE0, LANES - GATE_LANE0 - 4 * ML_HEADS))).reshape(DEPTH, 1, LANES)
    b_sp = _pad_last(jnp.swapaxes(b_spatial, 1, 2), LANES)
    w_router_p = _pad_last(w_router, LANES)
    w_router_hi = w_router_p.astype(BF16)
    w_router_p = jnp.concatenate([w_router_hi, (w_router_p - w_router_hi.astype(F32)).astype(BF16)], axis=-1)
    b_router_p = jnp.pad(b_router, ((0, 0), (0, LANES - N_EXPERTS)), constant_values=-1e30).reshape(DEPTH, 1, LANES)
    r3 = lambda a: a.reshape(DEPTH, 1, a.shape[-1])
    cache_kr_pad = jnp.pad(cache_mla_krope, ((0, 0), (0, 0), (0, 0), (MLA_NOPE, LANES - MLA_QK)))
    rope_tabs = _rope_tables()

    cvec = jnp.concatenate([c_ctx[None, :], c, jnp.zeros((SUBLANES - 1 - N_SEQ_SMP, D_MODEL), F32)], axis=0)
    mod = _adaln(cvec, w_ada, b_ada).reshape(DEPTH, SUBLANES, 1, 6 * D_MODEL)
    b1 = b_exp1.reshape(DEPTH, N_EXPERTS, 1, 2 * D_EXPERT)
    b2 = b_exp2.reshape(DEPTH, N_EXPERTS, 1, D_MODEL)
    kk_cache, v_cache = _cache_kv(cache_mla_ckv, cache_kr_pad, w_k_r, w_v_r, k_norm_p)

    x = jnp.concatenate([x_prompt.reshape(N_CTX, D_MODEL), x_sample.reshape(N_SMP, D_MODEL)], axis=0)
    new_ckv = jnp.zeros((N_SEQ_CTX, DEPTH, S_CTX, MLA_KV_RANK), F32)
    new_kr = jnp.zeros((N_SEQ_CTX, DEPTH, S_CTX, MLA_ROPE), F32)
    states = (jnp.zeros((N_SEQ_CTX, DEPTH, 2, ML_HEADS, ML_DIM, ML_DIM), F32),
              jnp.zeros((N_SEQ_CTX, DEPTH, 2, ML_HEADS, ML_DIM), F32),
              jnp.zeros((N_SEQ_CTX, DEPTH, 2 * ML_HEADS, LANES), F32))
    for l in range(DEPTH):
        za, zs, zc, zb = _inproj(l, x, r3(norm1), mod, w_in_r)
        oa, *states = _mlstm(l, za, zs, b_gates_p, r3(mlstm_norm), states=states)
        (oa,) = _mlstm(l, za, zs, b_gates_p, r3(mlstm_norm), init=(state_mlstm_C, state_mlstm_n, state_mlstm_m), ctx_out=oa)
        q, kk, v, new_ckv, new_kr = _mla_prep(l, zs, r3(mla_q_a_norm), r3(mla_kv_a_norm), w_uq_r, w_k_r, w_v_r,
                                              q_norm_p, k_norm_p, rope_tabs, new_ckv, new_kr)
        ob = _attn_smp(l, q, kk, v, kk_cache, v_cache, _attn_ctx(q, kk, v))
        xmid, h2, dest, wsel, cnt = _merge(
            l, x, oa, ob, zc, zb, r3(sg_norm), w_spatial.astype(BF16), b_sp, w_branch.astype(BF16), w_out.astype(BF16),
            mod, r3(norm2), w_router_p, b_router_p)
        x = _moe(l, xmid, h2, dest, wsel, cnt, mod, w_exp1, b1, w_exp2, b2, split_out=l == DEPTH - 1)
    y_ctx, y_smp = x
    return (
        y_ctx.reshape(N_SEQ_CTX, S_CTX, D_MODEL),
        y_smp.reshape(N_SEQ_SMP, S_SMP, D_MODEL),
        new_ckv,
        new_kr,
        states[0],
        states[1],
        states[2][:, :, :, 0].reshape(N_SEQ_CTX, DEPTH, 2, ML_HEADS),
    )
```

```python
import functools

import numpy as np
import jax
import jax.numpy as jnp
from jax import lax
from jax.experimental import pallas as pl
from jax.experimental.pallas import tpu as pltpu
from jax.experimental.pallas import tpu_sc as plsc

F32 = jnp.float32
BF16 = jnp.bfloat16
HI = lax.Precision.HIGHEST

D_MODEL = 1024
N_SEQ_CTX, S_CTX = 32, 256
N_SEQ_SMP, S_SMP = 2, 1024
DEPTH = 4
PAST_LEN = 512
GRID_W = 64
EPS = 1e-6
ML_HEADS, ML_DIM = 4, 128
ML_WIDTH = ML_HEADS * ML_DIM
MLA_HEADS, MLA_NOPE, MLA_ROPE, MLA_V = 8, 64, 32, 64
MLA_QK = MLA_NOPE + MLA_ROPE
MLA_Q_RANK, MLA_KV_RANK = 256, 128
ROPE_THETA = 10000.0
SG_GROUPS, SG_DIM, SG_CHUNK = 4, 128, 128
SG_WIDTH = SG_GROUPS * SG_DIM
N_BRANCH = 3
N_EXPERTS, TOP_K, D_EXPERT = 32, 4, 1024
SWIGLU_LIMIT, SWIGLU_ALPHA = 7.0, 1.702

N_CTX = N_SEQ_CTX * S_CTX
N_SMP = N_SEQ_SMP * S_SMP
N_TOK = N_CTX + N_SMP

LANES = 128
SUBLANES = 8
VMEM_LIMIT = 56 * 1024 * 1024

TW = 1024
TW_IN = 512
N_TILES = N_TOK // TW
N_TILES_CTX = N_CTX // TW
TILES_PER_SMP_SEQ = S_SMP // TW
N_SEQ_BLOCKS = N_TOK // S_CTX
HEAD_PAD = LANES
TQ = 256
EXPERT_CAP = N_TOK
SLOT_CHUNK = 256
FFN_BLOCK = 2 * SLOT_CHUNK
N_CHUNK_STEPS = N_TOK * TOP_K // FFN_BLOCK + N_EXPERTS
SC_ROW = 256
SC_SPLIT = D_MODEL // (2 * SC_ROW)
SC_WIN = 128

ZA_W = 4 * ML_WIDTH
ZS_W = 512
ZC_W = 2 * SG_WIDTH
ZB_W = N_BRANCH * D_MODEL
ZIN_W = ZA_W + ZS_W + ZC_W + ZB_W
GATE_LANE0 = MLA_ROPE


def _cparams(sem):
    return pltpu.CompilerParams(dimension_semantics=sem, vmem_limit_bytes=VMEM_LIMIT)


def _mod_row(i, tile=None):
    tile = tile or TW
    return jnp.where(i < N_CTX // tile, 0, 1 + (i - N_CTX // tile) // (S_SMP // tile))


def _rms(x, g, n=None):
    ms = jnp.sum(x * x, axis=-1, keepdims=True) * (1.0 / (n or x.shape[-1]))
    return x * lax.rsqrt(ms + EPS) * g


def _gelu(x):
    return 0.5 * x * (1.0 + jnp.tanh(0.7978845608028654 * (x + 0.044715 * (x * x * x))))


def _pack_pairs(lo, hi):
    lo_bits = lax.bitcast_convert_type(lo.astype(BF16).astype(F32), jnp.uint32)
    hi_bits = lax.bitcast_convert_type(hi.astype(BF16).astype(F32), jnp.uint32)
    return (lo_bits >> 16) | (hi_bits & jnp.uint32(0xFFFF0000))


def _unpack_pairs(u):
    return (lax.bitcast_convert_type(u << 16, F32), lax.bitcast_convert_type(u & jnp.uint32(0xFFFF0000), F32))


def _dot(a, b):
    return jnp.dot(a, b, preferred_element_type=F32)


def _dot_nt(a, b):
    return lax.dot_general(a, b, (((1,), (1,)), ((), ())), preferred_element_type=F32)


def _adaln_kernel(c_ref, w_ref, b_ref, o_ref):
    c = c_ref[...]
    s = c * jax.nn.sigmoid(c)
    w = w_ref[0]
    s_hi, w_hi = s.astype(BF16), w.astype(BF16)
    s_lo, w_lo = (s - s_hi.astype(F32)).astype(BF16), (w - w_hi.astype(F32)).astype(BF16)
    o_ref[0] = _dot(s_hi, w_hi) + _dot(s_hi, w_lo) + _dot(s_lo, w_hi) + b_ref[0]


def _adaln(cvec, w_ada, b_ada):
    nchunk = 4
    cw = 6 * D_MODEL // nchunk
    return pl.pallas_call(
        _adaln_kernel,
        grid=(DEPTH, nchunk),
        in_specs=[
            pl.BlockSpec((SUBLANES, D_MODEL), lambda l, j: (0, 0)),
            pl.BlockSpec((1, D_MODEL, cw), lambda l, j: (l, 0, j)),
            pl.BlockSpec((1, 1, cw), lambda l, j: (l, 0, j)),
        ],
        out_specs=pl.BlockSpec((1, SUBLANES, cw), lambda l, j: (l, 0, j)),
        out_shape=jax.ShapeDtypeStruct((DEPTH, SUBLANES, 6 * D_MODEL), F32),
        compiler_params=_cparams(("arbitrary", "arbitrary")),
        name="adaln",
    )(cvec, w_ada, b_ada.reshape(DEPTH, 1, 6 * D_MODEL))


IN_SPLITS = (ML_WIDTH, ML_WIDTH, ML_WIDTH, ML_WIDTH, 4 * ML_HEADS, MLA_Q_RANK, MLA_KV_RANK, MLA_ROPE, SG_WIDTH, SG_WIDTH,
             N_BRANCH * D_MODEL)
IN_OFFS = tuple(int(v) for v in np.cumsum((0,) + IN_SPLITS))
D_IN = IN_OFFS[-1]
W_PREP_ROWS = 256
W_PREP_COLS = 512


def _w_in_prep_kernel(wt_ref, o_ref):
    o = IN_OFFS

    def put(c0, rows):
        o_ref[0, :, c0:c0 + W_PREP_COLS] = rows.T.astype(BF16)

    for c0 in range(0, ZA_W, W_PREP_COLS):
        scale = ML_DIM ** -0.5 if o[1] <= c0 < o[2] else 1.0
        put(c0, wt_ref[0, c0:c0 + W_PREP_COLS, :] * scale)
    pad = jnp.zeros((ZS_W - (o[8] - o[4]), W_PREP_ROWS), F32)
    put(ZA_W, jnp.concatenate([wt_ref[0, o[5]:o[8], :], wt_ref[0, o[4]:o[5], :], pad], axis=0))
    for c0 in range(ZA_W + ZS_W, ZIN_W, W_PREP_COLS):
        src = c0 - (ZA_W + ZS_W) + o[8]
        put(c0, wt_ref[0, src:src + W_PREP_COLS, :])


def _w_in_prep(w_in):
    return pl.pallas_call(
        _w_in_prep_kernel,
        grid=(DEPTH, D_MODEL // W_PREP_ROWS),
        in_specs=[pl.BlockSpec((1, D_IN, W_PREP_ROWS), lambda l, r: (l, 0, r))],
        out_specs=pl.BlockSpec((1, W_PREP_ROWS, ZIN_W), lambda l, r: (l, r, 0)),
        out_shape=jax.ShapeDtypeStruct((DEPTH, D_MODEL, ZIN_W), BF16),
        compiler_params=_cparams(("arbitrary", "arbitrary")),
        name="w_in_prep",
    )(jnp.swapaxes(w_in, 1, 2))


def _tok_specs(x, tile):
    if isinstance(x, tuple):
        n_ctx = N_CTX // tile
        return [pl.BlockSpec((tile, D_MODEL), lambda i: (jnp.minimum(i, n_ctx - 1), 0)),
                pl.BlockSpec((tile, D_MODEL), lambda i: (jnp.maximum(i - n_ctx, 0), 0))], list(x)
    return [pl.BlockSpec((tile, D_MODEL), lambda i: (i, 0))], [x]


def _tok_value(x_refs, tile):
    if len(x_refs) == 2:
        return jnp.where(pl.program_id(0) < N_CTX // tile, x_refs[0][...], x_refs[1][...])
    return x_refs[0][...]


def _inproj_kernel(n_x, *refs):
    g_ref, sh_ref, sc_ref, w_ref, za_ref, zs_ref, zc_ref, zb_ref = refs[n_x:n_x + 8]
    x = _tok_value(refs[:n_x], TW_IN)
    if n_x == 2:
        refs[n_x + 8][...] = x
    h = _rms(x, g_ref[0]) * (1.0 + sc_ref[0, 0]) + sh_ref[0, 0]
    hb = h.astype(BF16)
    za_ref[...] = _dot(hb, w_ref[0, :, 0:ZA_W]).astype(BF16)
    zs_ref[...] = _dot(hb, w_ref[0, :, ZA_W:ZA_W + ZS_W])
    zc_ref[...] = _dot(hb, w_ref[0, :, ZA_W + ZS_W:ZA_W + ZS_W + ZC_W]).astype(BF16)
    zb_ref[...] = _dot(hb, w_ref[0, :, ZA_W + ZS_W + ZC_W:ZIN_W]).astype(BF16)


def _mod_spec(layer, k, tile=None):
    return pl.BlockSpec((1, 1, 1, D_MODEL), lambda i: (layer, _mod_row(i, tile), 0, k))


def _inproj(layer, x, norm1, mod, w_in_r):
    tok = lambda w: pl.BlockSpec((TW_IN, w), lambda i: (i, 0))
    x_specs, x_args = _tok_specs(x, TW_IN)
    return pl.pallas_call(
        functools.partial(_inproj_kernel, len(x_args)),
        grid=(N_TOK // TW_IN,),
        in_specs=x_specs + [
            pl.BlockSpec((1, 1, D_MODEL), lambda i: (layer, 0, 0)),
            _mod_spec(layer, 0, TW_IN),
            _mod_spec(layer, 1, TW_IN),
            pl.BlockSpec((1, D_MODEL, ZIN_W), lambda i: (layer, 0, 0)),
        ],
        out_specs=[tok(ZA_W), tok(ZS_W), tok(ZC_W), tok(ZB_W)] + [tok(D_MODEL)] * (len(x_args) - 1),
        out_shape=[
            jax.ShapeDtypeStruct((N_TOK, ZA_W), BF16),
            jax.ShapeDtypeStruct((N_TOK, ZS_W), F32),
            jax.ShapeDtypeStruct((N_TOK, ZC_W), BF16),
            jax.ShapeDtypeStruct((N_TOK, ZB_W), BF16),
        ] + [jax.ShapeDtypeStruct((N_TOK, D_MODEL), F32)] * (len(x_args) - 1),
        compiler_params=_cparams(("arbitrary",)),
        name="inproj",
    )(*x_args, norm1, mod, mod, w_in_r)


def _make_mlstm_kernel(seq, layer, has_init):
    nq = seq // TQ
    lane_if, lane_ff, lane_ib, lane_fb = (GATE_LANE0 + ML_HEADS * j for j in range(4))

    def kern(*refs):
        if has_init:
            body(*refs)
            return
        out = refs[10]
        b = pl.program_id(0)

        @pl.when(b < N_SEQ_CTX)
        def _():
            body(*refs)

        @pl.when(b >= N_SEQ_CTX)
        def _():
            out[...] = jnp.zeros_like(out)

    def body(*refs):
        if has_init:
            m0_ref, zq, zk, zv, zo, gz, bg, nrm, c0_ref, n0_ref, _, out, bp_scr, bs_scr = refs
        else:
            zq, zk, zv, zo, gz, bg, nrm, _, _, _, out, cf_ref, nf_ref, mf_ref, bp_scr, bs_scr = refs
        b = pl.program_id(0)
        g = gz[...] + bg[0]
        lane = lax.broadcasted_iota(jnp.int32, g.shape, 1)
        is_forget = ((lane >= lane_ff) & (lane < lane_ib)) | ((lane >= lane_fb) & (lane < lane_fb + ML_HEADS))
        log_sig = jnp.minimum(g, 0.0) - jnp.log1p(jnp.exp(-jnp.abs(g)))
        a = jnp.where(is_forget, log_sig, g)
        r_i = lax.broadcasted_iota(jnp.int32, (seq, seq), 0)
        c_i = lax.broadcasted_iota(jnp.int32, (seq, seq), 1)
        ltri = (c_i <= r_i).astype(F32)
        bp = jnp.dot(ltri, a, precision=HI, preferred_element_type=F32)
        bs = bp[seq - 1:seq, :] - bp + a
        bp_scr[...] = bp
        bs_scr[...] = bs
        eye = (lax.broadcasted_iota(jnp.int32, (LANES, LANES), 0)
               == lax.broadcasted_iota(jnp.int32, (LANES, LANES), 1)).astype(F32)
        tr = lambda x: lax.dot_general(eye, x, (((1,), (1,)), ((), ())), precision=HI, preferred_element_type=F32)
        if has_init:
            tr = lambda x: x.T
        a_t, bp_t, bs_t = tr(a), tr(bp), tr(bs)

        for h in range(ML_HEADS):
            hs = slice(h * ML_DIM, (h + 1) * ML_DIM)
            k = zk[:, hs]
            v = zv[:, hs]
            first_lane = lax.broadcasted_iota(jnp.int32, (seq, ML_DIM), 1) == 0
            v_aug = jnp.concatenate([v, jnp.where(first_lane, 1.0, 0.0).astype(BF16)], axis=1)
            rows = (
                a_t[lane_if + h:lane_if + h + 1, :] - bp_t[lane_ff + h:lane_ff + h + 1, :],
                a_t[lane_ib + h:lane_ib + h + 1, :] - bs_t[lane_fb + h:lane_fb + h + 1, :],
            )
            col_refs = ((bp_scr, lane_ff + h), (bs_scr, lane_fb + h))
            if has_init:
                m0 = tuple(m0_ref[((b * DEPTH + layer) * 2 + dr) * ML_HEADS + h] for dr in range(2))
                c0 = tuple(c0_ref[0, 0, dr, h].astype(BF16) for dr in range(2))
                n0 = tuple(jnp.broadcast_to(n0_ref[0, 0, dr, h:h + 1, :], (ML_DIM, ML_DIM)).astype(BF16) for dr in range(2))
            else:
                m0 = (0.0, 0.0)

            def qblock(qi, carry):
                q0 = pl.multiple_of(qi * TQ, TQ)
                qb = zq[pl.ds(q0, TQ), hs]
                sc = _dot_nt(qb, k)
                t_idx = q0 + lax.broadcasted_iota(jnp.int32, (TQ, seq), 0)
                s_idx = lax.broadcasted_iota(jnp.int32, (TQ, seq), 1)
                hsum = jnp.zeros((TQ, ML_DIM), F32)
                for dr in range(2):
                    cref, cl = col_refs[dr]
                    col = cref[pl.ds(q0, TQ), cl:cl + 1]
                    mask = (s_idx <= t_idx) if dr == 0 else (s_idx >= t_idx)
                    drow = jnp.where(mask, rows[dr], -jnp.inf)
                    c_t = jnp.maximum(m0[dr], jnp.max(drow, axis=1, keepdims=True))
                    s = sc * jnp.exp(drow - c_t)
                    na = _dot(s.astype(BF16), v_aug)
                    num, den = na[:, 0:ML_DIM], na[:, ML_DIM:ML_DIM + 1]
                    if has_init:
                        w_c = jnp.exp(m0[dr] - c_t)
                        num = num + w_c * _dot(qb, c0[dr])
                        den = den + w_c * _dot_nt(qb, n0[dr])[:, 0:1]
                    hsum = hsum + num / jnp.maximum(jnp.abs(den), jnp.exp(-(col + c_t)))
                hn = _rms(hsum, nrm[0][:, hs])
                og = zo[pl.ds(q0, TQ), hs].astype(F32)
                out[pl.ds(q0, TQ), hs] = (hn * jax.nn.sigmoid(og)).astype(out.dtype)
                return carry

            if nq == 1:
                qblock(0, 0)
            else:
                lax.fori_loop(0, nq, qblock, 0)

            if not has_init:
                k_t = _dot_nt(eye.astype(BF16), k)
                kf = k.astype(F32)
                tot = (bp_t[lane_ff + h:lane_ff + h + 1, seq - 1:seq], bp_t[lane_fb + h:lane_fb + h + 1, seq - 1:seq])
                gl = (
                    tot[0] + rows[0],
                    bp_t[lane_fb + h:lane_fb + h + 1, :] - a_t[lane_fb + h:lane_fb + h + 1, :]
                    + a_t[lane_ib + h:lane_ib + h + 1, :],
                )
                for dr in range(2):
                    m_new = jnp.maximum(tot[dr] + m0[dr], jnp.max(gl[dr], axis=1, keepdims=True))
                    w_s = jnp.exp(gl[dr] - m_new)
                    cf_ref[0, 0, dr, h] = _dot((k_t * w_s).astype(BF16), v)
                    n_new = jnp.dot(jnp.broadcast_to(w_s, (SUBLANES, seq)), kf, precision=HI, preferred_element_type=F32)
                    nf_ref[0, 0, dr, h:h + 1, :] = n_new[0:1, :]
                    mf_ref[0, 0, dr * ML_HEADS + h:dr * ML_HEADS + h + 1, :] = jnp.broadcast_to(m_new, (1, LANES))

    return kern


def _mlstm(layer, za, zs, b_gates, mlstm_norm, init=None, ctx_out=None, states=None):
    has_init = init is not None
    seq, nseq, row0 = (S_SMP, N_SEQ_SMP, N_CTX // S_SMP) if has_init else (S_CTX, N_SEQ_CTX, 0)
    qkvo = [pl.BlockSpec((seq, ML_WIDTH), functools.partial(lambda j, b: (row0 + b, j), j)) for j in range(4)]
    in_specs = qkvo + [
        pl.BlockSpec((seq, LANES), lambda b: (row0 + b, ZS_W // LANES - 1)),
        pl.BlockSpec((1, 1, LANES), lambda b: (layer, 0, 0)),
        pl.BlockSpec((1, 1, ML_WIDTH), lambda b: (layer, 0, 0)),
    ]
    args = [za, za, za, za, zs, b_gates, mlstm_norm]
    out_specs = [pl.BlockSpec((seq, ML_WIDTH), lambda b: (row0 + b, 0))]
    out_shape = [jax.ShapeDtypeStruct((N_TOK, ML_WIDTH), BF16)]
    aliases = {}
    if has_init:
        st_c, st_n, st_m = init
        in_specs = [pl.BlockSpec(memory_space=pltpu.SMEM)] + in_specs + [
            pl.BlockSpec((1, 1, 2, ML_HEADS, ML_DIM, ML_DIM), lambda b: (b, layer, 0, 0, 0, 0)),
            pl.BlockSpec((1, 1, 2, ML_HEADS, ML_DIM), lambda b: (b, layer, 0, 0, 0)),
            pl.BlockSpec(memory_space=pl.ANY),
        ]
        args = [st_m.reshape(-1)] + args + [st_c, st_n, ctx_out]
        aliases = {len(args) - 1: 0}
    else:
        seq_blk = lambda b: jnp.minimum(b, nseq - 1)
        in_specs += [pl.BlockSpec(memory_space=pl.ANY)] * 3
        args += list(states)
        aliases = {len(args) - 3 + j: 1 + j for j in range(3)}
        out_specs += [
            pl.BlockSpec((1, 1, 2, ML_HEADS, ML_DIM, ML_DIM), lambda b: (seq_blk(b), layer, 0, 0, 0, 0)),
            pl.BlockSpec((1, 1, 2, ML_HEADS, ML_DIM), lambda b: (seq_blk(b), layer, 0, 0, 0)),
            pl.BlockSpec((1, 1, 2 * ML_HEADS, LANES), lambda b: (seq_blk(b), layer, 0, 0)),
        ]
        out_shape += [jax.ShapeDtypeStruct(s.shape, s.dtype) for s in states]
    return pl.pallas_call(
        _make_mlstm_kernel(seq, layer, has_init),
        grid=(nseq if has_init else N_SEQ_BLOCKS,),
        in_specs=in_specs,
        out_specs=out_specs,
        out_shape=out_shape,
        scratch_shapes=[pltpu.VMEM((seq, LANES), F32), pltpu.VMEM((seq, LANES), F32)],
        input_output_aliases=aliases,
        compiler_params=_cparams(("arbitrary",)),
        name="mlstm_smp" if has_init else "mlstm_ctx",
    )(*args)


def _rope(x, cos, sin_a, sin_b):
    return x * cos + pltpu.roll(x, LANES - 8, 1) * sin_a + pltpu.roll(x, 8, 1) * sin_b


def _mla_prep_kernel(zs_ref, qa_ref, kva_ref, wuq_ref, wk_ref, wv_ref, qn_ref, kn_ref, cos_ref, sa_ref, sb_ref, _, __,
                     q_ref, kk_ref, v_ref, ckv_ref, kr_ref, qf_scr, kf_scr):
    cq = zs_ref[:, 0:MLA_Q_RANK]
    ckv = zs_ref[:, MLA_Q_RANK:MLA_Q_RANK + MLA_KV_RANK]
    last = zs_ref[:, ZS_W - LANES:ZS_W]
    qf_scr[...] = _dot(_rms(cq, qa_ref[0]).astype(BF16), wuq_ref[0])
    ckvn = _rms(ckv, kva_ref[0])

    @pl.when(pl.program_id(0) < N_TILES_CTX)
    def _():
        for j in range(TW // S_CTX):
            ckv_ref[j, 0] = ckvn[j * S_CTX:(j + 1) * S_CTX, :]
            kr_ref[j, 0] = last[j * S_CTX:(j + 1) * S_CTX, 0:MLA_ROPE]

    cb = ckvn.astype(BF16)
    kf_scr[...] = _dot(cb, wk_ref[0])
    v_ref[...] = _dot(cb, wv_ref[0]).astype(BF16)
    lane = lax.broadcasted_iota(jnp.int32, last.shape, 1)
    kr = jnp.where((lane >= MLA_NOPE) & (lane < MLA_QK), pltpu.roll(last, MLA_NOPE, 1), 0.0)
    is_latent = pl.program_id(0) >= N_TILES_CTX

    def heads(rotate):
        for h in range(MLA_HEADS):
            hs = slice(h * HEAD_PAD, (h + 1) * HEAD_PAD)
            q_ref[:, hs] = rotate(_rms(qf_scr[:, hs], qn_ref[0], n=MLA_QK)).astype(BF16)
            kk_ref[:, hs] = rotate(_rms(kf_scr[:, hs] + kr, kn_ref[0], n=MLA_QK)).astype(BF16)

    @pl.when(is_latent)
    def _():
        cos, sa, sb = cos_ref[...], sa_ref[...], sb_ref[...]
        heads(lambda x: _rope(x, cos, sa, sb))

    @pl.when(jnp.logical_not(is_latent))
    def _():
        heads(lambda x: x)


def _mla_prep(layer, zs, q_a_norm, kv_a_norm, w_uq_r, w_k_r, w_v_r, q_norm_p, k_norm_p, rope_tabs, new_ckv, new_kr):
    seq_blk = lambda w: pl.BlockSpec((TW // S_CTX, 1, S_CTX, w), lambda i: (jnp.minimum(i, N_TILES_CTX - 1), layer, 0, 0))
    lw = lambda shape: pl.BlockSpec((1,) + shape, lambda i: (layer,) + (0,) * len(shape))
    tab = pl.BlockSpec((TW, LANES), lambda i: (jnp.where(i < N_TILES_CTX, 0, 1 + (i - N_TILES_CTX) % TILES_PER_SMP_SEQ), 0))
    tok = lambda w: pl.BlockSpec((TW, w), lambda i: (i, 0))
    return pl.pallas_call(
        _mla_prep_kernel,
        grid=(N_TILES,),
        in_specs=[
            tok(ZS_W), lw((1, MLA_Q_RANK)), lw((1, MLA_KV_RANK)),
            lw((MLA_Q_RANK, MLA_HEADS * HEAD_PAD)), lw((MLA_KV_RANK, MLA_HEADS * HEAD_PAD)),
            lw((MLA_KV_RANK, MLA_HEADS * MLA_V)), lw((1, HEAD_PAD)), lw((1, HEAD_PAD)), tab, tab, tab,
            pl.BlockSpec(memory_space=pl.ANY), pl.BlockSpec(memory_space=pl.ANY),
        ],
        out_specs=[tok(MLA_HEADS * HEAD_PAD), tok(MLA_HEADS * HEAD_PAD), tok(MLA_HEADS * MLA_V),
                   seq_blk(MLA_KV_RANK), seq_blk(MLA_ROPE)],
        input_output_aliases={11: 3, 12: 4},
        out_shape=[
            jax.ShapeDtypeStruct((N_TOK, MLA_HEADS * HEAD_PAD), BF16),
            jax.ShapeDtypeStruct((N_TOK, MLA_HEADS * HEAD_PAD), BF16),
            jax.ShapeDtypeStruct((N_TOK, MLA_HEADS * MLA_V), BF16),
            jax.ShapeDtypeStruct(new_ckv.shape, F32),
            jax.ShapeDtypeStruct(new_kr.shape, F32),
        ],
        scratch_shapes=[pltpu.VMEM((TW, MLA_HEADS * HEAD_PAD), F32), pltpu.VMEM((TW, MLA_HEADS * HEAD_PAD), F32)],
        compiler_params=_cparams(("arbitrary",)),
        name="mla_prep",
    )(zs, q_a_norm, kv_a_norm, w_uq_r, w_k_r, w_v_r, q_norm_p, k_norm_p, *rope_tabs, new_ckv, new_kr)


def _cache_kv_kernel(ckv_ref, kr_ref, wk_ref, wv_ref, kn_ref, kk_ref, v_ref):
    cb = ckv_ref[...].astype(BF16)
    kf = _dot(cb, wk_ref[0])
    v_ref[...] = _dot(cb, wv_ref[0]).astype(BF16)
    kr = kr_ref[...]
    for h in range(MLA_HEADS):
        hs = slice(h * HEAD_PAD, (h + 1) * HEAD_PAD)
        kk_ref[:, hs] = _rms(kf[:, hs] + kr, kn_ref[0], n=MLA_QK).astype(BF16)


def _cache_kv(cache_ckv, cache_kr_pad, w_k_r, w_v_r, k_norm_p):
    lw = lambda shape: pl.BlockSpec((1,) + shape, lambda b, l: (l,) + (0,) * len(shape))
    blk = lambda w: pl.BlockSpec((None, None, PAST_LEN, w), lambda b, l: (b, l, 0, 0))
    return pl.pallas_call(
        _cache_kv_kernel,
        grid=(N_SEQ_SMP, DEPTH),
        in_specs=[blk(MLA_KV_RANK), blk(LANES), lw((MLA_KV_RANK, MLA_HEADS * HEAD_PAD)),
                  lw((MLA_KV_RANK, MLA_HEADS * MLA_V)), lw((1, HEAD_PAD))],
        out_specs=[blk(MLA_HEADS * HEAD_PAD), blk(MLA_HEADS * MLA_V)],
        out_shape=[
            jax.ShapeDtypeStruct((N_SEQ_SMP, DEPTH, PAST_LEN, MLA_HEADS * HEAD_PAD), BF16),
            jax.ShapeDtypeStruct((N_SEQ_SMP, DEPTH, PAST_LEN, MLA_HEADS * MLA_V), BF16),
        ],
        compiler_params=_cparams(("arbitrary", "arbitrary")),
        name="cache_kv",
    )(cache_ckv, cache_kr_pad, w_k_r, w_v_r, k_norm_p)


def _make_attn_kernel(n_src):
    scale = MLA_QK ** -0.5

    def kern(q_ref, *refs):
        o_ref = refs[-1]
        if n_src > 1:
            body(q_ref, *refs)
            return

        @pl.when(pl.program_id(0) < N_SEQ_CTX)
        def _():
            body(q_ref, *refs)

        @pl.when(pl.program_id(0) >= N_SEQ_CTX)
        def _():
            o_ref[...] = jnp.zeros_like(o_ref)

    def body(q_ref, *refs):
        o_ref = refs[-1]
        for h in range(MLA_HEADS):
            hs = slice(h * HEAD_PAD, (h + 1) * HEAD_PAD)
            vs = slice(h * MLA_V, (h + 1) * MLA_V)
            q = q_ref[:, hs]
            ss = [_dot_nt(q, refs[2 * j][:, hs]) * scale for j in range(n_src)]
            m = functools.reduce(jnp.maximum, [jnp.max(s, axis=1, keepdims=True) for s in ss])
            ps = [jnp.exp(s - m) for s in ss]
            l = functools.reduce(jnp.add, [jnp.sum(p, axis=1, keepdims=True) for p in ps])
            o = functools.reduce(jnp.add, [_dot(ps[j].astype(BF16), refs[2 * j + 1][:, vs]) for j in range(n_src)])
            o_ref[:, vs] = (o / l).astype(o_ref.dtype)

    return kern


def _attn_ctx(q, kk, v):
    blk = lambda w: pl.BlockSpec((S_CTX, w), lambda b: (b, 0))
    return pl.pallas_call(
        _make_attn_kernel(1),
        grid=(N_SEQ_BLOCKS,),
        in_specs=[blk(MLA_HEADS * HEAD_PAD), blk(MLA_HEADS * HEAD_PAD), blk(MLA_HEADS * MLA_V)],
        out_specs=blk(MLA_HEADS * MLA_V),
        out_shape=jax.ShapeDtypeStruct((N_TOK, MLA_HEADS * MLA_V), BF16),
        compiler_params=_cparams(("arbitrary",)),
        name="attn_ctx",
    )(q, kk, v)


def _attn_smp(layer, q, kk, v, kk_cache, v_cache, ctx_out):
    row0 = N_CTX // S_SMP
    nq = S_SMP // TQ
    seqb = lambda w: pl.BlockSpec((S_SMP, w), lambda b, i: (row0 + b, 0))
    cache = lambda w: pl.BlockSpec((None, None, PAST_LEN, w), lambda b, i: (b, layer, 0, 0))
    return pl.pallas_call(
        _make_attn_kernel(2),
        grid=(N_SEQ_SMP, nq),
        in_specs=[
            pl.BlockSpec((TQ, MLA_HEADS * HEAD_PAD), lambda b, i: (N_CTX // TQ + b * nq + i, 0)),
            seqb(MLA_HEADS * HEAD_PAD), seqb(MLA_HEADS * MLA_V),
            cache(MLA_HEADS * HEAD_PAD), cache(MLA_HEADS * MLA_V),
            pl.BlockSpec(memory_space=pl.ANY),
        ],
        out_specs=pl.BlockSpec((TQ, MLA_HEADS * MLA_V), lambda b, i: (N_CTX // TQ + b * nq + i, 0)),
        out_shape=jax.ShapeDtypeStruct((N_TOK, MLA_HEADS * MLA_V), BF16),
        input_output_aliases={5: 0},
        compiler_params=_cparams(("arbitrary", "arbitrary")),
        name="attn_smp",
    )(q, kk, v, kk_cache, v_cache, ctx_out)


def _merge_kernel(x_ref, oa_ref, ob_ref, zc_ref, zb_ref, sgn_ref, ws_ref, bs_ref, wb_ref, wo_ref, g1_ref, n2_ref,
                  sh2_ref, sc2_ref, wr_ref, br_ref,
                  xmid_ref, h2_ref, dest_ref, wsel_ref, cnt_ref, oc_scr, carry_scr):
    i = pl.program_id(0)

    @pl.when(i == 0)
    def _():
        carry_scr[...] = jnp.zeros_like(carry_scr)

    u = _gelu(zc_ref[:, 0:SG_WIDTH].astype(F32))
    vg = _gelu(zc_ref[:, SG_WIDTH:2 * SG_WIDTH].astype(F32))
    for g in range(SG_GROUPS):
        gs = slice(g * SG_DIM, (g + 1) * SG_DIM)
        vn = _rms(vg[:, gs], sgn_ref[0][:, gs]).astype(BF16)
        for c in range(TW // SG_CHUNK):
            cs = slice(c * SG_CHUNK, (c + 1) * SG_CHUNK)
            mixed = _dot(ws_ref[0, g], vn[cs, :]) + bs_ref[0][:, g:g + 1]
            oc_scr[cs, gs] = (u[cs, gs] * mixed).astype(BF16)

    acc = jnp.zeros((TW, D_MODEL), F32)
    for j, src in enumerate((oa_ref, ob_ref, oc_scr)):
        gate = jax.nn.sigmoid(zb_ref[:, j * D_MODEL:(j + 1) * D_MODEL].astype(F32))
        acc = acc + gate * _dot(src[...], wb_ref[0, j])
    xm = x_ref[...] + g1_ref[0, 0] * _dot(acc.astype(BF16), wo_ref[0])
    xmid_ref[...] = xm
    h2 = _rms(xm, n2_ref[0]) * (1.0 + sc2_ref[0, 0]) + sh2_ref[0, 0]
    for c in range(SC_SPLIT):
        h2_ref[c] = _pack_pairs(h2[:, 2 * c * SC_ROW:(2 * c + 1) * SC_ROW], h2[:, (2 * c + 1) * SC_ROW:(2 * c + 2) * SC_ROW])

    h_hi = h2.astype(BF16)
    h_lo = (h2 - h_hi.astype(F32)).astype(BF16)
    p_hi = _dot(h_hi, wr_ref[0])
    logits = p_hi[:, 0:LANES] + p_hi[:, LANES:2 * LANES] + _dot(h_lo, wr_ref[0, :, 0:LANES]) + br_ref[0]
    lane = lax.broadcasted_iota(jnp.int32, logits.shape, 1)
    hits, exps = [], []
    sel = jnp.zeros(logits.shape, F32)
    denom = jnp.zeros((TW, 1), F32)
    top = None
    for _ in range(TOP_K):
        m = jnp.max(logits, axis=1, keepdims=True)
        idx = jnp.min(jnp.where(logits == m, lane, LANES), axis=1, keepdims=True)
        hit = lane == idx
        top = m if top is None else top
        hits.append(hit)
        exps.append(jnp.exp(m - top))
        sel = jnp.where(hit, 1.0, sel)
        denom = denom + exps[-1]
        logits = jnp.where(hit, -jnp.inf, logits)

    r_i = lax.broadcasted_iota(jnp.int32, (TW, TW), 0)
    c_i = lax.broadcasted_iota(jnp.int32, (TW, TW), 1)
    carry = carry_scr[0:1, :]
    rank = _dot(jnp.where(c_i < r_i, 1.0, 0.0).astype(BF16), sel.astype(BF16)) + carry
    new_carry = carry + jnp.sum(sel, axis=0, keepdims=True)
    carry_scr[...] = jnp.broadcast_to(new_carry, (SUBLANES, LANES))
    cnt_ref[...] = jnp.broadcast_to(new_carry, (SUBLANES, LANES))
    slot = rank + lane.astype(F32) * float(EXPERT_CAP)
    dmat = jnp.zeros(logits.shape, F32)
    wmat = jnp.zeros(logits.shape, F32)
    for k in range(TOP_K):
        dk = jnp.sum(jnp.where(hits[k], slot, 0.0), axis=1, keepdims=True)
        dmat = jnp.where(lane == k, dk, dmat)
        wmat = jnp.where(lane == k, exps[k] / denom, wmat)
    dest_ref[...] = dmat.T[0:SUBLANES, :].astype(jnp.int32)
    wsel_ref[...] = wmat


def _merge(layer, x, oa, ob, zc, zb, sg_norm, w_sp, b_sp, w_branch, w_out, mod, norm2, w_router_p, b_router_p):
    lw = lambda shape: pl.BlockSpec((1,) + shape, lambda i: (layer,) + (0,) * len(shape))
    tok = lambda w: pl.BlockSpec((TW, w), lambda i: (i, 0))
    return pl.pallas_call(
        _merge_kernel,
        grid=(N_TILES,),
        in_specs=[
            tok(D_MODEL), tok(ML_WIDTH), tok(MLA_HEADS * MLA_V), tok(ZC_W), tok(ZB_W),
            lw((1, SG_WIDTH)), lw((SG_GROUPS, SG_CHUNK, SG_CHUNK)), lw((SG_CHUNK, LANES)),
            lw((N_BRANCH, ML_WIDTH, D_MODEL)), lw((D_MODEL, D_MODEL)),
            _mod_spec(layer, 2), lw((1, D_MODEL)), _mod_spec(layer, 3), _mod_spec(layer, 4),
            lw((D_MODEL, 2 * LANES)), lw((1, LANES)),
        ],
        out_specs=[tok(D_MODEL), pl.BlockSpec((SC_SPLIT, TW, SC_ROW), lambda i: (0, i, 0)),
                   pl.BlockSpec((SUBLANES, TW), lambda i: (0, i)), tok(LANES),
                   pl.BlockSpec((SUBLANES, LANES), lambda i: (0, 0))],
        out_shape=[
            jax.ShapeDtypeStruct((N_TOK, D_MODEL), F32),
            jax.ShapeDtypeStruct((SC_SPLIT, N_TOK, SC_ROW), jnp.uint32),
            jax.ShapeDtypeStruct((SUBLANES, N_TOK), jnp.int32),
            jax.ShapeDtypeStruct((N_TOK, LANES), F32),
            jax.ShapeDtypeStruct((SUBLANES, LANES), F32),
        ],
        scratch_shapes=[pltpu.VMEM((TW, SG_WIDTH), BF16), pltpu.VMEM((SUBLANES, LANES), F32)],
        compiler_params=_cparams(("arbitrary",)),
        name="merge_router",
    )(x, oa, ob, zc, zb, sg_norm, w_sp, b_sp, w_branch, w_out, mod, norm2, mod, mod, w_router_p, b_router_p)


def _sc_mesh():
    return plsc.VectorSubcoreMesh(core_axis_name="core", subcore_axis_name="subcore")


def _sc_scatter_rows(x, idxs, n_rows):
    @pl.kernel(out_type=jax.ShapeDtypeStruct((n_rows, SC_ROW), x.dtype), mesh=_sc_mesh(), scratch_types=[])
    def scatter(x_hbm, *refs):
        o_hbm = refs[-1]

        def body(x_vmem, *i_vmems):
            for i_vmem in i_vmems:
                pltpu.sync_copy(x_vmem, o_hbm.at[i_vmem.at[0]])

        pltpu.emit_pipeline(
            body,
            grid=(x.shape[0] // SC_WIN,),
            in_specs=[pl.BlockSpec((SC_WIN, SC_ROW), lambda i: (i, 0))]
            + [pl.BlockSpec((1, SC_WIN), lambda i: (0, i))] * len(idxs),
            out_specs=[],
            core_axis_name=("core", "subcore"),
            dimension_semantics=(pltpu.PARALLEL,),
        )(x_hbm, *refs[:-1])

    return scatter(x, *idxs)


def _sc_gather_rows(x, idx):
    m = idx.shape[1]

    @pl.kernel(out_type=jax.ShapeDtypeStruct((m, SC_ROW), x.dtype), mesh=_sc_mesh())
    def gather(x_hbm, i_hbm, o_hbm):
        def body(i_vmem, o_vmem):
            pltpu.sync_copy(x_hbm.at[i_vmem.at[0]], o_vmem)

        pltpu.emit_pipeline(
            body,
            grid=(m // SC_WIN,),
            in_specs=[pl.BlockSpec((1, SC_WIN), lambda i: (0, i))],
            out_specs=[pl.BlockSpec((SC_WIN, SC_ROW), lambda i: (i, 0))],
            core_axis_name=("core", "subcore"),
            dimension_semantics=(pltpu.PARALLEL,),
        )(i_hbm, o_hbm)

    return gather(x, idx)


STEP_VALID, STEP_FIRST, STEP_HAS_NEXT, STEP_FULL = 1, 2, 4, 8


def _moe_ffn_kernel(layer, be_ref, nx_ref, br_ref, fl_ref, xs_ref, b1_ref, b2_ref, w1_hbm, w2_hbm, y_ref,
                    w1f, w2f, w1b, w2b, sem):
    g = pl.program_id(0)
    flags = fl_ref[g]

    def weight_copies(e):
        return (pltpu.make_async_copy(w1_hbm.at[layer, e], w1f, sem.at[0]),
                pltpu.make_async_copy(w2_hbm.at[layer, e], w2f, sem.at[1]))

    @pl.when(g == 0)
    def _():
        for cp in weight_copies(be_ref[0]):
            cp.start()

    @pl.when((flags & STEP_FIRST) != 0)
    def _():
        for cp in weight_copies(be_ref[g]):
            cp.wait()
        w1b[...] = w1f[...].astype(BF16)
        w2b[...] = w2f[...].astype(BF16)

        @pl.when((flags & STEP_HAS_NEXT) != 0)
        def _():
            for cp in weight_copies(nx_ref[g]):
                cp.start()

    def ffn(n_rows):
        halves = [h.astype(BF16) for c in range(SC_SPLIT) for h in _unpack_pairs(xs_ref[c, 0:n_rows, :])]
        g1 = _dot(jnp.concatenate(halves, axis=1), w1b[...]) + b1_ref[0, 0]
        gate = jnp.minimum(g1[:, :D_EXPERT], SWIGLU_LIMIT)
        up = jnp.clip(g1[:, D_EXPERT:], -SWIGLU_LIMIT, SWIGLU_LIMIT)
        act = gate * jax.nn.sigmoid(SWIGLU_ALPHA * gate) * (up + 1.0)
        y = _dot(act.astype(BF16), w2b[...]) + b2_ref[0, 0]
        for c in range(SC_SPLIT):
            y_ref[c, 0:n_rows, :] = _pack_pairs(
                y[:, 2 * c * SC_ROW:(2 * c + 1) * SC_ROW], y[:, (2 * c + 1) * SC_ROW:(2 * c + 2) * SC_ROW])

    pl.when((flags & (STEP_VALID | STEP_FULL)) == STEP_VALID)(lambda: ffn(SLOT_CHUNK))
    pl.when((flags & STEP_FULL) != 0)(lambda: ffn(FFN_BLOCK))


def _moe_ffn(layer, xs, plan, w1, b1, w2, b2):
    eb = lambda c: pl.BlockSpec((1, 1, 1, c), lambda g, be, nx, br, fl: (layer, be[g], 0, 0))
    rows = pl.BlockSpec((SC_SPLIT, FFN_BLOCK, SC_ROW), lambda g, be, nx, br, fl: (0, br[g], 0))
    hbm = pl.BlockSpec(memory_space=pl.ANY)
    grid_spec = pltpu.PrefetchScalarGridSpec(
        num_scalar_prefetch=4,
        grid=(N_CHUNK_STEPS,),
        in_specs=[rows, eb(2 * D_EXPERT), eb(D_MODEL), hbm, hbm],
        out_specs=rows,
        scratch_shapes=[
            pltpu.VMEM((D_MODEL, 2 * D_EXPERT), F32), pltpu.VMEM((D_EXPERT, D_MODEL), F32),
            pltpu.VMEM((D_MODEL, 2 * D_EXPERT), BF16), pltpu.VMEM((D_EXPERT, D_MODEL), BF16),
            pltpu.SemaphoreType.DMA((2,)),
        ],
    )
    return pl.pallas_call(
        functools.partial(_moe_ffn_kernel, layer),
        grid_spec=grid_spec,
        out_shape=jax.ShapeDtypeStruct(xs.shape, xs.dtype),
        compiler_params=_cparams(("arbitrary",)),
        name="moe_ffn",
    )(*plan, xs, b1, b2, w1, w2)


def _chunk_plan_kernel(cnt_ref, be_ref, nx_ref, br_ref, fl_ref):
    def expert(e, carry):
        step0, prev_first = carry
        c = cnt_ref[e]
        n_blk = (c + FFN_BLOCK - 1) // FFN_BLOCK

        def block(j, _):
            s = step0 + j
            be_ref[s] = e
            nx_ref[s] = e
            br_ref[s] = e * (EXPERT_CAP // FFN_BLOCK) + j
            fl_ref[s] = (STEP_VALID + jnp.where(j == 0, STEP_FIRST, 0)
                         + jnp.where(c - j * FFN_BLOCK > SLOT_CHUNK, STEP_FULL, 0))
            return 0

        lax.fori_loop(0, n_blk, block, 0)

        @pl.when((n_blk > 0) & (prev_first >= 0))
        def _():
            nx_ref[prev_first] = e
            fl_ref[prev_first] = fl_ref[prev_first] + STEP_HAS_NEXT

        return step0 + n_blk, jnp.where(n_blk > 0, step0, prev_first)

    used, _ = lax.fori_loop(0, N_EXPERTS, expert, (jnp.int32(0), jnp.int32(-1)))

    def idle(s, _):
        be_ref[s] = be_ref[used - 1]
        nx_ref[s] = be_ref[used - 1]
        br_ref[s] = br_ref[used - 1]
        fl_ref[s] = 0
        return 0

    lax.fori_loop(used, N_CHUNK_STEPS, idle, 0)


def _chunk_plan(cnt):
    smem = pl.BlockSpec(memory_space=pltpu.SMEM)
    return pl.pallas_call(
        _chunk_plan_kernel,
        in_specs=[smem],
        out_specs=[smem] * 4,
        out_shape=[jax.ShapeDtypeStruct((N_CHUNK_STEPS,), jnp.int32)] * 4,
        name="chunk_plan",
    )(cnt)


def _combine_kernel(x_ref, yg_ref, w_ref, g_ref, *o_refs):
    def emit(o_ref):
        w = w_ref[...]
        for c in range(SC_SPLIT):
            parts = [_unpack_pairs(yg_ref[k, c]) for k in range(TOP_K)]
            for half in range(2):
                cs = slice((2 * c + half) * SC_ROW, (2 * c + half + 1) * SC_ROW)
                acc = w[:, 0:1] * parts[0][half]
                for k in range(1, TOP_K):
                    acc = acc + w[:, k:k + 1] * parts[k][half]
                o_ref[:, cs] = x_ref[:, cs] + g_ref[0, 0][:, cs] * acc

    if len(o_refs) == 1:
        emit(o_refs[0])
    else:
        pl.when(pl.program_id(0) < N_TILES_CTX)(lambda: emit(o_refs[0]))
        pl.when(pl.program_id(0) >= N_TILES_CTX)(lambda: emit(o_refs[1]))


def _combine(layer, xmid, yg, wsel, mod, split_out):
    tok = lambda w: pl.BlockSpec((TW, w), lambda i: (i, 0))
    if split_out:
        out_specs = [pl.BlockSpec((TW, D_MODEL), lambda i: (jnp.minimum(i, N_TILES_CTX - 1), 0)),
                     pl.BlockSpec((TW, D_MODEL), lambda i: (jnp.maximum(i - N_TILES_CTX, 0), 0))]
        out_shape = [jax.ShapeDtypeStruct((N_CTX, D_MODEL), F32), jax.ShapeDtypeStruct((N_SMP, D_MODEL), F32)]
    else:
        out_specs, out_shape = tok(D_MODEL), jax.ShapeDtypeStruct((N_TOK, D_MODEL), F32)
    return pl.pallas_call(
        _combine_kernel,
        grid=(N_TILES,),
        in_specs=[tok(D_MODEL), pl.BlockSpec((TOP_K, SC_SPLIT, TW, SC_ROW), lambda i: (0, 0, i, 0)), tok(LANES),
                  _mod_spec(layer, 5)],
        out_specs=out_specs,
        out_shape=out_shape,
        compiler_params=_cparams(("arbitrary",)),
        name="combine",
    )(xmid, yg, wsel, mod)


def _moe(layer, xmid, h2, dest, wsel, cnt, mod, w1, b1, w2, b2, split_out):
    n_slots = N_EXPERTS * EXPERT_CAP
    idx = dest[0:TOP_K][:, None, :] + (jnp.arange(SC_SPLIT, dtype=jnp.int32) * n_slots)[None, :, None]
    idx = idx.reshape(TOP_K, 1, SC_SPLIT * N_TOK)
    xs = _sc_scatter_rows(h2.reshape(SC_SPLIT * N_TOK, SC_ROW), [idx[k] for k in range(TOP_K)], SC_SPLIT * n_slots)
    plan = _chunk_plan(cnt[0, :N_EXPERTS].astype(jnp.int32))
    y = _moe_ffn(layer, xs.reshape(SC_SPLIT, n_slots, SC_ROW), plan, w1, b1, w2, b2)
    yg = _sc_gather_rows(y.reshape(SC_SPLIT * n_slots, SC_ROW), idx.reshape(1, TOP_K * SC_SPLIT * N_TOK))
    return _combine(layer, xmid, yg.reshape(TOP_K, SC_SPLIT, N_TOK, SC_ROW), wsel, mod, split_out)


def _rope_tables():
    pos = np.arange(S_SMP)
    half = MLA_ROPE // 2
    inv_freq = (ROPE_THETA ** (-(np.arange(0, half, 2, dtype=np.float32) / np.float32(half)))).astype(np.float32)
    angs = [((pos // GRID_W).astype(np.float32)[:, None] * inv_freq[None, :]).astype(np.float32),
            ((pos % GRID_W).astype(np.float32)[:, None] * inv_freq[None, :]).astype(np.float32)]
    nf = half // 2
    cos = np.ones((TW + S_SMP, LANES), np.float32)
    sin_a = np.zeros((TW + S_SMP, LANES), np.float32)
    sin_b = np.zeros((TW + S_SMP, LANES), np.float32)
    for axis, ang in enumerate(angs):
        base = MLA_NOPE + axis * half
        c, s = np.cos(ang.astype(np.float64)), np.sin(ang.astype(np.float64))
        cos[TW:, base:base + nf] = c
        cos[TW:, base + nf:base + half] = c
        sin_a[TW:, base:base + nf] = -s
        sin_b[TW:, base + nf:base + half] = s
    return jnp.asarray(cos), jnp.asarray(sin_a), jnp.asarray(sin_b)


def _pad_last(a, width):
    return jnp.pad(a, [(0, 0)] * (a.ndim - 1) + [(0, width - a.shape[-1])])


def kernel(x_prompt, x_sample, cache_mla_ckv, cache_mla_krope, state_mlstm_C, state_mlstm_n, state_mlstm_m, c, c_ctx, norm1, norm2, w_ada, b_ada, w_in, b_mlstm_gates, mlstm_norm, mla_q_a_norm, mla_kv_a_norm, w_uq, w_ukv, mla_q_norm, mla_k_norm, sg_norm, w_spatial, b_spatial, w_branch, w_out, w_router, b_router, w_exp1, b_exp1, w_exp2, b_exp2):
    w_in_r = _w_in_prep(w_in)
    w_uq_r = _pad_last(w_uq.reshape(DEPTH, MLA_Q_RANK, MLA_HEADS, MLA_QK), HEAD_PAD).reshape(
        DEPTH, MLA_Q_RANK, MLA_HEADS * HEAD_PAD).astype(BF16)
    w_ukv4 = w_ukv.reshape(DEPTH, MLA_KV_RANK, MLA_HEADS, MLA_NOPE + MLA_V)
    w_k_r = _pad_last(w_ukv4[..., :MLA_NOPE], HEAD_PAD).reshape(DEPTH, MLA_KV_RANK, MLA_HEADS * HEAD_PAD).astype(BF16)
    w_v_r = w_ukv4[..., MLA_NOPE:].reshape(DEPTH, MLA_KV_RANK, MLA_HEADS * MLA_V).astype(BF16)
    q_norm_p = _pad_last(mla_q_norm, HEAD_PAD).reshape(DEPTH, 1, HEAD_PAD)
    k_norm_p = _pad_last(mla_k_norm, HEAD_PAD).reshape(DEPTH, 1, HEAD_PAD)
    b_gates_p = jnp.pad(b_mlstm_gates, ((0, 0), (GATE_LANE0, LANES - GATE_LANE0 - 4 * ML_HEADS))).reshape(DEPTH, 1, LANES)
    b_sp = _pad_last(jnp.swapaxes(b_spatial, 1, 2), LANES)
    w_router_p = _pad_last(w_router, LANES)
    w_router_hi = w_router_p.astype(BF16)
    w_router_p = jnp.concatenate([w_router_hi, (w_router_p - w_router_hi.astype(F32)).astype(BF16)], axis=-1)
    b_router_p = jnp.pad(b_router, ((0, 0), (0, LANES - N_EXPERTS)), constant_values=-1e30).reshape(DEPTH, 1, LANES)
    r3 = lambda a: a.reshape(DEPTH, 1, a.shape[-1])
    cache_kr_pad = jnp.pad(cache_mla_krope, ((0, 0), (0, 0), (0, 0), (MLA_NOPE, LANES - MLA_QK)))
    rope_tabs = _rope_tables()

    cvec = jnp.concatenate([c_ctx[None, :], c, jnp.zeros((SUBLANES - 1 - N_SEQ_SMP, D_MODEL), F32)], axis=0)
    mod = _adaln(cvec, w_ada, b_ada).reshape(DEPTH, SUBLANES, 1, 6 * D_MODEL)
    b1 = b_exp1.reshape(DEPTH, N_EXPERTS, 1, 2 * D_EXPERT)
    b2 = b_exp2.reshape(DEPTH, N_EXPERTS, 1, D_MODEL)
    kk_cache, v_cache = _cache_kv(cache_mla_ckv, cache_kr_pad, w_k_r, w_v_r, k_norm_p)

    x = (x_prompt.reshape(N_CTX, D_MODEL), x_sample.reshape(N_SMP, D_MODEL))
    new_ckv = jnp.zeros((N_SEQ_CTX, DEPTH, S_CTX, MLA_KV_RANK), F32)
    new_kr = jnp.zeros((N_SEQ_CTX, DEPTH, S_CTX, MLA_ROPE), F32)
    states = (jnp.zeros((N_SEQ_CTX, DEPTH, 2, ML_HEADS, ML_DIM, ML_DIM), F32),
              jnp.zeros((N_SEQ_CTX, DEPTH, 2, ML_HEADS, ML_DIM), F32),
              jnp.zeros((N_SEQ_CTX, DEPTH, 2 * ML_HEADS, LANES), F32))
    for l in range(DEPTH):
        za, zs, zc, zb, *x_joined = _inproj(l, x, r3(norm1), mod, w_in_r)
        x = x_joined[0] if x_joined else x
        oa, *states = _mlstm(l, za, zs, b_gates_p, r3(mlstm_norm), states=states)
        (oa,) = _mlstm(l, za, zs, b_gates_p, r3(mlstm_norm), init=(state_mlstm_C, state_mlstm_n, state_mlstm_m), ctx_out=oa)
        q, kk, v, new_ckv, new_kr = _mla_prep(l, zs, r3(mla_q_a_norm), r3(mla_kv_a_norm), w_uq_r, w_k_r, w_v_r,
                                              q_norm_p, k_norm_p, rope_tabs, new_ckv, new_kr)
        ob = _attn_smp(l, q, kk, v, kk_cache, v_cache, _attn_ctx(q, kk, v))
        xmid, h2, dest, wsel, cnt = _merge(
            l, x, oa, ob, zc, zb, r3(sg_norm), w_spatial.astype(BF16), b_sp, w_branch.astype(BF16), w_out.astype(BF16),
            mod, r3(norm2), w_router_p, b_router_p)
        x = _moe(l, xmid, h2, dest, wsel, cnt, mod, w_exp1, b1, w_exp2, b2, split_out=l == DEPTH - 1)
    y_ctx, y_smp = x
    return (
        y_ctx.reshape(N_SEQ_CTX, S_CTX, D_MODEL),
        y_smp.reshape(N_SEQ_SMP, S_SMP, D_MODEL),
        new_ckv,
        new_kr,
        states[0],
        states[1],
        states[2][:, :, :, 0].reshape(N_SEQ_CTX, DEPTH, 2, ML_HEADS),
    )
```

```python
import functools

import numpy as np
import jax
import jax.numpy as jnp
from jax import lax
from jax.experimental import pallas as pl
from jax.experimental.pallas import tpu as pltpu
from jax.experimental.pallas import tpu_sc as plsc

F32 = jnp.float32
BF16 = jnp.bfloat16
HI = lax.Precision.HIGHEST

D_MODEL = 1024
N_SEQ_CTX, S_CTX = 32, 256
N_SEQ_SMP, S_SMP = 2, 1024
DEPTH = 4
PAST_LEN = 512
GRID_W = 64
EPS = 1e-6
ML_HEADS, ML_DIM = 4, 128
ML_WIDTH = ML_HEADS * ML_DIM
MLA_HEADS, MLA_NOPE, MLA_ROPE, MLA_V = 8, 64, 32, 64
MLA_QK = MLA_NOPE + MLA_ROPE
MLA_Q_RANK, MLA_KV_RANK = 256, 128
ROPE_THETA = 10000.0
SG_GROUPS, SG_DIM, SG_CHUNK = 4, 128, 128
SG_WIDTH = SG_GROUPS * SG_DIM
N_BRANCH = 3
N_EXPERTS, TOP_K, D_EXPERT = 32, 4, 1024
SWIGLU_LIMIT, SWIGLU_ALPHA = 7.0, 1.702

N_CTX = N_SEQ_CTX * S_CTX
N_SMP = N_SEQ_SMP * S_SMP
N_TOK = N_CTX + N_SMP

LANES = 128
SUBLANES = 8
VMEM_LIMIT = 56 * 1024 * 1024

TW = 1024
TW_IN = 512
N_TILES = N_TOK // TW
N_TILES_CTX = N_CTX // TW
TILES_PER_SMP_SEQ = S_SMP // TW
N_SEQ_BLOCKS = N_TOK // S_CTX
HEAD_PAD = LANES
TQ = 256
EXPERT_CAP = N_TOK
SLOT_CHUNK = 256
FFN_BLOCK = 2 * SLOT_CHUNK
N_CHUNK_STEPS = N_TOK * TOP_K // FFN_BLOCK + N_EXPERTS
SC_ROW = 256
SC_SPLIT = D_MODEL // (2 * SC_ROW)
SC_WIN = 128

ZA_W = 4 * ML_WIDTH
ZS_W = 512
ZC_W = 2 * SG_WIDTH
ZB_W = N_BRANCH * D_MODEL
ZIN_W = ZA_W + ZS_W + ZC_W + ZB_W
GATE_LANE0 = MLA_ROPE


def _cparams(sem):
    return pltpu.CompilerParams(dimension_semantics=sem, vmem_limit_bytes=VMEM_LIMIT)


def _mod_row(i, tile=None):
    tile = tile or TW
    return jnp.where(i < N_CTX // tile, 0, 1 + (i - N_CTX // tile) // (S_SMP // tile))


def _rms(x, g, n=None):
    ms = jnp.sum(x * x, axis=-1, keepdims=True) * (1.0 / (n or x.shape[-1]))
    return x * lax.rsqrt(ms + EPS) * g


def _gelu(x):
    return 0.5 * x * (1.0 + jnp.tanh(0.7978845608028654 * (x + 0.044715 * (x * x * x))))


def _pack_pairs(lo, hi):
    lo_bits = lax.bitcast_convert_type(lo.astype(BF16).astype(F32), jnp.uint32)
    hi_bits = lax.bitcast_convert_type(hi.astype(BF16).astype(F32), jnp.uint32)
    return (lo_bits >> 16) | (hi_bits & jnp.uint32(0xFFFF0000))


def _unpack_pairs(u):
    return (lax.bitcast_convert_type(u << 16, F32), lax.bitcast_convert_type(u & jnp.uint32(0xFFFF0000), F32))


def _dot(a, b):
    return jnp.dot(a, b, preferred_element_type=F32)


def _dot_nt(a, b):
    return lax.dot_general(a, b, (((1,), (1,)), ((), ())), preferred_element_type=F32)


def _adaln_kernel(c_ref, w_ref, b_ref, o_ref):
    c = c_ref[...]
    s = c * jax.nn.sigmoid(c)
    w = w_ref[0]
    s_hi, w_hi = s.astype(BF16), w.astype(BF16)
    s_lo, w_lo = (s - s_hi.astype(F32)).astype(BF16), (w - w_hi.astype(F32)).astype(BF16)
    o_ref[0] = _dot(s_hi, w_hi) + _dot(s_hi, w_lo) + _dot(s_lo, w_hi) + b_ref[0]


def _adaln(cvec, w_ada, b_ada):
    nchunk = 4
    cw = 6 * D_MODEL // nchunk
    return pl.pallas_call(
        _adaln_kernel,
        grid=(DEPTH, nchunk),
        in_specs=[
            pl.BlockSpec((SUBLANES, D_MODEL), lambda l, j: (0, 0)),
            pl.BlockSpec((1, D_MODEL, cw), lambda l, j: (l, 0, j)),
            pl.BlockSpec((1, 1, cw), lambda l, j: (l, 0, j)),
        ],
        out_specs=pl.BlockSpec((1, SUBLANES, cw), lambda l, j: (l, 0, j)),
        out_shape=jax.ShapeDtypeStruct((DEPTH, SUBLANES, 6 * D_MODEL), F32),
        compiler_params=_cparams(("arbitrary", "arbitrary")),
        name="adaln",
    )(cvec, w_ada, b_ada.reshape(DEPTH, 1, 6 * D_MODEL))


IN_SPLITS = (ML_WIDTH, ML_WIDTH, ML_WIDTH, ML_WIDTH, 4 * ML_HEADS, MLA_Q_RANK, MLA_KV_RANK, MLA_ROPE, SG_WIDTH, SG_WIDTH,
             N_BRANCH * D_MODEL)
IN_OFFS = tuple(int(v) for v in np.cumsum((0,) + IN_SPLITS))
D_IN = IN_OFFS[-1]
W_PREP_ROWS = 256
W_PREP_COLS = 512


def _w_in_prep_kernel(wt_ref, o_ref):
    o = IN_OFFS

    def put(c0, rows):
        o_ref[0, :, c0:c0 + W_PREP_COLS] = rows.T.astype(BF16)

    for c0 in range(0, ZA_W, W_PREP_COLS):
        scale = ML_DIM ** -0.5 if o[1] <= c0 < o[2] else 1.0
        put(c0, wt_ref[0, c0:c0 + W_PREP_COLS, :] * scale)
    pad = jnp.zeros((ZS_W - (o[8] - o[4]), W_PREP_ROWS), F32)
    put(ZA_W, jnp.concatenate([wt_ref[0, o[5]:o[8], :], wt_ref[0, o[4]:o[5], :], pad], axis=0))
    for c0 in range(ZA_W + ZS_W, ZIN_W, W_PREP_COLS):
        src = c0 - (ZA_W + ZS_W) + o[8]
        put(c0, wt_ref[0, src:src + W_PREP_COLS, :])


def _w_in_prep(w_in):
    return pl.pallas_call(
        _w_in_prep_kernel,
        grid=(DEPTH, D_MODEL // W_PREP_ROWS),
        in_specs=[pl.BlockSpec((1, D_IN, W_PREP_ROWS), lambda l, r: (l, 0, r))],
        out_specs=pl.BlockSpec((1, W_PREP_ROWS, ZIN_W), lambda l, r: (l, r, 0)),
        out_shape=jax.ShapeDtypeStruct((DEPTH, D_MODEL, ZIN_W), BF16),
        compiler_params=_cparams(("arbitrary", "arbitrary")),
        name="w_in_prep",
    )(jnp.swapaxes(w_in, 1, 2))


def _tok_specs(x, tile):
    if isinstance(x, tuple):
        n_ctx = N_CTX // tile
        return [pl.BlockSpec((tile, D_MODEL), lambda i: (jnp.minimum(i, n_ctx - 1), 0)),
                pl.BlockSpec((tile, D_MODEL), lambda i: (jnp.maximum(i - n_ctx, 0), 0))], list(x)
    return [pl.BlockSpec((tile, D_MODEL), lambda i: (i, 0))], [x]


def _tok_value(x_refs, tile):
    if len(x_refs) == 2:
        return jnp.where(pl.program_id(0) < N_CTX // tile, x_refs[0][...], x_refs[1][...])
    return x_refs[0][...]


def _inproj_kernel(n_x, *refs):
    g_ref, sh_ref, sc_ref, w_ref, za_ref, zs_ref, zc_ref, zb_ref = refs[n_x:n_x + 8]
    x = _tok_value(refs[:n_x], TW_IN)
    if n_x == 2:
        refs[n_x + 8][...] = x
    h = _rms(x, g_ref[0]) * (1.0 + sc_ref[0, 0]) + sh_ref[0, 0]
    hb = h.astype(BF16)
    za_ref[...] = _dot(hb, w_ref[0, :, 0:ZA_W]).astype(BF16)
    zs_ref[...] = _dot(hb, w_ref[0, :, ZA_W:ZA_W + ZS_W])
    zc_ref[...] = _dot(hb, w_ref[0, :, ZA_W + ZS_W:ZA_W + ZS_W + ZC_W]).astype(BF16)
    zb_ref[...] = _dot(hb, w_ref[0, :, ZA_W + ZS_W + ZC_W:ZIN_W]).astype(BF16)


def _mod_spec(layer, k, tile=None):
    return pl.BlockSpec((1, 1, 1, D_MODEL), lambda i: (layer, _mod_row(i, tile), 0, k))


def _inproj(layer, x, norm1, mod, w_in_r):
    tok = lambda w: pl.BlockSpec((TW_IN, w), lambda i: (i, 0))
    x_specs, x_args = _tok_specs(x, TW_IN)
    return pl.pallas_call(
        functools.partial(_inproj_kernel, len(x_args)),
        grid=(N_TOK // TW_IN,),
        in_specs=x_specs + [
            pl.BlockSpec((1, 1, D_MODEL), lambda i: (layer, 0, 0)),
            _mod_spec(layer, 0, TW_IN),
            _mod_spec(layer, 1, TW_IN),
            pl.BlockSpec((1, D_MODEL, ZIN_W), lambda i: (layer, 0, 0)),
        ],
        out_specs=[tok(ZA_W), tok(ZS_W), tok(ZC_W), tok(ZB_W)] + [tok(D_MODEL)] * (len(x_args) - 1),
        out_shape=[
            jax.ShapeDtypeStruct((N_TOK, ZA_W), BF16),
            jax.ShapeDtypeStruct((N_TOK, ZS_W), F32),
            jax.ShapeDtypeStruct((N_TOK, ZC_W), BF16),
            jax.ShapeDtypeStruct((N_TOK, ZB_W), BF16),
        ] + [jax.ShapeDtypeStruct((N_TOK, D_MODEL), F32)] * (len(x_args) - 1),
        compiler_params=_cparams(("arbitrary",)),
        name="inproj",
    )(*x_args, norm1, mod, mod, w_in_r)


def _make_mlstm_kernel(seq, layer, has_init):
    nq = seq // TQ
    lane_if, lane_ff, lane_ib, lane_fb = (GATE_LANE0 + ML_HEADS * j for j in range(4))

    def kern(*refs):
        if has_init:
            body(*refs)
            return
        out = refs[10]
        b = pl.program_id(0)

        @pl.when(b < N_SEQ_CTX)
        def _():
            body(*refs)

        @pl.when(b >= N_SEQ_CTX)
        def _():
            out[...] = jnp.zeros_like(out)

    def body(*refs):
        if has_init:
            m0_ref, zq, zk, zv, zo, gz, bg, nrm, c0_ref, n0_ref, _, out, bp_scr, bs_scr = refs
        else:
            zq, zk, zv, zo, gz, bg, nrm, _, _, _, out, cf_ref, nf_ref, mf_ref, bp_scr, bs_scr = refs
        b = pl.program_id(0)
        g = gz[...] + bg[0]
        lane = lax.broadcasted_iota(jnp.int32, g.shape, 1)
        is_forget = ((lane >= lane_ff) & (lane < lane_ib)) | ((lane >= lane_fb) & (lane < lane_fb + ML_HEADS))
        log_sig = jnp.minimum(g, 0.0) - jnp.log1p(jnp.exp(-jnp.abs(g)))
        a = jnp.where(is_forget, log_sig, g)
        r_i = lax.broadcasted_iota(jnp.int32, (seq, seq), 0)
        c_i = lax.broadcasted_iota(jnp.int32, (seq, seq), 1)
        ltri = (c_i <= r_i).astype(F32)
        bp = jnp.dot(ltri, a, precision=HI, preferred_element_type=F32)
        bs = bp[seq - 1:seq, :] - bp + a
        bp_scr[...] = bp
        bs_scr[...] = bs
        eye = (lax.broadcasted_iota(jnp.int32, (LANES, LANES), 0)
               == lax.broadcasted_iota(jnp.int32, (LANES, LANES), 1)).astype(F32)
        tr = lambda x: lax.dot_general(eye, x, (((1,), (1,)), ((), ())), precision=HI, preferred_element_type=F32)
        if has_init:
            tr = lambda x: x.T
        a_t, bp_t, bs_t = tr(a), tr(bp), tr(bs)

        for h in range(ML_HEADS):
            hs = slice(h * ML_DIM, (h + 1) * ML_DIM)
            k = zk[:, hs]
            v = zv[:, hs]
            first_lane = lax.broadcasted_iota(jnp.int32, (seq, ML_DIM), 1) == 0
            v_aug = jnp.concatenate([v, jnp.where(first_lane, 1.0, 0.0).astype(BF16)], axis=1)
            rows = (
                a_t[lane_if + h:lane_if + h + 1, :] - bp_t[lane_ff + h:lane_ff + h + 1, :],
                a_t[lane_ib + h:lane_ib + h + 1, :] - bs_t[lane_fb + h:lane_fb + h + 1, :],
            )
            col_refs = ((bp_scr, lane_ff + h), (bs_scr, lane_fb + h))
            if has_init:
                m0 = tuple(m0_ref[((b * DEPTH + layer) * 2 + dr) * ML_HEADS + h] for dr in range(2))
                c0 = tuple(c0_ref[0, 0, dr, h].astype(BF16) for dr in range(2))
                n0 = tuple(jnp.broadcast_to(n0_ref[0, 0, dr, h:h + 1, :], (ML_DIM, ML_DIM)).astype(BF16) for dr in range(2))
            else:
                m0 = (0.0, 0.0)

            def qblock(qi, carry):
                q0 = pl.multiple_of(qi * TQ, TQ)
                qb = zq[pl.ds(q0, TQ), hs]
                sc = _dot_nt(qb, k)
                t_idx = q0 + lax.broadcasted_iota(jnp.int32, (TQ, seq), 0)
                s_idx = lax.broadcasted_iota(jnp.int32, (TQ, seq), 1)
                hsum = jnp.zeros((TQ, ML_DIM), F32)
                for dr in range(2):
                    cref, cl = col_refs[dr]
                    col = cref[pl.ds(q0, TQ), cl:cl + 1]
                    mask = (s_idx <= t_idx) if dr == 0 else (s_idx >= t_idx)
                    drow = jnp.where(mask, rows[dr], -jnp.inf)
                    c_t = jnp.maximum(m0[dr], jnp.max(drow, axis=1, keepdims=True))
                    s = sc * jnp.exp(drow - c_t)
                    na = _dot(s.astype(BF16), v_aug)
                    num, den = na[:, 0:ML_DIM], na[:, ML_DIM:ML_DIM + 1]
                    if has_init:
                        w_c = jnp.exp(m0[dr] - c_t)
                        num = num + w_c * _dot(qb, c0[dr])
                        den = den + w_c * _dot_nt(qb, n0[dr])[:, 0:1]
                    hsum = hsum + num / jnp.maximum(jnp.abs(den), jnp.exp(-(col + c_t)))
                hn = _rms(hsum, nrm[0][:, hs])
                og = zo[pl.ds(q0, TQ), hs].astype(F32)
                out[pl.ds(q0, TQ), hs] = (hn * jax.nn.sigmoid(og)).astype(out.dtype)
                return carry

            if nq == 1:
                qblock(0, 0)
            else:
                lax.fori_loop(0, nq, qblock, 0)

            if not has_init:
                k_t = _dot_nt(eye.astype(BF16), k)
                kf = k.astype(F32)
                tot = (bp_t[lane_ff + h:lane_ff + h + 1, seq - 1:seq], bp_t[lane_fb + h:lane_fb + h + 1, seq - 1:seq])
                gl = (
                    tot[0] + rows[0],
                    bp_t[lane_fb + h:lane_fb + h + 1, :] - a_t[lane_fb + h:lane_fb + h + 1, :]
                    + a_t[lane_ib + h:lane_ib + h + 1, :],
                )
                for dr in range(2):
                    m_new = jnp.maximum(tot[dr] + m0[dr], jnp.max(gl[dr], axis=1, keepdims=True))
                    w_s = jnp.exp(gl[dr] - m_new)
                    cf_ref[0, 0, dr, h] = _dot((k_t * w_s).astype(BF16), v)
                    n_new = jnp.dot(jnp.broadcast_to(w_s, (SUBLANES, seq)), kf, precision=HI, preferred_element_type=F32)
                    nf_ref[0, 0, dr, h:h + 1, :] = n_new[0:1, :]
                    mf_ref[0, 0, dr * ML_HEADS + h:dr * ML_HEADS + h + 1, :] = jnp.broadcast_to(m_new, (1, LANES))

    return kern


def _mlstm(layer, za, zs, b_gates, mlstm_norm, init=None, ctx_out=None, states=None):
    has_init = init is not None
    seq, nseq, row0 = (S_SMP, N_SEQ_SMP, N_CTX // S_SMP) if has_init else (S_CTX, N_SEQ_CTX, 0)
    qkvo = [pl.BlockSpec((seq, ML_WIDTH), functools.partial(lambda j, b: (row0 + b, j), j)) for j in range(4)]
    in_specs = qkvo + [
        pl.BlockSpec((seq, LANES), lambda b: (row0 + b, ZS_W // LANES - 1)),
        pl.BlockSpec((1, 1, LANES), lambda b: (layer, 0, 0)),
        pl.BlockSpec((1, 1, ML_WIDTH), lambda b: (layer, 0, 0)),
    ]
    args = [za, za, za, za, zs, b_gates, mlstm_norm]
    out_specs = [pl.BlockSpec((seq, ML_WIDTH), lambda b: (row0 + b, 0))]
    out_shape = [jax.ShapeDtypeStruct((N_TOK, ML_WIDTH), BF16)]
    aliases = {}
    if has_init:
        st_c, st_n, st_m = init
        in_specs = [pl.BlockSpec(memory_space=pltpu.SMEM)] + in_specs + [
            pl.BlockSpec((1, 1, 2, ML_HEADS, ML_DIM, ML_DIM), lambda b: (b, layer, 0, 0, 0, 0)),
            pl.BlockSpec((1, 1, 2, ML_HEADS, ML_DIM), lambda b: (b, layer, 0, 0, 0)),
            pl.BlockSpec(memory_space=pl.ANY),
        ]
        args = [st_m.reshape(-1)] + args + [st_c, st_n, ctx_out]
        aliases = {len(args) - 1: 0}
    else:
        seq_blk = lambda b: jnp.minimum(b, nseq - 1)
        in_specs += [pl.BlockSpec(memory_space=pl.ANY)] * 3
        args += list(states)
        aliases = {len(args) - 3 + j: 1 + j for j in range(3)}
        out_specs += [
            pl.BlockSpec((1, 1, 2, ML_HEADS, ML_DIM, ML_DIM), lambda b: (seq_blk(b), layer, 0, 0, 0, 0)),
            pl.BlockSpec((1, 1, 2, ML_HEADS, ML_DIM), lambda b: (seq_blk(b), layer, 0, 0, 0)),
            pl.BlockSpec((1, 1, 2 * ML_HEADS, LANES), lambda b: (seq_blk(b), layer, 0, 0)),
        ]
        out_shape += [jax.ShapeDtypeStruct(s.shape, s.dtype) for s in states]
    return pl.pallas_call(
        _make_mlstm_kernel(seq, layer, has_init),
        grid=(nseq if has_init else N_SEQ_BLOCKS,),
        in_specs=in_specs,
        out_specs=out_specs,
        out_shape=out_shape,
        scratch_shapes=[pltpu.VMEM((seq, LANES), F32), pltpu.VMEM((seq, LANES), F32)],
        input_output_aliases=aliases,
        compiler_params=_cparams(("arbitrary",)),
        name="mlstm_smp" if has_init else "mlstm_ctx",
    )(*args)


def _rope(x, cos, sin_a, sin_b):
    return x * cos + pltpu.roll(x, LANES - 8, 1) * sin_a + pltpu.roll(x, 8, 1) * sin_b


def _mla_prep_kernel(zs_ref, qa_ref, kva_ref, wuq_ref, wk_ref, wv_ref, qn_ref, kn_ref, cos_ref, sa_ref, sb_ref, _, __,
                     q_ref, kk_ref, v_ref, ckv_ref, kr_ref, qf_scr, kf_scr):
    cq = zs_ref[:, 0:MLA_Q_RANK]
    ckv = zs_ref[:, MLA_Q_RANK:MLA_Q_RANK + MLA_KV_RANK]
    last = zs_ref[:, ZS_W - LANES:ZS_W]
    qf_scr[...] = _dot(_rms(cq, qa_ref[0]).astype(BF16), wuq_ref[0])
    ckvn = _rms(ckv, kva_ref[0])

    @pl.when(pl.program_id(0) < N_TILES_CTX)
    def _():
        for j in range(TW // S_CTX):
            ckv_ref[j, 0] = ckvn[j * S_CTX:(j + 1) * S_CTX, :]
            kr_ref[j, 0] = last[j * S_CTX:(j + 1) * S_CTX, :].T[0:MLA_ROPE, :]

    cb = ckvn.astype(BF16)
    kf_scr[...] = _dot(cb, wk_ref[0])
    v_ref[...] = _dot(cb, wv_ref[0]).astype(BF16)
    lane = lax.broadcasted_iota(jnp.int32, last.shape, 1)
    kr = jnp.where((lane >= MLA_NOPE) & (lane < MLA_QK), pltpu.roll(last, MLA_NOPE, 1), 0.0)
    is_latent = pl.program_id(0) >= N_TILES_CTX

    def heads(rotate):
        for h in range(MLA_HEADS):
            hs = slice(h * HEAD_PAD, (h + 1) * HEAD_PAD)
            q_ref[:, hs] = rotate(_rms(qf_scr[:, hs], qn_ref[0], n=MLA_QK)).astype(BF16)
            kk_ref[:, hs] = rotate(_rms(kf_scr[:, hs] + kr, kn_ref[0], n=MLA_QK)).astype(BF16)

    @pl.when(is_latent)
    def _():
        cos, sa, sb = cos_ref[...], sa_ref[...], sb_ref[...]
        heads(lambda x: _rope(x, cos, sa, sb))

    @pl.when(jnp.logical_not(is_latent))
    def _():
        heads(lambda x: x)


def _mla_prep(layer, zs, q_a_norm, kv_a_norm, w_uq_r, w_k_r, w_v_r, q_norm_p, k_norm_p, rope_tabs, new_ckv, new_kr):
    seq_blk = lambda r, c: pl.BlockSpec((TW // S_CTX, 1, r, c), lambda i: (jnp.minimum(i, N_TILES_CTX - 1), layer, 0, 0))
    lw = lambda shape: pl.BlockSpec((1,) + shape, lambda i: (layer,) + (0,) * len(shape))
    tab = pl.BlockSpec((TW, LANES), lambda i: (jnp.where(i < N_TILES_CTX, 0, 1 + (i - N_TILES_CTX) % TILES_PER_SMP_SEQ), 0))
    tok = lambda w: pl.BlockSpec((TW, w), lambda i: (i, 0))
    return pl.pallas_call(
        _mla_prep_kernel,
        grid=(N_TILES,),
        in_specs=[
            tok(ZS_W), lw((1, MLA_Q_RANK)), lw((1, MLA_KV_RANK)),
            lw((MLA_Q_RANK, MLA_HEADS * HEAD_PAD)), lw((MLA_KV_RANK, MLA_HEADS * HEAD_PAD)),
            lw((MLA_KV_RANK, MLA_HEADS * MLA_V)), lw((1, HEAD_PAD)), lw((1, HEAD_PAD)), tab, tab, tab,
            pl.BlockSpec(memory_space=pl.ANY), pl.BlockSpec(memory_space=pl.ANY),
        ],
        out_specs=[tok(MLA_HEADS * HEAD_PAD), tok(MLA_HEADS * HEAD_PAD), tok(MLA_HEADS * MLA_V),
                   seq_blk(S_CTX, MLA_KV_RANK), seq_blk(MLA_ROPE, S_CTX)],
        input_output_aliases={11: 3, 12: 4},
        out_shape=[
            jax.ShapeDtypeStruct((N_TOK, MLA_HEADS * HEAD_PAD), BF16),
            jax.ShapeDtypeStruct((N_TOK, MLA_HEADS * HEAD_PAD), BF16),
            jax.ShapeDtypeStruct((N_TOK, MLA_HEADS * MLA_V), BF16),
            jax.ShapeDtypeStruct(new_ckv.shape, F32),
            jax.ShapeDtypeStruct(new_kr.shape, F32),
        ],
        scratch_shapes=[pltpu.VMEM((TW, MLA_HEADS * HEAD_PAD), F32), pltpu.VMEM((TW, MLA_HEADS * HEAD_PAD), F32)],
        compiler_params=_cparams(("arbitrary",)),
        name="mla_prep",
    )(zs, q_a_norm, kv_a_norm, w_uq_r, w_k_r, w_v_r, q_norm_p, k_norm_p, *rope_tabs, new_ckv, new_kr)


def _cache_kv_kernel(ckv_ref, kr_ref, wk_ref, wv_ref, kn_ref, kk_ref, v_ref):
    cb = ckv_ref[...].astype(BF16)
    kf = _dot(cb, wk_ref[0])
    v_ref[...] = _dot(cb, wv_ref[0]).astype(BF16)
    kr = kr_ref[...]
    for h in range(MLA_HEADS):
        hs = slice(h * HEAD_PAD, (h + 1) * HEAD_PAD)
        kk_ref[:, hs] = _rms(kf[:, hs] + kr, kn_ref[0], n=MLA_QK).astype(BF16)


def _cache_kv(cache_ckv, cache_kr_pad, w_k_r, w_v_r, k_norm_p):
    lw = lambda shape: pl.BlockSpec((1,) + shape, lambda b, l: (l,) + (0,) * len(shape))
    blk = lambda w: pl.BlockSpec((None, None, PAST_LEN, w), lambda b, l: (b, l, 0, 0))
    return pl.pallas_call(
        _cache_kv_kernel,
        grid=(N_SEQ_SMP, DEPTH),
        in_specs=[blk(MLA_KV_RANK), blk(LANES), lw((MLA_KV_RANK, MLA_HEADS * HEAD_PAD)),
                  lw((MLA_KV_RANK, MLA_HEADS * MLA_V)), lw((1, HEAD_PAD))],
        out_specs=[blk(MLA_HEADS * HEAD_PAD), blk(MLA_HEADS * MLA_V)],
        out_shape=[
            jax.ShapeDtypeStruct((N_SEQ_SMP, DEPTH, PAST_LEN, MLA_HEADS * HEAD_PAD), BF16),
            jax.ShapeDtypeStruct((N_SEQ_SMP, DEPTH, PAST_LEN, MLA_HEADS * MLA_V), BF16),
        ],
        compiler_params=_cparams(("arbitrary", "arbitrary")),
        name="cache_kv",
    )(cache_ckv, cache_kr_pad, w_k_r, w_v_r, k_norm_p)


def _make_attn_kernel(n_src):
    scale = MLA_QK ** -0.5

    def kern(q_ref, *refs):
        o_ref = refs[-1]
        if n_src > 1:
            body(q_ref, *refs)
            return

        @pl.when(pl.program_id(0) < N_SEQ_CTX)
        def _():
            body(q_ref, *refs)

        @pl.when(pl.program_id(0) >= N_SEQ_CTX)
        def _():
            o_ref[...] = jnp.zeros_like(o_ref)

    def body(q_ref, *refs):
        o_ref = refs[-1]
        for h in range(MLA_HEADS):
            hs = slice(h * HEAD_PAD, (h + 1) * HEAD_PAD)
            vs = slice(h * MLA_V, (h + 1) * MLA_V)
            q = q_ref[:, hs]
            ss = [_dot_nt(q, refs[2 * j][:, hs]) * scale for j in range(n_src)]
            m = functools.reduce(jnp.maximum, [jnp.max(s, axis=1, keepdims=True) for s in ss])
            ps = [jnp.exp(s - m) for s in ss]
            l = functools.reduce(jnp.add, [jnp.sum(p, axis=1, keepdims=True) for p in ps])
            o = functools.reduce(jnp.add, [_dot(ps[j].astype(BF16), refs[2 * j + 1][:, vs]) for j in range(n_src)])
            o_ref[:, vs] = (o / l).astype(o_ref.dtype)

    return kern


def _attn_ctx(q, kk, v):
    blk = lambda w: pl.BlockSpec((S_CTX, w), lambda b: (b, 0))
    return pl.pallas_call(
        _make_attn_kernel(1),
        grid=(N_SEQ_BLOCKS,),
        in_specs=[blk(MLA_HEADS * HEAD_PAD), blk(MLA_HEADS * HEAD_PAD), blk(MLA_HEADS * MLA_V)],
        out_specs=blk(MLA_HEADS * MLA_V),
        out_shape=jax.ShapeDtypeStruct((N_TOK, MLA_HEADS * MLA_V), BF16),
        compiler_params=_cparams(("arbitrary",)),
        name="attn_ctx",
    )(q, kk, v)


def _attn_smp(layer, q, kk, v, kk_cache, v_cache, ctx_out):
    row0 = N_CTX // S_SMP
    nq = S_SMP // TQ
    seqb = lambda w: pl.BlockSpec((S_SMP, w), lambda b, i: (row0 + b, 0))
    cache = lambda w: pl.BlockSpec((None, None, PAST_LEN, w), lambda b, i: (b, layer, 0, 0))
    return pl.pallas_call(
        _make_attn_kernel(2),
        grid=(N_SEQ_SMP, nq),
        in_specs=[
            pl.BlockSpec((TQ, MLA_HEADS * HEAD_PAD), lambda b, i: (N_CTX // TQ + b * nq + i, 0)),
            seqb(MLA_HEADS * HEAD_PAD), seqb(MLA_HEADS * MLA_V),
            cache(MLA_HEADS * HEAD_PAD), cache(MLA_HEADS * MLA_V),
            pl.BlockSpec(memory_space=pl.ANY),
        ],
        out_specs=pl.BlockSpec((TQ, MLA_HEADS * MLA_V), lambda b, i: (N_CTX // TQ + b * nq + i, 0)),
        out_shape=jax.ShapeDtypeStruct((N_TOK, MLA_HEADS * MLA_V), BF16),
        input_output_aliases={5: 0},
        compiler_params=_cparams(("arbitrary", "arbitrary")),
        name="attn_smp",
    )(q, kk, v, kk_cache, v_cache, ctx_out)


def _merge_kernel(x_ref, oa_ref, ob_ref, zc_ref, zb_ref, sgn_ref, ws_ref, bs_ref, wb_ref, wo_ref, g1_ref, n2_ref,
                  sh2_ref, sc2_ref, wr_ref, br_ref,
                  xmid_ref, h2_ref, dest_ref, wsel_ref, cnt_ref, oc_scr, carry_scr):
    i = pl.program_id(0)

    @pl.when(i == 0)
    def _():
        carry_scr[...] = jnp.zeros_like(carry_scr)

    u = _gelu(zc_ref[:, 0:SG_WIDTH].astype(F32))
    vg = _gelu(zc_ref[:, SG_WIDTH:2 * SG_WIDTH].astype(F32))
    for g in range(SG_GROUPS):
        gs = slice(g * SG_DIM, (g + 1) * SG_DIM)
        vn = _rms(vg[:, gs], sgn_ref[0][:, gs]).astype(BF16)
        for c in range(TW // SG_CHUNK):
            cs = slice(c * SG_CHUNK, (c + 1) * SG_CHUNK)
            mixed = _dot(ws_ref[0, g], vn[cs, :]) + bs_ref[0][:, g:g + 1]
            oc_scr[cs, gs] = (u[cs, gs] * mixed).astype(BF16)

    acc = jnp.zeros((TW, D_MODEL), F32)
    for j, src in enumerate((oa_ref, ob_ref, oc_scr)):
        gate = jax.nn.sigmoid(zb_ref[:, j * D_MODEL:(j + 1) * D_MODEL].astype(F32))
        acc = acc + gate * _dot(src[...], wb_ref[0, j])
    xm = x_ref[...] + g1_ref[0, 0] * _dot(acc.astype(BF16), wo_ref[0])
    xmid_ref[...] = xm
    h2 = _rms(xm, n2_ref[0]) * (1.0 + sc2_ref[0, 0]) + sh2_ref[0, 0]
    for c in range(SC_SPLIT):
        h2_ref[c] = _pack_pairs(h2[:, 2 * c * SC_ROW:(2 * c + 1) * SC_ROW], h2[:, (2 * c + 1) * SC_ROW:(2 * c + 2) * SC_ROW])

    h_hi = h2.astype(BF16)
    h_lo = (h2 - h_hi.astype(F32)).astype(BF16)
    p_hi = _dot(h_hi, wr_ref[0])
    logits = p_hi[:, 0:LANES] + p_hi[:, LANES:2 * LANES] + _dot(h_lo, wr_ref[0, :, 0:LANES]) + br_ref[0]
    lane = lax.broadcasted_iota(jnp.int32, logits.shape, 1)
    hits, exps = [], []
    sel = jnp.zeros(logits.shape, F32)
    denom = jnp.zeros((TW, 1), F32)
    top = None
    for _ in range(TOP_K):
        m = jnp.max(logits, axis=1, keepdims=True)
        idx = jnp.min(jnp.where(logits == m, lane, LANES), axis=1, keepdims=True)
        hit = lane == idx
        top = m if top is None else top
        hits.append(hit)
        exps.append(jnp.exp(m - top))
        sel = jnp.where(hit, 1.0, sel)
        denom = denom + exps[-1]
        logits = jnp.where(hit, -jnp.inf, logits)

    r_i = lax.broadcasted_iota(jnp.int32, (TW, TW), 0)
    c_i = lax.broadcasted_iota(jnp.int32, (TW, TW), 1)
    carry = carry_scr[0:1, :]
    rank = _dot(jnp.where(c_i < r_i, 1.0, 0.0).astype(BF16), sel.astype(BF16)) + carry
    new_carry = carry + jnp.sum(sel, axis=0, keepdims=True)
    carry_scr[...] = jnp.broadcast_to(new_carry, (SUBLANES, LANES))
    cnt_ref[...] = jnp.broadcast_to(new_carry, (SUBLANES, LANES))
    slot = rank + lane.astype(F32) * float(EXPERT_CAP)
    dmat = jnp.zeros(logits.shape, F32)
    wmat = jnp.zeros(logits.shape, F32)
    for k in range(TOP_K):
        dk = jnp.sum(jnp.where(hits[k], slot, 0.0), axis=1, keepdims=True)
        dmat = jnp.where(lane == k, dk, dmat)
        wmat = jnp.where(lane == k, exps[k] / denom, wmat)
    dest_ref[...] = dmat.T[0:SUBLANES, :].astype(jnp.int32)
    wsel_ref[...] = wmat


def _merge(layer, x, oa, ob, zc, zb, sg_norm, w_sp, b_sp, w_branch, w_out, mod, norm2, w_router_p, b_router_p):
    lw = lambda shape: pl.BlockSpec((1,) + shape, lambda i: (layer,) + (0,) * len(shape))
    tok = lambda w: pl.BlockSpec((TW, w), lambda i: (i, 0))
    return pl.pallas_call(
        _merge_kernel,
        grid=(N_TILES,),
        in_specs=[
            tok(D_MODEL), tok(ML_WIDTH), tok(MLA_HEADS * MLA_V), tok(ZC_W), tok(ZB_W),
            lw((1, SG_WIDTH)), lw((SG_GROUPS, SG_CHUNK, SG_CHUNK)), lw((SG_CHUNK, LANES)),
            lw((N_BRANCH, ML_WIDTH, D_MODEL)), lw((D_MODEL, D_MODEL)),
            _mod_spec(layer, 2), lw((1, D_MODEL)), _mod_spec(layer, 3), _mod_spec(layer, 4),
            lw((D_MODEL, 2 * LANES)), lw((1, LANES)),
        ],
        out_specs=[tok(D_MODEL), pl.BlockSpec((SC_SPLIT, TW, SC_ROW), lambda i: (0, i, 0)),
                   pl.BlockSpec((SUBLANES, TW), lambda i: (0, i)), tok(LANES),
                   pl.BlockSpec((SUBLANES, LANES), lambda i: (0, 0))],
        out_shape=[
            jax.ShapeDtypeStruct((N_TOK, D_MODEL), F32),
            jax.ShapeDtypeStruct((SC_SPLIT, N_TOK, SC_ROW), jnp.uint32),
            jax.ShapeDtypeStruct((SUBLANES, N_TOK), jnp.int32),
            jax.ShapeDtypeStruct((N_TOK, LANES), F32),
            jax.ShapeDtypeStruct((SUBLANES, LANES), F32),
        ],
        scratch_shapes=[pltpu.VMEM((TW, SG_WIDTH), BF16), pltpu.VMEM((SUBLANES, LANES), F32)],
        compiler_params=_cparams(("arbitrary",)),
        name="merge_router",
    )(x, oa, ob, zc, zb, sg_norm, w_sp, b_sp, w_branch, w_out, mod, norm2, mod, mod, w_router_p, b_router_p)


def _sc_mesh():
    return plsc.VectorSubcoreMesh(core_axis_name="core", subcore_axis_name="subcore")


def _sc_scatter_rows(x, idxs, n_rows):
    @pl.kernel(out_type=jax.ShapeDtypeStruct((n_rows, SC_ROW), x.dtype), mesh=_sc_mesh(), scratch_types=[])
    def scatter(x_hbm, *refs):
        o_hbm = refs[-1]

        def body(x_vmem, *i_vmems):
            for i_vmem in i_vmems:
                pltpu.sync_copy(x_vmem, o_hbm.at[i_vmem.at[0]])

        pltpu.emit_pipeline(
            body,
            grid=(x.shape[0] // SC_WIN,),
            in_specs=[pl.BlockSpec((SC_WIN, SC_ROW), lambda i: (i, 0))]
            + [pl.BlockSpec((1, SC_WIN), lambda i: (0, i))] * len(idxs),
            out_specs=[],
            core_axis_name=("core", "subcore"),
            dimension_semantics=(pltpu.PARALLEL,),
        )(x_hbm, *refs[:-1])

    return scatter(x, *idxs)


def _sc_gather_rows(x, idx):
    m = idx.shape[1]

    @pl.kernel(out_type=jax.ShapeDtypeStruct((m, SC_ROW), x.dtype), mesh=_sc_mesh())
    def gather(x_hbm, i_hbm, o_hbm):
        def body(i_vmem, o_vmem):
            pltpu.sync_copy(x_hbm.at[i_vmem.at[0]], o_vmem)

        pltpu.emit_pipeline(
            body,
            grid=(m // SC_WIN,),
            in_specs=[pl.BlockSpec((1, SC_WIN), lambda i: (0, i))],
            out_specs=[pl.BlockSpec((SC_WIN, SC_ROW), lambda i: (i, 0))],
            core_axis_name=("core", "subcore"),
            dimension_semantics=(pltpu.PARALLEL,),
        )(i_hbm, o_hbm)

    return gather(x, idx)


STEP_VALID, STEP_FIRST, STEP_HAS_NEXT, STEP_FULL = 1, 2, 4, 8


def _moe_ffn_kernel(layer, be_ref, nx_ref, br_ref, fl_ref, xs_ref, b1_ref, b2_ref, w1_hbm, w2_hbm, y_ref,
                    w1f, w2f, w1b, w2b, sem):
    g = pl.program_id(0)
    flags = fl_ref[g]

    def weight_copies(e):
        return (pltpu.make_async_copy(w1_hbm.at[layer, e], w1f, sem.at[0]),
                pltpu.make_async_copy(w2_hbm.at[layer, e], w2f, sem.at[1]))

    @pl.when(g == 0)
    def _():
        for cp in weight_copies(be_ref[0]):
            cp.start()

    @pl.when((flags & STEP_FIRST) != 0)
    def _():
        for cp in weight_copies(be_ref[g]):
            cp.wait()
        w1b[...] = w1f[...].astype(BF16)
        w2b[...] = w2f[...].astype(BF16)

        @pl.when((flags & STEP_HAS_NEXT) != 0)
        def _():
            for cp in weight_copies(nx_ref[g]):
                cp.start()

    def ffn(n_rows):
        halves = [h.astype(BF16) for c in range(SC_SPLIT) for h in _unpack_pairs(xs_ref[c, 0:n_rows, :])]
        g1 = _dot(jnp.concatenate(halves, axis=1), w1b[...]) + b1_ref[0, 0]
        gate = jnp.minimum(g1[:, :D_EXPERT], SWIGLU_LIMIT)
        up = jnp.clip(g1[:, D_EXPERT:], -SWIGLU_LIMIT, SWIGLU_LIMIT)
        act = gate * jax.nn.sigmoid(SWIGLU_ALPHA * gate) * (up + 1.0)
        y = _dot(act.astype(BF16), w2b[...]) + b2_ref[0, 0]
        for c in range(SC_SPLIT):
            y_ref[c, 0:n_rows, :] = _pack_pairs(
                y[:, 2 * c * SC_ROW:(2 * c + 1) * SC_ROW], y[:, (2 * c + 1) * SC_ROW:(2 * c + 2) * SC_ROW])

    pl.when((flags & (STEP_VALID | STEP_FULL)) == STEP_VALID)(lambda: ffn(SLOT_CHUNK))
    pl.when((flags & STEP_FULL) != 0)(lambda: ffn(FFN_BLOCK))


def _moe_ffn(layer, xs, plan, w1, b1, w2, b2):
    eb = lambda c: pl.BlockSpec((1, 1, 1, c), lambda g, be, nx, br, fl: (layer, be[g], 0, 0))
    rows = pl.BlockSpec((SC_SPLIT, FFN_BLOCK, SC_ROW), lambda g, be, nx, br, fl: (0, br[g], 0))
    hbm = pl.BlockSpec(memory_space=pl.ANY)
    grid_spec = pltpu.PrefetchScalarGridSpec(
        num_scalar_prefetch=4,
        grid=(N_CHUNK_STEPS,),
        in_specs=[rows, eb(2 * D_EXPERT), eb(D_MODEL), hbm, hbm],
        out_specs=rows,
        scratch_shapes=[
            pltpu.VMEM((D_MODEL, 2 * D_EXPERT), F32), pltpu.VMEM((D_EXPERT, D_MODEL), F32),
            pltpu.VMEM((D_MODEL, 2 * D_EXPERT), BF16), pltpu.VMEM((D_EXPERT, D_MODEL), BF16),
            pltpu.SemaphoreType.DMA((2,)),
        ],
    )
    return pl.pallas_call(
        functools.partial(_moe_ffn_kernel, layer),
        grid_spec=grid_spec,
        out_shape=jax.ShapeDtypeStruct(xs.shape, xs.dtype),
        compiler_params=_cparams(("arbitrary",)),
        name="moe_ffn",
    )(*plan, xs, b1, b2, w1, w2)


def _chunk_plan_kernel(cnt_ref, be_ref, nx_ref, br_ref, fl_ref):
    def expert(e, carry):
        step0, prev_first = carry
        c = cnt_ref[e]
        n_blk = (c + FFN_BLOCK - 1) // FFN_BLOCK

        def block(j, _):
            s = step0 + j
            be_ref[s] = e
            nx_ref[s] = e
            br_ref[s] = e * (EXPERT_CAP // FFN_BLOCK) + j
            fl_ref[s] = (STEP_VALID + jnp.where(j == 0, STEP_FIRST, 0)
                         + jnp.where(c - j * FFN_BLOCK > SLOT_CHUNK, STEP_FULL, 0))
            return 0

        lax.fori_loop(0, n_blk, block, 0)

        @pl.when((n_blk > 0) & (prev_first >= 0))
        def _():
            nx_ref[prev_first] = e
            fl_ref[prev_first] = fl_ref[prev_first] + STEP_HAS_NEXT

        return step0 + n_blk, jnp.where(n_blk > 0, step0, prev_first)

    used, _ = lax.fori_loop(0, N_EXPERTS, expert, (jnp.int32(0), jnp.int32(-1)))

    def idle(s, _):
        be_ref[s] = be_ref[used - 1]
        nx_ref[s] = be_ref[used - 1]
        br_ref[s] = br_ref[used - 1]
        fl_ref[s] = 0
        return 0

    lax.fori_loop(used, N_CHUNK_STEPS, idle, 0)


def _chunk_plan(cnt):
    smem = pl.BlockSpec(memory_space=pltpu.SMEM)
    return pl.pallas_call(
        _chunk_plan_kernel,
        in_specs=[smem],
        out_specs=[smem] * 4,
        out_shape=[jax.ShapeDtypeStruct((N_CHUNK_STEPS,), jnp.int32)] * 4,
        name="chunk_plan",
    )(cnt)


def _combine_kernel(x_ref, yg_ref, w_ref, g_ref, *o_refs):
    def emit(o_ref):
        w = w_ref[...]
        for c in range(SC_SPLIT):
            parts = [_unpack_pairs(yg_ref[k, c]) for k in range(TOP_K)]
            for half in range(2):
                cs = slice((2 * c + half) * SC_ROW, (2 * c + half + 1) * SC_ROW)
                acc = w[:, 0:1] * parts[0][half]
                for k in range(1, TOP_K):
                    acc = acc + w[:, k:k + 1] * parts[k][half]
                o_ref[:, cs] = x_ref[:, cs] + g_ref[0, 0][:, cs] * acc

    if len(o_refs) == 1:
        emit(o_refs[0])
    else:
        pl.when(pl.program_id(0) < N_TILES_CTX)(lambda: emit(o_refs[0]))
        pl.when(pl.program_id(0) >= N_TILES_CTX)(lambda: emit(o_refs[1]))


def _combine(layer, xmid, yg, wsel, mod, split_out):
    tok = lambda w: pl.BlockSpec((TW, w), lambda i: (i, 0))
    if split_out:
        out_specs = [pl.BlockSpec((TW, D_MODEL), lambda i: (jnp.minimum(i, N_TILES_CTX - 1), 0)),
                     pl.BlockSpec((TW, D_MODEL), lambda i: (jnp.maximum(i - N_TILES_CTX, 0), 0))]
        out_shape = [jax.ShapeDtypeStruct((N_CTX, D_MODEL), F32), jax.ShapeDtypeStruct((N_SMP, D_MODEL), F32)]
    else:
        out_specs, out_shape = tok(D_MODEL), jax.ShapeDtypeStruct((N_TOK, D_MODEL), F32)
    return pl.pallas_call(
        _combine_kernel,
        grid=(N_TILES,),
        in_specs=[tok(D_MODEL), pl.BlockSpec((TOP_K, SC_SPLIT, TW, SC_ROW), lambda i: (0, 0, i, 0)), tok(LANES),
                  _mod_spec(layer, 5)],
        out_specs=out_specs,
        out_shape=out_shape,
        compiler_params=_cparams(("arbitrary",)),
        name="combine",
    )(xmid, yg, wsel, mod)


def _moe(layer, xmid, h2, dest, wsel, cnt, mod, w1, b1, w2, b2, split_out):
    n_slots = N_EXPERTS * EXPERT_CAP
    idx = dest[0:TOP_K][:, None, :] + (jnp.arange(SC_SPLIT, dtype=jnp.int32) * n_slots)[None, :, None]
    idx = idx.reshape(TOP_K, 1, SC_SPLIT * N_TOK)
    xs = _sc_scatter_rows(h2.reshape(SC_SPLIT * N_TOK, SC_ROW), [idx[k] for k in range(TOP_K)], SC_SPLIT * n_slots)
    plan = _chunk_plan(cnt[0, :N_EXPERTS].astype(jnp.int32))
    y = _moe_ffn(layer, xs.reshape(SC_SPLIT, n_slots, SC_ROW), plan, w1, b1, w2, b2)
    yg = _sc_gather_rows(y.reshape(SC_SPLIT * n_slots, SC_ROW), idx.reshape(1, TOP_K * SC_SPLIT * N_TOK))
    return _combine(layer, xmid, yg.reshape(TOP_K, SC_SPLIT, N_TOK, SC_ROW), wsel, mod, split_out)


def _rope_tables():
    pos = np.arange(S_SMP)
    half = MLA_ROPE // 2
    inv_freq = (ROPE_THETA ** (-(np.arange(0, half, 2, dtype=np.float32) / np.float32(half)))).astype(np.float32)
    angs = [((pos // GRID_W).astype(np.float32)[:, None] * inv_freq[None, :]).astype(np.float32),
            ((pos % GRID_W).astype(np.float32)[:, None] * inv_freq[None, :]).astype(np.float32)]
    nf = half // 2
    cos = np.ones((TW + S_SMP, LANES), np.float32)
    sin_a = np.zeros((TW + S_SMP, LANES), np.float32)
    sin_b = np.zeros((TW + S_SMP, LANES), np.float32)
    for axis, ang in enumerate(angs):
        base = MLA_NOPE + axis * half
        c, s = np.cos(ang.astype(np.float64)), np.sin(ang.astype(np.float64))
        cos[TW:, base:base + nf] = c
        cos[TW:, base + nf:base + half] = c
        sin_a[TW:, base:base + nf] = -s
        sin_b[TW:, base + nf:base + half] = s
    return jnp.asarray(cos), jnp.asarray(sin_a), jnp.asarray(sin_b)


def _pad_last(a, width):
    return jnp.pad(a, [(0, 0)] * (a.ndim - 1) + [(0, width - a.shape[-1])])


def kernel(x_prompt, x_sample, cache_mla_ckv, cache_mla_krope, state_mlstm_C, state_mlstm_n, state_mlstm_m, c, c_ctx, norm1, norm2, w_ada, b_ada, w_in, b_mlstm_gates, mlstm_norm, mla_q_a_norm, mla_kv_a_norm, w_uq, w_ukv, mla_q_norm, mla_k_norm, sg_norm, w_spatial, b_spatial, w_branch, w_out, w_router, b_router, w_exp1, b_exp1, w_exp2, b_exp2):
    w_in_r = _w_in_prep(w_in)
    w_uq_r = _pad_last(w_uq.reshape(DEPTH, MLA_Q_RANK, MLA_HEADS, MLA_QK), HEAD_PAD).reshape(
        DEPTH, MLA_Q_RANK, MLA_HEADS * HEAD_PAD).astype(BF16)
    w_ukv4 = w_ukv.reshape(DEPTH, MLA_KV_RANK, MLA_HEADS, MLA_NOPE + MLA_V)
    w_k_r = _pad_last(w_ukv4[..., :MLA_NOPE], HEAD_PAD).reshape(DEPTH, MLA_KV_RANK, MLA_HEADS * HEAD_PAD).astype(BF16)
    w_v_r = w_ukv4[..., MLA_NOPE:].reshape(DEPTH, MLA_KV_RANK, MLA_HEADS * MLA_V).astype(BF16)
    q_norm_p = _pad_last(mla_q_norm, HEAD_PAD).reshape(DEPTH, 1, HEAD_PAD)
    k_norm_p = _pad_last(mla_k_norm, HEAD_PAD).reshape(DEPTH, 1, HEAD_PAD)
    b_gates_p = jnp.pad(b_mlstm_gates, ((0, 0), (GATE_LANE0, LANES - GATE_LANE0 - 4 * ML_HEADS))).reshape(DEPTH, 1, LANES)
    b_sp = _pad_last(jnp.swapaxes(b_spatial, 1, 2), LANES)
    w_router_p = _pad_last(w_router, LANES)
    w_router_hi = w_router_p.astype(BF16)
    w_router_p = jnp.concatenate([w_router_hi, (w_router_p - w_router_hi.astype(F32)).astype(BF16)], axis=-1)
    b_router_p = jnp.pad(b_router, ((0, 0), (0, LANES - N_EXPERTS)), constant_values=-1e30).reshape(DEPTH, 1, LANES)
    r3 = lambda a: a.reshape(DEPTH, 1, a.shape[-1])
    cache_kr_pad = jnp.pad(cache_mla_krope, ((0, 0), (0, 0), (0, 0), (MLA_NOPE, LANES - MLA_QK)))
    rope_tabs = _rope_tables()

    cvec = jnp.concatenate([c_ctx[None, :], c, jnp.zeros((SUBLANES - 1 - N_SEQ_SMP, D_MODEL), F32)], axis=0)
    mod = _adaln(cvec, w_ada, b_ada).reshape(DEPTH, SUBLANES, 1, 6 * D_MODEL)
    b1 = b_exp1.reshape(DEPTH, N_EXPERTS, 1, 2 * D_EXPERT)
    b2 = b_exp2.reshape(DEPTH, N_EXPERTS, 1, D_MODEL)
    kk_cache, v_cache = _cache_kv(cache_mla_ckv, cache_kr_pad, w_k_r, w_v_r, k_norm_p)

    x = (x_prompt.reshape(N_CTX, D_MODEL), x_sample.reshape(N_SMP, D_MODEL))
    new_ckv = jnp.zeros((N_SEQ_CTX, DEPTH, S_CTX, MLA_KV_RANK), F32)
    new_kr = jnp.zeros((N_SEQ_CTX, DEPTH, MLA_ROPE, S_CTX), F32)
    states = (jnp.zeros((N_SEQ_CTX, DEPTH, 2, ML_HEADS, ML_DIM, ML_DIM), F32),
              jnp.zeros((N_SEQ_CTX, DEPTH, 2, ML_HEADS, ML_DIM), F32),
              jnp.zeros((N_SEQ_CTX, DEPTH, 2 * ML_HEADS, LANES), F32))
    for l in range(DEPTH):
        za, zs, zc, zb, *x_joined = _inproj(l, x, r3(norm1), mod, w_in_r)
        x = x_joined[0] if x_joined else x
        oa, *states = _mlstm(l, za, zs, b_gates_p, r3(mlstm_norm), states=states)
        (oa,) = _mlstm(l, za, zs, b_gates_p, r3(mlstm_norm), init=(state_mlstm_C, state_mlstm_n, state_mlstm_m), ctx_out=oa)
        q, kk, v, new_ckv, new_kr = _mla_prep(l, zs, r3(mla_q_a_norm), r3(mla_kv_a_norm), w_uq_r, w_k_r, w_v_r,
                                              q_norm_p, k_norm_p, rope_tabs, new_ckv, new_kr)
        ob = _attn_smp(l, q, kk, v, kk_cache, v_cache, _attn_ctx(q, kk, v))
        xmid, h2, dest, wsel, cnt = _merge(
            l, x, oa, ob, zc, zb, r3(sg_norm), w_spatial.astype(BF16), b_sp, w_branch.astype(BF16), w_out.astype(BF16),
            mod, r3(norm2), w_router_p, b_router_p)
        x = _moe(l, xmid, h2, dest, wsel, cnt, mod, w_exp1, b1, w_exp2, b2, split_out=l == DEPTH - 1)
    y_ctx, y_smp = x
    return (
        y_ctx.reshape(N_SEQ_CTX, S_CTX, D_MODEL),
        y_smp.reshape(N_SEQ_SMP, S_SMP, D_MODEL),
        new_ckv,
        jnp.swapaxes(new_kr, 2, 3),
        states[0],
        states[1],
        states[2][:, :, :, 0].reshape(N_SEQ_CTX, DEPTH, 2, ML_HEADS),
    )
```

```python
import functools

import numpy as np
import jax
import jax.numpy as jnp
from jax import lax
from jax.experimental import pallas as pl
from jax.experimental.pallas import tpu as pltpu
from jax.experimental.pallas import tpu_sc as plsc

F32 = jnp.float32
BF16 = jnp.bfloat16
HI = lax.Precision.HIGHEST

D_MODEL = 1024
N_SEQ_CTX, S_CTX = 32, 256
N_SEQ_SMP, S_SMP = 2, 1024
DEPTH = 4
PAST_LEN = 512
GRID_W = 64
EPS = 1e-6
ML_HEADS, ML_DIM = 4, 128
ML_WIDTH = ML_HEADS * ML_DIM
MLA_HEADS, MLA_NOPE, MLA_ROPE, MLA_V = 8, 64, 32, 64
MLA_QK = MLA_NOPE + MLA_ROPE
MLA_Q_RANK, MLA_KV_RANK = 256, 128
ROPE_THETA = 10000.0
SG_GROUPS, SG_DIM, SG_CHUNK = 4, 128, 128
SG_WIDTH = SG_GROUPS * SG_DIM
N_BRANCH = 3
N_EXPERTS, TOP_K, D_EXPERT = 32, 4, 1024
SWIGLU_LIMIT, SWIGLU_ALPHA = 7.0, 1.702

N_CTX = N_SEQ_CTX * S_CTX
N_SMP = N_SEQ_SMP * S_SMP
N_TOK = N_CTX + N_SMP

LANES = 128
SUBLANES = 8
VMEM_LIMIT = 56 * 1024 * 1024

TW = 1024
TW_IN = 512
N_TILES = N_TOK // TW
N_TILES_CTX = N_CTX // TW
TILES_PER_SMP_SEQ = S_SMP // TW
N_SEQ_BLOCKS = N_TOK // S_CTX
HEAD_PAD = LANES
TQ = 256
EXPERT_CAP = N_TOK
SLOT_CHUNK = 128
FFN_CHUNKS = 4
FFN_BLOCK = FFN_CHUNKS * SLOT_CHUNK
N_CHUNK_STEPS = N_TOK * TOP_K // FFN_BLOCK + N_EXPERTS
SC_ROW = 256
SC_SPLIT = D_MODEL // (2 * SC_ROW)
SC_WIN = 128

ZA_W = 4 * ML_WIDTH
ZS_W = 512
ZC_W = 2 * SG_WIDTH
ZB_W = N_BRANCH * D_MODEL
ZIN_W = ZA_W + ZS_W + ZC_W + ZB_W
GATE_LANE0 = MLA_ROPE


def _cparams(sem):
    return pltpu.CompilerParams(dimension_semantics=sem, vmem_limit_bytes=VMEM_LIMIT)


def _mod_row(i, tile=None):
    tile = tile or TW
    return jnp.where(i < N_CTX // tile, 0, 1 + (i - N_CTX // tile) // (S_SMP // tile))


def _rms(x, g, n=None):
    ms = jnp.sum(x * x, axis=-1, keepdims=True) * (1.0 / (n or x.shape[-1]))
    return x * lax.rsqrt(ms + EPS) * g


def _gelu(x):
    return 0.5 * x * (1.0 + jnp.tanh(0.7978845608028654 * (x + 0.044715 * (x * x * x))))


def _pack_pairs(lo, hi):
    lo_bits = lax.bitcast_convert_type(lo.astype(BF16).astype(F32), jnp.uint32)
    hi_bits = lax.bitcast_convert_type(hi.astype(BF16).astype(F32), jnp.uint32)
    return (lo_bits >> 16) | (hi_bits & jnp.uint32(0xFFFF0000))


def _unpack_pairs(u):
    return (lax.bitcast_convert_type(u << 16, F32), lax.bitcast_convert_type(u & jnp.uint32(0xFFFF0000), F32))


def _dot(a, b):
    return jnp.dot(a, b, preferred_element_type=F32)


def _dot_nt(a, b):
    return lax.dot_general(a, b, (((1,), (1,)), ((), ())), preferred_element_type=F32)


def _adaln_kernel(c_ref, w_ref, b_ref, o_ref):
    c = c_ref[...]
    s = c * jax.nn.sigmoid(c)
    w = w_ref[0]
    s_hi, w_hi = s.astype(BF16), w.astype(BF16)
    s_lo, w_lo = (s - s_hi.astype(F32)).astype(BF16), (w - w_hi.astype(F32)).astype(BF16)
    o_ref[0] = _dot(s_hi, w_hi) + _dot(s_hi, w_lo) + _dot(s_lo, w_hi) + b_ref[0]


def _adaln(cvec, w_ada, b_ada):
    nchunk = 4
    cw = 6 * D_MODEL // nchunk
    return pl.pallas_call(
        _adaln_kernel,
        grid=(DEPTH, nchunk),
        in_specs=[
            pl.BlockSpec((SUBLANES, D_MODEL), lambda l, j: (0, 0)),
            pl.BlockSpec((1, D_MODEL, cw), lambda l, j: (l, 0, j)),
            pl.BlockSpec((1, 1, cw), lambda l, j: (l, 0, j)),
        ],
        out_specs=pl.BlockSpec((1, SUBLANES, cw), lambda l, j: (l, 0, j)),
        out_shape=jax.ShapeDtypeStruct((DEPTH, SUBLANES, 6 * D_MODEL), F32),
        compiler_params=_cparams(("arbitrary", "arbitrary")),
        name="adaln",
    )(cvec, w_ada, b_ada.reshape(DEPTH, 1, 6 * D_MODEL))


IN_SPLITS = (ML_WIDTH, ML_WIDTH, ML_WIDTH, ML_WIDTH, 4 * ML_HEADS, MLA_Q_RANK, MLA_KV_RANK, MLA_ROPE, SG_WIDTH, SG_WIDTH,
             N_BRANCH * D_MODEL)
IN_OFFS = tuple(int(v) for v in np.cumsum((0,) + IN_SPLITS))
D_IN = IN_OFFS[-1]
W_PREP_ROWS = 256
W_PREP_COLS = 512


def _w_in_prep_kernel(wt_ref, o_ref):
    o = IN_OFFS

    def put(c0, rows):
        o_ref[0, :, c0:c0 + W_PREP_COLS] = rows.T.astype(BF16)

    for c0 in range(0, ZA_W, W_PREP_COLS):
        scale = ML_DIM ** -0.5 if o[1] <= c0 < o[2] else 1.0
        put(c0, wt_ref[0, c0:c0 + W_PREP_COLS, :] * scale)
    pad = jnp.zeros((ZS_W - (o[8] - o[4]), W_PREP_ROWS), F32)
    put(ZA_W, jnp.concatenate([wt_ref[0, o[5]:o[8], :], wt_ref[0, o[4]:o[5], :], pad], axis=0))
    for c0 in range(ZA_W + ZS_W, ZIN_W, W_PREP_COLS):
        src = c0 - (ZA_W + ZS_W) + o[8]
        put(c0, wt_ref[0, src:src + W_PREP_COLS, :])


def _w_in_prep(w_in):
    return pl.pallas_call(
        _w_in_prep_kernel,
        grid=(DEPTH, D_MODEL // W_PREP_ROWS),
        in_specs=[pl.BlockSpec((1, D_IN, W_PREP_ROWS), lambda l, r: (l, 0, r))],
        out_specs=pl.BlockSpec((1, W_PREP_ROWS, ZIN_W), lambda l, r: (l, r, 0)),
        out_shape=jax.ShapeDtypeStruct((DEPTH, D_MODEL, ZIN_W), BF16),
        compiler_params=_cparams(("arbitrary", "arbitrary")),
        name="w_in_prep",
    )(jnp.swapaxes(w_in, 1, 2))


def _tok_specs(x, tile):
    if isinstance(x, tuple):
        n_ctx = N_CTX // tile
        return [pl.BlockSpec((tile, D_MODEL), lambda i: (jnp.minimum(i, n_ctx - 1), 0)),
                pl.BlockSpec((tile, D_MODEL), lambda i: (jnp.maximum(i - n_ctx, 0), 0))], list(x)
    return [pl.BlockSpec((tile, D_MODEL), lambda i: (i, 0))], [x]


def _tok_value(x_refs, tile):
    if len(x_refs) == 2:
        return jnp.where(pl.program_id(0) < N_CTX // tile, x_refs[0][...], x_refs[1][...])
    return x_refs[0][...]


def _inproj_kernel(n_x, *refs):
    g_ref, sh_ref, sc_ref, w_ref, za_ref, zs_ref, zc_ref, zb_ref = refs[n_x:n_x + 8]
    x = _tok_value(refs[:n_x], TW_IN)
    if n_x == 2:
        refs[n_x + 8][...] = x
    h = _rms(x, g_ref[0]) * (1.0 + sc_ref[0, 0]) + sh_ref[0, 0]
    hb = h.astype(BF16)
    za_ref[...] = _dot(hb, w_ref[0, :, 0:ZA_W]).astype(BF16)
    zs_ref[...] = _dot(hb, w_ref[0, :, ZA_W:ZA_W + ZS_W])
    zc_ref[...] = _dot(hb, w_ref[0, :, ZA_W + ZS_W:ZA_W + ZS_W + ZC_W]).astype(BF16)
    zb_ref[...] = _dot(hb, w_ref[0, :, ZA_W + ZS_W + ZC_W:ZIN_W]).astype(BF16)


def _mod_spec(layer, k, tile=None):
    return pl.BlockSpec((1, 1, 1, D_MODEL), lambda i: (layer, _mod_row(i, tile), 0, k))


def _inproj(layer, x, norm1, mod, w_in_r):
    tok = lambda w: pl.BlockSpec((TW_IN, w), lambda i: (i, 0))
    x_specs, x_args = _tok_specs(x, TW_IN)
    return pl.pallas_call(
        functools.partial(_inproj_kernel, len(x_args)),
        grid=(N_TOK // TW_IN,),
        in_specs=x_specs + [
            pl.BlockSpec((1, 1, D_MODEL), lambda i: (layer, 0, 0)),
            _mod_spec(layer, 0, TW_IN),
            _mod_spec(layer, 1, TW_IN),
            pl.BlockSpec((1, D_MODEL, ZIN_W), lambda i: (layer, 0, 0)),
        ],
        out_specs=[tok(ZA_W), tok(ZS_W), tok(ZC_W), tok(ZB_W)] + [tok(D_MODEL)] * (len(x_args) - 1),
        out_shape=[
            jax.ShapeDtypeStruct((N_TOK, ZA_W), BF16),
            jax.ShapeDtypeStruct((N_TOK, ZS_W), F32),
            jax.ShapeDtypeStruct((N_TOK, ZC_W), BF16),
            jax.ShapeDtypeStruct((N_TOK, ZB_W), BF16),
        ] + [jax.ShapeDtypeStruct((N_TOK, D_MODEL), F32)] * (len(x_args) - 1),
        compiler_params=_cparams(("arbitrary",)),
        name="inproj",
    )(*x_args, norm1, mod, mod, w_in_r)


def _make_mlstm_kernel(seq, layer, has_init):
    nq = seq // TQ
    lane_if, lane_ff, lane_ib, lane_fb = (GATE_LANE0 + ML_HEADS * j for j in range(4))

    def kern(*refs):
        if has_init:
            body(*refs)
            return
        out = refs[10]
        b = pl.program_id(0)

        @pl.when(b < N_SEQ_CTX)
        def _():
            body(*refs)

        @pl.when(b >= N_SEQ_CTX)
        def _():
            out[...] = jnp.zeros_like(out)

    def body(*refs):
        if has_init:
            m0_ref, zq, zk, zv, zo, gz, bg, nrm, c0_ref, n0_ref, _, out, bp_scr, bs_scr = refs
        else:
            zq, zk, zv, zo, gz, bg, nrm, _, _, _, out, cf_ref, nf_ref, mf_ref, bp_scr, bs_scr = refs
        b = pl.program_id(0)
        g = gz[...] + bg[0]
        lane = lax.broadcasted_iota(jnp.int32, g.shape, 1)
        is_forget = ((lane >= lane_ff) & (lane < lane_ib)) | ((lane >= lane_fb) & (lane < lane_fb + ML_HEADS))
        log_sig = jnp.minimum(g, 0.0) - jnp.log1p(jnp.exp(-jnp.abs(g)))
        a = jnp.where(is_forget, log_sig, g)
        r_i = lax.broadcasted_iota(jnp.int32, (seq, seq), 0)
        c_i = lax.broadcasted_iota(jnp.int32, (seq, seq), 1)
        ltri = (c_i <= r_i).astype(F32)
        bp = jnp.dot(ltri, a, precision=HI, preferred_element_type=F32)
        bs = bp[seq - 1:seq, :] - bp + a
        bp_scr[...] = bp
        bs_scr[...] = bs
        eye = (lax.broadcasted_iota(jnp.int32, (LANES, LANES), 0)
               == lax.broadcasted_iota(jnp.int32, (LANES, LANES), 1)).astype(F32)
        tr = lambda x: lax.dot_general(eye, x, (((1,), (1,)), ((), ())), precision=HI, preferred_element_type=F32)
        if has_init:
            tr = lambda x: x.T
        a_t, bp_t, bs_t = tr(a), tr(bp), tr(bs)

        for h in range(ML_HEADS):
            hs = slice(h * ML_DIM, (h + 1) * ML_DIM)
            k = zk[:, hs]
            v = zv[:, hs]
            first_lane = lax.broadcasted_iota(jnp.int32, (seq, ML_DIM), 1) == 0
            v_aug = jnp.concatenate([v, jnp.where(first_lane, 1.0, 0.0).astype(BF16)], axis=1)
            rows = (
                a_t[lane_if + h:lane_if + h + 1, :] - bp_t[lane_ff + h:lane_ff + h + 1, :],
                a_t[lane_ib + h:lane_ib + h + 1, :] - bs_t[lane_fb + h:lane_fb + h + 1, :],
            )
            col_refs = ((bp_scr, lane_ff + h), (bs_scr, lane_fb + h))
            if has_init:
                m0 = tuple(m0_ref[((b * DEPTH + layer) * 2 + dr) * ML_HEADS + h] for dr in range(2))
                c0 = tuple(c0_ref[0, 0, dr, h].astype(BF16) for dr in range(2))
                n0 = tuple(jnp.broadcast_to(n0_ref[0, 0, dr, h:h + 1, :], (ML_DIM, ML_DIM)).astype(BF16) for dr in range(2))
            else:
                m0 = (0.0, 0.0)

            def qblock(qi, carry):
                q0 = pl.multiple_of(qi * TQ, TQ)
                qb = zq[pl.ds(q0, TQ), hs]
                sc = _dot_nt(qb, k)
                t_idx = q0 + lax.broadcasted_iota(jnp.int32, (TQ, seq), 0)
                s_idx = lax.broadcasted_iota(jnp.int32, (TQ, seq), 1)
                hsum = jnp.zeros((TQ, ML_DIM), F32)
                for dr in range(2):
                    cref, cl = col_refs[dr]
                    col = cref[pl.ds(q0, TQ), cl:cl + 1]
                    mask = (s_idx <= t_idx) if dr == 0 else (s_idx >= t_idx)
                    drow = jnp.where(mask, rows[dr], -jnp.inf)
                    c_t = jnp.maximum(m0[dr], jnp.max(drow, axis=1, keepdims=True))
                    s = sc * jnp.exp(drow - c_t)
                    na = _dot(s.astype(BF16), v_aug)
                    num, den = na[:, 0:ML_DIM], na[:, ML_DIM:ML_DIM + 1]
                    if has_init:
                        w_c = jnp.exp(m0[dr] - c_t)
                        num = num + w_c * _dot(qb, c0[dr])
                        den = den + w_c * _dot_nt(qb, n0[dr])[:, 0:1]
                    hsum = hsum + num / jnp.maximum(jnp.abs(den), jnp.exp(-(col + c_t)))
                hn = _rms(hsum, nrm[0][:, hs])
                og = zo[pl.ds(q0, TQ), hs].astype(F32)
                out[pl.ds(q0, TQ), hs] = (hn * jax.nn.sigmoid(og)).astype(out.dtype)
                return carry

            if nq == 1:
                qblock(0, 0)
            else:
                lax.fori_loop(0, nq, qblock, 0)

            if not has_init:
                k_t = _dot_nt(eye.astype(BF16), k)
                kf = k.astype(F32)
                tot = (bp_t[lane_ff + h:lane_ff + h + 1, seq - 1:seq], bp_t[lane_fb + h:lane_fb + h + 1, seq - 1:seq])
                gl = (
                    tot[0] + rows[0],
                    bp_t[lane_fb + h:lane_fb + h + 1, :] - a_t[lane_fb + h:lane_fb + h + 1, :]
                    + a_t[lane_ib + h:lane_ib + h + 1, :],
                )
                for dr in range(2):
                    m_new = jnp.maximum(tot[dr] + m0[dr], jnp.max(gl[dr], axis=1, keepdims=True))
                    w_s = jnp.exp(gl[dr] - m_new)
                    cf_ref[0, 0, dr, h] = _dot((k_t * w_s).astype(BF16), v)
                    n_new = jnp.dot(jnp.broadcast_to(w_s, (SUBLANES, seq)), kf, precision=HI, preferred_element_type=F32)
                    nf_ref[0, 0, dr, h:h + 1, :] = n_new[0:1, :]
                    mf_ref[0, 0, dr * ML_HEADS + h:dr * ML_HEADS + h + 1, :] = jnp.broadcast_to(m_new, (1, LANES))

    return kern


def _mlstm(layer, za, zs, b_gates, mlstm_norm, init=None, ctx_out=None, states=None):
    has_init = init is not None
    seq, nseq, row0 = (S_SMP, N_SEQ_SMP, N_CTX // S_SMP) if has_init else (S_CTX, N_SEQ_CTX, 0)
    qkvo = [pl.BlockSpec((seq, ML_WIDTH), functools.partial(lambda j, b: (row0 + b, j), j)) for j in range(4)]
    in_specs = qkvo + [
        pl.BlockSpec((seq, LANES), lambda b: (row0 + b, ZS_W // LANES - 1)),
        pl.BlockSpec((1, 1, LANES), lambda b: (layer, 0, 0)),
        pl.BlockSpec((1, 1, ML_WIDTH), lambda b: (layer, 0, 0)),
    ]
    args = [za, za, za, za, zs, b_gates, mlstm_norm]
    out_specs = [pl.BlockSpec((seq, ML_WIDTH), lambda b: (row0 + b, 0))]
    out_shape = [jax.ShapeDtypeStruct((N_TOK, ML_WIDTH), BF16)]
    aliases = {}
    if has_init:
        st_c, st_n, st_m = init
        in_specs = [pl.BlockSpec(memory_space=pltpu.SMEM)] + in_specs + [
            pl.BlockSpec((1, 1, 2, ML_HEADS, ML_DIM, ML_DIM), lambda b: (b, layer, 0, 0, 0, 0)),
            pl.BlockSpec((1, 1, 2, ML_HEADS, ML_DIM), lambda b: (b, layer, 0, 0, 0)),
            pl.BlockSpec(memory_space=pl.ANY),
        ]
        args = [st_m.reshape(-1)] + args + [st_c, st_n, ctx_out]
        aliases = {len(args) - 1: 0}
    else:
        seq_blk = lambda b: jnp.minimum(b, nseq - 1)
        in_specs += [pl.BlockSpec(memory_space=pl.ANY)] * 3
        args += list(states)
        aliases = {len(args) - 3 + j: 1 + j for j in range(3)}
        out_specs += [
            pl.BlockSpec((1, 1, 2, ML_HEADS, ML_DIM, ML_DIM), lambda b: (seq_blk(b), layer, 0, 0, 0, 0)),
            pl.BlockSpec((1, 1, 2, ML_HEADS, ML_DIM), lambda b: (seq_blk(b), layer, 0, 0, 0)),
            pl.BlockSpec((1, 1, 2 * ML_HEADS, LANES), lambda b: (seq_blk(b), layer, 0, 0)),
        ]
        out_shape += [jax.ShapeDtypeStruct(s.shape, s.dtype) for s in states]
    return pl.pallas_call(
        _make_mlstm_kernel(seq, layer, has_init),
        grid=(nseq if has_init else N_SEQ_BLOCKS,),
        in_specs=in_specs,
        out_specs=out_specs,
        out_shape=out_shape,
        scratch_shapes=[pltpu.VMEM((seq, LANES), F32), pltpu.VMEM((seq, LANES), F32)],
        input_output_aliases=aliases,
        compiler_params=_cparams(("arbitrary",)),
        name="mlstm_smp" if has_init else "mlstm_ctx",
    )(*args)


def _rope(x, cos, sin_a, sin_b):
    return x * cos + pltpu.roll(x, LANES - 8, 1) * sin_a + pltpu.roll(x, 8, 1) * sin_b


def _mla_prep_kernel(zs_ref, qa_ref, kva_ref, wuq_ref, wk_ref, wv_ref, qn_ref, kn_ref, cos_ref, sa_ref, sb_ref, _, __,
                     q_ref, kk_ref, v_ref, ckv_ref, kr_ref, qf_scr, kf_scr):
    cq = zs_ref[:, 0:MLA_Q_RANK]
    ckv = zs_ref[:, MLA_Q_RANK:MLA_Q_RANK + MLA_KV_RANK]
    last = zs_ref[:, ZS_W - LANES:ZS_W]
    qf_scr[...] = _dot(_rms(cq, qa_ref[0]).astype(BF16), wuq_ref[0])
    ckvn = _rms(ckv, kva_ref[0])

    @pl.when(pl.program_id(0) < N_TILES_CTX)
    def _():
        for j in range(TW // S_CTX):
            ckv_ref[j, 0] = ckvn[j * S_CTX:(j + 1) * S_CTX, :]
            kr_ref[j, 0] = last[j * S_CTX:(j + 1) * S_CTX, :].T[0:MLA_ROPE, :]

    cb = ckvn.astype(BF16)
    kf_scr[...] = _dot(cb, wk_ref[0])
    v_ref[...] = _dot(cb, wv_ref[0]).astype(BF16)
    lane = lax.broadcasted_iota(jnp.int32, last.shape, 1)
    kr = jnp.where((lane >= MLA_NOPE) & (lane < MLA_QK), pltpu.roll(last, MLA_NOPE, 1), 0.0)
    is_latent = pl.program_id(0) >= N_TILES_CTX

    def heads(rotate):
        for h in range(MLA_HEADS):
            hs = slice(h * HEAD_PAD, (h + 1) * HEAD_PAD)
            q_ref[:, hs] = rotate(_rms(qf_scr[:, hs], qn_ref[0], n=MLA_QK)).astype(BF16)
            kk_ref[:, hs] = rotate(_rms(kf_scr[:, hs] + kr, kn_ref[0], n=MLA_QK)).astype(BF16)

    @pl.when(is_latent)
    def _():
        cos, sa, sb = cos_ref[...], sa_ref[...], sb_ref[...]
        heads(lambda x: _rope(x, cos, sa, sb))

    @pl.when(jnp.logical_not(is_latent))
    def _():
        heads(lambda x: x)


def _mla_prep(layer, zs, q_a_norm, kv_a_norm, w_uq_r, w_k_r, w_v_r, q_norm_p, k_norm_p, rope_tabs, new_ckv, new_kr):
    seq_blk = lambda r, c: pl.BlockSpec((TW // S_CTX, 1, r, c), lambda i: (jnp.minimum(i, N_TILES_CTX - 1), layer, 0, 0))
    lw = lambda shape: pl.BlockSpec((1,) + shape, lambda i: (layer,) + (0,) * len(shape))
    tab = pl.BlockSpec((TW, LANES), lambda i: (jnp.where(i < N_TILES_CTX, 0, 1 + (i - N_TILES_CTX) % TILES_PER_SMP_SEQ), 0))
    tok = lambda w: pl.BlockSpec((TW, w), lambda i: (i, 0))
    return pl.pallas_call(
        _mla_prep_kernel,
        grid=(N_TILES,),
        in_specs=[
            tok(ZS_W), lw((1, MLA_Q_RANK)), lw((1, MLA_KV_RANK)),
            lw((MLA_Q_RANK, MLA_HEADS * HEAD_PAD)), lw((MLA_KV_RANK, MLA_HEADS * HEAD_PAD)),
            lw((MLA_KV_RANK, MLA_HEADS * MLA_V)), lw((1, HEAD_PAD)), lw((1, HEAD_PAD)), tab, tab, tab,
            pl.BlockSpec(memory_space=pl.ANY), pl.BlockSpec(memory_space=pl.ANY),
        ],
        out_specs=[tok(MLA_HEADS * HEAD_PAD), tok(MLA_HEADS * HEAD_PAD), tok(MLA_HEADS * MLA_V),
                   seq_blk(S_CTX, MLA_KV_RANK), seq_blk(MLA_ROPE, S_CTX)],
        input_output_aliases={11: 3, 12: 4},
        out_shape=[
            jax.ShapeDtypeStruct((N_TOK, MLA_HEADS * HEAD_PAD), BF16),
            jax.ShapeDtypeStruct((N_TOK, MLA_HEADS * HEAD_PAD), BF16),
            jax.ShapeDtypeStruct((N_TOK, MLA_HEADS * MLA_V), BF16),
            jax.ShapeDtypeStruct(new_ckv.shape, F32),
            jax.ShapeDtypeStruct(new_kr.shape, F32),
        ],
        scratch_shapes=[pltpu.VMEM((TW, MLA_HEADS * HEAD_PAD), F32), pltpu.VMEM((TW, MLA_HEADS * HEAD_PAD), F32)],
        compiler_params=_cparams(("arbitrary",)),
        name="mla_prep",
    )(zs, q_a_norm, kv_a_norm, w_uq_r, w_k_r, w_v_r, q_norm_p, k_norm_p, *rope_tabs, new_ckv, new_kr)


def _cache_kv_kernel(ckv_ref, kr_ref, wk_ref, wv_ref, kn_ref, kk_ref, v_ref):
    cb = ckv_ref[...].astype(BF16)
    kf = _dot(cb, wk_ref[0])
    v_ref[...] = _dot(cb, wv_ref[0]).astype(BF16)
    kr = kr_ref[...]
    for h in range(MLA_HEADS):
        hs = slice(h * HEAD_PAD, (h + 1) * HEAD_PAD)
        kk_ref[:, hs] = _rms(kf[:, hs] + kr, kn_ref[0], n=MLA_QK).astype(BF16)


def _cache_kv(cache_ckv, cache_kr_pad, w_k_r, w_v_r, k_norm_p):
    lw = lambda shape: pl.BlockSpec((1,) + shape, lambda b, l: (l,) + (0,) * len(shape))
    blk = lambda w: pl.BlockSpec((None, None, PAST_LEN, w), lambda b, l: (b, l, 0, 0))
    return pl.pallas_call(
        _cache_kv_kernel,
        grid=(N_SEQ_SMP, DEPTH),
        in_specs=[blk(MLA_KV_RANK), blk(LANES), lw((MLA_KV_RANK, MLA_HEADS * HEAD_PAD)),
                  lw((MLA_KV_RANK, MLA_HEADS * MLA_V)), lw((1, HEAD_PAD))],
        out_specs=[blk(MLA_HEADS * HEAD_PAD), blk(MLA_HEADS * MLA_V)],
        out_shape=[
            jax.ShapeDtypeStruct((N_SEQ_SMP, DEPTH, PAST_LEN, MLA_HEADS * HEAD_PAD), BF16),
            jax.ShapeDtypeStruct((N_SEQ_SMP, DEPTH, PAST_LEN, MLA_HEADS * MLA_V), BF16),
        ],
        compiler_params=_cparams(("arbitrary", "arbitrary")),
        name="cache_kv",
    )(cache_ckv, cache_kr_pad, w_k_r, w_v_r, k_norm_p)


def _make_attn_kernel(n_src):
    scale = MLA_QK ** -0.5

    def kern(q_ref, *refs):
        o_ref = refs[-1]
        if n_src > 1:
            body(q_ref, *refs)
            return

        @pl.when(pl.program_id(0) < N_SEQ_CTX)
        def _():
            body(q_ref, *refs)

        @pl.when(pl.program_id(0) >= N_SEQ_CTX)
        def _():
            o_ref[...] = jnp.zeros_like(o_ref)

    def body(q_ref, *refs):
        o_ref = refs[-1]
        for h in range(MLA_HEADS):
            hs = slice(h * HEAD_PAD, (h + 1) * HEAD_PAD)
            vs = slice(h * MLA_V, (h + 1) * MLA_V)
            q = q_ref[:, hs]
            ss = [_dot_nt(q, refs[2 * j][:, hs]) * scale for j in range(n_src)]
            m = functools.reduce(jnp.maximum, [jnp.max(s, axis=1, keepdims=True) for s in ss])
            ps = [jnp.exp(s - m) for s in ss]
            l = functools.reduce(jnp.add, [jnp.sum(p, axis=1, keepdims=True) for p in ps])
            o = functools.reduce(jnp.add, [_dot(ps[j].astype(BF16), refs[2 * j + 1][:, vs]) for j in range(n_src)])
            o_ref[:, vs] = (o / l).astype(o_ref.dtype)

    return kern


def _attn_ctx(q, kk, v):
    blk = lambda w: pl.BlockSpec((S_CTX, w), lambda b: (b, 0))
    return pl.pallas_call(
        _make_attn_kernel(1),
        grid=(N_SEQ_BLOCKS,),
        in_specs=[blk(MLA_HEADS * HEAD_PAD), blk(MLA_HEADS * HEAD_PAD), blk(MLA_HEADS * MLA_V)],
        out_specs=blk(MLA_HEADS * MLA_V),
        out_shape=jax.ShapeDtypeStruct((N_TOK, MLA_HEADS * MLA_V), BF16),
        compiler_params=_cparams(("arbitrary",)),
        name="attn_ctx",
    )(q, kk, v)


def _attn_smp(layer, q, kk, v, kk_cache, v_cache, ctx_out):
    row0 = N_CTX // S_SMP
    nq = S_SMP // TQ
    seqb = lambda w: pl.BlockSpec((S_SMP, w), lambda b, i: (row0 + b, 0))
    cache = lambda w: pl.BlockSpec((None, None, PAST_LEN, w), lambda b, i: (b, layer, 0, 0))
    return pl.pallas_call(
        _make_attn_kernel(2),
        grid=(N_SEQ_SMP, nq),
        in_specs=[
            pl.BlockSpec((TQ, MLA_HEADS * HEAD_PAD), lambda b, i: (N_CTX // TQ + b * nq + i, 0)),
            seqb(MLA_HEADS * HEAD_PAD), seqb(MLA_HEADS * MLA_V),
            cache(MLA_HEADS * HEAD_PAD), cache(MLA_HEADS * MLA_V),
            pl.BlockSpec(memory_space=pl.ANY),
        ],
        out_specs=pl.BlockSpec((TQ, MLA_HEADS * MLA_V), lambda b, i: (N_CTX // TQ + b * nq + i, 0)),
        out_shape=jax.ShapeDtypeStruct((N_TOK, MLA_HEADS * MLA_V), BF16),
        input_output_aliases={5: 0},
        compiler_params=_cparams(("arbitrary", "arbitrary")),
        name="attn_smp",
    )(q, kk, v, kk_cache, v_cache, ctx_out)


def _merge_kernel(x_ref, oa_ref, ob_ref, zc_ref, zb_ref, sgn_ref, ws_ref, bs_ref, wb_ref, wo_ref, g1_ref, n2_ref,
                  sh2_ref, sc2_ref, wr_ref, br_ref,
                  xmid_ref, h2_ref, dest_ref, wsel_ref, cnt_ref, oc_scr, carry_scr):
    i = pl.program_id(0)

    @pl.when(i == 0)
    def _():
        carry_scr[...] = jnp.zeros_like(carry_scr)

    u = _gelu(zc_ref[:, 0:SG_WIDTH].astype(F32))
    vg = _gelu(zc_ref[:, SG_WIDTH:2 * SG_WIDTH].astype(F32))
    for g in range(SG_GROUPS):
        gs = slice(g * SG_DIM, (g + 1) * SG_DIM)
        vn = _rms(vg[:, gs], sgn_ref[0][:, gs]).astype(BF16)
        for c in range(TW // SG_CHUNK):
            cs = slice(c * SG_CHUNK, (c + 1) * SG_CHUNK)
            mixed = _dot(ws_ref[0, g], vn[cs, :]) + bs_ref[0][:, g:g + 1]
            oc_scr[cs, gs] = (u[cs, gs] * mixed).astype(BF16)

    acc = jnp.zeros((TW, D_MODEL), F32)
    for j, src in enumerate((oa_ref, ob_ref, oc_scr)):
        gate = jax.nn.sigmoid(zb_ref[:, j * D_MODEL:(j + 1) * D_MODEL].astype(F32))
        acc = acc + gate * _dot(src[...], wb_ref[0, j])
    xm = x_ref[...] + g1_ref[0, 0] * _dot(acc.astype(BF16), wo_ref[0])
    xmid_ref[...] = xm
    h2 = _rms(xm, n2_ref[0]) * (1.0 + sc2_ref[0, 0]) + sh2_ref[0, 0]
    for c in range(SC_SPLIT):
        h2_ref[c] = _pack_pairs(h2[:, 2 * c * SC_ROW:(2 * c + 1) * SC_ROW], h2[:, (2 * c + 1) * SC_ROW:(2 * c + 2) * SC_ROW])

    h_hi = h2.astype(BF16)
    h_lo = (h2 - h_hi.astype(F32)).astype(BF16)
    p_hi = _dot(h_hi, wr_ref[0])
    logits = p_hi[:, 0:LANES] + p_hi[:, LANES:2 * LANES] + _dot(h_lo, wr_ref[0, :, 0:LANES]) + br_ref[0]
    lane = lax.broadcasted_iota(jnp.int32, logits.shape, 1)
    hits, exps = [], []
    sel = jnp.zeros(logits.shape, F32)
    denom = jnp.zeros((TW, 1), F32)
    top = None
    for _ in range(TOP_K):
        m = jnp.max(logits, axis=1, keepdims=True)
        idx = jnp.min(jnp.where(logits == m, lane, LANES), axis=1, keepdims=True)
        hit = lane == idx
        top = m if top is None else top
        hits.append(hit)
        exps.append(jnp.exp(m - top))
        sel = jnp.where(hit, 1.0, sel)
        denom = denom + exps[-1]
        logits = jnp.where(hit, -jnp.inf, logits)

    r_i = lax.broadcasted_iota(jnp.int32, (TW, TW), 0)
    c_i = lax.broadcasted_iota(jnp.int32, (TW, TW), 1)
    carry = carry_scr[0:1, :]
    rank = _dot(jnp.where(c_i < r_i, 1.0, 0.0).astype(BF16), sel.astype(BF16)) + carry
    new_carry = carry + jnp.sum(sel, axis=0, keepdims=True)
    carry_scr[...] = jnp.broadcast_to(new_carry, (SUBLANES, LANES))
    cnt_ref[...] = jnp.broadcast_to(new_carry, (SUBLANES, LANES))
    slot = rank + lane.astype(F32) * float(EXPERT_CAP)
    dmat = jnp.zeros(logits.shape, F32)
    wmat = jnp.zeros(logits.shape, F32)
    for k in range(TOP_K):
        dk = jnp.sum(jnp.where(hits[k], slot, 0.0), axis=1, keepdims=True)
        dmat = jnp.where(lane == k, dk, dmat)
        wmat = jnp.where(lane == k, exps[k] / denom, wmat)
    dest_ref[...] = dmat.T[0:SUBLANES, :].astype(jnp.int32)
    wsel_ref[...] = wmat


def _merge(layer, x, oa, ob, zc, zb, sg_norm, w_sp, b_sp, w_branch, w_out, mod, norm2, w_router_p, b_router_p):
    lw = lambda shape: pl.BlockSpec((1,) + shape, lambda i: (layer,) + (0,) * len(shape))
    tok = lambda w: pl.BlockSpec((TW, w), lambda i: (i, 0))
    return pl.pallas_call(
        _merge_kernel,
        grid=(N_TILES,),
        in_specs=[
            tok(D_MODEL), tok(ML_WIDTH), tok(MLA_HEADS * MLA_V), tok(ZC_W), tok(ZB_W),
            lw((1, SG_WIDTH)), lw((SG_GROUPS, SG_CHUNK, SG_CHUNK)), lw((SG_CHUNK, LANES)),
            lw((N_BRANCH, ML_WIDTH, D_MODEL)), lw((D_MODEL, D_MODEL)),
            _mod_spec(layer, 2), lw((1, D_MODEL)), _mod_spec(layer, 3), _mod_spec(layer, 4),
            lw((D_MODEL, 2 * LANES)), lw((1, LANES)),
        ],
        out_specs=[tok(D_MODEL), pl.BlockSpec((SC_SPLIT, TW, SC_ROW), lambda i: (0, i, 0)),
                   pl.BlockSpec((SUBLANES, TW), lambda i: (0, i)), tok(LANES),
                   pl.BlockSpec((SUBLANES, LANES), lambda i: (0, 0))],
        out_shape=[
            jax.ShapeDtypeStruct((N_TOK, D_MODEL), F32),
            jax.ShapeDtypeStruct((SC_SPLIT, N_TOK, SC_ROW), jnp.uint32),
            jax.ShapeDtypeStruct((SUBLANES, N_TOK), jnp.int32),
            jax.ShapeDtypeStruct((N_TOK, LANES), F32),
            jax.ShapeDtypeStruct((SUBLANES, LANES), F32),
        ],
        scratch_shapes=[pltpu.VMEM((TW, SG_WIDTH), BF16), pltpu.VMEM((SUBLANES, LANES), F32)],
        compiler_params=_cparams(("arbitrary",)),
        name="merge_router",
    )(x, oa, ob, zc, zb, sg_norm, w_sp, b_sp, w_branch, w_out, mod, norm2, mod, mod, w_router_p, b_router_p)


def _sc_mesh():
    return plsc.VectorSubcoreMesh(core_axis_name="core", subcore_axis_name="subcore")


def _sc_scatter_rows(x, idxs, n_rows):
    @pl.kernel(out_type=jax.ShapeDtypeStruct((n_rows, SC_ROW), x.dtype), mesh=_sc_mesh(), scratch_types=[])
    def scatter(x_hbm, *refs):
        o_hbm = refs[-1]

        def body(x_vmem, *i_vmems):
            for i_vmem in i_vmems:
                pltpu.sync_copy(x_vmem, o_hbm.at[i_vmem.at[0]])

        pltpu.emit_pipeline(
            body,
            grid=(x.shape[0] // SC_WIN,),
            in_specs=[pl.BlockSpec((SC_WIN, SC_ROW), lambda i: (i, 0))]
            + [pl.BlockSpec((1, SC_WIN), lambda i: (0, i))] * len(idxs),
            out_specs=[],
            core_axis_name=("core", "subcore"),
            dimension_semantics=(pltpu.PARALLEL,),
        )(x_hbm, *refs[:-1])

    return scatter(x, *idxs)


def _sc_gather_rows(x, idx):
    m = idx.shape[1]

    @pl.kernel(out_type=jax.ShapeDtypeStruct((m, SC_ROW), x.dtype), mesh=_sc_mesh())
    def gather(x_hbm, i_hbm, o_hbm):
        def body(i_vmem, o_vmem):
            pltpu.sync_copy(x_hbm.at[i_vmem.at[0]], o_vmem)

        pltpu.emit_pipeline(
            body,
            grid=(m // SC_WIN,),
            in_specs=[pl.BlockSpec((1, SC_WIN), lambda i: (0, i))],
            out_specs=[pl.BlockSpec((SC_WIN, SC_ROW), lambda i: (i, 0))],
            core_axis_name=("core", "subcore"),
            dimension_semantics=(pltpu.PARALLEL,),
        )(i_hbm, o_hbm)

    return gather(x, idx)


STEP_VALID, STEP_FIRST, STEP_HAS_NEXT = 1, 2, 4
STEP_CHUNKS_SHIFT = 3


def _moe_ffn_kernel(layer, be_ref, nx_ref, br_ref, fl_ref, xs_ref, b1_ref, b2_ref, w1_hbm, w2_hbm, y_ref,
                    w1f, w2f, w1b, w2b, sem):
    g = pl.program_id(0)
    flags = fl_ref[g]

    def weight_copies(e):
        return (pltpu.make_async_copy(w1_hbm.at[layer, e], w1f, sem.at[0]),
                pltpu.make_async_copy(w2_hbm.at[layer, e], w2f, sem.at[1]))

    @pl.when(g == 0)
    def _():
        for cp in weight_copies(be_ref[0]):
            cp.start()

    @pl.when((flags & STEP_FIRST) != 0)
    def _():
        for cp in weight_copies(be_ref[g]):
            cp.wait()
        w1b[...] = w1f[...].astype(BF16)
        w2b[...] = w2f[...].astype(BF16)

        @pl.when((flags & STEP_HAS_NEXT) != 0)
        def _():
            for cp in weight_copies(nx_ref[g]):
                cp.start()

    def ffn(n_rows):
        halves = [h.astype(BF16) for c in range(SC_SPLIT) for h in _unpack_pairs(xs_ref[c, 0:n_rows, :])]
        g1 = _dot(jnp.concatenate(halves, axis=1), w1b[...]) + b1_ref[0, 0]
        gate = jnp.minimum(g1[:, :D_EXPERT], SWIGLU_LIMIT)
        up = jnp.clip(g1[:, D_EXPERT:], -SWIGLU_LIMIT, SWIGLU_LIMIT)
        act = gate * jax.nn.sigmoid(SWIGLU_ALPHA * gate) * (up + 1.0)
        y = _dot(act.astype(BF16), w2b[...]) + b2_ref[0, 0]
        for c in range(SC_SPLIT):
            y_ref[c, 0:n_rows, :] = _pack_pairs(
                y[:, 2 * c * SC_ROW:(2 * c + 1) * SC_ROW], y[:, (2 * c + 1) * SC_ROW:(2 * c + 2) * SC_ROW])

    for n_chunks in range(1, FFN_CHUNKS + 1):
        want = STEP_VALID | ((n_chunks - 1) << STEP_CHUNKS_SHIFT)
        pl.when((flags & (STEP_VALID | ((FFN_CHUNKS - 1) << STEP_CHUNKS_SHIFT))) == want)(
            functools.partial(ffn, n_chunks * SLOT_CHUNK))


def _moe_ffn(layer, xs, plan, w1, b1, w2, b2):
    eb = lambda c: pl.BlockSpec((1, 1, 1, c), lambda g, be, nx, br, fl: (layer, be[g], 0, 0))
    rows = pl.BlockSpec((SC_SPLIT, FFN_BLOCK, SC_ROW), lambda g, be, nx, br, fl: (0, br[g], 0))
    hbm = pl.BlockSpec(memory_space=pl.ANY)
    grid_spec = pltpu.PrefetchScalarGridSpec(
        num_scalar_prefetch=4,
        grid=(N_CHUNK_STEPS,),
        in_specs=[rows, eb(2 * D_EXPERT), eb(D_MODEL), hbm, hbm],
        out_specs=rows,
        scratch_shapes=[
            pltpu.VMEM((D_MODEL, 2 * D_EXPERT), F32), pltpu.VMEM((D_EXPERT, D_MODEL), F32),
            pltpu.VMEM((D_MODEL, 2 * D_EXPERT), BF16), pltpu.VMEM((D_EXPERT, D_MODEL), BF16),
            pltpu.SemaphoreType.DMA((2,)),
        ],
    )
    return pl.pallas_call(
        functools.partial(_moe_ffn_kernel, layer),
        grid_spec=grid_spec,
        out_shape=jax.ShapeDtypeStruct(xs.shape, xs.dtype),
        compiler_params=_cparams(("arbitrary",)),
        name="moe_ffn",
    )(*plan, xs, b1, b2, w1, w2)


def _chunk_plan_kernel(cnt_ref, be_ref, nx_ref, br_ref, fl_ref):
    def expert(e, carry):
        step0, prev_first = carry
        c = cnt_ref[e]
        n_blk = (c + FFN_BLOCK - 1) // FFN_BLOCK

        def block(j, _):
            s = step0 + j
            be_ref[s] = e
            nx_ref[s] = e
            br_ref[s] = e * (EXPERT_CAP // FFN_BLOCK) + j
            fl_ref[s] = (STEP_VALID + jnp.where(j == 0, STEP_FIRST, 0)
                         + (jnp.minimum((c - j * FFN_BLOCK + SLOT_CHUNK - 1) // SLOT_CHUNK, FFN_CHUNKS) - 1)
                         * (1 << STEP_CHUNKS_SHIFT))
            return 0

        lax.fori_loop(0, n_blk, block, 0)

        @pl.when((n_blk > 0) & (prev_first >= 0))
        def _():
            nx_ref[prev_first] = e
            fl_ref[prev_first] = fl_ref[prev_first] + STEP_HAS_NEXT

        return step0 + n_blk, jnp.where(n_blk > 0, step0, prev_first)

    used, _ = lax.fori_loop(0, N_EXPERTS, expert, (jnp.int32(0), jnp.int32(-1)))

    def idle(s, _):
        be_ref[s] = be_ref[used - 1]
        nx_ref[s] = be_ref[used - 1]
        br_ref[s] = br_ref[used - 1]
        fl_ref[s] = 0
        return 0

    lax.fori_loop(used, N_CHUNK_STEPS, idle, 0)


def _chunk_plan(cnt):
    smem = pl.BlockSpec(memory_space=pltpu.SMEM)
    return pl.pallas_call(
        _chunk_plan_kernel,
        in_specs=[smem],
        out_specs=[smem] * 4,
        out_shape=[jax.ShapeDtypeStruct((N_CHUNK_STEPS,), jnp.int32)] * 4,
        name="chunk_plan",
    )(cnt)


def _combine_kernel(x_ref, yg_ref, w_ref, g_ref, *o_refs):
    def emit(o_ref):
        w = w_ref[...]
        for c in range(SC_SPLIT):
            parts = [_unpack_pairs(yg_ref[k, c]) for k in range(TOP_K)]
            for half in range(2):
                cs = slice((2 * c + half) * SC_ROW, (2 * c + half + 1) * SC_ROW)
                acc = w[:, 0:1] * parts[0][half]
                for k in range(1, TOP_K):
                    acc = acc + w[:, k:k + 1] * parts[k][half]
                o_ref[:, cs] = x_ref[:, cs] + g_ref[0, 0][:, cs] * acc

    if len(o_refs) == 1:
        emit(o_refs[0])
    else:
        pl.when(pl.program_id(0) < N_TILES_CTX)(lambda: emit(o_refs[0]))
        pl.when(pl.program_id(0) >= N_TILES_CTX)(lambda: emit(o_refs[1]))


def _combine(layer, xmid, yg, wsel, mod, split_out):
    tok = lambda w: pl.BlockSpec((TW, w), lambda i: (i, 0))
    if split_out:
        out_specs = [pl.BlockSpec((TW, D_MODEL), lambda i: (jnp.minimum(i, N_TILES_CTX - 1), 0)),
                     pl.BlockSpec((TW, D_MODEL), lambda i: (jnp.maximum(i - N_TILES_CTX, 0), 0))]
        out_shape = [jax.ShapeDtypeStruct((N_CTX, D_MODEL), F32), jax.ShapeDtypeStruct((N_SMP, D_MODEL), F32)]
    else:
        out_specs, out_shape = tok(D_MODEL), jax.ShapeDtypeStruct((N_TOK, D_MODEL), F32)
    return pl.pallas_call(
        _combine_kernel,
        grid=(N_TILES,),
        in_specs=[tok(D_MODEL), pl.BlockSpec((TOP_K, SC_SPLIT, TW, SC_ROW), lambda i: (0, 0, i, 0)), tok(LANES),
                  _mod_spec(layer, 5)],
        out_specs=out_specs,
        out_shape=out_shape,
        compiler_params=_cparams(("arbitrary",)),
        name="combine",
    )(xmid, yg, wsel, mod)


def _moe(layer, xmid, h2, dest, wsel, cnt, mod, w1, b1, w2, b2, split_out):
    n_slots = N_EXPERTS * EXPERT_CAP
    idx = dest[0:TOP_K][:, None, :] + (jnp.arange(SC_SPLIT, dtype=jnp.int32) * n_slots)[None, :, None]
    idx = idx.reshape(TOP_K, 1, SC_SPLIT * N_TOK)
    xs = _sc_scatter_rows(h2.reshape(SC_SPLIT * N_TOK, SC_ROW), [idx[k] for k in range(TOP_K)], SC_SPLIT * n_slots)
    plan = _chunk_plan(cnt[0, :N_EXPERTS].astype(jnp.int32))
    y = _moe_ffn(layer, xs.reshape(SC_SPLIT, n_slots, SC_ROW), plan, w1, b1, w2, b2)
    yg = _sc_gather_rows(y.reshape(SC_SPLIT * n_slots, SC_ROW), idx.reshape(1, TOP_K * SC_SPLIT * N_TOK))
    return _combine(layer, xmid, yg.reshape(TOP_K, SC_SPLIT, N_TOK, SC_ROW), wsel, mod, split_out)


def _rope_tables():
    pos = np.arange(S_SMP)
    half = MLA_ROPE // 2
    inv_freq = (ROPE_THETA ** (-(np.arange(0, half, 2, dtype=np.float32) / np.float32(half)))).astype(np.float32)
    angs = [((pos // GRID_W).astype(np.float32)[:, None] * inv_freq[None, :]).astype(np.float32),
            ((pos % GRID_W).astype(np.float32)[:, None] * inv_freq[None, :]).astype(np.float32)]
    nf = half // 2
    cos = np.ones((TW + S_SMP, LANES), np.float32)
    sin_a = np.zeros((TW + S_SMP, LANES), np.float32)
    sin_b = np.zeros((TW + S_SMP, LANES), np.float32)
    for axis, ang in enumerate(angs):
        base = MLA_NOPE + axis * half
        c, s = np.cos(ang.astype(np.float64)), np.sin(ang.astype(np.float64))
        cos[TW:, base:base + nf] = c
        cos[TW:, base + nf:base + half] = c
        sin_a[TW:, base:base + nf] = -s
        sin_b[TW:, base + nf:base + half] = s
    return jnp.asarray(cos), jnp.asarray(sin_a), jnp.asarray(sin_b)


def _pad_last(a, width):
    return jnp.pad(a, [(0, 0)] * (a.ndim - 1) + [(0, width - a.shape[-1])])


def kernel(x_prompt, x_sample, cache_mla_ckv, cache_mla_krope, state_mlstm_C, state_mlstm_n, state_mlstm_m, c, c_ctx, norm1, norm2, w_ada, b_ada, w_in, b_mlstm_gates, mlstm_norm, mla_q_a_norm, mla_kv_a_norm, w_uq, w_ukv, mla_q_norm, mla_k_norm, sg_norm, w_spatial, b_spatial, w_branch, w_out, w_router, b_router, w_exp1, b_exp1, w_exp2, b_exp2):
    w_in_r = _w_in_prep(w_in)
    w_uq_r = _pad_last(w_uq.reshape(DEPTH, MLA_Q_RANK, MLA_HEADS, MLA_QK), HEAD_PAD).reshape(
        DEPTH, MLA_Q_RANK, MLA_HEADS * HEAD_PAD).astype(BF16)
    w_ukv4 = w_ukv.reshape(DEPTH, MLA_KV_RANK, MLA_HEADS, MLA_NOPE + MLA_V)
    w_k_r = _pad_last(w_ukv4[..., :MLA_NOPE], HEAD_PAD).reshape(DEPTH, MLA_KV_RANK, MLA_HEADS * HEAD_PAD).astype(BF16)
    w_v_r = w_ukv4[..., MLA_NOPE:].reshape(DEPTH, MLA_KV_RANK, MLA_HEADS * MLA_V).astype(BF16)
    q_norm_p = _pad_last(mla_q_norm, HEAD_PAD).reshape(DEPTH, 1, HEAD_PAD)
    k_norm_p = _pad_last(mla_k_norm, HEAD_PAD).reshape(DEPTH, 1, HEAD_PAD)
    b_gates_p = jnp.pad(b_mlstm_gates, ((0, 0), (GATE_LANE0, LANES - GATE_LANE0 - 4 * ML_HEADS))).reshape(DEPTH, 1, LANES)
    b_sp = _pad_last(jnp.swapaxes(b_spatial, 1, 2), LANES)
    w_router_p = _pad_last(w_router, LANES)
    w_router_hi = w_router_p.astype(BF16)
    w_router_p = jnp.concatenate([w_router_hi, (w_router_p - w_router_hi.astype(F32)).astype(BF16)], axis=-1)
    b_router_p = jnp.pad(b_router, ((0, 0), (0, LANES - N_EXPERTS)), constant_values=-1e30).reshape(DEPTH, 1, LANES)
    r3 = lambda a: a.reshape(DEPTH, 1, a.shape[-1])
    cache_kr_pad = jnp.pad(cache_mla_krope, ((0, 0), (0, 0), (0, 0), (MLA_NOPE, LANES - MLA_QK)))
    rope_tabs = _rope_tables()

    cvec = jnp.concatenate([c_ctx[None, :], c, jnp.zeros((SUBLANES - 1 - N_SEQ_SMP, D_MODEL), F32)], axis=0)
    mod = _adaln(cvec, w_ada, b_ada).reshape(DEPTH, SUBLANES, 1, 6 * D_MODEL)
    b1 = b_exp1.reshape(DEPTH, N_EXPERTS, 1, 2 * D_EXPERT)
    b2 = b_exp2.reshape(DEPTH, N_EXPERTS, 1, D_MODEL)
    kk_cache, v_cache = _cache_kv(cache_mla_ckv, cache_kr_pad, w_k_r, w_v_r, k_norm_p)

    x = (x_prompt.reshape(N_CTX, D_MODEL), x_sample.reshape(N_SMP, D_MODEL))
    new_ckv = jnp.zeros((N_SEQ_CTX, DEPTH, S_CTX, MLA_KV_RANK), F32)
    new_kr = jnp.zeros((N_SEQ_CTX, DEPTH, MLA_ROPE, S_CTX), F32)
    states = (jnp.zeros((N_SEQ_CTX, DEPTH, 2, ML_HEADS, ML_DIM, ML_DIM), F32),
              jnp.zeros((N_SEQ_CTX, DEPTH, 2, ML_HEADS, ML_DIM), F32),
              jnp.zeros((N_SEQ_CTX, DEPTH, 2 * ML_HEADS, LANES), F32))
    for l in range(DEPTH):
        za, zs, zc, zb, *x_joined = _inproj(l, x, r3(norm1), mod, w_in_r)
        x = x_joined[0] if x_joined else x
        oa, *states = _mlstm(l, za, zs, b_gates_p, r3(mlstm_norm), states=states)
        (oa,) = _mlstm(l, za, zs, b_gates_p, r3(mlstm_norm), init=(state_mlstm_C, state_mlstm_n, state_mlstm_m), ctx_out=oa)
        q, kk, v, new_ckv, new_kr = _mla_prep(l, zs, r3(mla_q_a_norm), r3(mla_kv_a_norm), w_uq_r, w_k_r, w_v_r,
                                              q_norm_p, k_norm_p, rope_tabs, new_ckv, new_kr)
        ob = _attn_smp(l, q, kk, v, kk_cache, v_cache, _attn_ctx(q, kk, v))
        xmid, h2, dest, wsel, cnt = _merge(
            l, x, oa, ob, zc, zb, r3(sg_norm), w_spatial.astype(BF16), b_sp, w_branch.astype(BF16), w_out.astype(BF16),
            mod, r3(norm2), w_router_p, b_router_p)
        x = _moe(l, xmid, h2, dest, wsel, cnt, mod, w_exp1, b1, w_exp2, b2, split_out=l == DEPTH - 1)
    y_ctx, y_smp = x
    return (
        y_ctx.reshape(N_SEQ_CTX, S_CTX, D_MODEL),
        y_smp.reshape(N_SEQ_SMP, S_SMP, D_MODEL),
        new_ckv,
        jnp.swapaxes(new_kr, 2, 3),
        states[0],
        states[1],
        states[2][:, :, :, 0].reshape(N_SEQ_CTX, DEPTH, 2, ML_HEADS),
    )
```

```python
import functools

import numpy as np
import jax
import jax.numpy as jnp
from jax import lax
from jax.experimental import pallas as pl
from jax.experimental.pallas import tpu as pltpu
from jax.experimental.pallas import tpu_sc as plsc

F32 = jnp.float32
BF16 = jnp.bfloat16
HI = lax.Precision.HIGHEST

D_MODEL = 1024
N_SEQ_CTX, S_CTX = 32, 256
N_SEQ_SMP, S_SMP = 2, 1024
DEPTH = 4
PAST_LEN = 512
GRID_W = 64
EPS = 1e-6
ML_HEADS, ML_DIM = 4, 128
ML_WIDTH = ML_HEADS * ML_DIM
MLA_HEADS, MLA_NOPE, MLA_ROPE, MLA_V = 8, 64, 32, 64
MLA_QK = MLA_NOPE + MLA_ROPE
MLA_Q_RANK, MLA_KV_RANK = 256, 128
ROPE_THETA = 10000.0
SG_GROUPS, SG_DIM, SG_CHUNK = 4, 128, 128
SG_WIDTH = SG_GROUPS * SG_DIM
N_BRANCH = 3
N_EXPERTS, TOP_K, D_EXPERT = 32, 4, 1024
SWIGLU_LIMIT, SWIGLU_ALPHA = 7.0, 1.702

N_CTX = N_SEQ_CTX * S_CTX
N_SMP = N_SEQ_SMP * S_SMP
N_TOK = N_CTX + N_SMP

LANES = 128
SUBLANES = 8
VMEM_LIMIT = 56 * 1024 * 1024

TW = 1024
TW_IN = 512
N_TILES = N_TOK // TW
N_TILES_CTX = N_CTX // TW
TILES_PER_SMP_SEQ = S_SMP // TW
N_SEQ_BLOCKS = N_TOK // S_CTX
HEAD_PAD = LANES
TQ = 256
EXPERT_CAP = N_TOK
SLOT_CHUNK = 128
FFN_CHUNKS = 8
FFN_BLOCK = FFN_CHUNKS * SLOT_CHUNK
N_CHUNK_STEPS = N_TOK * TOP_K // FFN_BLOCK + N_EXPERTS
SC_ROW = 256
SC_SPLIT = D_MODEL // (2 * SC_ROW)
SC_WIN = 128

ZA_W = 4 * ML_WIDTH
ZS_W = 512
ZC_W = 2 * SG_WIDTH
ZB_W = N_BRANCH * D_MODEL
ZIN_W = ZA_W + ZS_W + ZC_W + ZB_W
GATE_LANE0 = MLA_ROPE


def _cparams(sem):
    return pltpu.CompilerParams(dimension_semantics=sem, vmem_limit_bytes=VMEM_LIMIT)


def _mod_row(i, tile=None):
    tile = tile or TW
    return jnp.where(i < N_CTX // tile, 0, 1 + (i - N_CTX // tile) // (S_SMP // tile))


def _rms(x, g, n=None):
    ms = jnp.sum(x * x, axis=-1, keepdims=True) * (1.0 / (n or x.shape[-1]))
    return x * lax.rsqrt(ms + EPS) * g


def _gelu(x):
    return 0.5 * x * (1.0 + jnp.tanh(0.7978845608028654 * (x + 0.044715 * (x * x * x))))


def _pack_pairs(lo, hi):
    lo_bits = lax.bitcast_convert_type(lo.astype(BF16).astype(F32), jnp.uint32)
    hi_bits = lax.bitcast_convert_type(hi.astype(BF16).astype(F32), jnp.uint32)
    return (lo_bits >> 16) | (hi_bits & jnp.uint32(0xFFFF0000))


def _unpack_pairs(u):
    return (lax.bitcast_convert_type(u << 16, F32), lax.bitcast_convert_type(u & jnp.uint32(0xFFFF0000), F32))


def _dot(a, b):
    return jnp.dot(a, b, preferred_element_type=F32)


def _dot_nt(a, b):
    return lax.dot_general(a, b, (((1,), (1,)), ((), ())), preferred_element_type=F32)


def _adaln_kernel(c_ref, w_ref, b_ref, o_ref):
    c = c_ref[...]
    s = c * jax.nn.sigmoid(c)
    w = w_ref[0]
    s_hi, w_hi = s.astype(BF16), w.astype(BF16)
    s_lo, w_lo = (s - s_hi.astype(F32)).astype(BF16), (w - w_hi.astype(F32)).astype(BF16)
    o_ref[0] = _dot(s_hi, w_hi) + _dot(s_hi, w_lo) + _dot(s_lo, w_hi) + b_ref[0]


def _adaln(cvec, w_ada, b_ada):
    nchunk = 4
    cw = 6 * D_MODEL // nchunk
    return pl.pallas_call(
        _adaln_kernel,
        grid=(DEPTH, nchunk),
        in_specs=[
            pl.BlockSpec((SUBLANES, D_MODEL), lambda l, j: (0, 0)),
            pl.BlockSpec((1, D_MODEL, cw), lambda l, j: (l, 0, j)),
            pl.BlockSpec((1, 1, cw), lambda l, j: (l, 0, j)),
        ],
        out_specs=pl.BlockSpec((1, SUBLANES, cw), lambda l, j: (l, 0, j)),
        out_shape=jax.ShapeDtypeStruct((DEPTH, SUBLANES, 6 * D_MODEL), F32),
        compiler_params=_cparams(("arbitrary", "arbitrary")),
        name="adaln",
    )(cvec, w_ada, b_ada.reshape(DEPTH, 1, 6 * D_MODEL))


IN_SPLITS = (ML_WIDTH, ML_WIDTH, ML_WIDTH, ML_WIDTH, 4 * ML_HEADS, MLA_Q_RANK, MLA_KV_RANK, MLA_ROPE, SG_WIDTH, SG_WIDTH,
             N_BRANCH * D_MODEL)
IN_OFFS = tuple(int(v) for v in np.cumsum((0,) + IN_SPLITS))
D_IN = IN_OFFS[-1]
W_PREP_ROWS = 256
W_PREP_COLS = 512


def _w_in_prep_kernel(wt_ref, o_ref):
    o = IN_OFFS

    def put(c0, rows):
        o_ref[0, :, c0:c0 + W_PREP_COLS] = rows.T.astype(BF16)

    for c0 in range(0, ZA_W, W_PREP_COLS):
        scale = ML_DIM ** -0.5 if o[1] <= c0 < o[2] else 1.0
        put(c0, wt_ref[0, c0:c0 + W_PREP_COLS, :] * scale)
    pad = jnp.zeros((ZS_W - (o[8] - o[4]), W_PREP_ROWS), F32)
    put(ZA_W, jnp.concatenate([wt_ref[0, o[5]:o[8], :], wt_ref[0, o[4]:o[5], :], pad], axis=0))
    for c0 in range(ZA_W + ZS_W, ZIN_W, W_PREP_COLS):
        src = c0 - (ZA_W + ZS_W) + o[8]
        put(c0, wt_ref[0, src:src + W_PREP_COLS, :])


def _w_in_prep(w_in):
    return pl.pallas_call(
        _w_in_prep_kernel,
        grid=(DEPTH, D_MODEL // W_PREP_ROWS),
        in_specs=[pl.BlockSpec((1, D_IN, W_PREP_ROWS), lambda l, r: (l, 0, r))],
        out_specs=pl.BlockSpec((1, W_PREP_ROWS, ZIN_W), lambda l, r: (l, r, 0)),
        out_shape=jax.ShapeDtypeStruct((DEPTH, D_MODEL, ZIN_W), BF16),
        compiler_params=_cparams(("arbitrary", "arbitrary")),
        name="w_in_prep",
    )(jnp.swapaxes(w_in, 1, 2))


def _tok_specs(x, tile):
    if isinstance(x, tuple):
        n_ctx = N_CTX // tile
        return [pl.BlockSpec((tile, D_MODEL), lambda i: (jnp.minimum(i, n_ctx - 1), 0)),
                pl.BlockSpec((tile, D_MODEL), lambda i: (jnp.maximum(i - n_ctx, 0), 0))], list(x)
    return [pl.BlockSpec((tile, D_MODEL), lambda i: (i, 0))], [x]


def _tok_value(x_refs, tile):
    if len(x_refs) == 2:
        return jnp.where(pl.program_id(0) < N_CTX // tile, x_refs[0][...], x_refs[1][...])
    return x_refs[0][...]


def _inproj_kernel(n_x, *refs):
    g_ref, sh_ref, sc_ref, w_ref, za_ref, zs_ref, zc_ref, zb_ref = refs[n_x:n_x + 8]
    x = _tok_value(refs[:n_x], TW_IN)
    if n_x == 2:
        refs[n_x + 8][...] = x
    h = _rms(x, g_ref[0]) * (1.0 + sc_ref[0, 0]) + sh_ref[0, 0]
    hb = h.astype(BF16)
    za_ref[...] = _dot(hb, w_ref[0, :, 0:ZA_W]).astype(BF16)
    zs_ref[...] = _dot(hb, w_ref[0, :, ZA_W:ZA_W + ZS_W])
    zc_ref[...] = _dot(hb, w_ref[0, :, ZA_W + ZS_W:ZA_W + ZS_W + ZC_W]).astype(BF16)
    zb_ref[...] = _dot(hb, w_ref[0, :, ZA_W + ZS_W + ZC_W:ZIN_W]).astype(BF16)


def _mod_spec(layer, k, tile=None):
    return pl.BlockSpec((1, 1, 1, D_MODEL), lambda i: (layer, _mod_row(i, tile), 0, k))


def _inproj(layer, x, norm1, mod, w_in_r):
    tok = lambda w: pl.BlockSpec((TW_IN, w), lambda i: (i, 0))
    x_specs, x_args = _tok_specs(x, TW_IN)
    return pl.pallas_call(
        functools.partial(_inproj_kernel, len(x_args)),
        grid=(N_TOK // TW_IN,),
        in_specs=x_specs + [
            pl.BlockSpec((1, 1, D_MODEL), lambda i: (layer, 0, 0)),
            _mod_spec(layer, 0, TW_IN),
            _mod_spec(layer, 1, TW_IN),
            pl.BlockSpec((1, D_MODEL, ZIN_W), lambda i: (layer, 0, 0)),
        ],
        out_specs=[tok(ZA_W), tok(ZS_W), tok(ZC_W), tok(ZB_W)] + [tok(D_MODEL)] * (len(x_args) - 1),
        out_shape=[
            jax.ShapeDtypeStruct((N_TOK, ZA_W), BF16),
            jax.ShapeDtypeStruct((N_TOK, ZS_W), F32),
            jax.ShapeDtypeStruct((N_TOK, ZC_W), BF16),
            jax.ShapeDtypeStruct((N_TOK, ZB_W), BF16),
        ] + [jax.ShapeDtypeStruct((N_TOK, D_MODEL), F32)] * (len(x_args) - 1),
        compiler_params=_cparams(("arbitrary",)),
        name="inproj",
    )(*x_args, norm1, mod, mod, w_in_r)


def _make_mlstm_kernel(seq, layer, has_init):
    nq = seq // TQ
    lane_if, lane_ff, lane_ib, lane_fb = (GATE_LANE0 + ML_HEADS * j for j in range(4))

    def kern(*refs):
        if has_init:
            body(*refs)
            return
        out = refs[10]
        b = pl.program_id(0)

        @pl.when(b < N_SEQ_CTX)
        def _():
            body(*refs)

        @pl.when(b >= N_SEQ_CTX)
        def _():
            out[...] = jnp.zeros_like(out)

    def body(*refs):
        if has_init:
            m0_ref, zq, zk, zv, zo, gz, bg, nrm, c0_ref, n0_ref, _, out, bp_scr, bs_scr = refs
        else:
            zq, zk, zv, zo, gz, bg, nrm, _, _, _, out, cf_ref, nf_ref, mf_ref, bp_scr, bs_scr = refs
        b = pl.program_id(0)
        g = gz[...] + bg[0]
        lane = lax.broadcasted_iota(jnp.int32, g.shape, 1)
        is_forget = ((lane >= lane_ff) & (lane < lane_ib)) | ((lane >= lane_fb) & (lane < lane_fb + ML_HEADS))
        log_sig = jnp.minimum(g, 0.0) - jnp.log1p(jnp.exp(-jnp.abs(g)))
        a = jnp.where(is_forget, log_sig, g)
        r_i = lax.broadcasted_iota(jnp.int32, (seq, seq), 0)
        c_i = lax.broadcasted_iota(jnp.int32, (seq, seq), 1)
        ltri = (c_i <= r_i).astype(F32)
        bp = jnp.dot(ltri, a, precision=HI, preferred_element_type=F32)
        bs = bp[seq - 1:seq, :] - bp + a
        bp_scr[...] = bp
        bs_scr[...] = bs
        eye = (lax.broadcasted_iota(jnp.int32, (LANES, LANES), 0)
               == lax.broadcasted_iota(jnp.int32, (LANES, LANES), 1)).astype(F32)
        tr = lambda x: lax.dot_general(eye, x, (((1,), (1,)), ((), ())), precision=HI, preferred_element_type=F32)
        if has_init:
            tr = lambda x: x.T
        a_t, bp_t, bs_t = tr(a), tr(bp), tr(bs)

        for h in range(ML_HEADS):
            hs = slice(h * ML_DIM, (h + 1) * ML_DIM)
            k = zk[:, hs]
            v = zv[:, hs]
            first_lane = lax.broadcasted_iota(jnp.int32, (seq, ML_DIM), 1) == 0
            v_aug = jnp.concatenate([v, jnp.where(first_lane, 1.0, 0.0).astype(BF16)], axis=1)
            rows = (
                a_t[lane_if + h:lane_if + h + 1, :] - bp_t[lane_ff + h:lane_ff + h + 1, :],
                a_t[lane_ib + h:lane_ib + h + 1, :] - bs_t[lane_fb + h:lane_fb + h + 1, :],
            )
            col_refs = ((bp_scr, lane_ff + h), (bs_scr, lane_fb + h))
            if has_init:
                m0 = tuple(m0_ref[((b * DEPTH + layer) * 2 + dr) * ML_HEADS + h] for dr in range(2))
                c0 = tuple(c0_ref[0, 0, dr, h].astype(BF16) for dr in range(2))
                n0 = tuple(jnp.broadcast_to(n0_ref[0, 0, dr, h:h + 1, :], (ML_DIM, ML_DIM)).astype(BF16) for dr in range(2))
            else:
                m0 = (0.0, 0.0)

            def qblock(qi, carry):
                q0 = pl.multiple_of(qi * TQ, TQ)
                qb = zq[pl.ds(q0, TQ), hs]
                sc = _dot_nt(qb, k)
                t_idx = q0 + lax.broadcasted_iota(jnp.int32, (TQ, seq), 0)
                s_idx = lax.broadcasted_iota(jnp.int32, (TQ, seq), 1)
                hsum = jnp.zeros((TQ, ML_DIM), F32)
                for dr in range(2):
                    cref, cl = col_refs[dr]
                    col = cref[pl.ds(q0, TQ), cl:cl + 1]
                    mask = (s_idx <= t_idx) if dr == 0 else (s_idx >= t_idx)
                    drow = jnp.where(mask, rows[dr], -jnp.inf)
                    c_t = jnp.maximum(m0[dr], jnp.max(drow, axis=1, keepdims=True))
                    s = sc * jnp.exp(drow - c_t)
                    na = _dot(s.astype(BF16), v_aug)
                    num, den = na[:, 0:ML_DIM], na[:, ML_DIM:ML_DIM + 1]
                    if has_init:
                        w_c = jnp.exp(m0[dr] - c_t)
                        num = num + w_c * _dot(qb, c0[dr])
                        den = den + w_c * _dot_nt(qb, n0[dr])[:, 0:1]
                    hsum = hsum + num / jnp.maximum(jnp.abs(den), jnp.exp(-(col + c_t)))
                hn = _rms(hsum, nrm[0][:, hs])
                og = zo[pl.ds(q0, TQ), hs].astype(F32)
                out[pl.ds(q0, TQ), hs] = (hn * jax.nn.sigmoid(og)).astype(out.dtype)
                return carry

            if nq == 1:
                qblock(0, 0)
            else:
                lax.fori_loop(0, nq, qblock, 0)

            if not has_init:
                k_t = _dot_nt(eye.astype(BF16), k)
                kf = k.astype(F32)
                tot = (bp_t[lane_ff + h:lane_ff + h + 1, seq - 1:seq], bp_t[lane_fb + h:lane_fb + h + 1, seq - 1:seq])
                gl = (
                    tot[0] + rows[0],
                    bp_t[lane_fb + h:lane_fb + h + 1, :] - a_t[lane_fb + h:lane_fb + h + 1, :]
                    + a_t[lane_ib + h:lane_ib + h + 1, :],
                )
                for dr in range(2):
                    m_new = jnp.maximum(tot[dr] + m0[dr], jnp.max(gl[dr], axis=1, keepdims=True))
                    w_s = jnp.exp(gl[dr] - m_new)
                    cf_ref[0, 0, dr, h] = _dot((k_t * w_s).astype(BF16), v)
                    n_new = jnp.dot(jnp.broadcast_to(w_s, (SUBLANES, seq)), kf, precision=HI, preferred_element_type=F32)
                    nf_ref[0, 0, dr, h:h + 1, :] = n_new[0:1, :]
                    mf_ref[0, 0, dr * ML_HEADS + h:dr * ML_HEADS + h + 1, :] = jnp.broadcast_to(m_new, (1, LANES))

    return kern


def _mlstm(layer, za, zs, b_gates, mlstm_norm, init=None, ctx_out=None, states=None):
    has_init = init is not None
    seq, nseq, row0 = (S_SMP, N_SEQ_SMP, N_CTX // S_SMP) if has_init else (S_CTX, N_SEQ_CTX, 0)
    qkvo = [pl.BlockSpec((seq, ML_WIDTH), functools.partial(lambda j, b: (row0 + b, j), j)) for j in range(4)]
    in_specs = qkvo + [
        pl.BlockSpec((seq, LANES), lambda b: (row0 + b, ZS_W // LANES - 1)),
        pl.BlockSpec((1, 1, LANES), lambda b: (layer, 0, 0)),
        pl.BlockSpec((1, 1, ML_WIDTH), lambda b: (layer, 0, 0)),
    ]
    args = [za, za, za, za, zs, b_gates, mlstm_norm]
    out_specs = [pl.BlockSpec((seq, ML_WIDTH), lambda b: (row0 + b, 0))]
    out_shape = [jax.ShapeDtypeStruct((N_TOK, ML_WIDTH), BF16)]
    aliases = {}
    if has_init:
        st_c, st_n, st_m = init
        in_specs = [pl.BlockSpec(memory_space=pltpu.SMEM)] + in_specs + [
            pl.BlockSpec((1, 1, 2, ML_HEADS, ML_DIM, ML_DIM), lambda b: (b, layer, 0, 0, 0, 0)),
            pl.BlockSpec((1, 1, 2, ML_HEADS, ML_DIM), lambda b: (b, layer, 0, 0, 0)),
            pl.BlockSpec(memory_space=pl.ANY),
        ]
        args = [st_m.reshape(-1)] + args + [st_c, st_n, ctx_out]
        aliases = {len(args) - 1: 0}
    else:
        seq_blk = lambda b: jnp.minimum(b, nseq - 1)
        in_specs += [pl.BlockSpec(memory_space=pl.ANY)] * 3
        args += list(states)
        aliases = {len(args) - 3 + j: 1 + j for j in range(3)}
        out_specs += [
            pl.BlockSpec((1, 1, 2, ML_HEADS, ML_DIM, ML_DIM), lambda b: (seq_blk(b), layer, 0, 0, 0, 0)),
            pl.BlockSpec((1, 1, 2, ML_HEADS, ML_DIM), lambda b: (seq_blk(b), layer, 0, 0, 0)),
            pl.BlockSpec((1, 1, 2 * ML_HEADS, LANES), lambda b: (seq_blk(b), layer, 0, 0)),
        ]
        out_shape += [jax.ShapeDtypeStruct(s.shape, s.dtype) for s in states]
    return pl.pallas_call(
        _make_mlstm_kernel(seq, layer, has_init),
        grid=(nseq if has_init else N_SEQ_BLOCKS,),
        in_specs=in_specs,
        out_specs=out_specs,
        out_shape=out_shape,
        scratch_shapes=[pltpu.VMEM((seq, LANES), F32), pltpu.VMEM((seq, LANES), F32)],
        input_output_aliases=aliases,
        compiler_params=_cparams(("arbitrary",)),
        name="mlstm_smp" if has_init else "mlstm_ctx",
    )(*args)


def _rope(x, cos, sin_a, sin_b):
    return x * cos + pltpu.roll(x, LANES - 8, 1) * sin_a + pltpu.roll(x, 8, 1) * sin_b


def _mla_prep_kernel(zs_ref, qa_ref, kva_ref, wuq_ref, wk_ref, wv_ref, qn_ref, kn_ref, cos_ref, sa_ref, sb_ref, _, __,
                     q_ref, kk_ref, v_ref, ckv_ref, kr_ref, qf_scr, kf_scr):
    cq = zs_ref[:, 0:MLA_Q_RANK]
    ckv = zs_ref[:, MLA_Q_RANK:MLA_Q_RANK + MLA_KV_RANK]
    last = zs_ref[:, ZS_W - LANES:ZS_W]
    qf_scr[...] = _dot(_rms(cq, qa_ref[0]).astype(BF16), wuq_ref[0])
    ckvn = _rms(ckv, kva_ref[0])

    @pl.when(pl.program_id(0) < N_TILES_CTX)
    def _():
        for j in range(TW // S_CTX):
            ckv_ref[j, 0] = ckvn[j * S_CTX:(j + 1) * S_CTX, :]
            kr_ref[j, 0] = last[j * S_CTX:(j + 1) * S_CTX, :].T[0:MLA_ROPE, :]

    cb = ckvn.astype(BF16)
    kf_scr[...] = _dot(cb, wk_ref[0])
    v_ref[...] = _dot(cb, wv_ref[0]).astype(BF16)
    lane = lax.broadcasted_iota(jnp.int32, last.shape, 1)
    kr = jnp.where((lane >= MLA_NOPE) & (lane < MLA_QK), pltpu.roll(last, MLA_NOPE, 1), 0.0)
    is_latent = pl.program_id(0) >= N_TILES_CTX

    def heads(rotate):
        for h in range(MLA_HEADS):
            hs = slice(h * HEAD_PAD, (h + 1) * HEAD_PAD)
            q_ref[:, hs] = rotate(_rms(qf_scr[:, hs], qn_ref[0], n=MLA_QK)).astype(BF16)
            kk_ref[:, hs] = rotate(_rms(kf_scr[:, hs] + kr, kn_ref[0], n=MLA_QK)).astype(BF16)

    @pl.when(is_latent)
    def _():
        cos, sa, sb = cos_ref[...], sa_ref[...], sb_ref[...]
        heads(lambda x: _rope(x, cos, sa, sb))

    @pl.when(jnp.logical_not(is_latent))
    def _():
        heads(lambda x: x)


def _mla_prep(layer, zs, q_a_norm, kv_a_norm, w_uq_r, w_k_r, w_v_r, q_norm_p, k_norm_p, rope_tabs, new_ckv, new_kr):
    seq_blk = lambda r, c: pl.BlockSpec((TW // S_CTX, 1, r, c), lambda i: (jnp.minimum(i, N_TILES_CTX - 1), layer, 0, 0))
    lw = lambda shape: pl.BlockSpec((1,) + shape, lambda i: (layer,) + (0,) * len(shape))
    tab = pl.BlockSpec((TW, LANES), lambda i: (jnp.where(i < N_TILES_CTX, 0, 1 + (i - N_TILES_CTX) % TILES_PER_SMP_SEQ), 0))
    tok = lambda w: pl.BlockSpec((TW, w), lambda i: (i, 0))
    return pl.pallas_call(
        _mla_prep_kernel,
        grid=(N_TILES,),
        in_specs=[
            tok(ZS_W), lw((1, MLA_Q_RANK)), lw((1, MLA_KV_RANK)),
            lw((MLA_Q_RANK, MLA_HEADS * HEAD_PAD)), lw((MLA_KV_RANK, MLA_HEADS * HEAD_PAD)),
            lw((MLA_KV_RANK, MLA_HEADS * MLA_V)), lw((1, HEAD_PAD)), lw((1, HEAD_PAD)), tab, tab, tab,
            pl.BlockSpec(memory_space=pl.ANY), pl.BlockSpec(memory_space=pl.ANY),
        ],
        out_specs=[tok(MLA_HEADS * HEAD_PAD), tok(MLA_HEADS * HEAD_PAD), tok(MLA_HEADS * MLA_V),
                   seq_blk(S_CTX, MLA_KV_RANK), seq_blk(MLA_ROPE, S_CTX)],
        input_output_aliases={11: 3, 12: 4},
        out_shape=[
            jax.ShapeDtypeStruct((N_TOK, MLA_HEADS * HEAD_PAD), BF16),
            jax.ShapeDtypeStruct((N_TOK, MLA_HEADS * HEAD_PAD), BF16),
            jax.ShapeDtypeStruct((N_TOK, MLA_HEADS * MLA_V), BF16),
            jax.ShapeDtypeStruct(new_ckv.shape, F32),
            jax.ShapeDtypeStruct(new_kr.shape, F32),
        ],
        scratch_shapes=[pltpu.VMEM((TW, MLA_HEADS * HEAD_PAD), F32), pltpu.VMEM((TW, MLA_HEADS * HEAD_PAD), F32)],
        compiler_params=_cparams(("arbitrary",)),
        name="mla_prep",
    )(zs, q_a_norm, kv_a_norm, w_uq_r, w_k_r, w_v_r, q_norm_p, k_norm_p, *rope_tabs, new_ckv, new_kr)


def _cache_kv_kernel(ckv_ref, kr_ref, wk_ref, wv_ref, kn_ref, kk_ref, v_ref):
    cb = ckv_ref[...].astype(BF16)
    kf = _dot(cb, wk_ref[0])
    v_ref[...] = _dot(cb, wv_ref[0]).astype(BF16)
    kr = kr_ref[...]
    for h in range(MLA_HEADS):
        hs = slice(h * HEAD_PAD, (h + 1) * HEAD_PAD)
        kk_ref[:, hs] = _rms(kf[:, hs] + kr, kn_ref[0], n=MLA_QK).astype(BF16)


def _cache_kv(cache_ckv, cache_kr_pad, w_k_r, w_v_r, k_norm_p):
    lw = lambda shape: pl.BlockSpec((1,) + shape, lambda b, l: (l,) + (0,) * len(shape))
    blk = lambda w: pl.BlockSpec((None, None, PAST_LEN, w), lambda b, l: (b, l, 0, 0))
    return pl.pallas_call(
        _cache_kv_kernel,
        grid=(N_SEQ_SMP, DEPTH),
        in_specs=[blk(MLA_KV_RANK), blk(LANES), lw((MLA_KV_RANK, MLA_HEADS * HEAD_PAD)),
                  lw((MLA_KV_RANK, MLA_HEADS * MLA_V)), lw((1, HEAD_PAD))],
        out_specs=[blk(MLA_HEADS * HEAD_PAD), blk(MLA_HEADS * MLA_V)],
        out_shape=[
            jax.ShapeDtypeStruct((N_SEQ_SMP, DEPTH, PAST_LEN, MLA_HEADS * HEAD_PAD), BF16),
            jax.ShapeDtypeStruct((N_SEQ_SMP, DEPTH, PAST_LEN, MLA_HEADS * MLA_V), BF16),
        ],
        compiler_params=_cparams(("arbitrary", "arbitrary")),
        name="cache_kv",
    )(cache_ckv, cache_kr_pad, w_k_r, w_v_r, k_norm_p)


def _make_attn_kernel(n_src):
    scale = MLA_QK ** -0.5

    def kern(q_ref, *refs):
        o_ref = refs[-1]
        if n_src > 1:
            body(q_ref, *refs)
            return

        @pl.when(pl.program_id(0) < N_SEQ_CTX)
        def _():
            body(q_ref, *refs)

        @pl.when(pl.program_id(0) >= N_SEQ_CTX)
        def _():
            o_ref[...] = jnp.zeros_like(o_ref)

    def body(q_ref, *refs):
        o_ref = refs[-1]
        for h in range(MLA_HEADS):
            hs = slice(h * HEAD_PAD, (h + 1) * HEAD_PAD)
            vs = slice(h * MLA_V, (h + 1) * MLA_V)
            q = q_ref[:, hs]
            ss = [_dot_nt(q, refs[2 * j][:, hs]) * scale for j in range(n_src)]
            m = functools.reduce(jnp.maximum, [jnp.max(s, axis=1, keepdims=True) for s in ss])
            ps = [jnp.exp(s - m) for s in ss]
            l = functools.reduce(jnp.add, [jnp.sum(p, axis=1, keepdims=True) for p in ps])
            o = functools.reduce(jnp.add, [_dot(ps[j].astype(BF16), refs[2 * j + 1][:, vs]) for j in range(n_src)])
            o_ref[:, vs] = (o / l).astype(o_ref.dtype)

    return kern


def _attn_ctx(q, kk, v):
    blk = lambda w: pl.BlockSpec((S_CTX, w), lambda b: (b, 0))
    return pl.pallas_call(
        _make_attn_kernel(1),
        grid=(N_SEQ_BLOCKS,),
        in_specs=[blk(MLA_HEADS * HEAD_PAD), blk(MLA_HEADS * HEAD_PAD), blk(MLA_HEADS * MLA_V)],
        out_specs=blk(MLA_HEADS * MLA_V),
        out_shape=jax.ShapeDtypeStruct((N_TOK, MLA_HEADS * MLA_V), BF16),
        compiler_params=_cparams(("arbitrary",)),
        name="attn_ctx",
    )(q, kk, v)


def _attn_smp(layer, q, kk, v, kk_cache, v_cache, ctx_out):
    row0 = N_CTX // S_SMP
    nq = S_SMP // TQ
    seqb = lambda w: pl.BlockSpec((S_SMP, w), lambda b, i: (row0 + b, 0))
    cache = lambda w: pl.BlockSpec((None, None, PAST_LEN, w), lambda b, i: (b, layer, 0, 0))
    return pl.pallas_call(
        _make_attn_kernel(2),
        grid=(N_SEQ_SMP, nq),
        in_specs=[
            pl.BlockSpec((TQ, MLA_HEADS * HEAD_PAD), lambda b, i: (N_CTX // TQ + b * nq + i, 0)),
            seqb(MLA_HEADS * HEAD_PAD), seqb(MLA_HEADS * MLA_V),
            cache(MLA_HEADS * HEAD_PAD), cache(MLA_HEADS * MLA_V),
            pl.BlockSpec(memory_space=pl.ANY),
        ],
        out_specs=pl.BlockSpec((TQ, MLA_HEADS * MLA_V), lambda b, i: (N_CTX // TQ + b * nq + i, 0)),
        out_shape=jax.ShapeDtypeStruct((N_TOK, MLA_HEADS * MLA_V), BF16),
        input_output_aliases={5: 0},
        compiler_params=_cparams(("arbitrary", "arbitrary")),
        name="attn_smp",
    )(q, kk, v, kk_cache, v_cache, ctx_out)


def _merge_kernel(x_ref, oa_ref, ob_ref, zc_ref, zb_ref, sgn_ref, ws_ref, bs_ref, wb_ref, wo_ref, g1_ref, n2_ref,
                  sh2_ref, sc2_ref, wr_ref, br_ref,
                  xmid_ref, h2_ref, dest_ref, wsel_ref, cnt_ref, oc_scr, carry_scr):
    i = pl.program_id(0)

    @pl.when(i == 0)
    def _():
        carry_scr[...] = jnp.zeros_like(carry_scr)

    u = _gelu(zc_ref[:, 0:SG_WIDTH].astype(F32))
    vg = _gelu(zc_ref[:, SG_WIDTH:2 * SG_WIDTH].astype(F32))
    for g in range(SG_GROUPS):
        gs = slice(g * SG_DIM, (g + 1) * SG_DIM)
        vn = _rms(vg[:, gs], sgn_ref[0][:, gs]).astype(BF16)
        for c in range(TW // SG_CHUNK):
            cs = slice(c * SG_CHUNK, (c + 1) * SG_CHUNK)
            mixed = _dot(ws_ref[0, g], vn[cs, :]) + bs_ref[0][:, g:g + 1]
            oc_scr[cs, gs] = (u[cs, gs] * mixed).astype(BF16)

    acc = jnp.zeros((TW, D_MODEL), F32)
    for j, src in enumerate((oa_ref, ob_ref, oc_scr)):
        gate = jax.nn.sigmoid(zb_ref[:, j * D_MODEL:(j + 1) * D_MODEL].astype(F32))
        acc = acc + gate * _dot(src[...], wb_ref[0, j])
    xm = x_ref[...] + g1_ref[0, 0] * _dot(acc.astype(BF16), wo_ref[0])
    xmid_ref[...] = xm
    h2 = _rms(xm, n2_ref[0]) * (1.0 + sc2_ref[0, 0]) + sh2_ref[0, 0]
    for c in range(SC_SPLIT):
        h2_ref[c] = _pack_pairs(h2[:, 2 * c * SC_ROW:(2 * c + 1) * SC_ROW], h2[:, (2 * c + 1) * SC_ROW:(2 * c + 2) * SC_ROW])

    h_hi = h2.astype(BF16)
    h_lo = (h2 - h_hi.astype(F32)).astype(BF16)
    p_hi = _dot(h_hi, wr_ref[0])
    logits = p_hi[:, 0:LANES] + p_hi[:, LANES:2 * LANES] + _dot(h_lo, wr_ref[0, :, 0:LANES]) + br_ref[0]
    lane = lax.broadcasted_iota(jnp.int32, logits.shape, 1)
    hits, exps = [], []
    sel = jnp.zeros(logits.shape, F32)
    denom = jnp.zeros((TW, 1), F32)
    top = None
    for _ in range(TOP_K):
        m = jnp.max(logits, axis=1, keepdims=True)
        idx = jnp.min(jnp.where(logits == m, lane, LANES), axis=1, keepdims=True)
        hit = lane == idx
        top = m if top is None else top
        hits.append(hit)
        exps.append(jnp.exp(m - top))
        sel = jnp.where(hit, 1.0, sel)
        denom = denom + exps[-1]
        logits = jnp.where(hit, -jnp.inf, logits)

    r_i = lax.broadcasted_iota(jnp.int32, (TW, TW), 0)
    c_i = lax.broadcasted_iota(jnp.int32, (TW, TW), 1)
    carry = carry_scr[0:1, :]
    rank = _dot(jnp.where(c_i < r_i, 1.0, 0.0).astype(BF16), sel.astype(BF16)) + carry
    new_carry = carry + jnp.sum(sel, axis=0, keepdims=True)
    carry_scr[...] = jnp.broadcast_to(new_carry, (SUBLANES, LANES))
    cnt_ref[...] = jnp.broadcast_to(new_carry, (SUBLANES, LANES))
    slot = rank + lane.astype(F32) * float(EXPERT_CAP)
    dmat = jnp.zeros(logits.shape, F32)
    wmat = jnp.zeros(logits.shape, F32)
    for k in range(TOP_K):
        dk = jnp.sum(jnp.where(hits[k], slot, 0.0), axis=1, keepdims=True)
        dmat = jnp.where(lane == k, dk, dmat)
        wmat = jnp.where(lane == k, exps[k] / denom, wmat)
    dest_ref[...] = dmat.T[0:SUBLANES, :].astype(jnp.int32)
    wsel_ref[...] = wmat


def _merge(layer, x, oa, ob, zc, zb, sg_norm, w_sp, b_sp, w_branch, w_out, mod, norm2, w_router_p, b_router_p):
    lw = lambda shape: pl.BlockSpec((1,) + shape, lambda i: (layer,) + (0,) * len(shape))
    tok = lambda w: pl.BlockSpec((TW, w), lambda i: (i, 0))
    return pl.pallas_call(
        _merge_kernel,
        grid=(N_TILES,),
        in_specs=[
            tok(D_MODEL), tok(ML_WIDTH), tok(MLA_HEADS * MLA_V), tok(ZC_W), tok(ZB_W),
            lw((1, SG_WIDTH)), lw((SG_GROUPS, SG_CHUNK, SG_CHUNK)), lw((SG_CHUNK, LANES)),
            lw((N_BRANCH, ML_WIDTH, D_MODEL)), lw((D_MODEL, D_MODEL)),
            _mod_spec(layer, 2), lw((1, D_MODEL)), _mod_spec(layer, 3), _mod_spec(layer, 4),
            lw((D_MODEL, 2 * LANES)), lw((1, LANES)),
        ],
        out_specs=[tok(D_MODEL), pl.BlockSpec((SC_SPLIT, TW, SC_ROW), lambda i: (0, i, 0)),
                   pl.BlockSpec((SUBLANES, TW), lambda i: (0, i)), tok(LANES),
                   pl.BlockSpec((SUBLANES, LANES), lambda i: (0, 0))],
        out_shape=[
            jax.ShapeDtypeStruct((N_TOK, D_MODEL), F32),
            jax.ShapeDtypeStruct((SC_SPLIT, N_TOK, SC_ROW), jnp.uint32),
            jax.ShapeDtypeStruct((SUBLANES, N_TOK), jnp.int32),
            jax.ShapeDtypeStruct((N_TOK, LANES), F32),
            jax.ShapeDtypeStruct((SUBLANES, LANES), F32),
        ],
        scratch_shapes=[pltpu.VMEM((TW, SG_WIDTH), BF16), pltpu.VMEM((SUBLANES, LANES), F32)],
        compiler_params=_cparams(("arbitrary",)),
        name="merge_router",
    )(x, oa, ob, zc, zb, sg_norm, w_sp, b_sp, w_branch, w_out, mod, norm2, mod, mod, w_router_p, b_router_p)


def _sc_mesh():
    return plsc.VectorSubcoreMesh(core_axis_name="core", subcore_axis_name="subcore")


def _sc_scatter_rows(x, idxs, n_rows):
    @pl.kernel(out_type=jax.ShapeDtypeStruct((n_rows, SC_ROW), x.dtype), mesh=_sc_mesh(), scratch_types=[])
    def scatter(x_hbm, *refs):
        o_hbm = refs[-1]

        def body(x_vmem, *i_vmems):
            for i_vmem in i_vmems:
                pltpu.sync_copy(x_vmem, o_hbm.at[i_vmem.at[0]])

        pltpu.emit_pipeline(
            body,
            grid=(x.shape[0] // SC_WIN,),
            in_specs=[pl.BlockSpec((SC_WIN, SC_ROW), lambda i: (i, 0))]
            + [pl.BlockSpec((1, SC_WIN), lambda i: (0, i))] * len(idxs),
            out_specs=[],
            core_axis_name=("core", "subcore"),
            dimension_semantics=(pltpu.PARALLEL,),
        )(x_hbm, *refs[:-1])

    return scatter(x, *idxs)


def _sc_gather_rows(x, idx):
    m = idx.shape[1]

    @pl.kernel(out_type=jax.ShapeDtypeStruct((m, SC_ROW), x.dtype), mesh=_sc_mesh())
    def gather(x_hbm, i_hbm, o_hbm):
        def body(i_vmem, o_vmem):
            pltpu.sync_copy(x_hbm.at[i_vmem.at[0]], o_vmem)

        pltpu.emit_pipeline(
            body,
            grid=(m // SC_WIN,),
            in_specs=[pl.BlockSpec((1, SC_WIN), lambda i: (0, i))],
            out_specs=[pl.BlockSpec((SC_WIN, SC_ROW), lambda i: (i, 0))],
            core_axis_name=("core", "subcore"),
            dimension_semantics=(pltpu.PARALLEL,),
        )(i_hbm, o_hbm)

    return gather(x, idx)


STEP_VALID, STEP_FIRST, STEP_HAS_NEXT = 1, 2, 4
STEP_CHUNKS_SHIFT = 3


def _moe_ffn_kernel(layer, be_ref, nx_ref, br_ref, fl_ref, xs_ref, b1_ref, b2_ref, w1_hbm, w2_hbm, y_ref,
                    w1f, w2f, w1b, w2b, sem):
    g = pl.program_id(0)
    flags = fl_ref[g]

    def weight_copies(e):
        return (pltpu.make_async_copy(w1_hbm.at[layer, e], w1f, sem.at[0]),
                pltpu.make_async_copy(w2_hbm.at[layer, e], w2f, sem.at[1]))

    @pl.when(g == 0)
    def _():
        for cp in weight_copies(be_ref[0]):
            cp.start()

    @pl.when((flags & STEP_FIRST) != 0)
    def _():
        for cp in weight_copies(be_ref[g]):
            cp.wait()
        w1b[...] = w1f[...].astype(BF16)
        w2b[...] = w2f[...].astype(BF16)

        @pl.when((flags & STEP_HAS_NEXT) != 0)
        def _():
            for cp in weight_copies(nx_ref[g]):
                cp.start()

    def ffn(n_rows):
        halves = [h.astype(BF16) for c in range(SC_SPLIT) for h in _unpack_pairs(xs_ref[c, 0:n_rows, :])]
        g1 = _dot(jnp.concatenate(halves, axis=1), w1b[...]) + b1_ref[0, 0]
        gate = jnp.minimum(g1[:, :D_EXPERT], SWIGLU_LIMIT)
        up = jnp.clip(g1[:, D_EXPERT:], -SWIGLU_LIMIT, SWIGLU_LIMIT)
        act = gate * jax.nn.sigmoid(SWIGLU_ALPHA * gate) * (up + 1.0)
        y = _dot(act.astype(BF16), w2b[...]) + b2_ref[0, 0]
        for c in range(SC_SPLIT):
            y_ref[c, 0:n_rows, :] = _pack_pairs(
                y[:, 2 * c * SC_ROW:(2 * c + 1) * SC_ROW], y[:, (2 * c + 1) * SC_ROW:(2 * c + 2) * SC_ROW])

    for n_chunks in range(1, FFN_CHUNKS + 1):
        want = STEP_VALID | ((n_chunks - 1) << STEP_CHUNKS_SHIFT)
        pl.when((flags & (STEP_VALID | ((FFN_CHUNKS - 1) << STEP_CHUNKS_SHIFT))) == want)(
            functools.partial(ffn, n_chunks * SLOT_CHUNK))


def _moe_ffn(layer, xs, plan, w1, b1, w2, b2):
    eb = lambda c: pl.BlockSpec((1, 1, 1, c), lambda g, be, nx, br, fl: (layer, be[g], 0, 0))
    rows = pl.BlockSpec((SC_SPLIT, FFN_BLOCK, SC_ROW), lambda g, be, nx, br, fl: (0, br[g], 0))
    hbm = pl.BlockSpec(memory_space=pl.ANY)
    grid_spec = pltpu.PrefetchScalarGridSpec(
        num_scalar_prefetch=4,
        grid=(N_CHUNK_STEPS,),
        in_specs=[rows, eb(2 * D_EXPERT), eb(D_MODEL), hbm, hbm],
        out_specs=rows,
        scratch_shapes=[
            pltpu.VMEM((D_MODEL, 2 * D_EXPERT), F32), pltpu.VMEM((D_EXPERT, D_MODEL), F32),
            pltpu.VMEM((D_MODEL, 2 * D_EXPERT), BF16), pltpu.VMEM((D_EXPERT, D_MODEL), BF16),
            pltpu.SemaphoreType.DMA((2,)),
        ],
    )
    return pl.pallas_call(
        functools.partial(_moe_ffn_kernel, layer),
        grid_spec=grid_spec,
        out_shape=jax.ShapeDtypeStruct(xs.shape, xs.dtype),
        compiler_params=_cparams(("arbitrary",)),
        name="moe_ffn",
    )(*plan, xs, b1, b2, w1, w2)


def _chunk_plan_kernel(cnt_ref, be_ref, nx_ref, br_ref, fl_ref):
    def expert(e, carry):
        step0, prev_first = carry
        c = cnt_ref[e]
        n_blk = (c + FFN_BLOCK - 1) // FFN_BLOCK

        def block(j, _):
            s = step0 + j
            be_ref[s] = e
            nx_ref[s] = e
            br_ref[s] = e * (EXPERT_CAP // FFN_BLOCK) + j
            fl_ref[s] = (STEP_VALID + jnp.where(j == 0, STEP_FIRST, 0)
                         + (jnp.minimum((c - j * FFN_BLOCK + SLOT_CHUNK - 1) // SLOT_CHUNK, FFN_CHUNKS) - 1)
                         * (1 << STEP_CHUNKS_SHIFT))
            return 0

        lax.fori_loop(0, n_blk, block, 0)

        @pl.when((n_blk > 0) & (prev_first >= 0))
        def _():
            nx_ref[prev_first] = e
            fl_ref[prev_first] = fl_ref[prev_first] + STEP_HAS_NEXT

        return step0 + n_blk, jnp.where(n_blk > 0, step0, prev_first)

    used, _ = lax.fori_loop(0, N_EXPERTS, expert, (jnp.int32(0), jnp.int32(-1)))

    def idle(s, _):
        be_ref[s] = be_ref[used - 1]
        nx_ref[s] = be_ref[used - 1]
        br_ref[s] = br_ref[used - 1]
        fl_ref[s] = 0
        return 0

    lax.fori_loop(used, N_CHUNK_STEPS, idle, 0)


def _chunk_plan(cnt):
    smem = pl.BlockSpec(memory_space=pltpu.SMEM)
    return pl.pallas_call(
        _chunk_plan_kernel,
        in_specs=[smem],
        out_specs=[smem] * 4,
        out_shape=[jax.ShapeDtypeStruct((N_CHUNK_STEPS,), jnp.int32)] * 4,
        name="chunk_plan",
    )(cnt)


def _combine_kernel(x_ref, yg_ref, w_ref, g_ref, *o_refs):
    def emit(o_ref):
        w = w_ref[...]
        for c in range(SC_SPLIT):
            parts = [_unpack_pairs(yg_ref[k, c]) for k in range(TOP_K)]
            for half in range(2):
                cs = slice((2 * c + half) * SC_ROW, (2 * c + half + 1) * SC_ROW)
                acc = w[:, 0:1] * parts[0][half]
                for k in range(1, TOP_K):
                    acc = acc + w[:, k:k + 1] * parts[k][half]
                o_ref[:, cs] = x_ref[:, cs] + g_ref[0, 0][:, cs] * acc

    if len(o_refs) == 1:
        emit(o_refs[0])
    else:
        pl.when(pl.program_id(0) < N_TILES_CTX)(lambda: emit(o_refs[0]))
        pl.when(pl.program_id(0) >= N_TILES_CTX)(lambda: emit(o_refs[1]))


def _combine(layer, xmid, yg, wsel, mod, split_out):
    tok = lambda w: pl.BlockSpec((TW, w), lambda i: (i, 0))
    if split_out:
        out_specs = [pl.BlockSpec((TW, D_MODEL), lambda i: (jnp.minimum(i, N_TILES_CTX - 1), 0)),
                     pl.BlockSpec((TW, D_MODEL), lambda i: (jnp.maximum(i - N_TILES_CTX, 0), 0))]
        out_shape = [jax.ShapeDtypeStruct((N_CTX, D_MODEL), F32), jax.ShapeDtypeStruct((N_SMP, D_MODEL), F32)]
    else:
        out_specs, out_shape = tok(D_MODEL), jax.ShapeDtypeStruct((N_TOK, D_MODEL), F32)
    return pl.pallas_call(
        _combine_kernel,
        grid=(N_TILES,),
        in_specs=[tok(D_MODEL), pl.BlockSpec((TOP_K, SC_SPLIT, TW, SC_ROW), lambda i: (0, 0, i, 0)), tok(LANES),
                  _mod_spec(layer, 5)],
        out_specs=out_specs,
        out_shape=out_shape,
        compiler_params=_cparams(("arbitrary",)),
        name="combine",
    )(xmid, yg, wsel, mod)


def _moe(layer, xmid, h2, dest, wsel, cnt, mod, w1, b1, w2, b2, split_out):
    n_slots = N_EXPERTS * EXPERT_CAP
    idx = dest[0:TOP_K][:, None, :] + (jnp.arange(SC_SPLIT, dtype=jnp.int32) * n_slots)[None, :, None]
    idx = idx.reshape(TOP_K, 1, SC_SPLIT * N_TOK)
    xs = _sc_scatter_rows(h2.reshape(SC_SPLIT * N_TOK, SC_ROW), [idx[k] for k in range(TOP_K)], SC_SPLIT * n_slots)
    plan = _chunk_plan(cnt[0, :N_EXPERTS].astype(jnp.int32))
    y = _moe_ffn(layer, xs.reshape(SC_SPLIT, n_slots, SC_ROW), plan, w1, b1, w2, b2)
    yg = _sc_gather_rows(y.reshape(SC_SPLIT * n_slots, SC_ROW), idx.reshape(1, TOP_K * SC_SPLIT * N_TOK))
    return _combine(layer, xmid, yg.reshape(TOP_K, SC_SPLIT, N_TOK, SC_ROW), wsel, mod, split_out)


def _rope_tables():
    pos = np.arange(S_SMP)
    half = MLA_ROPE // 2
    inv_freq = (ROPE_THETA ** (-(np.arange(0, half, 2, dtype=np.float32) / np.float32(half)))).astype(np.float32)
    angs = [((pos // GRID_W).astype(np.float32)[:, None] * inv_freq[None, :]).astype(np.float32),
            ((pos % GRID_W).astype(np.float32)[:, None] * inv_freq[None, :]).astype(np.float32)]
    nf = half // 2
    cos = np.ones((TW + S_SMP, LANES), np.float32)
    sin_a = np.zeros((TW + S_SMP, LANES), np.float32)
    sin_b = np.zeros((TW + S_SMP, LANES), np.float32)
    for axis, ang in enumerate(angs):
        base = MLA_NOPE + axis * half
        c, s = np.cos(ang.astype(np.float64)), np.sin(ang.astype(np.float64))
        cos[TW:, base:base + nf] = c
        cos[TW:, base + nf:base + half] = c
        sin_a[TW:, base:base + nf] = -s
        sin_b[TW:, base + nf:base + half] = s
    return jnp.asarray(cos), jnp.asarray(sin_a), jnp.asarray(sin_b)


def _pad_last(a, width):
    return jnp.pad(a, [(0, 0)] * (a.ndim - 1) + [(0, width - a.shape[-1])])


def kernel(x_prompt, x_sample, cache_mla_ckv, cache_mla_krope, state_mlstm_C, state_mlstm_n, state_mlstm_m, c, c_ctx, norm1, norm2, w_ada, b_ada, w_in, b_mlstm_gates, mlstm_norm, mla_q_a_norm, mla_kv_a_norm, w_uq, w_ukv, mla_q_norm, mla_k_norm, sg_norm, w_spatial, b_spatial, w_branch, w_out, w_router, b_router, w_exp1, b_exp1, w_exp2, b_exp2):
    w_in_r = _w_in_prep(w_in)
    w_uq_r = _pad_last(w_uq.reshape(DEPTH, MLA_Q_RANK, MLA_HEADS, MLA_QK), HEAD_PAD).reshape(
        DEPTH, MLA_Q_RANK, MLA_HEADS * HEAD_PAD).astype(BF16)
    w_ukv4 = w_ukv.reshape(DEPTH, MLA_KV_RANK, MLA_HEADS, MLA_NOPE + MLA_V)
    w_k_r = _pad_last(w_ukv4[..., :MLA_NOPE], HEAD_PAD).reshape(DEPTH, MLA_KV_RANK, MLA_HEADS * HEAD_PAD).astype(BF16)
    w_v_r = w_ukv4[..., MLA_NOPE:].reshape(DEPTH, MLA_KV_RANK, MLA_HEADS * MLA_V).astype(BF16)
    q_norm_p = _pad_last(mla_q_norm, HEAD_PAD).reshape(DEPTH, 1, HEAD_PAD)
    k_norm_p = _pad_last(mla_k_norm, HEAD_PAD).reshape(DEPTH, 1, HEAD_PAD)
    b_gates_p = jnp.pad(b_mlstm_gates, ((0, 0), (GATE_LANE0, LANES - GATE_LANE0 - 4 * ML_HEADS))).reshape(DEPTH, 1, LANES)
    b_sp = _pad_last(jnp.swapaxes(b_spatial, 1, 2), LANES)
    w_router_p = _pad_last(w_router, LANES)
    w_router_hi = w_router_p.astype(BF16)
    w_router_p = jnp.concatenate([w_router_hi, (w_router_p - w_router_hi.astype(F32)).astype(BF16)], axis=-1)
    b_router_p = jnp.pad(b_router, ((0, 0), (0, LANES - N_EXPERTS)), constant_values=-1e30).reshape(DEPTH, 1, LANES)
    r3 = lambda a: a.reshape(DEPTH, 1, a.shape[-1])
    cache_kr_pad = jnp.pad(cache_mla_krope, ((0, 0), (0, 0), (0, 0), (MLA_NOPE, LANES - MLA_QK)))
    rope_tabs = _rope_tables()

    cvec = jnp.concatenate([c_ctx[None, :], c, jnp.zeros((SUBLANES - 1 - N_SEQ_SMP, D_MODEL), F32)], axis=0)
    mod = _adaln(cvec, w_ada, b_ada).reshape(DEPTH, SUBLANES, 1, 6 * D_MODEL)
    b1 = b_exp1.reshape(DEPTH, N_EXPERTS, 1, 2 * D_EXPERT)
    b2 = b_exp2.reshape(DEPTH, N_EXPERTS, 1, D_MODEL)
    kk_cache, v_cache = _cache_kv(cache_mla_ckv, cache_kr_pad, w_k_r, w_v_r, k_norm_p)

    x = (x_prompt.reshape(N_CTX, D_MODEL), x_sample.reshape(N_SMP, D_MODEL))
    new_ckv = jnp.zeros((N_SEQ_CTX, DEPTH, S_CTX, MLA_KV_RANK), F32)
    new_kr = jnp.zeros((N_SEQ_CTX, DEPTH, MLA_ROPE, S_CTX), F32)
    states = (jnp.zeros((N_SEQ_CTX, DEPTH, 2, ML_HEADS, ML_DIM, ML_DIM), F32),
              jnp.zeros((N_SEQ_CTX, DEPTH, 2, ML_HEADS, ML_DIM), F32),
              jnp.zeros((N_SEQ_CTX, DEPTH, 2 * ML_HEADS, LANES), F32))
    for l in range(DEPTH):
        za, zs, zc, zb, *x_joined = _inproj(l, x, r3(norm1), mod, w_in_r)
        x = x_joined[0] if x_joined else x
        oa, *states = _mlstm(l, za, zs, b_gates_p, r3(mlstm_norm), states=states)
        (oa,) = _mlstm(l, za, zs, b_gates_p, r3(mlstm_norm), init=(state_mlstm_C, state_mlstm_n, state_mlstm_m), ctx_out=oa)
        q, kk, v, new_ckv, new_kr = _mla_prep(l, zs, r3(mla_q_a_norm), r3(mla_kv_a_norm), w_uq_r, w_k_r, w_v_r,
                                              q_norm_p, k_norm_p, rope_tabs, new_ckv, new_kr)
        ob = _attn_smp(l, q, kk, v, kk_cache, v_cache, _attn_ctx(q, kk, v))
        xmid, h2, dest, wsel, cnt = _merge(
            l, x, oa, ob, zc, zb, r3(sg_norm), w_spatial.astype(BF16), b_sp, w_branch.astype(BF16), w_out.astype(BF16),
            mod, r3(norm2), w_router_p, b_router_p)
        x = _moe(l, xmid, h2, dest, wsel, cnt, mod, w_exp1, b1, w_exp2, b2, split_out=l == DEPTH - 1)
    y_ctx, y_smp = x
    return (
        y_ctx.reshape(N_SEQ_CTX, S_CTX, D_MODEL),
        y_smp.reshape(N_SEQ_SMP, S_SMP, D_MODEL),
        new_ckv,
        jnp.swapaxes(new_kr, 2, 3),
        states[0],
        states[1],
        states[2][:, :, :, 0].reshape(N_SEQ_CTX, DEPTH, 2, ML_HEADS),
    )
```

```python
import functools

import numpy as np
import jax
import jax.numpy as jnp
from jax import lax
from jax.experimental import pallas as pl
from jax.experimental.pallas import tpu as pltpu
from jax.experimental.pallas import tpu_sc as plsc

F32 = jnp.float32
BF16 = jnp.bfloat16
HI = lax.Precision.HIGHEST

D_MODEL = 1024
N_SEQ_CTX, S_CTX = 32, 256
N_SEQ_SMP, S_SMP = 2, 1024
DEPTH = 4
PAST_LEN = 512
GRID_W = 64
EPS = 1e-6
ML_HEADS, ML_DIM = 4, 128
ML_WIDTH = ML_HEADS * ML_DIM
MLA_HEADS, MLA_NOPE, MLA_ROPE, MLA_V = 8, 64, 32, 64
MLA_QK = MLA_NOPE + MLA_ROPE
MLA_Q_RANK, MLA_KV_RANK = 256, 128
ROPE_THETA = 10000.0
SG_GROUPS, SG_DIM, SG_CHUNK = 4, 128, 128
SG_WIDTH = SG_GROUPS * SG_DIM
N_BRANCH = 3
N_EXPERTS, TOP_K, D_EXPERT = 32, 4, 1024
SWIGLU_LIMIT, SWIGLU_ALPHA = 7.0, 1.702

N_CTX = N_SEQ_CTX * S_CTX
N_SMP = N_SEQ_SMP * S_SMP
N_TOK = N_CTX + N_SMP

LANES = 128
SUBLANES = 8
VMEM_LIMIT = 56 * 1024 * 1024

TW = 1024
TW_IN = 512
N_TILES = N_TOK // TW
N_TILES_CTX = N_CTX // TW
TILES_PER_SMP_SEQ = S_SMP // TW
N_SEQ_BLOCKS = N_TOK // S_CTX
HEAD_PAD = LANES
TQ = 256
EXPERT_CAP = N_TOK
SLOT_CHUNK = 128
FFN_CHUNKS = 4
FFN_BLOCK = FFN_CHUNKS * SLOT_CHUNK
N_CHUNK_STEPS = N_TOK * TOP_K // FFN_BLOCK + N_EXPERTS
SC_ROW = 256
SC_SPLIT = D_MODEL // (2 * SC_ROW)
SC_WIN = 128

ZA_W = 4 * ML_WIDTH
ZS_W = 512
ZC_W = 2 * SG_WIDTH
ZB_W = N_BRANCH * D_MODEL
ZIN_W = ZA_W + ZS_W + ZC_W + ZB_W
GATE_LANE0 = MLA_ROPE


def _cparams(sem):
    return pltpu.CompilerParams(dimension_semantics=sem, vmem_limit_bytes=VMEM_LIMIT)


def _mod_row(i, tile=None):
    tile = tile or TW
    return jnp.where(i < N_CTX // tile, 0, 1 + (i - N_CTX // tile) // (S_SMP // tile))


def _rms(x, g, n=None):
    ms = jnp.sum(x * x, axis=-1, keepdims=True) * (1.0 / (n or x.shape[-1]))
    return x * lax.rsqrt(ms + EPS) * g


def _gelu(x):
    return 0.5 * x * (1.0 + jnp.tanh(0.7978845608028654 * (x + 0.044715 * (x * x * x))))


def _pack_pairs(lo, hi):
    lo_bits = lax.bitcast_convert_type(lo.astype(BF16).astype(F32), jnp.uint32)
    hi_bits = lax.bitcast_convert_type(hi.astype(BF16).astype(F32), jnp.uint32)
    return (lo_bits >> 16) | (hi_bits & jnp.uint32(0xFFFF0000))


def _unpack_pairs(u):
    return (lax.bitcast_convert_type(u << 16, F32), lax.bitcast_convert_type(u & jnp.uint32(0xFFFF0000), F32))


def _split3(x):
    x1 = x.astype(BF16)
    r = x - x1.astype(F32)
    x2 = r.astype(BF16)
    return x1, x2, (r - x2.astype(F32)).astype(BF16)


def _dot(a, b):
    return jnp.dot(a, b, preferred_element_type=F32)


def _dot_nt(a, b):
    return lax.dot_general(a, b, (((1,), (1,)), ((), ())), preferred_element_type=F32)


def _adaln_kernel(c_ref, w_ref, b_ref, o_ref):
    c = c_ref[...]
    s = c * jax.nn.sigmoid(c)
    w = w_ref[0]
    s_hi, w_hi = s.astype(BF16), w.astype(BF16)
    s_lo, w_lo = (s - s_hi.astype(F32)).astype(BF16), (w - w_hi.astype(F32)).astype(BF16)
    o_ref[0] = _dot(s_hi, w_hi) + _dot(s_hi, w_lo) + _dot(s_lo, w_hi) + b_ref[0]


def _adaln(cvec, w_ada, b_ada):
    nchunk = 4
    cw = 6 * D_MODEL // nchunk
    return pl.pallas_call(
        _adaln_kernel,
        grid=(DEPTH, nchunk),
        in_specs=[
            pl.BlockSpec((SUBLANES, D_MODEL), lambda l, j: (0, 0)),
            pl.BlockSpec((1, D_MODEL, cw), lambda l, j: (l, 0, j)),
            pl.BlockSpec((1, 1, cw), lambda l, j: (l, 0, j)),
        ],
        out_specs=pl.BlockSpec((1, SUBLANES, cw), lambda l, j: (l, 0, j)),
        out_shape=jax.ShapeDtypeStruct((DEPTH, SUBLANES, 6 * D_MODEL), F32),
        compiler_params=_cparams(("arbitrary", "arbitrary")),
        name="adaln",
    )(cvec, w_ada, b_ada.reshape(DEPTH, 1, 6 * D_MODEL))


IN_SPLITS = (ML_WIDTH, ML_WIDTH, ML_WIDTH, ML_WIDTH, 4 * ML_HEADS, MLA_Q_RANK, MLA_KV_RANK, MLA_ROPE, SG_WIDTH, SG_WIDTH,
             N_BRANCH * D_MODEL)
IN_OFFS = tuple(int(v) for v in np.cumsum((0,) + IN_SPLITS))
D_IN = IN_OFFS[-1]
W_PREP_ROWS = 256
W_PREP_COLS = 512


def _w_in_prep_kernel(wt_ref, o_ref):
    o = IN_OFFS

    def put(c0, rows):
        o_ref[0, :, c0:c0 + W_PREP_COLS] = rows.T.astype(BF16)

    for c0 in range(0, ZA_W, W_PREP_COLS):
        scale = ML_DIM ** -0.5 if o[1] <= c0 < o[2] else 1.0
        put(c0, wt_ref[0, c0:c0 + W_PREP_COLS, :] * scale)
    pad = jnp.zeros((ZS_W - (o[8] - o[4]), W_PREP_ROWS), F32)
    put(ZA_W, jnp.concatenate([wt_ref[0, o[5]:o[8], :], wt_ref[0, o[4]:o[5], :], pad], axis=0))
    for c0 in range(ZA_W + ZS_W, ZIN_W, W_PREP_COLS):
        src = c0 - (ZA_W + ZS_W) + o[8]
        put(c0, wt_ref[0, src:src + W_PREP_COLS, :])


def _w_in_prep(w_in):
    return pl.pallas_call(
        _w_in_prep_kernel,
        grid=(DEPTH, D_MODEL // W_PREP_ROWS),
        in_specs=[pl.BlockSpec((1, D_IN, W_PREP_ROWS), lambda l, r: (l, 0, r))],
        out_specs=pl.BlockSpec((1, W_PREP_ROWS, ZIN_W), lambda l, r: (l, r, 0)),
        out_shape=jax.ShapeDtypeStruct((DEPTH, D_MODEL, ZIN_W), BF16),
        compiler_params=_cparams(("arbitrary", "arbitrary")),
        name="w_in_prep",
    )(jnp.swapaxes(w_in, 1, 2))


def _tok_specs(x, tile):
    if isinstance(x, tuple):
        n_ctx = N_CTX // tile
        return [pl.BlockSpec((tile, D_MODEL), lambda i: (jnp.minimum(i, n_ctx - 1), 0)),
                pl.BlockSpec((tile, D_MODEL), lambda i: (jnp.maximum(i - n_ctx, 0), 0))], list(x)
    return [pl.BlockSpec((tile, D_MODEL), lambda i: (i, 0))], [x]


def _tok_value(x_refs, tile):
    if len(x_refs) == 2:
        return jnp.where(pl.program_id(0) < N_CTX // tile, x_refs[0][...], x_refs[1][...])
    return x_refs[0][...]


def _inproj_kernel(n_x, *refs):
    g_ref, sh_ref, sc_ref, w_ref, za_ref, zs_ref, zc_ref, zb_ref = refs[n_x:n_x + 8]
    x = _tok_value(refs[:n_x], TW_IN)
    if n_x == 2:
        refs[n_x + 8][...] = x
    h = _rms(x, g_ref[0]) * (1.0 + sc_ref[0, 0]) + sh_ref[0, 0]
    hb = h.astype(BF16)
    za_ref[...] = _dot(hb, w_ref[0, :, 0:ZA_W]).astype(BF16)
    zs_ref[...] = _dot(hb, w_ref[0, :, ZA_W:ZA_W + ZS_W])
    zc_ref[...] = _dot(hb, w_ref[0, :, ZA_W + ZS_W:ZA_W + ZS_W + ZC_W]).astype(BF16)
    zb_ref[...] = _dot(hb, w_ref[0, :, ZA_W + ZS_W + ZC_W:ZIN_W]).astype(BF16)


def _mod_spec(layer, k, tile=None):
    return pl.BlockSpec((1, 1, 1, D_MODEL), lambda i: (layer, _mod_row(i, tile), 0, k))


def _inproj(layer, x, norm1, mod, w_in_r):
    tok = lambda w: pl.BlockSpec((TW_IN, w), lambda i: (i, 0))
    x_specs, x_args = _tok_specs(x, TW_IN)
    return pl.pallas_call(
        functools.partial(_inproj_kernel, len(x_args)),
        grid=(N_TOK // TW_IN,),
        in_specs=x_specs + [
            pl.BlockSpec((1, 1, D_MODEL), lambda i: (layer, 0, 0)),
            _mod_spec(layer, 0, TW_IN),
            _mod_spec(layer, 1, TW_IN),
            pl.BlockSpec((1, D_MODEL, ZIN_W), lambda i: (layer, 0, 0)),
        ],
        out_specs=[tok(ZA_W), tok(ZS_W), tok(ZC_W), tok(ZB_W)] + [tok(D_MODEL)] * (len(x_args) - 1),
        out_shape=[
            jax.ShapeDtypeStruct((N_TOK, ZA_W), BF16),
            jax.ShapeDtypeStruct((N_TOK, ZS_W), F32),
            jax.ShapeDtypeStruct((N_TOK, ZC_W), BF16),
            jax.ShapeDtypeStruct((N_TOK, ZB_W), BF16),
        ] + [jax.ShapeDtypeStruct((N_TOK, D_MODEL), F32)] * (len(x_args) - 1),
        compiler_params=_cparams(("arbitrary",)),
        name="inproj",
    )(*x_args, norm1, mod, mod, w_in_r)


def _make_mlstm_kernel(seq, layer, has_init):
    nq = seq // TQ
    lane_if, lane_ff, lane_ib, lane_fb = (GATE_LANE0 + ML_HEADS * j for j in range(4))

    def kern(*refs):
        if has_init:
            body(*refs)
            return
        out = refs[10]
        b = pl.program_id(0)

        @pl.when(b < N_SEQ_CTX)
        def _():
            body(*refs)

        @pl.when(b >= N_SEQ_CTX)
        def _():
            out[...] = jnp.zeros_like(out)

    def body(*refs):
        if has_init:
            m0_ref, zq, zk, zv, zo, gz, bg, nrm, c0_ref, n0_ref, _, out, bp_scr, bs_scr = refs
        else:
            zq, zk, zv, zo, gz, bg, nrm, _, _, _, out, cf_ref, nf_ref, mf_ref, bp_scr, bs_scr = refs
        b = pl.program_id(0)
        g = gz[...] + bg[0]
        lane = lax.broadcasted_iota(jnp.int32, g.shape, 1)
        is_forget = ((lane >= lane_ff) & (lane < lane_ib)) | ((lane >= lane_fb) & (lane < lane_fb + ML_HEADS))
        log_sig = jnp.minimum(g, 0.0) - jnp.log1p(jnp.exp(-jnp.abs(g)))
        a = jnp.where(is_forget, log_sig, g)
        r_i = lax.broadcasted_iota(jnp.int32, (seq, seq), 0)
        c_i = lax.broadcasted_iota(jnp.int32, (seq, seq), 1)
        ltri = jnp.where(c_i <= r_i, 1.0, 0.0).astype(BF16)
        bp = functools.reduce(jnp.add, [_dot(ltri, t) for t in _split3(a)])
        bs = bp[seq - 1:seq, :] - bp + a
        bp_scr[...] = bp
        bs_scr[...] = bs
        eye = jnp.where(lax.broadcasted_iota(jnp.int32, (LANES, LANES), 0)
                        == lax.broadcasted_iota(jnp.int32, (LANES, LANES), 1), 1.0, 0.0).astype(BF16)
        tr = lambda x: functools.reduce(jnp.add, [_dot_nt(eye, t) for t in _split3(x)])
        if has_init:
            tr = lambda x: x.T
        a_t, bp_t, bs_t = tr(a), tr(bp), tr(bs)

        for h in range(ML_HEADS):
            hs = slice(h * ML_DIM, (h + 1) * ML_DIM)
            k = zk[:, hs]
            v = zv[:, hs]
            first_lane = lax.broadcasted_iota(jnp.int32, (seq, ML_DIM), 1) == 0
            v_aug = jnp.concatenate([v, jnp.where(first_lane, 1.0, 0.0).astype(BF16)], axis=1)
            rows = (
                a_t[lane_if + h:lane_if + h + 1, :] - bp_t[lane_ff + h:lane_ff + h + 1, :],
                a_t[lane_ib + h:lane_ib + h + 1, :] - bs_t[lane_fb + h:lane_fb + h + 1, :],
            )
            col_refs = ((bp_scr, lane_ff + h), (bs_scr, lane_fb + h))
            if has_init:
                m0 = tuple(m0_ref[((b * DEPTH + layer) * 2 + dr) * ML_HEADS + h] for dr in range(2))
                c0 = tuple(c0_ref[0, 0, dr, h].astype(BF16) for dr in range(2))
                n0 = tuple(jnp.broadcast_to(n0_ref[0, 0, dr, h:h + 1, :], (ML_DIM, ML_DIM)).astype(BF16) for dr in range(2))
            else:
                m0 = (0.0, 0.0)

            def qblock(qi, carry):
                q0 = pl.multiple_of(qi * TQ, TQ)
                qb = zq[pl.ds(q0, TQ), hs]
                sc = _dot_nt(qb, k)
                t_idx = q0 + lax.broadcasted_iota(jnp.int32, (TQ, seq), 0)
                s_idx = lax.broadcasted_iota(jnp.int32, (TQ, seq), 1)
                hsum = jnp.zeros((TQ, ML_DIM), F32)
                for dr in range(2):
                    cref, cl = col_refs[dr]
                    col = cref[pl.ds(q0, TQ), cl:cl + 1]
                    mask = (s_idx <= t_idx) if dr == 0 else (s_idx >= t_idx)
                    drow = jnp.where(mask, rows[dr], -jnp.inf)
                    c_t = jnp.maximum(m0[dr], jnp.max(drow, axis=1, keepdims=True))
                    s = sc * jnp.exp(drow - c_t)
                    na = _dot(s.astype(BF16), v_aug)
                    num, den = na[:, 0:ML_DIM], na[:, ML_DIM:ML_DIM + 1]
                    if has_init:
                        w_c = jnp.exp(m0[dr] - c_t)
                        num = num + w_c * _dot(qb, c0[dr])
                        den = den + w_c * _dot_nt(qb, n0[dr])[:, 0:1]
                    hsum = hsum + num / jnp.maximum(jnp.abs(den), jnp.exp(-(col + c_t)))
                hn = _rms(hsum, nrm[0][:, hs])
                og = zo[pl.ds(q0, TQ), hs].astype(F32)
                out[pl.ds(q0, TQ), hs] = (hn * jax.nn.sigmoid(og)).astype(out.dtype)
                return carry

            if nq == 1:
                qblock(0, 0)
            else:
                lax.fori_loop(0, nq, qblock, 0)

            if not has_init:
                k_t = _dot_nt(eye, k)
                kf = k.astype(F32)
                tot = (bp_t[lane_ff + h:lane_ff + h + 1, seq - 1:seq], bp_t[lane_fb + h:lane_fb + h + 1, seq - 1:seq])
                gl = (
                    tot[0] + rows[0],
                    bp_t[lane_fb + h:lane_fb + h + 1, :] - a_t[lane_fb + h:lane_fb + h + 1, :]
                    + a_t[lane_ib + h:lane_ib + h + 1, :],
                )
                for dr in range(2):
                    m_new = jnp.maximum(tot[dr] + m0[dr], jnp.max(gl[dr], axis=1, keepdims=True))
                    w_s = jnp.exp(gl[dr] - m_new)
                    cf_ref[0, 0, dr, h] = _dot((k_t * w_s).astype(BF16), v)
                    n_new = jnp.dot(jnp.broadcast_to(w_s, (SUBLANES, seq)), kf, precision=HI, preferred_element_type=F32)
                    nf_ref[0, 0, dr, h:h + 1, :] = n_new[0:1, :]
                    mf_ref[0, 0, dr * ML_HEADS + h:dr * ML_HEADS + h + 1, :] = jnp.broadcast_to(m_new, (1, LANES))

    return kern


def _mlstm(layer, za, zs, b_gates, mlstm_norm, init=None, ctx_out=None, states=None):
    has_init = init is not None
    seq, nseq, row0 = (S_SMP, N_SEQ_SMP, N_CTX // S_SMP) if has_init else (S_CTX, N_SEQ_CTX, 0)
    in_row = lambda b: row0 + jnp.minimum(b, nseq - 1)
    qkvo = [pl.BlockSpec((seq, ML_WIDTH), functools.partial(lambda j, b: (in_row(b), j), j)) for j in range(4)]
    in_specs = qkvo + [
        pl.BlockSpec((seq, LANES), lambda b: (in_row(b), ZS_W // LANES - 1)),
        pl.BlockSpec((1, 1, LANES), lambda b: (layer, 0, 0)),
        pl.BlockSpec((1, 1, ML_WIDTH), lambda b: (layer, 0, 0)),
    ]
    args = [za, za, za, za, zs, b_gates, mlstm_norm]
    out_specs = [pl.BlockSpec((seq, ML_WIDTH), lambda b: (row0 + b, 0))]
    out_shape = [jax.ShapeDtypeStruct((N_TOK, ML_WIDTH), BF16)]
    aliases = {}
    if has_init:
        st_c, st_n, st_m = init
        in_specs = [pl.BlockSpec(memory_space=pltpu.SMEM)] + in_specs + [
            pl.BlockSpec((1, 1, 2, ML_HEADS, ML_DIM, ML_DIM), lambda b: (b, layer, 0, 0, 0, 0)),
            pl.BlockSpec((1, 1, 2, ML_HEADS, ML_DIM), lambda b: (b, layer, 0, 0, 0)),
            pl.BlockSpec(memory_space=pl.ANY),
        ]
        args = [st_m.reshape(-1)] + args + [st_c, st_n, ctx_out]
        aliases = {len(args) - 1: 0}
    else:
        seq_blk = lambda b: jnp.minimum(b, nseq - 1)
        in_specs += [pl.BlockSpec(memory_space=pl.ANY)] * 3
        args += list(states)
        aliases = {len(args) - 3 + j: 1 + j for j in range(3)}
        out_specs += [
            pl.BlockSpec((1, 1, 2, ML_HEADS, ML_DIM, ML_DIM), lambda b: (seq_blk(b), layer, 0, 0, 0, 0)),
            pl.BlockSpec((1, 1, 2, ML_HEADS, ML_DIM), lambda b: (seq_blk(b), layer, 0, 0, 0)),
            pl.BlockSpec((1, 1, 2 * ML_HEADS, LANES), lambda b: (seq_blk(b), layer, 0, 0)),
        ]
        out_shape += [jax.ShapeDtypeStruct(s.shape, s.dtype) for s in states]
    return pl.pallas_call(
        _make_mlstm_kernel(seq, layer, has_init),
        grid=(nseq if has_init else N_SEQ_BLOCKS,),
        in_specs=in_specs,
        out_specs=out_specs,
        out_shape=out_shape,
        scratch_shapes=[pltpu.VMEM((seq, LANES), F32), pltpu.VMEM((seq, LANES), F32)],
        input_output_aliases=aliases,
        compiler_params=_cparams(("arbitrary",)),
        name="mlstm_smp" if has_init else "mlstm_ctx",
    )(*args)


def _rope(x, cos, sin_a, sin_b):
    return x * cos + pltpu.roll(x, LANES - 8, 1) * sin_a + pltpu.roll(x, 8, 1) * sin_b


def _mla_prep_kernel(zs_ref, qa_ref, kva_ref, wuq_ref, wk_ref, wv_ref, qn_ref, kn_ref, cos_ref, sa_ref, sb_ref, _, __,
                     q_ref, kk_ref, v_ref, ckv_ref, kr_ref, qf_scr, kf_scr):
    cq = zs_ref[:, 0:MLA_Q_RANK]
    ckv = zs_ref[:, MLA_Q_RANK:MLA_Q_RANK + MLA_KV_RANK]
    last = zs_ref[:, ZS_W - LANES:ZS_W]
    qf_scr[...] = _dot(_rms(cq, qa_ref[0]).astype(BF16), wuq_ref[0])
    ckvn = _rms(ckv, kva_ref[0])

    @pl.when(pl.program_id(0) < N_TILES_CTX)
    def _():
        for j in range(TW // S_CTX):
            ckv_ref[j, 0] = ckvn[j * S_CTX:(j + 1) * S_CTX, :]
            kr_ref[j, 0] = last[j * S_CTX:(j + 1) * S_CTX, :].T[0:MLA_ROPE, :]

    cb = ckvn.astype(BF16)
    kf_scr[...] = _dot(cb, wk_ref[0])
    v_ref[...] = _dot(cb, wv_ref[0]).astype(BF16)
    lane = lax.broadcasted_iota(jnp.int32, last.shape, 1)
    kr = jnp.where((lane >= MLA_NOPE) & (lane < MLA_QK), pltpu.roll(last, MLA_NOPE, 1), 0.0)
    is_latent = pl.program_id(0) >= N_TILES_CTX

    def heads(rotate):
        for h in range(MLA_HEADS):
            hs = slice(h * HEAD_PAD, (h + 1) * HEAD_PAD)
            q_ref[:, hs] = rotate(_rms(qf_scr[:, hs], qn_ref[0], n=MLA_QK)).astype(BF16)
            kk_ref[:, hs] = rotate(_rms(kf_scr[:, hs] + kr, kn_ref[0], n=MLA_QK)).astype(BF16)

    @pl.when(is_latent)
    def _():
        cos, sa, sb = cos_ref[...], sa_ref[...], sb_ref[...]
        heads(lambda x: _rope(x, cos, sa, sb))

    @pl.when(jnp.logical_not(is_latent))
    def _():
        heads(lambda x: x)


def _mla_prep(layer, zs, q_a_norm, kv_a_norm, w_uq_r, w_k_r, w_v_r, q_norm_p, k_norm_p, rope_tabs, new_ckv, new_kr):
    seq_blk = lambda r, c: pl.BlockSpec((TW // S_CTX, 1, r, c), lambda i: (jnp.minimum(i, N_TILES_CTX - 1), layer, 0, 0))
    lw = lambda shape: pl.BlockSpec((1,) + shape, lambda i: (layer,) + (0,) * len(shape))
    tab = pl.BlockSpec((TW, LANES), lambda i: (jnp.where(i < N_TILES_CTX, 0, 1 + (i - N_TILES_CTX) % TILES_PER_SMP_SEQ), 0))
    tok = lambda w: pl.BlockSpec((TW, w), lambda i: (i, 0))
    return pl.pallas_call(
        _mla_prep_kernel,
        grid=(N_TILES,),
        in_specs=[
            tok(ZS_W), lw((1, MLA_Q_RANK)), lw((1, MLA_KV_RANK)),
            lw((MLA_Q_RANK, MLA_HEADS * HEAD_PAD)), lw((MLA_KV_RANK, MLA_HEADS * HEAD_PAD)),
            lw((MLA_KV_RANK, MLA_HEADS * MLA_V)), lw((1, HEAD_PAD)), lw((1, HEAD_PAD)), tab, tab, tab,
            pl.BlockSpec(memory_space=pl.ANY), pl.BlockSpec(memory_space=pl.ANY),
        ],
        out_specs=[tok(MLA_HEADS * HEAD_PAD), tok(MLA_HEADS * HEAD_PAD), tok(MLA_HEADS * MLA_V),
                   seq_blk(S_CTX, MLA_KV_RANK), seq_blk(MLA_ROPE, S_CTX)],
        input_output_aliases={11: 3, 12: 4},
        out_shape=[
            jax.ShapeDtypeStruct((N_TOK, MLA_HEADS * HEAD_PAD), BF16),
            jax.ShapeDtypeStruct((N_TOK, MLA_HEADS * HEAD_PAD), BF16),
            jax.ShapeDtypeStruct((N_TOK, MLA_HEADS * MLA_V), BF16),
            jax.ShapeDtypeStruct(new_ckv.shape, F32),
            jax.ShapeDtypeStruct(new_kr.shape, F32),
        ],
        scratch_shapes=[pltpu.VMEM((TW, MLA_HEADS * HEAD_PAD), F32), pltpu.VMEM((TW, MLA_HEADS * HEAD_PAD), F32)],
        compiler_params=_cparams(("arbitrary",)),
        name="mla_prep",
    )(zs, q_a_norm, kv_a_norm, w_uq_r, w_k_r, w_v_r, q_norm_p, k_norm_p, *rope_tabs, new_ckv, new_kr)


def _cache_kv_kernel(ckv_ref, kr_ref, wk_ref, wv_ref, kn_ref, kk_ref, v_ref):
    cb = ckv_ref[...].astype(BF16)
    kf = _dot(cb, wk_ref[0])
    v_ref[...] = _dot(cb, wv_ref[0]).astype(BF16)
    kr = kr_ref[...]
    for h in range(MLA_HEADS):
        hs = slice(h * HEAD_PAD, (h + 1) * HEAD_PAD)
        kk_ref[:, hs] = _rms(kf[:, hs] + kr, kn_ref[0], n=MLA_QK).astype(BF16)


def _cache_kv(cache_ckv, cache_kr_pad, w_k_r, w_v_r, k_norm_p):
    lw = lambda shape: pl.BlockSpec((1,) + shape, lambda b, l: (l,) + (0,) * len(shape))
    blk = lambda w: pl.BlockSpec((None, None, PAST_LEN, w), lambda b, l: (b, l, 0, 0))
    return pl.pallas_call(
        _cache_kv_kernel,
        grid=(N_SEQ_SMP, DEPTH),
        in_specs=[blk(MLA_KV_RANK), blk(LANES), lw((MLA_KV_RANK, MLA_HEADS * HEAD_PAD)),
                  lw((MLA_KV_RANK, MLA_HEADS * MLA_V)), lw((1, HEAD_PAD))],
        out_specs=[blk(MLA_HEADS * HEAD_PAD), blk(MLA_HEADS * MLA_V)],
        out_shape=[
            jax.ShapeDtypeStruct((N_SEQ_SMP, DEPTH, PAST_LEN, MLA_HEADS * HEAD_PAD), BF16),
            jax.ShapeDtypeStruct((N_SEQ_SMP, DEPTH, PAST_LEN, MLA_HEADS * MLA_V), BF16),
        ],
        compiler_params=_cparams(("arbitrary", "arbitrary")),
        name="cache_kv",
    )(cache_ckv, cache_kr_pad, w_k_r, w_v_r, k_norm_p)


def _make_attn_kernel(n_src):
    scale = MLA_QK ** -0.5

    def kern(q_ref, *refs):
        o_ref = refs[-1]
        if n_src > 1:
            body(q_ref, *refs)
            return

        @pl.when(pl.program_id(0) < N_SEQ_CTX)
        def _():
            body(q_ref, *refs)

        @pl.when(pl.program_id(0) >= N_SEQ_CTX)
        def _():
            o_ref[...] = jnp.zeros_like(o_ref)

    def body(q_ref, *refs):
        o_ref = refs[-1]
        for h in range(MLA_HEADS):
            hs = slice(h * HEAD_PAD, (h + 1) * HEAD_PAD)
            vs = slice(h * MLA_V, (h + 1) * MLA_V)
            q = q_ref[:, hs]
            ss = [_dot_nt(q, refs[2 * j][:, hs]) * scale for j in range(n_src)]
            m = functools.reduce(jnp.maximum, [jnp.max(s, axis=1, keepdims=True) for s in ss])
            ps = [jnp.exp(s - m) for s in ss]
            l = functools.reduce(jnp.add, [jnp.sum(p, axis=1, keepdims=True) for p in ps])
            o = functools.reduce(jnp.add, [_dot(ps[j].astype(BF16), refs[2 * j + 1][:, vs]) for j in range(n_src)])
            o_ref[:, vs] = (o / l).astype(o_ref.dtype)

    return kern


def _attn_ctx(q, kk, v):
    blk = lambda w: pl.BlockSpec((S_CTX, w), lambda b: (jnp.minimum(b, N_SEQ_CTX - 1), 0))
    return pl.pallas_call(
        _make_attn_kernel(1),
        grid=(N_SEQ_BLOCKS,),
        in_specs=[blk(MLA_HEADS * HEAD_PAD), blk(MLA_HEADS * HEAD_PAD), blk(MLA_HEADS * MLA_V)],
        out_specs=pl.BlockSpec((S_CTX, MLA_HEADS * MLA_V), lambda b: (b, 0)),
        out_shape=jax.ShapeDtypeStruct((N_TOK, MLA_HEADS * MLA_V), BF16),
        compiler_params=_cparams(("arbitrary",)),
        name="attn_ctx",
    )(q, kk, v)


def _attn_smp(layer, q, kk, v, kk_cache, v_cache, ctx_out):
    row0 = N_CTX // S_SMP
    nq = S_SMP // TQ
    seqb = lambda w: pl.BlockSpec((S_SMP, w), lambda b, i: (row0 + b, 0))
    cache = lambda w: pl.BlockSpec((None, None, PAST_LEN, w), lambda b, i: (b, layer, 0, 0))
    return pl.pallas_call(
        _make_attn_kernel(2),
        grid=(N_SEQ_SMP, nq),
        in_specs=[
            pl.BlockSpec((TQ, MLA_HEADS * HEAD_PAD), lambda b, i: (N_CTX // TQ + b * nq + i, 0)),
            seqb(MLA_HEADS * HEAD_PAD), seqb(MLA_HEADS * MLA_V),
            cache(MLA_HEADS * HEAD_PAD), cache(MLA_HEADS * MLA_V),
            pl.BlockSpec(memory_space=pl.ANY),
        ],
        out_specs=pl.BlockSpec((TQ, MLA_HEADS * MLA_V), lambda b, i: (N_CTX // TQ + b * nq + i, 0)),
        out_shape=jax.ShapeDtypeStruct((N_TOK, MLA_HEADS * MLA_V), BF16),
        input_output_aliases={5: 0},
        compiler_params=_cparams(("arbitrary", "arbitrary")),
        name="attn_smp",
    )(q, kk, v, kk_cache, v_cache, ctx_out)


def _merge_kernel(x_ref, oa_ref, ob_ref, zc_ref, zb_ref, sgn_ref, ws_ref, bs_ref, wb_ref, wo_ref, g1_ref, n2_ref,
                  sh2_ref, sc2_ref, wr_ref, br_ref,
                  xmid_ref, h2_ref, dest_ref, wsel_ref, cnt_ref, oc_scr, carry_scr):
    i = pl.program_id(0)

    @pl.when(i == 0)
    def _():
        carry_scr[...] = jnp.zeros_like(carry_scr)

    u = _gelu(zc_ref[:, 0:SG_WIDTH].astype(F32))
    vg = _gelu(zc_ref[:, SG_WIDTH:2 * SG_WIDTH].astype(F32))
    for g in range(SG_GROUPS):
        gs = slice(g * SG_DIM, (g + 1) * SG_DIM)
        vn = _rms(vg[:, gs], sgn_ref[0][:, gs]).astype(BF16)
        for c in range(TW // SG_CHUNK):
            cs = slice(c * SG_CHUNK, (c + 1) * SG_CHUNK)
            mixed = _dot(ws_ref[0, g], vn[cs, :]) + bs_ref[0][:, g:g + 1]
            oc_scr[cs, gs] = (u[cs, gs] * mixed).astype(BF16)

    acc = jnp.zeros((TW, D_MODEL), F32)
    for j, src in enumerate((oa_ref, ob_ref, oc_scr)):
        gate = jax.nn.sigmoid(zb_ref[:, j * D_MODEL:(j + 1) * D_MODEL].astype(F32))
        acc = acc + gate * _dot(src[...], wb_ref[0, j])
    xm = x_ref[...] + g1_ref[0, 0] * _dot(acc.astype(BF16), wo_ref[0])
    xmid_ref[...] = xm
    h2 = _rms(xm, n2_ref[0]) * (1.0 + sc2_ref[0, 0]) + sh2_ref[0, 0]
    for c in range(SC_SPLIT):
        h2_ref[c] = _pack_pairs(h2[:, 2 * c * SC_ROW:(2 * c + 1) * SC_ROW], h2[:, (2 * c + 1) * SC_ROW:(2 * c + 2) * SC_ROW])

    h_hi = h2.astype(BF16)
    h_lo = (h2 - h_hi.astype(F32)).astype(BF16)
    p_hi = _dot(h_hi, wr_ref[0])
    logits = p_hi[:, 0:LANES] + p_hi[:, LANES:2 * LANES] + _dot(h_lo, wr_ref[0, :, 0:LANES]) + br_ref[0]
    lane = lax.broadcasted_iota(jnp.int32, logits.shape, 1)
    hits, exps = [], []
    sel = jnp.zeros(logits.shape, F32)
    denom = jnp.zeros((TW, 1), F32)
    top = None
    for _ in range(TOP_K):
        m = jnp.max(logits, axis=1, keepdims=True)
        idx = jnp.min(jnp.where(logits == m, lane, LANES), axis=1, keepdims=True)
        hit = lane == idx
        top = m if top is None else top
        hits.append(hit)
        exps.append(jnp.exp(m - top))
        sel = jnp.where(hit, 1.0, sel)
        denom = denom + exps[-1]
        logits = jnp.where(hit, -jnp.inf, logits)

    r_i = lax.broadcasted_iota(jnp.int32, (TW, TW), 0)
    c_i = lax.broadcasted_iota(jnp.int32, (TW, TW), 1)
    carry = carry_scr[0:1, :]
    rank = _dot(jnp.where(c_i < r_i, 1.0, 0.0).astype(BF16), sel.astype(BF16)) + carry
    new_carry = carry + jnp.sum(sel, axis=0, keepdims=True)
    carry_scr[...] = jnp.broadcast_to(new_carry, (SUBLANES, LANES))
    cnt_ref[...] = jnp.broadcast_to(new_carry, (SUBLANES, LANES))
    slot = rank + lane.astype(F32) * float(EXPERT_CAP)
    dmat = jnp.zeros(logits.shape, F32)
    wmat = jnp.zeros(logits.shape, F32)
    for k in range(TOP_K):
        dk = jnp.sum(jnp.where(hits[k], slot, 0.0), axis=1, keepdims=True)
        dmat = jnp.where(lane == k, dk, dmat)
        wmat = jnp.where(lane == k, exps[k] / denom, wmat)
    dest_ref[...] = dmat.T[0:SUBLANES, :].astype(jnp.int32)
    wsel_ref[...] = wmat


def _merge(layer, x, oa, ob, zc, zb, sg_norm, w_sp, b_sp, w_branch, w_out, mod, norm2, w_router_p, b_router_p):
    lw = lambda shape: pl.BlockSpec((1,) + shape, lambda i: (layer,) + (0,) * len(shape))
    tok = lambda w: pl.BlockSpec((TW, w), lambda i: (i, 0))
    return pl.pallas_call(
        _merge_kernel,
        grid=(N_TILES,),
        in_specs=[
            tok(D_MODEL), tok(ML_WIDTH), tok(MLA_HEADS * MLA_V), tok(ZC_W), tok(ZB_W),
            lw((1, SG_WIDTH)), lw((SG_GROUPS, SG_CHUNK, SG_CHUNK)), lw((SG_CHUNK, LANES)),
            lw((N_BRANCH, ML_WIDTH, D_MODEL)), lw((D_MODEL, D_MODEL)),
            _mod_spec(layer, 2), lw((1, D_MODEL)), _mod_spec(layer, 3), _mod_spec(layer, 4),
            lw((D_MODEL, 2 * LANES)), lw((1, LANES)),
        ],
        out_specs=[tok(D_MODEL), pl.BlockSpec((SC_SPLIT, TW, SC_ROW), lambda i: (0, i, 0)),
                   pl.BlockSpec((SUBLANES, TW), lambda i: (0, i)), tok(LANES),
                   pl.BlockSpec((SUBLANES, LANES), lambda i: (0, 0))],
        out_shape=[
            jax.ShapeDtypeStruct((N_TOK, D_MODEL), F32),
            jax.ShapeDtypeStruct((SC_SPLIT, N_TOK, SC_ROW), jnp.uint32),
            jax.ShapeDtypeStruct((SUBLANES, N_TOK), jnp.int32),
            jax.ShapeDtypeStruct((N_TOK, LANES), F32),
            jax.ShapeDtypeStruct((SUBLANES, LANES), F32),
        ],
        scratch_shapes=[pltpu.VMEM((TW, SG_WIDTH), BF16), pltpu.VMEM((SUBLANES, LANES), F32)],
        compiler_params=_cparams(("arbitrary",)),
        name="merge_router",
    )(x, oa, ob, zc, zb, sg_norm, w_sp, b_sp, w_branch, w_out, mod, norm2, mod, mod, w_router_p, b_router_p)


def _sc_mesh():
    return plsc.VectorSubcoreMesh(core_axis_name="core", subcore_axis_name="subcore")


def _sc_scatter_rows(x, idxs, n_rows):
    @pl.kernel(out_type=jax.ShapeDtypeStruct((n_rows, SC_ROW), x.dtype), mesh=_sc_mesh(), scratch_types=[])
    def scatter(x_hbm, *refs):
        o_hbm = refs[-1]

        def body(x_vmem, *i_vmems):
            for i_vmem in i_vmems:
                pltpu.sync_copy(x_vmem, o_hbm.at[i_vmem.at[0]])

        pltpu.emit_pipeline(
            body,
            grid=(x.shape[0] // SC_WIN,),
            in_specs=[pl.BlockSpec((SC_WIN, SC_ROW), lambda i: (i, 0))]
            + [pl.BlockSpec((1, SC_WIN), lambda i: (0, i))] * len(idxs),
            out_specs=[],
            core_axis_name=("core", "subcore"),
            dimension_semantics=(pltpu.PARALLEL,),
        )(x_hbm, *refs[:-1])

    return scatter(x, *idxs)


def _sc_gather_rows(x, idx):
    m = idx.shape[1]

    @pl.kernel(out_type=jax.ShapeDtypeStruct((m, SC_ROW), x.dtype), mesh=_sc_mesh())
    def gather(x_hbm, i_hbm, o_hbm):
        def body(i_vmem, o_vmem):
            pltpu.sync_copy(x_hbm.at[i_vmem.at[0]], o_vmem)

        pltpu.emit_pipeline(
            body,
            grid=(m // SC_WIN,),
            in_specs=[pl.BlockSpec((1, SC_WIN), lambda i: (0, i))],
            out_specs=[pl.BlockSpec((SC_WIN, SC_ROW), lambda i: (i, 0))],
            core_axis_name=("core", "subcore"),
            dimension_semantics=(pltpu.PARALLEL,),
        )(i_hbm, o_hbm)

    return gather(x, idx)


STEP_VALID, STEP_FIRST, STEP_HAS_NEXT = 1, 2, 4
STEP_CHUNKS_SHIFT = 3


def _moe_ffn_kernel(layer, be_ref, nx_ref, br_ref, fl_ref, xs_ref, b1_ref, b2_ref, w1_hbm, w2_hbm, y_ref,
                    w1f, w2f, w1b, w2b, sem):
    g = pl.program_id(0)
    flags = fl_ref[g]

    def weight_copies(e):
        return (pltpu.make_async_copy(w1_hbm.at[layer, e], w1f, sem.at[0]),
                pltpu.make_async_copy(w2_hbm.at[layer, e], w2f, sem.at[1]))

    @pl.when(g == 0)
    def _():
        for cp in weight_copies(be_ref[0]):
            cp.start()

    @pl.when((flags & STEP_FIRST) != 0)
    def _():
        for cp in weight_copies(be_ref[g]):
            cp.wait()
        w1b[...] = w1f[...].astype(BF16)
        w2b[...] = w2f[...].astype(BF16)

        @pl.when((flags & STEP_HAS_NEXT) != 0)
        def _():
            for cp in weight_copies(nx_ref[g]):
                cp.start()

    def ffn(n_rows):
        halves = [h.astype(BF16) for c in range(SC_SPLIT) for h in _unpack_pairs(xs_ref[c, 0:n_rows, :])]
        g1 = _dot(jnp.concatenate(halves, axis=1), w1b[...]) + b1_ref[0, 0]
        gate = jnp.minimum(g1[:, :D_EXPERT], SWIGLU_LIMIT)
        up = jnp.clip(g1[:, D_EXPERT:], -SWIGLU_LIMIT, SWIGLU_LIMIT)
        act = gate * jax.nn.sigmoid(SWIGLU_ALPHA * gate) * (up + 1.0)
        y = _dot(act.astype(BF16), w2b[...]) + b2_ref[0, 0]
        for c in range(SC_SPLIT):
            y_ref[c, 0:n_rows, :] = _pack_pairs(
                y[:, 2 * c * SC_ROW:(2 * c + 1) * SC_ROW], y[:, (2 * c + 1) * SC_ROW:(2 * c + 2) * SC_ROW])

    for n_chunks in range(1, FFN_CHUNKS + 1):
        want = STEP_VALID | ((n_chunks - 1) << STEP_CHUNKS_SHIFT)
        pl.when((flags & (STEP_VALID | ((FFN_CHUNKS - 1) << STEP_CHUNKS_SHIFT))) == want)(
            functools.partial(ffn, n_chunks * SLOT_CHUNK))


def _moe_ffn(layer, xs, plan, w1, b1, w2, b2):
    eb = lambda c: pl.BlockSpec((1, 1, 1, c), lambda g, be, nx, br, fl: (layer, be[g], 0, 0))
    rows = pl.BlockSpec((SC_SPLIT, FFN_BLOCK, SC_ROW), lambda g, be, nx, br, fl: (0, br[g], 0))
    hbm = pl.BlockSpec(memory_space=pl.ANY)
    grid_spec = pltpu.PrefetchScalarGridSpec(
        num_scalar_prefetch=4,
        grid=(N_CHUNK_STEPS,),
        in_specs=[rows, eb(2 * D_EXPERT), eb(D_MODEL), hbm, hbm],
        out_specs=rows,
        scratch_shapes=[
            pltpu.VMEM((D_MODEL, 2 * D_EXPERT), F32), pltpu.VMEM((D_EXPERT, D_MODEL), F32),
            pltpu.VMEM((D_MODEL, 2 * D_EXPERT), BF16), pltpu.VMEM((D_EXPERT, D_MODEL), BF16),
            pltpu.SemaphoreType.DMA((2,)),
        ],
    )
    return pl.pallas_call(
        functools.partial(_moe_ffn_kernel, layer),
        grid_spec=grid_spec,
        out_shape=jax.ShapeDtypeStruct(xs.shape, xs.dtype),
        compiler_params=_cparams(("arbitrary",)),
        name="moe_ffn",
    )(*plan, xs, b1, b2, w1, w2)


def _chunk_plan_kernel(cnt_ref, be_ref, nx_ref, br_ref, fl_ref):
    def expert(e, carry):
        step0, prev_first = carry
        c = cnt_ref[e]
        n_blk = (c + FFN_BLOCK - 1) // FFN_BLOCK

        def block(j, _):
            s = step0 + j
            be_ref[s] = e
            nx_ref[s] = e
            br_ref[s] = e * (EXPERT_CAP // FFN_BLOCK) + j
            fl_ref[s] = (STEP_VALID + jnp.where(j == 0, STEP_FIRST, 0)
                         + (jnp.minimum((c - j * FFN_BLOCK + SLOT_CHUNK - 1) // SLOT_CHUNK, FFN_CHUNKS) - 1)
                         * (1 << STEP_CHUNKS_SHIFT))
            return 0

        lax.fori_loop(0, n_blk, block, 0)

        @pl.when((n_blk > 0) & (prev_first >= 0))
        def _():
            nx_ref[prev_first] = e
            fl_ref[prev_first] = fl_ref[prev_first] + STEP_HAS_NEXT

        return step0 + n_blk, jnp.where(n_blk > 0, step0, prev_first)

    used, _ = lax.fori_loop(0, N_EXPERTS, expert, (jnp.int32(0), jnp.int32(-1)))

    def idle(s, _):
        be_ref[s] = be_ref[used - 1]
        nx_ref[s] = be_ref[used - 1]
        br_ref[s] = br_ref[used - 1]
        fl_ref[s] = 0
        return 0

    lax.fori_loop(used, N_CHUNK_STEPS, idle, 0)


def _chunk_plan(cnt):
    smem = pl.BlockSpec(memory_space=pltpu.SMEM)
    return pl.pallas_call(
        _chunk_plan_kernel,
        in_specs=[smem],
        out_specs=[smem] * 4,
        out_shape=[jax.ShapeDtypeStruct((N_CHUNK_STEPS,), jnp.int32)] * 4,
        name="chunk_plan",
    )(cnt)


def _combine_kernel(x_ref, yg_ref, w_ref, g_ref, *o_refs):
    def emit(o_ref):
        w = w_ref[...]
        for c in range(SC_SPLIT):
            parts = [_unpack_pairs(yg_ref[k, c]) for k in range(TOP_K)]
            for half in range(2):
                cs = slice((2 * c + half) * SC_ROW, (2 * c + half + 1) * SC_ROW)
                acc = w[:, 0:1] * parts[0][half]
                for k in range(1, TOP_K):
                    acc = acc + w[:, k:k + 1] * parts[k][half]
                o_ref[:, cs] = x_ref[:, cs] + g_ref[0, 0][:, cs] * acc

    if len(o_refs) == 1:
        emit(o_refs[0])
    else:
        pl.when(pl.program_id(0) < N_TILES_CTX)(lambda: emit(o_refs[0]))
        pl.when(pl.program_id(0) >= N_TILES_CTX)(lambda: emit(o_refs[1]))


def _combine(layer, xmid, yg, wsel, mod, split_out):
    tok = lambda w: pl.BlockSpec((TW, w), lambda i: (i, 0))
    if split_out:
        out_specs = [pl.BlockSpec((TW, D_MODEL), lambda i: (jnp.minimum(i, N_TILES_CTX - 1), 0)),
                     pl.BlockSpec((TW, D_MODEL), lambda i: (jnp.maximum(i - N_TILES_CTX, 0), 0))]
        out_shape = [jax.ShapeDtypeStruct((N_CTX, D_MODEL), F32), jax.ShapeDtypeStruct((N_SMP, D_MODEL), F32)]
    else:
        out_specs, out_shape = tok(D_MODEL), jax.ShapeDtypeStruct((N_TOK, D_MODEL), F32)
    return pl.pallas_call(
        _combine_kernel,
        grid=(N_TILES,),
        in_specs=[tok(D_MODEL), pl.BlockSpec((TOP_K, SC_SPLIT, TW, SC_ROW), lambda i: (0, 0, i, 0)), tok(LANES),
                  _mod_spec(layer, 5)],
        out_specs=out_specs,
        out_shape=out_shape,
        compiler_params=_cparams(("arbitrary",)),
        name="combine",
    )(xmid, yg, wsel, mod)


def _moe(layer, xmid, h2, dest, wsel, cnt, mod, w1, b1, w2, b2, split_out):
    n_slots = N_EXPERTS * EXPERT_CAP
    idx = dest[0:TOP_K][:, None, :] + (jnp.arange(SC_SPLIT, dtype=jnp.int32) * n_slots)[None, :, None]
    idx = idx.reshape(TOP_K, 1, SC_SPLIT * N_TOK)
    xs = _sc_scatter_rows(h2.reshape(SC_SPLIT * N_TOK, SC_ROW), [idx[k] for k in range(TOP_K)], SC_SPLIT * n_slots)
    plan = _chunk_plan(cnt[0, :N_EXPERTS].astype(jnp.int32))
    y = _moe_ffn(layer, xs.reshape(SC_SPLIT, n_slots, SC_ROW), plan, w1, b1, w2, b2)
    yg = _sc_gather_rows(y.reshape(SC_SPLIT * n_slots, SC_ROW), idx.reshape(1, TOP_K * SC_SPLIT * N_TOK))
    return _combine(layer, xmid, yg.reshape(TOP_K, SC_SPLIT, N_TOK, SC_ROW), wsel, mod, split_out)


def _rope_tables():
    pos = np.arange(S_SMP)
    half = MLA_ROPE // 2
    inv_freq = (ROPE_THETA ** (-(np.arange(0, half, 2, dtype=np.float32) / np.float32(half)))).astype(np.float32)
    angs = [((pos // GRID_W).astype(np.float32)[:, None] * inv_freq[None, :]).astype(np.float32),
            ((pos % GRID_W).astype(np.float32)[:, None] * inv_freq[None, :]).astype(np.float32)]
    nf = half // 2
    cos = np.ones((TW + S_SMP, LANES), np.float32)
    sin_a = np.zeros((TW + S_SMP, LANES), np.float32)
    sin_b = np.zeros((TW + S_SMP, LANES), np.float32)
    for axis, ang in enumerate(angs):
        base = MLA_NOPE + axis * half
        c, s = np.cos(ang.astype(np.float64)), np.sin(ang.astype(np.float64))
        cos[TW:, base:base + nf] = c
        cos[TW:, base + nf:base + half] = c
        sin_a[TW:, base:base + nf] = -s
        sin_b[TW:, base + nf:base + half] = s
    return jnp.asarray(cos), jnp.asarray(sin_a), jnp.asarray(sin_b)


def _pad_last(a, width):
    return jnp.pad(a, [(0, 0)] * (a.ndim - 1) + [(0, width - a.shape[-1])])


def kernel(x_prompt, x_sample, cache_mla_ckv, cache_mla_krope, state_mlstm_C, state_mlstm_n, state_mlstm_m, c, c_ctx, norm1, norm2, w_ada, b_ada, w_in, b_mlstm_gates, mlstm_norm, mla_q_a_norm, mla_kv_a_norm, w_uq, w_ukv, mla_q_norm, mla_k_norm, sg_norm, w_spatial, b_spatial, w_branch, w_out, w_router, b_router, w_exp1, b_exp1, w_exp2, b_exp2):
    w_in_r = _w_in_prep(w_in)
    w_uq_r = _pad_last(w_uq.reshape(DEPTH, MLA_Q_RANK, MLA_HEADS, MLA_QK), HEAD_PAD).reshape(
        DEPTH, MLA_Q_RANK, MLA_HEADS * HEAD_PAD).astype(BF16)
    w_ukv4 = w_ukv.reshape(DEPTH, MLA_KV_RANK, MLA_HEADS, MLA_NOPE + MLA_V)
    w_k_r = _pad_last(w_ukv4[..., :MLA_NOPE], HEAD_PAD).reshape(DEPTH, MLA_KV_RANK, MLA_HEADS * HEAD_PAD).astype(BF16)
    w_v_r = w_ukv4[..., MLA_NOPE:].reshape(DEPTH, MLA_KV_RANK, MLA_HEADS * MLA_V).astype(BF16)
    q_norm_p = _pad_last(mla_q_norm, HEAD_PAD).reshape(DEPTH, 1, HEAD_PAD)
    k_norm_p = _pad_last(mla_k_norm, HEAD_PAD).reshape(DEPTH, 1, HEAD_PAD)
    b_gates_p = jnp.pad(b_mlstm_gates, ((0, 0), (GATE_LANE0, LANES - GATE_LANE0 - 4 * ML_HEADS))).reshape(DEPTH, 1, LANES)
    b_sp = _pad_last(jnp.swapaxes(b_spatial, 1, 2), LANES)
    w_router_p = _pad_last(w_router, LANES)
    w_router_hi = w_router_p.astype(BF16)
    w_router_p = jnp.concatenate([w_router_hi, (w_router_p - w_router_hi.astype(F32)).astype(BF16)], axis=-1)
    b_router_p = jnp.pad(b_router, ((0, 0), (0, LANES - N_EXPERTS)), constant_values=-1e30).reshape(DEPTH, 1, LANES)
    r3 = lambda a: a.reshape(DEPTH, 1, a.shape[-1])
    cache_kr_pad = jnp.pad(cache_mla_krope, ((0, 0), (0, 0), (0, 0), (MLA_NOPE, LANES - MLA_QK)))
    rope_tabs = _rope_tables()

    cvec = jnp.concatenate([c_ctx[None, :], c, jnp.zeros((SUBLANES - 1 - N_SEQ_SMP, D_MODEL), F32)], axis=0)
    mod = _adaln(cvec, w_ada, b_ada).reshape(DEPTH, SUBLANES, 1, 6 * D_MODEL)
    b1 = b_exp1.reshape(DEPTH, N_EXPERTS, 1, 2 * D_EXPERT)
    b2 = b_exp2.reshape(DEPTH, N_EXPERTS, 1, D_MODEL)
    kk_cache, v_cache = _cache_kv(cache_mla_ckv, cache_kr_pad, w_k_r, w_v_r, k_norm_p)

    x = (x_prompt.reshape(N_CTX, D_MODEL), x_sample.reshape(N_SMP, D_MODEL))
    new_ckv = jnp.zeros((N_SEQ_CTX, DEPTH, S_CTX, MLA_KV_RANK), F32)
    new_kr = jnp.zeros((N_SEQ_CTX, DEPTH, MLA_ROPE, S_CTX), F32)
    states = (jnp.zeros((N_SEQ_CTX, DEPTH, 2, ML_HEADS, ML_DIM, ML_DIM), F32),
              jnp.zeros((N_SEQ_CTX, DEPTH, 2, ML_HEADS, ML_DIM), F32),
              jnp.zeros((N_SEQ_CTX, DEPTH, 2 * ML_HEADS, LANES), F32))
    for l in range(DEPTH):
        za, zs, zc, zb, *x_joined = _inproj(l, x, r3(norm1), mod, w_in_r)
        x = x_joined[0] if x_joined else x
        oa, *states = _mlstm(l, za, zs, b_gates_p, r3(mlstm_norm), states=states)
        (oa,) = _mlstm(l, za, zs, b_gates_p, r3(mlstm_norm), init=(state_mlstm_C, state_mlstm_n, state_mlstm_m), ctx_out=oa)
        q, kk, v, new_ckv, new_kr = _mla_prep(l, zs, r3(mla_q_a_norm), r3(mla_kv_a_norm), w_uq_r, w_k_r, w_v_r,
                                              q_norm_p, k_norm_p, rope_tabs, new_ckv, new_kr)
        ob = _attn_smp(l, q, kk, v, kk_cache, v_cache, _attn_ctx(q, kk, v))
        xmid, h2, dest, wsel, cnt = _merge(
            l, x, oa, ob, zc, zb, r3(sg_norm), w_spatial.astype(BF16), b_sp, w_branch.astype(BF16), w_out.astype(BF16),
            mod, r3(norm2), w_router_p, b_router_p)
        x = _moe(l, xmid, h2, dest, wsel, cnt, mod, w_exp1, b1, w_exp2, b2, split_out=l == DEPTH - 1)
    y_ctx, y_smp = x
    return (
        y_ctx.reshape(N_SEQ_CTX, S_CTX, D_MODEL),
        y_smp.reshape(N_SEQ_SMP, S_SMP, D_MODEL),
        new_ckv,
        jnp.swapaxes(new_kr, 2, 3),
        states[0],
        states[1],
        states[2][:, :, :, 0].reshape(N_SEQ_CTX, DEPTH, 2, ML_HEADS),
    )
```

```python
import functools

import numpy as np
import jax
import jax.numpy as jnp
from jax import lax
from jax.experimental import pallas as pl
from jax.experimental.pallas import tpu as pltpu
from jax.experimental.pallas import tpu_sc as plsc

F32 = jnp.float32
BF16 = jnp.bfloat16
HI = lax.Precision.HIGHEST

D_MODEL = 1024
N_SEQ_CTX, S_CTX = 32, 256
N_SEQ_SMP, S_SMP = 2, 1024
DEPTH = 4
PAST_LEN = 512
GRID_W = 64
EPS = 1e-6
ML_HEADS, ML_DIM = 4, 128
ML_WIDTH = ML_HEADS * ML_DIM
MLA_HEADS, MLA_NOPE, MLA_ROPE, MLA_V = 8, 64, 32, 64
MLA_QK = MLA_NOPE + MLA_ROPE
MLA_Q_RANK, MLA_KV_RANK = 256, 128
ROPE_THETA = 10000.0
SG_GROUPS, SG_DIM, SG_CHUNK = 4, 128, 128
SG_WIDTH = SG_GROUPS * SG_DIM
N_BRANCH = 3
N_EXPERTS, TOP_K, D_EXPERT = 32, 4, 1024
SWIGLU_LIMIT, SWIGLU_ALPHA = 7.0, 1.702

N_CTX = N_SEQ_CTX * S_CTX
N_SMP = N_SEQ_SMP * S_SMP
N_TOK = N_CTX + N_SMP

LANES = 128
SUBLANES = 8
VMEM_LIMIT = 56 * 1024 * 1024

TW = 1024
TW_IN = 512
N_TILES = N_TOK // TW
N_TILES_CTX = N_CTX // TW
TILES_PER_SMP_SEQ = S_SMP // TW
N_SEQ_BLOCKS = N_TOK // S_CTX
HEAD_PAD = LANES
TQ = 512
EXPERT_CAP = N_TOK
SLOT_CHUNK = 128
FFN_CHUNKS = 4
FFN_BLOCK = FFN_CHUNKS * SLOT_CHUNK
N_CHUNK_STEPS = N_TOK * TOP_K // FFN_BLOCK + N_EXPERTS
SC_ROW = 256
SC_SPLIT = D_MODEL // (2 * SC_ROW)
SC_WIN = 128

ZA_W = 4 * ML_WIDTH
ZS_W = 512
ZC_W = 2 * SG_WIDTH
ZB_W = N_BRANCH * D_MODEL
ZIN_W = ZA_W + ZS_W + ZC_W + ZB_W
GATE_LANE0 = MLA_ROPE


def _cparams(sem):
    return pltpu.CompilerParams(dimension_semantics=sem, vmem_limit_bytes=VMEM_LIMIT)


def _mod_row(i, tile=None):
    tile = tile or TW
    return jnp.where(i < N_CTX // tile, 0, 1 + (i - N_CTX // tile) // (S_SMP // tile))


def _rms(x, g, n=None):
    ms = jnp.sum(x * x, axis=-1, keepdims=True) * (1.0 / (n or x.shape[-1]))
    return x * lax.rsqrt(ms + EPS) * g


def _gelu(x):
    return 0.5 * x * (1.0 + jnp.tanh(0.7978845608028654 * (x + 0.044715 * (x * x * x))))


def _pack_pairs(lo, hi):
    lo_bits = lax.bitcast_convert_type(lo.astype(BF16).astype(F32), jnp.uint32)
    hi_bits = lax.bitcast_convert_type(hi.astype(BF16).astype(F32), jnp.uint32)
    return (lo_bits >> 16) | (hi_bits & jnp.uint32(0xFFFF0000))


def _unpack_pairs(u):
    return (lax.bitcast_convert_type(u << 16, F32), lax.bitcast_convert_type(u & jnp.uint32(0xFFFF0000), F32))


def _split3(x):
    x1 = x.astype(BF16)
    r = x - x1.astype(F32)
    x2 = r.astype(BF16)
    return x1, x2, (r - x2.astype(F32)).astype(BF16)


def _dot(a, b):
    return jnp.dot(a, b, preferred_element_type=F32)


def _dot_nt(a, b):
    return lax.dot_general(a, b, (((1,), (1,)), ((), ())), preferred_element_type=F32)


def _adaln_kernel(c_ref, w_ref, b_ref, o_ref):
    c = c_ref[...]
    s = c * jax.nn.sigmoid(c)
    w = w_ref[0]
    s_hi, w_hi = s.astype(BF16), w.astype(BF16)
    s_lo, w_lo = (s - s_hi.astype(F32)).astype(BF16), (w - w_hi.astype(F32)).astype(BF16)
    o_ref[0] = _dot(s_hi, w_hi) + _dot(s_hi, w_lo) + _dot(s_lo, w_hi) + b_ref[0]


def _adaln(cvec, w_ada, b_ada):
    nchunk = 4
    cw = 6 * D_MODEL // nchunk
    return pl.pallas_call(
        _adaln_kernel,
        grid=(DEPTH, nchunk),
        in_specs=[
            pl.BlockSpec((SUBLANES, D_MODEL), lambda l, j: (0, 0)),
            pl.BlockSpec((1, D_MODEL, cw), lambda l, j: (l, 0, j)),
            pl.BlockSpec((1, 1, cw), lambda l, j: (l, 0, j)),
        ],
        out_specs=pl.BlockSpec((1, SUBLANES, cw), lambda l, j: (l, 0, j)),
        out_shape=jax.ShapeDtypeStruct((DEPTH, SUBLANES, 6 * D_MODEL), F32),
        compiler_params=_cparams(("arbitrary", "arbitrary")),
        name="adaln",
    )(cvec, w_ada, b_ada.reshape(DEPTH, 1, 6 * D_MODEL))


IN_SPLITS = (ML_WIDTH, ML_WIDTH, ML_WIDTH, ML_WIDTH, 4 * ML_HEADS, MLA_Q_RANK, MLA_KV_RANK, MLA_ROPE, SG_WIDTH, SG_WIDTH,
             N_BRANCH * D_MODEL)
IN_OFFS = tuple(int(v) for v in np.cumsum((0,) + IN_SPLITS))
D_IN = IN_OFFS[-1]
W_PREP_ROWS = 256
W_PREP_COLS = 512


def _w_in_prep_kernel(wt_ref, o_ref):
    o = IN_OFFS

    def put(c0, rows):
        o_ref[0, :, c0:c0 + W_PREP_COLS] = rows.T.astype(BF16)

    for c0 in range(0, ZA_W, W_PREP_COLS):
        scale = ML_DIM ** -0.5 if o[1] <= c0 < o[2] else 1.0
        put(c0, wt_ref[0, c0:c0 + W_PREP_COLS, :] * scale)
    pad = jnp.zeros((ZS_W - (o[8] - o[4]), W_PREP_ROWS), F32)
    put(ZA_W, jnp.concatenate([wt_ref[0, o[5]:o[8], :], wt_ref[0, o[4]:o[5], :], pad], axis=0))
    for c0 in range(ZA_W + ZS_W, ZIN_W, W_PREP_COLS):
        src = c0 - (ZA_W + ZS_W) + o[8]
        put(c0, wt_ref[0, src:src + W_PREP_COLS, :])


def _w_in_prep(w_in):
    return pl.pallas_call(
        _w_in_prep_kernel,
        grid=(DEPTH, D_MODEL // W_PREP_ROWS),
        in_specs=[pl.BlockSpec((1, D_IN, W_PREP_ROWS), lambda l, r: (l, 0, r))],
        out_specs=pl.BlockSpec((1, W_PREP_ROWS, ZIN_W), lambda l, r: (l, r, 0)),
        out_shape=jax.ShapeDtypeStruct((DEPTH, D_MODEL, ZIN_W), BF16),
        compiler_params=_cparams(("arbitrary", "arbitrary")),
        name="w_in_prep",
    )(jnp.swapaxes(w_in, 1, 2))


def _tok_specs(x, tile):
    if isinstance(x, tuple):
        n_ctx = N_CTX // tile
        return [pl.BlockSpec((tile, D_MODEL), lambda i: (jnp.minimum(i, n_ctx - 1), 0)),
                pl.BlockSpec((tile, D_MODEL), lambda i: (jnp.maximum(i - n_ctx, 0), 0))], list(x)
    return [pl.BlockSpec((tile, D_MODEL), lambda i: (i, 0))], [x]


def _tok_value(x_refs, tile):
    if len(x_refs) == 2:
        return jnp.where(pl.program_id(0) < N_CTX // tile, x_refs[0][...], x_refs[1][...])
    return x_refs[0][...]


def _inproj_kernel(n_x, *refs):
    g_ref, sh_ref, sc_ref, w_ref, za_ref, zs_ref, zc_ref, zb_ref = refs[n_x:n_x + 8]
    x = _tok_value(refs[:n_x], TW_IN)
    if n_x == 2:
        refs[n_x + 8][...] = x
    h = _rms(x, g_ref[0]) * (1.0 + sc_ref[0, 0]) + sh_ref[0, 0]
    hb = h.astype(BF16)
    za_ref[...] = _dot(hb, w_ref[0, :, 0:ZA_W]).astype(BF16)
    zs_ref[...] = _dot(hb, w_ref[0, :, ZA_W:ZA_W + ZS_W])
    zc_ref[...] = _dot(hb, w_ref[0, :, ZA_W + ZS_W:ZA_W + ZS_W + ZC_W]).astype(BF16)
    zb_ref[...] = _dot(hb, w_ref[0, :, ZA_W + ZS_W + ZC_W:ZIN_W]).astype(BF16)


def _mod_spec(layer, k, tile=None):
    return pl.BlockSpec((1, 1, 1, D_MODEL), lambda i: (layer, _mod_row(i, tile), 0, k))


def _inproj(layer, x, norm1, mod, w_in_r):
    tok = lambda w: pl.BlockSpec((TW_IN, w), lambda i: (i, 0))
    x_specs, x_args = _tok_specs(x, TW_IN)
    return pl.pallas_call(
        functools.partial(_inproj_kernel, len(x_args)),
        grid=(N_TOK // TW_IN,),
        in_specs=x_specs + [
            pl.BlockSpec((1, 1, D_MODEL), lambda i: (layer, 0, 0)),
            _mod_spec(layer, 0, TW_IN),
            _mod_spec(layer, 1, TW_IN),
            pl.BlockSpec((1, D_MODEL, ZIN_W), lambda i: (layer, 0, 0)),
        ],
        out_specs=[tok(ZA_W), tok(ZS_W), tok(ZC_W), tok(ZB_W)] + [tok(D_MODEL)] * (len(x_args) - 1),
        out_shape=[
            jax.ShapeDtypeStruct((N_TOK, ZA_W), BF16),
            jax.ShapeDtypeStruct((N_TOK, ZS_W), F32),
            jax.ShapeDtypeStruct((N_TOK, ZC_W), BF16),
            jax.ShapeDtypeStruct((N_TOK, ZB_W), BF16),
        ] + [jax.ShapeDtypeStruct((N_TOK, D_MODEL), F32)] * (len(x_args) - 1),
        compiler_params=_cparams(("arbitrary",)),
        name="inproj",
    )(*x_args, norm1, mod, mod, w_in_r)


def _make_mlstm_kernel(seq, layer, has_init):
    tq = min(TQ, seq)
    nq = seq // tq
    lane_if, lane_ff, lane_ib, lane_fb = (GATE_LANE0 + ML_HEADS * j for j in range(4))

    def kern(*refs):
        if has_init:
            body(*refs)
            return
        out = refs[10]
        b = pl.program_id(0)

        @pl.when(b < N_SEQ_CTX)
        def _():
            body(*refs)

        @pl.when(b >= N_SEQ_CTX)
        def _():
            out[...] = jnp.zeros_like(out)

    def body(*refs):
        if has_init:
            m0_ref, zq, zk, zv, zo, gz, bg, nrm, c0_ref, n0_ref, _, out, bp_scr, bs_scr = refs
        else:
            zq, zk, zv, zo, gz, bg, nrm, _, _, _, out, cf_ref, nf_ref, mf_ref, bp_scr, bs_scr = refs
        b = pl.program_id(0)
        g = gz[...] + bg[0]
        lane = lax.broadcasted_iota(jnp.int32, g.shape, 1)
        is_forget = ((lane >= lane_ff) & (lane < lane_ib)) | ((lane >= lane_fb) & (lane < lane_fb + ML_HEADS))
        log_sig = jnp.minimum(g, 0.0) - jnp.log1p(jnp.exp(-jnp.abs(g)))
        a = jnp.where(is_forget, log_sig, g)
        r_i = lax.broadcasted_iota(jnp.int32, (seq, seq), 0)
        c_i = lax.broadcasted_iota(jnp.int32, (seq, seq), 1)
        ltri = jnp.where(c_i <= r_i, 1.0, 0.0).astype(BF16)
        bp = functools.reduce(jnp.add, [_dot(ltri, t) for t in _split3(a)])
        bs = bp[seq - 1:seq, :] - bp + a
        bp_scr[...] = bp
        bs_scr[...] = bs
        eye = jnp.where(lax.broadcasted_iota(jnp.int32, (LANES, LANES), 0)
                        == lax.broadcasted_iota(jnp.int32, (LANES, LANES), 1), 1.0, 0.0).astype(BF16)
        tr = lambda x: functools.reduce(jnp.add, [_dot_nt(eye, t) for t in _split3(x)])
        if has_init:
            tr = lambda x: x.T
        a_t, bp_t, bs_t = tr(a), tr(bp), tr(bs)

        for h in range(ML_HEADS):
            hs = slice(h * ML_DIM, (h + 1) * ML_DIM)
            k = zk[:, hs]
            v = zv[:, hs]
            first_lane = lax.broadcasted_iota(jnp.int32, (seq, ML_DIM), 1) == 0
            v_aug = jnp.concatenate([v, jnp.where(first_lane, 1.0, 0.0).astype(BF16)], axis=1)
            rows = (
                a_t[lane_if + h:lane_if + h + 1, :] - bp_t[lane_ff + h:lane_ff + h + 1, :],
                a_t[lane_ib + h:lane_ib + h + 1, :] - bs_t[lane_fb + h:lane_fb + h + 1, :],
            )
            col_refs = ((bp_scr, lane_ff + h), (bs_scr, lane_fb + h))
            if has_init:
                m0 = tuple(m0_ref[((b * DEPTH + layer) * 2 + dr) * ML_HEADS + h] for dr in range(2))
                c0 = tuple(c0_ref[0, 0, dr, h].astype(BF16) for dr in range(2))
                n0 = tuple(jnp.broadcast_to(n0_ref[0, 0, dr, h:h + 1, :], (ML_DIM, ML_DIM)).astype(BF16) for dr in range(2))
            else:
                m0 = (0.0, 0.0)

            def qblock(qi, carry):
                q0 = pl.multiple_of(qi * tq, tq)
                qb = zq[pl.ds(q0, tq), hs]
                sc = _dot_nt(qb, k)
                t_idx = q0 + lax.broadcasted_iota(jnp.int32, (tq, seq), 0)
                s_idx = lax.broadcasted_iota(jnp.int32, (tq, seq), 1)
                hsum = jnp.zeros((tq, ML_DIM), F32)
                for dr in range(2):
                    cref, cl = col_refs[dr]
                    col = cref[pl.ds(q0, tq), cl:cl + 1]
                    mask = (s_idx <= t_idx) if dr == 0 else (s_idx >= t_idx)
                    drow = jnp.where(mask, rows[dr], -jnp.inf)
                    c_t = jnp.maximum(m0[dr], jnp.max(drow, axis=1, keepdims=True))
                    s = sc * jnp.exp(drow - c_t)
                    na = _dot(s.astype(BF16), v_aug)
                    num, den = na[:, 0:ML_DIM], na[:, ML_DIM:ML_DIM + 1]
                    if has_init:
                        w_c = jnp.exp(m0[dr] - c_t)
                        num = num + w_c * _dot(qb, c0[dr])
                        den = den + w_c * _dot_nt(qb, n0[dr])[:, 0:1]
                    hsum = hsum + num / jnp.maximum(jnp.abs(den), jnp.exp(-(col + c_t)))
                hn = _rms(hsum, nrm[0][:, hs])
                og = zo[pl.ds(q0, tq), hs].astype(F32)
                out[pl.ds(q0, tq), hs] = (hn * jax.nn.sigmoid(og)).astype(out.dtype)
                return carry

            if nq == 1:
                qblock(0, 0)
            else:
                lax.fori_loop(0, nq, qblock, 0)

            if not has_init:
                k_t = _dot_nt(eye, k)
                kf = k.astype(F32)
                tot = (bp_t[lane_ff + h:lane_ff + h + 1, seq - 1:seq], bp_t[lane_fb + h:lane_fb + h + 1, seq - 1:seq])
                gl = (
                    tot[0] + rows[0],
                    bp_t[lane_fb + h:lane_fb + h + 1, :] - a_t[lane_fb + h:lane_fb + h + 1, :]
                    + a_t[lane_ib + h:lane_ib + h + 1, :],
                )
                for dr in range(2):
                    m_new = jnp.maximum(tot[dr] + m0[dr], jnp.max(gl[dr], axis=1, keepdims=True))
                    w_s = jnp.exp(gl[dr] - m_new)
                    cf_ref[0, 0, dr, h] = _dot((k_t * w_s).astype(BF16), v)
                    n_new = jnp.dot(jnp.broadcast_to(w_s, (SUBLANES, seq)), kf, precision=HI, preferred_element_type=F32)
                    nf_ref[0, 0, dr, h:h + 1, :] = n_new[0:1, :]
                    mf_ref[0, 0, dr * ML_HEADS + h:dr * ML_HEADS + h + 1, :] = jnp.broadcast_to(m_new, (1, LANES))

    return kern


def _mlstm(layer, za, zs, b_gates, mlstm_norm, init=None, ctx_out=None, states=None):
    has_init = init is not None
    seq, nseq, row0 = (S_SMP, N_SEQ_SMP, N_CTX // S_SMP) if has_init else (S_CTX, N_SEQ_CTX, 0)
    in_row = lambda b: row0 + jnp.minimum(b, nseq - 1)
    qkvo = [pl.BlockSpec((seq, ML_WIDTH), functools.partial(lambda j, b: (in_row(b), j), j)) for j in range(4)]
    in_specs = qkvo + [
        pl.BlockSpec((seq, LANES), lambda b: (in_row(b), ZS_W // LANES - 1)),
        pl.BlockSpec((1, 1, LANES), lambda b: (layer, 0, 0)),
        pl.BlockSpec((1, 1, ML_WIDTH), lambda b: (layer, 0, 0)),
    ]
    args = [za, za, za, za, zs, b_gates, mlstm_norm]
    out_specs = [pl.BlockSpec((seq, ML_WIDTH), lambda b: (row0 + b, 0))]
    out_shape = [jax.ShapeDtypeStruct((N_TOK, ML_WIDTH), BF16)]
    aliases = {}
    if has_init:
        st_c, st_n, st_m = init
        in_specs = [pl.BlockSpec(memory_space=pltpu.SMEM)] + in_specs + [
            pl.BlockSpec((1, 1, 2, ML_HEADS, ML_DIM, ML_DIM), lambda b: (b, layer, 0, 0, 0, 0)),
            pl.BlockSpec((1, 1, 2, ML_HEADS, ML_DIM), lambda b: (b, layer, 0, 0, 0)),
            pl.BlockSpec(memory_space=pl.ANY),
        ]
        args = [st_m.reshape(-1)] + args + [st_c, st_n, ctx_out]
        aliases = {len(args) - 1: 0}
    else:
        seq_blk = lambda b: jnp.minimum(b, nseq - 1)
        in_specs += [pl.BlockSpec(memory_space=pl.ANY)] * 3
        args += list(states)
        aliases = {len(args) - 3 + j: 1 + j for j in range(3)}
        out_specs += [
            pl.BlockSpec((1, 1, 2, ML_HEADS, ML_DIM, ML_DIM), lambda b: (seq_blk(b), layer, 0, 0, 0, 0)),
            pl.BlockSpec((1, 1, 2, ML_HEADS, ML_DIM), lambda b: (seq_blk(b), layer, 0, 0, 0)),
            pl.BlockSpec((1, 1, 2 * ML_HEADS, LANES), lambda b: (seq_blk(b), layer, 0, 0)),
        ]
        out_shape += [jax.ShapeDtypeStruct(s.shape, s.dtype) for s in states]
    return pl.pallas_call(
        _make_mlstm_kernel(seq, layer, has_init),
        grid=(nseq if has_init else N_SEQ_BLOCKS,),
        in_specs=in_specs,
        out_specs=out_specs,
        out_shape=out_shape,
        scratch_shapes=[pltpu.VMEM((seq, LANES), F32), pltpu.VMEM((seq, LANES), F32)],
        input_output_aliases=aliases,
        compiler_params=_cparams(("arbitrary",)),
        name="mlstm_smp" if has_init else "mlstm_ctx",
    )(*args)


def _rope(x, cos, sin_a, sin_b):
    return x * cos + pltpu.roll(x, LANES - 8, 1) * sin_a + pltpu.roll(x, 8, 1) * sin_b


def _mla_prep_kernel(zs_ref, qa_ref, kva_ref, wuq_ref, wk_ref, wv_ref, qn_ref, kn_ref, cos_ref, sa_ref, sb_ref, _, __,
                     q_ref, kk_ref, v_ref, ckv_ref, kr_ref, qf_scr, kf_scr):
    cq = zs_ref[:, 0:MLA_Q_RANK]
    ckv = zs_ref[:, MLA_Q_RANK:MLA_Q_RANK + MLA_KV_RANK]
    last = zs_ref[:, ZS_W - LANES:ZS_W]
    qf_scr[...] = _dot(_rms(cq, qa_ref[0]).astype(BF16), wuq_ref[0])
    ckvn = _rms(ckv, kva_ref[0])

    @pl.when(pl.program_id(0) < N_TILES_CTX)
    def _():
        for j in range(TW // S_CTX):
            ckv_ref[j, 0] = ckvn[j * S_CTX:(j + 1) * S_CTX, :]
            kr_ref[j, 0] = last[j * S_CTX:(j + 1) * S_CTX, :].T[0:MLA_ROPE, :]

    cb = ckvn.astype(BF16)
    kf_scr[...] = _dot(cb, wk_ref[0])
    v_ref[...] = _dot(cb, wv_ref[0]).astype(BF16)
    lane = lax.broadcasted_iota(jnp.int32, last.shape, 1)
    kr = jnp.where((lane >= MLA_NOPE) & (lane < MLA_QK), pltpu.roll(last, MLA_NOPE, 1), 0.0)
    is_latent = pl.program_id(0) >= N_TILES_CTX

    def heads(rotate):
        for h in range(MLA_HEADS):
            hs = slice(h * HEAD_PAD, (h + 1) * HEAD_PAD)
            q_ref[:, hs] = rotate(_rms(qf_scr[:, hs], qn_ref[0], n=MLA_QK)).astype(BF16)
            kk_ref[:, hs] = rotate(_rms(kf_scr[:, hs] + kr, kn_ref[0], n=MLA_QK)).astype(BF16)

    @pl.when(is_latent)
    def _():
        cos, sa, sb = cos_ref[...], sa_ref[...], sb_ref[...]
        heads(lambda x: _rope(x, cos, sa, sb))

    @pl.when(jnp.logical_not(is_latent))
    def _():
        heads(lambda x: x)


def _mla_prep(layer, zs, q_a_norm, kv_a_norm, w_uq_r, w_k_r, w_v_r, q_norm_p, k_norm_p, rope_tabs, new_ckv, new_kr):
    seq_blk = lambda r, c: pl.BlockSpec((TW // S_CTX, 1, r, c), lambda i: (jnp.minimum(i, N_TILES_CTX - 1), layer, 0, 0))
    lw = lambda shape: pl.BlockSpec((1,) + shape, lambda i: (layer,) + (0,) * len(shape))
    tab = pl.BlockSpec((TW, LANES), lambda i: (jnp.where(i < N_TILES_CTX, 0, 1 + (i - N_TILES_CTX) % TILES_PER_SMP_SEQ), 0))
    tok = lambda w: pl.BlockSpec((TW, w), lambda i: (i, 0))
    return pl.pallas_call(
        _mla_prep_kernel,
        grid=(N_TILES,),
        in_specs=[
            tok(ZS_W), lw((1, MLA_Q_RANK)), lw((1, MLA_KV_RANK)),
            lw((MLA_Q_RANK, MLA_HEADS * HEAD_PAD)), lw((MLA_KV_RANK, MLA_HEADS * HEAD_PAD)),
            lw((MLA_KV_RANK, MLA_HEADS * MLA_V)), lw((1, HEAD_PAD)), lw((1, HEAD_PAD)), tab, tab, tab,
            pl.BlockSpec(memory_space=pl.ANY), pl.BlockSpec(memory_space=pl.ANY),
        ],
        out_specs=[tok(MLA_HEADS * HEAD_PAD), tok(MLA_HEADS * HEAD_PAD), tok(MLA_HEADS * MLA_V),
                   seq_blk(S_CTX, MLA_KV_RANK), seq_blk(MLA_ROPE, S_CTX)],
        input_output_aliases={11: 3, 12: 4},
        out_shape=[
            jax.ShapeDtypeStruct((N_TOK, MLA_HEADS * HEAD_PAD), BF16),
            jax.ShapeDtypeStruct((N_TOK, MLA_HEADS * HEAD_PAD), BF16),
            jax.ShapeDtypeStruct((N_TOK, MLA_HEADS * MLA_V), BF16),
            jax.ShapeDtypeStruct(new_ckv.shape, F32),
            jax.ShapeDtypeStruct(new_kr.shape, F32),
        ],
        scratch_shapes=[pltpu.VMEM((TW, MLA_HEADS * HEAD_PAD), F32), pltpu.VMEM((TW, MLA_HEADS * HEAD_PAD), F32)],
        compiler_params=_cparams(("arbitrary",)),
        name="mla_prep",
    )(zs, q_a_norm, kv_a_norm, w_uq_r, w_k_r, w_v_r, q_norm_p, k_norm_p, *rope_tabs, new_ckv, new_kr)


def _cache_kv_kernel(ckv_ref, kr_ref, wk_ref, wv_ref, kn_ref, kk_ref, v_ref):
    cb = ckv_ref[...].astype(BF16)
    kf = _dot(cb, wk_ref[0])
    v_ref[...] = _dot(cb, wv_ref[0]).astype(BF16)
    kr = kr_ref[...]
    for h in range(MLA_HEADS):
        hs = slice(h * HEAD_PAD, (h + 1) * HEAD_PAD)
        kk_ref[:, hs] = _rms(kf[:, hs] + kr, kn_ref[0], n=MLA_QK).astype(BF16)


def _cache_kv(cache_ckv, cache_kr_pad, w_k_r, w_v_r, k_norm_p):
    lw = lambda shape: pl.BlockSpec((1,) + shape, lambda b, l: (l,) + (0,) * len(shape))
    blk = lambda w: pl.BlockSpec((None, None, PAST_LEN, w), lambda b, l: (b, l, 0, 0))
    return pl.pallas_call(
        _cache_kv_kernel,
        grid=(N_SEQ_SMP, DEPTH),
        in_specs=[blk(MLA_KV_RANK), blk(LANES), lw((MLA_KV_RANK, MLA_HEADS * HEAD_PAD)),
                  lw((MLA_KV_RANK, MLA_HEADS * MLA_V)), lw((1, HEAD_PAD))],
        out_specs=[blk(MLA_HEADS * HEAD_PAD), blk(MLA_HEADS * MLA_V)],
        out_shape=[
            jax.ShapeDtypeStruct((N_SEQ_SMP, DEPTH, PAST_LEN, MLA_HEADS * HEAD_PAD), BF16),
            jax.ShapeDtypeStruct((N_SEQ_SMP, DEPTH, PAST_LEN, MLA_HEADS * MLA_V), BF16),
        ],
        compiler_params=_cparams(("arbitrary", "arbitrary")),
        name="cache_kv",
    )(cache_ckv, cache_kr_pad, w_k_r, w_v_r, k_norm_p)


def _make_attn_kernel(n_src):
    scale = MLA_QK ** -0.5

    def kern(q_ref, *refs):
        o_ref = refs[-1]
        if n_src > 1:
            body(q_ref, *refs)
            return

        @pl.when(pl.program_id(0) < N_SEQ_CTX)
        def _():
            body(q_ref, *refs)

        @pl.when(pl.program_id(0) >= N_SEQ_CTX)
        def _():
            o_ref[...] = jnp.zeros_like(o_ref)

    def body(q_ref, *refs):
        o_ref = refs[-1]
        for h in range(MLA_HEADS):
            hs = slice(h * HEAD_PAD, (h + 1) * HEAD_PAD)
            vs = slice(h * MLA_V, (h + 1) * MLA_V)
            q = q_ref[:, hs]
            ss = [_dot_nt(q, refs[2 * j][:, hs]) * scale for j in range(n_src)]
            m = functools.reduce(jnp.maximum, [jnp.max(s, axis=1, keepdims=True) for s in ss])
            ps = [jnp.exp(s - m) for s in ss]
            l = functools.reduce(jnp.add, [jnp.sum(p, axis=1, keepdims=True) for p in ps])
            o = functools.reduce(jnp.add, [_dot(ps[j].astype(BF16), refs[2 * j + 1][:, vs]) for j in range(n_src)])
            o_ref[:, vs] = (o / l).astype(o_ref.dtype)

    return kern


def _attn_ctx(q, kk, v):
    blk = lambda w: pl.BlockSpec((S_CTX, w), lambda b: (jnp.minimum(b, N_SEQ_CTX - 1), 0))
    return pl.pallas_call(
        _make_attn_kernel(1),
        grid=(N_SEQ_BLOCKS,),
        in_specs=[blk(MLA_HEADS * HEAD_PAD), blk(MLA_HEADS * HEAD_PAD), blk(MLA_HEADS * MLA_V)],
        out_specs=pl.BlockSpec((S_CTX, MLA_HEADS * MLA_V), lambda b: (b, 0)),
        out_shape=jax.ShapeDtypeStruct((N_TOK, MLA_HEADS * MLA_V), BF16),
        compiler_params=_cparams(("arbitrary",)),
        name="attn_ctx",
    )(q, kk, v)


def _attn_smp(layer, q, kk, v, kk_cache, v_cache, ctx_out):
    row0 = N_CTX // S_SMP
    nq = S_SMP // TQ
    seqb = lambda w: pl.BlockSpec((S_SMP, w), lambda b, i: (row0 + b, 0))
    cache = lambda w: pl.BlockSpec((None, None, PAST_LEN, w), lambda b, i: (b, layer, 0, 0))
    return pl.pallas_call(
        _make_attn_kernel(2),
        grid=(N_SEQ_SMP, nq),
        in_specs=[
            pl.BlockSpec((TQ, MLA_HEADS * HEAD_PAD), lambda b, i: (N_CTX // TQ + b * nq + i, 0)),
            seqb(MLA_HEADS * HEAD_PAD), seqb(MLA_HEADS * MLA_V),
            cache(MLA_HEADS * HEAD_PAD), cache(MLA_HEADS * MLA_V),
            pl.BlockSpec(memory_space=pl.ANY),
        ],
        out_specs=pl.BlockSpec((TQ, MLA_HEADS * MLA_V), lambda b, i: (N_CTX // TQ + b * nq + i, 0)),
        out_shape=jax.ShapeDtypeStruct((N_TOK, MLA_HEADS * MLA_V), BF16),
        input_output_aliases={5: 0},
        compiler_params=_cparams(("arbitrary", "arbitrary")),
        name="attn_smp",
    )(q, kk, v, kk_cache, v_cache, ctx_out)


def _merge_kernel(x_ref, oa_ref, ob_ref, zc_ref, zb_ref, sgn_ref, ws_ref, bs_ref, wb_ref, wo_ref, g1_ref, n2_ref,
                  sh2_ref, sc2_ref, wr_ref, br_ref,
                  xmid_ref, h2_ref, dest_ref, wsel_ref, cnt_ref, oc_scr, carry_scr):
    i = pl.program_id(0)

    @pl.when(i == 0)
    def _():
        carry_scr[...] = jnp.zeros_like(carry_scr)

    u = _gelu(zc_ref[:, 0:SG_WIDTH].astype(F32))
    vg = _gelu(zc_ref[:, SG_WIDTH:2 * SG_WIDTH].astype(F32))
    for g in range(SG_GROUPS):
        gs = slice(g * SG_DIM, (g + 1) * SG_DIM)
        vn = _rms(vg[:, gs], sgn_ref[0][:, gs]).astype(BF16)
        for c in range(TW // SG_CHUNK):
            cs = slice(c * SG_CHUNK, (c + 1) * SG_CHUNK)
            mixed = _dot(ws_ref[0, g], vn[cs, :]) + bs_ref[0][:, g:g + 1]
            oc_scr[cs, gs] = (u[cs, gs] * mixed).astype(BF16)

    acc = jnp.zeros((TW, D_MODEL), F32)
    for j, src in enumerate((oa_ref, ob_ref, oc_scr)):
        gate = jax.nn.sigmoid(zb_ref[:, j * D_MODEL:(j + 1) * D_MODEL].astype(F32))
        acc = acc + gate * _dot(src[...], wb_ref[0, j])
    xm = x_ref[...] + g1_ref[0, 0] * _dot(acc.astype(BF16), wo_ref[0])
    xmid_ref[...] = xm
    h2 = _rms(xm, n2_ref[0]) * (1.0 + sc2_ref[0, 0]) + sh2_ref[0, 0]
    for c in range(SC_SPLIT):
        h2_ref[c] = _pack_pairs(h2[:, 2 * c * SC_ROW:(2 * c + 1) * SC_ROW], h2[:, (2 * c + 1) * SC_ROW:(2 * c + 2) * SC_ROW])

    h_hi = h2.astype(BF16)
    h_lo = (h2 - h_hi.astype(F32)).astype(BF16)
    p_hi = _dot(h_hi, wr_ref[0])
    logits = p_hi[:, 0:LANES] + p_hi[:, LANES:2 * LANES] + _dot(h_lo, wr_ref[0, :, 0:LANES]) + br_ref[0]
    lane = lax.broadcasted_iota(jnp.int32, logits.shape, 1)
    hits, exps = [], []
    sel = jnp.zeros(logits.shape, F32)
    denom = jnp.zeros((TW, 1), F32)
    top = None
    for _ in range(TOP_K):
        m = jnp.max(logits, axis=1, keepdims=True)
        idx = jnp.min(jnp.where(logits == m, lane, LANES), axis=1, keepdims=True)
        hit = lane == idx
        top = m if top is None else top
        hits.append(hit)
        exps.append(jnp.exp(m - top))
        sel = jnp.where(hit, 1.0, sel)
        denom = denom + exps[-1]
        logits = jnp.where(hit, -jnp.inf, logits)

    r_i = lax.broadcasted_iota(jnp.int32, (TW, TW), 0)
    c_i = lax.broadcasted_iota(jnp.int32, (TW, TW), 1)
    carry = carry_scr[0:1, :]
    rank = _dot(jnp.where(c_i < r_i, 1.0, 0.0).astype(BF16), sel.astype(BF16)) + carry
    new_carry = carry + jnp.sum(sel, axis=0, keepdims=True)
    carry_scr[...] = jnp.broadcast_to(new_carry, (SUBLANES, LANES))
    cnt_ref[...] = jnp.broadcast_to(new_carry, (SUBLANES, LANES))
    slot = rank + lane.astype(F32) * float(EXPERT_CAP)
    dmat = jnp.zeros(logits.shape, F32)
    wmat = jnp.zeros(logits.shape, F32)
    for k in range(TOP_K):
        dk = jnp.sum(jnp.where(hits[k], slot, 0.0), axis=1, keepdims=True)
        dmat = jnp.where(lane == k, dk, dmat)
        wmat = jnp.where(lane == k, exps[k] / denom, wmat)
    dest_ref[...] = dmat.T[0:SUBLANES, :].astype(jnp.int32)
    wsel_ref[...] = wmat


def _merge(layer, x, oa, ob, zc, zb, sg_norm, w_sp, b_sp, w_branch, w_out, mod, norm2, w_router_p, b_router_p):
    lw = lambda shape: pl.BlockSpec((1,) + shape, lambda i: (layer,) + (0,) * len(shape))
    tok = lambda w: pl.BlockSpec((TW, w), lambda i: (i, 0))
    return pl.pallas_call(
        _merge_kernel,
        grid=(N_TILES,),
        in_specs=[
            tok(D_MODEL), tok(ML_WIDTH), tok(MLA_HEADS * MLA_V), tok(ZC_W), tok(ZB_W),
            lw((1, SG_WIDTH)), lw((SG_GROUPS, SG_CHUNK, SG_CHUNK)), lw((SG_CHUNK, LANES)),
            lw((N_BRANCH, ML_WIDTH, D_MODEL)), lw((D_MODEL, D_MODEL)),
            _mod_spec(layer, 2), lw((1, D_MODEL)), _mod_spec(layer, 3), _mod_spec(layer, 4),
            lw((D_MODEL, 2 * LANES)), lw((1, LANES)),
        ],
        out_specs=[tok(D_MODEL), pl.BlockSpec((SC_SPLIT, TW, SC_ROW), lambda i: (0, i, 0)),
                   pl.BlockSpec((SUBLANES, TW), lambda i: (0, i)), tok(LANES),
                   pl.BlockSpec((SUBLANES, LANES), lambda i: (0, 0))],
        out_shape=[
            jax.ShapeDtypeStruct((N_TOK, D_MODEL), F32),
            jax.ShapeDtypeStruct((SC_SPLIT, N_TOK, SC_ROW), jnp.uint32),
            jax.ShapeDtypeStruct((SUBLANES, N_TOK), jnp.int32),
            jax.ShapeDtypeStruct((N_TOK, LANES), F32),
            jax.ShapeDtypeStruct((SUBLANES, LANES), F32),
        ],
        scratch_shapes=[pltpu.VMEM((TW, SG_WIDTH), BF16), pltpu.VMEM((SUBLANES, LANES), F32)],
        compiler_params=_cparams(("arbitrary",)),
        name="merge_router",
    )(x, oa, ob, zc, zb, sg_norm, w_sp, b_sp, w_branch, w_out, mod, norm2, mod, mod, w_router_p, b_router_p)


def _sc_mesh():
    return plsc.VectorSubcoreMesh(core_axis_name="core", subcore_axis_name="subcore")


def _sc_scatter_rows(x, idxs, n_rows):
    @pl.kernel(out_type=jax.ShapeDtypeStruct((n_rows, SC_ROW), x.dtype), mesh=_sc_mesh(), scratch_types=[])
    def scatter(x_hbm, *refs):
        o_hbm = refs[-1]

        def body(x_vmem, *i_vmems):
            for i_vmem in i_vmems:
                pltpu.sync_copy(x_vmem, o_hbm.at[i_vmem.at[0]])

        pltpu.emit_pipeline(
            body,
            grid=(x.shape[0] // SC_WIN,),
            in_specs=[pl.BlockSpec((SC_WIN, SC_ROW), lambda i: (i, 0))]
            + [pl.BlockSpec((1, SC_WIN), lambda i: (0, i))] * len(idxs),
            out_specs=[],
            core_axis_name=("core", "subcore"),
            dimension_semantics=(pltpu.PARALLEL,),
        )(x_hbm, *refs[:-1])

    return scatter(x, *idxs)


def _sc_gather_rows(x, idx):
    m = idx.shape[1]

    @pl.kernel(out_type=jax.ShapeDtypeStruct((m, SC_ROW), x.dtype), mesh=_sc_mesh())
    def gather(x_hbm, i_hbm, o_hbm):
        def body(i_vmem, o_vmem):
            pltpu.sync_copy(x_hbm.at[i_vmem.at[0]], o_vmem)

        pltpu.emit_pipeline(
            body,
            grid=(m // SC_WIN,),
            in_specs=[pl.BlockSpec((1, SC_WIN), lambda i: (0, i))],
            out_specs=[pl.BlockSpec((SC_WIN, SC_ROW), lambda i: (i, 0))],
            core_axis_name=("core", "subcore"),
            dimension_semantics=(pltpu.PARALLEL,),
        )(i_hbm, o_hbm)

    return gather(x, idx)


STEP_VALID, STEP_FIRST, STEP_HAS_NEXT = 1, 2, 4
STEP_CHUNKS_SHIFT = 3


def _moe_ffn_kernel(layer, be_ref, nx_ref, br_ref, fl_ref, xs_ref, b1_ref, b2_ref, w1_hbm, w2_hbm, y_ref,
                    w1f, w2f, w1b, w2b, sem):
    g = pl.program_id(0)
    flags = fl_ref[g]

    def weight_copies(e):
        return (pltpu.make_async_copy(w1_hbm.at[layer, e], w1f, sem.at[0]),
                pltpu.make_async_copy(w2_hbm.at[layer, e], w2f, sem.at[1]))

    @pl.when(g == 0)
    def _():
        for cp in weight_copies(be_ref[0]):
            cp.start()

    @pl.when((flags & STEP_FIRST) != 0)
    def _():
        for cp in weight_copies(be_ref[g]):
            cp.wait()
        w1b[...] = w1f[...].astype(BF16)
        w2b[...] = w2f[...].astype(BF16)

        @pl.when((flags & STEP_HAS_NEXT) != 0)
        def _():
            for cp in weight_copies(nx_ref[g]):
                cp.start()

    def ffn(n_rows):
        halves = [h.astype(BF16) for c in range(SC_SPLIT) for h in _unpack_pairs(xs_ref[c, 0:n_rows, :])]
        g1 = _dot(jnp.concatenate(halves, axis=1), w1b[...]) + b1_ref[0, 0]
        gate = jnp.minimum(g1[:, :D_EXPERT], SWIGLU_LIMIT)
        up = jnp.clip(g1[:, D_EXPERT:], -SWIGLU_LIMIT, SWIGLU_LIMIT)
        act = gate * jax.nn.sigmoid(SWIGLU_ALPHA * gate) * (up + 1.0)
        y = _dot(act.astype(BF16), w2b[...]) + b2_ref[0, 0]
        for c in range(SC_SPLIT):
            y_ref[c, 0:n_rows, :] = _pack_pairs(
                y[:, 2 * c * SC_ROW:(2 * c + 1) * SC_ROW], y[:, (2 * c + 1) * SC_ROW:(2 * c + 2) * SC_ROW])

    for n_chunks in range(1, FFN_CHUNKS + 1):
        want = STEP_VALID | ((n_chunks - 1) << STEP_CHUNKS_SHIFT)
        pl.when((flags & (STEP_VALID | ((FFN_CHUNKS - 1) << STEP_CHUNKS_SHIFT))) == want)(
            functools.partial(ffn, n_chunks * SLOT_CHUNK))


def _moe_ffn(layer, xs, plan, w1, b1, w2, b2):
    eb = lambda c: pl.BlockSpec((1, 1, 1, c), lambda g, be, nx, br, fl: (layer, be[g], 0, 0))
    rows = pl.BlockSpec((SC_SPLIT, FFN_BLOCK, SC_ROW), lambda g, be, nx, br, fl: (0, br[g], 0))
    hbm = pl.BlockSpec(memory_space=pl.ANY)
    grid_spec = pltpu.PrefetchScalarGridSpec(
        num_scalar_prefetch=4,
        grid=(N_CHUNK_STEPS,),
        in_specs=[rows, eb(2 * D_EXPERT), eb(D_MODEL), hbm, hbm],
        out_specs=rows,
        scratch_shapes=[
            pltpu.VMEM((D_MODEL, 2 * D_EXPERT), F32), pltpu.VMEM((D_EXPERT, D_MODEL), F32),
            pltpu.VMEM((D_MODEL, 2 * D_EXPERT), BF16), pltpu.VMEM((D_EXPERT, D_MODEL), BF16),
            pltpu.SemaphoreType.DMA((2,)),
        ],
    )
    return pl.pallas_call(
        functools.partial(_moe_ffn_kernel, layer),
        grid_spec=grid_spec,
        out_shape=jax.ShapeDtypeStruct(xs.shape, xs.dtype),
        compiler_params=_cparams(("arbitrary",)),
        name="moe_ffn",
    )(*plan, xs, b1, b2, w1, w2)


def _chunk_plan_kernel(cnt_ref, be_ref, nx_ref, br_ref, fl_ref):
    def expert(e, carry):
        step0, prev_first = carry
        c = cnt_ref[e]
        n_blk = (c + FFN_BLOCK - 1) // FFN_BLOCK

        def block(j, _):
            s = step0 + j
            be_ref[s] = e
            nx_ref[s] = e
            br_ref[s] = e * (EXPERT_CAP // FFN_BLOCK) + j
            fl_ref[s] = (STEP_VALID + jnp.where(j == 0, STEP_FIRST, 0)
                         + (jnp.minimum((c - j * FFN_BLOCK + SLOT_CHUNK - 1) // SLOT_CHUNK, FFN_CHUNKS) - 1)
                         * (1 << STEP_CHUNKS_SHIFT))
            return 0

        lax.fori_loop(0, n_blk, block, 0)

        @pl.when((n_blk > 0) & (prev_first >= 0))
        def _():
            nx_ref[prev_first] = e
            fl_ref[prev_first] = fl_ref[prev_first] + STEP_HAS_NEXT

        return step0 + n_blk, jnp.where(n_blk > 0, step0, prev_first)

    used, _ = lax.fori_loop(0, N_EXPERTS, expert, (jnp.int32(0), jnp.int32(-1)))

    def idle(s, _):
        be_ref[s] = be_ref[used - 1]
        nx_ref[s] = be_ref[used - 1]
        br_ref[s] = br_ref[used - 1]
        fl_ref[s] = 0
        return 0

    lax.fori_loop(used, N_CHUNK_STEPS, idle, 0)


def _chunk_plan(cnt):
    smem = pl.BlockSpec(memory_space=pltpu.SMEM)
    return pl.pallas_call(
        _chunk_plan_kernel,
        in_specs=[smem],
        out_specs=[smem] * 4,
        out_shape=[jax.ShapeDtypeStruct((N_CHUNK_STEPS,), jnp.int32)] * 4,
        name="chunk_plan",
    )(cnt)


def _combine_kernel(x_ref, yg_ref, w_ref, g_ref, *o_refs):
    def emit(o_ref):
        w = w_ref[...]
        for c in range(SC_SPLIT):
            parts = [_unpack_pairs(yg_ref[k, c]) for k in range(TOP_K)]
            for half in range(2):
                cs = slice((2 * c + half) * SC_ROW, (2 * c + half + 1) * SC_ROW)
                acc = w[:, 0:1] * parts[0][half]
                for k in range(1, TOP_K):
                    acc = acc + w[:, k:k + 1] * parts[k][half]
                o_ref[:, cs] = x_ref[:, cs] + g_ref[0, 0][:, cs] * acc

    if len(o_refs) == 1:
        emit(o_refs[0])
    else:
        pl.when(pl.program_id(0) < N_TILES_CTX)(lambda: emit(o_refs[0]))
        pl.when(pl.program_id(0) >= N_TILES_CTX)(lambda: emit(o_refs[1]))


def _combine(layer, xmid, yg, wsel, mod, split_out):
    tok = lambda w: pl.BlockSpec((TW, w), lambda i: (i, 0))
    if split_out:
        out_specs = [pl.BlockSpec((TW, D_MODEL), lambda i: (jnp.minimum(i, N_TILES_CTX - 1), 0)),
                     pl.BlockSpec((TW, D_MODEL), lambda i: (jnp.maximum(i - N_TILES_CTX, 0), 0))]
        out_shape = [jax.ShapeDtypeStruct((N_CTX, D_MODEL), F32), jax.ShapeDtypeStruct((N_SMP, D_MODEL), F32)]
    else:
        out_specs, out_shape = tok(D_MODEL), jax.ShapeDtypeStruct((N_TOK, D_MODEL), F32)
    return pl.pallas_call(
        _combine_kernel,
        grid=(N_TILES,),
        in_specs=[tok(D_MODEL), pl.BlockSpec((TOP_K, SC_SPLIT, TW, SC_ROW), lambda i: (0, 0, i, 0)), tok(LANES),
                  _mod_spec(layer, 5)],
        out_specs=out_specs,
        out_shape=out_shape,
        compiler_params=_cparams(("arbitrary",)),
        name="combine",
    )(xmid, yg, wsel, mod)


def _moe(layer, xmid, h2, dest, wsel, cnt, mod, w1, b1, w2, b2, split_out):
    n_slots = N_EXPERTS * EXPERT_CAP
    idx = dest[0:TOP_K][:, None, :] + (jnp.arange(SC_SPLIT, dtype=jnp.int32) * n_slots)[None, :, None]
    idx = idx.reshape(TOP_K, 1, SC_SPLIT * N_TOK)
    xs = _sc_scatter_rows(h2.reshape(SC_SPLIT * N_TOK, SC_ROW), [idx[k] for k in range(TOP_K)], SC_SPLIT * n_slots)
    plan = _chunk_plan(cnt[0, :N_EXPERTS].astype(jnp.int32))
    y = _moe_ffn(layer, xs.reshape(SC_SPLIT, n_slots, SC_ROW), plan, w1, b1, w2, b2)
    yg = _sc_gather_rows(y.reshape(SC_SPLIT * n_slots, SC_ROW), idx.reshape(1, TOP_K * SC_SPLIT * N_TOK))
    return _combine(layer, xmid, yg.reshape(TOP_K, SC_SPLIT, N_TOK, SC_ROW), wsel, mod, split_out)


def _rope_tables():
    pos = np.arange(S_SMP)
    half = MLA_ROPE // 2
    inv_freq = (ROPE_THETA ** (-(np.arange(0, half, 2, dtype=np.float32) / np.float32(half)))).astype(np.float32)
    angs = [((pos // GRID_W).astype(np.float32)[:, None] * inv_freq[None, :]).astype(np.float32),
            ((pos % GRID_W).astype(np.float32)[:, None] * inv_freq[None, :]).astype(np.float32)]
    nf = half // 2
    cos = np.ones((TW + S_SMP, LANES), np.float32)
    sin_a = np.zeros((TW + S_SMP, LANES), np.float32)
    sin_b = np.zeros((TW + S_SMP, LANES), np.float32)
    for axis, ang in enumerate(angs):
        base = MLA_NOPE + axis * half
        c, s = np.cos(ang.astype(np.float64)), np.sin(ang.astype(np.float64))
        cos[TW:, base:base + nf] = c
        cos[TW:, base + nf:base + half] = c
        sin_a[TW:, base:base + nf] = -s
        sin_b[TW:, base + nf:base + half] = s
    return jnp.asarray(cos), jnp.asarray(sin_a), jnp.asarray(sin_b)


def _pad_last(a, width):
    return jnp.pad(a, [(0, 0)] * (a.ndim - 1) + [(0, width - a.shape[-1])])


def kernel(x_prompt, x_sample, cache_mla_ckv, cache_mla_krope, state_mlstm_C, state_mlstm_n, state_mlstm_m, c, c_ctx, norm1, norm2, w_ada, b_ada, w_in, b_mlstm_gates, mlstm_norm, mla_q_a_norm, mla_kv_a_norm, w_uq, w_ukv, mla_q_norm, mla_k_norm, sg_norm, w_spatial, b_spatial, w_branch, w_out, w_router, b_router, w_exp1, b_exp1, w_exp2, b_exp2):
    w_in_r = _w_in_prep(w_in)
    w_uq_r = _pad_last(w_uq.reshape(DEPTH, MLA_Q_RANK, MLA_HEADS, MLA_QK), HEAD_PAD).reshape(
        DEPTH, MLA_Q_RANK, MLA_HEADS * HEAD_PAD).astype(BF16)
    w_ukv4 = w_ukv.reshape(DEPTH, MLA_KV_RANK, MLA_HEADS, MLA_NOPE + MLA_V)
    w_k_r = _pad_last(w_ukv4[..., :MLA_NOPE], HEAD_PAD).reshape(DEPTH, MLA_KV_RANK, MLA_HEADS * HEAD_PAD).astype(BF16)
    w_v_r = w_ukv4[..., MLA_NOPE:].reshape(DEPTH, MLA_KV_RANK, MLA_HEADS * MLA_V).astype(BF16)
    q_norm_p = _pad_last(mla_q_norm, HEAD_PAD).reshape(DEPTH, 1, HEAD_PAD)
    k_norm_p = _pad_last(mla_k_norm, HEAD_PAD).reshape(DEPTH, 1, HEAD_PAD)
    b_gates_p = jnp.pad(b_mlstm_gates, ((0, 0), (GATE_LANE0, LANES - GATE_LANE0 - 4 * ML_HEADS))).reshape(DEPTH, 1, LANES)
    b_sp = _pad_last(jnp.swapaxes(b_spatial, 1, 2), LANES)
    w_router_p = _pad_last(w_router, LANES)
    w_router_hi = w_router_p.astype(BF16)
    w_router_p = jnp.concatenate([w_router_hi, (w_router_p - w_router_hi.astype(F32)).astype(BF16)], axis=-1)
    b_router_p = jnp.pad(b_router, ((0, 0), (0, LANES - N_EXPERTS)), constant_values=-1e30).reshape(DEPTH, 1, LANES)
    r3 = lambda a: a.reshape(DEPTH, 1, a.shape[-1])
    cache_kr_pad = jnp.pad(cache_mla_krope, ((0, 0), (0, 0), (0, 0), (MLA_NOPE, LANES - MLA_QK)))
    rope_tabs = _rope_tables()

    cvec = jnp.concatenate([c_ctx[None, :], c, jnp.zeros((SUBLANES - 1 - N_SEQ_SMP, D_MODEL), F32)], axis=0)
    mod = _adaln(cvec, w_ada, b_ada).reshape(DEPTH, SUBLANES, 1, 6 * D_MODEL)
    b1 = b_exp1.reshape(DEPTH, N_EXPERTS, 1, 2 * D_EXPERT)
    b2 = b_exp2.reshape(DEPTH, N_EXPERTS, 1, D_MODEL)
    kk_cache, v_cache = _cache_kv(cache_mla_ckv, cache_kr_pad, w_k_r, w_v_r, k_norm_p)

    x = (x_prompt.reshape(N_CTX, D_MODEL), x_sample.reshape(N_SMP, D_MODEL))
    new_ckv = jnp.zeros((N_SEQ_CTX, DEPTH, S_CTX, MLA_KV_RANK), F32)
    new_kr = jnp.zeros((N_SEQ_CTX, DEPTH, MLA_ROPE, S_CTX), F32)
    states = (jnp.zeros((N_SEQ_CTX, DEPTH, 2, ML_HEADS, ML_DIM, ML_DIM), F32),
              jnp.zeros((N_SEQ_CTX, DEPTH, 2, ML_HEADS, ML_DIM), F32),
              jnp.zeros((N_SEQ_CTX, DEPTH, 2 * ML_HEADS, LANES), F32))
    for l in range(DEPTH):
        za, zs, zc, zb, *x_joined = _inproj(l, x, r3(norm1), mod, w_in_r)
        x = x_joined[0] if x_joined else x
        oa, *states = _mlstm(l, za, zs, b_gates_p, r3(mlstm_norm), states=states)
        (oa,) = _mlstm(l, za, zs, b_gates_p, r3(mlstm_norm), init=(state_mlstm_C, state_mlstm_n, state_mlstm_m), ctx_out=oa)
        q, kk, v, new_ckv, new_kr = _mla_prep(l, zs, r3(mla_q_a_norm), r3(mla_kv_a_norm), w_uq_r, w_k_r, w_v_r,
                                              q_norm_p, k_norm_p, rope_tabs, new_ckv, new_kr)
        ob = _attn_smp(l, q, kk, v, kk_cache, v_cache, _attn_ctx(q, kk, v))
        xmid, h2, dest, wsel, cnt = _merge(
            l, x, oa, ob, zc, zb, r3(sg_norm), w_spatial.astype(BF16), b_sp, w_branch.astype(BF16), w_out.astype(BF16),
            mod, r3(norm2), w_router_p, b_router_p)
        x = _moe(l, xmid, h2, dest, wsel, cnt, mod, w_exp1, b1, w_exp2, b2, split_out=l == DEPTH - 1)
    y_ctx, y_smp = x
    return (
        y_ctx.reshape(N_SEQ_CTX, S_CTX, D_MODEL),
        y_smp.reshape(N_SEQ_SMP, S_SMP, D_MODEL),
        new_ckv,
        jnp.swapaxes(new_kr, 2, 3),
        states[0],
        states[1],
        states[2][:, :, :, 0].reshape(N_SEQ_CTX, DEPTH, 2, ML_HEADS),
    )
```

```python
import functools

import numpy as np
import jax
import jax.numpy as jnp
from jax import lax
from jax.experimental import pallas as pl
from jax.experimental.pallas import tpu as pltpu
from jax.experimental.pallas import tpu_sc as plsc

F32 = jnp.float32
BF16 = jnp.bfloat16
HI = lax.Precision.HIGHEST

D_MODEL = 1024
N_SEQ_CTX, S_CTX = 32, 256
N_SEQ_SMP, S_SMP = 2, 1024
DEPTH = 4
PAST_LEN = 512
GRID_W = 64
EPS = 1e-6
ML_HEADS, ML_DIM = 4, 128
ML_WIDTH = ML_HEADS * ML_DIM
MLA_HEADS, MLA_NOPE, MLA_ROPE, MLA_V = 8, 64, 32, 64
MLA_QK = MLA_NOPE + MLA_ROPE
MLA_Q_RANK, MLA_KV_RANK = 256, 128
ROPE_THETA = 10000.0
SG_GROUPS, SG_DIM, SG_CHUNK = 4, 128, 128
SG_WIDTH = SG_GROUPS * SG_DIM
N_BRANCH = 3
N_EXPERTS, TOP_K, D_EXPERT = 32, 4, 1024
SWIGLU_LIMIT, SWIGLU_ALPHA = 7.0, 1.702

N_CTX = N_SEQ_CTX * S_CTX
N_SMP = N_SEQ_SMP * S_SMP
N_TOK = N_CTX + N_SMP

LANES = 128
SUBLANES = 8
VMEM_LIMIT = 56 * 1024 * 1024

TW = 1024
TW_IN = 512
N_TILES = N_TOK // TW
N_TILES_CTX = N_CTX // TW
TILES_PER_SMP_SEQ = S_SMP // TW
N_SEQ_BLOCKS = N_TOK // S_CTX
HEAD_PAD = LANES
TQ = 512
EXPERT_CAP = N_TOK
SLOT_CHUNK = 128
FFN_CHUNKS = 4
FFN_BLOCK = FFN_CHUNKS * SLOT_CHUNK
N_CHUNK_STEPS = N_TOK * TOP_K // FFN_BLOCK + N_EXPERTS
SC_ROW = 256
SC_SPLIT = D_MODEL // (2 * SC_ROW)
SC_WIN = 128

ZA_W = 4 * ML_WIDTH
ZS_W = 512
ZC_W = 2 * SG_WIDTH
ZB_W = N_BRANCH * D_MODEL
ZIN_W = ZA_W + ZS_W + ZC_W + ZB_W
GATE_LANE0 = MLA_ROPE


def _cparams(sem):
    return pltpu.CompilerParams(dimension_semantics=sem, vmem_limit_bytes=VMEM_LIMIT)


def _mod_row(i, tile=None):
    tile = tile or TW
    return jnp.where(i < N_CTX // tile, 0, 1 + (i - N_CTX // tile) // (S_SMP // tile))


def _rms(x, g, n=None):
    ms = jnp.sum(x * x, axis=-1, keepdims=True) * (1.0 / (n or x.shape[-1]))
    return x * lax.rsqrt(ms + EPS) * g


def _gelu(x):
    return 0.5 * x * (1.0 + jnp.tanh(0.7978845608028654 * (x + 0.044715 * (x * x * x))))


def _pack_pairs(lo, hi):
    lo_bits = lax.bitcast_convert_type(lo.astype(BF16).astype(F32), jnp.uint32)
    hi_bits = lax.bitcast_convert_type(hi.astype(BF16).astype(F32), jnp.uint32)
    return (lo_bits >> 16) | (hi_bits & jnp.uint32(0xFFFF0000))


def _unpack_pairs(u):
    return (lax.bitcast_convert_type(u << 16, F32), lax.bitcast_convert_type(u & jnp.uint32(0xFFFF0000), F32))


def _split3(x):
    x1 = x.astype(BF16)
    r = x - x1.astype(F32)
    x2 = r.astype(BF16)
    return x1, x2, (r - x2.astype(F32)).astype(BF16)


def _dot(a, b):
    return jnp.dot(a, b, preferred_element_type=F32)


def _dot_nt(a, b):
    return lax.dot_general(a, b, (((1,), (1,)), ((), ())), preferred_element_type=F32)


def _adaln_kernel(c_ref, w_ref, b_ref, o_ref):
    c = c_ref[...]
    s = c * jax.nn.sigmoid(c)
    w = w_ref[0]
    s_hi, w_hi = s.astype(BF16), w.astype(BF16)
    s_lo, w_lo = (s - s_hi.astype(F32)).astype(BF16), (w - w_hi.astype(F32)).astype(BF16)
    o_ref[0] = _dot(s_hi, w_hi) + _dot(s_hi, w_lo) + _dot(s_lo, w_hi) + b_ref[0]


def _adaln(cvec, w_ada, b_ada):
    nchunk = 4
    cw = 6 * D_MODEL // nchunk
    return pl.pallas_call(
        _adaln_kernel,
        grid=(DEPTH, nchunk),
        in_specs=[
            pl.BlockSpec((SUBLANES, D_MODEL), lambda l, j: (0, 0)),
            pl.BlockSpec((1, D_MODEL, cw), lambda l, j: (l, 0, j)),
            pl.BlockSpec((1, 1, cw), lambda l, j: (l, 0, j)),
        ],
        out_specs=pl.BlockSpec((1, SUBLANES, cw), lambda l, j: (l, 0, j)),
        out_shape=jax.ShapeDtypeStruct((DEPTH, SUBLANES, 6 * D_MODEL), F32),
        compiler_params=_cparams(("arbitrary", "arbitrary")),
        name="adaln",
    )(cvec, w_ada, b_ada.reshape(DEPTH, 1, 6 * D_MODEL))


IN_SPLITS = (ML_WIDTH, ML_WIDTH, ML_WIDTH, ML_WIDTH, 4 * ML_HEADS, MLA_Q_RANK, MLA_KV_RANK, MLA_ROPE, SG_WIDTH, SG_WIDTH,
             N_BRANCH * D_MODEL)
IN_OFFS = tuple(int(v) for v in np.cumsum((0,) + IN_SPLITS))
D_IN = IN_OFFS[-1]
W_PREP_ROWS = 256
W_PREP_COLS = 512


def _w_in_prep_kernel(wt_ref, o_ref):
    o = IN_OFFS

    def put(c0, rows):
        o_ref[0, :, c0:c0 + W_PREP_COLS] = rows.T.astype(BF16)

    for c0 in range(0, ZA_W, W_PREP_COLS):
        scale = ML_DIM ** -0.5 if o[1] <= c0 < o[2] else 1.0
        put(c0, wt_ref[0, c0:c0 + W_PREP_COLS, :] * scale)
    pad = jnp.zeros((ZS_W - (o[8] - o[4]), W_PREP_ROWS), F32)
    put(ZA_W, jnp.concatenate([wt_ref[0, o[5]:o[8], :], wt_ref[0, o[4]:o[5], :], pad], axis=0))
    for c0 in range(ZA_W + ZS_W, ZIN_W, W_PREP_COLS):
        src = c0 - (ZA_W + ZS_W) + o[8]
        put(c0, wt_ref[0, src:src + W_PREP_COLS, :])


def _w_in_prep(w_in):
    return pl.pallas_call(
        _w_in_prep_kernel,
        grid=(DEPTH, D_MODEL // W_PREP_ROWS),
        in_specs=[pl.BlockSpec((1, D_IN, W_PREP_ROWS), lambda l, r: (l, 0, r))],
        out_specs=pl.BlockSpec((1, W_PREP_ROWS, ZIN_W), lambda l, r: (l, r, 0)),
        out_shape=jax.ShapeDtypeStruct((DEPTH, D_MODEL, ZIN_W), BF16),
        compiler_params=_cparams(("arbitrary", "arbitrary")),
        name="w_in_prep",
    )(jnp.swapaxes(w_in, 1, 2))


def _tok_specs(x, tile):
    if isinstance(x, tuple):
        n_ctx = N_CTX // tile
        return [pl.BlockSpec((tile, D_MODEL), lambda i: (jnp.minimum(i, n_ctx - 1), 0)),
                pl.BlockSpec((tile, D_MODEL), lambda i: (jnp.maximum(i - n_ctx, 0), 0))], list(x)
    return [pl.BlockSpec((tile, D_MODEL), lambda i: (i, 0))], [x]


def _tok_value(x_refs, tile):
    if len(x_refs) == 2:
        return jnp.where(pl.program_id(0) < N_CTX // tile, x_refs[0][...], x_refs[1][...])
    return x_refs[0][...]


def _inproj_kernel(n_x, *refs):
    g_ref, sh_ref, sc_ref, w_ref, za_ref, zs_ref, zc_ref, zb_ref = refs[n_x:n_x + 8]
    x = _tok_value(refs[:n_x], TW_IN)
    if n_x == 2:
        refs[n_x + 8][...] = x
    h = _rms(x, g_ref[0]) * (1.0 + sc_ref[0, 0]) + sh_ref[0, 0]
    hb = h.astype(BF16)
    za_ref[...] = _dot(hb, w_ref[0, :, 0:ZA_W]).astype(BF16)
    zs_ref[...] = _dot(hb, w_ref[0, :, ZA_W:ZA_W + ZS_W])
    zc_ref[...] = _dot(hb, w_ref[0, :, ZA_W + ZS_W:ZA_W + ZS_W + ZC_W]).astype(BF16)
    zb_ref[...] = _dot(hb, w_ref[0, :, ZA_W + ZS_W + ZC_W:ZIN_W]).astype(BF16)


def _mod_spec(layer, k, tile=None):
    return pl.BlockSpec((1, 1, 1, D_MODEL), lambda i: (layer, _mod_row(i, tile), 0, k))


def _inproj(layer, x, norm1, mod, w_in_r):
    tok = lambda w: pl.BlockSpec((TW_IN, w), lambda i: (i, 0))
    x_specs, x_args = _tok_specs(x, TW_IN)
    return pl.pallas_call(
        functools.partial(_inproj_kernel, len(x_args)),
        grid=(N_TOK // TW_IN,),
        in_specs=x_specs + [
            pl.BlockSpec((1, 1, D_MODEL), lambda i: (layer, 0, 0)),
            _mod_spec(layer, 0, TW_IN),
            _mod_spec(layer, 1, TW_IN),
            pl.BlockSpec((1, D_MODEL, ZIN_W), lambda i: (layer, 0, 0)),
        ],
        out_specs=[tok(ZA_W), tok(ZS_W), tok(ZC_W), tok(ZB_W)] + [tok(D_MODEL)] * (len(x_args) - 1),
        out_shape=[
            jax.ShapeDtypeStruct((N_TOK, ZA_W), BF16),
            jax.ShapeDtypeStruct((N_TOK, ZS_W), F32),
            jax.ShapeDtypeStruct((N_TOK, ZC_W), BF16),
            jax.ShapeDtypeStruct((N_TOK, ZB_W), BF16),
        ] + [jax.ShapeDtypeStruct((N_TOK, D_MODEL), F32)] * (len(x_args) - 1),
        compiler_params=_cparams(("arbitrary",)),
        name="inproj",
    )(*x_args, norm1, mod, mod, w_in_r)


def _make_mlstm_kernel(seq, layer, has_init):
    tq = min(TQ, seq)
    nq = seq // tq
    lane_if, lane_ff, lane_ib, lane_fb = (GATE_LANE0 + ML_HEADS * j for j in range(4))

    def kern(*refs):
        if has_init:
            body(*refs)
            return
        out = refs[10]
        b = pl.program_id(0)

        @pl.when(b < N_SEQ_CTX)
        def _():
            body(*refs)

        @pl.when(b >= N_SEQ_CTX)
        def _():
            out[...] = jnp.zeros_like(out)

    def body(*refs):
        if has_init:
            m0_ref, zq, zk, zv, zo, gz, bg, nrm, c0_ref, n0_ref, _, out, bp_scr, bs_scr = refs
        else:
            zq, zk, zv, zo, gz, bg, nrm, _, _, _, out, cf_ref, nf_ref, mf_ref, bp_scr, bs_scr = refs
        b = pl.program_id(0)
        g = gz[...] + bg[0]
        lane = lax.broadcasted_iota(jnp.int32, g.shape, 1)
        is_forget = ((lane >= lane_ff) & (lane < lane_ib)) | ((lane >= lane_fb) & (lane < lane_fb + ML_HEADS))
        log_sig = jnp.minimum(g, 0.0) - jnp.log1p(jnp.exp(-jnp.abs(g)))
        a = jnp.where(is_forget, log_sig, g)
        r_i = lax.broadcasted_iota(jnp.int32, (seq, seq), 0)
        c_i = lax.broadcasted_iota(jnp.int32, (seq, seq), 1)
        ltri = jnp.where(c_i <= r_i, 1.0, 0.0).astype(BF16)
        bp = functools.reduce(jnp.add, [_dot(ltri, t) for t in _split3(a)])
        bs = bp[seq - 1:seq, :] - bp + a
        bp_scr[...] = bp
        bs_scr[...] = bs
        eye = jnp.where(lax.broadcasted_iota(jnp.int32, (LANES, LANES), 0)
                        == lax.broadcasted_iota(jnp.int32, (LANES, LANES), 1), 1.0, 0.0).astype(BF16)
        tr = lambda x: functools.reduce(jnp.add, [_dot_nt(eye, t) for t in _split3(x)])
        if has_init:
            tr = lambda x: x.T
        a_t, bp_t, bs_t = tr(a), tr(bp), tr(bs)

        for h in range(ML_HEADS):
            hs = slice(h * ML_DIM, (h + 1) * ML_DIM)
            k = zk[:, hs]
            v = zv[:, hs]
            first_lane = lax.broadcasted_iota(jnp.int32, (seq, ML_DIM), 1) == 0
            v_aug = jnp.concatenate([v, jnp.where(first_lane, 1.0, 0.0).astype(BF16)], axis=1)
            rows = (
                a_t[lane_if + h:lane_if + h + 1, :] - bp_t[lane_ff + h:lane_ff + h + 1, :],
                a_t[lane_ib + h:lane_ib + h + 1, :] - bs_t[lane_fb + h:lane_fb + h + 1, :],
            )
            col_refs = ((bp_scr, lane_ff + h), (bs_scr, lane_fb + h))
            if has_init:
                m0 = tuple(m0_ref[((b * DEPTH + layer) * 2 + dr) * ML_HEADS + h] for dr in range(2))
                c0 = tuple(c0_ref[0, 0, dr, h].astype(BF16) for dr in range(2))
                n0 = tuple(jnp.broadcast_to(n0_ref[0, 0, dr, h:h + 1, :], (ML_DIM, ML_DIM)).astype(BF16) for dr in range(2))
            else:
                m0 = (0.0, 0.0)

            def qblock(qi, carry):
                q0 = pl.multiple_of(qi * tq, tq)
                qb = zq[pl.ds(q0, tq), hs]
                sc = _dot_nt(qb, k)
                t_idx = q0 + lax.broadcasted_iota(jnp.int32, (tq, seq), 0)
                s_idx = lax.broadcasted_iota(jnp.int32, (tq, seq), 1)
                hsum = jnp.zeros((tq, ML_DIM), F32)
                for dr in range(2):
                    cref, cl = col_refs[dr]
                    col = cref[pl.ds(q0, tq), cl:cl + 1]
                    mask = (s_idx <= t_idx) if dr == 0 else (s_idx >= t_idx)
                    drow = jnp.where(mask, rows[dr], -jnp.inf)
                    c_t = jnp.maximum(m0[dr], jnp.max(drow, axis=1, keepdims=True))
                    s = sc * jnp.exp(drow - c_t)
                    na = _dot(s.astype(BF16), v_aug)
                    num, den = na[:, 0:ML_DIM], na[:, ML_DIM:ML_DIM + 1]
                    if has_init:
                        w_c = jnp.exp(m0[dr] - c_t)
                        num = num + w_c * _dot(qb, c0[dr])
                        den = den + w_c * _dot_nt(qb, n0[dr])[:, 0:1]
                    hsum = hsum + num / jnp.maximum(jnp.abs(den), jnp.exp(-(col + c_t)))
                hn = _rms(hsum, nrm[0][:, hs])
                og = zo[pl.ds(q0, tq), hs].astype(F32)
                out[pl.ds(q0, tq), hs] = (hn * jax.nn.sigmoid(og)).astype(out.dtype)
                return carry

            if nq == 1:
                qblock(0, 0)
            else:
                lax.fori_loop(0, nq, qblock, 0)

            if not has_init:
                k_t = _dot_nt(eye, k)
                kf = k.astype(F32)
                tot = (bp_t[lane_ff + h:lane_ff + h + 1, seq - 1:seq], bp_t[lane_fb + h:lane_fb + h + 1, seq - 1:seq])
                gl = (
                    tot[0] + rows[0],
                    bp_t[lane_fb + h:lane_fb + h + 1, :] - a_t[lane_fb + h:lane_fb + h + 1, :]
                    + a_t[lane_ib + h:lane_ib + h + 1, :],
                )
                for dr in range(2):
                    m_new = jnp.maximum(tot[dr] + m0[dr], jnp.max(gl[dr], axis=1, keepdims=True))
                    w_s = jnp.exp(gl[dr] - m_new)
                    cf_ref[0, 0, dr, h] = _dot((k_t * w_s).astype(BF16), v)
                    n_new = jnp.dot(jnp.broadcast_to(w_s, (SUBLANES, seq)), kf, precision=HI, preferred_element_type=F32)
                    nf_ref[0, 0, dr, h:h + 1, :] = n_new[0:1, :]
                    mf_ref[0, 0, dr * ML_HEADS + h:dr * ML_HEADS + h + 1, :] = jnp.broadcast_to(m_new, (1, LANES))

    return kern


def _mlstm(layer, za, zs, b_gates, mlstm_norm, init=None, ctx_out=None, states=None):
    has_init = init is not None
    seq, nseq, row0 = (S_SMP, N_SEQ_SMP, N_CTX // S_SMP) if has_init else (S_CTX, N_SEQ_CTX, 0)
    in_row = lambda b: row0 + jnp.minimum(b, nseq - 1)
    qkvo = [pl.BlockSpec((seq, ML_WIDTH), functools.partial(lambda j, b: (in_row(b), j), j)) for j in range(4)]
    in_specs = qkvo + [
        pl.BlockSpec((seq, LANES), lambda b: (in_row(b), ZS_W // LANES - 1)),
        pl.BlockSpec((1, 1, LANES), lambda b: (layer, 0, 0)),
        pl.BlockSpec((1, 1, ML_WIDTH), lambda b: (layer, 0, 0)),
    ]
    args = [za, za, za, za, zs, b_gates, mlstm_norm]
    out_specs = [pl.BlockSpec((seq, ML_WIDTH), lambda b: (row0 + b, 0))]
    out_shape = [jax.ShapeDtypeStruct((N_TOK, ML_WIDTH), BF16)]
    aliases = {}
    if has_init:
        st_c, st_n, st_m = init
        in_specs = [pl.BlockSpec(memory_space=pltpu.SMEM)] + in_specs + [
            pl.BlockSpec((1, 1, 2, ML_HEADS, ML_DIM, ML_DIM), lambda b: (b, layer, 0, 0, 0, 0)),
            pl.BlockSpec((1, 1, 2, ML_HEADS, ML_DIM), lambda b: (b, layer, 0, 0, 0)),
            pl.BlockSpec(memory_space=pl.ANY),
        ]
        args = [st_m.reshape(-1)] + args + [st_c, st_n, ctx_out]
        aliases = {len(args) - 1: 0}
    else:
        seq_blk = lambda b: jnp.minimum(b, nseq - 1)
        in_specs += [pl.BlockSpec(memory_space=pl.ANY)] * 3
        args += list(states)
        aliases = {len(args) - 3 + j: 1 + j for j in range(3)}
        out_specs += [
            pl.BlockSpec((1, 1, 2, ML_HEADS, ML_DIM, ML_DIM), lambda b: (seq_blk(b), layer, 0, 0, 0, 0)),
            pl.BlockSpec((1, 1, 2, ML_HEADS, ML_DIM), lambda b: (seq_blk(b), layer, 0, 0, 0)),
            pl.BlockSpec((1, 1, 2 * ML_HEADS, LANES), lambda b: (seq_blk(b), layer, 0, 0)),
        ]
        out_shape += [jax.ShapeDtypeStruct(s.shape, s.dtype) for s in states]
    return pl.pallas_call(
        _make_mlstm_kernel(seq, layer, has_init),
        grid=(nseq if has_init else N_SEQ_BLOCKS,),
        in_specs=in_specs,
        out_specs=out_specs,
        out_shape=out_shape,
        scratch_shapes=[pltpu.VMEM((seq, LANES), F32), pltpu.VMEM((seq, LANES), F32)],
        input_output_aliases=aliases,
        compiler_params=_cparams(("arbitrary",)),
        name="mlstm_smp" if has_init else "mlstm_ctx",
    )(*args)


def _rope(x, cos, sin_a, sin_b):
    return x * cos + pltpu.roll(x, LANES - 8, 1) * sin_a + pltpu.roll(x, 8, 1) * sin_b


def _mla_prep_kernel(zs_ref, qa_ref, kva_ref, wuq_ref, wk_ref, wv_ref, qn_ref, kn_ref, cos_ref, sa_ref, sb_ref, _, __,
                     q_ref, kk_ref, v_ref, ckv_ref, kr_ref, qf_scr, kf_scr):
    cq = zs_ref[:, 0:MLA_Q_RANK]
    ckv = zs_ref[:, MLA_Q_RANK:MLA_Q_RANK + MLA_KV_RANK]
    last = zs_ref[:, ZS_W - LANES:ZS_W]
    qf_scr[...] = _dot(_rms(cq, qa_ref[0]).astype(BF16), wuq_ref[0])
    ckvn = _rms(ckv, kva_ref[0])

    @pl.when(pl.program_id(0) < N_TILES_CTX)
    def _():
        for j in range(TW // S_CTX):
            ckv_ref[j, 0] = ckvn[j * S_CTX:(j + 1) * S_CTX, :]
            kr_ref[j, 0] = last[j * S_CTX:(j + 1) * S_CTX, :].T[0:MLA_ROPE, :]

    cb = ckvn.astype(BF16)
    kf_scr[...] = _dot(cb, wk_ref[0])
    v_ref[...] = _dot(cb, wv_ref[0]).astype(BF16)
    lane = lax.broadcasted_iota(jnp.int32, last.shape, 1)
    kr = jnp.where((lane >= MLA_NOPE) & (lane < MLA_QK), pltpu.roll(last, MLA_NOPE, 1), 0.0)
    is_latent = pl.program_id(0) >= N_TILES_CTX

    def heads(rotate):
        for h in range(MLA_HEADS):
            hs = slice(h * HEAD_PAD, (h + 1) * HEAD_PAD)
            q_ref[:, hs] = (rotate(_rms(qf_scr[:, hs], qn_ref[0], n=MLA_QK)) * (MLA_QK ** -0.5)).astype(BF16)
            kk_ref[:, hs] = rotate(_rms(kf_scr[:, hs] + kr, kn_ref[0], n=MLA_QK)).astype(BF16)

    @pl.when(is_latent)
    def _():
        cos, sa, sb = cos_ref[...], sa_ref[...], sb_ref[...]
        heads(lambda x: _rope(x, cos, sa, sb))

    @pl.when(jnp.logical_not(is_latent))
    def _():
        heads(lambda x: x)


def _mla_prep(layer, zs, q_a_norm, kv_a_norm, w_uq_r, w_k_r, w_v_r, q_norm_p, k_norm_p, rope_tabs, new_ckv, new_kr):
    seq_blk = lambda r, c: pl.BlockSpec((TW // S_CTX, 1, r, c), lambda i: (jnp.minimum(i, N_TILES_CTX - 1), layer, 0, 0))
    lw = lambda shape: pl.BlockSpec((1,) + shape, lambda i: (layer,) + (0,) * len(shape))
    tab = pl.BlockSpec((TW, LANES), lambda i: (jnp.where(i < N_TILES_CTX, 0, 1 + (i - N_TILES_CTX) % TILES_PER_SMP_SEQ), 0))
    tok = lambda w: pl.BlockSpec((TW, w), lambda i: (i, 0))
    return pl.pallas_call(
        _mla_prep_kernel,
        grid=(N_TILES,),
        in_specs=[
            tok(ZS_W), lw((1, MLA_Q_RANK)), lw((1, MLA_KV_RANK)),
            lw((MLA_Q_RANK, MLA_HEADS * HEAD_PAD)), lw((MLA_KV_RANK, MLA_HEADS * HEAD_PAD)),
            lw((MLA_KV_RANK, MLA_HEADS * MLA_V)), lw((1, HEAD_PAD)), lw((1, HEAD_PAD)), tab, tab, tab,
            pl.BlockSpec(memory_space=pl.ANY), pl.BlockSpec(memory_space=pl.ANY),
        ],
        out_specs=[tok(MLA_HEADS * HEAD_PAD), tok(MLA_HEADS * HEAD_PAD), tok(MLA_HEADS * MLA_V),
                   seq_blk(S_CTX, MLA_KV_RANK), seq_blk(MLA_ROPE, S_CTX)],
        input_output_aliases={11: 3, 12: 4},
        out_shape=[
            jax.ShapeDtypeStruct((N_TOK, MLA_HEADS * HEAD_PAD), BF16),
            jax.ShapeDtypeStruct((N_TOK, MLA_HEADS * HEAD_PAD), BF16),
            jax.ShapeDtypeStruct((N_TOK, MLA_HEADS * MLA_V), BF16),
            jax.ShapeDtypeStruct(new_ckv.shape, F32),
            jax.ShapeDtypeStruct(new_kr.shape, F32),
        ],
        scratch_shapes=[pltpu.VMEM((TW, MLA_HEADS * HEAD_PAD), F32), pltpu.VMEM((TW, MLA_HEADS * HEAD_PAD), F32)],
        compiler_params=_cparams(("arbitrary",)),
        name="mla_prep",
    )(zs, q_a_norm, kv_a_norm, w_uq_r, w_k_r, w_v_r, q_norm_p, k_norm_p, *rope_tabs, new_ckv, new_kr)


def _cache_kv_kernel(ckv_ref, kr_ref, wk_ref, wv_ref, kn_ref, kk_ref, v_ref):
    cb = ckv_ref[...].astype(BF16)
    kf = _dot(cb, wk_ref[0])
    v_ref[...] = _dot(cb, wv_ref[0]).astype(BF16)
    kr = kr_ref[...]
    for h in range(MLA_HEADS):
        hs = slice(h * HEAD_PAD, (h + 1) * HEAD_PAD)
        kk_ref[:, hs] = _rms(kf[:, hs] + kr, kn_ref[0], n=MLA_QK).astype(BF16)


def _cache_kv(cache_ckv, cache_kr_pad, w_k_r, w_v_r, k_norm_p):
    lw = lambda shape: pl.BlockSpec((1,) + shape, lambda b, l: (l,) + (0,) * len(shape))
    blk = lambda w: pl.BlockSpec((None, None, PAST_LEN, w), lambda b, l: (b, l, 0, 0))
    return pl.pallas_call(
        _cache_kv_kernel,
        grid=(N_SEQ_SMP, DEPTH),
        in_specs=[blk(MLA_KV_RANK), blk(LANES), lw((MLA_KV_RANK, MLA_HEADS * HEAD_PAD)),
                  lw((MLA_KV_RANK, MLA_HEADS * MLA_V)), lw((1, HEAD_PAD))],
        out_specs=[blk(MLA_HEADS * HEAD_PAD), blk(MLA_HEADS * MLA_V)],
        out_shape=[
            jax.ShapeDtypeStruct((N_SEQ_SMP, DEPTH, PAST_LEN, MLA_HEADS * HEAD_PAD), BF16),
            jax.ShapeDtypeStruct((N_SEQ_SMP, DEPTH, PAST_LEN, MLA_HEADS * MLA_V), BF16),
        ],
        compiler_params=_cparams(("arbitrary", "arbitrary")),
        name="cache_kv",
    )(cache_ckv, cache_kr_pad, w_k_r, w_v_r, k_norm_p)


def _make_attn_kernel(n_src):
    def kern(q_ref, *refs):
        o_ref = refs[-1]
        if n_src > 1:
            body(q_ref, *refs)
            return

        @pl.when(pl.program_id(0) < N_SEQ_CTX)
        def _():
            body(q_ref, *refs)

        @pl.when(pl.program_id(0) >= N_SEQ_CTX)
        def _():
            o_ref[...] = jnp.zeros_like(o_ref)

    def body(q_ref, *refs):
        o_ref = refs[-1]
        for h in range(MLA_HEADS):
            hs = slice(h * HEAD_PAD, (h + 1) * HEAD_PAD)
            vs = slice(h * MLA_V, (h + 1) * MLA_V)
            q = q_ref[:, hs]
            ss = [_dot_nt(q, refs[2 * j][:, hs]) for j in range(n_src)]
            m = functools.reduce(jnp.maximum, [jnp.max(s, axis=1, keepdims=True) for s in ss])
            ps = [jnp.exp(s - m) for s in ss]
            l = functools.reduce(jnp.add, [jnp.sum(p, axis=1, keepdims=True) for p in ps])
            o = functools.reduce(jnp.add, [_dot(ps[j].astype(BF16), refs[2 * j + 1][:, vs]) for j in range(n_src)])
            o_ref[:, vs] = (o / l).astype(o_ref.dtype)

    return kern


def _attn_ctx(q, kk, v):
    blk = lambda w: pl.BlockSpec((S_CTX, w), lambda b: (jnp.minimum(b, N_SEQ_CTX - 1), 0))
    return pl.pallas_call(
        _make_attn_kernel(1),
        grid=(N_SEQ_BLOCKS,),
        in_specs=[blk(MLA_HEADS * HEAD_PAD), blk(MLA_HEADS * HEAD_PAD), blk(MLA_HEADS * MLA_V)],
        out_specs=pl.BlockSpec((S_CTX, MLA_HEADS * MLA_V), lambda b: (b, 0)),
        out_shape=jax.ShapeDtypeStruct((N_TOK, MLA_HEADS * MLA_V), BF16),
        compiler_params=_cparams(("arbitrary",)),
        name="attn_ctx",
    )(q, kk, v)


def _attn_smp(layer, q, kk, v, kk_cache, v_cache, ctx_out):
    row0 = N_CTX // S_SMP
    nq = S_SMP // TQ
    seqb = lambda w: pl.BlockSpec((S_SMP, w), lambda b, i: (row0 + b, 0))
    cache = lambda w: pl.BlockSpec((None, None, PAST_LEN, w), lambda b, i: (b, layer, 0, 0))
    return pl.pallas_call(
        _make_attn_kernel(2),
        grid=(N_SEQ_SMP, nq),
        in_specs=[
            pl.BlockSpec((TQ, MLA_HEADS * HEAD_PAD), lambda b, i: (N_CTX // TQ + b * nq + i, 0)),
            seqb(MLA_HEADS * HEAD_PAD), seqb(MLA_HEADS * MLA_V),
            cache(MLA_HEADS * HEAD_PAD), cache(MLA_HEADS * MLA_V),
            pl.BlockSpec(memory_space=pl.ANY),
        ],
        out_specs=pl.BlockSpec((TQ, MLA_HEADS * MLA_V), lambda b, i: (N_CTX // TQ + b * nq + i, 0)),
        out_shape=jax.ShapeDtypeStruct((N_TOK, MLA_HEADS * MLA_V), BF16),
        input_output_aliases={5: 0},
        compiler_params=_cparams(("arbitrary", "arbitrary")),
        name="attn_smp",
    )(q, kk, v, kk_cache, v_cache, ctx_out)


def _merge_kernel(x_ref, oa_ref, ob_ref, zc_ref, zb_ref, sgn_ref, ws_ref, bs_ref, wb_ref, wo_ref, g1_ref, n2_ref,
                  sh2_ref, sc2_ref, wr_ref, br_ref,
                  xmid_ref, h2_ref, dest_ref, wsel_ref, cnt_ref, oc_scr, carry_scr):
    i = pl.program_id(0)

    @pl.when(i == 0)
    def _():
        carry_scr[...] = jnp.zeros_like(carry_scr)

    u = _gelu(zc_ref[:, 0:SG_WIDTH].astype(F32))
    vg = _gelu(zc_ref[:, SG_WIDTH:2 * SG_WIDTH].astype(F32))
    for g in range(SG_GROUPS):
        gs = slice(g * SG_DIM, (g + 1) * SG_DIM)
        vn = _rms(vg[:, gs], sgn_ref[0][:, gs]).astype(BF16)
        for c in range(TW // SG_CHUNK):
            cs = slice(c * SG_CHUNK, (c + 1) * SG_CHUNK)
            mixed = _dot(ws_ref[0, g], vn[cs, :]) + bs_ref[0][:, g:g + 1]
            oc_scr[cs, gs] = (u[cs, gs] * mixed).astype(BF16)

    acc = jnp.zeros((TW, D_MODEL), F32)
    for j, src in enumerate((oa_ref, ob_ref, oc_scr)):
        gate = jax.nn.sigmoid(zb_ref[:, j * D_MODEL:(j + 1) * D_MODEL].astype(F32))
        acc = acc + gate * _dot(src[...], wb_ref[0, j])
    xm = x_ref[...] + g1_ref[0, 0] * _dot(acc.astype(BF16), wo_ref[0])
    xmid_ref[...] = xm
    h2 = _rms(xm, n2_ref[0]) * (1.0 + sc2_ref[0, 0]) + sh2_ref[0, 0]
    for c in range(SC_SPLIT):
        h2_ref[c] = _pack_pairs(h2[:, 2 * c * SC_ROW:(2 * c + 1) * SC_ROW], h2[:, (2 * c + 1) * SC_ROW:(2 * c + 2) * SC_ROW])

    h_hi = h2.astype(BF16)
    h_lo = (h2 - h_hi.astype(F32)).astype(BF16)
    p_hi = _dot(h_hi, wr_ref[0])
    logits = p_hi[:, 0:LANES] + p_hi[:, LANES:2 * LANES] + _dot(h_lo, wr_ref[0, :, 0:LANES]) + br_ref[0]
    lane = lax.broadcasted_iota(jnp.int32, logits.shape, 1)
    hits, exps = [], []
    sel = jnp.zeros(logits.shape, F32)
    denom = jnp.zeros((TW, 1), F32)
    top = None
    for _ in range(TOP_K):
        m = jnp.max(logits, axis=1, keepdims=True)
        idx = jnp.min(jnp.where(logits == m, lane, LANES), axis=1, keepdims=True)
        hit = lane == idx
        top = m if top is None else top
        hits.append(hit)
        exps.append(jnp.exp(m - top))
        sel = jnp.where(hit, 1.0, sel)
        denom = denom + exps[-1]
        logits = jnp.where(hit, -jnp.inf, logits)

    r_i = lax.broadcasted_iota(jnp.int32, (TW, TW), 0)
    c_i = lax.broadcasted_iota(jnp.int32, (TW, TW), 1)
    carry = carry_scr[0:1, :]
    rank = _dot(jnp.where(c_i < r_i, 1.0, 0.0).astype(BF16), sel.astype(BF16)) + carry
    new_carry = carry + jnp.sum(sel, axis=0, keepdims=True)
    carry_scr[...] = jnp.broadcast_to(new_carry, (SUBLANES, LANES))
    cnt_ref[...] = jnp.broadcast_to(new_carry, (SUBLANES, LANES))
    slot = rank + lane.astype(F32) * float(EXPERT_CAP)
    dmat = jnp.zeros(logits.shape, F32)
    wmat = jnp.zeros(logits.shape, F32)
    for k in range(TOP_K):
        dk = jnp.sum(jnp.where(hits[k], slot, 0.0), axis=1, keepdims=True)
        dmat = jnp.where(lane == k, dk, dmat)
        wmat = jnp.where(lane == k, exps[k] / denom, wmat)
    dest_ref[...] = dmat.T[0:SUBLANES, :].astype(jnp.int32)
    wsel_ref[...] = wmat


def _merge(layer, x, oa, ob, zc, zb, sg_norm, w_sp, b_sp, w_branch, w_out, mod, norm2, w_router_p, b_router_p):
    lw = lambda shape: pl.BlockSpec((1,) + shape, lambda i: (layer,) + (0,) * len(shape))
    tok = lambda w: pl.BlockSpec((TW, w), lambda i: (i, 0))
    return pl.pallas_call(
        _merge_kernel,
        grid=(N_TILES,),
        in_specs=[
            tok(D_MODEL), tok(ML_WIDTH), tok(MLA_HEADS * MLA_V), tok(ZC_W), tok(ZB_W),
            lw((1, SG_WIDTH)), lw((SG_GROUPS, SG_CHUNK, SG_CHUNK)), lw((SG_CHUNK, LANES)),
            lw((N_BRANCH, ML_WIDTH, D_MODEL)), lw((D_MODEL, D_MODEL)),
            _mod_spec(layer, 2), lw((1, D_MODEL)), _mod_spec(layer, 3), _mod_spec(layer, 4),
            lw((D_MODEL, 2 * LANES)), lw((1, LANES)),
        ],
        out_specs=[tok(D_MODEL), pl.BlockSpec((SC_SPLIT, TW, SC_ROW), lambda i: (0, i, 0)),
                   pl.BlockSpec((SUBLANES, TW), lambda i: (0, i)), tok(LANES),
                   pl.BlockSpec((SUBLANES, LANES), lambda i: (0, 0))],
        out_shape=[
            jax.ShapeDtypeStruct((N_TOK, D_MODEL), F32),
            jax.ShapeDtypeStruct((SC_SPLIT, N_TOK, SC_ROW), jnp.uint32),
            jax.ShapeDtypeStruct((SUBLANES, N_TOK), jnp.int32),
            jax.ShapeDtypeStruct((N_TOK, LANES), F32),
            jax.ShapeDtypeStruct((SUBLANES, LANES), F32),
        ],
        scratch_shapes=[pltpu.VMEM((TW, SG_WIDTH), BF16), pltpu.VMEM((SUBLANES, LANES), F32)],
        compiler_params=_cparams(("arbitrary",)),
        name="merge_router",
    )(x, oa, ob, zc, zb, sg_norm, w_sp, b_sp, w_branch, w_out, mod, norm2, mod, mod, w_router_p, b_router_p)


def _sc_mesh():
    return plsc.VectorSubcoreMesh(core_axis_name="core", subcore_axis_name="subcore")


def _sc_scatter_rows(x, idxs, n_rows):
    @pl.kernel(out_type=jax.ShapeDtypeStruct((n_rows, SC_ROW), x.dtype), mesh=_sc_mesh(), scratch_types=[])
    def scatter(x_hbm, *refs):
        o_hbm = refs[-1]

        def body(x_vmem, *i_vmems):
            for i_vmem in i_vmems:
                pltpu.sync_copy(x_vmem, o_hbm.at[i_vmem.at[0]])

        pltpu.emit_pipeline(
            body,
            grid=(x.shape[0] // SC_WIN,),
            in_specs=[pl.BlockSpec((SC_WIN, SC_ROW), lambda i: (i, 0))]
            + [pl.BlockSpec((1, SC_WIN), lambda i: (0, i))] * len(idxs),
            out_specs=[],
            core_axis_name=("core", "subcore"),
            dimension_semantics=(pltpu.PARALLEL,),
        )(x_hbm, *refs[:-1])

    return scatter(x, *idxs)


def _sc_gather_rows(x, idx):
    m = idx.shape[1]

    @pl.kernel(out_type=jax.ShapeDtypeStruct((m, SC_ROW), x.dtype), mesh=_sc_mesh())
    def gather(x_hbm, i_hbm, o_hbm):
        def body(i_vmem, o_vmem):
            pltpu.sync_copy(x_hbm.at[i_vmem.at[0]], o_vmem)

        pltpu.emit_pipeline(
            body,
            grid=(m // SC_WIN,),
            in_specs=[pl.BlockSpec((1, SC_WIN), lambda i: (0, i))],
            out_specs=[pl.BlockSpec((SC_WIN, SC_ROW), lambda i: (i, 0))],
            core_axis_name=("core", "subcore"),
            dimension_semantics=(pltpu.PARALLEL,),
        )(i_hbm, o_hbm)

    return gather(x, idx)


STEP_VALID, STEP_FIRST, STEP_HAS_NEXT = 1, 2, 4
STEP_CHUNKS_SHIFT = 3


def _moe_ffn_kernel(layer, be_ref, nx_ref, br_ref, fl_ref, xs_ref, b1_ref, b2_ref, w1_hbm, w2_hbm, y_ref,
                    w1f, w2f, w1b, w2b, sem):
    g = pl.program_id(0)
    flags = fl_ref[g]

    def weight_copies(e):
        return (pltpu.make_async_copy(w1_hbm.at[layer, e], w1f, sem.at[0]),
                pltpu.make_async_copy(w2_hbm.at[layer, e], w2f, sem.at[1]))

    @pl.when(g == 0)
    def _():
        for cp in weight_copies(be_ref[0]):
            cp.start()

    @pl.when((flags & STEP_FIRST) != 0)
    def _():
        for cp in weight_copies(be_ref[g]):
            cp.wait()
        w1b[...] = w1f[...].astype(BF16)
        w2b[...] = w2f[...].astype(BF16)

        @pl.when((flags & STEP_HAS_NEXT) != 0)
        def _():
            for cp in weight_copies(nx_ref[g]):
                cp.start()

    def ffn(n_rows):
        halves = [h.astype(BF16) for c in range(SC_SPLIT) for h in _unpack_pairs(xs_ref[c, 0:n_rows, :])]
        g1 = _dot(jnp.concatenate(halves, axis=1), w1b[...]) + b1_ref[0, 0]
        gate = jnp.minimum(g1[:, :D_EXPERT], SWIGLU_LIMIT)
        up = jnp.clip(g1[:, D_EXPERT:], -SWIGLU_LIMIT, SWIGLU_LIMIT)
        act = gate * jax.nn.sigmoid(SWIGLU_ALPHA * gate) * (up + 1.0)
        y = _dot(act.astype(BF16), w2b[...]) + b2_ref[0, 0]
        for c in range(SC_SPLIT):
            y_ref[c, 0:n_rows, :] = _pack_pairs(
                y[:, 2 * c * SC_ROW:(2 * c + 1) * SC_ROW], y[:, (2 * c + 1) * SC_ROW:(2 * c + 2) * SC_ROW])

    for n_chunks in range(1, FFN_CHUNKS + 1):
        want = STEP_VALID | ((n_chunks - 1) << STEP_CHUNKS_SHIFT)
        pl.when((flags & (STEP_VALID | ((FFN_CHUNKS - 1) << STEP_CHUNKS_SHIFT))) == want)(
            functools.partial(ffn, n_chunks * SLOT_CHUNK))


def _moe_ffn(layer, xs, plan, w1, b1, w2, b2):
    eb = lambda c: pl.BlockSpec((1, 1, 1, c), lambda g, be, nx, br, fl: (layer, be[g], 0, 0))
    rows = pl.BlockSpec((SC_SPLIT, FFN_BLOCK, SC_ROW), lambda g, be, nx, br, fl: (0, br[g], 0))
    hbm = pl.BlockSpec(memory_space=pl.ANY)
    grid_spec = pltpu.PrefetchScalarGridSpec(
        num_scalar_prefetch=4,
        grid=(N_CHUNK_STEPS,),
        in_specs=[rows, eb(2 * D_EXPERT), eb(D_MODEL), hbm, hbm],
        out_specs=rows,
        scratch_shapes=[
            pltpu.VMEM((D_MODEL, 2 * D_EXPERT), F32), pltpu.VMEM((D_EXPERT, D_MODEL), F32),
            pltpu.VMEM((D_MODEL, 2 * D_EXPERT), BF16), pltpu.VMEM((D_EXPERT, D_MODEL), BF16),
            pltpu.SemaphoreType.DMA((2,)),
        ],
    )
    return pl.pallas_call(
        functools.partial(_moe_ffn_kernel, layer),
        grid_spec=grid_spec,
        out_shape=jax.ShapeDtypeStruct(xs.shape, xs.dtype),
        compiler_params=_cparams(("arbitrary",)),
        name="moe_ffn",
    )(*plan, xs, b1, b2, w1, w2)


def _chunk_plan_kernel(cnt_ref, be_ref, nx_ref, br_ref, fl_ref):
    def expert(e, carry):
        step0, prev_first = carry
        c = cnt_ref[e]
        n_blk = (c + FFN_BLOCK - 1) // FFN_BLOCK

        def block(j, _):
            s = step0 + j
            be_ref[s] = e
            nx_ref[s] = e
            br_ref[s] = e * (EXPERT_CAP // FFN_BLOCK) + j
            fl_ref[s] = (STEP_VALID + jnp.where(j == 0, STEP_FIRST, 0)
                         + (jnp.minimum((c - j * FFN_BLOCK + SLOT_CHUNK - 1) // SLOT_CHUNK, FFN_CHUNKS) - 1)
                         * (1 << STEP_CHUNKS_SHIFT))
            return 0

        lax.fori_loop(0, n_blk, block, 0)

        @pl.when((n_blk > 0) & (prev_first >= 0))
        def _():
            nx_ref[prev_first] = e
            fl_ref[prev_first] = fl_ref[prev_first] + STEP_HAS_NEXT

        return step0 + n_blk, jnp.where(n_blk > 0, step0, prev_first)

    used, _ = lax.fori_loop(0, N_EXPERTS, expert, (jnp.int32(0), jnp.int32(-1)))

    def idle(s, _):
        be_ref[s] = be_ref[used - 1]
        nx_ref[s] = be_ref[used - 1]
        br_ref[s] = br_ref[used - 1]
        fl_ref[s] = 0
        return 0

    lax.fori_loop(used, N_CHUNK_STEPS, idle, 0)


def _chunk_plan(cnt):
    smem = pl.BlockSpec(memory_space=pltpu.SMEM)
    return pl.pallas_call(
        _chunk_plan_kernel,
        in_specs=[smem],
        out_specs=[smem] * 4,
        out_shape=[jax.ShapeDtypeStruct((N_CHUNK_STEPS,), jnp.int32)] * 4,
        name="chunk_plan",
    )(cnt)


def _combine_kernel(x_ref, yg_ref, w_ref, g_ref, *o_refs):
    def emit(o_ref):
        w = w_ref[...]
        for c in range(SC_SPLIT):
            parts = [_unpack_pairs(yg_ref[k, c]) for k in range(TOP_K)]
            for half in range(2):
                cs = slice((2 * c + half) * SC_ROW, (2 * c + half + 1) * SC_ROW)
                acc = w[:, 0:1] * parts[0][half]
                for k in range(1, TOP_K):
                    acc = acc + w[:, k:k + 1] * parts[k][half]
                o_ref[:, cs] = x_ref[:, cs] + g_ref[0, 0][:, cs] * acc

    if len(o_refs) == 1:
        emit(o_refs[0])
    else:
        pl.when(pl.program_id(0) < N_TILES_CTX)(lambda: emit(o_refs[0]))
        pl.when(pl.program_id(0) >= N_TILES_CTX)(lambda: emit(o_refs[1]))


def _combine(layer, xmid, yg, wsel, mod, split_out):
    tok = lambda w: pl.BlockSpec((TW, w), lambda i: (i, 0))
    if split_out:
        out_specs = [pl.BlockSpec((TW, D_MODEL), lambda i: (jnp.minimum(i, N_TILES_CTX - 1), 0)),
                     pl.BlockSpec((TW, D_MODEL), lambda i: (jnp.maximum(i - N_TILES_CTX, 0), 0))]
        out_shape = [jax.ShapeDtypeStruct((N_CTX, D_MODEL), F32), jax.ShapeDtypeStruct((N_SMP, D_MODEL), F32)]
    else:
        out_specs, out_shape = tok(D_MODEL), jax.ShapeDtypeStruct((N_TOK, D_MODEL), F32)
    return pl.pallas_call(
        _combine_kernel,
        grid=(N_TILES,),
        in_specs=[tok(D_MODEL), pl.BlockSpec((TOP_K, SC_SPLIT, TW, SC_ROW), lambda i: (0, 0, i, 0)), tok(LANES),
                  _mod_spec(layer, 5)],
        out_specs=out_specs,
        out_shape=out_shape,
        compiler_params=_cparams(("arbitrary",)),
        name="combine",
    )(xmid, yg, wsel, mod)


def _moe(layer, xmid, h2, dest, wsel, cnt, mod, w1, b1, w2, b2, split_out):
    n_slots = N_EXPERTS * EXPERT_CAP
    idx = dest[0:TOP_K][:, None, :] + (jnp.arange(SC_SPLIT, dtype=jnp.int32) * n_slots)[None, :, None]
    idx = idx.reshape(TOP_K, 1, SC_SPLIT * N_TOK)
    xs = _sc_scatter_rows(h2.reshape(SC_SPLIT * N_TOK, SC_ROW), [idx[k] for k in range(TOP_K)], SC_SPLIT * n_slots)
    plan = _chunk_plan(cnt[0, :N_EXPERTS].astype(jnp.int32))
    y = _moe_ffn(layer, xs.reshape(SC_SPLIT, n_slots, SC_ROW), plan, w1, b1, w2, b2)
    yg = _sc_gather_rows(y.reshape(SC_SPLIT * n_slots, SC_ROW), idx.reshape(1, TOP_K * SC_SPLIT * N_TOK))
    return _combine(layer, xmid, yg.reshape(TOP_K, SC_SPLIT, N_TOK, SC_ROW), wsel, mod, split_out)


def _rope_tables():
    pos = np.arange(S_SMP)
    half = MLA_ROPE // 2
    inv_freq = (ROPE_THETA ** (-(np.arange(0, half, 2, dtype=np.float32) / np.float32(half)))).astype(np.float32)
    angs = [((pos // GRID_W).astype(np.float32)[:, None] * inv_freq[None, :]).astype(np.float32),
            ((pos % GRID_W).astype(np.float32)[:, None] * inv_freq[None, :]).astype(np.float32)]
    nf = half // 2
    cos = np.ones((TW + S_SMP, LANES), np.float32)
    sin_a = np.zeros((TW + S_SMP, LANES), np.float32)
    sin_b = np.zeros((TW + S_SMP, LANES), np.float32)
    for axis, ang in enumerate(angs):
        base = MLA_NOPE + axis * half
        c, s = np.cos(ang.astype(np.float64)), np.sin(ang.astype(np.float64))
        cos[TW:, base:base + nf] = c
        cos[TW:, base + nf:base + half] = c
        sin_a[TW:, base:base + nf] = -s
        sin_b[TW:, base + nf:base + half] = s
    return jnp.asarray(cos), jnp.asarray(sin_a), jnp.asarray(sin_b)


def _pad_last(a, width):
    return jnp.pad(a, [(0, 0)] * (a.ndim - 1) + [(0, width - a.shape[-1])])


def kernel(x_prompt, x_sample, cache_mla_ckv, cache_mla_krope, state_mlstm_C, state_mlstm_n, state_mlstm_m, c, c_ctx, norm1, norm2, w_ada, b_ada, w_in, b_mlstm_gates, mlstm_norm, mla_q_a_norm, mla_kv_a_norm, w_uq, w_ukv, mla_q_norm, mla_k_norm, sg_norm, w_spatial, b_spatial, w_branch, w_out, w_router, b_router, w_exp1, b_exp1, w_exp2, b_exp2):
    w_in_r = _w_in_prep(w_in)
    w_uq_r = _pad_last(w_uq.reshape(DEPTH, MLA_Q_RANK, MLA_HEADS, MLA_QK), HEAD_PAD).reshape(
        DEPTH, MLA_Q_RANK, MLA_HEADS * HEAD_PAD).astype(BF16)
    w_ukv4 = w_ukv.reshape(DEPTH, MLA_KV_RANK, MLA_HEADS, MLA_NOPE + MLA_V)
    w_k_r = _pad_last(w_ukv4[..., :MLA_NOPE], HEAD_PAD).reshape(DEPTH, MLA_KV_RANK, MLA_HEADS * HEAD_PAD).astype(BF16)
    w_v_r = w_ukv4[..., MLA_NOPE:].reshape(DEPTH, MLA_KV_RANK, MLA_HEADS * MLA_V).astype(BF16)
    q_norm_p = _pad_last(mla_q_norm, HEAD_PAD).reshape(DEPTH, 1, HEAD_PAD)
    k_norm_p = _pad_last(mla_k_norm, HEAD_PAD).reshape(DEPTH, 1, HEAD_PAD)
    b_gates_p = jnp.pad(b_mlstm_gates, ((0, 0), (GATE_LANE0, LANES - GATE_LANE0 - 4 * ML_HEADS))).reshape(DEPTH, 1, LANES)
    b_sp = _pad_last(jnp.swapaxes(b_spatial, 1, 2), LANES)
    w_router_p = _pad_last(w_router, LANES)
    w_router_hi = w_router_p.astype(BF16)
    w_router_p = jnp.concatenate([w_router_hi, (w_router_p - w_router_hi.astype(F32)).astype(BF16)], axis=-1)
    b_router_p = jnp.pad(b_router, ((0, 0), (0, LANES - N_EXPERTS)), constant_values=-1e30).reshape(DEPTH, 1, LANES)
    r3 = lambda a: a.reshape(DEPTH, 1, a.shape[-1])
    cache_kr_pad = jnp.pad(cache_mla_krope, ((0, 0), (0, 0), (0, 0), (MLA_NOPE, LANES - MLA_QK)))
    rope_tabs = _rope_tables()

    cvec = jnp.concatenate([c_ctx[None, :], c, jnp.zeros((SUBLANES - 1 - N_SEQ_SMP, D_MODEL), F32)], axis=0)
    mod = _adaln(cvec, w_ada, b_ada).reshape(DEPTH, SUBLANES, 1, 6 * D_MODEL)
    b1 = b_exp1.reshape(DEPTH, N_EXPERTS, 1, 2 * D_EXPERT)
    b2 = b_exp2.reshape(DEPTH, N_EXPERTS, 1, D_MODEL)
    kk_cache, v_cache = _cache_kv(cache_mla_ckv, cache_kr_pad, w_k_r, w_v_r, k_norm_p)

    x = (x_prompt.reshape(N_CTX, D_MODEL), x_sample.reshape(N_SMP, D_MODEL))
    new_ckv = jnp.zeros((N_SEQ_CTX, DEPTH, S_CTX, MLA_KV_RANK), F32)
    new_kr = jnp.zeros((N_SEQ_CTX, DEPTH, MLA_ROPE, S_CTX), F32)
    states = (jnp.zeros((N_SEQ_CTX, DEPTH, 2, ML_HEADS, ML_DIM, ML_DIM), F32),
              jnp.zeros((N_SEQ_CTX, DEPTH, 2, ML_HEADS, ML_DIM), F32),
              jnp.zeros((N_SEQ_CTX, DEPTH, 2 * ML_HEADS, LANES), F32))
    for l in range(DEPTH):
        za, zs, zc, zb, *x_joined = _inproj(l, x, r3(norm1), mod, w_in_r)
        x = x_joined[0] if x_joined else x
        oa, *states = _mlstm(l, za, zs, b_gates_p, r3(mlstm_norm), states=states)
        (oa,) = _mlstm(l, za, zs, b_gates_p, r3(mlstm_norm), init=(state_mlstm_C, state_mlstm_n, state_mlstm_m), ctx_out=oa)
        q, kk, v, new_ckv, new_kr = _mla_prep(l, zs, r3(mla_q_a_norm), r3(mla_kv_a_norm), w_uq_r, w_k_r, w_v_r,
                                              q_norm_p, k_norm_p, rope_tabs, new_ckv, new_kr)
        ob = _attn_smp(l, q, kk, v, kk_cache, v_cache, _attn_ctx(q, kk, v))
        xmid, h2, dest, wsel, cnt = _merge(
            l, x, oa, ob, zc, zb, r3(sg_norm), w_spatial.astype(BF16), b_sp, w_branch.astype(BF16), w_out.astype(BF16),
            mod, r3(norm2), w_router_p, b_router_p)
        x = _moe(l, xmid, h2, dest, wsel, cnt, mod, w_exp1, b1, w_exp2, b2, split_out=l == DEPTH - 1)
    y_ctx, y_smp = x
    return (
        y_ctx.reshape(N_SEQ_CTX, S_CTX, D_MODEL),
        y_smp.reshape(N_SEQ_SMP, S_SMP, D_MODEL),
        new_ckv,
        jnp.swapaxes(new_kr, 2, 3),
        states[0],
        states[1],
        states[2][:, :, :, 0].reshape(N_SEQ_CTX, DEPTH, 2, ML_HEADS),
    )
```
